```python
import math
import jax, jax.numpy as jnp
from jax import lax
import numpy as np

D_MODEL = 1024
BATCH = 8
SEQ = 2048
DEPTH = 2

D_MIX = D_MODEL
DN_HEADS = 4
DN_HEAD_DIM = 128
DN_WIDTH = DN_HEADS * DN_HEAD_DIM
DN_CONV = 4
DN_CHUNK = 64
NSA_HEADS = 4
NSA_HEAD_DIM = 64
NSA_WIDTH = NSA_HEADS * NSA_HEAD_DIM
CMP_LEN = 32
CMP_STRIDE = 16
CMP_HIDDEN = 2 * NSA_HEAD_DIM
SLC_BLOCK = 64
SLC_TOP_N = 16
WINDOW = 512
Q_BLOCK = 128
CONV_WIDTH = D_MIX - DN_WIDTH - NSA_WIDTH
CONV_KERNEL = 31
REL_BUCKETS = 32
REL_MAX_DIST = 128
N_EXPERTS = 32
TOP_K = 4
D_FF = D_MODEL
SWIGLU_LIMIT = 7.0
SWIGLU_ALPHA = 1.702
MOE_BLOCK = 128

EPS = 1e-6
NEG_INF = -1e30
FORCE = 1e4

IN_SPLITS = (
    DN_WIDTH,
    DN_WIDTH,
    DN_WIDTH,
    DN_WIDTH,
    DN_HEADS,
    DN_HEADS,
    NSA_WIDTH,
    6 * NSA_HEAD_DIM,
    3 * NSA_HEADS,
    2 * CONV_WIDTH,
)
D_IN = sum(IN_SPLITS)

kernel_name = 'hybrid_deltanet_nsa_conformer_moe'

f32 = jnp.float32


def rms_norm(x, w):
    xf = x.astype(f32)
    y = xf * lax.rsqrt(jnp.mean(xf * xf, axis=-1, keepdims=True) + EPS)
    return (y * w.astype(f32)).astype(x.dtype)


def layer_norm(x, w, b):
    xf = x.astype(f32)
    mu = jnp.mean(xf, axis=-1, keepdims=True)
    var = jnp.mean(jnp.square(xf - mu), axis=-1, keepdims=True)
    return ((xf - mu) * lax.rsqrt(var + EPS) * w.astype(f32) + b.astype(f32)).astype(x.dtype)


def l2_normalize(x):
    return x * lax.rsqrt(jnp.sum(x * x, axis=-1, keepdims=True) + EPS)


def masked_softmax(s, mask):
    s = jnp.where(mask, s, NEG_INF)
    return jax.nn.softmax(s, axis=-1) * mask


def causal_depthwise_conv(x, w):
    k = w.shape[0]
    return lax.conv_general_dilated(
        x, w[:, None, :].astype(x.dtype), window_strides=(1,), padding=[(k - 1, 0)],
        dimension_numbers=('NHC', 'HIO', 'NHC'), feature_group_count=x.shape[-1])


def t5_bucket(dist):
    n = jnp.maximum(dist, 0)
    max_exact = REL_BUCKETS // 2
    nf = jnp.maximum(n, 1).astype(f32)
    large = max_exact + (jnp.log(nf / max_exact) / math.log(REL_MAX_DIST / max_exact)
                         * (REL_BUCKETS - max_exact)).astype(jnp.int32)
    large = jnp.minimum(large, REL_BUCKETS - 1)
    return jnp.where(n < max_exact, n, large)


def chunk_gated_delta_rule(q, k, v, g, beta):
    B, T, H, Dk = q.shape
    Dv = v.shape[-1]
    C = DN_CHUNK
    N = T // C
    to_chunks = lambda t: t.reshape(B, N, C, H, -1).transpose(1, 0, 3, 2, 4)
    q, k, v = to_chunks(q), to_chunks(k), to_chunks(v)
    g = jnp.cumsum(g.reshape(B, N, C, H).transpose(1, 0, 3, 2), axis=-1)
    beta = beta.reshape(B, N, C, H).transpose(1, 0, 3, 2)
    k_beta = k * beta[..., None]
    v_beta = v * beta[..., None]
    tril = jnp.tril(jnp.ones((C, C), bool))
    strict = jnp.tril(jnp.ones((C, C), bool), -1)
    gdiff = jnp.where(tril, g[..., :, None] - g[..., None, :], 0.0)
    decay = jnp.where(tril, jnp.exp(gdiff), 0.0)
    lower = jnp.where(strict, jnp.einsum('nbhid,nbhjd->nbhij', k_beta, k) * decay, 0.0)
    rhs = jnp.concatenate([v_beta, k_beta * jnp.exp(g)[..., None]], axis=-1)
    sol = lax.linalg.triangular_solve(lower, rhs, left_side=True, lower=True, unit_diagonal=True)
    u, w = sol[..., :Dv], sol[..., Dv:]
    attn_intra = jnp.where(tril, jnp.einsum('nbhid,nbhjd->nbhij', q, k) * decay, 0.0)

    def step(S, inp):
        q_c, k_c, u_c, w_c, g_c, a_c = inp
        v_new = u_c - jnp.einsum('bhck,bhkv->bhcv', w_c, S)
        o = (jnp.einsum('bhck,bhkv->bhcv', q_c * jnp.exp(g_c)[..., None], S)
             + jnp.einsum('bhij,bhjv->bhiv', a_c, v_new))
        g_last = g_c[..., -1:]
        S = (S * jnp.exp(g_last)[..., None]
             + jnp.einsum('bhck,bhcv->bhkv', k_c * jnp.exp(g_last - g_c)[..., None], v_new))
        return S, o

    S0 = jnp.zeros((B, H, Dk, Dv), f32)
    _, o = lax.scan(step, S0, (q, k, u, w, g, attn_intra))
    return o.transpose(1, 0, 3, 2, 4).reshape(B, T, H, Dv)


def gated_deltanet(q, k, v, z, a, b, conv_w, a_log, dt_bias, norm_w):
    B, T, _ = q.shape
    qkv = jax.nn.silu(causal_depthwise_conv(jnp.concatenate([q, k, v], axis=-1), conv_w))
    q, k, v = jnp.split(qkv, 3, axis=-1)
    heads = lambda t: t.reshape(B, T, DN_HEADS, DN_HEAD_DIM).astype(f32)
    q = l2_normalize(heads(q)) * DN_HEAD_DIM ** -0.5
    k = l2_normalize(heads(k))
    v = heads(v)
    beta = jax.nn.sigmoid(b.astype(f32))
    g = -jnp.exp(a_log.astype(f32)) * jax.nn.softplus(a.astype(f32) + dt_bias.astype(f32))
    o = chunk_gated_delta_rule(q, k, v, g, beta)
    o = rms_norm(o, norm_w) * jax.nn.silu(heads(z))
    return o.reshape(B, T, DN_WIDTH).astype(z.dtype)


def nsa_attention(q, kv, gate_logits, q_norm_w, k_norm_w, cmp_pos, cmp_w1, cmp_w2, rel_bias):
    B, T, _ = q.shape
    H, Dh = NSA_HEADS, NSA_HEAD_DIM
    dt = q.dtype
    scale = Dh ** -0.5
    rel_bias = rel_bias.astype(f32)
    q = rms_norm(q.reshape(B, T, H, Dh), q_norm_w)
    k_cmp_raw, v_cmp_raw, k_slc, v_slc, k_win, v_win = jnp.split(kv, 6, axis=-1)
    t_pos = jnp.arange(T)

    n_cmp = (T - CMP_LEN) // CMP_STRIDE + 1
    blk_idx = np.arange(n_cmp)[:, None] * CMP_STRIDE + np.arange(CMP_LEN)[None, :]

    def compress(t, pos, w1, w2):
        blocks = t[:, blk_idx] + pos.astype(t.dtype)
        return jax.nn.silu(blocks.reshape(B, n_cmp, CMP_LEN * Dh) @ w1) @ w2

    k_cmp = rms_norm(compress(k_cmp_raw, cmp_pos[0], cmp_w1[0], cmp_w2[0]), k_norm_w[0])
    v_cmp = compress(v_cmp_raw, cmp_pos[1], cmp_w1[1], cmp_w2[1])
    cmp_end = jnp.arange(n_cmp) * CMP_STRIDE + CMP_LEN - 1
    dist_cmp = t_pos[:, None] - cmp_end[None, :]
    bias_cmp = rel_bias[t5_bucket(dist_cmp)].transpose(2, 0, 1)
    s_cmp = jnp.einsum('bthd,bjd->bhtj', q, k_cmp).astype(f32) * scale + bias_cmp
    p_cmp = masked_softmax(s_cmp, dist_cmp >= 0)
    o_cmp = jnp.einsum('bhtj,bjd->bthd', p_cmp.astype(dt), v_cmp)

    n_slc = T // SLC_BLOCK
    cmp_start = np.arange(n_cmp) * CMP_STRIDE
    slc_start = np.arange(n_slc) * SLC_BLOCK
    overlap = ((cmp_start[:, None] < slc_start[None, :] + SLC_BLOCK)
               & (cmp_start[:, None] + CMP_LEN > slc_start[None, :])).astype(np.float32)
    imp = jnp.einsum('bhtj,js->bts', p_cmp, jnp.asarray(overlap))
    cur = (t_pos // SLC_BLOCK)[:, None]
    s_idx = jnp.arange(n_slc)[None, :]
    causal_blk = s_idx <= cur
    forced = (s_idx == 0) | (s_idx == cur) | (s_idx == cur - 1)
    imp = jnp.where(causal_blk & forced, FORCE, jnp.where(causal_blk, imp, -1.0))
    top_n = min(SLC_TOP_N, n_slc)
    _, sel = lax.top_k(imp, top_n)

    k_slc = rms_norm(k_slc, k_norm_w[1]).reshape(B, n_slc, SLC_BLOCK, Dh)
    v_slc = v_slc.reshape(B, n_slc, SLC_BLOCK, Dh)
    k_win_p = jnp.pad(rms_norm(k_win, k_norm_w[2]), ((0, 0), (WINDOW, 0), (0, 0)))
    v_win_p = jnp.pad(v_win, ((0, 0), (WINDOW, 0), (0, 0)))
    n_qb = T // Q_BLOCK
    q_blocks = q.reshape(B, n_qb, Q_BLOCK, H, Dh).transpose(1, 0, 2, 3, 4)
    sel_blocks = sel.reshape(B, n_qb, Q_BLOCK, top_n).transpose(1, 0, 2, 3)
    b_idx = jnp.arange(B)[:, None, None]
    n_keys = top_n * SLC_BLOCK

    def query_block(args):
        i, q_i, sel_i = args
        tq = i * Q_BLOCK + jnp.arange(Q_BLOCK)
        kg = k_slc[b_idx, sel_i]
        vg = v_slc[b_idx, sel_i].reshape(B, Q_BLOCK, n_keys, Dh)
        key_pos = sel_i[..., None] * SLC_BLOCK + jnp.arange(SLC_BLOCK)
        dist = tq[None, :, None, None] - key_pos
        bias = rel_bias[t5_bucket(dist)].transpose(0, 4, 1, 2, 3)
        s = jnp.einsum('bqhd,bqnkd->bhqnk', q_i, kg).astype(f32) * scale + bias
        p = masked_softmax(s.reshape(B, H, Q_BLOCK, n_keys),
                           (dist >= 0).reshape(B, 1, Q_BLOCK, n_keys))
        o_s = jnp.einsum('bhqm,bqmd->bqhd', p.astype(dt), vg)
        kw = lax.dynamic_slice_in_dim(k_win_p, i * Q_BLOCK, WINDOW + Q_BLOCK, axis=1)
        vw = lax.dynamic_slice_in_dim(v_win_p, i * Q_BLOCK, WINDOW + Q_BLOCK, axis=1)
        kpos = i * Q_BLOCK - WINDOW + jnp.arange(WINDOW + Q_BLOCK)
        dist_w = tq[:, None] - kpos[None, :]
        mask_w = (dist_w >= 0) & (dist_w < WINDOW) & (kpos[None, :] >= 0)
        bias_w = rel_bias[t5_bucket(dist_w)].transpose(2, 0, 1)
        s_w = jnp.einsum('bqhd,bkd->bhqk', q_i, kw).astype(f32) * scale + bias_w
        p_w = masked_softmax(s_w, mask_w)
        o_w = jnp.einsum('bhqk,bkd->bqhd', p_w.astype(dt), vw)
        return o_s, o_w

    o_slc, o_win = lax.map(query_block, (jnp.arange(n_qb), q_blocks, sel_blocks))
    o_slc = o_slc.transpose(1, 0, 2, 3, 4).reshape(B, T, H, Dh)
    o_win = o_win.transpose(1, 0, 2, 3, 4).reshape(B, T, H, Dh)
    gates = jax.nn.sigmoid(gate_logits.astype(f32)).reshape(B, T, H, 3)
    o = (gates[..., 0, None] * o_cmp.astype(f32) + gates[..., 1, None] * o_slc.astype(f32)
         + gates[..., 2, None] * o_win.astype(f32))
    return o.reshape(B, T, NSA_WIDTH).astype(dt)


def conformer_conv(u, dw_w, dw_b, ln_w, ln_b):
    a, gate = jnp.split(u, 2, axis=-1)
    h = a * jax.nn.sigmoid(gate)
    h = causal_depthwise_conv(h, dw_w) + dw_b
    h = layer_norm(h, ln_w, ln_b)
    return jax.nn.silu(h)


def moe_ffn(h, router_w, router_b, w_gate_up, b_gate_up, w_down, b_down):
    B, T, D = h.shape
    x = h.reshape(-1, D)
    n_tok = x.shape[0]
    n_assign = n_tok * TOP_K
    logits = (x @ router_w + router_b).astype(f32)
    top_logits, top_idx = lax.top_k(logits, TOP_K)
    weights = jax.nn.softmax(top_logits, axis=-1)
    flat_e = top_idx.reshape(-1)
    order = jnp.argsort(flat_e)
    sorted_e = flat_e[order]
    sorted_tok = order // TOP_K
    counts = jnp.bincount(flat_e, length=N_EXPERTS)
    padded = (counts + MOE_BLOCK - 1) // MOE_BLOCK * MOE_BLOCK
    pad_end = jnp.cumsum(padded)
    pad_start = pad_end - padded
    start = jnp.cumsum(counts) - counts
    slot = pad_start[sorted_e] + jnp.arange(n_assign) - start[sorted_e]
    n_slots = n_assign + N_EXPERTS * MOE_BLOCK
    n_blocks = n_slots // MOE_BLOCK
    tok_of_slot = jnp.zeros((n_slots,), jnp.int32).at[slot].set(sorted_tok)
    w_of_slot = jnp.zeros((n_slots,), f32).at[slot].set(weights.reshape(-1)[order])
    block_expert = jnp.minimum(
        jnp.searchsorted(pad_end, jnp.arange(n_blocks) * MOE_BLOCK, side='right'), N_EXPERTS - 1)
    xs = x[tok_of_slot].reshape(n_blocks, MOE_BLOCK, D)

    def expert_block(args):
        e, xb = args
        gu = xb @ w_gate_up[e] + b_gate_up[e]
        gate, up = jnp.split(gu, 2, axis=-1)
        gate = jnp.minimum(gate, SWIGLU_LIMIT)
        up = jnp.clip(up, -SWIGLU_LIMIT, SWIGLU_LIMIT)
        act = (up + 1.0) * gate * jax.nn.sigmoid(SWIGLU_ALPHA * gate)
        return act @ w_down[e] + b_down[e]

    ys = lax.map(expert_block, (block_expert, xs)).reshape(n_slots, D)
    out = jnp.zeros_like(x).at[tok_of_slot].add(ys * w_of_slot[:, None].astype(ys.dtype))
    return out.reshape(B, T, D)


def split_columns(p):
    offsets = [int(o) for o in np.cumsum(IN_SPLITS)[:-1]]
    return jnp.split(p, offsets, axis=-1)


def setup_inputs(seed: int = 0) -> dict:
    key = jax.random.key(seed)
    ks = iter(jax.random.split(key, 32))
    L = DEPTH

    def nrm(shape, scale):
        return scale * jax.random.normal(next(ks), shape, f32)

    dt = jnp.exp(jax.random.uniform(next(ks), (L, DN_HEADS), f32, math.log(1e-3), math.log(1e-1)))
    a_init = jax.random.uniform(next(ks), (L, DN_HEADS), f32, 1.0, 16.0)
    return {
        'x': nrm((BATCH, SEQ, D_MODEL), 1.0),
        'attn_norm_w': 1.0 + nrm((L, D_MODEL), 0.01),
        'w_in': nrm((L, D_MODEL, D_IN), D_MODEL ** -0.5),
        'dn_conv_w': nrm((L, DN_CONV, 3 * DN_WIDTH), DN_CONV ** -0.5),
        'dn_a_log': jnp.log(a_init),
        'dn_dt_bias': dt + jnp.log(-jnp.expm1(-dt)),
        'dn_norm_w': 1.0 + nrm((L, DN_HEAD_DIM), 0.01),
        'nsa_q_norm_w': 1.0 + nrm((L, NSA_HEAD_DIM), 0.01),
        'nsa_k_norm_w': 1.0 + nrm((L, 3, NSA_HEAD_DIM), 0.01),
        'nsa_cmp_pos': nrm((L, 2, CMP_LEN, NSA_HEAD_DIM), 0.1),
        'nsa_cmp_w1': nrm((L, 2, CMP_LEN * NSA_HEAD_DIM, CMP_HIDDEN), (CMP_LEN * NSA_HEAD_DIM) ** -0.5),
        'nsa_cmp_w2': nrm((L, 2, CMP_HIDDEN, NSA_HEAD_DIM), CMP_HIDDEN ** -0.5),
        'conv_dw_w': nrm((L, CONV_KERNEL, CONV_WIDTH), CONV_KERNEL ** -0.5),
        'conv_dw_b': nrm((L, CONV_WIDTH), 0.01),
        'conv_ln_w': 1.0 + nrm((L, CONV_WIDTH), 0.01),
        'conv_ln_b': nrm((L, CONV_WIDTH), 0.01),
        'w_out': nrm((L, D_MIX, D_MODEL), 0.5 * D_MIX ** -0.5),
        'ffn_norm_w': 1.0 + nrm((L, D_MODEL), 0.01),
        'router_w': nrm((L, D_MODEL, N_EXPERTS), D_MODEL ** -0.5),
        'router_b': nrm((L, N_EXPERTS), 0.01),
        'w_gate_up': nrm((L, N_EXPERTS, D_MODEL, 2 * D_FF), D_MODEL ** -0.5),
        'b_gate_up': nrm((L, N_EXPERTS, 2 * D_FF), 0.01),
        'w_down': nrm((L, N_EXPERTS, D_FF, D_MODEL), 0.5 * D_FF ** -0.5),
        'b_down': nrm((L, N_EXPERTS, D_MODEL), 0.01),
        'rel_bias': nrm((REL_BUCKETS, NSA_HEADS), 0.5),
    }


def reference(x, attn_norm_w, w_in, dn_conv_w, dn_a_log, dn_dt_bias, dn_norm_w,
              nsa_q_norm_w, nsa_k_norm_w, nsa_cmp_pos, nsa_cmp_w1, nsa_cmp_w2,
              conv_dw_w, conv_dw_b, conv_ln_w, conv_ln_b, w_out, ffn_norm_w,
              router_w, router_b, w_gate_up, b_gate_up, w_down, b_down, rel_bias):
    for l in range(DEPTH):
        h = rms_norm(x, attn_norm_w[l])
        (dn_q, dn_k, dn_v, dn_z, dn_a, dn_b,
         nsa_q, nsa_kv, nsa_g, conv_u) = split_columns(h @ w_in[l])
        y_a = gated_deltanet(dn_q, dn_k, dn_v, dn_z, dn_a, dn_b,
                             dn_conv_w[l], dn_a_log[l], dn_dt_bias[l], dn_norm_w[l])
        y_b = nsa_attention(nsa_q, nsa_kv, nsa_g, nsa_q_norm_w[l], nsa_k_norm_w[l],
                            nsa_cmp_pos[l], nsa_cmp_w1[l], nsa_cmp_w2[l], rel_bias)
        y_c = conformer_conv(conv_u, conv_dw_w[l], conv_dw_b[l], conv_ln_w[l], conv_ln_b[l])
        x = x + jnp.concatenate([y_a, y_b, y_c], axis=-1) @ w_out[l]
        x = x + moe_ffn(rms_norm(x, ffn_norm_w[l]), router_w[l], router_b[l],
                        w_gate_up[l], b_gate_up[l], w_down[l], b_down[l])
    return x
```

```python
import functools
import math

import numpy as np
import jax
import jax.numpy as jnp
from jax import lax
from jax.experimental import pallas as pl
from jax.experimental.pallas import tpu as pltpu

F32 = jnp.float32
BF16 = jnp.bfloat16

D_MODEL = 1024
SEQ = 2048
DN_HEADS = 4
DN_HEAD_DIM = 128
DN_WIDTH = DN_HEADS * DN_HEAD_DIM
DN_CONV = 4
DN_CHUNK = 64
NSA_HEADS = 4
NSA_HEAD_DIM = 64
NSA_WIDTH = NSA_HEADS * NSA_HEAD_DIM
CMP_LEN = 32
CMP_STRIDE = 16
CMP_HIDDEN = 2 * NSA_HEAD_DIM
SLC_BLOCK = 64
SLC_TOP_N = 16
WINDOW = 512
CONV_WIDTH = 256
CONV_KERNEL = 31
REL_BUCKETS = 32
REL_MAX_DIST = 128
N_EXPERTS = 32
TOP_K = 4
D_FF = D_MODEL
SWIGLU_LIMIT = 7.0
SWIGLU_ALPHA = 1.702
EPS = 1e-6
NEG_INF = -1e30
FORCE = 1e4

LANES = 128
VMEM_LIMIT_BYTES = 48 * 1024 * 1024

ROW_TILE = 512
SEQ_TILE = 256
Q_TILE = 128
MOE_TILE = 512
GATHER_BLOCK = 512
COMBINE_TILE = 256

N_CMP_PAD = 128
N_SLC = SEQ // SLC_BLOCK


def _params(sem=None):
    return pltpu.CompilerParams(dimension_semantics=sem, vmem_limit_bytes=VMEM_LIMIT_BYTES)


def _mm(a, b):
    return jnp.dot(a.astype(BF16), b.astype(BF16), preferred_element_type=F32)


def _mm_nt(a, b):
    return lax.dot_general(a.astype(BF16), b.astype(BF16), (((1,), (1,)), ((), ())),
                           preferred_element_type=F32)


def _mm_tn(a, b):
    return lax.dot_general(a.astype(BF16), b.astype(BF16), (((0,), (0,)), ((), ())),
                           preferred_element_type=F32)


def _split3(x):
    hi = x.astype(BF16)
    r1 = x - hi.astype(F32)
    mid = r1.astype(BF16)
    lo = (r1 - mid.astype(F32)).astype(BF16)
    return hi, mid, lo


def _dot01_right(x, m01):
    hi, mid, lo = _split3(x)
    d = lambda p: jnp.dot(p, m01, preferred_element_type=F32)
    return d(hi) + d(mid) + d(lo)


def _dot01_left(m01, x):
    hi, mid, lo = _split3(x)
    d = lambda p: jnp.dot(m01, p, preferred_element_type=F32)
    return d(hi) + d(mid) + d(lo)


def _dot_f32(a, b):
    a_hi = a.astype(BF16)
    a_lo = (a - a_hi.astype(F32)).astype(BF16)
    b_hi = b.astype(BF16)
    b_lo = (b - b_hi.astype(F32)).astype(BF16)
    d = lambda p, q: jnp.dot(p, q, preferred_element_type=F32)
    return d(a_hi, b_hi) + d(a_hi, b_lo) + d(a_lo, b_hi) + d(a_lo, b_lo)


def _sigmoid(x):
    return 1.0 / (1.0 + jnp.exp(-x))


def _silu(x):
    return x * _sigmoid(x)


def _softplus(x):
    return jnp.maximum(x, 0.0) + jnp.log(1.0 + jnp.exp(-jnp.abs(x)))


def _bcast_col(x, j, width):
    return jnp.broadcast_to(x[:, j:j + 1], (x.shape[0], width))


IN_SEGS = (3 * DN_WIDTH, DN_WIDTH, NSA_WIDTH, 6 * NSA_HEAD_DIM, 2 * CONV_WIDTH, LANES)
IN_COLS = sum(IN_SEGS)


def _in_proj_kernel(x_ref, nw_ref, w_ref, wabt_ref,
                    qkv_ref, z_ref, nq_ref, nkv_ref, cu_ref, small_ref, abt_ref):
    xf = x_ref[...]
    ms = jnp.mean(xf * xf, axis=-1, keepdims=True)
    hb = (xf * lax.rsqrt(ms + EPS) * nw_ref[...]).astype(BF16)
    off = 0
    for ref, width in zip((qkv_ref, z_ref, nq_ref, nkv_ref, cu_ref, small_ref), IN_SEGS):
        ref[...] = jnp.dot(hb, w_ref[:, off:off + width], preferred_element_type=F32)
        off += width
    abt_ref[...] = lax.dot_general(wabt_ref[...], hb, (((1,), (1,)), ((), ())),
                                   preferred_element_type=F32)


def _in_proj(x2, norm_w, w_cat, w_abt):
    n = x2.shape[0]
    tm = ROW_TILE
    out_shape = [jax.ShapeDtypeStruct((n, w), F32) for w in IN_SEGS]
    out_shape.append(jax.ShapeDtypeStruct((16, n), F32))
    out_specs = [pl.BlockSpec((tm, w), lambda i: (i, 0)) for w in IN_SEGS]
    out_specs.append(pl.BlockSpec((16, tm), lambda i: (0, i)))
    return pl.pallas_call(
        _in_proj_kernel,
        out_shape=out_shape,
        grid=(n // tm,),
        in_specs=[
            pl.BlockSpec((tm, D_MODEL), lambda i: (i, 0)),
            pl.BlockSpec((1, D_MODEL), lambda i: (0, 0)),
            pl.BlockSpec((D_MODEL, IN_COLS), lambda i: (0, 0)),
            pl.BlockSpec((16, D_MODEL), lambda i: (0, 0)),
        ],
        out_specs=out_specs,
        compiler_params=_params(("parallel",)),
        name="in_proj",
    )(x2, norm_w, w_cat, w_abt)


def _dn_prep_kernel(qkv_ref, small_ref, abt_ref, cw_ref, pcol_ref, prow_ref,
                    q_ref, k_ref, v_ref, gb_ref, grow_ref, buf):
    t = pl.program_id(1)
    ts = SEQ_TILE

    @pl.when(t == 0)
    def _():
        buf[0:8, :] = jnp.zeros((8, 3 * DN_WIDTH), F32)

    buf[8:8 + ts, :] = qkv_ref[0]
    for c in range(3 * DN_HEADS):
        cs = slice(c * LANES, (c + 1) * LANES)
        acc = cw_ref[0:1, cs] * buf[5:5 + ts, cs]
        for j in range(1, DN_CONV):
            acc = acc + cw_ref[j:j + 1, cs] * buf[5 + j:5 + j + ts, cs]
        y = _silu(acc)
        if c < 2 * DN_HEADS:
            y = y * lax.rsqrt(jnp.sum(y * y, axis=-1, keepdims=True) + EPS)
        if c < DN_HEADS:
            q_ref[0, :, cs] = y * DN_HEAD_DIM ** -0.5
        elif c < 2 * DN_HEADS:
            k_ref[0, :, (c - DN_HEADS) * LANES:(c - DN_HEADS + 1) * LANES] = y
        else:
            v_ref[0, :, (c - 2 * DN_HEADS) * LANES:(c - 2 * DN_HEADS + 1) * LANES] = y
    buf[0:8, :] = buf[ts:ts + 8, :]

    sm = small_ref[0]
    lane = lax.broadcasted_iota(jnp.int32, sm.shape, 1)
    g_col = -jnp.exp(pcol_ref[0:1, :]) * _softplus(sm + pcol_ref[1:2, :])
    gb_ref[0] = jnp.where(lane < DN_HEADS, g_col, _sigmoid(sm))
    a_t = abt_ref[...]
    g_row = -jnp.exp(prow_ref[:, 0:1]) * _softplus(a_t + prow_ref[:, 1:2])
    grow_ref[...] = g_row[0:8, :]


def _dn_prep(qkv, small, abt, conv_w, pcol, prow):
    b = qkv.shape[0]
    ts = SEQ_TILE
    nt = SEQ // ts
    seq_spec = lambda w: pl.BlockSpec((1, ts, w), lambda i, t: (i, t, 0))
    full = lambda shape: pl.BlockSpec(shape, lambda i, t: (0,) * len(shape))
    return pl.pallas_call(
        _dn_prep_kernel,
        out_shape=[jax.ShapeDtypeStruct((b, SEQ, DN_WIDTH), F32)] * 3
        + [jax.ShapeDtypeStruct((b, SEQ, LANES), F32), jax.ShapeDtypeStruct((8, b * SEQ), F32)],
        grid=(b, nt),
        in_specs=[
            seq_spec(3 * DN_WIDTH), seq_spec(LANES),
            pl.BlockSpec((16, ts), lambda i, t: (0, i * nt + t)),
            full((DN_CONV, 3 * DN_WIDTH)), full((8, LANES)), full((16, LANES)),
        ],
        out_specs=[seq_spec(DN_WIDTH)] * 3
        + [seq_spec(LANES), pl.BlockSpec((8, ts), lambda i, t: (0, i * nt + t))],
        scratch_shapes=[pltpu.VMEM((ts + 8, 3 * DN_WIDTH), F32)],
        compiler_params=_params(("parallel", "arbitrary")),
        name="dn_prep",
    )(qkv, small, abt, conv_w, pcol, prow)


def _delta_kernel(q_ref, k_ref, v_ref, gb_ref, grow_ref, z_ref, nw_ref, o_ref, s_ref):
    t = pl.program_id(1)
    ts = SEQ_TILE
    ch = DN_CHUNK
    hd = DN_HEAD_DIM

    @pl.when(t == 0)
    def _():
        s_ref[...] = jnp.zeros(s_ref.shape, F32)

    r = lax.broadcasted_iota(jnp.int32, (ts, ts), 0)
    c = lax.broadcasted_iota(jnp.int32, (ts, ts), 1)
    same_chunk = (r // ch) == (c // ch)
    tril = same_chunk & (r >= c)
    strict = same_chunk & (r > c)
    same16 = (r // 16) == (c // 16)
    eye = jnp.where(r == c, 1.0, 0.0).astype(F32)
    m_col = jnp.where(tril, 1.0, 0.0).astype(BF16)
    m_row = jnp.where(same_chunk & (r <= c), 1.0, 0.0).astype(BF16)

    gb = gb_ref[0]
    gc_col = _dot01_left(m_col, gb)
    gc_row = _dot01_right(grow_ref[...], m_row)

    pre = []
    for h in range(DN_HEADS):
        hs = slice(h * hd, (h + 1) * hd)
        qh = q_ref[0, :, hs]
        kh = k_ref[0, :, hs]
        vh = v_ref[0, :, hs]
        gcb = _bcast_col(gc_col, h, ts)
        grb = jnp.broadcast_to(gc_row[h:h + 1, :], (ts, ts))
        decay = jnp.where(tril, jnp.exp(jnp.where(tril, gcb - grb, 0.0)), 0.0)
        beta = _bcast_col(gb, DN_HEADS + h, hd)
        gc128 = gcb[:, :hd]
        expg = jnp.exp(gc128)
        kb = kh * beta
        vb = vh * beta
        a_mat = jnp.where(strict, _mm_nt(kb, kh) * decay, 0.0)
        d_mat = jnp.where(same16, a_mat, 0.0)
        e_mat = a_mat - d_mat
        d2 = _mm(d_mat, d_mat)
        d4 = _mm(d2, d2)
        d8 = _mm(d4, d4)
        p_mat = _mm(_mm(_mm(eye - d_mat, eye + d2), eye + d4), eye + d8)
        m_mat = _mm(p_mat, e_mat)
        m2 = _mm(m_mat, m_mat)
        t_mat = _mm(_mm(eye - m_mat, eye + m2), p_mat)
        sol = _mm(t_mat, jnp.concatenate([vb, kb * expg], axis=1))
        attn = jnp.where(tril, _mm_nt(qh, kh) * decay, 0.0)
        pre.append((sol[:, :hd], sol[:, hd:], attn, qh * expg, kh, gc128))

    outs = [[] for _ in range(DN_HEADS)]
    for ci in range(ts // ch):
        rs = slice(ci * ch, (ci + 1) * ch)
        for h in range(DN_HEADS):
            u, w, attn, qg, kh, gc128 = pre[h]
            s_mat = s_ref[h]
            v_new = u[rs] - _mm(w[rs], s_mat)
            o_c = _mm(qg[rs], s_mat) + _mm(attn[rs, ci * ch:(ci + 1) * ch], v_new)
            g_last = gc128[ci * ch + ch - 1:ci * ch + ch, :]
            k_dec = kh[rs] * jnp.exp(g_last - gc128[rs])
            s_ref[h] = s_mat * jnp.exp(g_last) + _mm_tn(k_dec, v_new)
            outs[h].append(o_c)

    for h in range(DN_HEADS):
        hs = slice(h * hd, (h + 1) * hd)
        o = jnp.concatenate(outs[h], axis=0)
        o = o * lax.rsqrt(jnp.mean(o * o, axis=-1, keepdims=True) + EPS) * nw_ref[...]
        o_ref[0, :, hs] = o * _silu(z_ref[0, :, hs])


def _delta_rule(q, k, v, gb, grow, z, norm_w):
    b = q.shape[0]
    ts = SEQ_TILE
    nt = SEQ // ts
    seq_spec = lambda w: pl.BlockSpec((1, ts, w), lambda i, t: (i, t, 0))
    return pl.pallas_call(
        _delta_kernel,
        out_shape=jax.ShapeDtypeStruct((b, SEQ, DN_WIDTH), F32),
        grid=(b, nt),
        in_specs=[
            seq_spec(DN_WIDTH), seq_spec(DN_WIDTH), seq_spec(DN_WIDTH), seq_spec(LANES),
            pl.BlockSpec((8, ts), lambda i, t: (0, i * nt + t)),
            seq_spec(DN_WIDTH),
            pl.BlockSpec((1, DN_HEAD_DIM), lambda i, t: (0, 0)),
        ],
        out_specs=seq_spec(DN_WIDTH),
        scratch_shapes=[pltpu.VMEM((DN_HEADS, DN_HEAD_DIM, DN_HEAD_DIM), F32)],
        compiler_params=_params(("parallel", "arbitrary")),
        name="delta_rule",
    )(q, k, v, gb, grow, z, norm_w)


CONV_HALO = 32


def _conformer_kernel(u_ref, w_ref, b_ref, lnw_ref, lnb_ref, o_ref, buf):
    t = pl.program_id(1)
    ts = SEQ_TILE

    @pl.when(t == 0)
    def _():
        buf[0:CONV_HALO, :] = jnp.zeros((CONV_HALO, CONV_WIDTH), F32)

    u = u_ref[0]
    buf[CONV_HALO:CONV_HALO + ts, :] = u[:, :CONV_WIDTH] * _sigmoid(u[:, CONV_WIDTH:])
    base = CONV_HALO - (CONV_KERNEL - 1)
    rows = 64
    for rc in range(ts // rows):
        parts = []
        for cc in range(CONV_WIDTH // LANES):
            cs = slice(cc * LANES, (cc + 1) * LANES)
            r0 = base + rc * rows
            acc = w_ref[0:1, cs] * buf[r0:r0 + rows, cs]
            for j in range(1, CONV_KERNEL):
                acc = acc + w_ref[j:j + 1, cs] * buf[r0 + j:r0 + j + rows, cs]
            parts.append(acc)
        h = jnp.concatenate(parts, axis=1) + b_ref[...]
        mu = jnp.mean(h, axis=-1, keepdims=True)
        var = jnp.mean(jnp.square(h - mu), axis=-1, keepdims=True)
        hn = (h - mu) * lax.rsqrt(var + EPS) * lnw_ref[...] + lnb_ref[...]
        o_ref[0, rc * rows:(rc + 1) * rows, :] = _silu(hn)
    buf[0:CONV_HALO, :] = buf[ts:ts + CONV_HALO, :]


def _conformer(u, dw_w, dw_b, ln_w, ln_b):
    b = u.shape[0]
    ts = SEQ_TILE
    full = lambda shape: pl.BlockSpec(shape, lambda i, t: (0,) * len(shape))
    return pl.pallas_call(
        _conformer_kernel,
        out_shape=jax.ShapeDtypeStruct((b, SEQ, CONV_WIDTH), F32),
        grid=(b, SEQ // ts),
        in_specs=[
            pl.BlockSpec((1, ts, 2 * CONV_WIDTH), lambda i, t: (i, t, 0)),
            full((CONV_KERNEL, CONV_WIDTH)), full((1, CONV_WIDTH)),
            full((1, CONV_WIDTH)), full((1, CONV_WIDTH)),
        ],
        out_specs=pl.BlockSpec((1, ts, CONV_WIDTH), lambda i, t: (i, t, 0)),
        scratch_shapes=[pltpu.VMEM((ts + CONV_HALO, CONV_WIDTH), F32)],
        compiler_params=_params(("parallel", "arbitrary")),
        name="conformer",
    )(u, dw_w, dw_b, ln_w, ln_b)


def _rms_rows(x, w):
    return x * lax.rsqrt(jnp.mean(x * x, axis=-1, keepdims=True) + EPS) * w


def _nsa_prep_kernel(kc_ref, vc_ref, ks_ref, kw_ref, pos_ref, w1_ref, w2_ref, knw_ref,
                     kcmp_ref, vcmp_ref, ksn_ref, kwn_ref):
    half = CMP_STRIDE * NSA_HEAD_DIM

    def compress(x, i):
        u_lo = _mm(x + pos_ref[2 * i:2 * i + 1, :], w1_ref[i, 0:half, :])
        u_hi = _mm(x + pos_ref[2 * i + 1:2 * i + 2, :], w1_ref[i, half:2 * half, :])
        hid = _silu(u_lo + pltpu.roll(u_hi, N_CMP_PAD - 1, axis=0))
        return _mm(hid, w2_ref[i])

    kcmp_ref[0] = _rms_rows(compress(kc_ref[0], 0), knw_ref[0:1, :])
    vcmp_ref[0] = compress(vc_ref[0], 1)
    ksn_ref[0] = _rms_rows(ks_ref[0], knw_ref[1:2, :])
    kwn_ref[0] = _rms_rows(kw_ref[0], knw_ref[2:3, :])


def _nsa_prep(kc, vc, ks, kw, pos, w1, w2, knw):
    b = kc.shape[0]
    hd = NSA_HEAD_DIM
    flat = CMP_STRIDE * hd
    full = lambda shape: pl.BlockSpec(shape, lambda i: (0,) * len(shape))
    bspec = lambda r, w: pl.BlockSpec((1, r, w), lambda i: (i, 0, 0))
    return pl.pallas_call(
        _nsa_prep_kernel,
        out_shape=[jax.ShapeDtypeStruct((b, N_CMP_PAD, hd), F32)] * 2
        + [jax.ShapeDtypeStruct((b, SEQ, hd), F32)] * 2,
        grid=(b,),
        in_specs=[
            bspec(N_CMP_PAD, flat), bspec(N_CMP_PAD, flat), bspec(SEQ, hd), bspec(SEQ, hd),
            full((4, flat)), full((2, 2 * flat, CMP_HIDDEN)), full((2, CMP_HIDDEN, hd)), full((3, hd)),
        ],
        out_specs=[bspec(N_CMP_PAD, hd)] * 2 + [bspec(SEQ, hd)] * 2,
        compiler_params=_params(("parallel",)),
        name="nsa_prep",
    )(kc, vc, ks, kw, pos, w1, w2, knw)


def _nsa_attn_kernel(q_ref, qnw_ref, kcmp_ref, vcmp_ref, ks_ref, vs_ref, kw_ref, vw_ref,
                     gate_ref, bcmp_ref, btab_ref, o_ref, selx_ref):
    i = pl.program_id(1)
    tq = Q_TILE
    nh = NSA_HEADS
    hd = NSA_HEAD_DIM
    t0 = i * tq

    qs = jnp.concatenate(
        [_rms_rows(q_ref[0, h], qnw_ref[...]) * hd ** -0.5 for h in range(nh)], axis=0).astype(BF16)

    row = lax.broadcasted_iota(jnp.int32, (tq, LANES), 0)
    lane = lax.broadcasted_iota(jnp.int32, (tq, LANES), 1)
    qpos = t0 + row

    s_all = _mm_nt(qs, kcmp_ref[0])
    cmp_valid = (qpos >= lane * CMP_STRIDE + (CMP_LEN - 1)) & (lane < N_CMP_PAD - 1)
    p_rows = []
    p_sum = jnp.zeros((tq, LANES), F32)
    for h in range(nh):
        s = jnp.where(cmp_valid, s_all[h * tq:(h + 1) * tq] + bcmp_ref[h], NEG_INF)
        m = jnp.max(s, axis=-1, keepdims=True)
        p = jnp.where(cmp_valid, jnp.exp(s - m), 0.0)
        l = jnp.sum(p, axis=-1, keepdims=True)
        p = p * jnp.where(l > 0.0, 1.0 / l, 0.0)
        p_rows.append(p)
        p_sum = p_sum + p
    o_cmp = _mm(jnp.concatenate(p_rows, axis=0), vcmp_ref[0])

    jj = lax.broadcasted_iota(jnp.int32, (N_CMP_PAD, LANES), 0)
    ss = lax.broadcasted_iota(jnp.int32, (N_CMP_PAD, LANES), 1)
    overlap = ((jj * CMP_STRIDE < ss * SLC_BLOCK + SLC_BLOCK)
               & (jj * CMP_STRIDE + CMP_LEN > ss * SLC_BLOCK)
               & (jj < N_CMP_PAD - 1) & (ss < N_SLC))
    imp = _dot01_right(p_sum, jnp.where(overlap, 1.0, 0.0).astype(BF16))
    cur = qpos // SLC_BLOCK
    causal_blk = lane <= cur
    forced = (lane == 0) | (lane == cur) | (lane == cur - 1)
    imp = jnp.where(causal_blk & forced, FORCE, jnp.where(causal_blk, imp, -1.0))
    imp = jnp.where(lane < N_SLC, imp, -2.0)
    rank = jnp.zeros((tq, LANES), jnp.int32)
    for s2 in range(N_SLC):
        col = _bcast_col(imp, s2, LANES)
        beats = (col > imp) | ((col == imp) & (lane > s2))
        rank = rank + jnp.where(beats, 1, 0)
    sel = jnp.where((rank < SLC_TOP_N) & (lane < N_SLC), 1.0, 0.0).astype(BF16)
    er = lax.broadcasted_iota(jnp.int32, (LANES, SEQ), 0)
    ec = lax.broadcasted_iota(jnp.int32, (LANES, SEQ), 1)
    expand = jnp.where(ec // SLC_BLOCK == er, 1.0, 0.0).astype(BF16)
    selx_ref[...] = jnp.dot(sel, expand, preferred_element_type=F32)

    kcol = lax.broadcasted_iota(jnp.int32, (tq, tq), 1)
    qrow = t0 + lax.broadcasted_iota(jnp.int32, (tq, tq), 0)

    def attend(carry, k_tile, v_tile, bias_idx, mask):
        m, l, acc = carry
        s_t = _mm_nt(qs, k_tile)
        rows = [jnp.where(mask, s_t[h * tq:(h + 1) * tq] + btab_ref[h, bias_idx], NEG_INF)
                for h in range(nh)]
        s_m = jnp.concatenate(rows, axis=0)
        m_new = jnp.maximum(m, jnp.max(s_m, axis=-1, keepdims=True))
        alpha = jnp.exp(m - m_new)
        p = jnp.where(s_m > 0.5 * NEG_INF, jnp.exp(s_m - m_new), 0.0)
        l = l * alpha + jnp.sum(p, axis=-1, keepdims=True)
        acc = acc * alpha + _mm(p, v_tile)
        return m_new, l, acc

    def finish(carry):
        _, l, acc = carry
        return acc * jnp.where(l > 0.0, 1.0 / l, 0.0)

    init = (jnp.full((nh * tq, 1), NEG_INF, F32), jnp.zeros((nh * tq, 1), F32),
            jnp.zeros((nh * tq, hd), F32))

    def slc_body(kt, carry):
        k0 = pl.multiple_of(kt * tq, tq)
        mask = (selx_ref[:, pl.ds(k0, tq)] > 0.5) & (qrow >= k0 + kcol)
        return attend(carry, ks_ref[0, pl.ds(k0, tq), :], vs_ref[0, pl.ds(k0, tq), :],
                      jnp.minimum(i - kt, 2), mask)

    o_slc = finish(lax.fori_loop(0, i + 1, slc_body, init))

    carry = init
    for d in range(WINDOW // tq, -1, -1):
        kt = jnp.maximum(i - d, 0)
        k0 = pl.multiple_of(kt * tq, tq)
        dist = qrow - (k0 + kcol)
        mask = (dist >= 0) & (dist < WINDOW) & (i >= d)
        carry = attend(carry, kw_ref[0, pl.ds(k0, tq), :], vw_ref[0, pl.ds(k0, tq), :],
                       min(d, 2), mask)
    o_win = finish(carry)

    gates = _sigmoid(gate_ref[0])
    for h in range(nh):
        hs = slice(h * tq, (h + 1) * tq)
        g0 = 2 * DN_HEADS + 3 * h
        o_ref[0, h] = (_bcast_col(gates, g0, hd) * o_cmp[hs]
                       + _bcast_col(gates, g0 + 1, hd) * o_slc[hs]
                       + _bcast_col(gates, g0 + 2, hd) * o_win[hs])


def _nsa_attn(q, qnw, kcmp, vcmp, ks, vs, kw, vw, small, bias_cmp, btab):
    b = q.shape[0]
    tq = Q_TILE
    nh, hd = NSA_HEADS, NSA_HEAD_DIM
    full = lambda shape: pl.BlockSpec(shape, lambda bi, i: (0,) * len(shape))
    per_b = lambda r: pl.BlockSpec((1, r, hd), lambda bi, i: (bi, 0, 0))
    return pl.pallas_call(
        _nsa_attn_kernel,
        out_shape=jax.ShapeDtypeStruct((b, nh, SEQ, hd), F32),
        grid=(b, SEQ // tq),
        in_specs=[
            pl.BlockSpec((1, nh, tq, hd), lambda bi, i: (bi, 0, i, 0)),
            full((1, hd)),
            per_b(N_CMP_PAD), per_b(N_CMP_PAD), per_b(SEQ), per_b(SEQ), per_b(SEQ), per_b(SEQ),
            pl.BlockSpec((1, tq, LANES), lambda bi, i: (bi, i, 0)),
            pl.BlockSpec((nh, tq, N_CMP_PAD), lambda bi, i: (0, i, 0)),
            full((nh, 3, tq, tq)),
        ],
        out_specs=pl.BlockSpec((1, nh, tq, hd), lambda bi, i: (bi, 0, i, 0)),
        scratch_shapes=[pltpu.VMEM((tq, SEQ), F32)],
        compiler_params=_params(("parallel", "arbitrary")),
        name="nsa_attn",
    )(q, qnw, kcmp, vcmp, ks, vs, kw, vw, small, bias_cmp, btab)


def _out_router_kernel(ya_ref, yb_ref, yc_ref, x_ref, wo_ref, fnw_ref, rw_ref, rb_ref,
                       xo_ref, h_ref, idx_ref, wt_ref):
    wa = DN_WIDTH
    wb = wa + NSA_WIDTH
    y = (jnp.dot(ya_ref[...].astype(BF16), wo_ref[0:wa, :], preferred_element_type=F32)
         + jnp.dot(yb_ref[...].astype(BF16), wo_ref[wa:wb, :], preferred_element_type=F32)
         + jnp.dot(yc_ref[...].astype(BF16), wo_ref[wb:, :], preferred_element_type=F32))
    xn = x_ref[...] + y
    xo_ref[...] = xn
    h = xn * lax.rsqrt(jnp.mean(xn * xn, axis=-1, keepdims=True) + EPS) * fnw_ref[...]
    h_ref[...] = h
    logits = _dot_f32(h, rw_ref[...]) + rb_ref[...]
    lane = lax.broadcasted_iota(jnp.int32, logits.shape, 1)
    vals, idxs = [], []
    for _ in range(TOP_K):
        m = jnp.max(logits, axis=-1, keepdims=True)
        ix = jnp.min(jnp.where(logits == m, lane, LANES), axis=-1, keepdims=True)
        vals.append(m)
        idxs.append(ix)
        logits = jnp.where(lane == ix, -jnp.inf, logits)
    es = [jnp.exp(v - vals[0]) for v in vals]
    inv = 1.0 / (es[0] + es[1] + es[2] + es[3])
    idx_out = jnp.zeros(lane.shape, jnp.int32)
    wt_out = jnp.zeros(lane.shape, F32)
    for k in range(TOP_K):
        idx_out = jnp.where(lane == k, idxs[k], idx_out)
        wt_out = jnp.where(lane == k, es[k] * inv, wt_out)
    idx_ref[...] = idx_out
    wt_ref[...] = wt_out


def _out_router(ya, yb, yc, x2, w_out, fnw, rw, rb):
    n = x2.shape[0]
    tm = ROW_TILE
    row = lambda w: pl.BlockSpec((tm, w), lambda i: (i, 0))
    full = lambda shape: pl.BlockSpec(shape, lambda i: (0,) * len(shape))
    return pl.pallas_call(
        _out_router_kernel,
        out_shape=[jax.ShapeDtypeStruct((n, D_MODEL), F32), jax.ShapeDtypeStruct((n, D_MODEL), F32),
                   jax.ShapeDtypeStruct((n, LANES), jnp.int32), jax.ShapeDtypeStruct((n, LANES), F32)],
        grid=(n // tm,),
        in_specs=[row(DN_WIDTH), row(NSA_WIDTH), row(CONV_WIDTH), row(D_MODEL),
                  full((D_MODEL, D_MODEL)), full((1, D_MODEL)), full((D_MODEL, LANES)), full((1, LANES))],
        out_specs=[row(D_MODEL), row(D_MODEL), row(LANES), row(LANES)],
        compiler_params=_params(("parallel",)),
        name="out_router",
    )(ya, yb, yc, x2, w_out, fnw, rw, rb)


def _row_copy(src_hbm, dst_hbm, src_row, dst_row, sem):
    return pltpu.make_async_copy(src_hbm.at[pl.ds(src_row, 1)], dst_hbm.at[pl.ds(dst_row, 1)], sem)


def _gather_rows_kernel(idx_ref, src_hbm, dst_hbm, sem):
    i = pl.program_id(0)
    base = i * GATHER_BLOCK

    def start(r, c):
        _row_copy(src_hbm, dst_hbm, idx_ref[base + r], base + r, sem).start()
        return c

    def wait_one(r, c):
        _row_copy(src_hbm, dst_hbm, 0, 0, sem).wait()
        return c

    lax.fori_loop(0, GATHER_BLOCK, start, 0)

    @pl.when(i > 0)
    def _():
        lax.fori_loop(0, GATHER_BLOCK, wait_one, 0)

    @pl.when(i == pl.num_programs(0) - 1)
    def _():
        lax.fori_loop(0, GATHER_BLOCK, wait_one, 0)


def _gather_rows(idx, src, n_out):
    return pl.pallas_call(
        _gather_rows_kernel,
        out_shape=jax.ShapeDtypeStruct((n_out, src.shape[1]), src.dtype),
        grid_spec=pltpu.PrefetchScalarGridSpec(
            num_scalar_prefetch=1,
            grid=(n_out // GATHER_BLOCK,),
            in_specs=[pl.BlockSpec(memory_space=pl.ANY)],
            out_specs=pl.BlockSpec(memory_space=pl.ANY),
            scratch_shapes=[pltpu.SemaphoreType.DMA],
        ),
        compiler_params=_params(("arbitrary",)),
        name="gather_rows",
    )(idx, src)


def _expert_kernel(be_ref, nv_ref, xs_ref, wgu_ref, bgu_ref, wd_ref, bd_ref, ws_ref, o_ref):
    i = pl.program_id(0)

    @pl.when(i < nv_ref[0])
    def _():
        gu = jnp.dot(xs_ref[...].astype(BF16), wgu_ref[0], preferred_element_type=F32) + bgu_ref[0]
        gate = jnp.minimum(gu[:, :D_FF], SWIGLU_LIMIT)
        up = jnp.clip(gu[:, D_FF:], -SWIGLU_LIMIT, SWIGLU_LIMIT)
        act = (up + 1.0) * gate * _sigmoid(SWIGLU_ALPHA * gate)
        y = jnp.dot(act.astype(BF16), wd_ref[0], preferred_element_type=F32) + bd_ref[0]
        o_ref[...] = y * ws_ref[...]

    @pl.when(i >= nv_ref[0])
    def _():
        o_ref[...] = jnp.zeros(o_ref.shape, F32)


def _expert_ffn(block_expert, n_valid, xs, wgu, bgu, wd, bd, w_slot):
    n_slots = xs.shape[0]
    tm = MOE_TILE
    return pl.pallas_call(
        _expert_kernel,
        out_shape=jax.ShapeDtypeStruct((n_slots, D_MODEL), F32),
        grid_spec=pltpu.PrefetchScalarGridSpec(
            num_scalar_prefetch=2,
            grid=(n_slots // tm,),
            in_specs=[
                pl.BlockSpec((tm, D_MODEL), lambda i, be, nv: (i, 0)),
                pl.BlockSpec((1, D_MODEL, 2 * D_FF), lambda i, be, nv: (be[i], 0, 0)),
                pl.BlockSpec((1, 1, 2 * D_FF), lambda i, be, nv: (be[i], 0, 0)),
                pl.BlockSpec((1, D_FF, D_MODEL), lambda i, be, nv: (be[i], 0, 0)),
                pl.BlockSpec((1, 1, D_MODEL), lambda i, be, nv: (be[i], 0, 0)),
                pl.BlockSpec((tm, 1), lambda i, be, nv: (i, 0)),
            ],
            out_specs=pl.BlockSpec((tm, D_MODEL), lambda i, be, nv: (i, 0)),
        ),
        compiler_params=_params(("arbitrary",)),
        name="expert_ffn",
    )(block_expert, n_valid, xs, wgu, bgu, wd, bd, w_slot)


def _combine_kernel(slot_ref, ys_hbm, x_ref, o_ref, buf, sem):
    i = pl.program_id(0)
    tc = COMBINE_TILE
    base = i * tc * TOP_K

    def issue(r, c):
        for k in range(TOP_K):
            pltpu.make_async_copy(ys_hbm.at[pl.ds(slot_ref[base + r * TOP_K + k], 1)],
                                  buf.at[k, pl.ds(r, 1)], sem).start()
        return c

    def wait(r, c):
        pltpu.make_async_copy(ys_hbm.at[pl.ds(0, 1)], buf.at[0, pl.ds(0, 1)], sem).wait()
        return c

    lax.fori_loop(0, tc, issue, 0)
    lax.fori_loop(0, tc * TOP_K, wait, 0)
    o_ref[...] = x_ref[...] + ((buf[0] + buf[1]) + (buf[2] + buf[3]))


def _combine(slot_flat, ys, x2):
    n = x2.shape[0]
    tc = COMBINE_TILE
    return pl.pallas_call(
        _combine_kernel,
        out_shape=jax.ShapeDtypeStruct((n, D_MODEL), F32),
        grid_spec=pltpu.PrefetchScalarGridSpec(
            num_scalar_prefetch=1,
            grid=(n // tc,),
            in_specs=[pl.BlockSpec(memory_space=pl.ANY),
                      pl.BlockSpec((tc, D_MODEL), lambda i, s: (i, 0))],
            out_specs=pl.BlockSpec((tc, D_MODEL), lambda i, s: (i, 0)),
            scratch_shapes=[pltpu.VMEM((TOP_K, tc, D_MODEL), F32), pltpu.SemaphoreType.DMA],
        ),
        compiler_params=_params(("arbitrary",)),
        name="combine",
    )(slot_flat, ys, x2)


def _moe(h, x2, top_idx, top_w, wgu, bgu, wd, bd):
    n = h.shape[0]
    tm = MOE_TILE
    n_assign = n * TOP_K
    n_slots = n_assign + N_EXPERTS * tm
    n_blocks = n_slots // tm
    flat_e = top_idx.reshape(-1)
    order = jnp.argsort(flat_e)
    sorted_e = flat_e[order]
    counts = jnp.bincount(flat_e, length=N_EXPERTS)
    padded = (counts + tm - 1) // tm * tm
    pad_end = jnp.cumsum(padded)
    pad_start = pad_end - padded
    start = jnp.cumsum(counts) - counts
    slot = (pad_start[sorted_e] + jnp.arange(n_assign) - start[sorted_e]).astype(jnp.int32)
    tok_of_slot = jnp.zeros((n_slots,), jnp.int32).at[slot].set((order // TOP_K).astype(jnp.int32))
    w_of_slot = jnp.zeros((n_slots,), F32).at[slot].set(top_w.reshape(-1)[order])
    slot_of_assign = jnp.zeros((n_assign,), jnp.int32).at[order].set(slot)
    block_expert = jnp.minimum(
        jnp.searchsorted(pad_end, jnp.arange(n_blocks) * tm, side='right'), N_EXPERTS - 1).astype(jnp.int32)
    n_valid = (pad_end[-1:] // tm).astype(jnp.int32)

    xs = _gather_rows(tok_of_slot, h, n_slots)
    ys = _expert_ffn(block_expert, n_valid, xs, wgu, bgu, wd, bd, w_of_slot[:, None])
    return _combine(slot_of_assign, ys, x2)


def _t5_bucket(dist):
    n = jnp.maximum(dist, 0)
    max_exact = REL_BUCKETS // 2
    nf = jnp.maximum(n, 1).astype(F32)
    large = max_exact + (jnp.log(nf / max_exact) / math.log(REL_MAX_DIST / max_exact)
                         * (REL_BUCKETS - max_exact)).astype(jnp.int32)
    large = jnp.minimum(large, REL_BUCKETS - 1)
    return jnp.where(n < max_exact, n, large)


def _bias_tables(rel_bias):
    rel_bias = rel_bias.astype(F32)
    tq = Q_TILE
    t_pos = jnp.arange(SEQ)
    cmp_end = jnp.arange(N_CMP_PAD) * CMP_STRIDE + CMP_LEN - 1
    bias_cmp = rel_bias[_t5_bucket(t_pos[:, None] - cmp_end[None, :])].transpose(2, 0, 1)
    rr = jnp.arange(tq)[:, None] - jnp.arange(tq)[None, :]
    dist = jnp.stack([rr, rr + tq, rr + 2 * tq])
    btab = rel_bias[_t5_bucket(dist)].transpose(3, 0, 1, 2)
    return bias_cmp, btab


def _layer(x2, b, p, bias_cmp, btab):
    n = x2.shape[0]
    w = p['w_in']
    o_a, o_b, o_q, o_kv, o_g, o_u = 2048, 2052, 2056, 2312, 2696, 2708
    w_small = jnp.concatenate([w[:, o_a:o_q], w[:, o_g:o_u],
                               jnp.zeros((D_MODEL, LANES - 8 - 3 * NSA_HEADS), F32)], axis=1)
    w_cat = jnp.concatenate([w[:, :o_a], w[:, o_q:o_kv], w[:, o_kv:o_g], w[:, o_u:], w_small],
                            axis=1).astype(BF16)
    w_abt = jnp.concatenate([w[:, o_a:o_q].T, jnp.zeros((8, D_MODEL), F32)], axis=0).astype(BF16)
    qkv, z, nq, nkv, cu, small, abt = _in_proj(x2, p['attn_norm_w'][None, :], w_cat, w_abt)

    pcol = jnp.zeros((8, LANES), F32).at[0, :DN_HEADS].set(p['dn_a_log']).at[1, :DN_HEADS].set(p['dn_dt_bias'])
    prow = jnp.zeros((16, LANES), F32).at[:DN_HEADS, 0].set(p['dn_a_log']).at[:DN_HEADS, 1].set(p['dn_dt_bias'])
    small3 = small.reshape(b, SEQ, LANES)
    dq, dk, dv, gb, grow = _dn_prep(qkv.reshape(b, SEQ, 3 * DN_WIDTH), small3, abt,
                                    p['dn_conv_w'], pcol, prow)
    y_a = _delta_rule(dq, dk, dv, gb, grow, z.reshape(b, SEQ, DN_WIDTH), p['dn_norm_w'][None, :])

    y_c = _conformer(cu.reshape(b, SEQ, 2 * CONV_WIDTH), p['conv_dw_w'], p['conv_dw_b'][None, :],
                     p['conv_ln_w'][None, :], p['conv_ln_b'][None, :])

    hd = NSA_HEAD_DIM
    kv = nkv.reshape(b, SEQ, 6, hd)
    flat = CMP_STRIDE * hd
    kc = kv[:, :, 0].reshape(b, N_CMP_PAD, flat)
    vc = kv[:, :, 1].reshape(b, N_CMP_PAD, flat)
    pos = p['nsa_cmp_pos'].reshape(4, flat)
    kcmp, vcmp, ksn, kwn = _nsa_prep(kc, vc, kv[:, :, 2], kv[:, :, 4], pos,
                                     p['nsa_cmp_w1'].astype(BF16), p['nsa_cmp_w2'].astype(BF16),
                                     p['nsa_k_norm_w'])
    q4 = nq.reshape(b, SEQ, NSA_HEADS, hd).transpose(0, 2, 1, 3)
    o4 = _nsa_attn(q4, p['nsa_q_norm_w'][None, :], kcmp, vcmp, ksn, kv[:, :, 3], kwn, kv[:, :, 5],
                   small3, bias_cmp, btab)
    y_b = o4.transpose(0, 2, 1, 3).reshape(n, NSA_WIDTH)

    rw = jnp.concatenate([p['router_w'], jnp.zeros((D_MODEL, LANES - N_EXPERTS), F32)], axis=1)
    rb = jnp.concatenate([p['router_b'], jnp.full((LANES - N_EXPERTS,), NEG_INF, F32)])[None, :]
    x_new, h, idx, wt = _out_router(y_a.reshape(n, DN_WIDTH), y_b, y_c.reshape(n, CONV_WIDTH), x2,
                                    p['w_out'].astype(BF16), p['ffn_norm_w'][None, :], rw, rb)
    return _moe(h, x_new, idx[:, :TOP_K], wt[:, :TOP_K],
                p['w_gate_up'].astype(BF16), p['b_gate_up'][:, None, :],
                p['w_down'].astype(BF16), p['b_down'][:, None, :])


def kernel(x, attn_norm_w, w_in, dn_conv_w, dn_a_log, dn_dt_bias, dn_norm_w, nsa_q_norm_w, nsa_k_norm_w, nsa_cmp_pos, nsa_cmp_w1, nsa_cmp_w2, conv_dw_w, conv_dw_b, conv_ln_w, conv_ln_b, w_out, ffn_norm_w, router_w, router_b, w_gate_up, b_gate_up, w_down, b_down, rel_bias):
    b, t, d = x.shape
    assert (t, d) == (SEQ, D_MODEL)
    stacked = dict(attn_norm_w=attn_norm_w, w_in=w_in, dn_conv_w=dn_conv_w, dn_a_log=dn_a_log,
                   dn_dt_bias=dn_dt_bias, dn_norm_w=dn_norm_w, nsa_q_norm_w=nsa_q_norm_w,
                   nsa_k_norm_w=nsa_k_norm_w, nsa_cmp_pos=nsa_cmp_pos, nsa_cmp_w1=nsa_cmp_w1,
                   nsa_cmp_w2=nsa_cmp_w2, conv_dw_w=conv_dw_w, conv_dw_b=conv_dw_b,
                   conv_ln_w=conv_ln_w, conv_ln_b=conv_ln_b, w_out=w_out, ffn_norm_w=ffn_norm_w,
                   router_w=router_w, router_b=router_b, w_gate_up=w_gate_up, b_gate_up=b_gate_up,
                   w_down=w_down, b_down=b_down)
    bias_cmp, btab = _bias_tables(rel_bias)
    x2 = x.reshape(b * t, d)
    for l in range(w_in.shape[0]):
        x2 = _layer(x2, b, {k: v[l] for k, v in stacked.items()}, bias_cmp, btab)
    return x2.reshape(b, t, d)
```

```python
import functools
import math

import numpy as np
import jax
import jax.numpy as jnp
from jax import lax
from jax.experimental import pallas as pl
from jax.experimental.pallas import tpu as pltpu

F32 = jnp.float32
BF16 = jnp.bfloat16

D_MODEL = 1024
SEQ = 2048
DN_HEADS = 4
DN_HEAD_DIM = 128
DN_WIDTH = DN_HEADS * DN_HEAD_DIM
DN_CONV = 4
DN_CHUNK = 64
NSA_HEADS = 4
NSA_HEAD_DIM = 64
NSA_WIDTH = NSA_HEADS * NSA_HEAD_DIM
CMP_LEN = 32
CMP_STRIDE = 16
CMP_HIDDEN = 2 * NSA_HEAD_DIM
SLC_BLOCK = 64
SLC_TOP_N = 16
WINDOW = 512
CONV_WIDTH = 256
CONV_KERNEL = 31
REL_BUCKETS = 32
REL_MAX_DIST = 128
N_EXPERTS = 32
TOP_K = 4
D_FF = D_MODEL
SWIGLU_LIMIT = 7.0
SWIGLU_ALPHA = 1.702
EPS = 1e-6
NEG_INF = -1e30
FORCE = 1e4

LANES = 128
VMEM_LIMIT_BYTES = 48 * 1024 * 1024

ROW_TILE = 512
SEQ_TILE = 256
Q_TILE = 128
MOE_TILE = 512
COMBINE_TILE = 256

N_CMP_PAD = 128
N_SLC = SEQ // SLC_BLOCK


def _params(sem=None):
    return pltpu.CompilerParams(dimension_semantics=sem, vmem_limit_bytes=VMEM_LIMIT_BYTES)


def _mm(a, b):
    return jnp.dot(a.astype(BF16), b.astype(BF16), preferred_element_type=F32)


def _mm_nt(a, b):
    return lax.dot_general(a.astype(BF16), b.astype(BF16), (((1,), (1,)), ((), ())),
                           preferred_element_type=F32)


def _mm_tn(a, b):
    return lax.dot_general(a.astype(BF16), b.astype(BF16), (((0,), (0,)), ((), ())),
                           preferred_element_type=F32)


def _split3(x):
    hi = x.astype(BF16)
    r1 = x - hi.astype(F32)
    mid = r1.astype(BF16)
    lo = (r1 - mid.astype(F32)).astype(BF16)
    return hi, mid, lo


def _dot01_right(x, m01):
    hi, mid, lo = _split3(x)
    d = lambda p: jnp.dot(p, m01, preferred_element_type=F32)
    return d(hi) + d(mid) + d(lo)


def _dot01_left(m01, x):
    hi, mid, lo = _split3(x)
    d = lambda p: jnp.dot(m01, p, preferred_element_type=F32)
    return d(hi) + d(mid) + d(lo)


def _dot_f32(a, b):
    a_hi = a.astype(BF16)
    a_lo = (a - a_hi.astype(F32)).astype(BF16)
    b_hi = b.astype(BF16)
    b_lo = (b - b_hi.astype(F32)).astype(BF16)
    d = lambda p, q: jnp.dot(p, q, preferred_element_type=F32)
    return d(a_hi, b_hi) + d(a_hi, b_lo) + d(a_lo, b_hi) + d(a_lo, b_lo)


def _sigmoid(x):
    return 1.0 / (1.0 + jnp.exp(-x))


def _silu(x):
    return x * _sigmoid(x)


def _softplus(x):
    return jnp.maximum(x, 0.0) + jnp.log(1.0 + jnp.exp(-jnp.abs(x)))


def _bcast_col(x, j, width):
    return jnp.broadcast_to(x[:, j:j + 1], (x.shape[0], width))


IN_SEGS = (3 * DN_WIDTH, DN_WIDTH, NSA_WIDTH, 6 * NSA_HEAD_DIM, 2 * CONV_WIDTH, LANES)
IN_COLS = sum(IN_SEGS)


def _in_proj_kernel(x_ref, nw_ref, w_ref, wabt_ref,
                    qkv_ref, z_ref, nq_ref, nkv_ref, cu_ref, small_ref, abt_ref):
    xf = x_ref[...]
    ms = jnp.mean(xf * xf, axis=-1, keepdims=True)
    hb = (xf * lax.rsqrt(ms + EPS) * nw_ref[...]).astype(BF16)
    off = 0
    for ref, width in zip((qkv_ref, z_ref, nq_ref, nkv_ref, cu_ref, small_ref), IN_SEGS):
        ref[...] = jnp.dot(hb, w_ref[:, off:off + width], preferred_element_type=F32)
        off += width
    abt_ref[...] = lax.dot_general(wabt_ref[...], hb, (((1,), (1,)), ((), ())),
                                   preferred_element_type=F32)


def _in_proj(x2, norm_w, w_cat, w_abt):
    n = x2.shape[0]
    tm = ROW_TILE
    out_shape = [jax.ShapeDtypeStruct((n, w), F32) for w in IN_SEGS]
    out_shape.append(jax.ShapeDtypeStruct((16, n), F32))
    out_specs = [pl.BlockSpec((tm, w), lambda i: (i, 0)) for w in IN_SEGS]
    out_specs.append(pl.BlockSpec((16, tm), lambda i: (0, i)))
    return pl.pallas_call(
        _in_proj_kernel,
        out_shape=out_shape,
        grid=(n // tm,),
        in_specs=[
            pl.BlockSpec((tm, D_MODEL), lambda i: (i, 0)),
            pl.BlockSpec((1, D_MODEL), lambda i: (0, 0)),
            pl.BlockSpec((D_MODEL, IN_COLS), lambda i: (0, 0)),
            pl.BlockSpec((16, D_MODEL), lambda i: (0, 0)),
        ],
        out_specs=out_specs,
        compiler_params=_params(("parallel",)),
        name="in_proj",
    )(x2, norm_w, w_cat, w_abt)


def _dn_prep_kernel(qkv_ref, small_ref, abt_ref, cw_ref, pcol_ref, prow_ref,
                    q_ref, k_ref, v_ref, gb_ref, grow_ref, buf):
    t = pl.program_id(1)
    ts = SEQ_TILE

    @pl.when(t == 0)
    def _():
        buf[0:8, :] = jnp.zeros((8, 3 * DN_WIDTH), F32)

    buf[8:8 + ts, :] = qkv_ref[0]
    for c in range(3 * DN_HEADS):
        cs = slice(c * LANES, (c + 1) * LANES)
        acc = cw_ref[0:1, cs] * buf[5:5 + ts, cs]
        for j in range(1, DN_CONV):
            acc = acc + cw_ref[j:j + 1, cs] * buf[5 + j:5 + j + ts, cs]
        y = _silu(acc)
        if c < 2 * DN_HEADS:
            y = y * lax.rsqrt(jnp.sum(y * y, axis=-1, keepdims=True) + EPS)
        if c < DN_HEADS:
            q_ref[0, :, cs] = y * DN_HEAD_DIM ** -0.5
        elif c < 2 * DN_HEADS:
            k_ref[0, :, (c - DN_HEADS) * LANES:(c - DN_HEADS + 1) * LANES] = y
        else:
            v_ref[0, :, (c - 2 * DN_HEADS) * LANES:(c - 2 * DN_HEADS + 1) * LANES] = y
    buf[0:8, :] = buf[ts:ts + 8, :]

    sm = small_ref[0]
    lane = lax.broadcasted_iota(jnp.int32, sm.shape, 1)
    g_col = -jnp.exp(pcol_ref[0:1, :]) * _softplus(sm + pcol_ref[1:2, :])
    gb_ref[0] = jnp.where(lane < DN_HEADS, g_col, _sigmoid(sm))
    a_t = abt_ref[...]
    g_row = -jnp.exp(prow_ref[:, 0:1]) * _softplus(a_t + prow_ref[:, 1:2])
    grow_ref[...] = g_row[0:8, :]


def _dn_prep(qkv, small, abt, conv_w, pcol, prow):
    b = qkv.shape[0]
    ts = SEQ_TILE
    nt = SEQ // ts
    seq_spec = lambda w: pl.BlockSpec((1, ts, w), lambda i, t: (i, t, 0))
    full = lambda shape: pl.BlockSpec(shape, lambda i, t: (0,) * len(shape))
    return pl.pallas_call(
        _dn_prep_kernel,
        out_shape=[jax.ShapeDtypeStruct((b, SEQ, DN_WIDTH), F32)] * 3
        + [jax.ShapeDtypeStruct((b, SEQ, LANES), F32), jax.ShapeDtypeStruct((8, b * SEQ), F32)],
        grid=(b, nt),
        in_specs=[
            seq_spec(3 * DN_WIDTH), seq_spec(LANES),
            pl.BlockSpec((16, ts), lambda i, t: (0, i * nt + t)),
            full((DN_CONV, 3 * DN_WIDTH)), full((8, LANES)), full((16, LANES)),
        ],
        out_specs=[seq_spec(DN_WIDTH)] * 3
        + [seq_spec(LANES), pl.BlockSpec((8, ts), lambda i, t: (0, i * nt + t))],
        scratch_shapes=[pltpu.VMEM((ts + 8, 3 * DN_WIDTH), F32)],
        compiler_params=_params(("parallel", "arbitrary")),
        name="dn_prep",
    )(qkv, small, abt, conv_w, pcol, prow)


def _delta_kernel(q_ref, k_ref, v_ref, gb_ref, grow_ref, z_ref, nw_ref, o_ref, s_ref):
    t = pl.program_id(1)
    ts = SEQ_TILE
    ch = DN_CHUNK
    hd = DN_HEAD_DIM

    @pl.when(t == 0)
    def _():
        s_ref[...] = jnp.zeros(s_ref.shape, F32)

    r = lax.broadcasted_iota(jnp.int32, (ts, ts), 0)
    c = lax.broadcasted_iota(jnp.int32, (ts, ts), 1)
    same_chunk = (r // ch) == (c // ch)
    tril = same_chunk & (r >= c)
    strict = same_chunk & (r > c)
    same16 = (r // 16) == (c // 16)
    eye = jnp.where(r == c, 1.0, 0.0).astype(F32)
    m_col = jnp.where(tril, 1.0, 0.0).astype(BF16)
    m_row = jnp.where(same_chunk & (r <= c), 1.0, 0.0).astype(BF16)

    gb = gb_ref[0]
    gc_col = _dot01_left(m_col, gb)
    gc_row = _dot01_right(grow_ref[...], m_row)

    pre = []
    for h in range(DN_HEADS):
        hs = slice(h * hd, (h + 1) * hd)
        qh = q_ref[0, :, hs]
        kh = k_ref[0, :, hs]
        vh = v_ref[0, :, hs]
        gcb = _bcast_col(gc_col, h, ts)
        grb = jnp.broadcast_to(gc_row[h:h + 1, :], (ts, ts))
        decay = jnp.where(tril, jnp.exp(jnp.where(tril, gcb - grb, 0.0)), 0.0)
        beta = _bcast_col(gb, DN_HEADS + h, hd)
        gc128 = gcb[:, :hd]
        expg = jnp.exp(gc128)
        kb = kh * beta
        vb = vh * beta
        a_mat = jnp.where(strict, _mm_nt(kb, kh) * decay, 0.0)
        d_mat = jnp.where(same16, a_mat, 0.0)
        e_mat = a_mat - d_mat
        d2 = _mm(d_mat, d_mat)
        d4 = _mm(d2, d2)
        d8 = _mm(d4, d4)
        p_mat = _mm(_mm(_mm(eye - d_mat, eye + d2), eye + d4), eye + d8)
        m_mat = _mm(p_mat, e_mat)
        m2 = _mm(m_mat, m_mat)
        t_mat = _mm(_mm(eye - m_mat, eye + m2), p_mat)
        sol = _mm(t_mat, jnp.concatenate([vb, kb * expg], axis=1))
        attn = jnp.where(tril, _mm_nt(qh, kh) * decay, 0.0)
        pre.append((sol[:, :hd], sol[:, hd:], attn, qh * expg, kh, gc128))

    outs = [[] for _ in range(DN_HEADS)]
    for ci in range(ts // ch):
        rs = slice(ci * ch, (ci + 1) * ch)
        for h in range(DN_HEADS):
            u, w, attn, qg, kh, gc128 = pre[h]
            s_mat = s_ref[h]
            v_new = u[rs] - _mm(w[rs], s_mat)
            o_c = _mm(qg[rs], s_mat) + _mm(attn[rs, ci * ch:(ci + 1) * ch], v_new)
            g_last = gc128[ci * ch + ch - 1:ci * ch + ch, :]
            k_dec = kh[rs] * jnp.exp(g_last - gc128[rs])
            s_ref[h] = s_mat * jnp.exp(g_last) + _mm_tn(k_dec, v_new)
            outs[h].append(o_c)

    for h in range(DN_HEADS):
        hs = slice(h * hd, (h + 1) * hd)
        o = jnp.concatenate(outs[h], axis=0)
        o = o * lax.rsqrt(jnp.mean(o * o, axis=-1, keepdims=True) + EPS) * nw_ref[...]
        o_ref[0, :, hs] = o * _silu(z_ref[0, :, hs])


def _delta_rule(q, k, v, gb, grow, z, norm_w):
    b = q.shape[0]
    ts = SEQ_TILE
    nt = SEQ // ts
    seq_spec = lambda w: pl.BlockSpec((1, ts, w), lambda i, t: (i, t, 0))
    return pl.pallas_call(
        _delta_kernel,
        out_shape=jax.ShapeDtypeStruct((b, SEQ, DN_WIDTH), F32),
        grid=(b, nt),
        in_specs=[
            seq_spec(DN_WIDTH), seq_spec(DN_WIDTH), seq_spec(DN_WIDTH), seq_spec(LANES),
            pl.BlockSpec((8, ts), lambda i, t: (0, i * nt + t)),
            seq_spec(DN_WIDTH),
            pl.BlockSpec((1, DN_HEAD_DIM), lambda i, t: (0, 0)),
        ],
        out_specs=seq_spec(DN_WIDTH),
        scratch_shapes=[pltpu.VMEM((DN_HEADS, DN_HEAD_DIM, DN_HEAD_DIM), F32)],
        compiler_params=_params(("parallel", "arbitrary")),
        name="delta_rule",
    )(q, k, v, gb, grow, z, norm_w)


CONV_HALO = 32


def _conformer_kernel(u_ref, w_ref, b_ref, lnw_ref, lnb_ref, o_ref, buf):
    t = pl.program_id(1)
    ts = SEQ_TILE

    @pl.when(t == 0)
    def _():
        buf[0:CONV_HALO, :] = jnp.zeros((CONV_HALO, CONV_WIDTH), F32)

    u = u_ref[0]
    buf[CONV_HALO:CONV_HALO + ts, :] = u[:, :CONV_WIDTH] * _sigmoid(u[:, CONV_WIDTH:])
    base = CONV_HALO - (CONV_KERNEL - 1)
    rows = 64
    for rc in range(ts // rows):
        parts = []
        for cc in range(CONV_WIDTH // LANES):
            cs = slice(cc * LANES, (cc + 1) * LANES)
            r0 = base + rc * rows
            acc = w_ref[0:1, cs] * buf[r0:r0 + rows, cs]
            for j in range(1, CONV_KERNEL):
                acc = acc + w_ref[j:j + 1, cs] * buf[r0 + j:r0 + j + rows, cs]
            parts.append(acc)
        h = jnp.concatenate(parts, axis=1) + b_ref[...]
        mu = jnp.mean(h, axis=-1, keepdims=True)
        var = jnp.mean(jnp.square(h - mu), axis=-1, keepdims=True)
        hn = (h - mu) * lax.rsqrt(var + EPS) * lnw_ref[...] + lnb_ref[...]
        o_ref[0, rc * rows:(rc + 1) * rows, :] = _silu(hn)
    buf[0:CONV_HALO, :] = buf[ts:ts + CONV_HALO, :]


def _conformer(u, dw_w, dw_b, ln_w, ln_b):
    b = u.shape[0]
    ts = SEQ_TILE
    full = lambda shape: pl.BlockSpec(shape, lambda i, t: (0,) * len(shape))
    return pl.pallas_call(
        _conformer_kernel,
        out_shape=jax.ShapeDtypeStruct((b, SEQ, CONV_WIDTH), F32),
        grid=(b, SEQ // ts),
        in_specs=[
            pl.BlockSpec((1, ts, 2 * CONV_WIDTH), lambda i, t: (i, t, 0)),
            full((CONV_KERNEL, CONV_WIDTH)), full((1, CONV_WIDTH)),
            full((1, CONV_WIDTH)), full((1, CONV_WIDTH)),
        ],
        out_specs=pl.BlockSpec((1, ts, CONV_WIDTH), lambda i, t: (i, t, 0)),
        scratch_shapes=[pltpu.VMEM((ts + CONV_HALO, CONV_WIDTH), F32)],
        compiler_params=_params(("parallel", "arbitrary")),
        name="conformer",
    )(u, dw_w, dw_b, ln_w, ln_b)


def _rms_rows(x, w):
    return x * lax.rsqrt(jnp.mean(x * x, axis=-1, keepdims=True) + EPS) * w


def _nsa_prep_kernel(kc_ref, vc_ref, ks_ref, kw_ref, pos_ref, w1_ref, w2_ref, knw_ref,
                     kcmp_ref, vcmp_ref, ksn_ref, kwn_ref):
    half = CMP_STRIDE * NSA_HEAD_DIM

    def compress(x, i):
        u_lo = _mm(x + pos_ref[2 * i:2 * i + 1, :], w1_ref[i, 0:half, :])
        u_hi = _mm(x + pos_ref[2 * i + 1:2 * i + 2, :], w1_ref[i, half:2 * half, :])
        hid = _silu(u_lo + pltpu.roll(u_hi, N_CMP_PAD - 1, axis=0))
        return _mm(hid, w2_ref[i])

    kcmp_ref[0] = _rms_rows(compress(kc_ref[0], 0), knw_ref[0:1, :])
    vcmp_ref[0] = compress(vc_ref[0], 1)
    ksn_ref[0] = _rms_rows(ks_ref[0], knw_ref[1:2, :])
    kwn_ref[0] = _rms_rows(kw_ref[0], knw_ref[2:3, :])


def _nsa_prep(kc, vc, ks, kw, pos, w1, w2, knw):
    b = kc.shape[0]
    hd = NSA_HEAD_DIM
    flat = CMP_STRIDE * hd
    full = lambda shape: pl.BlockSpec(shape, lambda i: (0,) * len(shape))
    bspec = lambda r, w: pl.BlockSpec((1, r, w), lambda i: (i, 0, 0))
    return pl.pallas_call(
        _nsa_prep_kernel,
        out_shape=[jax.ShapeDtypeStruct((b, N_CMP_PAD, hd), F32)] * 2
        + [jax.ShapeDtypeStruct((b, SEQ, hd), F32)] * 2,
        grid=(b,),
        in_specs=[
            bspec(N_CMP_PAD, flat), bspec(N_CMP_PAD, flat), bspec(SEQ, hd), bspec(SEQ, hd),
            full((4, flat)), full((2, 2 * flat, CMP_HIDDEN)), full((2, CMP_HIDDEN, hd)), full((3, hd)),
        ],
        out_specs=[bspec(N_CMP_PAD, hd)] * 2 + [bspec(SEQ, hd)] * 2,
        compiler_params=_params(("parallel",)),
        name="nsa_prep",
    )(kc, vc, ks, kw, pos, w1, w2, knw)


def _nsa_attn_kernel(q_ref, qnw_ref, kcmp_ref, vcmp_ref, ks_ref, vs_ref, kw_ref, vw_ref,
                     gate_ref, bcmp_ref, btab_ref, o_ref, selx_ref):
    i = pl.program_id(1)
    tq = Q_TILE
    nh = NSA_HEADS
    hd = NSA_HEAD_DIM
    t0 = i * tq

    qs = jnp.concatenate(
        [_rms_rows(q_ref[0, h], qnw_ref[...]) * hd ** -0.5 for h in range(nh)], axis=0).astype(BF16)

    row = lax.broadcasted_iota(jnp.int32, (tq, LANES), 0)
    lane = lax.broadcasted_iota(jnp.int32, (tq, LANES), 1)
    qpos = t0 + row

    s_all = _mm_nt(qs, kcmp_ref[0])
    cmp_valid = (qpos >= lane * CMP_STRIDE + (CMP_LEN - 1)) & (lane < N_CMP_PAD - 1)
    p_rows = []
    p_sum = jnp.zeros((tq, LANES), F32)
    for h in range(nh):
        s = jnp.where(cmp_valid, s_all[h * tq:(h + 1) * tq] + bcmp_ref[h], NEG_INF)
        m = jnp.max(s, axis=-1, keepdims=True)
        p = jnp.where(cmp_valid, jnp.exp(s - m), 0.0)
        l = jnp.sum(p, axis=-1, keepdims=True)
        p = p * jnp.where(l > 0.0, 1.0 / l, 0.0)
        p_rows.append(p)
        p_sum = p_sum + p
    o_cmp = _mm(jnp.concatenate(p_rows, axis=0), vcmp_ref[0])

    jj = lax.broadcasted_iota(jnp.int32, (N_CMP_PAD, LANES), 0)
    ss = lax.broadcasted_iota(jnp.int32, (N_CMP_PAD, LANES), 1)
    overlap = ((jj * CMP_STRIDE < ss * SLC_BLOCK + SLC_BLOCK)
               & (jj * CMP_STRIDE + CMP_LEN > ss * SLC_BLOCK)
               & (jj < N_CMP_PAD - 1) & (ss < N_SLC))
    imp = _dot01_right(p_sum, jnp.where(overlap, 1.0, 0.0).astype(BF16))
    cur = qpos // SLC_BLOCK
    causal_blk = lane <= cur
    forced = (lane == 0) | (lane == cur) | (lane == cur - 1)
    imp = jnp.where(causal_blk & forced, FORCE, jnp.where(causal_blk, imp, -1.0))
    imp = jnp.where(lane < N_SLC, imp, -2.0)
    rank = jnp.zeros((tq, LANES), jnp.int32)
    for s2 in range(N_SLC):
        col = _bcast_col(imp, s2, LANES)
        beats = (col > imp) | ((col == imp) & (lane > s2))
        rank = rank + jnp.where(beats, 1, 0)
    sel = jnp.where((rank < SLC_TOP_N) & (lane < N_SLC), 1.0, 0.0).astype(BF16)
    er = lax.broadcasted_iota(jnp.int32, (LANES, SEQ), 0)
    ec = lax.broadcasted_iota(jnp.int32, (LANES, SEQ), 1)
    expand = jnp.where(ec // SLC_BLOCK == er, 1.0, 0.0).astype(BF16)
    selx_ref[...] = jnp.dot(sel, expand, preferred_element_type=F32)

    kcol = lax.broadcasted_iota(jnp.int32, (tq, tq), 1)
    qrow = t0 + lax.broadcasted_iota(jnp.int32, (tq, tq), 0)

    def attend(carry, k_tile, v_tile, bias_idx, mask):
        m, l, acc = carry
        s_t = _mm_nt(qs, k_tile)
        rows = [jnp.where(mask, s_t[h * tq:(h + 1) * tq] + btab_ref[h, bias_idx], NEG_INF)
                for h in range(nh)]
        s_m = jnp.concatenate(rows, axis=0)
        m_new = jnp.maximum(m, jnp.max(s_m, axis=-1, keepdims=True))
        alpha = jnp.exp(m - m_new)
        p = jnp.where(s_m > 0.5 * NEG_INF, jnp.exp(s_m - m_new), 0.0)
        l = l * alpha + jnp.sum(p, axis=-1, keepdims=True)
        acc = acc * alpha + _mm(p, v_tile)
        return m_new, l, acc

    def finish(carry):
        _, l, acc = carry
        return acc * jnp.where(l > 0.0, 1.0 / l, 0.0)

    init = (jnp.full((nh * tq, 1), NEG_INF, F32), jnp.zeros((nh * tq, 1), F32),
            jnp.zeros((nh * tq, hd), F32))

    def slc_body(kt, carry):
        k0 = pl.multiple_of(kt * tq, tq)
        mask = (selx_ref[:, pl.ds(k0, tq)] > 0.5) & (qrow >= k0 + kcol)
        return attend(carry, ks_ref[0, pl.ds(k0, tq), :], vs_ref[0, pl.ds(k0, tq), :],
                      jnp.minimum(i - kt, 2), mask)

    o_slc = finish(lax.fori_loop(0, i + 1, slc_body, init))

    carry = init
    for d in range(WINDOW // tq, -1, -1):
        kt = jnp.maximum(i - d, 0)
        k0 = pl.multiple_of(kt * tq, tq)
        dist = qrow - (k0 + kcol)
        mask = (dist >= 0) & (dist < WINDOW) & (i >= d)
        carry = attend(carry, kw_ref[0, pl.ds(k0, tq), :], vw_ref[0, pl.ds(k0, tq), :],
                       min(d, 2), mask)
    o_win = finish(carry)

    gates = _sigmoid(gate_ref[0])
    for h in range(nh):
        hs = slice(h * tq, (h + 1) * tq)
        g0 = 2 * DN_HEADS + 3 * h
        o_ref[0, h] = (_bcast_col(gates, g0, hd) * o_cmp[hs]
                       + _bcast_col(gates, g0 + 1, hd) * o_slc[hs]
                       + _bcast_col(gates, g0 + 2, hd) * o_win[hs])


def _nsa_attn(q, qnw, kcmp, vcmp, ks, vs, kw, vw, small, bias_cmp, btab):
    b = q.shape[0]
    tq = Q_TILE
    nh, hd = NSA_HEADS, NSA_HEAD_DIM
    full = lambda shape: pl.BlockSpec(shape, lambda bi, i: (0,) * len(shape))
    per_b = lambda r: pl.BlockSpec((1, r, hd), lambda bi, i: (bi, 0, 0))
    return pl.pallas_call(
        _nsa_attn_kernel,
        out_shape=jax.ShapeDtypeStruct((b, nh, SEQ, hd), F32),
        grid=(b, SEQ // tq),
        in_specs=[
            pl.BlockSpec((1, nh, tq, hd), lambda bi, i: (bi, 0, i, 0)),
            full((1, hd)),
            per_b(N_CMP_PAD), per_b(N_CMP_PAD), per_b(SEQ), per_b(SEQ), per_b(SEQ), per_b(SEQ),
            pl.BlockSpec((1, tq, LANES), lambda bi, i: (bi, i, 0)),
            pl.BlockSpec((nh, tq, N_CMP_PAD), lambda bi, i: (0, i, 0)),
            full((nh, 3, tq, tq)),
        ],
        out_specs=pl.BlockSpec((1, nh, tq, hd), lambda bi, i: (bi, 0, i, 0)),
        scratch_shapes=[pltpu.VMEM((tq, SEQ), F32)],
        compiler_params=_params(("parallel", "arbitrary")),
        name="nsa_attn",
    )(q, qnw, kcmp, vcmp, ks, vs, kw, vw, small, bias_cmp, btab)


def _out_router_kernel(ya_ref, yb_ref, yc_ref, x_ref, wo_ref, fnw_ref, rw_ref, rb_ref,
                       xo_ref, h_ref, idx_ref, wt_ref, hist_ref):
    wa = DN_WIDTH
    wb = wa + NSA_WIDTH
    y = (jnp.dot(ya_ref[...].astype(BF16), wo_ref[0:wa, :], preferred_element_type=F32)
         + jnp.dot(yb_ref[...].astype(BF16), wo_ref[wa:wb, :], preferred_element_type=F32)
         + jnp.dot(yc_ref[...].astype(BF16), wo_ref[wb:, :], preferred_element_type=F32))
    xn = x_ref[...] + y
    xo_ref[...] = xn
    h = xn * lax.rsqrt(jnp.mean(xn * xn, axis=-1, keepdims=True) + EPS) * fnw_ref[...]
    h_ref[...] = h
    logits = _dot_f32(h, rw_ref[...]) + rb_ref[...]
    lane = lax.broadcasted_iota(jnp.int32, logits.shape, 1)
    vals, idxs = [], []
    for _ in range(TOP_K):
        m = jnp.max(logits, axis=-1, keepdims=True)
        ix = jnp.min(jnp.where(logits == m, lane, LANES), axis=-1, keepdims=True)
        vals.append(m)
        idxs.append(ix)
        logits = jnp.where(lane == ix, -jnp.inf, logits)
    es = [jnp.exp(v - vals[0]) for v in vals]
    inv = 1.0 / (es[0] + es[1] + es[2] + es[3])
    idx_out = jnp.zeros(lane.shape, jnp.int32)
    wt_out = jnp.zeros(lane.shape, F32)
    chosen = jnp.zeros(lane.shape, F32)
    for k in range(TOP_K):
        idx_out = jnp.where(lane == k, idxs[k], idx_out)
        wt_out = jnp.where(lane == k, es[k] * inv, wt_out)
        chosen = chosen + jnp.where(lane == idxs[k], 1.0, 0.0)
    idx_ref[...] = idx_out
    wt_ref[...] = wt_out
    hist_ref[0] = jnp.broadcast_to(jnp.sum(chosen, axis=0, keepdims=True), (8, LANES))


def _out_router(ya, yb, yc, x2, w_out, fnw, rw, rb):
    n = x2.shape[0]
    tm = ROW_TILE
    row = lambda w: pl.BlockSpec((tm, w), lambda i: (i, 0))
    full = lambda shape: pl.BlockSpec(shape, lambda i: (0,) * len(shape))
    return pl.pallas_call(
        _out_router_kernel,
        out_shape=[jax.ShapeDtypeStruct((n, D_MODEL), F32), jax.ShapeDtypeStruct((n, D_MODEL), F32),
                   jax.ShapeDtypeStruct((n, LANES), jnp.int32), jax.ShapeDtypeStruct((n, LANES), F32),
                   jax.ShapeDtypeStruct((n // tm, 8, LANES), F32)],
        grid=(n // tm,),
        in_specs=[row(DN_WIDTH), row(NSA_WIDTH), row(CONV_WIDTH), row(D_MODEL),
                  full((D_MODEL, D_MODEL)), full((1, D_MODEL)), full((D_MODEL, LANES)), full((1, LANES))],
        out_specs=[row(D_MODEL), row(D_MODEL), row(LANES), row(LANES),
                   pl.BlockSpec((1, 8, LANES), lambda i: (i, 0, 0))],
        compiler_params=_params(("parallel",)),
        name="out_router",
    )(ya, yb, yc, x2, w_out, fnw, rw, rb)


def _slots_kernel(idx_ref, base_ref, slot_ref):
    tm = ROW_TILE
    idx = idx_ref[...]
    lane = lax.broadcasted_iota(jnp.int32, (tm, LANES), 1)
    onehots = [jnp.where(lane == _bcast_col(idx, k, LANES), 1.0, 0.0) for k in range(TOP_K)]
    cnt = (onehots[0] + onehots[1]) + (onehots[2] + onehots[3])
    r = lax.broadcasted_iota(jnp.int32, (tm, tm), 0)
    c = lax.broadcasted_iota(jnp.int32, (tm, tm), 1)
    earlier = jnp.where(r > c, 1.0, 0.0).astype(BF16)
    rank = jnp.dot(earlier, cnt.astype(BF16), preferred_element_type=F32) + base_ref[0, 0:1, :]
    ones = jnp.ones((8, LANES), BF16)
    row8 = lax.broadcasted_iota(jnp.int32, (8, tm), 0)
    out = jnp.zeros((8, tm), F32)
    for k in range(TOP_K):
        hi, mid, lo = _split3(rank * onehots[k])
        d = lambda p: lax.dot_general(ones, p, (((1,), (1,)), ((), ())), preferred_element_type=F32)
        out = jnp.where(row8 == k, d(hi) + d(mid) + d(lo), out)
    slot_ref[...] = out.astype(jnp.int32)


def _slots(idx, base3):
    n = idx.shape[0]
    tm = ROW_TILE
    return pl.pallas_call(
        _slots_kernel,
        out_shape=jax.ShapeDtypeStruct((8, n), jnp.int32),
        grid=(n // tm,),
        in_specs=[pl.BlockSpec((tm, LANES), lambda i: (i, 0)),
                  pl.BlockSpec((1, 8, LANES), lambda i: (i, 0, 0))],
        out_specs=pl.BlockSpec((8, tm), lambda i: (0, i)),
        compiler_params=_params(("parallel",)),
        name="moe_slots",
    )(idx, base3)


def _dispatch_kernel(slot_ref, h_ref, xs_hbm, sem):
    i = pl.program_id(0)
    tm = ROW_TILE
    n = slot_ref.shape[0] // TOP_K
    base = i * tm

    def body(r, c):
        for k in range(TOP_K):
            pltpu.make_async_copy(h_ref.at[pl.ds(r, 1)],
                                  xs_hbm.at[pl.ds(slot_ref[k * n + base + r], 1)], sem).start()
        return c

    lax.fori_loop(0, tm, body, 0, unroll=4)
    done = xs_hbm.at[pl.ds(0, TOP_K * tm)]
    pltpu.make_async_copy(done, done, sem).wait()


def _dispatch(slot_flat, h, n_slots):
    n = h.shape[0]
    tm = ROW_TILE
    return pl.pallas_call(
        _dispatch_kernel,
        out_shape=jax.ShapeDtypeStruct((n_slots, D_MODEL), F32),
        grid_spec=pltpu.PrefetchScalarGridSpec(
            num_scalar_prefetch=1,
            grid=(n // tm,),
            in_specs=[pl.BlockSpec((tm, D_MODEL), lambda i, s: (i, 0))],
            out_specs=pl.BlockSpec(memory_space=pl.ANY),
            scratch_shapes=[pltpu.SemaphoreType.DMA],
        ),
        compiler_params=_params(("arbitrary",)),
        name="moe_dispatch",
    )(slot_flat, h)


def _expert_kernel(be_ref, rows_ref, nv_ref, xs_ref, wgu_ref, bgu_ref, wd_ref, bd_ref, o_ref):
    i = pl.program_id(0)

    @pl.when(i < nv_ref[0])
    def _():
        row = lax.broadcasted_iota(jnp.int32, xs_ref.shape, 0)
        xb = jnp.where(row < rows_ref[i], xs_ref[...], 0.0).astype(BF16)
        gu = jnp.dot(xb, wgu_ref[0], preferred_element_type=F32) + bgu_ref[0]
        gate = jnp.minimum(gu[:, :D_FF], SWIGLU_LIMIT)
        up = jnp.clip(gu[:, D_FF:], -SWIGLU_LIMIT, SWIGLU_LIMIT)
        act = (up + 1.0) * gate * _sigmoid(SWIGLU_ALPHA * gate)
        o_ref[...] = jnp.dot(act.astype(BF16), wd_ref[0], preferred_element_type=F32) + bd_ref[0]

    @pl.when(i >= nv_ref[0])
    def _():
        o_ref[...] = jnp.zeros(o_ref.shape, F32)


def _expert_ffn(block_expert, rows_valid, n_valid, xs, wgu, bgu, wd, bd):
    n_slots = xs.shape[0]
    tm = MOE_TILE
    return pl.pallas_call(
        _expert_kernel,
        out_shape=jax.ShapeDtypeStruct((n_slots, D_MODEL), F32),
        grid_spec=pltpu.PrefetchScalarGridSpec(
            num_scalar_prefetch=3,
            grid=(n_slots // tm,),
            in_specs=[
                pl.BlockSpec((tm, D_MODEL), lambda i, be, rv, nv: (i, 0)),
                pl.BlockSpec((1, D_MODEL, 2 * D_FF), lambda i, be, rv, nv: (be[i], 0, 0)),
                pl.BlockSpec((1, 1, 2 * D_FF), lambda i, be, rv, nv: (be[i], 0, 0)),
                pl.BlockSpec((1, D_FF, D_MODEL), lambda i, be, rv, nv: (be[i], 0, 0)),
                pl.BlockSpec((1, 1, D_MODEL), lambda i, be, rv, nv: (be[i], 0, 0)),
            ],
            out_specs=pl.BlockSpec((tm, D_MODEL), lambda i, be, rv, nv: (i, 0)),
        ),
        compiler_params=_params(("arbitrary",)),
        name="expert_ffn",
    )(block_expert, rows_valid, n_valid, xs, wgu, bgu, wd, bd)


def _combine_kernel(slot_ref, ys_hbm, x_ref, wt_ref, o_ref, buf, sems):
    i = pl.program_id(0)
    tc = COMBINE_TILE
    nsteps = pl.num_programs(0)
    n = slot_ref.shape[0] // TOP_K
    cur = lax.rem(i, 2)

    def issue(step, par):
        base = step * tc

        def body(r, c):
            for k in range(TOP_K):
                pltpu.make_async_copy(ys_hbm.at[pl.ds(slot_ref[k * n + base + r], 1)],
                                      buf.at[par, k, pl.ds(r, 1)], sems.at[par]).start()
            return c

        lax.fori_loop(0, tc, body, 0, unroll=4)

    @pl.when(i == 0)
    def _():
        issue(0, 0)

    @pl.when(i + 1 < nsteps)
    def _():
        issue(i + 1, 1 - cur)

    pltpu.make_async_copy(buf.at[cur], buf.at[cur], sems.at[cur]).wait()
    w = wt_ref[...]
    acc = x_ref[...]
    for k in range(TOP_K):
        acc = acc + _bcast_col(w, k, D_MODEL) * buf[cur, k]
    o_ref[...] = acc


def _combine(slot_flat, ys, x2, wt):
    n = x2.shape[0]
    tc = COMBINE_TILE
    return pl.pallas_call(
        _combine_kernel,
        out_shape=jax.ShapeDtypeStruct((n, D_MODEL), F32),
        grid_spec=pltpu.PrefetchScalarGridSpec(
            num_scalar_prefetch=1,
            grid=(n // tc,),
            in_specs=[pl.BlockSpec(memory_space=pl.ANY),
                      pl.BlockSpec((tc, D_MODEL), lambda i, s: (i, 0)),
                      pl.BlockSpec((tc, LANES), lambda i, s: (i, 0))],
            out_specs=pl.BlockSpec((tc, D_MODEL), lambda i, s: (i, 0)),
            scratch_shapes=[pltpu.VMEM((2, TOP_K, tc, D_MODEL), F32), pltpu.SemaphoreType.DMA((2,))],
        ),
        compiler_params=_params(("arbitrary",)),
        name="moe_combine",
    )(slot_flat, ys, x2, wt)


def _moe(h, x2, idx, wt, hist3, wgu, bgu, wd, bd):
    n = h.shape[0]
    tm = MOE_TILE
    n_slots = n * TOP_K + N_EXPERTS * tm
    n_blocks = n_slots // tm
    hist = hist3[:, 0, :N_EXPERTS]
    counts = jnp.sum(hist, axis=0).astype(jnp.int32)
    padded = (counts + tm - 1) // tm * tm
    pad_end = jnp.cumsum(padded)
    pad_start = pad_end - padded
    tile_base = pad_start[None, :].astype(F32) + (jnp.cumsum(hist, axis=0) - hist)
    base3 = jnp.broadcast_to(jnp.pad(tile_base, ((0, 0), (0, LANES - N_EXPERTS)))[:, None, :],
                             (hist.shape[0], 8, LANES))
    blk0 = jnp.arange(n_blocks) * tm
    block_expert = jnp.minimum(jnp.searchsorted(pad_end, blk0, side='right'), N_EXPERTS - 1).astype(jnp.int32)
    rows_valid = jnp.clip(counts[block_expert] - (blk0 - pad_start[block_expert]), 0, tm).astype(jnp.int32)
    n_valid = (pad_end[-1:] // tm).astype(jnp.int32)

    slot_flat = _slots(idx, base3)[:TOP_K].reshape(-1)
    xs = _dispatch(slot_flat, h, n_slots)
    ys = _expert_ffn(block_expert, rows_valid, n_valid, xs, wgu, bgu, wd, bd)
    return _combine(slot_flat, ys, x2, wt)


def _t5_bucket(dist):
    n = jnp.maximum(dist, 0)
    max_exact = REL_BUCKETS // 2
    nf = jnp.maximum(n, 1).astype(F32)
    large = max_exact + (jnp.log(nf / max_exact) / math.log(REL_MAX_DIST / max_exact)
                         * (REL_BUCKETS - max_exact)).astype(jnp.int32)
    large = jnp.minimum(large, REL_BUCKETS - 1)
    return jnp.where(n < max_exact, n, large)


def _bias_tables(rel_bias):
    rel_bias = rel_bias.astype(F32)
    tq = Q_TILE
    t_pos = jnp.arange(SEQ)
    cmp_end = jnp.arange(N_CMP_PAD) * CMP_STRIDE + CMP_LEN - 1
    bias_cmp = rel_bias[_t5_bucket(t_pos[:, None] - cmp_end[None, :])].transpose(2, 0, 1)
    rr = jnp.arange(tq)[:, None] - jnp.arange(tq)[None, :]
    dist = jnp.stack([rr, rr + tq, rr + 2 * tq])
    btab = rel_bias[_t5_bucket(dist)].transpose(3, 0, 1, 2)
    return bias_cmp, btab


def _layer(x2, b, p, bias_cmp, btab):
    n = x2.shape[0]
    w = p['w_in']
    o_a, o_b, o_q, o_kv, o_g, o_u = 2048, 2052, 2056, 2312, 2696, 2708
    w_small = jnp.concatenate([w[:, o_a:o_q], w[:, o_g:o_u],
                               jnp.zeros((D_MODEL, LANES - 8 - 3 * NSA_HEADS), F32)], axis=1)
    w_cat = jnp.concatenate([w[:, :o_a], w[:, o_q:o_kv], w[:, o_kv:o_g], w[:, o_u:], w_small],
                            axis=1).astype(BF16)
    w_abt = jnp.concatenate([w[:, o_a:o_q].T, jnp.zeros((8, D_MODEL), F32)], axis=0).astype(BF16)
    qkv, z, nq, nkv, cu, small, abt = _in_proj(x2, p['attn_norm_w'][None, :], w_cat, w_abt)

    pcol = jnp.zeros((8, LANES), F32).at[0, :DN_HEADS].set(p['dn_a_log']).at[1, :DN_HEADS].set(p['dn_dt_bias'])
    prow = jnp.zeros((16, LANES), F32).at[:DN_HEADS, 0].set(p['dn_a_log']).at[:DN_HEADS, 1].set(p['dn_dt_bias'])
    small3 = small.reshape(b, SEQ, LANES)
    dq, dk, dv, gb, grow = _dn_prep(qkv.reshape(b, SEQ, 3 * DN_WIDTH), small3, abt,
                                    p['dn_conv_w'], pcol, prow)
    y_a = _delta_rule(dq, dk, dv, gb, grow, z.reshape(b, SEQ, DN_WIDTH), p['dn_norm_w'][None, :])

    y_c = _conformer(cu.reshape(b, SEQ, 2 * CONV_WIDTH), p['conv_dw_w'], p['conv_dw_b'][None, :],
                     p['conv_ln_w'][None, :], p['conv_ln_b'][None, :])

    hd = NSA_HEAD_DIM
    kv = nkv.reshape(b, SEQ, 6, hd)
    flat = CMP_STRIDE * hd
    kc = kv[:, :, 0].reshape(b, N_CMP_PAD, flat)
    vc = kv[:, :, 1].reshape(b, N_CMP_PAD, flat)
    pos = p['nsa_cmp_pos'].reshape(4, flat)
    kcmp, vcmp, ksn, kwn = _nsa_prep(kc, vc, kv[:, :, 2], kv[:, :, 4], pos,
                                     p['nsa_cmp_w1'].astype(BF16), p['nsa_cmp_w2'].astype(BF16),
                                     p['nsa_k_norm_w'])
    q4 = nq.reshape(b, SEQ, NSA_HEADS, hd).transpose(0, 2, 1, 3)
    o4 = _nsa_attn(q4, p['nsa_q_norm_w'][None, :], kcmp, vcmp, ksn, kv[:, :, 3], kwn, kv[:, :, 5],
                   small3, bias_cmp, btab)
    y_b = o4.transpose(0, 2, 1, 3).reshape(n, NSA_WIDTH)

    rw = jnp.concatenate([p['router_w'], jnp.zeros((D_MODEL, LANES - N_EXPERTS), F32)], axis=1)
    rb = jnp.concatenate([p['router_b'], jnp.full((LANES - N_EXPERTS,), NEG_INF, F32)])[None, :]
    x_new, h, idx, wt, hist3 = _out_router(y_a.reshape(n, DN_WIDTH), y_b, y_c.reshape(n, CONV_WIDTH), x2,
                                           p['w_out'].astype(BF16), p['ffn_norm_w'][None, :], rw, rb)
    return _moe(h, x_new, idx, wt, hist3,
                p['w_gate_up'].astype(BF16), p['b_gate_up'][:, None, :],
                p['w_down'].astype(BF16), p['b_down'][:, None, :])


def kernel(x, attn_norm_w, w_in, dn_conv_w, dn_a_log, dn_dt_bias, dn_norm_w, nsa_q_norm_w, nsa_k_norm_w, nsa_cmp_pos, nsa_cmp_w1, nsa_cmp_w2, conv_dw_w, conv_dw_b, conv_ln_w, conv_ln_b, w_out, ffn_norm_w, router_w, router_b, w_gate_up, b_gate_up, w_down, b_down, rel_bias):
    b, t, d = x.shape
    assert (t, d) == (SEQ, D_MODEL)
    stacked = dict(attn_norm_w=attn_norm_w, w_in=w_in, dn_conv_w=dn_conv_w, dn_a_log=dn_a_log,
                   dn_dt_bias=dn_dt_bias, dn_norm_w=dn_norm_w, nsa_q_norm_w=nsa_q_norm_w,
                   nsa_k_norm_w=nsa_k_norm_w, nsa_cmp_pos=nsa_cmp_pos, nsa_cmp_w1=nsa_cmp_w1,
                   nsa_cmp_w2=nsa_cmp_w2, conv_dw_w=conv_dw_w, conv_dw_b=conv_dw_b,
                   conv_ln_w=conv_ln_w, conv_ln_b=conv_ln_b, w_out=w_out, ffn_norm_w=ffn_norm_w,
                   router_w=router_w, router_b=router_b, w_gate_up=w_gate_up, b_gate_up=b_gate_up,
                   w_down=w_down, b_down=b_down)
    bias_cmp, btab = _bias_tables(rel_bias)
    x2 = x.reshape(b * t, d)
    for l in range(w_in.shape[0]):
        x2 = _layer(x2, b, {k: v[l] for k, v in stacked.items()}, bias_cmp, btab)
    return x2.reshape(b, t, d)
```

```python
import functools
import math

import numpy as np
import jax
import jax.numpy as jnp
from jax import lax
from jax.experimental import pallas as pl
from jax.experimental.pallas import tpu as pltpu

F32 = jnp.float32
BF16 = jnp.bfloat16

D_MODEL = 1024
SEQ = 2048
DN_HEADS = 4
DN_HEAD_DIM = 128
DN_WIDTH = DN_HEADS * DN_HEAD_DIM
DN_CONV = 4
DN_CHUNK = 64
NSA_HEADS = 4
NSA_HEAD_DIM = 64
NSA_WIDTH = NSA_HEADS * NSA_HEAD_DIM
CMP_LEN = 32
CMP_STRIDE = 16
CMP_HIDDEN = 2 * NSA_HEAD_DIM
SLC_BLOCK = 64
SLC_TOP_N = 16
WINDOW = 512
CONV_WIDTH = 256
CONV_KERNEL = 31
REL_BUCKETS = 32
REL_MAX_DIST = 128
N_EXPERTS = 32
TOP_K = 4
D_FF = D_MODEL
SWIGLU_LIMIT = 7.0
SWIGLU_ALPHA = 1.702
EPS = 1e-6
NEG_INF = -1e30
FORCE = 1e4

LANES = 128
VMEM_LIMIT_BYTES = 48 * 1024 * 1024

ROW_TILE = 512
SEQ_TILE = 256
Q_TILE = 128
MOE_TILE = 512
COMBINE_TILE = 256

N_CMP_PAD = 128
N_SLC = SEQ // SLC_BLOCK


def _params(sem=None):
    return pltpu.CompilerParams(dimension_semantics=sem, vmem_limit_bytes=VMEM_LIMIT_BYTES)


def _mm(a, b):
    return jnp.dot(a.astype(BF16), b.astype(BF16), preferred_element_type=F32)


def _mm_nt(a, b):
    return lax.dot_general(a.astype(BF16), b.astype(BF16), (((1,), (1,)), ((), ())),
                           preferred_element_type=F32)


def _mm_tn(a, b):
    return lax.dot_general(a.astype(BF16), b.astype(BF16), (((0,), (0,)), ((), ())),
                           preferred_element_type=F32)


def _split3(x):
    hi = x.astype(BF16)
    r1 = x - hi.astype(F32)
    mid = r1.astype(BF16)
    lo = (r1 - mid.astype(F32)).astype(BF16)
    return hi, mid, lo


def _dot01_right(x, m01):
    hi, mid, lo = _split3(x)
    d = lambda p: jnp.dot(p, m01, preferred_element_type=F32)
    return d(hi) + d(mid) + d(lo)


def _dot01_left(m01, x):
    hi, mid, lo = _split3(x)
    d = lambda p: jnp.dot(m01, p, preferred_element_type=F32)
    return d(hi) + d(mid) + d(lo)


def _dot_f32(a, b):
    a_hi = a.astype(BF16)
    a_lo = (a - a_hi.astype(F32)).astype(BF16)
    b_hi = b.astype(BF16)
    b_lo = (b - b_hi.astype(F32)).astype(BF16)
    d = lambda p, q: jnp.dot(p, q, preferred_element_type=F32)
    return d(a_hi, b_hi) + d(a_hi, b_lo) + d(a_lo, b_hi) + d(a_lo, b_lo)


def _sigmoid(x):
    return 1.0 / (1.0 + jnp.exp(-x))


def _silu(x):
    return x * _sigmoid(x)


def _softplus(x):
    return jnp.maximum(x, 0.0) + jnp.log(1.0 + jnp.exp(-jnp.abs(x)))


def _bcast_col(x, j, width):
    return jnp.broadcast_to(x[:, j:j + 1], (x.shape[0], width))


IN_SEGS = (3 * DN_WIDTH, DN_WIDTH, NSA_WIDTH, 6 * NSA_HEAD_DIM, 2 * CONV_WIDTH, LANES)
IN_COLS = sum(IN_SEGS)


def _in_proj_kernel(x_ref, nw_ref, w_ref, wabt_ref,
                    qkv_ref, z_ref, nq_ref, nkv_ref, cu_ref, small_ref, abt_ref):
    xf = x_ref[...]
    ms = jnp.mean(xf * xf, axis=-1, keepdims=True)
    hb = (xf * lax.rsqrt(ms + EPS) * nw_ref[...]).astype(BF16)
    off = 0
    for ref, width in zip((qkv_ref, z_ref, nq_ref, nkv_ref, cu_ref, small_ref), IN_SEGS):
        ref[...] = jnp.dot(hb, w_ref[:, off:off + width], preferred_element_type=F32)
        off += width
    abt_ref[...] = lax.dot_general(wabt_ref[...], hb, (((1,), (1,)), ((), ())),
                                   preferred_element_type=F32)


def _in_proj(x2, norm_w, w_cat, w_abt):
    n = x2.shape[0]
    tm = ROW_TILE
    out_shape = [jax.ShapeDtypeStruct((n, w), F32) for w in IN_SEGS]
    out_shape.append(jax.ShapeDtypeStruct((16, n), F32))
    out_specs = [pl.BlockSpec((tm, w), lambda i: (i, 0)) for w in IN_SEGS]
    out_specs.append(pl.BlockSpec((16, tm), lambda i: (0, i)))
    return pl.pallas_call(
        _in_proj_kernel,
        out_shape=out_shape,
        grid=(n // tm,),
        in_specs=[
            pl.BlockSpec((tm, D_MODEL), lambda i: (i, 0)),
            pl.BlockSpec((1, D_MODEL), lambda i: (0, 0)),
            pl.BlockSpec((D_MODEL, IN_COLS), lambda i: (0, 0)),
            pl.BlockSpec((16, D_MODEL), lambda i: (0, 0)),
        ],
        out_specs=out_specs,
        compiler_params=_params(("parallel",)),
        name="in_proj",
    )(x2, norm_w, w_cat, w_abt)


def _dn_prep_kernel(qkv_ref, small_ref, abt_ref, cw_ref, pcol_ref, prow_ref,
                    q_ref, k_ref, v_ref, gb_ref, grow_ref, buf):
    t = pl.program_id(1)
    ts = SEQ_TILE

    @pl.when(t == 0)
    def _():
        buf[0:8, :] = jnp.zeros((8, 3 * DN_WIDTH), F32)

    buf[8:8 + ts, :] = qkv_ref[0]
    for c in range(3 * DN_HEADS):
        cs = slice(c * LANES, (c + 1) * LANES)
        acc = cw_ref[0:1, cs] * buf[5:5 + ts, cs]
        for j in range(1, DN_CONV):
            acc = acc + cw_ref[j:j + 1, cs] * buf[5 + j:5 + j + ts, cs]
        y = _silu(acc)
        if c < 2 * DN_HEADS:
            y = y * lax.rsqrt(jnp.sum(y * y, axis=-1, keepdims=True) + EPS)
        if c < DN_HEADS:
            q_ref[0, :, cs] = y * DN_HEAD_DIM ** -0.5
        elif c < 2 * DN_HEADS:
            k_ref[0, :, (c - DN_HEADS) * LANES:(c - DN_HEADS + 1) * LANES] = y
        else:
            v_ref[0, :, (c - 2 * DN_HEADS) * LANES:(c - 2 * DN_HEADS + 1) * LANES] = y
    buf[0:8, :] = buf[ts:ts + 8, :]

    sm = small_ref[0]
    lane = lax.broadcasted_iota(jnp.int32, sm.shape, 1)
    g_col = -jnp.exp(pcol_ref[0:1, :]) * _softplus(sm + pcol_ref[1:2, :])
    gb_ref[0] = jnp.where(lane < DN_HEADS, g_col, _sigmoid(sm))
    a_t = abt_ref[...]
    g_row = -jnp.exp(prow_ref[:, 0:1]) * _softplus(a_t + prow_ref[:, 1:2])
    grow_ref[...] = g_row[0:8, :]


def _dn_prep(qkv, small, abt, conv_w, pcol, prow):
    b = qkv.shape[0]
    ts = SEQ_TILE
    nt = SEQ // ts
    seq_spec = lambda w: pl.BlockSpec((1, ts, w), lambda i, t: (i, t, 0))
    full = lambda shape: pl.BlockSpec(shape, lambda i, t: (0,) * len(shape))
    return pl.pallas_call(
        _dn_prep_kernel,
        out_shape=[jax.ShapeDtypeStruct((b, SEQ, DN_WIDTH), F32)] * 3
        + [jax.ShapeDtypeStruct((b, SEQ, LANES), F32), jax.ShapeDtypeStruct((8, b * SEQ), F32)],
        grid=(b, nt),
        in_specs=[
            seq_spec(3 * DN_WIDTH), seq_spec(LANES),
            pl.BlockSpec((16, ts), lambda i, t: (0, i * nt + t)),
            full((DN_CONV, 3 * DN_WIDTH)), full((8, LANES)), full((16, LANES)),
        ],
        out_specs=[seq_spec(DN_WIDTH)] * 3
        + [seq_spec(LANES), pl.BlockSpec((8, ts), lambda i, t: (0, i * nt + t))],
        scratch_shapes=[pltpu.VMEM((ts + 8, 3 * DN_WIDTH), F32)],
        compiler_params=_params(("parallel", "arbitrary")),
        name="dn_prep",
    )(qkv, small, abt, conv_w, pcol, prow)


def _delta_kernel(q_ref, k_ref, v_ref, gb_ref, grow_ref, z_ref, nw_ref, o_ref, s_ref):
    t = pl.program_id(1)
    ts = SEQ_TILE
    ch = DN_CHUNK
    hd = DN_HEAD_DIM

    @pl.when(t == 0)
    def _():
        s_ref[...] = jnp.zeros(s_ref.shape, F32)

    r = lax.broadcasted_iota(jnp.int32, (ts, ts), 0)
    c = lax.broadcasted_iota(jnp.int32, (ts, ts), 1)
    same_chunk = (r // ch) == (c // ch)
    tril = same_chunk & (r >= c)
    strict = same_chunk & (r > c)
    same16 = (r // 16) == (c // 16)
    eye = jnp.where(r == c, 1.0, 0.0).astype(F32)
    m_col = jnp.where(tril, 1.0, 0.0).astype(BF16)
    m_row = jnp.where(same_chunk & (r <= c), 1.0, 0.0).astype(BF16)

    gb = gb_ref[0]
    gc_col = _dot01_left(m_col, gb)
    gc_row = _dot01_right(grow_ref[...], m_row)

    pre = []
    for h in range(DN_HEADS):
        hs = slice(h * hd, (h + 1) * hd)
        qh = q_ref[0, :, hs]
        kh = k_ref[0, :, hs]
        vh = v_ref[0, :, hs]
        gcb = _bcast_col(gc_col, h, ts)
        grb = jnp.broadcast_to(gc_row[h:h + 1, :], (ts, ts))
        decay = jnp.where(tril, jnp.exp(jnp.where(tril, gcb - grb, 0.0)), 0.0)
        beta = _bcast_col(gb, DN_HEADS + h, hd)
        gc128 = gcb[:, :hd]
        expg = jnp.exp(gc128)
        kb = kh * beta
        vb = vh * beta
        a_mat = jnp.where(strict, _mm_nt(kb, kh) * decay, 0.0)
        d_mat = jnp.where(same16, a_mat, 0.0)
        e_mat = a_mat - d_mat
        d2 = _mm(d_mat, d_mat)
        d4 = _mm(d2, d2)
        d8 = _mm(d4, d4)
        p_mat = _mm(_mm(_mm(eye - d_mat, eye + d2), eye + d4), eye + d8)
        m_mat = _mm(p_mat, e_mat)
        m2 = _mm(m_mat, m_mat)
        t_mat = _mm(_mm(eye - m_mat, eye + m2), p_mat)
        sol = _mm(t_mat, jnp.concatenate([vb, kb * expg], axis=1))
        attn = jnp.where(tril, _mm_nt(qh, kh) * decay, 0.0)
        pre.append((sol[:, :hd], sol[:, hd:], attn, qh * expg, kh, gc128))

    outs = [[] for _ in range(DN_HEADS)]
    for ci in range(ts // ch):
        rs = slice(ci * ch, (ci + 1) * ch)
        for h in range(DN_HEADS):
            u, w, attn, qg, kh, gc128 = pre[h]
            s_mat = s_ref[h]
            v_new = u[rs] - _mm(w[rs], s_mat)
            o_c = _mm(qg[rs], s_mat) + _mm(attn[rs, ci * ch:(ci + 1) * ch], v_new)
            g_last = gc128[ci * ch + ch - 1:ci * ch + ch, :]
            k_dec = kh[rs] * jnp.exp(g_last - gc128[rs])
            s_ref[h] = s_mat * jnp.exp(g_last) + _mm_tn(k_dec, v_new)
            outs[h].append(o_c)

    for h in range(DN_HEADS):
        hs = slice(h * hd, (h + 1) * hd)
        o = jnp.concatenate(outs[h], axis=0)
        o = o * lax.rsqrt(jnp.mean(o * o, axis=-1, keepdims=True) + EPS) * nw_ref[...]
        o_ref[0, :, hs] = o * _silu(z_ref[0, :, hs])


def _delta_rule(q, k, v, gb, grow, z, norm_w):
    b = q.shape[0]
    ts = SEQ_TILE
    nt = SEQ // ts
    seq_spec = lambda w: pl.BlockSpec((1, ts, w), lambda i, t: (i, t, 0))
    return pl.pallas_call(
        _delta_kernel,
        out_shape=jax.ShapeDtypeStruct((b, SEQ, DN_WIDTH), F32),
        grid=(b, nt),
        in_specs=[
            seq_spec(DN_WIDTH), seq_spec(DN_WIDTH), seq_spec(DN_WIDTH), seq_spec(LANES),
            pl.BlockSpec((8, ts), lambda i, t: (0, i * nt + t)),
            seq_spec(DN_WIDTH),
            pl.BlockSpec((1, DN_HEAD_DIM), lambda i, t: (0, 0)),
        ],
        out_specs=seq_spec(DN_WIDTH),
        scratch_shapes=[pltpu.VMEM((DN_HEADS, DN_HEAD_DIM, DN_HEAD_DIM), F32)],
        compiler_params=_params(("parallel", "arbitrary")),
        name="delta_rule",
    )(q, k, v, gb, grow, z, norm_w)


CONV_HALO = 32


def _conformer_kernel(u_ref, w_ref, b_ref, lnw_ref, lnb_ref, o_ref, buf):
    t = pl.program_id(1)
    ts = SEQ_TILE

    @pl.when(t == 0)
    def _():
        buf[0:CONV_HALO, :] = jnp.zeros((CONV_HALO, CONV_WIDTH), F32)

    u = u_ref[0]
    buf[CONV_HALO:CONV_HALO + ts, :] = u[:, :CONV_WIDTH] * _sigmoid(u[:, CONV_WIDTH:])
    base = CONV_HALO - (CONV_KERNEL - 1)
    rows = 64
    for rc in range(ts // rows):
        parts = []
        for cc in range(CONV_WIDTH // LANES):
            cs = slice(cc * LANES, (cc + 1) * LANES)
            r0 = base + rc * rows
            acc = w_ref[0:1, cs] * buf[r0:r0 + rows, cs]
            for j in range(1, CONV_KERNEL):
                acc = acc + w_ref[j:j + 1, cs] * buf[r0 + j:r0 + j + rows, cs]
            parts.append(acc)
        h = jnp.concatenate(parts, axis=1) + b_ref[...]
        mu = jnp.mean(h, axis=-1, keepdims=True)
        var = jnp.mean(jnp.square(h - mu), axis=-1, keepdims=True)
        hn = (h - mu) * lax.rsqrt(var + EPS) * lnw_ref[...] + lnb_ref[...]
        o_ref[0, rc * rows:(rc + 1) * rows, :] = _silu(hn)
    buf[0:CONV_HALO, :] = buf[ts:ts + CONV_HALO, :]


def _conformer(u, dw_w, dw_b, ln_w, ln_b):
    b = u.shape[0]
    ts = SEQ_TILE
    full = lambda shape: pl.BlockSpec(shape, lambda i, t: (0,) * len(shape))
    return pl.pallas_call(
        _conformer_kernel,
        out_shape=jax.ShapeDtypeStruct((b, SEQ, CONV_WIDTH), F32),
        grid=(b, SEQ // ts),
        in_specs=[
            pl.BlockSpec((1, ts, 2 * CONV_WIDTH), lambda i, t: (i, t, 0)),
            full((CONV_KERNEL, CONV_WIDTH)), full((1, CONV_WIDTH)),
            full((1, CONV_WIDTH)), full((1, CONV_WIDTH)),
        ],
        out_specs=pl.BlockSpec((1, ts, CONV_WIDTH), lambda i, t: (i, t, 0)),
        scratch_shapes=[pltpu.VMEM((ts + CONV_HALO, CONV_WIDTH), F32)],
        compiler_params=_params(("parallel", "arbitrary")),
        name="conformer",
    )(u, dw_w, dw_b, ln_w, ln_b)


def _rms_rows(x, w):
    return x * lax.rsqrt(jnp.mean(x * x, axis=-1, keepdims=True) + EPS) * w


def _nsa_prep_kernel(kc_ref, vc_ref, ks_ref, kw_ref, pos_ref, w1_ref, w2_ref, knw_ref,
                     kcmp_ref, vcmp_ref, ksn_ref, kwn_ref):
    half = CMP_STRIDE * NSA_HEAD_DIM

    def compress(x, i):
        u_lo = _mm(x + pos_ref[2 * i:2 * i + 1, :], w1_ref[i, 0:half, :])
        u_hi = _mm(x + pos_ref[2 * i + 1:2 * i + 2, :], w1_ref[i, half:2 * half, :])
        hid = _silu(u_lo + pltpu.roll(u_hi, N_CMP_PAD - 1, axis=0))
        return _mm(hid, w2_ref[i])

    kcmp_ref[0] = _rms_rows(compress(kc_ref[0], 0), knw_ref[0:1, :])
    vcmp_ref[0] = compress(vc_ref[0], 1)
    ksn_ref[0] = _rms_rows(ks_ref[0], knw_ref[1:2, :])
    kwn_ref[0] = _rms_rows(kw_ref[0], knw_ref[2:3, :])


def _nsa_prep(kc, vc, ks, kw, pos, w1, w2, knw):
    b = kc.shape[0]
    hd = NSA_HEAD_DIM
    flat = CMP_STRIDE * hd
    full = lambda shape: pl.BlockSpec(shape, lambda i: (0,) * len(shape))
    bspec = lambda r, w: pl.BlockSpec((1, r, w), lambda i: (i, 0, 0))
    return pl.pallas_call(
        _nsa_prep_kernel,
        out_shape=[jax.ShapeDtypeStruct((b, N_CMP_PAD, hd), F32)] * 2
        + [jax.ShapeDtypeStruct((b, SEQ, hd), F32)] * 2,
        grid=(b,),
        in_specs=[
            bspec(N_CMP_PAD, flat), bspec(N_CMP_PAD, flat), bspec(SEQ, hd), bspec(SEQ, hd),
            full((4, flat)), full((2, 2 * flat, CMP_HIDDEN)), full((2, CMP_HIDDEN, hd)), full((3, hd)),
        ],
        out_specs=[bspec(N_CMP_PAD, hd)] * 2 + [bspec(SEQ, hd)] * 2,
        compiler_params=_params(("parallel",)),
        name="nsa_prep",
    )(kc, vc, ks, kw, pos, w1, w2, knw)


def _nsa_attn_kernel(q_ref, qnw_ref, kcmp_ref, vcmp_ref, ks_ref, vs_ref, kw_ref, vw_ref,
                     gate_ref, bcmp_ref, btab_ref, o_ref, selx_ref):
    i = pl.program_id(1)
    tq = Q_TILE
    nh = NSA_HEADS
    hd = NSA_HEAD_DIM
    t0 = i * tq

    qs = jnp.concatenate(
        [_rms_rows(q_ref[0, h], qnw_ref[...]) * hd ** -0.5 for h in range(nh)], axis=0).astype(BF16)

    row = lax.broadcasted_iota(jnp.int32, (tq, LANES), 0)
    lane = lax.broadcasted_iota(jnp.int32, (tq, LANES), 1)
    qpos = t0 + row

    s_all = _mm_nt(qs, kcmp_ref[0])
    cmp_valid = (qpos >= lane * CMP_STRIDE + (CMP_LEN - 1)) & (lane < N_CMP_PAD - 1)
    p_rows = []
    p_sum = jnp.zeros((tq, LANES), F32)
    for h in range(nh):
        s = jnp.where(cmp_valid, s_all[h * tq:(h + 1) * tq] + bcmp_ref[h], NEG_INF)
        m = jnp.max(s, axis=-1, keepdims=True)
        p = jnp.where(cmp_valid, jnp.exp(s - m), 0.0)
        l = jnp.sum(p, axis=-1, keepdims=True)
        p = p * jnp.where(l > 0.0, 1.0 / l, 0.0)
        p_rows.append(p)
        p_sum = p_sum + p
    o_cmp = _mm(jnp.concatenate(p_rows, axis=0), vcmp_ref[0])

    jj = lax.broadcasted_iota(jnp.int32, (N_CMP_PAD, LANES), 0)
    ss = lax.broadcasted_iota(jnp.int32, (N_CMP_PAD, LANES), 1)
    overlap = ((jj * CMP_STRIDE < ss * SLC_BLOCK + SLC_BLOCK)
               & (jj * CMP_STRIDE + CMP_LEN > ss * SLC_BLOCK)
               & (jj < N_CMP_PAD - 1) & (ss < N_SLC))
    imp = _dot01_right(p_sum, jnp.where(overlap, 1.0, 0.0).astype(BF16))
    cur = qpos // SLC_BLOCK
    causal_blk = lane <= cur
    forced = (lane == 0) | (lane == cur) | (lane == cur - 1)
    imp = jnp.where(causal_blk & forced, FORCE, jnp.where(causal_blk, imp, -1.0))
    imp = jnp.where(lane < N_SLC, imp, -2.0)
    rank = jnp.zeros((tq, LANES), jnp.int32)
    for s2 in range(N_SLC):
        col = _bcast_col(imp, s2, LANES)
        beats = (col > imp) | ((col == imp) & (lane > s2))
        rank = rank + jnp.where(beats, 1, 0)
    sel = jnp.where((rank < SLC_TOP_N) & (lane < N_SLC), 1.0, 0.0).astype(BF16)
    er = lax.broadcasted_iota(jnp.int32, (LANES, SEQ), 0)
    ec = lax.broadcasted_iota(jnp.int32, (LANES, SEQ), 1)
    expand = jnp.where(ec // SLC_BLOCK == er, 1.0, 0.0).astype(BF16)
    selx_ref[...] = jnp.dot(sel, expand, preferred_element_type=F32)

    kcol = lax.broadcasted_iota(jnp.int32, (tq, tq), 1)
    qrow = t0 + lax.broadcasted_iota(jnp.int32, (tq, tq), 0)

    def attend(carry, k_tile, v_tile, bias_idx, mask):
        m, l, acc = carry
        s_t = _mm_nt(qs, k_tile)
        rows = [jnp.where(mask, s_t[h * tq:(h + 1) * tq] + btab_ref[h, bias_idx], NEG_INF)
                for h in range(nh)]
        s_m = jnp.concatenate(rows, axis=0)
        m_new = jnp.maximum(m, jnp.max(s_m, axis=-1, keepdims=True))
        alpha = jnp.exp(m - m_new)
        p = jnp.where(s_m > 0.5 * NEG_INF, jnp.exp(s_m - m_new), 0.0)
        l = l * alpha + jnp.sum(p, axis=-1, keepdims=True)
        acc = acc * alpha + _mm(p, v_tile)
        return m_new, l, acc

    def finish(carry):
        _, l, acc = carry
        return acc * jnp.where(l > 0.0, 1.0 / l, 0.0)

    init = (jnp.full((nh * tq, 1), NEG_INF, F32), jnp.zeros((nh * tq, 1), F32),
            jnp.zeros((nh * tq, hd), F32))

    def slc_body(kt, carry):
        k0 = pl.multiple_of(kt * tq, tq)
        mask = (selx_ref[:, pl.ds(k0, tq)] > 0.5) & (qrow >= k0 + kcol)
        return attend(carry, ks_ref[0, pl.ds(k0, tq), :], vs_ref[0, pl.ds(k0, tq), :],
                      jnp.minimum(i - kt, 2), mask)

    o_slc = finish(lax.fori_loop(0, i + 1, slc_body, init))

    carry = init
    for d in range(WINDOW // tq, -1, -1):
        kt = jnp.maximum(i - d, 0)
        k0 = pl.multiple_of(kt * tq, tq)
        dist = qrow - (k0 + kcol)
        mask = (dist >= 0) & (dist < WINDOW) & (i >= d)
        carry = attend(carry, kw_ref[0, pl.ds(k0, tq), :], vw_ref[0, pl.ds(k0, tq), :],
                       min(d, 2), mask)
    o_win = finish(carry)

    gates = _sigmoid(gate_ref[0])
    for h in range(nh):
        hs = slice(h * tq, (h + 1) * tq)
        g0 = 2 * DN_HEADS + 3 * h
        o_ref[0, h] = (_bcast_col(gates, g0, hd) * o_cmp[hs]
                       + _bcast_col(gates, g0 + 1, hd) * o_slc[hs]
                       + _bcast_col(gates, g0 + 2, hd) * o_win[hs])


def _nsa_attn(q, qnw, kcmp, vcmp, ks, vs, kw, vw, small, bias_cmp, btab):
    b = q.shape[0]
    tq = Q_TILE
    nh, hd = NSA_HEADS, NSA_HEAD_DIM
    full = lambda shape: pl.BlockSpec(shape, lambda bi, i: (0,) * len(shape))
    per_b = lambda r: pl.BlockSpec((1, r, hd), lambda bi, i: (bi, 0, 0))
    return pl.pallas_call(
        _nsa_attn_kernel,
        out_shape=jax.ShapeDtypeStruct((b, nh, SEQ, hd), F32),
        grid=(b, SEQ // tq),
        in_specs=[
            pl.BlockSpec((1, nh, tq, hd), lambda bi, i: (bi, 0, i, 0)),
            full((1, hd)),
            per_b(N_CMP_PAD), per_b(N_CMP_PAD), per_b(SEQ), per_b(SEQ), per_b(SEQ), per_b(SEQ),
            pl.BlockSpec((1, tq, LANES), lambda bi, i: (bi, i, 0)),
            pl.BlockSpec((nh, tq, N_CMP_PAD), lambda bi, i: (0, i, 0)),
            full((nh, 3, tq, tq)),
        ],
        out_specs=pl.BlockSpec((1, nh, tq, hd), lambda bi, i: (bi, 0, i, 0)),
        scratch_shapes=[pltpu.VMEM((tq, SEQ), F32)],
        compiler_params=_params(("parallel", "arbitrary")),
        name="nsa_attn",
    )(q, qnw, kcmp, vcmp, ks, vs, kw, vw, small, bias_cmp, btab)


def _out_router_kernel(ya_ref, yb_ref, yc_ref, x_ref, wo_ref, fnw_ref, rw_ref, rb_ref,
                       xo_ref, h_ref, idx_ref, wt_ref, hist_ref):
    wa = DN_WIDTH
    wb = wa + NSA_WIDTH
    y = (jnp.dot(ya_ref[...].astype(BF16), wo_ref[0:wa, :], preferred_element_type=F32)
         + jnp.dot(yb_ref[...].astype(BF16), wo_ref[wa:wb, :], preferred_element_type=F32)
         + jnp.dot(yc_ref[...].astype(BF16), wo_ref[wb:, :], preferred_element_type=F32))
    xn = x_ref[...] + y
    xo_ref[...] = xn
    h = xn * lax.rsqrt(jnp.mean(xn * xn, axis=-1, keepdims=True) + EPS) * fnw_ref[...]
    h_ref[...] = h
    logits = _dot_f32(h, rw_ref[...]) + rb_ref[...]
    lane = lax.broadcasted_iota(jnp.int32, logits.shape, 1)
    vals, idxs = [], []
    for _ in range(TOP_K):
        m = jnp.max(logits, axis=-1, keepdims=True)
        ix = jnp.min(jnp.where(logits == m, lane, LANES), axis=-1, keepdims=True)
        vals.append(m)
        idxs.append(ix)
        logits = jnp.where(lane == ix, -jnp.inf, logits)
    es = [jnp.exp(v - vals[0]) for v in vals]
    inv = 1.0 / (es[0] + es[1] + es[2] + es[3])
    idx_out = jnp.zeros(lane.shape, jnp.int32)
    wt_out = jnp.zeros(lane.shape, F32)
    chosen = jnp.zeros(lane.shape, F32)
    for k in range(TOP_K):
        idx_out = jnp.where(lane == k, idxs[k], idx_out)
        wt_out = jnp.where(lane == k, es[k] * inv, wt_out)
        chosen = chosen + jnp.where(lane == idxs[k], 1.0, 0.0)
    idx_ref[...] = idx_out
    wt_ref[...] = wt_out
    hist_ref[0] = jnp.broadcast_to(jnp.sum(chosen, axis=0, keepdims=True), (8, LANES))


def _out_router(ya, yb, yc, x2, w_out, fnw, rw, rb):
    n = x2.shape[0]
    tm = ROW_TILE
    row = lambda w: pl.BlockSpec((tm, w), lambda i: (i, 0))
    full = lambda shape: pl.BlockSpec(shape, lambda i: (0,) * len(shape))
    return pl.pallas_call(
        _out_router_kernel,
        out_shape=[jax.ShapeDtypeStruct((n, D_MODEL), F32), jax.ShapeDtypeStruct((n, D_MODEL), F32),
                   jax.ShapeDtypeStruct((n, LANES), jnp.int32), jax.ShapeDtypeStruct((n, LANES), F32),
                   jax.ShapeDtypeStruct((n // tm, 8, LANES), F32)],
        grid=(n // tm,),
        in_specs=[row(DN_WIDTH), row(NSA_WIDTH), row(CONV_WIDTH), row(D_MODEL),
                  full((D_MODEL, D_MODEL)), full((1, D_MODEL)), full((D_MODEL, LANES)), full((1, LANES))],
        out_specs=[row(D_MODEL), row(D_MODEL), row(LANES), row(LANES),
                   pl.BlockSpec((1, 8, LANES), lambda i: (i, 0, 0))],
        compiler_params=_params(("parallel",)),
        name="out_router",
    )(ya, yb, yc, x2, w_out, fnw, rw, rb)


def _slots_kernel(idx_ref, base_ref, slot_ref):
    tm = ROW_TILE
    idx = idx_ref[...]
    lane = lax.broadcasted_iota(jnp.int32, (tm, LANES), 1)
    onehots = [jnp.where(lane == _bcast_col(idx, k, LANES), 1.0, 0.0) for k in range(TOP_K)]
    cnt = (onehots[0] + onehots[1]) + (onehots[2] + onehots[3])
    r = lax.broadcasted_iota(jnp.int32, (tm, tm), 0)
    c = lax.broadcasted_iota(jnp.int32, (tm, tm), 1)
    earlier = jnp.where(r > c, 1.0, 0.0).astype(BF16)
    rank = jnp.dot(earlier, cnt.astype(BF16), preferred_element_type=F32) + base_ref[0, 0:1, :]
    ones = jnp.ones((8, LANES), BF16)
    row8 = lax.broadcasted_iota(jnp.int32, (8, tm), 0)
    out = jnp.zeros((8, tm), F32)
    for k in range(TOP_K):
        hi, mid, lo = _split3(rank * onehots[k])
        d = lambda p: lax.dot_general(ones, p, (((1,), (1,)), ((), ())), preferred_element_type=F32)
        out = jnp.where(row8 == k, d(hi) + d(mid) + d(lo), out)
    slot_ref[...] = out.astype(jnp.int32)


def _slots(idx, base3):
    n = idx.shape[0]
    tm = ROW_TILE
    return pl.pallas_call(
        _slots_kernel,
        out_shape=jax.ShapeDtypeStruct((8, n), jnp.int32),
        grid=(n // tm,),
        in_specs=[pl.BlockSpec((tm, LANES), lambda i: (i, 0)),
                  pl.BlockSpec((1, 8, LANES), lambda i: (i, 0, 0))],
        out_specs=pl.BlockSpec((8, tm), lambda i: (0, i)),
        compiler_params=_params(("parallel",)),
        name="moe_slots",
    )(idx, base3)


def _dispatch_kernel(slot_ref, zoff_ref, nv_ref, h_ref, xs_hbm, zbuf, sem, zsem):
    i = pl.program_id(0)
    tm = ROW_TILE
    n = slot_ref.shape[0] // TOP_K
    base = i * tm

    @pl.when(i == 0)
    def _():
        zbuf[...] = jnp.zeros(zbuf.shape, F32)
        n_blocks = xs_hbm.shape[0] // MOE_TILE

        def fill(row0):
            return pltpu.make_async_copy(zbuf, xs_hbm.at[pl.ds(pl.multiple_of(row0, MOE_TILE), MOE_TILE)], zsem)

        def fill_region_end(e, c):
            fill(zoff_ref[e]).start()
            return c

        def fill_tail(b, c):
            fill(b * MOE_TILE).start()
            return c

        def wait_fill(b, c):
            fill(0).wait()
            return c

        lax.fori_loop(0, N_EXPERTS, fill_region_end, 0)
        lax.fori_loop(nv_ref[0], n_blocks, fill_tail, 0)
        lax.fori_loop(0, N_EXPERTS + n_blocks - nv_ref[0], wait_fill, 0)

    def body(r, c):
        for k in range(TOP_K):
            pltpu.make_async_copy(h_ref.at[pl.ds(r, 1)],
                                  xs_hbm.at[pl.ds(slot_ref[k * n + base + r], 1)], sem).start()
        return c

    lax.fori_loop(0, tm, body, 0, unroll=4)
    done = xs_hbm.at[pl.ds(0, TOP_K * tm)]
    pltpu.make_async_copy(done, done, sem).wait()


def _dispatch(slot_flat, zero_off, n_valid, h, n_slots):
    n = h.shape[0]
    tm = ROW_TILE
    return pl.pallas_call(
        _dispatch_kernel,
        out_shape=jax.ShapeDtypeStruct((n_slots, D_MODEL), F32),
        grid_spec=pltpu.PrefetchScalarGridSpec(
            num_scalar_prefetch=3,
            grid=(n // tm,),
            in_specs=[pl.BlockSpec((tm, D_MODEL), lambda i, s, z, nv: (i, 0))],
            out_specs=pl.BlockSpec(memory_space=pl.ANY),
            scratch_shapes=[pltpu.VMEM((MOE_TILE, D_MODEL), F32),
                            pltpu.SemaphoreType.DMA, pltpu.SemaphoreType.DMA],
        ),
        compiler_params=_params(("arbitrary",)),
        name="moe_dispatch",
    )(slot_flat, zero_off, n_valid, h)


def _expert_kernel(be_ref, nv_ref, xs_ref, wgu_ref, bgu_ref, wd_ref, bd_ref, o_ref):
    i = pl.program_id(0)

    @pl.when(i < nv_ref[0])
    def _():
        gu = jnp.dot(xs_ref[...].astype(BF16), wgu_ref[0], preferred_element_type=F32) + bgu_ref[0]
        gate = jnp.minimum(gu[:, :D_FF], SWIGLU_LIMIT)
        up = jnp.clip(gu[:, D_FF:], -SWIGLU_LIMIT, SWIGLU_LIMIT)
        act = (up + 1.0) * gate * _sigmoid(SWIGLU_ALPHA * gate)
        o_ref[...] = jnp.dot(act.astype(BF16), wd_ref[0], preferred_element_type=F32) + bd_ref[0]

    @pl.when(i >= nv_ref[0])
    def _():
        o_ref[...] = jnp.zeros(o_ref.shape, F32)


def _expert_ffn(block_expert, n_valid, xs, wgu, bgu, wd, bd):
    n_slots = xs.shape[0]
    tm = MOE_TILE
    return pl.pallas_call(
        _expert_kernel,
        out_shape=jax.ShapeDtypeStruct((n_slots, D_MODEL), F32),
        grid_spec=pltpu.PrefetchScalarGridSpec(
            num_scalar_prefetch=2,
            grid=(n_slots // tm,),
            in_specs=[
                pl.BlockSpec((tm, D_MODEL), lambda i, be, nv: (i, 0)),
                pl.BlockSpec((1, D_MODEL, 2 * D_FF), lambda i, be, nv: (be[i], 0, 0)),
                pl.BlockSpec((1, 1, 2 * D_FF), lambda i, be, nv: (be[i], 0, 0)),
                pl.BlockSpec((1, D_FF, D_MODEL), lambda i, be, nv: (be[i], 0, 0)),
                pl.BlockSpec((1, 1, D_MODEL), lambda i, be, nv: (be[i], 0, 0)),
            ],
            out_specs=pl.BlockSpec((tm, D_MODEL), lambda i, be, nv: (i, 0)),
        ),
        compiler_params=_params(("arbitrary",)),
        name="expert_ffn",
    )(block_expert, n_valid, xs, wgu, bgu, wd, bd)


def _combine_kernel(slot_ref, ys_hbm, x_ref, wt_ref, o_ref, buf, sems):
    i = pl.program_id(0)
    tc = COMBINE_TILE
    nsteps = pl.num_programs(0)
    n = slot_ref.shape[0] // TOP_K
    cur = lax.rem(i, 2)

    def issue(step, par):
        base = step * tc

        def body(r, c):
            for k in range(TOP_K):
                pltpu.make_async_copy(ys_hbm.at[pl.ds(slot_ref[k * n + base + r], 1)],
                                      buf.at[par, k, pl.ds(r, 1)], sems.at[par]).start()
            return c

        lax.fori_loop(0, tc, body, 0, unroll=4)

    @pl.when(i == 0)
    def _():
        issue(0, 0)

    @pl.when(i + 1 < nsteps)
    def _():
        issue(i + 1, 1 - cur)

    pltpu.make_async_copy(buf.at[cur], buf.at[cur], sems.at[cur]).wait()
    w = wt_ref[...]
    acc = x_ref[...]
    for k in range(TOP_K):
        acc = acc + _bcast_col(w, k, D_MODEL) * buf[cur, k]
    o_ref[...] = acc


def _combine(slot_flat, ys, x2, wt):
    n = x2.shape[0]
    tc = COMBINE_TILE
    return pl.pallas_call(
        _combine_kernel,
        out_shape=jax.ShapeDtypeStruct((n, D_MODEL), F32),
        grid_spec=pltpu.PrefetchScalarGridSpec(
            num_scalar_prefetch=1,
            grid=(n // tc,),
            in_specs=[pl.BlockSpec(memory_space=pl.ANY),
                      pl.BlockSpec((tc, D_MODEL), lambda i, s: (i, 0)),
                      pl.BlockSpec((tc, LANES), lambda i, s: (i, 0))],
            out_specs=pl.BlockSpec((tc, D_MODEL), lambda i, s: (i, 0)),
            scratch_shapes=[pltpu.VMEM((2, TOP_K, tc, D_MODEL), F32), pltpu.SemaphoreType.DMA((2,))],
        ),
        compiler_params=_params(("arbitrary",)),
        name="moe_combine",
    )(slot_flat, ys, x2, wt)


def _moe(h, x2, idx, wt, hist3, wgu, bgu, wd, bd):
    n = h.shape[0]
    tm = MOE_TILE
    n_slots = n * TOP_K + N_EXPERTS * tm
    n_blocks = n_slots // tm
    hist = hist3[:, 0, :N_EXPERTS]
    counts = jnp.sum(hist, axis=0).astype(jnp.int32)
    padded = (counts + tm - 1) // tm * tm
    pad_end = jnp.cumsum(padded)
    pad_start = pad_end - padded
    tile_base = pad_start[None, :].astype(F32) + (jnp.cumsum(hist, axis=0) - hist)
    base3 = jnp.broadcast_to(jnp.pad(tile_base, ((0, 0), (0, LANES - N_EXPERTS)))[:, None, :],
                             (hist.shape[0], 8, LANES))
    blk0 = jnp.arange(n_blocks) * tm
    block_expert = jnp.minimum(jnp.sum(blk0[:, None] >= pad_end[None, :], axis=1), N_EXPERTS - 1).astype(jnp.int32)
    n_valid = (pad_end[-1:] // tm).astype(jnp.int32)
    zero_off = jnp.maximum(pad_end - tm, 0).astype(jnp.int32)

    slot_flat = _slots(idx, base3)[:TOP_K].reshape(-1)
    xs = _dispatch(slot_flat, zero_off, n_valid, h, n_slots)
    ys = _expert_ffn(block_expert, n_valid, xs, wgu, bgu, wd, bd)
    return _combine(slot_flat, ys, x2, wt)


def _t5_bucket(dist):
    n = jnp.maximum(dist, 0)
    max_exact = REL_BUCKETS // 2
    nf = jnp.maximum(n, 1).astype(F32)
    large = max_exact + (jnp.log(nf / max_exact) / math.log(REL_MAX_DIST / max_exact)
                         * (REL_BUCKETS - max_exact)).astype(jnp.int32)
    large = jnp.minimum(large, REL_BUCKETS - 1)
    return jnp.where(n < max_exact, n, large)


def _bias_tables(rel_bias):
    rel_bias = rel_bias.astype(F32)
    tq = Q_TILE

    def lookup(dist):
        bucket = _t5_bucket(dist)
        out = jnp.zeros((NSA_HEADS,) + dist.shape, F32)
        for bk in range(REL_BUCKETS):
            out = jnp.where(bucket[None] == bk, rel_bias[bk].reshape((NSA_HEADS,) + (1,) * dist.ndim), out)
        return out

    t_pos = jnp.arange(SEQ)
    cmp_end = jnp.arange(N_CMP_PAD) * CMP_STRIDE + CMP_LEN - 1
    bias_cmp = lookup(t_pos[:, None] - cmp_end[None, :])
    rr = jnp.arange(tq)[:, None] - jnp.arange(tq)[None, :]
    btab = lookup(jnp.stack([rr, rr + tq, rr + 2 * tq]))
    return bias_cmp, btab


def _layer(x2, b, p, bias_cmp, btab):
    n = x2.shape[0]
    w = p['w_in']
    o_a, o_b, o_q, o_kv, o_g, o_u = 2048, 2052, 2056, 2312, 2696, 2708
    w_small = jnp.concatenate([w[:, o_a:o_q], w[:, o_g:o_u],
                               jnp.zeros((D_MODEL, LANES - 8 - 3 * NSA_HEADS), F32)], axis=1)
    w_cat = jnp.concatenate([w[:, :o_a], w[:, o_q:o_kv], w[:, o_kv:o_g], w[:, o_u:], w_small],
                            axis=1).astype(BF16)
    w_abt = jnp.concatenate([w[:, o_a:o_q].T, jnp.zeros((8, D_MODEL), F32)], axis=0).astype(BF16)
    qkv, z, nq, nkv, cu, small, abt = _in_proj(x2, p['attn_norm_w'][None, :], w_cat, w_abt)

    pcol = jnp.zeros((8, LANES), F32).at[0, :DN_HEADS].set(p['dn_a_log']).at[1, :DN_HEADS].set(p['dn_dt_bias'])
    prow = jnp.zeros((16, LANES), F32).at[:DN_HEADS, 0].set(p['dn_a_log']).at[:DN_HEADS, 1].set(p['dn_dt_bias'])
    small3 = small.reshape(b, SEQ, LANES)
    dq, dk, dv, gb, grow = _dn_prep(qkv.reshape(b, SEQ, 3 * DN_WIDTH), small3, abt,
                                    p['dn_conv_w'], pcol, prow)
    y_a = _delta_rule(dq, dk, dv, gb, grow, z.reshape(b, SEQ, DN_WIDTH), p['dn_norm_w'][None, :])

    y_c = _conformer(cu.reshape(b, SEQ, 2 * CONV_WIDTH), p['conv_dw_w'], p['conv_dw_b'][None, :],
                     p['conv_ln_w'][None, :], p['conv_ln_b'][None, :])

    hd = NSA_HEAD_DIM
    kv = nkv.reshape(b, SEQ, 6, hd)
    flat = CMP_STRIDE * hd
    kc = kv[:, :, 0].reshape(b, N_CMP_PAD, flat)
    vc = kv[:, :, 1].reshape(b, N_CMP_PAD, flat)
    pos = p['nsa_cmp_pos'].reshape(4, flat)
    kcmp, vcmp, ksn, kwn = _nsa_prep(kc, vc, kv[:, :, 2], kv[:, :, 4], pos,
                                     p['nsa_cmp_w1'].astype(BF16), p['nsa_cmp_w2'].astype(BF16),
                                     p['nsa_k_norm_w'])
    q4 = nq.reshape(b, SEQ, NSA_HEADS, hd).transpose(0, 2, 1, 3)
    o4 = _nsa_attn(q4, p['nsa_q_norm_w'][None, :], kcmp, vcmp, ksn, kv[:, :, 3], kwn, kv[:, :, 5],
                   small3, bias_cmp, btab)
    y_b = o4.transpose(0, 2, 1, 3).reshape(n, NSA_WIDTH)

    rw = jnp.concatenate([p['router_w'], jnp.zeros((D_MODEL, LANES - N_EXPERTS), F32)], axis=1)
    rb = jnp.concatenate([p['router_b'], jnp.full((LANES - N_EXPERTS,), NEG_INF, F32)])[None, :]
    x_new, h, idx, wt, hist3 = _out_router(y_a.reshape(n, DN_WIDTH), y_b, y_c.reshape(n, CONV_WIDTH), x2,
                                           p['w_out'].astype(BF16), p['ffn_norm_w'][None, :], rw, rb)
    return _moe(h, x_new, idx, wt, hist3,
                p['w_gate_up'].astype(BF16), p['b_gate_up'][:, None, :],
                p['w_down'].astype(BF16), p['b_down'][:, None, :])


def kernel(x, attn_norm_w, w_in, dn_conv_w, dn_a_log, dn_dt_bias, dn_norm_w, nsa_q_norm_w, nsa_k_norm_w, nsa_cmp_pos, nsa_cmp_w1, nsa_cmp_w2, conv_dw_w, conv_dw_b, conv_ln_w, conv_ln_b, w_out, ffn_norm_w, router_w, router_b, w_gate_up, b_gate_up, w_down, b_down, rel_bias):
    b, t, d = x.shape
    assert (t, d) == (SEQ, D_MODEL)
    stacked = dict(attn_norm_w=attn_norm_w, w_in=w_in, dn_conv_w=dn_conv_w, dn_a_log=dn_a_log,
                   dn_dt_bias=dn_dt_bias, dn_norm_w=dn_norm_w, nsa_q_norm_w=nsa_q_norm_w,
                   nsa_k_norm_w=nsa_k_norm_w, nsa_cmp_pos=nsa_cmp_pos, nsa_cmp_w1=nsa_cmp_w1,
                   nsa_cmp_w2=nsa_cmp_w2, conv_dw_w=conv_dw_w, conv_dw_b=conv_dw_b,
                   conv_ln_w=conv_ln_w, conv_ln_b=conv_ln_b, w_out=w_out, ffn_norm_w=ffn_norm_w,
                   router_w=router_w, router_b=router_b, w_gate_up=w_gate_up, b_gate_up=b_gate_up,
                   w_down=w_down, b_down=b_down)
    bias_cmp, btab = _bias_tables(rel_bias)
    x2 = x.reshape(b * t, d)
    for l in range(w_in.shape[0]):
        x2 = _layer(x2, b, {k: v[l] for k, v in stacked.items()}, bias_cmp, btab)
    return x2.reshape(b, t, d)
```

```python
import functools
import math

import numpy as np
import jax
import jax.numpy as jnp
from jax import lax
from jax.experimental import pallas as pl
from jax.experimental.pallas import tpu as pltpu

F32 = jnp.float32
BF16 = jnp.bfloat16

D_MODEL = 1024
SEQ = 2048
DN_HEADS = 4
DN_HEAD_DIM = 128
DN_WIDTH = DN_HEADS * DN_HEAD_DIM
DN_CONV = 4
DN_CHUNK = 64
NSA_HEADS = 4
NSA_HEAD_DIM = 64
NSA_WIDTH = NSA_HEADS * NSA_HEAD_DIM
CMP_LEN = 32
CMP_STRIDE = 16
CMP_HIDDEN = 2 * NSA_HEAD_DIM
SLC_BLOCK = 64
SLC_TOP_N = 16
WINDOW = 512
CONV_WIDTH = 256
CONV_KERNEL = 31
REL_BUCKETS = 32
REL_MAX_DIST = 128
N_EXPERTS = 32
TOP_K = 4
D_FF = D_MODEL
SWIGLU_LIMIT = 7.0
SWIGLU_ALPHA = 1.702
EPS = 1e-6
NEG_INF = -1e30
FORCE = 1e4

LANES = 128
VMEM_LIMIT_BYTES = 48 * 1024 * 1024
EXPERT_VMEM_LIMIT_BYTES = 56 * 1024 * 1024

ROW_TILE = 512
SEQ_TILE = 256
Q_TILE = 128
MOE_TILE = 512
COMBINE_TILE = 256

N_CMP_PAD = 128
N_SLC = SEQ // SLC_BLOCK


def _params(sem=None):
    return pltpu.CompilerParams(dimension_semantics=sem, vmem_limit_bytes=VMEM_LIMIT_BYTES)


def _mm(a, b):
    return jnp.dot(a.astype(BF16), b.astype(BF16), preferred_element_type=F32)


def _mm_nt(a, b):
    return lax.dot_general(a.astype(BF16), b.astype(BF16), (((1,), (1,)), ((), ())),
                           preferred_element_type=F32)


def _mm_tn(a, b):
    return lax.dot_general(a.astype(BF16), b.astype(BF16), (((0,), (0,)), ((), ())),
                           preferred_element_type=F32)


def _split3(x):
    hi = x.astype(BF16)
    r1 = x - hi.astype(F32)
    mid = r1.astype(BF16)
    lo = (r1 - mid.astype(F32)).astype(BF16)
    return hi, mid, lo


def _dot01_right(x, m01):
    hi, mid, lo = _split3(x)
    d = lambda p: jnp.dot(p, m01, preferred_element_type=F32)
    return d(hi) + d(mid) + d(lo)


def _dot01_left(m01, x):
    hi, mid, lo = _split3(x)
    d = lambda p: jnp.dot(m01, p, preferred_element_type=F32)
    return d(hi) + d(mid) + d(lo)


def _dot_f32(a, b):
    a_hi = a.astype(BF16)
    a_lo = (a - a_hi.astype(F32)).astype(BF16)
    b_hi = b.astype(BF16)
    b_lo = (b - b_hi.astype(F32)).astype(BF16)
    d = lambda p, q: jnp.dot(p, q, preferred_element_type=F32)
    return d(a_hi, b_hi) + d(a_hi, b_lo) + d(a_lo, b_hi) + d(a_lo, b_lo)


def _sigmoid(x):
    return 1.0 / (1.0 + jnp.exp(-x))


def _silu(x):
    return x * _sigmoid(x)


def _softplus(x):
    return jnp.maximum(x, 0.0) + jnp.log(1.0 + jnp.exp(-jnp.abs(x)))


def _bcast_col(x, j, width):
    return jnp.broadcast_to(x[:, j:j + 1], (x.shape[0], width))


IN_SEGS = (3 * DN_WIDTH, DN_WIDTH, NSA_WIDTH, 6 * NSA_HEAD_DIM, 2 * CONV_WIDTH, LANES)
IN_COLS = sum(IN_SEGS)


def _in_proj_kernel(x_ref, nw_ref, w_ref, wabt_ref,
                    qkv_ref, z_ref, nq_ref, nkv_ref, cu_ref, small_ref, abt_ref):
    xf = x_ref[...]
    ms = jnp.mean(xf * xf, axis=-1, keepdims=True)
    hb = (xf * lax.rsqrt(ms + EPS) * nw_ref[...]).astype(BF16)
    off = 0
    for ref, width in zip((qkv_ref, z_ref, nq_ref, nkv_ref, cu_ref, small_ref), IN_SEGS):
        ref[...] = jnp.dot(hb, w_ref[:, off:off + width], preferred_element_type=F32)
        off += width
    abt_ref[...] = lax.dot_general(wabt_ref[...], hb, (((1,), (1,)), ((), ())),
                                   preferred_element_type=F32)


def _in_proj(x2, norm_w, w_cat, w_abt):
    n = x2.shape[0]
    tm = ROW_TILE
    out_shape = [jax.ShapeDtypeStruct((n, w), F32) for w in IN_SEGS]
    out_shape.append(jax.ShapeDtypeStruct((16, n), F32))
    out_specs = [pl.BlockSpec((tm, w), lambda i: (i, 0)) for w in IN_SEGS]
    out_specs.append(pl.BlockSpec((16, tm), lambda i: (0, i)))
    return pl.pallas_call(
        _in_proj_kernel,
        out_shape=out_shape,
        grid=(n // tm,),
        in_specs=[
            pl.BlockSpec((tm, D_MODEL), lambda i: (i, 0)),
            pl.BlockSpec((1, D_MODEL), lambda i: (0, 0)),
            pl.BlockSpec((D_MODEL, IN_COLS), lambda i: (0, 0)),
            pl.BlockSpec((16, D_MODEL), lambda i: (0, 0)),
        ],
        out_specs=out_specs,
        compiler_params=_params(("parallel",)),
        name="in_proj",
    )(x2, norm_w, w_cat, w_abt)


def _dn_prep_kernel(qkv_ref, small_ref, abt_ref, cw_ref, pcol_ref, prow_ref,
                    q_ref, k_ref, v_ref, gb_ref, grow_ref, buf):
    t = pl.program_id(1)
    ts = SEQ_TILE

    @pl.when(t == 0)
    def _():
        buf[0:8, :] = jnp.zeros((8, 3 * DN_WIDTH), F32)

    buf[8:8 + ts, :] = qkv_ref[0]
    for c in range(3 * DN_HEADS):
        cs = slice(c * LANES, (c + 1) * LANES)
        acc = cw_ref[0:1, cs] * buf[5:5 + ts, cs]
        for j in range(1, DN_CONV):
            acc = acc + cw_ref[j:j + 1, cs] * buf[5 + j:5 + j + ts, cs]
        y = _silu(acc)
        if c < 2 * DN_HEADS:
            y = y * lax.rsqrt(jnp.sum(y * y, axis=-1, keepdims=True) + EPS)
        if c < DN_HEADS:
            q_ref[0, :, cs] = y * DN_HEAD_DIM ** -0.5
        elif c < 2 * DN_HEADS:
            k_ref[0, :, (c - DN_HEADS) * LANES:(c - DN_HEADS + 1) * LANES] = y
        else:
            v_ref[0, :, (c - 2 * DN_HEADS) * LANES:(c - 2 * DN_HEADS + 1) * LANES] = y
    buf[0:8, :] = buf[ts:ts + 8, :]

    sm = small_ref[0]
    lane = lax.broadcasted_iota(jnp.int32, sm.shape, 1)
    g_col = -jnp.exp(pcol_ref[0:1, :]) * _softplus(sm + pcol_ref[1:2, :])
    gb_ref[0] = jnp.where(lane < DN_HEADS, g_col, _sigmoid(sm))
    a_t = abt_ref[...]
    g_row = -jnp.exp(prow_ref[:, 0:1]) * _softplus(a_t + prow_ref[:, 1:2])
    grow_ref[...] = g_row[0:8, :]


def _dn_prep(qkv, small, abt, conv_w, pcol, prow):
    b = qkv.shape[0]
    ts = SEQ_TILE
    nt = SEQ // ts
    seq_spec = lambda w: pl.BlockSpec((1, ts, w), lambda i, t: (i, t, 0))
    full = lambda shape: pl.BlockSpec(shape, lambda i, t: (0,) * len(shape))
    return pl.pallas_call(
        _dn_prep_kernel,
        out_shape=[jax.ShapeDtypeStruct((b, SEQ, DN_WIDTH), F32)] * 3
        + [jax.ShapeDtypeStruct((b, SEQ, LANES), F32), jax.ShapeDtypeStruct((8, b * SEQ), F32)],
        grid=(b, nt),
        in_specs=[
            seq_spec(3 * DN_WIDTH), seq_spec(LANES),
            pl.BlockSpec((16, ts), lambda i, t: (0, i * nt + t)),
            full((DN_CONV, 3 * DN_WIDTH)), full((8, LANES)), full((16, LANES)),
        ],
        out_specs=[seq_spec(DN_WIDTH)] * 3
        + [seq_spec(LANES), pl.BlockSpec((8, ts), lambda i, t: (0, i * nt + t))],
        scratch_shapes=[pltpu.VMEM((ts + 8, 3 * DN_WIDTH), F32)],
        compiler_params=_params(("parallel", "arbitrary")),
        name="dn_prep",
    )(qkv, small, abt, conv_w, pcol, prow)


def _delta_kernel(q_ref, k_ref, v_ref, gb_ref, grow_ref, z_ref, nw_ref, o_ref, s_ref):
    t = pl.program_id(1)
    ts = SEQ_TILE
    ch = DN_CHUNK
    hd = DN_HEAD_DIM

    @pl.when(t == 0)
    def _():
        s_ref[...] = jnp.zeros(s_ref.shape, F32)

    r = lax.broadcasted_iota(jnp.int32, (ts, ts), 0)
    c = lax.broadcasted_iota(jnp.int32, (ts, ts), 1)
    same_chunk = (r // ch) == (c // ch)
    tril = same_chunk & (r >= c)
    strict = same_chunk & (r > c)
    same16 = (r // 16) == (c // 16)
    eye = jnp.where(r == c, 1.0, 0.0).astype(F32)
    m_col = jnp.where(tril, 1.0, 0.0).astype(BF16)
    m_row = jnp.where(same_chunk & (r <= c), 1.0, 0.0).astype(BF16)

    gb = gb_ref[0]
    gc_col = _dot01_left(m_col, gb)
    gc_row = _dot01_right(grow_ref[...], m_row)

    pre = []
    for h in range(DN_HEADS):
        hs = slice(h * hd, (h + 1) * hd)
        qh = q_ref[0, :, hs]
        kh = k_ref[0, :, hs]
        vh = v_ref[0, :, hs]
        gcb = _bcast_col(gc_col, h, ts)
        grb = jnp.broadcast_to(gc_row[h:h + 1, :], (ts, ts))
        decay = jnp.where(tril, jnp.exp(jnp.where(tril, gcb - grb, 0.0)), 0.0)
        beta = _bcast_col(gb, DN_HEADS + h, hd)
        gc128 = gcb[:, :hd]
        expg = jnp.exp(gc128)
        kb = kh * beta
        vb = vh * beta
        a_mat = jnp.where(strict, _mm_nt(kb, kh) * decay, 0.0)
        d_mat = jnp.where(same16, a_mat, 0.0)
        e_mat = a_mat - d_mat
        d2 = _mm(d_mat, d_mat)
        d4 = _mm(d2, d2)
        d8 = _mm(d4, d4)
        p_mat = _mm(_mm(_mm(eye - d_mat, eye + d2), eye + d4), eye + d8)
        m_mat = _mm(p_mat, e_mat)
        m2 = _mm(m_mat, m_mat)
        t_mat = _mm(_mm(eye - m_mat, eye + m2), p_mat)
        sol = _mm(t_mat, jnp.concatenate([vb, kb * expg], axis=1))
        attn = jnp.where(tril, _mm_nt(qh, kh) * decay, 0.0)
        pre.append((sol[:, :hd], sol[:, hd:], attn, qh * expg, kh, gc128))

    outs = [[] for _ in range(DN_HEADS)]
    for ci in range(ts // ch):
        rs = slice(ci * ch, (ci + 1) * ch)
        for h in range(DN_HEADS):
            u, w, attn, qg, kh, gc128 = pre[h]
            s_mat = s_ref[h]
            v_new = u[rs] - _mm(w[rs], s_mat)
            o_c = _mm(qg[rs], s_mat) + _mm(attn[rs, ci * ch:(ci + 1) * ch], v_new)
            g_last = gc128[ci * ch + ch - 1:ci * ch + ch, :]
            k_dec = kh[rs] * jnp.exp(g_last - gc128[rs])
            s_ref[h] = s_mat * jnp.exp(g_last) + _mm_tn(k_dec, v_new)
            outs[h].append(o_c)

    for h in range(DN_HEADS):
        hs = slice(h * hd, (h + 1) * hd)
        o = jnp.concatenate(outs[h], axis=0)
        o = o * lax.rsqrt(jnp.mean(o * o, axis=-1, keepdims=True) + EPS) * nw_ref[...]
        o_ref[0, :, hs] = o * _silu(z_ref[0, :, hs])


def _delta_rule(q, k, v, gb, grow, z, norm_w):
    b = q.shape[0]
    ts = SEQ_TILE
    nt = SEQ // ts
    seq_spec = lambda w: pl.BlockSpec((1, ts, w), lambda i, t: (i, t, 0))
    return pl.pallas_call(
        _delta_kernel,
        out_shape=jax.ShapeDtypeStruct((b, SEQ, DN_WIDTH), F32),
        grid=(b, nt),
        in_specs=[
            seq_spec(DN_WIDTH), seq_spec(DN_WIDTH), seq_spec(DN_WIDTH), seq_spec(LANES),
            pl.BlockSpec((8, ts), lambda i, t: (0, i * nt + t)),
            seq_spec(DN_WIDTH),
            pl.BlockSpec((1, DN_HEAD_DIM), lambda i, t: (0, 0)),
        ],
        out_specs=seq_spec(DN_WIDTH),
        scratch_shapes=[pltpu.VMEM((DN_HEADS, DN_HEAD_DIM, DN_HEAD_DIM), F32)],
        compiler_params=_params(("parallel", "arbitrary")),
        name="delta_rule",
    )(q, k, v, gb, grow, z, norm_w)


CONV_HALO = 32


def _conformer_kernel(u_ref, w_ref, b_ref, lnw_ref, lnb_ref, o_ref, buf):
    t = pl.program_id(1)
    ts = SEQ_TILE

    @pl.when(t == 0)
    def _():
        buf[0:CONV_HALO, :] = jnp.zeros((CONV_HALO, CONV_WIDTH), F32)

    u = u_ref[0]
    buf[CONV_HALO:CONV_HALO + ts, :] = u[:, :CONV_WIDTH] * _sigmoid(u[:, CONV_WIDTH:])
    base = CONV_HALO - (CONV_KERNEL - 1)
    rows = 64
    for rc in range(ts // rows):
        parts = []
        for cc in range(CONV_WIDTH // LANES):
            cs = slice(cc * LANES, (cc + 1) * LANES)
            r0 = base + rc * rows
            acc = w_ref[0:1, cs] * buf[r0:r0 + rows, cs]
            for j in range(1, CONV_KERNEL):
                acc = acc + w_ref[j:j + 1, cs] * buf[r0 + j:r0 + j + rows, cs]
            parts.append(acc)
        h = jnp.concatenate(parts, axis=1) + b_ref[...]
        mu = jnp.mean(h, axis=-1, keepdims=True)
        var = jnp.mean(jnp.square(h - mu), axis=-1, keepdims=True)
        hn = (h - mu) * lax.rsqrt(var + EPS) * lnw_ref[...] + lnb_ref[...]
        o_ref[0, rc * rows:(rc + 1) * rows, :] = _silu(hn)
    buf[0:CONV_HALO, :] = buf[ts:ts + CONV_HALO, :]


def _conformer(u, dw_w, dw_b, ln_w, ln_b):
    b = u.shape[0]
    ts = SEQ_TILE
    full = lambda shape: pl.BlockSpec(shape, lambda i, t: (0,) * len(shape))
    return pl.pallas_call(
        _conformer_kernel,
        out_shape=jax.ShapeDtypeStruct((b, SEQ, CONV_WIDTH), F32),
        grid=(b, SEQ // ts),
        in_specs=[
            pl.BlockSpec((1, ts, 2 * CONV_WIDTH), lambda i, t: (i, t, 0)),
            full((CONV_KERNEL, CONV_WIDTH)), full((1, CONV_WIDTH)),
            full((1, CONV_WIDTH)), full((1, CONV_WIDTH)),
        ],
        out_specs=pl.BlockSpec((1, ts, CONV_WIDTH), lambda i, t: (i, t, 0)),
        scratch_shapes=[pltpu.VMEM((ts + CONV_HALO, CONV_WIDTH), F32)],
        compiler_params=_params(("parallel", "arbitrary")),
        name="conformer",
    )(u, dw_w, dw_b, ln_w, ln_b)


def _rms_rows(x, w):
    return x * lax.rsqrt(jnp.mean(x * x, axis=-1, keepdims=True) + EPS) * w


def _nsa_prep_kernel(kc_ref, vc_ref, ks_ref, kw_ref, pos_ref, w1_ref, w2_ref, knw_ref,
                     kcmp_ref, vcmp_ref, ksn_ref, kwn_ref):
    half = CMP_STRIDE * NSA_HEAD_DIM

    def compress(x, i):
        u_lo = _mm(x + pos_ref[2 * i:2 * i + 1, :], w1_ref[i, 0:half, :])
        u_hi = _mm(x + pos_ref[2 * i + 1:2 * i + 2, :], w1_ref[i, half:2 * half, :])
        hid = _silu(u_lo + pltpu.roll(u_hi, N_CMP_PAD - 1, axis=0))
        return _mm(hid, w2_ref[i])

    kcmp_ref[0] = _rms_rows(compress(kc_ref[0], 0), knw_ref[0:1, :])
    vcmp_ref[0] = compress(vc_ref[0], 1)
    ksn_ref[0] = _rms_rows(ks_ref[0], knw_ref[1:2, :])
    kwn_ref[0] = _rms_rows(kw_ref[0], knw_ref[2:3, :])


def _nsa_prep(kc, vc, ks, kw, pos, w1, w2, knw):
    b = kc.shape[0]
    hd = NSA_HEAD_DIM
    flat = CMP_STRIDE * hd
    full = lambda shape: pl.BlockSpec(shape, lambda i: (0,) * len(shape))
    bspec = lambda r, w: pl.BlockSpec((1, r, w), lambda i: (i, 0, 0))
    return pl.pallas_call(
        _nsa_prep_kernel,
        out_shape=[jax.ShapeDtypeStruct((b, N_CMP_PAD, hd), F32)] * 2
        + [jax.ShapeDtypeStruct((b, SEQ, hd), F32)] * 2,
        grid=(b,),
        in_specs=[
            bspec(N_CMP_PAD, flat), bspec(N_CMP_PAD, flat), bspec(SEQ, hd), bspec(SEQ, hd),
            full((4, flat)), full((2, 2 * flat, CMP_HIDDEN)), full((2, CMP_HIDDEN, hd)), full((3, hd)),
        ],
        out_specs=[bspec(N_CMP_PAD, hd)] * 2 + [bspec(SEQ, hd)] * 2,
        compiler_params=_params(("parallel",)),
        name="nsa_prep",
    )(kc, vc, ks, kw, pos, w1, w2, knw)


def _nsa_attn_kernel(q_ref, qnw_ref, kcmp_ref, vcmp_ref, ks_ref, vs_ref, kw_ref, vw_ref,
                     gate_ref, bcmp_ref, btab_ref, o_ref,
                     madd_ref, s_ref, ksb_ref, kwb_ref, vsa_ref, vwa_ref):
    i = pl.program_id(1)
    tq = Q_TILE
    nh = NSA_HEADS
    hd = NSA_HEAD_DIM
    t0 = i * tq

    @pl.when(i == 0)
    def _():
        ones = jnp.ones((SEQ, hd), BF16)
        ksb_ref[...] = ks_ref[0].astype(BF16)
        kwb_ref[...] = kw_ref[0].astype(BF16)
        vsa_ref[...] = jnp.concatenate([vs_ref[0].astype(BF16), ones], axis=1)
        vwa_ref[...] = jnp.concatenate([vw_ref[0].astype(BF16), ones], axis=1)

    qs = jnp.concatenate(
        [_rms_rows(q_ref[0, h], qnw_ref[...]) * hd ** -0.5 for h in range(nh)], axis=0).astype(BF16)

    row = lax.broadcasted_iota(jnp.int32, (tq, LANES), 0)
    lane = lax.broadcasted_iota(jnp.int32, (tq, LANES), 1)
    qpos = t0 + row

    s_all = _mm_nt(qs, kcmp_ref[0])
    cmp_valid = (qpos >= lane * CMP_STRIDE + (CMP_LEN - 1)) & (lane < N_CMP_PAD - 1)
    p_rows = []
    p_sum = jnp.zeros((tq, LANES), F32)
    for h in range(nh):
        s = jnp.where(cmp_valid, s_all[h * tq:(h + 1) * tq] + bcmp_ref[h], NEG_INF)
        m = jnp.max(s, axis=-1, keepdims=True)
        p = jnp.where(cmp_valid, jnp.exp(s - m), 0.0)
        l = jnp.sum(p, axis=-1, keepdims=True)
        p = p * jnp.where(l > 0.0, 1.0 / l, 0.0)
        p_rows.append(p)
        p_sum = p_sum + p
    o_cmp = _mm(jnp.concatenate(p_rows, axis=0), vcmp_ref[0])

    jj = lax.broadcasted_iota(jnp.int32, (N_CMP_PAD, LANES), 0)
    ss = lax.broadcasted_iota(jnp.int32, (N_CMP_PAD, LANES), 1)
    overlap = ((jj * CMP_STRIDE < ss * SLC_BLOCK + SLC_BLOCK)
               & (jj * CMP_STRIDE + CMP_LEN > ss * SLC_BLOCK)
               & (jj < N_CMP_PAD - 1) & (ss < N_SLC))
    imp = _dot01_right(p_sum, jnp.where(overlap, 1.0, 0.0).astype(BF16))
    cur = qpos // SLC_BLOCK
    causal_blk = lane <= cur
    forced = (lane == 0) | (lane == cur) | (lane == cur - 1)
    imp = jnp.where(causal_blk & forced, FORCE, jnp.where(causal_blk, imp, -1.0))
    imp = jnp.where(lane < N_SLC, imp, -2.0)
    rank = jnp.zeros((tq, LANES), jnp.int32)
    for s2 in range(N_SLC):
        col = _bcast_col(imp, s2, LANES)
        beats = (col > imp) | ((col == imp) & (lane > s2))
        rank = rank + jnp.where(beats, 1, 0)
    sel = jnp.where((rank < SLC_TOP_N) & (lane < N_SLC), 1.0, 0.0).astype(BF16)
    er = lax.broadcasted_iota(jnp.int32, (LANES, SEQ), 0)
    ec = lax.broadcasted_iota(jnp.int32, (LANES, SEQ), 1)
    expand = jnp.where(ec // SLC_BLOCK == er, 1.0, 0.0).astype(BF16)
    key = lax.broadcasted_iota(jnp.int32, (tq, SEQ), 1)
    qall = t0 + lax.broadcasted_iota(jnp.int32, (tq, SEQ), 0)
    chosen = (jnp.dot(sel, expand, preferred_element_type=F32) > 0.5) & (qall >= key)
    madd_ref[...] = jnp.where(chosen, 0.0, NEG_INF)

    neg = jnp.full((nh * tq, tq), NEG_INF, F32)
    zero = jnp.zeros((nh * tq, LANES), F32)

    def scores(k_tile, bias_idx, add):
        s_t = _mm_nt(qs, k_tile)
        return jnp.concatenate(
            [s_t[h * tq:(h + 1) * tq] + (btab_ref[h, bias_idx] + add) for h in range(nh)], axis=0)

    def normalise(acc):
        return acc[:, :hd] * (1.0 / _bcast_col(acc, hd, hd))

    def slc_pass1(kt, m_run):
        k0 = pl.multiple_of(kt * tq, tq)
        s_m = scores(ksb_ref[pl.ds(k0, tq), :], jnp.minimum(i - kt, 2), madd_ref[:, pl.ds(k0, tq)])
        s_ref[:, pl.ds(k0, tq)] = s_m
        return jnp.maximum(m_run, s_m)

    m_slc = jnp.broadcast_to(
        jnp.max(lax.fori_loop(0, i + 1, slc_pass1, neg), axis=-1, keepdims=True), (nh * tq, tq))

    def slc_pass2(kt, acc):
        k0 = pl.multiple_of(kt * tq, tq)
        p = jnp.exp(s_ref[:, pl.ds(k0, tq)] - m_slc).astype(BF16)
        return acc + jnp.dot(p, vsa_ref[pl.ds(k0, tq), :], preferred_element_type=F32)

    o_slc = normalise(lax.fori_loop(0, i + 1, slc_pass2, zero))

    kcol = lax.broadcasted_iota(jnp.int32, (tq, tq), 1)
    qrow = t0 + lax.broadcasted_iota(jnp.int32, (tq, tq), 0)
    n_win = WINDOW // tq + 1
    starts = []
    m_run = neg
    for j in range(n_win):
        d = n_win - 1 - j
        k0 = pl.multiple_of(jnp.maximum(i - d, 0) * tq, tq)
        starts.append(k0)
        dist = qrow - (k0 + kcol)
        ok = (dist >= 0) & (dist < WINDOW) & (i >= d)
        s_m = scores(kwb_ref[pl.ds(k0, tq), :], min(d, 2), jnp.where(ok, 0.0, NEG_INF))
        s_ref[:, j * tq:(j + 1) * tq] = s_m
        m_run = jnp.maximum(m_run, s_m)
    m_win = jnp.broadcast_to(jnp.max(m_run, axis=-1, keepdims=True), (nh * tq, tq))
    acc = zero
    for j in range(n_win):
        p = jnp.exp(s_ref[:, j * tq:(j + 1) * tq] - m_win).astype(BF16)
        acc = acc + jnp.dot(p, vwa_ref[pl.ds(starts[j], tq), :], preferred_element_type=F32)
    o_win = normalise(acc)

    gates = _sigmoid(gate_ref[0])
    for h in range(nh):
        hs = slice(h * tq, (h + 1) * tq)
        g0 = 2 * DN_HEADS + 3 * h
        o_ref[0, h] = (_bcast_col(gates, g0, hd) * o_cmp[hs]
                       + _bcast_col(gates, g0 + 1, hd) * o_slc[hs]
                       + _bcast_col(gates, g0 + 2, hd) * o_win[hs])


def _nsa_attn(q, qnw, kcmp, vcmp, ks, vs, kw, vw, small, bias_cmp, btab):
    b = q.shape[0]
    tq = Q_TILE
    nh, hd = NSA_HEADS, NSA_HEAD_DIM
    full = lambda shape: pl.BlockSpec(shape, lambda bi, i: (0,) * len(shape))
    per_b = lambda r: pl.BlockSpec((1, r, hd), lambda bi, i: (bi, 0, 0))
    return pl.pallas_call(
        _nsa_attn_kernel,
        out_shape=jax.ShapeDtypeStruct((b, nh, SEQ, hd), F32),
        grid=(b, SEQ // tq),
        in_specs=[
            pl.BlockSpec((1, nh, tq, hd), lambda bi, i: (bi, 0, i, 0)),
            full((1, hd)),
            per_b(N_CMP_PAD), per_b(N_CMP_PAD), per_b(SEQ), per_b(SEQ), per_b(SEQ), per_b(SEQ),
            pl.BlockSpec((1, tq, LANES), lambda bi, i: (bi, i, 0)),
            pl.BlockSpec((nh, tq, N_CMP_PAD), lambda bi, i: (0, i, 0)),
            full((nh, 3, tq, tq)),
        ],
        out_specs=pl.BlockSpec((1, nh, tq, hd), lambda bi, i: (bi, 0, i, 0)),
        scratch_shapes=[pltpu.VMEM((tq, SEQ), F32), pltpu.VMEM((nh * tq, SEQ), F32),
                        pltpu.VMEM((SEQ, hd), BF16), pltpu.VMEM((SEQ, hd), BF16),
                        pltpu.VMEM((SEQ, 2 * hd), BF16), pltpu.VMEM((SEQ, 2 * hd), BF16)],
        compiler_params=_params(("parallel", "arbitrary")),
        name="nsa_attn",
    )(q, qnw, kcmp, vcmp, ks, vs, kw, vw, small, bias_cmp, btab)


def _out_router_kernel(ya_ref, yb_ref, yc_ref, x_ref, wo_ref, fnw_ref, rw_ref, rb_ref,
                       xo_ref, h_ref, idx_ref, wt_ref, hist_ref):
    wa = DN_WIDTH
    wb = wa + NSA_WIDTH
    y = (jnp.dot(ya_ref[...].astype(BF16), wo_ref[0:wa, :], preferred_element_type=F32)
         + jnp.dot(yb_ref[...].astype(BF16), wo_ref[wa:wb, :], preferred_element_type=F32)
         + jnp.dot(yc_ref[...].astype(BF16), wo_ref[wb:, :], preferred_element_type=F32))
    xn = x_ref[...] + y
    xo_ref[...] = xn
    h = xn * lax.rsqrt(jnp.mean(xn * xn, axis=-1, keepdims=True) + EPS) * fnw_ref[...]
    h_ref[...] = h
    logits = _dot_f32(h, rw_ref[...]) + rb_ref[...]
    lane = lax.broadcasted_iota(jnp.int32, logits.shape, 1)
    vals, idxs = [], []
    for _ in range(TOP_K):
        m = jnp.max(logits, axis=-1, keepdims=True)
        ix = jnp.min(jnp.where(logits == m, lane, LANES), axis=-1, keepdims=True)
        vals.append(m)
        idxs.append(ix)
        logits = jnp.where(lane == ix, -jnp.inf, logits)
    es = [jnp.exp(v - vals[0]) for v in vals]
    inv = 1.0 / (es[0] + es[1] + es[2] + es[3])
    idx_out = jnp.zeros(lane.shape, jnp.int32)
    wt_out = jnp.zeros(lane.shape, F32)
    chosen = jnp.zeros(lane.shape, F32)
    for k in range(TOP_K):
        idx_out = jnp.where(lane == k, idxs[k], idx_out)
        wt_out = jnp.where(lane == k, es[k] * inv, wt_out)
        chosen = chosen + jnp.where(lane == idxs[k], 1.0, 0.0)
    idx_ref[...] = idx_out
    wt_ref[...] = wt_out
    hist_ref[0] = jnp.broadcast_to(jnp.sum(chosen, axis=0, keepdims=True), (8, LANES))


def _out_router(ya, yb, yc, x2, w_out, fnw, rw, rb):
    n = x2.shape[0]
    tm = ROW_TILE
    row = lambda w: pl.BlockSpec((tm, w), lambda i: (i, 0))
    full = lambda shape: pl.BlockSpec(shape, lambda i: (0,) * len(shape))
    return pl.pallas_call(
        _out_router_kernel,
        out_shape=[jax.ShapeDtypeStruct((n, D_MODEL), F32), jax.ShapeDtypeStruct((n, D_MODEL), F32),
                   jax.ShapeDtypeStruct((n, LANES), jnp.int32), jax.ShapeDtypeStruct((n, LANES), F32),
                   jax.ShapeDtypeStruct((n // tm, 8, LANES), F32)],
        grid=(n // tm,),
        in_specs=[row(DN_WIDTH), row(NSA_WIDTH), row(CONV_WIDTH), row(D_MODEL),
                  full((D_MODEL, D_MODEL)), full((1, D_MODEL)), full((D_MODEL, LANES)), full((1, LANES))],
        out_specs=[row(D_MODEL), row(D_MODEL), row(LANES), row(LANES),
                   pl.BlockSpec((1, 8, LANES), lambda i: (i, 0, 0))],
        compiler_params=_params(("parallel",)),
        name="out_router",
    )(ya, yb, yc, x2, w_out, fnw, rw, rb)


def _slots_kernel(idx_ref, base_ref, slot_ref):
    tm = ROW_TILE
    idx = idx_ref[...]
    lane = lax.broadcasted_iota(jnp.int32, (tm, LANES), 1)
    onehots = [jnp.where(lane == _bcast_col(idx, k, LANES), 1.0, 0.0) for k in range(TOP_K)]
    cnt = (onehots[0] + onehots[1]) + (onehots[2] + onehots[3])
    r = lax.broadcasted_iota(jnp.int32, (tm, tm), 0)
    c = lax.broadcasted_iota(jnp.int32, (tm, tm), 1)
    earlier = jnp.where(r > c, 1.0, 0.0).astype(BF16)
    rank = jnp.dot(earlier, cnt.astype(BF16), preferred_element_type=F32) + base_ref[0, 0:1, :]
    ones = jnp.ones((8, LANES), BF16)
    row8 = lax.broadcasted_iota(jnp.int32, (8, tm), 0)
    out = jnp.zeros((8, tm), F32)
    for k in range(TOP_K):
        hi, mid, lo = _split3(rank * onehots[k])
        d = lambda p: lax.dot_general(ones, p, (((1,), (1,)), ((), ())), preferred_element_type=F32)
        out = jnp.where(row8 == k, d(hi) + d(mid) + d(lo), out)
    slot_ref[...] = out.astype(jnp.int32)


def _slots(idx, base3):
    n = idx.shape[0]
    tm = ROW_TILE
    return pl.pallas_call(
        _slots_kernel,
        out_shape=jax.ShapeDtypeStruct((8, n), jnp.int32),
        grid=(n // tm,),
        in_specs=[pl.BlockSpec((tm, LANES), lambda i: (i, 0)),
                  pl.BlockSpec((1, 8, LANES), lambda i: (i, 0, 0))],
        out_specs=pl.BlockSpec((8, tm), lambda i: (0, i)),
        compiler_params=_params(("parallel",)),
        name="moe_slots",
    )(idx, base3)


def _dispatch_kernel(slot_ref, zoff_ref, nv_ref, h_ref, xs_hbm, zbuf, sem, zsem):
    i = pl.program_id(0)
    tm = ROW_TILE
    n = slot_ref.shape[0] // TOP_K
    base = i * tm

    @pl.when(i == 0)
    def _():
        zbuf[...] = jnp.zeros(zbuf.shape, F32)
        n_blocks = xs_hbm.shape[0] // MOE_TILE

        def fill(row0):
            return pltpu.make_async_copy(zbuf, xs_hbm.at[pl.ds(pl.multiple_of(row0, MOE_TILE), MOE_TILE)], zsem)

        def fill_region_end(e, c):
            fill(zoff_ref[e]).start()
            return c

        def fill_tail(b, c):
            fill(b * MOE_TILE).start()
            return c

        def wait_fill(b, c):
            fill(0).wait()
            return c

        lax.fori_loop(0, N_EXPERTS, fill_region_end, 0)
        lax.fori_loop(nv_ref[0], n_blocks, fill_tail, 0)
        lax.fori_loop(0, N_EXPERTS + n_blocks - nv_ref[0], wait_fill, 0)

    def body(r, c):
        for k in range(TOP_K):
            pltpu.make_async_copy(h_ref.at[pl.ds(r, 1)],
                                  xs_hbm.at[pl.ds(slot_ref[k * n + base + r], 1)], sem).start()
        return c

    lax.fori_loop(0, tm, body, 0, unroll=4)
    done = xs_hbm.at[pl.ds(0, TOP_K * tm)]
    pltpu.make_async_copy(done, done, sem).wait()


def _dispatch(slot_flat, zero_off, n_valid, h, n_slots):
    n = h.shape[0]
    tm = ROW_TILE
    return pl.pallas_call(
        _dispatch_kernel,
        out_shape=jax.ShapeDtypeStruct((n_slots, D_MODEL), F32),
        grid_spec=pltpu.PrefetchScalarGridSpec(
            num_scalar_prefetch=3,
            grid=(n // tm,),
            in_specs=[pl.BlockSpec((tm, D_MODEL), lambda i, s, z, nv: (i, 0))],
            out_specs=pl.BlockSpec(memory_space=pl.ANY),
            scratch_shapes=[pltpu.VMEM((MOE_TILE, D_MODEL), F32),
                            pltpu.SemaphoreType.DMA, pltpu.SemaphoreType.DMA],
        ),
        compiler_params=_params(("arbitrary",)),
        name="moe_dispatch",
    )(slot_flat, zero_off, n_valid, h)


def _expert_kernel(be_ref, nv_ref, xs_ref, wgu_ref, bgu_ref, wd_ref, bd_ref, o_ref, wgu_bf, wd_bf):
    i = pl.program_id(0)
    new_expert = (i == 0) | (be_ref[i] != be_ref[jnp.maximum(i - 1, 0)])

    @pl.when((i < nv_ref[0]) & new_expert)
    def _():
        step = 512
        for c0 in range(0, 2 * D_FF, step):
            wgu_bf[:, c0:c0 + step] = wgu_ref[:, c0:c0 + step].astype(BF16)
        for c0 in range(0, D_MODEL, step):
            wd_bf[:, c0:c0 + step] = wd_ref[:, c0:c0 + step].astype(BF16)

    @pl.when(i < nv_ref[0])
    def _():
        gu = jnp.dot(xs_ref[...].astype(BF16), wgu_bf[...], preferred_element_type=F32) + bgu_ref[...]
        gate = jnp.minimum(gu[:, :D_FF], SWIGLU_LIMIT)
        up = jnp.clip(gu[:, D_FF:], -SWIGLU_LIMIT, SWIGLU_LIMIT)
        act = (up + 1.0) * gate * _sigmoid(SWIGLU_ALPHA * gate)
        o_ref[...] = jnp.dot(act.astype(BF16), wd_bf[...], preferred_element_type=F32) + bd_ref[...]

    @pl.when(i >= nv_ref[0])
    def _():
        o_ref[...] = jnp.zeros(o_ref.shape, F32)


def _expert_ffn(layer, block_expert, n_valid, xs, wgu, bgu, wd, bd):
    n_slots = xs.shape[0]
    tm = MOE_TILE
    return pl.pallas_call(
        _expert_kernel,
        out_shape=jax.ShapeDtypeStruct((n_slots, D_MODEL), F32),
        grid_spec=pltpu.PrefetchScalarGridSpec(
            num_scalar_prefetch=2,
            grid=(n_slots // tm,),
            in_specs=[
                pl.BlockSpec((tm, D_MODEL), lambda i, be, nv: (i, 0)),
                pl.BlockSpec((None, None, D_MODEL, 2 * D_FF), lambda i, be, nv: (layer, be[i], 0, 0)),
                pl.BlockSpec((None, None, 1, 2 * D_FF), lambda i, be, nv: (layer, be[i], 0, 0)),
                pl.BlockSpec((None, None, D_FF, D_MODEL), lambda i, be, nv: (layer, be[i], 0, 0)),
                pl.BlockSpec((None, None, 1, D_MODEL), lambda i, be, nv: (layer, be[i], 0, 0)),
            ],
            out_specs=pl.BlockSpec((tm, D_MODEL), lambda i, be, nv: (i, 0)),
            scratch_shapes=[pltpu.VMEM((D_MODEL, 2 * D_FF), BF16), pltpu.VMEM((D_FF, D_MODEL), BF16)],
        ),
        compiler_params=pltpu.CompilerParams(dimension_semantics=("arbitrary",),
                                             vmem_limit_bytes=EXPERT_VMEM_LIMIT_BYTES),
        name="expert_ffn",
    )(block_expert, n_valid, xs, wgu, bgu, wd, bd)


def _combine_kernel(slot_ref, ys_hbm, x_ref, wt_ref, o_ref, buf, sems):
    i = pl.program_id(0)
    tc = COMBINE_TILE
    nsteps = pl.num_programs(0)
    n = slot_ref.shape[0] // TOP_K
    cur = lax.rem(i, 2)

    def issue(step, par):
        base = step * tc

        def body(r, c):
            for k in range(TOP_K):
                pltpu.make_async_copy(ys_hbm.at[pl.ds(slot_ref[k * n + base + r], 1)],
                                      buf.at[par, k, pl.ds(r, 1)], sems.at[par]).start()
            return c

        lax.fori_loop(0, tc, body, 0, unroll=4)

    @pl.when(i == 0)
    def _():
        issue(0, 0)

    @pl.when(i + 1 < nsteps)
    def _():
        issue(i + 1, 1 - cur)

    pltpu.make_async_copy(buf.at[cur], buf.at[cur], sems.at[cur]).wait()
    w = wt_ref[...]
    acc = x_ref[...]
    for k in range(TOP_K):
        acc = acc + _bcast_col(w, k, D_MODEL) * buf[cur, k]
    o_ref[...] = acc


def _combine(slot_flat, ys, x2, wt):
    n = x2.shape[0]
    tc = COMBINE_TILE
    return pl.pallas_call(
        _combine_kernel,
        out_shape=jax.ShapeDtypeStruct((n, D_MODEL), F32),
        grid_spec=pltpu.PrefetchScalarGridSpec(
            num_scalar_prefetch=1,
            grid=(n // tc,),
            in_specs=[pl.BlockSpec(memory_space=pl.ANY),
                      pl.BlockSpec((tc, D_MODEL), lambda i, s: (i, 0)),
                      pl.BlockSpec((tc, LANES), lambda i, s: (i, 0))],
            out_specs=pl.BlockSpec((tc, D_MODEL), lambda i, s: (i, 0)),
            scratch_shapes=[pltpu.VMEM((2, TOP_K, tc, D_MODEL), F32), pltpu.SemaphoreType.DMA((2,))],
        ),
        compiler_params=_params(("arbitrary",)),
        name="moe_combine",
    )(slot_flat, ys, x2, wt)


def _moe(layer, h, x2, idx, wt, hist3, wgu, bgu, wd, bd):
    n = h.shape[0]
    tm = MOE_TILE
    n_slots = n * TOP_K + N_EXPERTS * tm
    n_blocks = n_slots // tm
    hist = hist3[:, 0, :N_EXPERTS]
    counts = jnp.sum(hist, axis=0).astype(jnp.int32)
    padded = (counts + tm - 1) // tm * tm
    pad_end = jnp.cumsum(padded)
    pad_start = pad_end - padded
    tile_base = pad_start[None, :].astype(F32) + (jnp.cumsum(hist, axis=0) - hist)
    base3 = jnp.broadcast_to(jnp.pad(tile_base, ((0, 0), (0, LANES - N_EXPERTS)))[:, None, :],
                             (hist.shape[0], 8, LANES))
    blk0 = jnp.arange(n_blocks) * tm
    block_expert = jnp.minimum(jnp.sum(blk0[:, None] >= pad_end[None, :], axis=1), N_EXPERTS - 1).astype(jnp.int32)
    n_valid = (pad_end[-1:] // tm).astype(jnp.int32)
    zero_off = jnp.maximum(pad_end - tm, 0).astype(jnp.int32)

    slot_flat = _slots(idx, base3)[:TOP_K].reshape(-1)
    xs = _dispatch(slot_flat, zero_off, n_valid, h, n_slots)
    ys = _expert_ffn(layer, block_expert, n_valid, xs, wgu, bgu, wd, bd)
    return _combine(slot_flat, ys, x2, wt)


def _t5_bucket(dist):
    n = jnp.maximum(dist, 0)
    max_exact = REL_BUCKETS // 2
    nf = jnp.maximum(n, 1).astype(F32)
    large = max_exact + (jnp.log(nf / max_exact) / math.log(REL_MAX_DIST / max_exact)
                         * (REL_BUCKETS - max_exact)).astype(jnp.int32)
    large = jnp.minimum(large, REL_BUCKETS - 1)
    return jnp.where(n < max_exact, n, large)


def _bias_tables(rel_bias):
    rel_bias = rel_bias.astype(F32)
    tq = Q_TILE

    def lookup(dist):
        bucket = _t5_bucket(dist)
        out = jnp.zeros((NSA_HEADS,) + dist.shape, F32)
        for bk in range(REL_BUCKETS):
            out = jnp.where(bucket[None] == bk, rel_bias[bk].reshape((NSA_HEADS,) + (1,) * dist.ndim), out)
        return out

    t_pos = jnp.arange(SEQ)
    cmp_end = jnp.arange(N_CMP_PAD) * CMP_STRIDE + CMP_LEN - 1
    bias_cmp = lookup(t_pos[:, None] - cmp_end[None, :])
    rr = jnp.arange(tq)[:, None] - jnp.arange(tq)[None, :]
    btab = lookup(jnp.stack([rr, rr + tq, rr + 2 * tq]))
    return bias_cmp, btab


def _layer(layer, x2, b, p, experts, bias_cmp, btab):
    n = x2.shape[0]
    w = p['w_in']
    o_a, o_b, o_q, o_kv, o_g, o_u = 2048, 2052, 2056, 2312, 2696, 2708
    w_small = jnp.concatenate([w[:, o_a:o_q], w[:, o_g:o_u],
                               jnp.zeros((D_MODEL, LANES - 8 - 3 * NSA_HEADS), F32)], axis=1)
    w_cat = jnp.concatenate([w[:, :o_a], w[:, o_q:o_kv], w[:, o_kv:o_g], w[:, o_u:], w_small],
                            axis=1).astype(BF16)
    w_abt = jnp.concatenate([w[:, o_a:o_q].T, jnp.zeros((8, D_MODEL), F32)], axis=0).astype(BF16)
    qkv, z, nq, nkv, cu, small, abt = _in_proj(x2, p['attn_norm_w'][None, :], w_cat, w_abt)

    pcol = jnp.zeros((8, LANES), F32).at[0, :DN_HEADS].set(p['dn_a_log']).at[1, :DN_HEADS].set(p['dn_dt_bias'])
    prow = jnp.zeros((16, LANES), F32).at[:DN_HEADS, 0].set(p['dn_a_log']).at[:DN_HEADS, 1].set(p['dn_dt_bias'])
    small3 = small.reshape(b, SEQ, LANES)
    dq, dk, dv, gb, grow = _dn_prep(qkv.reshape(b, SEQ, 3 * DN_WIDTH), small3, abt,
                                    p['dn_conv_w'], pcol, prow)
    y_a = _delta_rule(dq, dk, dv, gb, grow, z.reshape(b, SEQ, DN_WIDTH), p['dn_norm_w'][None, :])

    y_c = _conformer(cu.reshape(b, SEQ, 2 * CONV_WIDTH), p['conv_dw_w'], p['conv_dw_b'][None, :],
                     p['conv_ln_w'][None, :], p['conv_ln_b'][None, :])

    hd = NSA_HEAD_DIM
    kv = nkv.reshape(b, SEQ, 6, hd)
    flat = CMP_STRIDE * hd
    kc = kv[:, :, 0].reshape(b, N_CMP_PAD, flat)
    vc = kv[:, :, 1].reshape(b, N_CMP_PAD, flat)
    pos = p['nsa_cmp_pos'].reshape(4, flat)
    kcmp, vcmp, ksn, kwn = _nsa_prep(kc, vc, kv[:, :, 2], kv[:, :, 4], pos,
                                     p['nsa_cmp_w1'].astype(BF16), p['nsa_cmp_w2'].astype(BF16),
                                     p['nsa_k_norm_w'])
    q4 = nq.reshape(b, SEQ, NSA_HEADS, hd).transpose(0, 2, 1, 3)
    o4 = _nsa_attn(q4, p['nsa_q_norm_w'][None, :], kcmp, vcmp, ksn, kv[:, :, 3], kwn, kv[:, :, 5],
                   small3, bias_cmp, btab)
    y_b = o4.transpose(0, 2, 1, 3).reshape(n, NSA_WIDTH)

    rw = jnp.concatenate([p['router_w'], jnp.zeros((D_MODEL, LANES - N_EXPERTS), F32)], axis=1)
    rb = jnp.concatenate([p['router_b'], jnp.full((LANES - N_EXPERTS,), NEG_INF, F32)])[None, :]
    x_new, h, idx, wt, hist3 = _out_router(y_a.reshape(n, DN_WIDTH), y_b, y_c.reshape(n, CONV_WIDTH), x2,
                                           p['w_out'].astype(BF16), p['ffn_norm_w'][None, :], rw, rb)
    return _moe(layer, h, x_new, idx, wt, hist3, *experts)


def kernel(x, attn_norm_w, w_in, dn_conv_w, dn_a_log, dn_dt_bias, dn_norm_w, nsa_q_norm_w, nsa_k_norm_w, nsa_cmp_pos, nsa_cmp_w1, nsa_cmp_w2, conv_dw_w, conv_dw_b, conv_ln_w, conv_ln_b, w_out, ffn_norm_w, router_w, router_b, w_gate_up, b_gate_up, w_down, b_down, rel_bias):
    b, t, d = x.shape
    assert (t, d) == (SEQ, D_MODEL)
    stacked = dict(attn_norm_w=attn_norm_w, w_in=w_in, dn_conv_w=dn_conv_w, dn_a_log=dn_a_log,
                   dn_dt_bias=dn_dt_bias, dn_norm_w=dn_norm_w, nsa_q_norm_w=nsa_q_norm_w,
                   nsa_k_norm_w=nsa_k_norm_w, nsa_cmp_pos=nsa_cmp_pos, nsa_cmp_w1=nsa_cmp_w1,
                   nsa_cmp_w2=nsa_cmp_w2, conv_dw_w=conv_dw_w, conv_dw_b=conv_dw_b,
                   conv_ln_w=conv_ln_w, conv_ln_b=conv_ln_b, w_out=w_out, ffn_norm_w=ffn_norm_w,
                   router_w=router_w, router_b=router_b)
    experts = (w_gate_up, b_gate_up[:, :, None, :], w_down, b_down[:, :, None, :])
    bias_cmp, btab = _bias_tables(rel_bias)
    x2 = x.reshape(b * t, d)
    for l in range(w_in.shape[0]):
        x2 = _layer(l, x2, b, {k: v[l] for k, v in stacked.items()}, experts, bias_cmp, btab)
    return x2.reshape(b, t, d)
```

```python
import functools
import math

import numpy as np
import jax
import jax.numpy as jnp
from jax import lax
from jax.experimental import pallas as pl
from jax.experimental.pallas import tpu as pltpu

F32 = jnp.float32
BF16 = jnp.bfloat16

D_MODEL = 1024
SEQ = 2048
DN_HEADS = 4
DN_HEAD_DIM = 128
DN_WIDTH = DN_HEADS * DN_HEAD_DIM
DN_CONV = 4
DN_CHUNK = 64
NSA_HEADS = 4
NSA_HEAD_DIM = 64
NSA_WIDTH = NSA_HEADS * NSA_HEAD_DIM
CMP_LEN = 32
CMP_STRIDE = 16
CMP_HIDDEN = 2 * NSA_HEAD_DIM
SLC_BLOCK = 64
SLC_TOP_N = 16
WINDOW = 512
CONV_WIDTH = 256
CONV_KERNEL = 31
REL_BUCKETS = 32
REL_MAX_DIST = 128
N_EXPERTS = 32
TOP_K = 4
D_FF = D_MODEL
SWIGLU_LIMIT = 7.0
SWIGLU_ALPHA = 1.702
EPS = 1e-6
NEG_INF = -1e30
FORCE = 1e4

LANES = 128
VMEM_LIMIT_BYTES = 48 * 1024 * 1024
EXPERT_VMEM_LIMIT_BYTES = 56 * 1024 * 1024

ROW_TILE = 512
SEQ_TILE = 256
Q_TILE = 128
MOE_TILE = 512
COMBINE_TILE = 256

N_CMP_PAD = 128
N_SLC = SEQ // SLC_BLOCK


def _params(sem=None):
    return pltpu.CompilerParams(dimension_semantics=sem, vmem_limit_bytes=VMEM_LIMIT_BYTES)


def _mm(a, b):
    return jnp.dot(a.astype(BF16), b.astype(BF16), preferred_element_type=F32)


def _mm_nt(a, b):
    return lax.dot_general(a.astype(BF16), b.astype(BF16), (((1,), (1,)), ((), ())),
                           preferred_element_type=F32)


def _mm_tn(a, b):
    return lax.dot_general(a.astype(BF16), b.astype(BF16), (((0,), (0,)), ((), ())),
                           preferred_element_type=F32)


def _split3(x):
    hi = x.astype(BF16)
    r1 = x - hi.astype(F32)
    mid = r1.astype(BF16)
    lo = (r1 - mid.astype(F32)).astype(BF16)
    return hi, mid, lo


def _dot01_right(x, m01):
    hi, mid, lo = _split3(x)
    d = lambda p: jnp.dot(p, m01, preferred_element_type=F32)
    return d(hi) + d(mid) + d(lo)


def _dot01_left(m01, x):
    hi, mid, lo = _split3(x)
    d = lambda p: jnp.dot(m01, p, preferred_element_type=F32)
    return d(hi) + d(mid) + d(lo)


def _dot_f32(a, b):
    a_hi = a.astype(BF16)
    a_lo = (a - a_hi.astype(F32)).astype(BF16)
    b_hi = b.astype(BF16)
    b_lo = (b - b_hi.astype(F32)).astype(BF16)
    d = lambda p, q: jnp.dot(p, q, preferred_element_type=F32)
    return d(a_hi, b_hi) + d(a_hi, b_lo) + d(a_lo, b_hi) + d(a_lo, b_lo)


def _sigmoid(x):
    return 1.0 / (1.0 + jnp.exp(-x))


def _silu(x):
    return x * _sigmoid(x)


def _softplus(x):
    return jnp.maximum(x, 0.0) + jnp.log(1.0 + jnp.exp(-jnp.abs(x)))


def _bcast_col(x, j, width):
    return jnp.broadcast_to(x[:, j:j + 1], (x.shape[0], width))


IN_SEGS = (3 * DN_WIDTH, DN_WIDTH, NSA_WIDTH, 6 * NSA_HEAD_DIM, 2 * CONV_WIDTH, LANES)
IN_COLS = sum(IN_SEGS)


def _in_proj_kernel(x_ref, nw_ref, w_ref, wabt_ref,
                    qkv_ref, z_ref, nq_ref, nkv_ref, cu_ref, small_ref, abt_ref):
    xf = x_ref[...]
    ms = jnp.mean(xf * xf, axis=-1, keepdims=True)
    hb = (xf * lax.rsqrt(ms + EPS) * nw_ref[...]).astype(BF16)
    off = 0
    for ref, width in zip((qkv_ref, z_ref, nq_ref, nkv_ref, cu_ref, small_ref), IN_SEGS):
        ref[...] = jnp.dot(hb, w_ref[:, off:off + width], preferred_element_type=F32)
        off += width
    abt_ref[...] = lax.dot_general(wabt_ref[...], hb, (((1,), (1,)), ((), ())),
                                   preferred_element_type=F32)


def _in_proj(x2, norm_w, w_cat, w_abt):
    n = x2.shape[0]
    tm = ROW_TILE
    out_shape = [jax.ShapeDtypeStruct((n, w), F32) for w in IN_SEGS]
    out_shape.append(jax.ShapeDtypeStruct((16, n), F32))
    out_specs = [pl.BlockSpec((tm, w), lambda i: (i, 0)) for w in IN_SEGS]
    out_specs.append(pl.BlockSpec((16, tm), lambda i: (0, i)))
    return pl.pallas_call(
        _in_proj_kernel,
        out_shape=out_shape,
        grid=(n // tm,),
        in_specs=[
            pl.BlockSpec((tm, D_MODEL), lambda i: (i, 0)),
            pl.BlockSpec((1, D_MODEL), lambda i: (0, 0)),
            pl.BlockSpec((D_MODEL, IN_COLS), lambda i: (0, 0)),
            pl.BlockSpec((16, D_MODEL), lambda i: (0, 0)),
        ],
        out_specs=out_specs,
        compiler_params=_params(("parallel",)),
        name="in_proj",
    )(x2, norm_w, w_cat, w_abt)


def _dn_prep_kernel(qkv_ref, small_ref, abt_ref, cw_ref, pcol_ref, prow_ref,
                    q_ref, k_ref, v_ref, gb_ref, grow_ref, buf):
    t = pl.program_id(1)
    ts = SEQ_TILE

    @pl.when(t == 0)
    def _():
        buf[0:8, :] = jnp.zeros((8, 3 * DN_WIDTH), F32)

    buf[8:8 + ts, :] = qkv_ref[0]
    for c in range(3 * DN_HEADS):
        cs = slice(c * LANES, (c + 1) * LANES)
        acc = cw_ref[0:1, cs] * buf[5:5 + ts, cs]
        for j in range(1, DN_CONV):
            acc = acc + cw_ref[j:j + 1, cs] * buf[5 + j:5 + j + ts, cs]
        y = _silu(acc)
        if c < 2 * DN_HEADS:
            y = y * lax.rsqrt(jnp.sum(y * y, axis=-1, keepdims=True) + EPS)
        if c < DN_HEADS:
            q_ref[0, :, cs] = y * DN_HEAD_DIM ** -0.5
        elif c < 2 * DN_HEADS:
            k_ref[0, :, (c - DN_HEADS) * LANES:(c - DN_HEADS + 1) * LANES] = y
        else:
            v_ref[0, :, (c - 2 * DN_HEADS) * LANES:(c - 2 * DN_HEADS + 1) * LANES] = y
    buf[0:8, :] = buf[ts:ts + 8, :]

    sm = small_ref[0]
    lane = lax.broadcasted_iota(jnp.int32, sm.shape, 1)
    g_col = -jnp.exp(pcol_ref[0:1, :]) * _softplus(sm + pcol_ref[1:2, :])
    gb_ref[0] = jnp.where(lane < DN_HEADS, g_col, _sigmoid(sm))
    a_t = abt_ref[...]
    g_row = -jnp.exp(prow_ref[:, 0:1]) * _softplus(a_t + prow_ref[:, 1:2])
    grow_ref[...] = g_row[0:8, :]


def _dn_prep(qkv, small, abt, conv_w, pcol, prow):
    b = qkv.shape[0]
    ts = SEQ_TILE
    nt = SEQ // ts
    seq_spec = lambda w: pl.BlockSpec((1, ts, w), lambda i, t: (i, t, 0))
    full = lambda shape: pl.BlockSpec(shape, lambda i, t: (0,) * len(shape))
    return pl.pallas_call(
        _dn_prep_kernel,
        out_shape=[jax.ShapeDtypeStruct((b, SEQ, DN_WIDTH), F32)] * 3
        + [jax.ShapeDtypeStruct((b, SEQ, LANES), F32), jax.ShapeDtypeStruct((8, b * SEQ), F32)],
        grid=(b, nt),
        in_specs=[
            seq_spec(3 * DN_WIDTH), seq_spec(LANES),
            pl.BlockSpec((16, ts), lambda i, t: (0, i * nt + t)),
            full((DN_CONV, 3 * DN_WIDTH)), full((8, LANES)), full((16, LANES)),
        ],
        out_specs=[seq_spec(DN_WIDTH)] * 3
        + [seq_spec(LANES), pl.BlockSpec((8, ts), lambda i, t: (0, i * nt + t))],
        scratch_shapes=[pltpu.VMEM((ts + 8, 3 * DN_WIDTH), F32)],
        compiler_params=_params(("parallel", "arbitrary")),
        name="dn_prep",
    )(qkv, small, abt, conv_w, pcol, prow)


def _delta_kernel(q_ref, k_ref, v_ref, gb_ref, grow_ref, z_ref, nw_ref, o_ref, s_ref):
    t = pl.program_id(1)
    ts = SEQ_TILE
    ch = DN_CHUNK
    hd = DN_HEAD_DIM

    @pl.when(t == 0)
    def _():
        s_ref[...] = jnp.zeros(s_ref.shape, F32)

    r = lax.broadcasted_iota(jnp.int32, (ts, ts), 0)
    c = lax.broadcasted_iota(jnp.int32, (ts, ts), 1)
    same_chunk = (r // ch) == (c // ch)
    tril = same_chunk & (r >= c)
    strict = same_chunk & (r > c)
    same16 = (r // 16) == (c // 16)
    eye = jnp.where(r == c, 1.0, 0.0).astype(F32)
    m_col = jnp.where(tril, 1.0, 0.0).astype(BF16)
    m_row = jnp.where(same_chunk & (r <= c), 1.0, 0.0).astype(BF16)

    gb = gb_ref[0]
    gc_col = _dot01_left(m_col, gb)
    gc_row = _dot01_right(grow_ref[...], m_row)

    pre = []
    for h in range(DN_HEADS):
        hs = slice(h * hd, (h + 1) * hd)
        qh = q_ref[0, :, hs]
        kh = k_ref[0, :, hs]
        vh = v_ref[0, :, hs]
        gcb = _bcast_col(gc_col, h, ts)
        grb = jnp.broadcast_to(gc_row[h:h + 1, :], (ts, ts))
        decay = jnp.where(tril, jnp.exp(jnp.where(tril, gcb - grb, 0.0)), 0.0)
        beta = _bcast_col(gb, DN_HEADS + h, hd)
        gc128 = gcb[:, :hd]
        expg = jnp.exp(gc128)
        kb = kh * beta
        vb = vh * beta
        a_mat = jnp.where(strict, _mm_nt(kb, kh) * decay, 0.0)
        d_mat = jnp.where(same16, a_mat, 0.0)
        e_mat = a_mat - d_mat
        d2 = _mm(d_mat, d_mat)
        d4 = _mm(d2, d2)
        d8 = _mm(d4, d4)
        p_mat = _mm(_mm(_mm(eye - d_mat, eye + d2), eye + d4), eye + d8)
        m_mat = _mm(p_mat, e_mat)
        m2 = _mm(m_mat, m_mat)
        t_mat = _mm(_mm(eye - m_mat, eye + m2), p_mat)
        sol = _mm(t_mat, jnp.concatenate([vb, kb * expg], axis=1))
        attn = jnp.where(tril, _mm_nt(qh, kh) * decay, 0.0)
        pre.append((sol[:, :hd], sol[:, hd:], attn, qh * expg, kh, gc128))

    outs = [[] for _ in range(DN_HEADS)]
    for ci in range(ts // ch):
        rs = slice(ci * ch, (ci + 1) * ch)
        for h in range(DN_HEADS):
            u, w, attn, qg, kh, gc128 = pre[h]
            s_mat = s_ref[h]
            v_new = u[rs] - _mm(w[rs], s_mat)
            o_c = _mm(qg[rs], s_mat) + _mm(attn[rs, ci * ch:(ci + 1) * ch], v_new)
            g_last = gc128[ci * ch + ch - 1:ci * ch + ch, :]
            k_dec = kh[rs] * jnp.exp(g_last - gc128[rs])
            s_ref[h] = s_mat * jnp.exp(g_last) + _mm_tn(k_dec, v_new)
            outs[h].append(o_c)

    for h in range(DN_HEADS):
        hs = slice(h * hd, (h + 1) * hd)
        o = jnp.concatenate(outs[h], axis=0)
        o = o * lax.rsqrt(jnp.mean(o * o, axis=-1, keepdims=True) + EPS) * nw_ref[...]
        o_ref[0, :, hs] = o * _silu(z_ref[0, :, hs])


def _delta_rule(q, k, v, gb, grow, z, norm_w):
    b = q.shape[0]
    ts = SEQ_TILE
    nt = SEQ // ts
    seq_spec = lambda w: pl.BlockSpec((1, ts, w), lambda i, t: (i, t, 0))
    return pl.pallas_call(
        _delta_kernel,
        out_shape=jax.ShapeDtypeStruct((b, SEQ, DN_WIDTH), F32),
        grid=(b, nt),
        in_specs=[
            seq_spec(DN_WIDTH), seq_spec(DN_WIDTH), seq_spec(DN_WIDTH), seq_spec(LANES),
            pl.BlockSpec((8, ts), lambda i, t: (0, i * nt + t)),
            seq_spec(DN_WIDTH),
            pl.BlockSpec((1, DN_HEAD_DIM), lambda i, t: (0, 0)),
        ],
        out_specs=seq_spec(DN_WIDTH),
        scratch_shapes=[pltpu.VMEM((DN_HEADS, DN_HEAD_DIM, DN_HEAD_DIM), F32)],
        compiler_params=_params(("parallel", "arbitrary")),
        name="delta_rule",
    )(q, k, v, gb, grow, z, norm_w)


CONV_HALO = 32


def _conformer_kernel(u_ref, w_ref, b_ref, lnw_ref, lnb_ref, o_ref, buf):
    t = pl.program_id(1)
    ts = SEQ_TILE

    @pl.when(t == 0)
    def _():
        buf[0:CONV_HALO, :] = jnp.zeros((CONV_HALO, CONV_WIDTH), F32)

    u = u_ref[0]
    buf[CONV_HALO:CONV_HALO + ts, :] = u[:, :CONV_WIDTH] * _sigmoid(u[:, CONV_WIDTH:])
    base = CONV_HALO - (CONV_KERNEL - 1)
    rows = 64
    for rc in range(ts // rows):
        parts = []
        for cc in range(CONV_WIDTH // LANES):
            cs = slice(cc * LANES, (cc + 1) * LANES)
            r0 = base + rc * rows
            acc = w_ref[0:1, cs] * buf[r0:r0 + rows, cs]
            for j in range(1, CONV_KERNEL):
                acc = acc + w_ref[j:j + 1, cs] * buf[r0 + j:r0 + j + rows, cs]
            parts.append(acc)
        h = jnp.concatenate(parts, axis=1) + b_ref[...]
        mu = jnp.mean(h, axis=-1, keepdims=True)
        var = jnp.mean(jnp.square(h - mu), axis=-1, keepdims=True)
        hn = (h - mu) * lax.rsqrt(var + EPS) * lnw_ref[...] + lnb_ref[...]
        o_ref[0, rc * rows:(rc + 1) * rows, :] = _silu(hn)
    buf[0:CONV_HALO, :] = buf[ts:ts + CONV_HALO, :]


def _conformer(u, dw_w, dw_b, ln_w, ln_b):
    b = u.shape[0]
    ts = SEQ_TILE
    full = lambda shape: pl.BlockSpec(shape, lambda i, t: (0,) * len(shape))
    return pl.pallas_call(
        _conformer_kernel,
        out_shape=jax.ShapeDtypeStruct((b, SEQ, CONV_WIDTH), F32),
        grid=(b, SEQ // ts),
        in_specs=[
            pl.BlockSpec((1, ts, 2 * CONV_WIDTH), lambda i, t: (i, t, 0)),
            full((CONV_KERNEL, CONV_WIDTH)), full((1, CONV_WIDTH)),
            full((1, CONV_WIDTH)), full((1, CONV_WIDTH)),
        ],
        out_specs=pl.BlockSpec((1, ts, CONV_WIDTH), lambda i, t: (i, t, 0)),
        scratch_shapes=[pltpu.VMEM((ts + CONV_HALO, CONV_WIDTH), F32)],
        compiler_params=_params(("parallel", "arbitrary")),
        name="conformer",
    )(u, dw_w, dw_b, ln_w, ln_b)


def _rms_rows(x, w):
    return x * lax.rsqrt(jnp.mean(x * x, axis=-1, keepdims=True) + EPS) * w


def _nsa_prep_kernel(kc_ref, vc_ref, ks_ref, kw_ref, pos_ref, w1_ref, w2_ref, knw_ref,
                     kcmp_ref, vcmp_ref, ksn_ref, kwn_ref):
    half = CMP_STRIDE * NSA_HEAD_DIM

    def compress(x, i):
        u_lo = _mm(x + pos_ref[2 * i:2 * i + 1, :], w1_ref[i, 0:half, :])
        u_hi = _mm(x + pos_ref[2 * i + 1:2 * i + 2, :], w1_ref[i, half:2 * half, :])
        hid = _silu(u_lo + pltpu.roll(u_hi, N_CMP_PAD - 1, axis=0))
        return _mm(hid, w2_ref[i])

    kcmp_ref[0] = _rms_rows(compress(kc_ref[0], 0), knw_ref[0:1, :])
    vcmp_ref[0] = compress(vc_ref[0], 1)
    ksn_ref[0] = _rms_rows(ks_ref[0], knw_ref[1:2, :])
    kwn_ref[0] = _rms_rows(kw_ref[0], knw_ref[2:3, :])


def _nsa_prep(kc, vc, ks, kw, pos, w1, w2, knw):
    b = kc.shape[0]
    hd = NSA_HEAD_DIM
    flat = CMP_STRIDE * hd
    full = lambda shape: pl.BlockSpec(shape, lambda i: (0,) * len(shape))
    bspec = lambda r, w: pl.BlockSpec((1, r, w), lambda i: (i, 0, 0))
    return pl.pallas_call(
        _nsa_prep_kernel,
        out_shape=[jax.ShapeDtypeStruct((b, N_CMP_PAD, hd), F32)] * 2
        + [jax.ShapeDtypeStruct((b, SEQ, hd), F32)] * 2,
        grid=(b,),
        in_specs=[
            bspec(N_CMP_PAD, flat), bspec(N_CMP_PAD, flat), bspec(SEQ, hd), bspec(SEQ, hd),
            full((4, flat)), full((2, 2 * flat, CMP_HIDDEN)), full((2, CMP_HIDDEN, hd)), full((3, hd)),
        ],
        out_specs=[bspec(N_CMP_PAD, hd)] * 2 + [bspec(SEQ, hd)] * 2,
        compiler_params=_params(("parallel",)),
        name="nsa_prep",
    )(kc, vc, ks, kw, pos, w1, w2, knw)


def _nsa_attn_kernel(q_ref, qnw_ref, kcmp_ref, vcmp_ref, ks_ref, vs_ref, kw_ref, vw_ref,
                     gate_ref, bcmp_ref, btab_ref, o_ref,
                     madd_ref, s_ref, ksb_ref, kwb_ref, vsa_ref, vwa_ref):
    i = pl.program_id(1)
    tq = Q_TILE
    nh = NSA_HEADS
    hd = NSA_HEAD_DIM
    t0 = i * tq

    @pl.when(i == 0)
    def _():
        ones = jnp.ones((SEQ, hd), BF16)
        ksb_ref[...] = ks_ref[0].astype(BF16)
        kwb_ref[...] = kw_ref[0].astype(BF16)
        vsa_ref[...] = jnp.concatenate([vs_ref[0].astype(BF16), ones], axis=1)
        vwa_ref[...] = jnp.concatenate([vw_ref[0].astype(BF16), ones], axis=1)

    qs = jnp.concatenate(
        [_rms_rows(q_ref[0, h], qnw_ref[...]) * hd ** -0.5 for h in range(nh)], axis=0).astype(BF16)

    row = lax.broadcasted_iota(jnp.int32, (tq, LANES), 0)
    lane = lax.broadcasted_iota(jnp.int32, (tq, LANES), 1)
    qpos = t0 + row

    s_all = _mm_nt(qs, kcmp_ref[0])
    cmp_valid = (qpos >= lane * CMP_STRIDE + (CMP_LEN - 1)) & (lane < N_CMP_PAD - 1)
    p_rows = []
    p_sum = jnp.zeros((tq, LANES), F32)
    for h in range(nh):
        s = jnp.where(cmp_valid, s_all[h * tq:(h + 1) * tq] + bcmp_ref[h], NEG_INF)
        m = jnp.max(s, axis=-1, keepdims=True)
        p = jnp.where(cmp_valid, jnp.exp(s - m), 0.0)
        l = jnp.sum(p, axis=-1, keepdims=True)
        p = p * jnp.where(l > 0.0, 1.0 / l, 0.0)
        p_rows.append(p)
        p_sum = p_sum + p
    o_cmp = _mm(jnp.concatenate(p_rows, axis=0), vcmp_ref[0])

    ss = lax.broadcasted_iota(jnp.int32, (N_SLC, N_CMP_PAD), 0)
    jj = lax.broadcasted_iota(jnp.int32, (N_SLC, N_CMP_PAD), 1)
    overlap_t = ((jj * CMP_STRIDE < ss * SLC_BLOCK + SLC_BLOCK)
                 & (jj * CMP_STRIDE + CMP_LEN > ss * SLC_BLOCK) & (jj < N_CMP_PAD - 1))
    overlap_t = jnp.where(overlap_t, 1.0, 0.0).astype(BF16)
    nt = lambda a, b_: lax.dot_general(a, b_, (((1,), (1,)), ((), ())), preferred_element_type=F32)
    p_hi, p_mid, p_lo = _split3(p_sum)
    imp = nt(overlap_t, p_hi) + nt(overlap_t, p_mid) + nt(overlap_t, p_lo)
    blk = lax.broadcasted_iota(jnp.int32, (N_SLC, tq), 0)
    cur = (t0 + lax.broadcasted_iota(jnp.int32, (N_SLC, tq), 1)) // SLC_BLOCK
    causal_blk = blk <= cur
    forced = (blk == 0) | (blk == cur) | (blk == cur - 1)
    imp = jnp.where(causal_blk & forced, FORCE, jnp.where(causal_blk, imp, -1.0))
    rank = jnp.zeros((N_SLC, tq), jnp.int32)
    for s2 in range(N_SLC):
        other = jnp.broadcast_to(imp[s2:s2 + 1, :], (N_SLC, tq))
        beats = (other > imp) | ((other == imp) & (blk > s2))
        rank = rank + jnp.where(beats, 1, 0)
    sel_t = jnp.where(rank < SLC_TOP_N, 1.0, 0.0).astype(BF16)
    er = lax.broadcasted_iota(jnp.int32, (N_SLC, SEQ), 0)
    ec = lax.broadcasted_iota(jnp.int32, (N_SLC, SEQ), 1)
    expand = jnp.where(ec // SLC_BLOCK == er, 1.0, 0.0).astype(BF16)
    key = lax.broadcasted_iota(jnp.int32, (tq, SEQ), 1)
    qall = t0 + lax.broadcasted_iota(jnp.int32, (tq, SEQ), 0)
    chosen = (_mm_tn(sel_t, expand) > 0.5) & (qall >= key)
    madd_ref[...] = jnp.where(chosen, 0.0, NEG_INF)

    neg = jnp.full((nh * tq, tq), NEG_INF, F32)
    zero = jnp.zeros((nh * tq, LANES), F32)

    def scores(k_tile, bias_idx, add):
        s_t = _mm_nt(qs, k_tile)
        return jnp.concatenate(
            [s_t[h * tq:(h + 1) * tq] + (btab_ref[h, bias_idx] + add) for h in range(nh)], axis=0)

    def normalise(acc):
        return acc[:, :hd] * (1.0 / _bcast_col(acc, hd, hd))

    n_pairs = (i + 2) // 2

    def slc_pass1(j, m_run):
        for u in range(2):
            kt = 2 * j + u
            k0 = pl.multiple_of(kt * tq, tq)
            s_m = scores(ksb_ref[pl.ds(k0, tq), :], jnp.clip(i - kt, 0, 2), madd_ref[:, pl.ds(k0, tq)])
            s_ref[:, pl.ds(k0, tq)] = s_m
            m_run = jnp.maximum(m_run, s_m)
        return m_run

    m_slc = jnp.broadcast_to(
        jnp.max(lax.fori_loop(0, n_pairs, slc_pass1, neg), axis=-1, keepdims=True), (nh * tq, tq))

    def slc_pass2(j, acc):
        for u in range(2):
            k0 = pl.multiple_of((2 * j + u) * tq, tq)
            p = jnp.exp(s_ref[:, pl.ds(k0, tq)] - m_slc).astype(BF16)
            acc = acc + jnp.dot(p, vsa_ref[pl.ds(k0, tq), :], preferred_element_type=F32)
        return acc

    o_slc = normalise(lax.fori_loop(0, n_pairs, slc_pass2, zero))

    kcol = lax.broadcasted_iota(jnp.int32, (tq, tq), 1)
    qrow = t0 + lax.broadcasted_iota(jnp.int32, (tq, tq), 0)
    n_win = WINDOW // tq + 1
    starts = []
    m_run = neg
    for j in range(n_win):
        d = n_win - 1 - j
        k0 = pl.multiple_of(jnp.maximum(i - d, 0) * tq, tq)
        starts.append(k0)
        dist = qrow - (k0 + kcol)
        ok = (dist >= 0) & (dist < WINDOW) & (i >= d)
        s_m = scores(kwb_ref[pl.ds(k0, tq), :], min(d, 2), jnp.where(ok, 0.0, NEG_INF))
        s_ref[:, j * tq:(j + 1) * tq] = s_m
        m_run = jnp.maximum(m_run, s_m)
    m_win = jnp.broadcast_to(jnp.max(m_run, axis=-1, keepdims=True), (nh * tq, tq))
    acc = zero
    for j in range(n_win):
        p = jnp.exp(s_ref[:, j * tq:(j + 1) * tq] - m_win).astype(BF16)
        acc = acc + jnp.dot(p, vwa_ref[pl.ds(starts[j], tq), :], preferred_element_type=F32)
    o_win = normalise(acc)

    gates = _sigmoid(gate_ref[0])
    for h in range(nh):
        hs = slice(h * tq, (h + 1) * tq)
        g0 = 2 * DN_HEADS + 3 * h
        o_ref[0, h] = (_bcast_col(gates, g0, hd) * o_cmp[hs]
                       + _bcast_col(gates, g0 + 1, hd) * o_slc[hs]
                       + _bcast_col(gates, g0 + 2, hd) * o_win[hs])


def _nsa_attn(q, qnw, kcmp, vcmp, ks, vs, kw, vw, small, bias_cmp, btab):
    b = q.shape[0]
    tq = Q_TILE
    nh, hd = NSA_HEADS, NSA_HEAD_DIM
    full = lambda shape: pl.BlockSpec(shape, lambda bi, i: (0,) * len(shape))
    per_b = lambda r: pl.BlockSpec((1, r, hd), lambda bi, i: (bi, 0, 0))
    return pl.pallas_call(
        _nsa_attn_kernel,
        out_shape=jax.ShapeDtypeStruct((b, nh, SEQ, hd), F32),
        grid=(b, SEQ // tq),
        in_specs=[
            pl.BlockSpec((1, nh, tq, hd), lambda bi, i: (bi, 0, i, 0)),
            full((1, hd)),
            per_b(N_CMP_PAD), per_b(N_CMP_PAD), per_b(SEQ), per_b(SEQ), per_b(SEQ), per_b(SEQ),
            pl.BlockSpec((1, tq, LANES), lambda bi, i: (bi, i, 0)),
            pl.BlockSpec((nh, tq, N_CMP_PAD), lambda bi, i: (0, i, 0)),
            full((nh, 3, tq, tq)),
        ],
        out_specs=pl.BlockSpec((1, nh, tq, hd), lambda bi, i: (bi, 0, i, 0)),
        scratch_shapes=[pltpu.VMEM((tq, SEQ), F32), pltpu.VMEM((nh * tq, SEQ), F32),
                        pltpu.VMEM((SEQ, hd), BF16), pltpu.VMEM((SEQ, hd), BF16),
                        pltpu.VMEM((SEQ, 2 * hd), BF16), pltpu.VMEM((SEQ, 2 * hd), BF16)],
        compiler_params=_params(("parallel", "arbitrary")),
        name="nsa_attn",
    )(q, qnw, kcmp, vcmp, ks, vs, kw, vw, small, bias_cmp, btab)


def _out_router_kernel(ya_ref, yb_ref, yc_ref, x_ref, wo_ref, fnw_ref, rw_ref, rb_ref,
                       xo_ref, h_ref, idx_ref, wt_ref, hist_ref):
    wa = DN_WIDTH
    wb = wa + NSA_WIDTH
    y = (jnp.dot(ya_ref[...].astype(BF16), wo_ref[0:wa, :], preferred_element_type=F32)
         + jnp.dot(yb_ref[...].astype(BF16), wo_ref[wa:wb, :], preferred_element_type=F32)
         + jnp.dot(yc_ref[...].astype(BF16), wo_ref[wb:, :], preferred_element_type=F32))
    xn = x_ref[...] + y
    xo_ref[...] = xn
    h = xn * lax.rsqrt(jnp.mean(xn * xn, axis=-1, keepdims=True) + EPS) * fnw_ref[...]
    h_ref[...] = h
    logits = _dot_f32(h, rw_ref[...]) + rb_ref[...]
    lane = lax.broadcasted_iota(jnp.int32, logits.shape, 1)
    vals, idxs = [], []
    for _ in range(TOP_K):
        m = jnp.max(logits, axis=-1, keepdims=True)
        ix = jnp.min(jnp.where(logits == m, lane, LANES), axis=-1, keepdims=True)
        vals.append(m)
        idxs.append(ix)
        logits = jnp.where(lane == ix, -jnp.inf, logits)
    es = [jnp.exp(v - vals[0]) for v in vals]
    inv = 1.0 / (es[0] + es[1] + es[2] + es[3])
    idx_out = jnp.zeros(lane.shape, jnp.int32)
    wt_out = jnp.zeros(lane.shape, F32)
    chosen = jnp.zeros(lane.shape, F32)
    for k in range(TOP_K):
        idx_out = jnp.where(lane == k, idxs[k], idx_out)
        wt_out = jnp.where(lane == k, es[k] * inv, wt_out)
        chosen = chosen + jnp.where(lane == idxs[k], 1.0, 0.0)
    idx_ref[...] = idx_out
    wt_ref[...] = wt_out
    hist_ref[0] = jnp.broadcast_to(jnp.sum(chosen, axis=0, keepdims=True), (8, LANES))


def _out_router(ya, yb, yc, x2, w_out, fnw, rw, rb):
    n = x2.shape[0]
    tm = ROW_TILE
    row = lambda w: pl.BlockSpec((tm, w), lambda i: (i, 0))
    full = lambda shape: pl.BlockSpec(shape, lambda i: (0,) * len(shape))
    return pl.pallas_call(
        _out_router_kernel,
        out_shape=[jax.ShapeDtypeStruct((n, D_MODEL), F32), jax.ShapeDtypeStruct((n, D_MODEL), F32),
                   jax.ShapeDtypeStruct((n, LANES), jnp.int32), jax.ShapeDtypeStruct((n, LANES), F32),
                   jax.ShapeDtypeStruct((n // tm, 8, LANES), F32)],
        grid=(n // tm,),
        in_specs=[row(DN_WIDTH), row(NSA_WIDTH), row(CONV_WIDTH), row(D_MODEL),
                  full((D_MODEL, D_MODEL)), full((1, D_MODEL)), full((D_MODEL, LANES)), full((1, LANES))],
        out_specs=[row(D_MODEL), row(D_MODEL), row(LANES), row(LANES),
                   pl.BlockSpec((1, 8, LANES), lambda i: (i, 0, 0))],
        compiler_params=_params(("parallel",)),
        name="out_router",
    )(ya, yb, yc, x2, w_out, fnw, rw, rb)


def _slots_kernel(idx_ref, base_ref, slot_ref):
    tm = ROW_TILE
    idx = idx_ref[...]
    lane = lax.broadcasted_iota(jnp.int32, (tm, LANES), 1)
    onehots = [jnp.where(lane == _bcast_col(idx, k, LANES), 1.0, 0.0) for k in range(TOP_K)]
    cnt = (onehots[0] + onehots[1]) + (onehots[2] + onehots[3])
    r = lax.broadcasted_iota(jnp.int32, (tm, tm), 0)
    c = lax.broadcasted_iota(jnp.int32, (tm, tm), 1)
    earlier = jnp.where(r > c, 1.0, 0.0).astype(BF16)
    rank = jnp.dot(earlier, cnt.astype(BF16), preferred_element_type=F32) + base_ref[0, 0:1, :]
    ones = jnp.ones((8, LANES), BF16)
    row8 = lax.broadcasted_iota(jnp.int32, (8, tm), 0)
    out = jnp.zeros((8, tm), F32)
    for k in range(TOP_K):
        hi, mid, lo = _split3(rank * onehots[k])
        d = lambda p: lax.dot_general(ones, p, (((1,), (1,)), ((), ())), preferred_element_type=F32)
        out = jnp.where(row8 == k, d(hi) + d(mid) + d(lo), out)
    slot_ref[...] = out.astype(jnp.int32)


def _slots(idx, base3):
    n = idx.shape[0]
    tm = ROW_TILE
    return pl.pallas_call(
        _slots_kernel,
        out_shape=jax.ShapeDtypeStruct((8, n), jnp.int32),
        grid=(n // tm,),
        in_specs=[pl.BlockSpec((tm, LANES), lambda i: (i, 0)),
                  pl.BlockSpec((1, 8, LANES), lambda i: (i, 0, 0))],
        out_specs=pl.BlockSpec((8, tm), lambda i: (0, i)),
        compiler_params=_params(("parallel",)),
        name="moe_slots",
    )(idx, base3)


def _dispatch_kernel(slot_ref, zoff_ref, nv_ref, h_ref, xs_hbm, zbuf, sem, zsem):
    i = pl.program_id(0)
    tm = ROW_TILE
    n = slot_ref.shape[0] // TOP_K
    base = i * tm

    @pl.when(i == 0)
    def _():
        zbuf[...] = jnp.zeros(zbuf.shape, F32)
        n_blocks = xs_hbm.shape[0] // MOE_TILE

        def fill(row0):
            return pltpu.make_async_copy(zbuf, xs_hbm.at[pl.ds(pl.multiple_of(row0, MOE_TILE), MOE_TILE)], zsem)

        def fill_region_end(e, c):
            fill(zoff_ref[e]).start()
            return c

        def fill_tail(b, c):
            fill(b * MOE_TILE).start()
            return c

        def wait_fill(b, c):
            fill(0).wait()
            return c

        lax.fori_loop(0, N_EXPERTS, fill_region_end, 0)
        lax.fori_loop(nv_ref[0], n_blocks, fill_tail, 0)
        lax.fori_loop(0, N_EXPERTS + n_blocks - nv_ref[0], wait_fill, 0)

    def body(r, c):
        for k in range(TOP_K):
            pltpu.make_async_copy(h_ref.at[pl.ds(r, 1)],
                                  xs_hbm.at[pl.ds(slot_ref[k * n + base + r], 1)], sem).start()
        return c

    lax.fori_loop(0, tm, body, 0, unroll=4)
    done = xs_hbm.at[pl.ds(0, TOP_K * tm)]
    pltpu.make_async_copy(done, done, sem).wait()


def _dispatch(slot_flat, zero_off, n_valid, h, n_slots):
    n = h.shape[0]
    tm = ROW_TILE
    return pl.pallas_call(
        _dispatch_kernel,
        out_shape=jax.ShapeDtypeStruct((n_slots, D_MODEL), F32),
        grid_spec=pltpu.PrefetchScalarGridSpec(
            num_scalar_prefetch=3,
            grid=(n // tm,),
            in_specs=[pl.BlockSpec((tm, D_MODEL), lambda i, s, z, nv: (i, 0))],
            out_specs=pl.BlockSpec(memory_space=pl.ANY),
            scratch_shapes=[pltpu.VMEM((MOE_TILE, D_MODEL), F32),
                            pltpu.SemaphoreType.DMA, pltpu.SemaphoreType.DMA],
        ),
        compiler_params=_params(("arbitrary",)),
        name="moe_dispatch",
    )(slot_flat, zero_off, n_valid, h)


def _expert_kernel(be_ref, nv_ref, xs_ref, wgu_ref, bgu_ref, wd_ref, bd_ref, o_ref, wgu_bf, wd_bf):
    i = pl.program_id(0)
    new_expert = (i == 0) | (be_ref[i] != be_ref[jnp.maximum(i - 1, 0)])

    @pl.when((i < nv_ref[0]) & new_expert)
    def _():
        step = 512
        for c0 in range(0, 2 * D_FF, step):
            wgu_bf[:, c0:c0 + step] = wgu_ref[:, c0:c0 + step].astype(BF16)
        for c0 in range(0, D_MODEL, step):
            wd_bf[:, c0:c0 + step] = wd_ref[:, c0:c0 + step].astype(BF16)

    @pl.when(i < nv_ref[0])
    def _():
        gu = jnp.dot(xs_ref[...].astype(BF16), wgu_bf[...], preferred_element_type=F32) + bgu_ref[...]
        gate = jnp.minimum(gu[:, :D_FF], SWIGLU_LIMIT)
        up = jnp.clip(gu[:, D_FF:], -SWIGLU_LIMIT, SWIGLU_LIMIT)
        act = (up + 1.0) * gate * _sigmoid(SWIGLU_ALPHA * gate)
        o_ref[...] = jnp.dot(act.astype(BF16), wd_bf[...], preferred_element_type=F32) + bd_ref[...]

    @pl.when(i >= nv_ref[0])
    def _():
        o_ref[...] = jnp.zeros(o_ref.shape, F32)


def _expert_ffn(layer, block_expert, n_valid, xs, wgu, bgu, wd, bd):
    n_slots = xs.shape[0]
    tm = MOE_TILE
    return pl.pallas_call(
        _expert_kernel,
        out_shape=jax.ShapeDtypeStruct((n_slots, D_MODEL), F32),
        grid_spec=pltpu.PrefetchScalarGridSpec(
            num_scalar_prefetch=2,
            grid=(n_slots // tm,),
            in_specs=[
                pl.BlockSpec((tm, D_MODEL), lambda i, be, nv: (i, 0)),
                pl.BlockSpec((None, None, D_MODEL, 2 * D_FF), lambda i, be, nv: (layer, be[i], 0, 0)),
                pl.BlockSpec((None, None, 1, 2 * D_FF), lambda i, be, nv: (layer, be[i], 0, 0)),
                pl.BlockSpec((None, None, D_FF, D_MODEL), lambda i, be, nv: (layer, be[i], 0, 0)),
                pl.BlockSpec((None, None, 1, D_MODEL), lambda i, be, nv: (layer, be[i], 0, 0)),
            ],
            out_specs=pl.BlockSpec((tm, D_MODEL), lambda i, be, nv: (i, 0)),
            scratch_shapes=[pltpu.VMEM((D_MODEL, 2 * D_FF), BF16), pltpu.VMEM((D_FF, D_MODEL), BF16)],
        ),
        compiler_params=pltpu.CompilerParams(dimension_semantics=("arbitrary",),
                                             vmem_limit_bytes=EXPERT_VMEM_LIMIT_BYTES),
        name="expert_ffn",
    )(block_expert, n_valid, xs, wgu, bgu, wd, bd)


def _combine_kernel(slot_ref, ys_hbm, x_ref, wt_ref, o_ref, buf, sems):
    i = pl.program_id(0)
    tc = COMBINE_TILE
    nsteps = pl.num_programs(0)
    n = slot_ref.shape[0] // TOP_K
    cur = lax.rem(i, 2)

    def issue(step, par):
        base = step * tc

        def body(r, c):
            for k in range(TOP_K):
                pltpu.make_async_copy(ys_hbm.at[pl.ds(slot_ref[k * n + base + r], 1)],
                                      buf.at[par, k, pl.ds(r, 1)], sems.at[par]).start()
            return c

        lax.fori_loop(0, tc, body, 0, unroll=4)

    @pl.when(i == 0)
    def _():
        issue(0, 0)

    @pl.when(i + 1 < nsteps)
    def _():
        issue(i + 1, 1 - cur)

    pltpu.make_async_copy(buf.at[cur], buf.at[cur], sems.at[cur]).wait()
    w = wt_ref[...]
    acc = x_ref[...]
    for k in range(TOP_K):
        acc = acc + _bcast_col(w, k, D_MODEL) * buf[cur, k]
    o_ref[...] = acc


def _combine(slot_flat, ys, x2, wt):
    n = x2.shape[0]
    tc = COMBINE_TILE
    return pl.pallas_call(
        _combine_kernel,
        out_shape=jax.ShapeDtypeStruct((n, D_MODEL), F32),
        grid_spec=pltpu.PrefetchScalarGridSpec(
            num_scalar_prefetch=1,
            grid=(n // tc,),
            in_specs=[pl.BlockSpec(memory_space=pl.ANY),
                      pl.BlockSpec((tc, D_MODEL), lambda i, s: (i, 0)),
                      pl.BlockSpec((tc, LANES), lambda i, s: (i, 0))],
            out_specs=pl.BlockSpec((tc, D_MODEL), lambda i, s: (i, 0)),
            scratch_shapes=[pltpu.VMEM((2, TOP_K, tc, D_MODEL), F32), pltpu.SemaphoreType.DMA((2,))],
        ),
        compiler_params=_params(("arbitrary",)),
        name="moe_combine",
    )(slot_flat, ys, x2, wt)


def _moe(layer, h, x2, idx, wt, hist3, wgu, bgu, wd, bd):
    n = h.shape[0]
    tm = MOE_TILE
    n_slots = n * TOP_K + N_EXPERTS * tm
    n_blocks = n_slots // tm
    hist = hist3[:, 0, :N_EXPERTS]
    counts = jnp.sum(hist, axis=0).astype(jnp.int32)
    padded = (counts + tm - 1) // tm * tm
    pad_end = jnp.cumsum(padded)
    pad_start = pad_end - padded
    tile_base = pad_start[None, :].astype(F32) + (jnp.cumsum(hist, axis=0) - hist)
    base3 = jnp.broadcast_to(jnp.pad(tile_base, ((0, 0), (0, LANES - N_EXPERTS)))[:, None, :],
                             (hist.shape[0], 8, LANES))
    blk0 = jnp.arange(n_blocks) * tm
    block_expert = jnp.minimum(jnp.sum(blk0[:, None] >= pad_end[None, :], axis=1), N_EXPERTS - 1).astype(jnp.int32)
    n_valid = (pad_end[-1:] // tm).astype(jnp.int32)
    zero_off = jnp.maximum(pad_end - tm, 0).astype(jnp.int32)

    slot_flat = _slots(idx, base3)[:TOP_K].reshape(-1)
    xs = _dispatch(slot_flat, zero_off, n_valid, h, n_slots)
    ys = _expert_ffn(layer, block_expert, n_valid, xs, wgu, bgu, wd, bd)
    return _combine(slot_flat, ys, x2, wt)


def _t5_bucket(dist):
    n = jnp.maximum(dist, 0)
    max_exact = REL_BUCKETS // 2
    nf = jnp.maximum(n, 1).astype(F32)
    large = max_exact + (jnp.log(nf / max_exact) / math.log(REL_MAX_DIST / max_exact)
                         * (REL_BUCKETS - max_exact)).astype(jnp.int32)
    large = jnp.minimum(large, REL_BUCKETS - 1)
    return jnp.where(n < max_exact, n, large)


def _bias_tables(rel_bias):
    rel_bias = rel_bias.astype(F32)
    tq = Q_TILE

    def lookup(dist):
        bucket = _t5_bucket(dist)
        out = jnp.zeros((NSA_HEADS,) + dist.shape, F32)
        for bk in range(REL_BUCKETS):
            out = jnp.where(bucket[None] == bk, rel_bias[bk].reshape((NSA_HEADS,) + (1,) * dist.ndim), out)
        return out

    t_pos = jnp.arange(SEQ)
    cmp_end = jnp.arange(N_CMP_PAD) * CMP_STRIDE + CMP_LEN - 1
    bias_cmp = lookup(t_pos[:, None] - cmp_end[None, :])
    rr = jnp.arange(tq)[:, None] - jnp.arange(tq)[None, :]
    btab = lookup(jnp.stack([rr, rr + tq, rr + 2 * tq]))
    return bias_cmp, btab


def _layer(layer, x2, b, p, experts, bias_cmp, btab):
    n = x2.shape[0]
    w = p['w_in']
    o_a, o_b, o_q, o_kv, o_g, o_u = 2048, 2052, 2056, 2312, 2696, 2708
    w_small = jnp.concatenate([w[:, o_a:o_q], w[:, o_g:o_u],
                               jnp.zeros((D_MODEL, LANES - 8 - 3 * NSA_HEADS), F32)], axis=1)
    w_cat = jnp.concatenate([w[:, :o_a], w[:, o_q:o_kv], w[:, o_kv:o_g], w[:, o_u:], w_small],
                            axis=1).astype(BF16)
    w_abt = jnp.concatenate([w[:, o_a:o_q].T, jnp.zeros((8, D_MODEL), F32)], axis=0).astype(BF16)
    qkv, z, nq, nkv, cu, small, abt = _in_proj(x2, p['attn_norm_w'][None, :], w_cat, w_abt)

    pcol = jnp.zeros((8, LANES), F32).at[0, :DN_HEADS].set(p['dn_a_log']).at[1, :DN_HEADS].set(p['dn_dt_bias'])
    prow = jnp.zeros((16, LANES), F32).at[:DN_HEADS, 0].set(p['dn_a_log']).at[:DN_HEADS, 1].set(p['dn_dt_bias'])
    small3 = small.reshape(b, SEQ, LANES)
    dq, dk, dv, gb, grow = _dn_prep(qkv.reshape(b, SEQ, 3 * DN_WIDTH), small3, abt,
                                    p['dn_conv_w'], pcol, prow)
    y_a = _delta_rule(dq, dk, dv, gb, grow, z.reshape(b, SEQ, DN_WIDTH), p['dn_norm_w'][None, :])

    y_c = _conformer(cu.reshape(b, SEQ, 2 * CONV_WIDTH), p['conv_dw_w'], p['conv_dw_b'][None, :],
                     p['conv_ln_w'][None, :], p['conv_ln_b'][None, :])

    hd = NSA_HEAD_DIM
    kv = nkv.reshape(b, SEQ, 6, hd)
    flat = CMP_STRIDE * hd
    kc = kv[:, :, 0].reshape(b, N_CMP_PAD, flat)
    vc = kv[:, :, 1].reshape(b, N_CMP_PAD, flat)
    pos = p['nsa_cmp_pos'].reshape(4, flat)
    kcmp, vcmp, ksn, kwn = _nsa_prep(kc, vc, kv[:, :, 2], kv[:, :, 4], pos,
                                     p['nsa_cmp_w1'].astype(BF16), p['nsa_cmp_w2'].astype(BF16),
                                     p['nsa_k_norm_w'])
    q4 = nq.reshape(b, SEQ, NSA_HEADS, hd).transpose(0, 2, 1, 3)
    o4 = _nsa_attn(q4, p['nsa_q_norm_w'][None, :], kcmp, vcmp, ksn, kv[:, :, 3], kwn, kv[:, :, 5],
                   small3, bias_cmp, btab)
    y_b = o4.transpose(0, 2, 1, 3).reshape(n, NSA_WIDTH)

    rw = jnp.concatenate([p['router_w'], jnp.zeros((D_MODEL, LANES - N_EXPERTS), F32)], axis=1)
    rb = jnp.concatenate([p['router_b'], jnp.full((LANES - N_EXPERTS,), NEG_INF, F32)])[None, :]
    x_new, h, idx, wt, hist3 = _out_router(y_a.reshape(n, DN_WIDTH), y_b, y_c.reshape(n, CONV_WIDTH), x2,
                                           p['w_out'].astype(BF16), p['ffn_norm_w'][None, :], rw, rb)
    return _moe(layer, h, x_new, idx, wt, hist3, *experts)


def kernel(x, attn_norm_w, w_in, dn_conv_w, dn_a_log, dn_dt_bias, dn_norm_w, nsa_q_norm_w, nsa_k_norm_w, nsa_cmp_pos, nsa_cmp_w1, nsa_cmp_w2, conv_dw_w, conv_dw_b, conv_ln_w, conv_ln_b, w_out, ffn_norm_w, router_w, router_b, w_gate_up, b_gate_up, w_down, b_down, rel_bias):
    b, t, d = x.shape
    assert (t, d) == (SEQ, D_MODEL)
    stacked = dict(attn_norm_w=attn_norm_w, w_in=w_in, dn_conv_w=dn_conv_w, dn_a_log=dn_a_log,
                   dn_dt_bias=dn_dt_bias, dn_norm_w=dn_norm_w, nsa_q_norm_w=nsa_q_norm_w,
                   nsa_k_norm_w=nsa_k_norm_w, nsa_cmp_pos=nsa_cmp_pos, nsa_cmp_w1=nsa_cmp_w1,
                   nsa_cmp_w2=nsa_cmp_w2, conv_dw_w=conv_dw_w, conv_dw_b=conv_dw_b,
                   conv_ln_w=conv_ln_w, conv_ln_b=conv_ln_b, w_out=w_out, ffn_norm_w=ffn_norm_w,
                   router_w=router_w, router_b=router_b)
    experts = (w_gate_up, b_gate_up[:, :, None, :], w_down, b_down[:, :, None, :])
    bias_cmp, btab = _bias_tables(rel_bias)
    x2 = x.reshape(b * t, d)
    for l in range(w_in.shape[0]):
        x2 = _layer(l, x2, b, {k: v[l] for k, v in stacked.items()}, experts, bias_cmp, btab)
    return x2.reshape(b, t, d)
```

```python
import functools
import math

import numpy as np
import jax
import jax.numpy as jnp
from jax import lax
from jax.experimental import pallas as pl
from jax.experimental.pallas import tpu as pltpu

F32 = jnp.float32
BF16 = jnp.bfloat16

D_MODEL = 1024
SEQ = 2048
DN_HEADS = 4
DN_HEAD_DIM = 128
DN_WIDTH = DN_HEADS * DN_HEAD_DIM
DN_CONV = 4
DN_CHUNK = 64
NSA_HEADS = 4
NSA_HEAD_DIM = 64
NSA_WIDTH = NSA_HEADS * NSA_HEAD_DIM
CMP_LEN = 32
CMP_STRIDE = 16
CMP_HIDDEN = 2 * NSA_HEAD_DIM
SLC_BLOCK = 64
SLC_TOP_N = 16
WINDOW = 512
CONV_WIDTH = 256
CONV_KERNEL = 31
REL_BUCKETS = 32
REL_MAX_DIST = 128
N_EXPERTS = 32
TOP_K = 4
D_FF = D_MODEL
SWIGLU_LIMIT = 7.0
SWIGLU_ALPHA = 1.702
EPS = 1e-6
NEG_INF = -1e30
FORCE = 1e4

LANES = 128
VMEM_LIMIT_BYTES = 48 * 1024 * 1024
EXPERT_VMEM_LIMIT_BYTES = 56 * 1024 * 1024

ROW_TILE = 512
SEQ_TILE = 256
Q_TILE = 128
MOE_TILE = 512
COMBINE_TILE = 256

N_CMP_PAD = 128
N_SLC = SEQ // SLC_BLOCK


def _params(sem=None):
    return pltpu.CompilerParams(dimension_semantics=sem, vmem_limit_bytes=VMEM_LIMIT_BYTES)


def _mm(a, b):
    return jnp.dot(a.astype(BF16), b.astype(BF16), preferred_element_type=F32)


def _mm_nt(a, b):
    return lax.dot_general(a.astype(BF16), b.astype(BF16), (((1,), (1,)), ((), ())),
                           preferred_element_type=F32)


def _mm_tn(a, b):
    return lax.dot_general(a.astype(BF16), b.astype(BF16), (((0,), (0,)), ((), ())),
                           preferred_element_type=F32)


def _split3(x):
    hi = x.astype(BF16)
    r1 = x - hi.astype(F32)
    mid = r1.astype(BF16)
    lo = (r1 - mid.astype(F32)).astype(BF16)
    return hi, mid, lo


def _dot01_right(x, m01):
    hi, mid, lo = _split3(x)
    d = lambda p: jnp.dot(p, m01, preferred_element_type=F32)
    return d(hi) + d(mid) + d(lo)


def _dot01_left(m01, x):
    hi, mid, lo = _split3(x)
    d = lambda p: jnp.dot(m01, p, preferred_element_type=F32)
    return d(hi) + d(mid) + d(lo)


def _dot_f32(a, b):
    a_hi = a.astype(BF16)
    a_lo = (a - a_hi.astype(F32)).astype(BF16)
    b_hi = b.astype(BF16)
    b_lo = (b - b_hi.astype(F32)).astype(BF16)
    d = lambda p, q: jnp.dot(p, q, preferred_element_type=F32)
    return d(a_hi, b_hi) + d(a_hi, b_lo) + d(a_lo, b_hi) + d(a_lo, b_lo)


def _sigmoid(x):
    return 1.0 / (1.0 + jnp.exp(-x))


def _silu(x):
    return x * _sigmoid(x)


def _softplus(x):
    return jnp.maximum(x, 0.0) + jnp.log(1.0 + jnp.exp(-jnp.abs(x)))


def _bcast_col(x, j, width):
    return jnp.broadcast_to(x[:, j:j + 1], (x.shape[0], width))


IN_SEGS = (3 * DN_WIDTH, DN_WIDTH, NSA_WIDTH, 6 * NSA_HEAD_DIM, 2 * CONV_WIDTH, LANES)
IN_COLS = sum(IN_SEGS)


def _in_proj_kernel(x_ref, nw_ref, w_ref, wabt_ref,
                    qkv_ref, z_ref, nq_ref, nkv_ref, cu_ref, small_ref, abt_ref):
    xf = x_ref[...]
    ms = jnp.mean(xf * xf, axis=-1, keepdims=True)
    hb = (xf * lax.rsqrt(ms + EPS) * nw_ref[...]).astype(BF16)
    off = 0
    for ref, width in zip((qkv_ref, z_ref, nq_ref, nkv_ref, cu_ref, small_ref), IN_SEGS):
        ref[...] = jnp.dot(hb, w_ref[:, off:off + width], preferred_element_type=F32)
        off += width
    abt_ref[...] = lax.dot_general(wabt_ref[...], hb, (((1,), (1,)), ((), ())),
                                   preferred_element_type=F32)


def _in_proj(x2, norm_w, w_cat, w_abt):
    n = x2.shape[0]
    tm = ROW_TILE
    out_shape = [jax.ShapeDtypeStruct((n, w), F32) for w in IN_SEGS]
    out_shape.append(jax.ShapeDtypeStruct((16, n), F32))
    out_specs = [pl.BlockSpec((tm, w), lambda i: (i, 0)) for w in IN_SEGS]
    out_specs.append(pl.BlockSpec((16, tm), lambda i: (0, i)))
    return pl.pallas_call(
        _in_proj_kernel,
        out_shape=out_shape,
        grid=(n // tm,),
        in_specs=[
            pl.BlockSpec((tm, D_MODEL), lambda i: (i, 0)),
            pl.BlockSpec((1, D_MODEL), lambda i: (0, 0)),
            pl.BlockSpec((D_MODEL, IN_COLS), lambda i: (0, 0)),
            pl.BlockSpec((16, D_MODEL), lambda i: (0, 0)),
        ],
        out_specs=out_specs,
        compiler_params=_params(("parallel",)),
        name="in_proj",
    )(x2, norm_w, w_cat, w_abt)


def _dn_prep_kernel(qkv_ref, small_ref, abt_ref, cw_ref, pcol_ref, prow_ref,
                    q_ref, k_ref, v_ref, gb_ref, grow_ref, buf):
    t = pl.program_id(1)
    ts = SEQ_TILE

    @pl.when(t == 0)
    def _():
        buf[0:8, :] = jnp.zeros((8, 3 * DN_WIDTH), F32)

    buf[8:8 + ts, :] = qkv_ref[0]
    for c in range(3 * DN_HEADS):
        cs = slice(c * LANES, (c + 1) * LANES)
        acc = cw_ref[0:1, cs] * buf[5:5 + ts, cs]
        for j in range(1, DN_CONV):
            acc = acc + cw_ref[j:j + 1, cs] * buf[5 + j:5 + j + ts, cs]
        y = _silu(acc)
        if c < 2 * DN_HEADS:
            y = y * lax.rsqrt(jnp.sum(y * y, axis=-1, keepdims=True) + EPS)
        if c < DN_HEADS:
            q_ref[0, :, cs] = y * DN_HEAD_DIM ** -0.5
        elif c < 2 * DN_HEADS:
            k_ref[0, :, (c - DN_HEADS) * LANES:(c - DN_HEADS + 1) * LANES] = y
        else:
            v_ref[0, :, (c - 2 * DN_HEADS) * LANES:(c - 2 * DN_HEADS + 1) * LANES] = y
    buf[0:8, :] = buf[ts:ts + 8, :]

    sm = small_ref[0]
    lane = lax.broadcasted_iota(jnp.int32, sm.shape, 1)
    g_col = -jnp.exp(pcol_ref[0:1, :]) * _softplus(sm + pcol_ref[1:2, :])
    gb_ref[0] = jnp.where(lane < DN_HEADS, g_col, _sigmoid(sm))
    a_t = abt_ref[...]
    g_row = -jnp.exp(prow_ref[:, 0:1]) * _softplus(a_t + prow_ref[:, 1:2])
    grow_ref[...] = g_row[0:8, :]


def _dn_prep(qkv, small, abt, conv_w, pcol, prow):
    b = qkv.shape[0]
    ts = SEQ_TILE
    nt = SEQ // ts
    seq_spec = lambda w: pl.BlockSpec((1, ts, w), lambda i, t: (i, t, 0))
    full = lambda shape: pl.BlockSpec(shape, lambda i, t: (0,) * len(shape))
    return pl.pallas_call(
        _dn_prep_kernel,
        out_shape=[jax.ShapeDtypeStruct((b, SEQ, DN_WIDTH), F32)] * 3
        + [jax.ShapeDtypeStruct((b, SEQ, LANES), F32), jax.ShapeDtypeStruct((8, b * SEQ), F32)],
        grid=(b, nt),
        in_specs=[
            seq_spec(3 * DN_WIDTH), seq_spec(LANES),
            pl.BlockSpec((16, ts), lambda i, t: (0, i * nt + t)),
            full((DN_CONV, 3 * DN_WIDTH)), full((8, LANES)), full((16, LANES)),
        ],
        out_specs=[seq_spec(DN_WIDTH)] * 3
        + [seq_spec(LANES), pl.BlockSpec((8, ts), lambda i, t: (0, i * nt + t))],
        scratch_shapes=[pltpu.VMEM((ts + 8, 3 * DN_WIDTH), F32)],
        compiler_params=_params(("parallel", "arbitrary")),
        name="dn_prep",
    )(qkv, small, abt, conv_w, pcol, prow)


def _delta_prepare(q_ref, k_ref, v_ref, gb_ref, grow_ref, wy):
    u_ref, w_ref, attn_ref, qg_ref, kk_ref, gc_ref = wy
    ts = SEQ_TILE
    ch = DN_CHUNK
    hd = DN_HEAD_DIM
    r = lax.broadcasted_iota(jnp.int32, (ts, ts), 0)
    c = lax.broadcasted_iota(jnp.int32, (ts, ts), 1)
    same_chunk = (r // ch) == (c // ch)
    tril = same_chunk & (r >= c)
    strict = same_chunk & (r > c)
    same16 = (r // 16) == (c // 16)
    eye = jnp.where(r == c, 1.0, 0.0).astype(F32)
    m_col = jnp.where(tril, 1.0, 0.0).astype(BF16)
    m_row = jnp.where(same_chunk & (r <= c), 1.0, 0.0).astype(BF16)

    gb = gb_ref[0]
    gc_col = _dot01_left(m_col, gb)
    gc_row = _dot01_right(grow_ref[...], m_row)

    heads = range(DN_HEADS)
    hsl = [slice(h * hd, (h + 1) * hd) for h in heads]
    kh = [k_ref[0, :, hsl[h]] for h in heads]
    gcb = [_bcast_col(gc_col, h, ts) for h in heads]
    decay = [jnp.where(tril, jnp.exp(jnp.where(
        tril, gcb[h] - jnp.broadcast_to(gc_row[h:h + 1, :], (ts, ts)), 0.0)), 0.0) for h in heads]
    beta = [_bcast_col(gb, DN_HEADS + h, hd) for h in heads]
    kb = [kh[h] * beta[h] for h in heads]
    yield
    a_mat = [jnp.where(strict, _mm_nt(kb[h], kh[h]) * decay[h], 0.0) for h in heads]
    yield
    d_mat = [jnp.where(same16, a_mat[h], 0.0) for h in heads]
    e_mat = [a_mat[h] - d_mat[h] for h in heads]
    d2 = [_mm(d_mat[h], d_mat[h]) for h in heads]
    yield
    d4 = [_mm(d2[h], d2[h]) for h in heads]
    yield
    t1 = [_mm(eye - d_mat[h], eye + d2[h]) for h in heads]
    yield
    d8 = [_mm(d4[h], d4[h]) for h in heads]
    yield
    t2 = [_mm(t1[h], eye + d4[h]) for h in heads]
    yield
    p_mat = [_mm(t2[h], eye + d8[h]) for h in heads]
    yield
    m_mat = [_mm(p_mat[h], e_mat[h]) for h in heads]
    yield
    m2 = [_mm(m_mat[h], m_mat[h]) for h in heads]
    yield
    t3 = [_mm(eye - m_mat[h], eye + m2[h]) for h in heads]
    yield
    t_mat = [_mm(t3[h], p_mat[h]) for h in heads]
    yield
    for h in heads:
        gc128 = gcb[h][:, :hd]
        expg = jnp.exp(gc128)
        qh = q_ref[0, :, hsl[h]]
        sol = _mm(t_mat[h], jnp.concatenate([v_ref[0, :, hsl[h]] * beta[h], kb[h] * expg], axis=1))
        u_ref[h] = sol[:, :hd]
        w_ref[h] = sol[:, hd:].astype(BF16)
        qg_ref[h] = (qh * expg).astype(BF16)
        kk_ref[h] = kh[h]
        gc_ref[h] = gc128
    yield
    for h in heads:
        attn_ref[h] = jnp.where(tril, _mm_nt(q_ref[0, :, hsl[h]], kh[h]) * decay[h], 0.0).astype(BF16)


def _delta_recur(wy, z_ref, nw_ref, o_ref, s_ref):
    u_ref, w_ref, attn_ref, qg_ref, kk_ref, gc_ref = wy
    ch = DN_CHUNK
    hd = DN_HEAD_DIM
    heads = range(DN_HEADS)
    state = [s_ref[h] for h in heads]
    outs = [[] for _ in heads]
    for ci in range(SEQ_TILE // ch):
        rs = slice(ci * ch, (ci + 1) * ch)
        v_new = [u_ref[h, rs, :] - _mm(w_ref[h, rs, :], state[h]) for h in heads]
        o_state = [_mm(qg_ref[h, rs, :], state[h]) for h in heads]
        yield
        for h in heads:
            gc = gc_ref[h, rs, :]
            g_last = gc[ch - 1:ch, :]
            k_dec = kk_ref[h, rs, :] * jnp.exp(g_last - gc)
            outs[h].append(o_state[h] + _mm(attn_ref[h, rs, ci * ch:(ci + 1) * ch], v_new[h]))
            state[h] = state[h] * jnp.exp(g_last) + _mm_tn(k_dec, v_new[h])
        yield

    for h in heads:
        s_ref[h] = state[h]
        hs = slice(h * hd, (h + 1) * hd)
        o = jnp.concatenate(outs[h], axis=0)
        o = o * lax.rsqrt(jnp.mean(o * o, axis=-1, keepdims=True) + EPS) * nw_ref[...]
        o_ref[0, :, hs] = o * _silu(z_ref[0, :, hs])


def _interleave(first, second, ratio):
    live = [first, second]
    while live:
        for gen, n in ((first, ratio), (second, 1)):
            for _ in range(n):
                if gen in live and next(gen, live) is live:
                    live.remove(gen)


def _delta_kernel(q_ref, k_ref, v_ref, gb_ref, grow_ref, z_ref, nw_ref, o_ref, s_ref, *wy_refs):
    t = pl.program_id(1)
    set_a, set_b = wy_refs[:6], wy_refs[6:]

    @pl.when(t == 0)
    def _():
        s_ref[...] = jnp.zeros(s_ref.shape, F32)
        for ref in set_b:
            ref[...] = jnp.zeros(ref.shape, ref.dtype)

    def step(read_set, write_set):
        _interleave(_delta_prepare(q_ref, k_ref, v_ref, gb_ref, grow_ref, write_set),
                    _delta_recur(read_set, z_ref, nw_ref, o_ref, s_ref), 2)

    @pl.when(lax.rem(t, 2) == 0)
    def _():
        step(set_b, set_a)

    @pl.when(lax.rem(t, 2) == 1)
    def _():
        step(set_a, set_b)


def _delta_rule(q, k, v, gb, grow, z, norm_w):
    b = q.shape[0]
    ts = SEQ_TILE
    nt = SEQ // ts
    nxt = lambda w: pl.BlockSpec((1, ts, w), lambda i, t: (i, jnp.minimum(t, nt - 1), 0))
    cur = lambda w: pl.BlockSpec((1, ts, w), lambda i, t: (i, jnp.maximum(t - 1, 0), 0))
    hd = DN_HEAD_DIM
    wy_set = [pltpu.VMEM((DN_HEADS, ts, hd), F32), pltpu.VMEM((DN_HEADS, ts, hd), BF16),
              pltpu.VMEM((DN_HEADS, ts, ts), BF16), pltpu.VMEM((DN_HEADS, ts, hd), BF16),
              pltpu.VMEM((DN_HEADS, ts, hd), F32), pltpu.VMEM((DN_HEADS, ts, hd), F32)]
    return pl.pallas_call(
        _delta_kernel,
        out_shape=jax.ShapeDtypeStruct((b, SEQ, DN_WIDTH), F32),
        grid=(b, nt + 1),
        in_specs=[
            nxt(DN_WIDTH), nxt(DN_WIDTH), nxt(DN_WIDTH), nxt(LANES),
            pl.BlockSpec((8, ts), lambda i, t: (0, i * nt + jnp.minimum(t, nt - 1))),
            cur(DN_WIDTH),
            pl.BlockSpec((1, DN_HEAD_DIM), lambda i, t: (0, 0)),
        ],
        out_specs=cur(DN_WIDTH),
        scratch_shapes=[pltpu.VMEM((DN_HEADS, hd, hd), F32)] + wy_set + wy_set,
        compiler_params=_params(("parallel", "arbitrary")),
        name="delta_rule",
    )(q, k, v, gb, grow, z, norm_w)


CONV_HALO = 32


def _conformer_kernel(u_ref, w_ref, b_ref, lnw_ref, lnb_ref, o_ref, buf):
    t = pl.program_id(1)
    ts = SEQ_TILE

    @pl.when(t == 0)
    def _():
        buf[0:CONV_HALO, :] = jnp.zeros((CONV_HALO, CONV_WIDTH), F32)

    u = u_ref[0]
    buf[CONV_HALO:CONV_HALO + ts, :] = u[:, :CONV_WIDTH] * _sigmoid(u[:, CONV_WIDTH:])
    base = CONV_HALO - (CONV_KERNEL - 1)
    rows = 64
    for rc in range(ts // rows):
        parts = []
        for cc in range(CONV_WIDTH // LANES):
            cs = slice(cc * LANES, (cc + 1) * LANES)
            r0 = base + rc * rows
            acc = w_ref[0:1, cs] * buf[r0:r0 + rows, cs]
            for j in range(1, CONV_KERNEL):
                acc = acc + w_ref[j:j + 1, cs] * buf[r0 + j:r0 + j + rows, cs]
            parts.append(acc)
        h = jnp.concatenate(parts, axis=1) + b_ref[...]
        mu = jnp.mean(h, axis=-1, keepdims=True)
        var = jnp.mean(jnp.square(h - mu), axis=-1, keepdims=True)
        hn = (h - mu) * lax.rsqrt(var + EPS) * lnw_ref[...] + lnb_ref[...]
        o_ref[0, rc * rows:(rc + 1) * rows, :] = _silu(hn)
    buf[0:CONV_HALO, :] = buf[ts:ts + CONV_HALO, :]


def _conformer(u, dw_w, dw_b, ln_w, ln_b):
    b = u.shape[0]
    ts = SEQ_TILE
    full = lambda shape: pl.BlockSpec(shape, lambda i, t: (0,) * len(shape))
    return pl.pallas_call(
        _conformer_kernel,
        out_shape=jax.ShapeDtypeStruct((b, SEQ, CONV_WIDTH), F32),
        grid=(b, SEQ // ts),
        in_specs=[
            pl.BlockSpec((1, ts, 2 * CONV_WIDTH), lambda i, t: (i, t, 0)),
            full((CONV_KERNEL, CONV_WIDTH)), full((1, CONV_WIDTH)),
            full((1, CONV_WIDTH)), full((1, CONV_WIDTH)),
        ],
        out_specs=pl.BlockSpec((1, ts, CONV_WIDTH), lambda i, t: (i, t, 0)),
        scratch_shapes=[pltpu.VMEM((ts + CONV_HALO, CONV_WIDTH), F32)],
        compiler_params=_params(("parallel", "arbitrary")),
        name="conformer",
    )(u, dw_w, dw_b, ln_w, ln_b)


def _rms_rows(x, w):
    return x * lax.rsqrt(jnp.mean(x * x, axis=-1, keepdims=True) + EPS) * w


def _nsa_prep_kernel(kc_ref, vc_ref, ks_ref, kw_ref, pos_ref, w1_ref, w2_ref, knw_ref,
                     kcmp_ref, vcmp_ref, ksn_ref, kwn_ref):
    half = CMP_STRIDE * NSA_HEAD_DIM

    def compress(x, i):
        u_lo = _mm(x + pos_ref[2 * i:2 * i + 1, :], w1_ref[i, 0:half, :])
        u_hi = _mm(x + pos_ref[2 * i + 1:2 * i + 2, :], w1_ref[i, half:2 * half, :])
        hid = _silu(u_lo + pltpu.roll(u_hi, N_CMP_PAD - 1, axis=0))
        return _mm(hid, w2_ref[i])

    kcmp_ref[0] = _rms_rows(compress(kc_ref[0], 0), knw_ref[0:1, :])
    vcmp_ref[0] = compress(vc_ref[0], 1)
    ksn_ref[0] = _rms_rows(ks_ref[0], knw_ref[1:2, :])
    kwn_ref[0] = _rms_rows(kw_ref[0], knw_ref[2:3, :])


def _nsa_prep(kc, vc, ks, kw, pos, w1, w2, knw):
    b = kc.shape[0]
    hd = NSA_HEAD_DIM
    flat = CMP_STRIDE * hd
    full = lambda shape: pl.BlockSpec(shape, lambda i: (0,) * len(shape))
    bspec = lambda r, w: pl.BlockSpec((1, r, w), lambda i: (i, 0, 0))
    return pl.pallas_call(
        _nsa_prep_kernel,
        out_shape=[jax.ShapeDtypeStruct((b, N_CMP_PAD, hd), F32)] * 2
        + [jax.ShapeDtypeStruct((b, SEQ, hd), F32)] * 2,
        grid=(b,),
        in_specs=[
            bspec(N_CMP_PAD, flat), bspec(N_CMP_PAD, flat), bspec(SEQ, hd), bspec(SEQ, hd),
            full((4, flat)), full((2, 2 * flat, CMP_HIDDEN)), full((2, CMP_HIDDEN, hd)), full((3, hd)),
        ],
        out_specs=[bspec(N_CMP_PAD, hd)] * 2 + [bspec(SEQ, hd)] * 2,
        compiler_params=_params(("parallel",)),
        name="nsa_prep",
    )(kc, vc, ks, kw, pos, w1, w2, knw)


def _nsa_attn_kernel(q_ref, qnw_ref, kcmp_ref, vcmp_ref, ks_ref, vs_ref, kw_ref, vw_ref,
                     gate_ref, bcmp_ref, btab_ref, o_ref,
                     madd_ref, s_ref, ksb_ref, kwb_ref, vsa_ref, vwa_ref):
    i = pl.program_id(1)
    tq = Q_TILE
    nh = NSA_HEADS
    hd = NSA_HEAD_DIM
    t0 = i * tq

    @pl.when(i == 0)
    def _():
        ones = jnp.ones((SEQ, hd), BF16)
        ksb_ref[...] = ks_ref[0].astype(BF16)
        kwb_ref[...] = kw_ref[0].astype(BF16)
        vsa_ref[...] = jnp.concatenate([vs_ref[0].astype(BF16), ones], axis=1)
        vwa_ref[...] = jnp.concatenate([vw_ref[0].astype(BF16), ones], axis=1)

    qs = jnp.concatenate(
        [_rms_rows(q_ref[0, h], qnw_ref[...]) * hd ** -0.5 for h in range(nh)], axis=0).astype(BF16)

    row = lax.broadcasted_iota(jnp.int32, (tq, LANES), 0)
    lane = lax.broadcasted_iota(jnp.int32, (tq, LANES), 1)
    qpos = t0 + row

    s_all = _mm_nt(qs, kcmp_ref[0])
    cmp_valid = (qpos >= lane * CMP_STRIDE + (CMP_LEN - 1)) & (lane < N_CMP_PAD - 1)
    p_rows = []
    p_sum = jnp.zeros((tq, LANES), F32)
    for h in range(nh):
        s = jnp.where(cmp_valid, s_all[h * tq:(h + 1) * tq] + bcmp_ref[h], NEG_INF)
        m = jnp.max(s, axis=-1, keepdims=True)
        p = jnp.where(cmp_valid, jnp.exp(s - m), 0.0)
        l = jnp.sum(p, axis=-1, keepdims=True)
        p = p * jnp.where(l > 0.0, 1.0 / l, 0.0)
        p_rows.append(p)
        p_sum = p_sum + p
    o_cmp = _mm(jnp.concatenate(p_rows, axis=0), vcmp_ref[0])

    ss = lax.broadcasted_iota(jnp.int32, (N_SLC, N_CMP_PAD), 0)
    jj = lax.broadcasted_iota(jnp.int32, (N_SLC, N_CMP_PAD), 1)
    overlap_t = ((jj * CMP_STRIDE < ss * SLC_BLOCK + SLC_BLOCK)
                 & (jj * CMP_STRIDE + CMP_LEN > ss * SLC_BLOCK) & (jj < N_CMP_PAD - 1))
    overlap_t = jnp.where(overlap_t, 1.0, 0.0).astype(BF16)
    nt = lambda a, b_: lax.dot_general(a, b_, (((1,), (1,)), ((), ())), preferred_element_type=F32)
    p_hi, p_mid, p_lo = _split3(p_sum)
    imp = nt(overlap_t, p_hi) + nt(overlap_t, p_mid) + nt(overlap_t, p_lo)
    blk = lax.broadcasted_iota(jnp.int32, (N_SLC, tq), 0)
    cur = (t0 + lax.broadcasted_iota(jnp.int32, (N_SLC, tq), 1)) // SLC_BLOCK
    causal_blk = blk <= cur
    forced = (blk == 0) | (blk == cur) | (blk == cur - 1)
    imp = jnp.where(causal_blk & forced, FORCE, jnp.where(causal_blk, imp, -1.0))
    rank = jnp.zeros((N_SLC, tq), jnp.int32)
    for s2 in range(N_SLC):
        other = jnp.broadcast_to(imp[s2:s2 + 1, :], (N_SLC, tq))
        beats = (other > imp) | ((other == imp) & (blk > s2))
        rank = rank + jnp.where(beats, 1, 0)
    sel_t = jnp.where(rank < SLC_TOP_N, 1.0, 0.0).astype(BF16)
    er = lax.broadcasted_iota(jnp.int32, (N_SLC, SEQ), 0)
    ec = lax.broadcasted_iota(jnp.int32, (N_SLC, SEQ), 1)
    expand = jnp.where(ec // SLC_BLOCK == er, 1.0, 0.0).astype(BF16)
    key = lax.broadcasted_iota(jnp.int32, (tq, SEQ), 1)
    qall = t0 + lax.broadcasted_iota(jnp.int32, (tq, SEQ), 0)
    chosen = (_mm_tn(sel_t, expand) > 0.5) & (qall >= key)
    madd_ref[...] = jnp.where(chosen, 0.0, NEG_INF)

    neg = jnp.full((nh * tq, tq), NEG_INF, F32)
    zero = jnp.zeros((nh * tq, LANES), F32)

    def scores(k_tile, bias_idx, add):
        s_t = _mm_nt(qs, k_tile)
        return jnp.concatenate(
            [s_t[h * tq:(h + 1) * tq] + (btab_ref[h, bias_idx] + add) for h in range(nh)], axis=0)

    def normalise(acc):
        return acc[:, :hd] * (1.0 / _bcast_col(acc, hd, hd))

    n_pairs = (i + 2) // 2

    def slc_pass1(j, m_run):
        for u in range(2):
            kt = 2 * j + u
            k0 = pl.multiple_of(kt * tq, tq)
            s_m = scores(ksb_ref[pl.ds(k0, tq), :], jnp.clip(i - kt, 0, 2), madd_ref[:, pl.ds(k0, tq)])
            s_ref[:, pl.ds(k0, tq)] = s_m
            m_run = jnp.maximum(m_run, s_m)
        return m_run

    m_slc = jnp.broadcast_to(
        jnp.max(lax.fori_loop(0, n_pairs, slc_pass1, neg), axis=-1, keepdims=True), (nh * tq, tq))

    def slc_pass2(j, acc):
        for u in range(2):
            k0 = pl.multiple_of((2 * j + u) * tq, tq)
            p = jnp.exp(s_ref[:, pl.ds(k0, tq)] - m_slc).astype(BF16)
            acc = acc + jnp.dot(p, vsa_ref[pl.ds(k0, tq), :], preferred_element_type=F32)
        return acc

    o_slc = normalise(lax.fori_loop(0, n_pairs, slc_pass2, zero))

    kcol = lax.broadcasted_iota(jnp.int32, (tq, tq), 1)
    qrow = t0 + lax.broadcasted_iota(jnp.int32, (tq, tq), 0)
    n_win = WINDOW // tq + 1
    starts = []
    m_run = neg
    for j in range(n_win):
        d = n_win - 1 - j
        k0 = pl.multiple_of(jnp.maximum(i - d, 0) * tq, tq)
        starts.append(k0)
        dist = qrow - (k0 + kcol)
        ok = (dist >= 0) & (dist < WINDOW) & (i >= d)
        s_m = scores(kwb_ref[pl.ds(k0, tq), :], min(d, 2), jnp.where(ok, 0.0, NEG_INF))
        s_ref[:, j * tq:(j + 1) * tq] = s_m
        m_run = jnp.maximum(m_run, s_m)
    m_win = jnp.broadcast_to(jnp.max(m_run, axis=-1, keepdims=True), (nh * tq, tq))
    acc = zero
    for j in range(n_win):
        p = jnp.exp(s_ref[:, j * tq:(j + 1) * tq] - m_win).astype(BF16)
        acc = acc + jnp.dot(p, vwa_ref[pl.ds(starts[j], tq), :], preferred_element_type=F32)
    o_win = normalise(acc)

    gates = _sigmoid(gate_ref[0])
    for h in range(nh):
        hs = slice(h * tq, (h + 1) * tq)
        g0 = 2 * DN_HEADS + 3 * h
        o_ref[0, h] = (_bcast_col(gates, g0, hd) * o_cmp[hs]
                       + _bcast_col(gates, g0 + 1, hd) * o_slc[hs]
                       + _bcast_col(gates, g0 + 2, hd) * o_win[hs])


def _nsa_attn(q, qnw, kcmp, vcmp, ks, vs, kw, vw, small, bias_cmp, btab):
    b = q.shape[0]
    tq = Q_TILE
    nh, hd = NSA_HEADS, NSA_HEAD_DIM
    full = lambda shape: pl.BlockSpec(shape, lambda bi, i: (0,) * len(shape))
    per_b = lambda r: pl.BlockSpec((1, r, hd), lambda bi, i: (bi, 0, 0))
    return pl.pallas_call(
        _nsa_attn_kernel,
        out_shape=jax.ShapeDtypeStruct((b, nh, SEQ, hd), F32),
        grid=(b, SEQ // tq),
        in_specs=[
            pl.BlockSpec((1, nh, tq, hd), lambda bi, i: (bi, 0, i, 0)),
            full((1, hd)),
            per_b(N_CMP_PAD), per_b(N_CMP_PAD), per_b(SEQ), per_b(SEQ), per_b(SEQ), per_b(SEQ),
            pl.BlockSpec((1, tq, LANES), lambda bi, i: (bi, i, 0)),
            pl.BlockSpec((nh, tq, N_CMP_PAD), lambda bi, i: (0, i, 0)),
            full((nh, 3, tq, tq)),
        ],
        out_specs=pl.BlockSpec((1, nh, tq, hd), lambda bi, i: (bi, 0, i, 0)),
        scratch_shapes=[pltpu.VMEM((tq, SEQ), F32), pltpu.VMEM((nh * tq, SEQ), F32),
                        pltpu.VMEM((SEQ, hd), BF16), pltpu.VMEM((SEQ, hd), BF16),
                        pltpu.VMEM((SEQ, 2 * hd), BF16), pltpu.VMEM((SEQ, 2 * hd), BF16)],
        compiler_params=_params(("parallel", "arbitrary")),
        name="nsa_attn",
    )(q, qnw, kcmp, vcmp, ks, vs, kw, vw, small, bias_cmp, btab)


def _out_router_kernel(ya_ref, yb_ref, yc_ref, x_ref, wo_ref, fnw_ref, rw_ref, rb_ref,
                       xo_ref, h_ref, idx_ref, wt_ref, hist_ref):
    wa = DN_WIDTH
    wb = wa + NSA_WIDTH
    y = (jnp.dot(ya_ref[...].astype(BF16), wo_ref[0:wa, :], preferred_element_type=F32)
         + jnp.dot(yb_ref[...].astype(BF16), wo_ref[wa:wb, :], preferred_element_type=F32)
         + jnp.dot(yc_ref[...].astype(BF16), wo_ref[wb:, :], preferred_element_type=F32))
    xn = x_ref[...] + y
    xo_ref[...] = xn
    h = xn * lax.rsqrt(jnp.mean(xn * xn, axis=-1, keepdims=True) + EPS) * fnw_ref[...]
    h_ref[...] = h
    logits = _dot_f32(h, rw_ref[...]) + rb_ref[...]
    lane = lax.broadcasted_iota(jnp.int32, logits.shape, 1)
    vals, idxs = [], []
    for _ in range(TOP_K):
        m = jnp.max(logits, axis=-1, keepdims=True)
        ix = jnp.min(jnp.where(logits == m, lane, LANES), axis=-1, keepdims=True)
        vals.append(m)
        idxs.append(ix)
        logits = jnp.where(lane == ix, -jnp.inf, logits)
    es = [jnp.exp(v - vals[0]) for v in vals]
    inv = 1.0 / (es[0] + es[1] + es[2] + es[3])
    idx_out = jnp.zeros(lane.shape, jnp.int32)
    wt_out = jnp.zeros(lane.shape, F32)
    chosen = jnp.zeros(lane.shape, F32)
    for k in range(TOP_K):
        idx_out = jnp.where(lane == k, idxs[k], idx_out)
        wt_out = jnp.where(lane == k, es[k] * inv, wt_out)
        chosen = chosen + jnp.where(lane == idxs[k], 1.0, 0.0)
    idx_ref[...] = idx_out
    wt_ref[...] = wt_out
    hist_ref[0] = jnp.broadcast_to(jnp.sum(chosen, axis=0, keepdims=True), (8, LANES))


def _out_router(ya, yb, yc, x2, w_out, fnw, rw, rb):
    n = x2.shape[0]
    tm = ROW_TILE
    row = lambda w: pl.BlockSpec((tm, w), lambda i: (i, 0))
    full = lambda shape: pl.BlockSpec(shape, lambda i: (0,) * len(shape))
    return pl.pallas_call(
        _out_router_kernel,
        out_shape=[jax.ShapeDtypeStruct((n, D_MODEL), F32), jax.ShapeDtypeStruct((n, D_MODEL), F32),
                   jax.ShapeDtypeStruct((n, LANES), jnp.int32), jax.ShapeDtypeStruct((n, LANES), F32),
                   jax.ShapeDtypeStruct((n // tm, 8, LANES), F32)],
        grid=(n // tm,),
        in_specs=[row(DN_WIDTH), row(NSA_WIDTH), row(CONV_WIDTH), row(D_MODEL),
                  full((D_MODEL, D_MODEL)), full((1, D_MODEL)), full((D_MODEL, LANES)), full((1, LANES))],
        out_specs=[row(D_MODEL), row(D_MODEL), row(LANES), row(LANES),
                   pl.BlockSpec((1, 8, LANES), lambda i: (i, 0, 0))],
        compiler_params=_params(("parallel",)),
        name="out_router",
    )(ya, yb, yc, x2, w_out, fnw, rw, rb)


def _slots_kernel(idx_ref, base_ref, slot_ref):
    tm = ROW_TILE
    idx = idx_ref[...]
    lane = lax.broadcasted_iota(jnp.int32, (tm, LANES), 1)
    onehots = [jnp.where(lane == _bcast_col(idx, k, LANES), 1.0, 0.0) for k in range(TOP_K)]
    cnt = (onehots[0] + onehots[1]) + (onehots[2] + onehots[3])
    r = lax.broadcasted_iota(jnp.int32, (tm, tm), 0)
    c = lax.broadcasted_iota(jnp.int32, (tm, tm), 1)
    earlier = jnp.where(r > c, 1.0, 0.0).astype(BF16)
    rank = jnp.dot(earlier, cnt.astype(BF16), preferred_element_type=F32) + base_ref[0, 0:1, :]
    ones = jnp.ones((8, LANES), BF16)
    row8 = lax.broadcasted_iota(jnp.int32, (8, tm), 0)
    out = jnp.zeros((8, tm), F32)
    for k in range(TOP_K):
        hi, mid, lo = _split3(rank * onehots[k])
        d = lambda p: lax.dot_general(ones, p, (((1,), (1,)), ((), ())), preferred_element_type=F32)
        out = jnp.where(row8 == k, d(hi) + d(mid) + d(lo), out)
    slot_ref[...] = out.astype(jnp.int32)


def _slots(idx, base3):
    n = idx.shape[0]
    tm = ROW_TILE
    return pl.pallas_call(
        _slots_kernel,
        out_shape=jax.ShapeDtypeStruct((8, n), jnp.int32),
        grid=(n // tm,),
        in_specs=[pl.BlockSpec((tm, LANES), lambda i: (i, 0)),
                  pl.BlockSpec((1, 8, LANES), lambda i: (i, 0, 0))],
        out_specs=pl.BlockSpec((8, tm), lambda i: (0, i)),
        compiler_params=_params(("parallel",)),
        name="moe_slots",
    )(idx, base3)


def _dispatch_kernel(slot_ref, zoff_ref, nv_ref, h_ref, xs_hbm, zbuf, sem, zsem):
    i = pl.program_id(0)
    tm = ROW_TILE
    n = slot_ref.shape[0] // TOP_K
    base = i * tm

    @pl.when(i == 0)
    def _():
        zbuf[...] = jnp.zeros(zbuf.shape, F32)
        n_blocks = xs_hbm.shape[0] // MOE_TILE

        def fill(row0):
            return pltpu.make_async_copy(zbuf, xs_hbm.at[pl.ds(pl.multiple_of(row0, MOE_TILE), MOE_TILE)], zsem)

        def fill_region_end(e, c):
            fill(zoff_ref[e]).start()
            return c

        def fill_tail(b, c):
            fill(b * MOE_TILE).start()
            return c

        def wait_fill(b, c):
            fill(0).wait()
            return c

        lax.fori_loop(0, N_EXPERTS, fill_region_end, 0)
        lax.fori_loop(nv_ref[0], n_blocks, fill_tail, 0)
        lax.fori_loop(0, N_EXPERTS + n_blocks - nv_ref[0], wait_fill, 0)

    def body(r, c):
        for k in range(TOP_K):
            pltpu.make_async_copy(h_ref.at[pl.ds(r, 1)],
                                  xs_hbm.at[pl.ds(slot_ref[k * n + base + r], 1)], sem).start(priority=k % 2)
        return c

    lax.fori_loop(0, tm, body, 0, unroll=4)
    done = xs_hbm.at[pl.ds(0, TOP_K * tm)]
    pltpu.make_async_copy(done, done, sem).wait()


def _dispatch(slot_flat, zero_off, n_valid, h, n_slots):
    n = h.shape[0]
    tm = ROW_TILE
    return pl.pallas_call(
        _dispatch_kernel,
        out_shape=jax.ShapeDtypeStruct((n_slots, D_MODEL), F32),
        grid_spec=pltpu.PrefetchScalarGridSpec(
            num_scalar_prefetch=3,
            grid=(n // tm,),
            in_specs=[pl.BlockSpec((tm, D_MODEL), lambda i, s, z, nv: (i, 0))],
            out_specs=pl.BlockSpec(memory_space=pl.ANY),
            scratch_shapes=[pltpu.VMEM((MOE_TILE, D_MODEL), F32),
                            pltpu.SemaphoreType.DMA, pltpu.SemaphoreType.DMA],
        ),
        compiler_params=_params(("arbitrary",)),
        name="moe_dispatch",
    )(slot_flat, zero_off, n_valid, h)


def _expert_kernel(be_ref, nv_ref, xs_ref, wgu_ref, bgu_ref, wd_ref, bd_ref, o_ref, wgu_bf, wd_bf):
    i = pl.program_id(0)
    new_expert = (i == 0) | (be_ref[i] != be_ref[jnp.maximum(i - 1, 0)])

    @pl.when((i < nv_ref[0]) & new_expert)
    def _():
        step = 512
        for c0 in range(0, 2 * D_FF, step):
            wgu_bf[:, c0:c0 + step] = wgu_ref[:, c0:c0 + step].astype(BF16)
        for c0 in range(0, D_MODEL, step):
            wd_bf[:, c0:c0 + step] = wd_ref[:, c0:c0 + step].astype(BF16)

    @pl.when(i < nv_ref[0])
    def _():
        gu = jnp.dot(xs_ref[...].astype(BF16), wgu_bf[...], preferred_element_type=F32) + bgu_ref[...]
        gate = jnp.minimum(gu[:, :D_FF], SWIGLU_LIMIT)
        up = jnp.clip(gu[:, D_FF:], -SWIGLU_LIMIT, SWIGLU_LIMIT)
        act = (up + 1.0) * gate * _sigmoid(SWIGLU_ALPHA * gate)
        o_ref[...] = jnp.dot(act.astype(BF16), wd_bf[...], preferred_element_type=F32) + bd_ref[...]

    @pl.when(i >= nv_ref[0])
    def _():
        o_ref[...] = jnp.zeros(o_ref.shape, F32)


def _expert_ffn(layer, block_expert, n_valid, xs, wgu, bgu, wd, bd):
    n_slots = xs.shape[0]
    tm = MOE_TILE
    return pl.pallas_call(
        _expert_kernel,
        out_shape=jax.ShapeDtypeStruct((n_slots, D_MODEL), F32),
        grid_spec=pltpu.PrefetchScalarGridSpec(
            num_scalar_prefetch=2,
            grid=(n_slots // tm,),
            in_specs=[
                pl.BlockSpec((tm, D_MODEL), lambda i, be, nv: (i, 0)),
                pl.BlockSpec((None, None, D_MODEL, 2 * D_FF), lambda i, be, nv: (layer, be[i], 0, 0)),
                pl.BlockSpec((None, None, 1, 2 * D_FF), lambda i, be, nv: (layer, be[i], 0, 0)),
                pl.BlockSpec((None, None, D_FF, D_MODEL), lambda i, be, nv: (layer, be[i], 0, 0)),
                pl.BlockSpec((None, None, 1, D_MODEL), lambda i, be, nv: (layer, be[i], 0, 0)),
            ],
            out_specs=pl.BlockSpec((tm, D_MODEL), lambda i, be, nv: (i, 0)),
            scratch_shapes=[pltpu.VMEM((D_MODEL, 2 * D_FF), BF16), pltpu.VMEM((D_FF, D_MODEL), BF16)],
        ),
        compiler_params=pltpu.CompilerParams(dimension_semantics=("arbitrary",),
                                             vmem_limit_bytes=EXPERT_VMEM_LIMIT_BYTES),
        name="expert_ffn",
    )(block_expert, n_valid, xs, wgu, bgu, wd, bd)


def _combine_kernel(slot_ref, ys_hbm, x_ref, wt_ref, o_ref, buf, sems):
    i = pl.program_id(0)
    tc = COMBINE_TILE
    nsteps = pl.num_programs(0)
    n = slot_ref.shape[0] // TOP_K
    cur = lax.rem(i, 2)

    def issue(step, par):
        base = step * tc

        def body(r, c):
            for k in range(TOP_K):
                pltpu.make_async_copy(ys_hbm.at[pl.ds(slot_ref[k * n + base + r], 1)],
                                      buf.at[par, k, pl.ds(r, 1)], sems.at[par]).start(priority=k % 2)
            return c

        lax.fori_loop(0, tc, body, 0, unroll=4)

    @pl.when(i == 0)
    def _():
        issue(0, 0)

    @pl.when(i + 1 < nsteps)
    def _():
        issue(i + 1, 1 - cur)

    pltpu.make_async_copy(buf.at[cur], buf.at[cur], sems.at[cur]).wait()
    w = wt_ref[...]
    acc = x_ref[...]
    for k in range(TOP_K):
        acc = acc + _bcast_col(w, k, D_MODEL) * buf[cur, k]
    o_ref[...] = acc


def _combine(slot_flat, ys, x2, wt):
    n = x2.shape[0]
    tc = COMBINE_TILE
    return pl.pallas_call(
        _combine_kernel,
        out_shape=jax.ShapeDtypeStruct((n, D_MODEL), F32),
        grid_spec=pltpu.PrefetchScalarGridSpec(
            num_scalar_prefetch=1,
            grid=(n // tc,),
            in_specs=[pl.BlockSpec(memory_space=pl.ANY),
                      pl.BlockSpec((tc, D_MODEL), lambda i, s: (i, 0)),
                      pl.BlockSpec((tc, LANES), lambda i, s: (i, 0))],
            out_specs=pl.BlockSpec((tc, D_MODEL), lambda i, s: (i, 0)),
            scratch_shapes=[pltpu.VMEM((2, TOP_K, tc, D_MODEL), F32), pltpu.SemaphoreType.DMA((2,))],
        ),
        compiler_params=_params(("arbitrary",)),
        name="moe_combine",
    )(slot_flat, ys, x2, wt)


def _moe(layer, h, x2, idx, wt, hist3, wgu, bgu, wd, bd):
    n = h.shape[0]
    tm = MOE_TILE
    n_slots = n * TOP_K + N_EXPERTS * tm
    n_blocks = n_slots // tm
    hist = hist3[:, 0, :N_EXPERTS]
    counts = jnp.sum(hist, axis=0).astype(jnp.int32)
    padded = (counts + tm - 1) // tm * tm
    pad_end = jnp.cumsum(padded)
    pad_start = pad_end - padded
    tile_base = pad_start[None, :].astype(F32) + (jnp.cumsum(hist, axis=0) - hist)
    base3 = jnp.broadcast_to(jnp.pad(tile_base, ((0, 0), (0, LANES - N_EXPERTS)))[:, None, :],
                             (hist.shape[0], 8, LANES))
    blk0 = jnp.arange(n_blocks) * tm
    block_expert = jnp.minimum(jnp.sum(blk0[:, None] >= pad_end[None, :], axis=1), N_EXPERTS - 1).astype(jnp.int32)
    n_valid = (pad_end[-1:] // tm).astype(jnp.int32)
    zero_off = jnp.maximum(pad_end - tm, 0).astype(jnp.int32)

    slot_flat = _slots(idx, base3)[:TOP_K].reshape(-1)
    xs = _dispatch(slot_flat, zero_off, n_valid, h, n_slots)
    ys = _expert_ffn(layer, block_expert, n_valid, xs, wgu, bgu, wd, bd)
    return _combine(slot_flat, ys, x2, wt)


def _t5_bucket(dist):
    n = jnp.maximum(dist, 0)
    max_exact = REL_BUCKETS // 2
    nf = jnp.maximum(n, 1).astype(F32)
    large = max_exact + (jnp.log(nf / max_exact) / math.log(REL_MAX_DIST / max_exact)
                         * (REL_BUCKETS - max_exact)).astype(jnp.int32)
    large = jnp.minimum(large, REL_BUCKETS - 1)
    return jnp.where(n < max_exact, n, large)


def _bias_tables(rel_bias):
    rel_bias = rel_bias.astype(F32)
    tq = Q_TILE

    def lookup(dist):
        bucket = _t5_bucket(dist)
        out = jnp.zeros((NSA_HEADS,) + dist.shape, F32)
        for bk in range(REL_BUCKETS):
            out = jnp.where(bucket[None] == bk, rel_bias[bk].reshape((NSA_HEADS,) + (1,) * dist.ndim), out)
        return out

    t_pos = jnp.arange(SEQ)
    cmp_end = jnp.arange(N_CMP_PAD) * CMP_STRIDE + CMP_LEN - 1
    bias_cmp = lookup(t_pos[:, None] - cmp_end[None, :])
    rr = jnp.arange(tq)[:, None] - jnp.arange(tq)[None, :]
    btab = lookup(jnp.stack([rr, rr + tq, rr + 2 * tq]))
    return bias_cmp, btab


def _layer(layer, x2, b, p, experts, bias_cmp, btab):
    n = x2.shape[0]
    w = p['w_in']
    o_a, o_b, o_q, o_kv, o_g, o_u = 2048, 2052, 2056, 2312, 2696, 2708
    w_small = jnp.concatenate([w[:, o_a:o_q], w[:, o_g:o_u],
                               jnp.zeros((D_MODEL, LANES - 8 - 3 * NSA_HEADS), F32)], axis=1)
    w_cat = jnp.concatenate([w[:, :o_a], w[:, o_q:o_kv], w[:, o_kv:o_g], w[:, o_u:], w_small],
                            axis=1).astype(BF16)
    w_abt = jnp.concatenate([w[:, o_a:o_q].T, jnp.zeros((8, D_MODEL), F32)], axis=0).astype(BF16)
    qkv, z, nq, nkv, cu, small, abt = _in_proj(x2, p['attn_norm_w'][None, :], w_cat, w_abt)

    pcol = jnp.zeros((8, LANES), F32).at[0, :DN_HEADS].set(p['dn_a_log']).at[1, :DN_HEADS].set(p['dn_dt_bias'])
    prow = jnp.zeros((16, LANES), F32).at[:DN_HEADS, 0].set(p['dn_a_log']).at[:DN_HEADS, 1].set(p['dn_dt_bias'])
    small3 = small.reshape(b, SEQ, LANES)
    dq, dk, dv, gb, grow = _dn_prep(qkv.reshape(b, SEQ, 3 * DN_WIDTH), small3, abt,
                                    p['dn_conv_w'], pcol, prow)
    y_a = _delta_rule(dq, dk, dv, gb, grow, z.reshape(b, SEQ, DN_WIDTH), p['dn_norm_w'][None, :])

    y_c = _conformer(cu.reshape(b, SEQ, 2 * CONV_WIDTH), p['conv_dw_w'], p['conv_dw_b'][None, :],
                     p['conv_ln_w'][None, :], p['conv_ln_b'][None, :])

    hd = NSA_HEAD_DIM
    kv = nkv.reshape(b, SEQ, 6, hd)
    flat = CMP_STRIDE * hd
    kc = kv[:, :, 0].reshape(b, N_CMP_PAD, flat)
    vc = kv[:, :, 1].reshape(b, N_CMP_PAD, flat)
    pos = p['nsa_cmp_pos'].reshape(4, flat)
    kcmp, vcmp, ksn, kwn = _nsa_prep(kc, vc, kv[:, :, 2], kv[:, :, 4], pos,
                                     p['nsa_cmp_w1'].astype(BF16), p['nsa_cmp_w2'].astype(BF16),
                                     p['nsa_k_norm_w'])
    q4 = nq.reshape(b, SEQ, NSA_HEADS, hd).transpose(0, 2, 1, 3)
    o4 = _nsa_attn(q4, p['nsa_q_norm_w'][None, :], kcmp, vcmp, ksn, kv[:, :, 3], kwn, kv[:, :, 5],
                   small3, bias_cmp, btab)
    y_b = o4.transpose(0, 2, 1, 3).reshape(n, NSA_WIDTH)

    rw = jnp.concatenate([p['router_w'], jnp.zeros((D_MODEL, LANES - N_EXPERTS), F32)], axis=1)
    rb = jnp.concatenate([p['router_b'], jnp.full((LANES - N_EXPERTS,), NEG_INF, F32)])[None, :]
    x_new, h, idx, wt, hist3 = _out_router(y_a.reshape(n, DN_WIDTH), y_b, y_c.reshape(n, CONV_WIDTH), x2,
                                           p['w_out'].astype(BF16), p['ffn_norm_w'][None, :], rw, rb)
    return _moe(layer, h, x_new, idx, wt, hist3, *experts)


def kernel(x, attn_norm_w, w_in, dn_conv_w, dn_a_log, dn_dt_bias, dn_norm_w, nsa_q_norm_w, nsa_k_norm_w, nsa_cmp_pos, nsa_cmp_w1, nsa_cmp_w2, conv_dw_w, conv_dw_b, conv_ln_w, conv_ln_b, w_out, ffn_norm_w, router_w, router_b, w_gate_up, b_gate_up, w_down, b_down, rel_bias):
    b, t, d = x.shape
    assert (t, d) == (SEQ, D_MODEL)
    stacked = dict(attn_norm_w=attn_norm_w, w_in=w_in, dn_conv_w=dn_conv_w, dn_a_log=dn_a_log,
                   dn_dt_bias=dn_dt_bias, dn_norm_w=dn_norm_w, nsa_q_norm_w=nsa_q_norm_w,
                   nsa_k_norm_w=nsa_k_norm_w, nsa_cmp_pos=nsa_cmp_pos, nsa_cmp_w1=nsa_cmp_w1,
                   nsa_cmp_w2=nsa_cmp_w2, conv_dw_w=conv_dw_w, conv_dw_b=conv_dw_b,
                   conv_ln_w=conv_ln_w, conv_ln_b=conv_ln_b, w_out=w_out, ffn_norm_w=ffn_norm_w,
                   router_w=router_w, router_b=router_b)
    experts = (w_gate_up, b_gate_up[:, :, None, :], w_down, b_down[:, :, None, :])
    bias_cmp, btab = _bias_tables(rel_bias)
    x2 = x.reshape(b * t, d)
    for l in range(w_in.shape[0]):
        x2 = _layer(l, x2, b, {k: v[l] for k, v in stacked.items()}, experts, bias_cmp, btab)
    return x2.reshape(b, t, d)
```

```python
import functools
import math

import numpy as np
import jax
import jax.numpy as jnp
from jax import lax
from jax.experimental import pallas as pl
from jax.experimental.pallas import tpu as pltpu

F32 = jnp.float32
BF16 = jnp.bfloat16

D_MODEL = 1024
SEQ = 2048
DN_HEADS = 4
DN_HEAD_DIM = 128
DN_WIDTH = DN_HEADS * DN_HEAD_DIM
DN_CONV = 4
DN_CHUNK = 64
NSA_HEADS = 4
NSA_HEAD_DIM = 64
NSA_WIDTH = NSA_HEADS * NSA_HEAD_DIM
CMP_LEN = 32
CMP_STRIDE = 16
CMP_HIDDEN = 2 * NSA_HEAD_DIM
SLC_BLOCK = 64
SLC_TOP_N = 16
WINDOW = 512
CONV_WIDTH = 256
CONV_KERNEL = 31
REL_BUCKETS = 32
REL_MAX_DIST = 128
N_EXPERTS = 32
TOP_K = 4
D_FF = D_MODEL
SWIGLU_LIMIT = 7.0
SWIGLU_ALPHA = 1.702
EPS = 1e-6
NEG_INF = -1e30
FORCE = 1e4

LANES = 128
VMEM_LIMIT_BYTES = 48 * 1024 * 1024
EXPERT_VMEM_LIMIT_BYTES = 56 * 1024 * 1024

ROW_TILE = 512
SEQ_TILE = 256
Q_TILE = 128
MOE_TILE = 512
COMBINE_TILE = 256

N_CMP_PAD = 128
N_SLC = SEQ // SLC_BLOCK


def _params(sem=None):
    return pltpu.CompilerParams(dimension_semantics=sem, vmem_limit_bytes=VMEM_LIMIT_BYTES)


def _mm(a, b):
    return jnp.dot(a.astype(BF16), b.astype(BF16), preferred_element_type=F32)


def _mm_nt(a, b):
    return lax.dot_general(a.astype(BF16), b.astype(BF16), (((1,), (1,)), ((), ())),
                           preferred_element_type=F32)


def _mm_tn(a, b):
    return lax.dot_general(a.astype(BF16), b.astype(BF16), (((0,), (0,)), ((), ())),
                           preferred_element_type=F32)


def _split3(x):
    hi = x.astype(BF16)
    r1 = x - hi.astype(F32)
    mid = r1.astype(BF16)
    lo = (r1 - mid.astype(F32)).astype(BF16)
    return hi, mid, lo


def _dot01_right(x, m01):
    hi, mid, lo = _split3(x)
    d = lambda p: jnp.dot(p, m01, preferred_element_type=F32)
    return d(hi) + d(mid) + d(lo)


def _dot01_left(m01, x):
    hi, mid, lo = _split3(x)
    d = lambda p: jnp.dot(m01, p, preferred_element_type=F32)
    return d(hi) + d(mid) + d(lo)


def _dot_f32(a, b):
    a_hi = a.astype(BF16)
    a_lo = (a - a_hi.astype(F32)).astype(BF16)
    b_hi = b.astype(BF16)
    b_lo = (b - b_hi.astype(F32)).astype(BF16)
    d = lambda p, q: jnp.dot(p, q, preferred_element_type=F32)
    return d(a_hi, b_hi) + d(a_hi, b_lo) + d(a_lo, b_hi) + d(a_lo, b_lo)


def _sigmoid(x):
    return 1.0 / (1.0 + jnp.exp(-x))


def _silu(x):
    return x * _sigmoid(x)


def _softplus(x):
    return jnp.maximum(x, 0.0) + jnp.log(1.0 + jnp.exp(-jnp.abs(x)))


SUBLANES = 8
ROW_CHUNKS = D_MODEL // LANES
assert ROW_CHUNKS == SUBLANES


def _store_tile_rows(ref, value, rows):
    for s in range(ROW_CHUNKS):
        ref[pl.ds(s, rows, stride=SUBLANES), :] = value[:, s * LANES:(s + 1) * LANES]


def _load_tile_rows(ref, rows):
    return jnp.concatenate([ref[pl.ds(s, rows, stride=SUBLANES), :] for s in range(ROW_CHUNKS)], axis=1)


def _bcast_col(x, j, width):
    return jnp.broadcast_to(x[:, j:j + 1], (x.shape[0], width))


IN_SEGS = (3 * DN_WIDTH, DN_WIDTH, NSA_WIDTH, 6 * NSA_HEAD_DIM, 2 * CONV_WIDTH, LANES)
IN_COLS = sum(IN_SEGS)


def _in_proj_kernel(x_ref, nw_ref, w_ref, wabt_ref,
                    qkv_ref, z_ref, nq_ref, nkv_ref, cu_ref, small_ref, abt_ref):
    xf = x_ref[...]
    ms = jnp.mean(xf * xf, axis=-1, keepdims=True)
    hb = (xf * lax.rsqrt(ms + EPS) * nw_ref[...]).astype(BF16)
    off = 0
    for ref, width in zip((qkv_ref, z_ref, nq_ref, nkv_ref, cu_ref, small_ref), IN_SEGS):
        ref[...] = jnp.dot(hb, w_ref[:, off:off + width], preferred_element_type=F32)
        off += width
    abt_ref[...] = lax.dot_general(wabt_ref[...], hb, (((1,), (1,)), ((), ())),
                                   preferred_element_type=F32)


def _in_proj(x2, norm_w, w_cat, w_abt):
    n = x2.shape[0]
    tm = ROW_TILE
    out_shape = [jax.ShapeDtypeStruct((n, w), F32) for w in IN_SEGS]
    out_shape.append(jax.ShapeDtypeStruct((16, n), F32))
    out_specs = [pl.BlockSpec((tm, w), lambda i: (i, 0)) for w in IN_SEGS]
    out_specs.append(pl.BlockSpec((16, tm), lambda i: (0, i)))
    return pl.pallas_call(
        _in_proj_kernel,
        out_shape=out_shape,
        grid=(n // tm,),
        in_specs=[
            pl.BlockSpec((tm, D_MODEL), lambda i: (i, 0)),
            pl.BlockSpec((1, D_MODEL), lambda i: (0, 0)),
            pl.BlockSpec((D_MODEL, IN_COLS), lambda i: (0, 0)),
            pl.BlockSpec((16, D_MODEL), lambda i: (0, 0)),
        ],
        out_specs=out_specs,
        compiler_params=_params(("parallel",)),
        name="in_proj",
    )(x2, norm_w, w_cat, w_abt)


def _dn_prep_kernel(qkv_ref, small_ref, abt_ref, cw_ref, pcol_ref, prow_ref,
                    q_ref, k_ref, v_ref, gb_ref, grow_ref, buf):
    t = pl.program_id(1)
    ts = SEQ_TILE

    @pl.when(t == 0)
    def _():
        buf[0:8, :] = jnp.zeros((8, 3 * DN_WIDTH), F32)

    buf[8:8 + ts, :] = qkv_ref[0]
    for c in range(3 * DN_HEADS):
        cs = slice(c * LANES, (c + 1) * LANES)
        acc = cw_ref[0:1, cs] * buf[5:5 + ts, cs]
        for j in range(1, DN_CONV):
            acc = acc + cw_ref[j:j + 1, cs] * buf[5 + j:5 + j + ts, cs]
        y = _silu(acc)
        if c < 2 * DN_HEADS:
            y = y * lax.rsqrt(jnp.sum(y * y, axis=-1, keepdims=True) + EPS)
        if c < DN_HEADS:
            q_ref[0, :, cs] = y * DN_HEAD_DIM ** -0.5
        elif c < 2 * DN_HEADS:
            k_ref[0, :, (c - DN_HEADS) * LANES:(c - DN_HEADS + 1) * LANES] = y
        else:
            v_ref[0, :, (c - 2 * DN_HEADS) * LANES:(c - 2 * DN_HEADS + 1) * LANES] = y
    buf[0:8, :] = buf[ts:ts + 8, :]

    sm = small_ref[0]
    lane = lax.broadcasted_iota(jnp.int32, sm.shape, 1)
    g_col = -jnp.exp(pcol_ref[0:1, :]) * _softplus(sm + pcol_ref[1:2, :])
    gb_ref[0] = jnp.where(lane < DN_HEADS, g_col, _sigmoid(sm))
    a_t = abt_ref[...]
    g_row = -jnp.exp(prow_ref[:, 0:1]) * _softplus(a_t + prow_ref[:, 1:2])
    grow_ref[...] = g_row[0:8, :]


def _dn_prep(qkv, small, abt, conv_w, pcol, prow):
    b = qkv.shape[0]
    ts = SEQ_TILE
    nt = SEQ // ts
    seq_spec = lambda w: pl.BlockSpec((1, ts, w), lambda i, t: (i, t, 0))
    full = lambda shape: pl.BlockSpec(shape, lambda i, t: (0,) * len(shape))
    return pl.pallas_call(
        _dn_prep_kernel,
        out_shape=[jax.ShapeDtypeStruct((b, SEQ, DN_WIDTH), F32)] * 3
        + [jax.ShapeDtypeStruct((b, SEQ, LANES), F32), jax.ShapeDtypeStruct((8, b * SEQ), F32)],
        grid=(b, nt),
        in_specs=[
            seq_spec(3 * DN_WIDTH), seq_spec(LANES),
            pl.BlockSpec((16, ts), lambda i, t: (0, i * nt + t)),
            full((DN_CONV, 3 * DN_WIDTH)), full((8, LANES)), full((16, LANES)),
        ],
        out_specs=[seq_spec(DN_WIDTH)] * 3
        + [seq_spec(LANES), pl.BlockSpec((8, ts), lambda i, t: (0, i * nt + t))],
        scratch_shapes=[pltpu.VMEM((ts + 8, 3 * DN_WIDTH), F32)],
        compiler_params=_params(("parallel", "arbitrary")),
        name="dn_prep",
    )(qkv, small, abt, conv_w, pcol, prow)


def _delta_prepare(q_ref, k_ref, v_ref, gb_ref, grow_ref, wy):
    u_ref, w_ref, attn_ref, qg_ref, kk_ref, gc_ref = wy
    ts = SEQ_TILE
    ch = DN_CHUNK
    hd = DN_HEAD_DIM
    r = lax.broadcasted_iota(jnp.int32, (ts, ts), 0)
    c = lax.broadcasted_iota(jnp.int32, (ts, ts), 1)
    same_chunk = (r // ch) == (c // ch)
    tril = same_chunk & (r >= c)
    strict = same_chunk & (r > c)
    same16 = (r // 16) == (c // 16)
    eye = jnp.where(r == c, 1.0, 0.0).astype(F32)
    m_col = jnp.where(tril, 1.0, 0.0).astype(BF16)
    m_row = jnp.where(same_chunk & (r <= c), 1.0, 0.0).astype(BF16)

    gb = gb_ref[0]
    gc_col = _dot01_left(m_col, gb)
    gc_row = _dot01_right(grow_ref[...], m_row)

    heads = range(DN_HEADS)
    hsl = [slice(h * hd, (h + 1) * hd) for h in heads]
    kh = [k_ref[0, :, hsl[h]] for h in heads]
    gcb = [_bcast_col(gc_col, h, ts) for h in heads]
    decay = [jnp.where(tril, jnp.exp(jnp.where(
        tril, gcb[h] - jnp.broadcast_to(gc_row[h:h + 1, :], (ts, ts)), 0.0)), 0.0) for h in heads]
    beta = [_bcast_col(gb, DN_HEADS + h, hd) for h in heads]
    kb = [kh[h] * beta[h] for h in heads]
    yield
    a_mat = [jnp.where(strict, _mm_nt(kb[h], kh[h]) * decay[h], 0.0) for h in heads]
    yield
    d_mat = [jnp.where(same16, a_mat[h], 0.0) for h in heads]
    e_mat = [a_mat[h] - d_mat[h] for h in heads]
    d2 = [_mm(d_mat[h], d_mat[h]) for h in heads]
    yield
    d4 = [_mm(d2[h], d2[h]) for h in heads]
    yield
    t1 = [_mm(eye - d_mat[h], eye + d2[h]) for h in heads]
    yield
    d8 = [_mm(d4[h], d4[h]) for h in heads]
    yield
    t2 = [_mm(t1[h], eye + d4[h]) for h in heads]
    yield
    p_mat = [_mm(t2[h], eye + d8[h]) for h in heads]
    yield
    m_mat = [_mm(p_mat[h], e_mat[h]) for h in heads]
    yield
    m2 = [_mm(m_mat[h], m_mat[h]) for h in heads]
    yield
    t3 = [_mm(eye - m_mat[h], eye + m2[h]) for h in heads]
    yield
    t_mat = [_mm(t3[h], p_mat[h]) for h in heads]
    yield
    for h in heads:
        gc128 = gcb[h][:, :hd]
        expg = jnp.exp(gc128)
        qh = q_ref[0, :, hsl[h]]
        sol = _mm(t_mat[h], jnp.concatenate([v_ref[0, :, hsl[h]] * beta[h], kb[h] * expg], axis=1))
        u_ref[h] = sol[:, :hd]
        w_ref[h] = sol[:, hd:].astype(BF16)
        qg_ref[h] = (qh * expg).astype(BF16)
        kk_ref[h] = kh[h]
        gc_ref[h] = gc128
    yield
    for h in heads:
        attn_ref[h] = jnp.where(tril, _mm_nt(q_ref[0, :, hsl[h]], kh[h]) * decay[h], 0.0).astype(BF16)


def _delta_recur(wy, z_ref, nw_ref, o_ref, s_ref):
    u_ref, w_ref, attn_ref, qg_ref, kk_ref, gc_ref = wy
    ch = DN_CHUNK
    hd = DN_HEAD_DIM
    heads = range(DN_HEADS)
    state = [s_ref[h] for h in heads]
    outs = [[] for _ in heads]
    for ci in range(SEQ_TILE // ch):
        rs = slice(ci * ch, (ci + 1) * ch)
        v_new = [u_ref[h, rs, :] - _mm(w_ref[h, rs, :], state[h]) for h in heads]
        o_state = [_mm(qg_ref[h, rs, :], state[h]) for h in heads]
        yield
        for h in heads:
            gc = gc_ref[h, rs, :]
            g_last = gc[ch - 1:ch, :]
            k_dec = kk_ref[h, rs, :] * jnp.exp(g_last - gc)
            outs[h].append(o_state[h] + _mm(attn_ref[h, rs, ci * ch:(ci + 1) * ch], v_new[h]))
            state[h] = state[h] * jnp.exp(g_last) + _mm_tn(k_dec, v_new[h])
        yield

    for h in heads:
        s_ref[h] = state[h]
        hs = slice(h * hd, (h + 1) * hd)
        o = jnp.concatenate(outs[h], axis=0)
        o = o * lax.rsqrt(jnp.mean(o * o, axis=-1, keepdims=True) + EPS) * nw_ref[...]
        o_ref[0, :, hs] = o * _silu(z_ref[0, :, hs])


def _interleave(first, second, ratio):
    live = [first, second]
    while live:
        for gen, n in ((first, ratio), (second, 1)):
            for _ in range(n):
                if gen in live and next(gen, live) is live:
                    live.remove(gen)


def _delta_kernel(q_ref, k_ref, v_ref, gb_ref, grow_ref, z_ref, nw_ref, o_ref, s_ref, *wy_refs):
    t = pl.program_id(1)
    set_a, set_b = wy_refs[:6], wy_refs[6:]

    @pl.when(t == 0)
    def _():
        s_ref[...] = jnp.zeros(s_ref.shape, F32)
        for ref in set_b:
            ref[...] = jnp.zeros(ref.shape, ref.dtype)

    def step(read_set, write_set):
        _interleave(_delta_prepare(q_ref, k_ref, v_ref, gb_ref, grow_ref, write_set),
                    _delta_recur(read_set, z_ref, nw_ref, o_ref, s_ref), 2)

    @pl.when(lax.rem(t, 2) == 0)
    def _():
        step(set_b, set_a)

    @pl.when(lax.rem(t, 2) == 1)
    def _():
        step(set_a, set_b)


def _delta_rule(q, k, v, gb, grow, z, norm_w):
    b = q.shape[0]
    ts = SEQ_TILE
    nt = SEQ // ts
    nxt = lambda w: pl.BlockSpec((1, ts, w), lambda i, t: (i, jnp.minimum(t, nt - 1), 0))
    cur = lambda w: pl.BlockSpec((1, ts, w), lambda i, t: (i, jnp.maximum(t - 1, 0), 0))
    hd = DN_HEAD_DIM
    wy_set = [pltpu.VMEM((DN_HEADS, ts, hd), F32), pltpu.VMEM((DN_HEADS, ts, hd), BF16),
              pltpu.VMEM((DN_HEADS, ts, ts), BF16), pltpu.VMEM((DN_HEADS, ts, hd), BF16),
              pltpu.VMEM((DN_HEADS, ts, hd), F32), pltpu.VMEM((DN_HEADS, ts, hd), F32)]
    return pl.pallas_call(
        _delta_kernel,
        out_shape=jax.ShapeDtypeStruct((b, SEQ, DN_WIDTH), F32),
        grid=(b, nt + 1),
        in_specs=[
            nxt(DN_WIDTH), nxt(DN_WIDTH), nxt(DN_WIDTH), nxt(LANES),
            pl.BlockSpec((8, ts), lambda i, t: (0, i * nt + jnp.minimum(t, nt - 1))),
            cur(DN_WIDTH),
            pl.BlockSpec((1, DN_HEAD_DIM), lambda i, t: (0, 0)),
        ],
        out_specs=cur(DN_WIDTH),
        scratch_shapes=[pltpu.VMEM((DN_HEADS, hd, hd), F32)] + wy_set + wy_set,
        compiler_params=_params(("parallel", "arbitrary")),
        name="delta_rule",
    )(q, k, v, gb, grow, z, norm_w)


CONV_HALO = 32


def _conformer_kernel(u_ref, w_ref, b_ref, lnw_ref, lnb_ref, o_ref, buf):
    t = pl.program_id(1)
    ts = SEQ_TILE

    @pl.when(t == 0)
    def _():
        buf[0:CONV_HALO, :] = jnp.zeros((CONV_HALO, CONV_WIDTH), F32)

    u = u_ref[0]
    buf[CONV_HALO:CONV_HALO + ts, :] = u[:, :CONV_WIDTH] * _sigmoid(u[:, CONV_WIDTH:])
    base = CONV_HALO - (CONV_KERNEL - 1)
    rows = 64
    for rc in range(ts // rows):
        parts = []
        for cc in range(CONV_WIDTH // LANES):
            cs = slice(cc * LANES, (cc + 1) * LANES)
            r0 = base + rc * rows
            acc = w_ref[0:1, cs] * buf[r0:r0 + rows, cs]
            for j in range(1, CONV_KERNEL):
                acc = acc + w_ref[j:j + 1, cs] * buf[r0 + j:r0 + j + rows, cs]
            parts.append(acc)
        h = jnp.concatenate(parts, axis=1) + b_ref[...]
        mu = jnp.mean(h, axis=-1, keepdims=True)
        var = jnp.mean(jnp.square(h - mu), axis=-1, keepdims=True)
        hn = (h - mu) * lax.rsqrt(var + EPS) * lnw_ref[...] + lnb_ref[...]
        o_ref[0, rc * rows:(rc + 1) * rows, :] = _silu(hn)
    buf[0:CONV_HALO, :] = buf[ts:ts + CONV_HALO, :]


def _conformer(u, dw_w, dw_b, ln_w, ln_b):
    b = u.shape[0]
    ts = SEQ_TILE
    full = lambda shape: pl.BlockSpec(shape, lambda i, t: (0,) * len(shape))
    return pl.pallas_call(
        _conformer_kernel,
        out_shape=jax.ShapeDtypeStruct((b, SEQ, CONV_WIDTH), F32),
        grid=(b, SEQ // ts),
        in_specs=[
            pl.BlockSpec((1, ts, 2 * CONV_WIDTH), lambda i, t: (i, t, 0)),
            full((CONV_KERNEL, CONV_WIDTH)), full((1, CONV_WIDTH)),
            full((1, CONV_WIDTH)), full((1, CONV_WIDTH)),
        ],
        out_specs=pl.BlockSpec((1, ts, CONV_WIDTH), lambda i, t: (i, t, 0)),
        scratch_shapes=[pltpu.VMEM((ts + CONV_HALO, CONV_WIDTH), F32)],
        compiler_params=_params(("parallel", "arbitrary")),
        name="conformer",
    )(u, dw_w, dw_b, ln_w, ln_b)


def _rms_rows(x, w):
    return x * lax.rsqrt(jnp.mean(x * x, axis=-1, keepdims=True) + EPS) * w


def _nsa_prep_kernel(kc_ref, vc_ref, ks_ref, kw_ref, pos_ref, w1_ref, w2_ref, knw_ref,
                     kcmp_ref, vcmp_ref, ksn_ref, kwn_ref):
    half = CMP_STRIDE * NSA_HEAD_DIM

    def compress(x, i):
        u_lo = _mm(x + pos_ref[2 * i:2 * i + 1, :], w1_ref[i, 0:half, :])
        u_hi = _mm(x + pos_ref[2 * i + 1:2 * i + 2, :], w1_ref[i, half:2 * half, :])
        hid = _silu(u_lo + pltpu.roll(u_hi, N_CMP_PAD - 1, axis=0))
        return _mm(hid, w2_ref[i])

    kcmp_ref[0] = _rms_rows(compress(kc_ref[0], 0), knw_ref[0:1, :])
    vcmp_ref[0] = compress(vc_ref[0], 1)
    ksn_ref[0] = _rms_rows(ks_ref[0], knw_ref[1:2, :])
    kwn_ref[0] = _rms_rows(kw_ref[0], knw_ref[2:3, :])


def _nsa_prep(kc, vc, ks, kw, pos, w1, w2, knw):
    b = kc.shape[0]
    hd = NSA_HEAD_DIM
    flat = CMP_STRIDE * hd
    full = lambda shape: pl.BlockSpec(shape, lambda i: (0,) * len(shape))
    bspec = lambda r, w: pl.BlockSpec((1, r, w), lambda i: (i, 0, 0))
    return pl.pallas_call(
        _nsa_prep_kernel,
        out_shape=[jax.ShapeDtypeStruct((b, N_CMP_PAD, hd), F32)] * 2
        + [jax.ShapeDtypeStruct((b, SEQ, hd), F32)] * 2,
        grid=(b,),
        in_specs=[
            bspec(N_CMP_PAD, flat), bspec(N_CMP_PAD, flat), bspec(SEQ, hd), bspec(SEQ, hd),
            full((4, flat)), full((2, 2 * flat, CMP_HIDDEN)), full((2, CMP_HIDDEN, hd)), full((3, hd)),
        ],
        out_specs=[bspec(N_CMP_PAD, hd)] * 2 + [bspec(SEQ, hd)] * 2,
        compiler_params=_params(("parallel",)),
        name="nsa_prep",
    )(kc, vc, ks, kw, pos, w1, w2, knw)


def _nsa_attn_kernel(q_ref, qnw_ref, kcmp_ref, vcmp_ref, ks_ref, vs_ref, kw_ref, vw_ref,
                     gate_ref, bcmp_ref, btab_ref, o_ref,
                     madd_ref, s_ref, ksb_ref, kwb_ref, vsa_ref, vwa_ref):
    i = pl.program_id(1)
    tq = Q_TILE
    nh = NSA_HEADS
    hd = NSA_HEAD_DIM
    t0 = i * tq

    @pl.when(i == 0)
    def _():
        ones = jnp.ones((SEQ, hd), BF16)
        ksb_ref[...] = ks_ref[0].astype(BF16)
        kwb_ref[...] = kw_ref[0].astype(BF16)
        vsa_ref[...] = jnp.concatenate([vs_ref[0].astype(BF16), ones], axis=1)
        vwa_ref[...] = jnp.concatenate([vw_ref[0].astype(BF16), ones], axis=1)

    qs = jnp.concatenate(
        [_rms_rows(q_ref[0, h], qnw_ref[...]) * hd ** -0.5 for h in range(nh)], axis=0).astype(BF16)

    row = lax.broadcasted_iota(jnp.int32, (tq, LANES), 0)
    lane = lax.broadcasted_iota(jnp.int32, (tq, LANES), 1)
    qpos = t0 + row

    s_all = _mm_nt(qs, kcmp_ref[0])
    cmp_valid = (qpos >= lane * CMP_STRIDE + (CMP_LEN - 1)) & (lane < N_CMP_PAD - 1)
    p_rows = []
    p_sum = jnp.zeros((tq, LANES), F32)
    for h in range(nh):
        s = jnp.where(cmp_valid, s_all[h * tq:(h + 1) * tq] + bcmp_ref[h], NEG_INF)
        m = jnp.max(s, axis=-1, keepdims=True)
        p = jnp.where(cmp_valid, jnp.exp(s - m), 0.0)
        l = jnp.sum(p, axis=-1, keepdims=True)
        p = p * jnp.where(l > 0.0, 1.0 / l, 0.0)
        p_rows.append(p)
        p_sum = p_sum + p
    o_cmp = _mm(jnp.concatenate(p_rows, axis=0), vcmp_ref[0])

    ss = lax.broadcasted_iota(jnp.int32, (N_SLC, N_CMP_PAD), 0)
    jj = lax.broadcasted_iota(jnp.int32, (N_SLC, N_CMP_PAD), 1)
    overlap_t = ((jj * CMP_STRIDE < ss * SLC_BLOCK + SLC_BLOCK)
                 & (jj * CMP_STRIDE + CMP_LEN > ss * SLC_BLOCK) & (jj < N_CMP_PAD - 1))
    overlap_t = jnp.where(overlap_t, 1.0, 0.0).astype(BF16)
    nt = lambda a, b_: lax.dot_general(a, b_, (((1,), (1,)), ((), ())), preferred_element_type=F32)
    p_hi, p_mid, p_lo = _split3(p_sum)
    imp = nt(overlap_t, p_hi) + nt(overlap_t, p_mid) + nt(overlap_t, p_lo)
    blk = lax.broadcasted_iota(jnp.int32, (N_SLC, tq), 0)
    cur = (t0 + lax.broadcasted_iota(jnp.int32, (N_SLC, tq), 1)) // SLC_BLOCK
    causal_blk = blk <= cur
    forced = (blk == 0) | (blk == cur) | (blk == cur - 1)
    imp = jnp.where(causal_blk & forced, FORCE, jnp.where(causal_blk, imp, -1.0))
    rank = jnp.zeros((N_SLC, tq), jnp.int32)
    for s2 in range(N_SLC):
        other = jnp.broadcast_to(imp[s2:s2 + 1, :], (N_SLC, tq))
        beats = (other > imp) | ((other == imp) & (blk > s2))
        rank = rank + jnp.where(beats, 1, 0)
    sel_t = jnp.where(rank < SLC_TOP_N, 1.0, 0.0).astype(BF16)
    er = lax.broadcasted_iota(jnp.int32, (N_SLC, SEQ), 0)
    ec = lax.broadcasted_iota(jnp.int32, (N_SLC, SEQ), 1)
    expand = jnp.where(ec // SLC_BLOCK == er, 1.0, 0.0).astype(BF16)
    key = lax.broadcasted_iota(jnp.int32, (tq, SEQ), 1)
    qall = t0 + lax.broadcasted_iota(jnp.int32, (tq, SEQ), 0)
    chosen = (_mm_tn(sel_t, expand) > 0.5) & (qall >= key)
    madd_ref[...] = jnp.where(chosen, 0.0, NEG_INF)

    neg = jnp.full((nh * tq, tq), NEG_INF, F32)
    zero = jnp.zeros((nh * tq, LANES), F32)

    def scores(k_tile, bias_idx, add):
        s_t = _mm_nt(qs, k_tile)
        return jnp.concatenate(
            [s_t[h * tq:(h + 1) * tq] + (btab_ref[h, bias_idx] + add) for h in range(nh)], axis=0)

    def normalise(acc):
        return acc[:, :hd] * (1.0 / _bcast_col(acc, hd, hd))

    n_pairs = (i + 2) // 2

    def slc_pass1(j, m_run):
        for u in range(2):
            kt = 2 * j + u
            k0 = pl.multiple_of(kt * tq, tq)
            s_m = scores(ksb_ref[pl.ds(k0, tq), :], jnp.clip(i - kt, 0, 2), madd_ref[:, pl.ds(k0, tq)])
            s_ref[:, pl.ds(k0, tq)] = s_m
            m_run = jnp.maximum(m_run, s_m)
        return m_run

    m_slc = jnp.broadcast_to(
        jnp.max(lax.fori_loop(0, n_pairs, slc_pass1, neg), axis=-1, keepdims=True), (nh * tq, tq))

    def slc_pass2(j, acc):
        for u in range(2):
            k0 = pl.multiple_of((2 * j + u) * tq, tq)
            p = jnp.exp(s_ref[:, pl.ds(k0, tq)] - m_slc).astype(BF16)
            acc = acc + jnp.dot(p, vsa_ref[pl.ds(k0, tq), :], preferred_element_type=F32)
        return acc

    o_slc = normalise(lax.fori_loop(0, n_pairs, slc_pass2, zero))

    kcol = lax.broadcasted_iota(jnp.int32, (tq, tq), 1)
    qrow = t0 + lax.broadcasted_iota(jnp.int32, (tq, tq), 0)
    n_win = WINDOW // tq + 1
    starts = []
    m_run = neg
    for j in range(n_win):
        d = n_win - 1 - j
        k0 = pl.multiple_of(jnp.maximum(i - d, 0) * tq, tq)
        starts.append(k0)
        dist = qrow - (k0 + kcol)
        ok = (dist >= 0) & (dist < WINDOW) & (i >= d)
        s_m = scores(kwb_ref[pl.ds(k0, tq), :], min(d, 2), jnp.where(ok, 0.0, NEG_INF))
        s_ref[:, j * tq:(j + 1) * tq] = s_m
        m_run = jnp.maximum(m_run, s_m)
    m_win = jnp.broadcast_to(jnp.max(m_run, axis=-1, keepdims=True), (nh * tq, tq))
    acc = zero
    for j in range(n_win):
        p = jnp.exp(s_ref[:, j * tq:(j + 1) * tq] - m_win).astype(BF16)
        acc = acc + jnp.dot(p, vwa_ref[pl.ds(starts[j], tq), :], preferred_element_type=F32)
    o_win = normalise(acc)

    gates = _sigmoid(gate_ref[0])
    for h in range(nh):
        hs = slice(h * tq, (h + 1) * tq)
        g0 = 2 * DN_HEADS + 3 * h
        o_ref[0, h] = (_bcast_col(gates, g0, hd) * o_cmp[hs]
                       + _bcast_col(gates, g0 + 1, hd) * o_slc[hs]
                       + _bcast_col(gates, g0 + 2, hd) * o_win[hs])


def _nsa_attn(q, qnw, kcmp, vcmp, ks, vs, kw, vw, small, bias_cmp, btab):
    b = q.shape[0]
    tq = Q_TILE
    nh, hd = NSA_HEADS, NSA_HEAD_DIM
    full = lambda shape: pl.BlockSpec(shape, lambda bi, i: (0,) * len(shape))
    per_b = lambda r: pl.BlockSpec((1, r, hd), lambda bi, i: (bi, 0, 0))
    return pl.pallas_call(
        _nsa_attn_kernel,
        out_shape=jax.ShapeDtypeStruct((b, nh, SEQ, hd), F32),
        grid=(b, SEQ // tq),
        in_specs=[
            pl.BlockSpec((1, nh, tq, hd), lambda bi, i: (bi, 0, i, 0)),
            full((1, hd)),
            per_b(N_CMP_PAD), per_b(N_CMP_PAD), per_b(SEQ), per_b(SEQ), per_b(SEQ), per_b(SEQ),
            pl.BlockSpec((1, tq, LANES), lambda bi, i: (bi, i, 0)),
            pl.BlockSpec((nh, tq, N_CMP_PAD), lambda bi, i: (0, i, 0)),
            full((nh, 3, tq, tq)),
        ],
        out_specs=pl.BlockSpec((1, nh, tq, hd), lambda bi, i: (bi, 0, i, 0)),
        scratch_shapes=[pltpu.VMEM((tq, SEQ), F32), pltpu.VMEM((nh * tq, SEQ), F32),
                        pltpu.VMEM((SEQ, hd), BF16), pltpu.VMEM((SEQ, hd), BF16),
                        pltpu.VMEM((SEQ, 2 * hd), BF16), pltpu.VMEM((SEQ, 2 * hd), BF16)],
        compiler_params=_params(("parallel", "arbitrary")),
        name="nsa_attn",
    )(q, qnw, kcmp, vcmp, ks, vs, kw, vw, small, bias_cmp, btab)


def _out_router_kernel(ya_ref, yb_ref, yc_ref, x_ref, wo_ref, fnw_ref, rw_ref, rb_ref,
                       xo_ref, h_ref, idx_ref, wt_ref, hist_ref):
    wa = DN_WIDTH
    wb = wa + NSA_WIDTH
    y = (jnp.dot(ya_ref[...].astype(BF16), wo_ref[0:wa, :], preferred_element_type=F32)
         + jnp.dot(yb_ref[...].astype(BF16), wo_ref[wa:wb, :], preferred_element_type=F32)
         + jnp.dot(yc_ref[...].astype(BF16), wo_ref[wb:, :], preferred_element_type=F32))
    xn = x_ref[...] + y
    xo_ref[...] = xn
    h = xn * lax.rsqrt(jnp.mean(xn * xn, axis=-1, keepdims=True) + EPS) * fnw_ref[...]
    _store_tile_rows(h_ref, h, h.shape[0])
    logits = _dot_f32(h, rw_ref[...]) + rb_ref[...]
    lane = lax.broadcasted_iota(jnp.int32, logits.shape, 1)
    vals, idxs = [], []
    for _ in range(TOP_K):
        m = jnp.max(logits, axis=-1, keepdims=True)
        ix = jnp.min(jnp.where(logits == m, lane, LANES), axis=-1, keepdims=True)
        vals.append(m)
        idxs.append(ix)
        logits = jnp.where(lane == ix, -jnp.inf, logits)
    es = [jnp.exp(v - vals[0]) for v in vals]
    inv = 1.0 / (es[0] + es[1] + es[2] + es[3])
    idx_out = jnp.zeros(lane.shape, jnp.int32)
    wt_out = jnp.zeros(lane.shape, F32)
    chosen = jnp.zeros(lane.shape, F32)
    for k in range(TOP_K):
        idx_out = jnp.where(lane == k, idxs[k], idx_out)
        wt_out = jnp.where(lane == k, es[k] * inv, wt_out)
        chosen = chosen + jnp.where(lane == idxs[k], 1.0, 0.0)
    idx_ref[...] = idx_out
    wt_ref[...] = wt_out
    hist_ref[0] = jnp.broadcast_to(jnp.sum(chosen, axis=0, keepdims=True), (8, LANES))


def _out_router(ya, yb, yc, x2, w_out, fnw, rw, rb):
    n = x2.shape[0]
    tm = ROW_TILE
    row = lambda w: pl.BlockSpec((tm, w), lambda i: (i, 0))
    full = lambda shape: pl.BlockSpec(shape, lambda i: (0,) * len(shape))
    return pl.pallas_call(
        _out_router_kernel,
        out_shape=[jax.ShapeDtypeStruct((n, D_MODEL), F32), jax.ShapeDtypeStruct((n * SUBLANES, LANES), F32),
                   jax.ShapeDtypeStruct((n, LANES), jnp.int32), jax.ShapeDtypeStruct((n, LANES), F32),
                   jax.ShapeDtypeStruct((n // tm, 8, LANES), F32)],
        grid=(n // tm,),
        in_specs=[row(DN_WIDTH), row(NSA_WIDTH), row(CONV_WIDTH), row(D_MODEL),
                  full((D_MODEL, D_MODEL)), full((1, D_MODEL)), full((D_MODEL, LANES)), full((1, LANES))],
        out_specs=[row(D_MODEL), pl.BlockSpec((tm * SUBLANES, LANES), lambda i: (i, 0)), row(LANES), row(LANES),
                   pl.BlockSpec((1, 8, LANES), lambda i: (i, 0, 0))],
        compiler_params=_params(("parallel",)),
        name="out_router",
    )(ya, yb, yc, x2, w_out, fnw, rw, rb)


def _slots_kernel(idx_ref, base_ref, slot_ref):
    tm = ROW_TILE
    idx = idx_ref[...]
    lane = lax.broadcasted_iota(jnp.int32, (tm, LANES), 1)
    onehots = [jnp.where(lane == _bcast_col(idx, k, LANES), 1.0, 0.0) for k in range(TOP_K)]
    cnt = (onehots[0] + onehots[1]) + (onehots[2] + onehots[3])
    r = lax.broadcasted_iota(jnp.int32, (tm, tm), 0)
    c = lax.broadcasted_iota(jnp.int32, (tm, tm), 1)
    earlier = jnp.where(r > c, 1.0, 0.0).astype(BF16)
    rank = jnp.dot(earlier, cnt.astype(BF16), preferred_element_type=F32) + base_ref[0, 0:1, :]
    ones = jnp.ones((8, LANES), BF16)
    row8 = lax.broadcasted_iota(jnp.int32, (8, tm), 0)
    out = jnp.zeros((8, tm), F32)
    for k in range(TOP_K):
        hi, mid, lo = _split3(rank * onehots[k])
        d = lambda p: lax.dot_general(ones, p, (((1,), (1,)), ((), ())), preferred_element_type=F32)
        out = jnp.where(row8 == k, d(hi) + d(mid) + d(lo), out)
    slot_ref[...] = out.astype(jnp.int32)


def _slots(idx, base3):
    n = idx.shape[0]
    tm = ROW_TILE
    return pl.pallas_call(
        _slots_kernel,
        out_shape=jax.ShapeDtypeStruct((8, n), jnp.int32),
        grid=(n // tm,),
        in_specs=[pl.BlockSpec((tm, LANES), lambda i: (i, 0)),
                  pl.BlockSpec((1, 8, LANES), lambda i: (i, 0, 0))],
        out_specs=pl.BlockSpec((8, tm), lambda i: (0, i)),
        compiler_params=_params(("parallel",)),
        name="moe_slots",
    )(idx, base3)


def _dispatch_kernel(slot_ref, zoff_ref, nv_ref, h_ref, xs_hbm, zbuf, sem, zsem):
    i = pl.program_id(0)
    tm = ROW_TILE
    n = slot_ref.shape[0] // TOP_K
    base = i * tm

    @pl.when(i == 0)
    def _():
        zbuf[...] = jnp.zeros(zbuf.shape, F32)
        blk = MOE_TILE * SUBLANES
        n_blocks = xs_hbm.shape[0] // blk

        def fill(row0):
            return pltpu.make_async_copy(
                zbuf, xs_hbm.at[pl.ds(pl.multiple_of(row0 * SUBLANES, blk), blk)], zsem)

        def fill_region_end(e, c):
            fill(zoff_ref[e]).start()
            return c

        def fill_tail(b, c):
            fill(b * MOE_TILE).start()
            return c

        def wait_fill(b, c):
            fill(0).wait()
            return c

        lax.fori_loop(0, N_EXPERTS, fill_region_end, 0)
        lax.fori_loop(nv_ref[0], n_blocks, fill_tail, 0)
        lax.fori_loop(0, N_EXPERTS + n_blocks - nv_ref[0], wait_fill, 0)

    def body(r, c):
        src = h_ref.at[pl.ds(pl.multiple_of(r * SUBLANES, SUBLANES), SUBLANES)]
        for k in range(TOP_K):
            row0 = pl.multiple_of(slot_ref[k * n + base + r] * SUBLANES, SUBLANES)
            pltpu.make_async_copy(src, xs_hbm.at[pl.ds(row0, SUBLANES)], sem).start(priority=k % 2)
        return c

    lax.fori_loop(0, tm, body, 0, unroll=4)
    done = xs_hbm.at[pl.ds(0, TOP_K * tm * SUBLANES)]
    pltpu.make_async_copy(done, done, sem).wait()


def _dispatch(slot_flat, zero_off, n_valid, h, n_slots):
    n = h.shape[0] // SUBLANES
    tm = ROW_TILE
    return pl.pallas_call(
        _dispatch_kernel,
        out_shape=jax.ShapeDtypeStruct((n_slots * SUBLANES, LANES), F32),
        grid_spec=pltpu.PrefetchScalarGridSpec(
            num_scalar_prefetch=3,
            grid=(n // tm,),
            in_specs=[pl.BlockSpec((tm * SUBLANES, LANES), lambda i, s, z, nv: (i, 0))],
            out_specs=pl.BlockSpec(memory_space=pl.ANY),
            scratch_shapes=[pltpu.VMEM((MOE_TILE * SUBLANES, LANES), F32),
                            pltpu.SemaphoreType.DMA, pltpu.SemaphoreType.DMA],
        ),
        compiler_params=_params(("arbitrary",)),
        name="moe_dispatch",
    )(slot_flat, zero_off, n_valid, h)


def _expert_kernel(be_ref, nv_ref, xs_ref, wgu_ref, bgu_ref, wd_ref, bd_ref, o_ref, wgu_bf, wd_bf):
    i = pl.program_id(0)
    new_expert = (i == 0) | (be_ref[i] != be_ref[jnp.maximum(i - 1, 0)])

    @pl.when((i < nv_ref[0]) & new_expert)
    def _():
        step = 512
        for c0 in range(0, 2 * D_FF, step):
            wgu_bf[:, c0:c0 + step] = wgu_ref[:, c0:c0 + step].astype(BF16)
        for c0 in range(0, D_MODEL, step):
            wd_bf[:, c0:c0 + step] = wd_ref[:, c0:c0 + step].astype(BF16)

    @pl.when(i < nv_ref[0])
    def _():
        xb = _load_tile_rows(xs_ref, MOE_TILE).astype(BF16)
        gu = jnp.dot(xb, wgu_bf[...], preferred_element_type=F32) + bgu_ref[...]
        gate = jnp.minimum(gu[:, :D_FF], SWIGLU_LIMIT)
        up = jnp.clip(gu[:, D_FF:], -SWIGLU_LIMIT, SWIGLU_LIMIT)
        act = (up + 1.0) * gate * _sigmoid(SWIGLU_ALPHA * gate)
        y = jnp.dot(act.astype(BF16), wd_bf[...], preferred_element_type=F32) + bd_ref[...]
        _store_tile_rows(o_ref, y, MOE_TILE)

    @pl.when(i >= nv_ref[0])
    def _():
        o_ref[...] = jnp.zeros(o_ref.shape, F32)


def _expert_ffn(layer, block_expert, n_valid, xs, wgu, bgu, wd, bd):
    n_slots = xs.shape[0] // SUBLANES
    tm = MOE_TILE
    return pl.pallas_call(
        _expert_kernel,
        out_shape=jax.ShapeDtypeStruct(xs.shape, F32),
        grid_spec=pltpu.PrefetchScalarGridSpec(
            num_scalar_prefetch=2,
            grid=(n_slots // tm,),
            in_specs=[
                pl.BlockSpec((tm * SUBLANES, LANES), lambda i, be, nv: (i, 0)),
                pl.BlockSpec((None, None, D_MODEL, 2 * D_FF), lambda i, be, nv: (layer, be[i], 0, 0)),
                pl.BlockSpec((None, None, 1, 2 * D_FF), lambda i, be, nv: (layer, be[i], 0, 0)),
                pl.BlockSpec((None, None, D_FF, D_MODEL), lambda i, be, nv: (layer, be[i], 0, 0)),
                pl.BlockSpec((None, None, 1, D_MODEL), lambda i, be, nv: (layer, be[i], 0, 0)),
            ],
            out_specs=pl.BlockSpec((tm * SUBLANES, LANES), lambda i, be, nv: (i, 0)),
            scratch_shapes=[pltpu.VMEM((D_MODEL, 2 * D_FF), BF16), pltpu.VMEM((D_FF, D_MODEL), BF16)],
        ),
        compiler_params=pltpu.CompilerParams(dimension_semantics=("arbitrary",),
                                             vmem_limit_bytes=EXPERT_VMEM_LIMIT_BYTES),
        name="expert_ffn",
    )(block_expert, n_valid, xs, wgu, bgu, wd, bd)


def _combine_kernel(slot_ref, ys_hbm, x_ref, wt_ref, o_ref, buf, sems):
    i = pl.program_id(0)
    tc = COMBINE_TILE
    nsteps = pl.num_programs(0)
    n = slot_ref.shape[0] // TOP_K
    cur = lax.rem(i, 2)

    def issue(step, par):
        base = step * tc

        def body(r, c):
            dst0 = pl.multiple_of(r * SUBLANES, SUBLANES)
            for k in range(TOP_K):
                row0 = pl.multiple_of(slot_ref[k * n + base + r] * SUBLANES, SUBLANES)
                pltpu.make_async_copy(ys_hbm.at[pl.ds(row0, SUBLANES)],
                                      buf.at[par, k, pl.ds(dst0, SUBLANES)],
                                      sems.at[par]).start(priority=k % 2)
            return c

        lax.fori_loop(0, tc, body, 0, unroll=4)

    @pl.when(i == 0)
    def _():
        issue(0, 0)

    @pl.when(i + 1 < nsteps)
    def _():
        issue(i + 1, 1 - cur)

    pltpu.make_async_copy(buf.at[cur], buf.at[cur], sems.at[cur]).wait()
    w = [_bcast_col(wt_ref[...], k, LANES) for k in range(TOP_K)]
    for s in range(ROW_CHUNKS):
        cs = slice(s * LANES, (s + 1) * LANES)
        acc = x_ref[:, cs]
        for k in range(TOP_K):
            acc = acc + w[k] * buf[cur, k, pl.ds(s, tc, stride=SUBLANES), :]
        o_ref[:, cs] = acc


def _combine(slot_flat, ys, x2, wt):
    n = x2.shape[0]
    tc = COMBINE_TILE
    return pl.pallas_call(
        _combine_kernel,
        out_shape=jax.ShapeDtypeStruct((n, D_MODEL), F32),
        grid_spec=pltpu.PrefetchScalarGridSpec(
            num_scalar_prefetch=1,
            grid=(n // tc,),
            in_specs=[pl.BlockSpec(memory_space=pl.ANY),
                      pl.BlockSpec((tc, D_MODEL), lambda i, s: (i, 0)),
                      pl.BlockSpec((tc, LANES), lambda i, s: (i, 0))],
            out_specs=pl.BlockSpec((tc, D_MODEL), lambda i, s: (i, 0)),
            scratch_shapes=[pltpu.VMEM((2, TOP_K, tc * SUBLANES, LANES), F32), pltpu.SemaphoreType.DMA((2,))],
        ),
        compiler_params=_params(("arbitrary",)),
        name="moe_combine",
    )(slot_flat, ys, x2, wt)


def _moe(layer, h, x2, idx, wt, hist3, wgu, bgu, wd, bd):
    n = x2.shape[0]
    tm = MOE_TILE
    n_slots = n * TOP_K + N_EXPERTS * tm
    n_blocks = n_slots // tm
    hist = hist3[:, 0, :N_EXPERTS]
    counts = jnp.sum(hist, axis=0).astype(jnp.int32)
    padded = (counts + tm - 1) // tm * tm
    pad_end = jnp.cumsum(padded)
    pad_start = pad_end - padded
    tile_base = pad_start[None, :].astype(F32) + (jnp.cumsum(hist, axis=0) - hist)
    base3 = jnp.broadcast_to(jnp.pad(tile_base, ((0, 0), (0, LANES - N_EXPERTS)))[:, None, :],
                             (hist.shape[0], 8, LANES))
    blk0 = jnp.arange(n_blocks) * tm
    block_expert = jnp.minimum(jnp.sum(blk0[:, None] >= pad_end[None, :], axis=1), N_EXPERTS - 1).astype(jnp.int32)
    n_valid = (pad_end[-1:] // tm).astype(jnp.int32)
    zero_off = jnp.maximum(pad_end - tm, 0).astype(jnp.int32)

    slot_flat = _slots(idx, base3)[:TOP_K].reshape(-1)
    xs = _dispatch(slot_flat, zero_off, n_valid, h, n_slots)
    ys = _expert_ffn(layer, block_expert, n_valid, xs, wgu, bgu, wd, bd)
    return _combine(slot_flat, ys, x2, wt)


def _t5_bucket(dist):
    n = jnp.maximum(dist, 0)
    max_exact = REL_BUCKETS // 2
    nf = jnp.maximum(n, 1).astype(F32)
    large = max_exact + (jnp.log(nf / max_exact) / math.log(REL_MAX_DIST / max_exact)
                         * (REL_BUCKETS - max_exact)).astype(jnp.int32)
    large = jnp.minimum(large, REL_BUCKETS - 1)
    return jnp.where(n < max_exact, n, large)


def _bias_tables(rel_bias):
    rel_bias = rel_bias.astype(F32)
    tq = Q_TILE

    def lookup(dist):
        bucket = _t5_bucket(dist)
        out = jnp.zeros((NSA_HEADS,) + dist.shape, F32)
        for bk in range(REL_BUCKETS):
            out = jnp.where(bucket[None] == bk, rel_bias[bk].reshape((NSA_HEADS,) + (1,) * dist.ndim), out)
        return out

    t_pos = jnp.arange(SEQ)
    cmp_end = jnp.arange(N_CMP_PAD) * CMP_STRIDE + CMP_LEN - 1
    bias_cmp = lookup(t_pos[:, None] - cmp_end[None, :])
    rr = jnp.arange(tq)[:, None] - jnp.arange(tq)[None, :]
    btab = lookup(jnp.stack([rr, rr + tq, rr + 2 * tq]))
    return bias_cmp, btab


def _layer(layer, x2, b, p, experts, bias_cmp, btab):
    n = x2.shape[0]
    w = p['w_in']
    o_a, o_b, o_q, o_kv, o_g, o_u = 2048, 2052, 2056, 2312, 2696, 2708
    w_small = jnp.concatenate([w[:, o_a:o_q], w[:, o_g:o_u],
                               jnp.zeros((D_MODEL, LANES - 8 - 3 * NSA_HEADS), F32)], axis=1)
    w_cat = jnp.concatenate([w[:, :o_a], w[:, o_q:o_kv], w[:, o_kv:o_g], w[:, o_u:], w_small],
                            axis=1).astype(BF16)
    w_abt = jnp.concatenate([w[:, o_a:o_q].T, jnp.zeros((8, D_MODEL), F32)], axis=0).astype(BF16)
    qkv, z, nq, nkv, cu, small, abt = _in_proj(x2, p['attn_norm_w'][None, :], w_cat, w_abt)

    pcol = jnp.zeros((8, LANES), F32).at[0, :DN_HEADS].set(p['dn_a_log']).at[1, :DN_HEADS].set(p['dn_dt_bias'])
    prow = jnp.zeros((16, LANES), F32).at[:DN_HEADS, 0].set(p['dn_a_log']).at[:DN_HEADS, 1].set(p['dn_dt_bias'])
    small3 = small.reshape(b, SEQ, LANES)
    dq, dk, dv, gb, grow = _dn_prep(qkv.reshape(b, SEQ, 3 * DN_WIDTH), small3, abt,
                                    p['dn_conv_w'], pcol, prow)
    y_a = _delta_rule(dq, dk, dv, gb, grow, z.reshape(b, SEQ, DN_WIDTH), p['dn_norm_w'][None, :])

    y_c = _conformer(cu.reshape(b, SEQ, 2 * CONV_WIDTH), p['conv_dw_w'], p['conv_dw_b'][None, :],
                     p['conv_ln_w'][None, :], p['conv_ln_b'][None, :])

    hd = NSA_HEAD_DIM
    kv = nkv.reshape(b, SEQ, 6, hd)
    flat = CMP_STRIDE * hd
    kc = kv[:, :, 0].reshape(b, N_CMP_PAD, flat)
    vc = kv[:, :, 1].reshape(b, N_CMP_PAD, flat)
    pos = p['nsa_cmp_pos'].reshape(4, flat)
    kcmp, vcmp, ksn, kwn = _nsa_prep(kc, vc, kv[:, :, 2], kv[:, :, 4], pos,
                                     p['nsa_cmp_w1'].astype(BF16), p['nsa_cmp_w2'].astype(BF16),
                                     p['nsa_k_norm_w'])
    q4 = nq.reshape(b, SEQ, NSA_HEADS, hd).transpose(0, 2, 1, 3)
    o4 = _nsa_attn(q4, p['nsa_q_norm_w'][None, :], kcmp, vcmp, ksn, kv[:, :, 3], kwn, kv[:, :, 5],
                   small3, bias_cmp, btab)
    y_b = o4.transpose(0, 2, 1, 3).reshape(n, NSA_WIDTH)

    rw = jnp.concatenate([p['router_w'], jnp.zeros((D_MODEL, LANES - N_EXPERTS), F32)], axis=1)
    rb = jnp.concatenate([p['router_b'], jnp.full((LANES - N_EXPERTS,), NEG_INF, F32)])[None, :]
    x_new, h, idx, wt, hist3 = _out_router(y_a.reshape(n, DN_WIDTH), y_b, y_c.reshape(n, CONV_WIDTH), x2,
                                           p['w_out'].astype(BF16), p['ffn_norm_w'][None, :], rw, rb)
    return _moe(layer, h, x_new, idx, wt, hist3, *experts)


def kernel(x, attn_norm_w, w_in, dn_conv_w, dn_a_log, dn_dt_bias, dn_norm_w, nsa_q_norm_w, nsa_k_norm_w, nsa_cmp_pos, nsa_cmp_w1, nsa_cmp_w2, conv_dw_w, conv_dw_b, conv_ln_w, conv_ln_b, w_out, ffn_norm_w, router_w, router_b, w_gate_up, b_gate_up, w_down, b_down, rel_bias):
    b, t, d = x.shape
    assert (t, d) == (SEQ, D_MODEL)
    stacked = dict(attn_norm_w=attn_norm_w, w_in=w_in, dn_conv_w=dn_conv_w, dn_a_log=dn_a_log,
                   dn_dt_bias=dn_dt_bias, dn_norm_w=dn_norm_w, nsa_q_norm_w=nsa_q_norm_w,
                   nsa_k_norm_w=nsa_k_norm_w, nsa_cmp_pos=nsa_cmp_pos, nsa_cmp_w1=nsa_cmp_w1,
                   nsa_cmp_w2=nsa_cmp_w2, conv_dw_w=conv_dw_w, conv_dw_b=conv_dw_b,
                   conv_ln_w=conv_ln_w, conv_ln_b=conv_ln_b, w_out=w_out, ffn_norm_w=ffn_norm_w,
                   router_w=router_w, router_b=router_b)
    experts = (w_gate_up, b_gate_up[:, :, None, :], w_down, b_down[:, :, None, :])
    bias_cmp, btab = _bias_tables(rel_bias)
    x2 = x.reshape(b * t, d)
    for l in range(w_in.shape[0]):
        x2 = _layer(l, x2, b, {k: v[l] for k, v in stacked.items()}, experts, bias_cmp, btab)
    return x2.reshape(b, t, d)
```

```python
import functools
import math

import numpy as np
import jax
import jax.numpy as jnp
from jax import lax
from jax.experimental import pallas as pl
from jax.experimental.pallas import tpu as pltpu

F32 = jnp.float32
BF16 = jnp.bfloat16

D_MODEL = 1024
SEQ = 2048
DN_HEADS = 4
DN_HEAD_DIM = 128
DN_WIDTH = DN_HEADS * DN_HEAD_DIM
DN_CONV = 4
DN_CHUNK = 64
NSA_HEADS = 4
NSA_HEAD_DIM = 64
NSA_WIDTH = NSA_HEADS * NSA_HEAD_DIM
CMP_LEN = 32
CMP_STRIDE = 16
CMP_HIDDEN = 2 * NSA_HEAD_DIM
SLC_BLOCK = 64
SLC_TOP_N = 16
WINDOW = 512
CONV_WIDTH = 256
CONV_KERNEL = 31
REL_BUCKETS = 32
REL_MAX_DIST = 128
N_EXPERTS = 32
TOP_K = 4
D_FF = D_MODEL
SWIGLU_LIMIT = 7.0
SWIGLU_ALPHA = 1.702
EPS = 1e-6
NEG_INF = -1e30
FORCE = 1e4

LANES = 128
VMEM_LIMIT_BYTES = 48 * 1024 * 1024
EXPERT_VMEM_LIMIT_BYTES = 56 * 1024 * 1024

ROW_TILE = 512
SEQ_TILE = 256
Q_TILE = 128
MOE_TILE = 512
FF_CHUNK = 256
COMBINE_TILE = 256

N_CMP_PAD = 128
N_SLC = SEQ // SLC_BLOCK


def _params(sem=None):
    return pltpu.CompilerParams(dimension_semantics=sem, vmem_limit_bytes=VMEM_LIMIT_BYTES)


def _mm(a, b):
    return jnp.dot(a.astype(BF16), b.astype(BF16), preferred_element_type=F32)


def _mm_nt(a, b):
    return lax.dot_general(a.astype(BF16), b.astype(BF16), (((1,), (1,)), ((), ())),
                           preferred_element_type=F32)


def _mm_tn(a, b):
    return lax.dot_general(a.astype(BF16), b.astype(BF16), (((0,), (0,)), ((), ())),
                           preferred_element_type=F32)


def _split3(x):
    hi = x.astype(BF16)
    r1 = x - hi.astype(F32)
    mid = r1.astype(BF16)
    lo = (r1 - mid.astype(F32)).astype(BF16)
    return hi, mid, lo


def _dot01_right(x, m01):
    hi, mid, lo = _split3(x)
    d = lambda p: jnp.dot(p, m01, preferred_element_type=F32)
    return d(hi) + d(mid) + d(lo)


def _dot01_left(m01, x):
    hi, mid, lo = _split3(x)
    d = lambda p: jnp.dot(m01, p, preferred_element_type=F32)
    return d(hi) + d(mid) + d(lo)


def _dot_f32(a, b):
    a_hi = a.astype(BF16)
    a_lo = (a - a_hi.astype(F32)).astype(BF16)
    b_hi = b.astype(BF16)
    b_lo = (b - b_hi.astype(F32)).astype(BF16)
    d = lambda p, q: jnp.dot(p, q, preferred_element_type=F32)
    return d(a_hi, b_hi) + d(a_hi, b_lo) + d(a_lo, b_hi) + d(a_lo, b_lo)


def _sigmoid(x):
    return 1.0 / (1.0 + jnp.exp(-x))


def _silu(x):
    return x * _sigmoid(x)


def _softplus(x):
    return jnp.maximum(x, 0.0) + jnp.log(1.0 + jnp.exp(-jnp.abs(x)))


SUBLANES = 8
ROW_CHUNKS = D_MODEL // LANES
assert ROW_CHUNKS == SUBLANES


def _store_tile_rows(ref, value, rows):
    for s in range(ROW_CHUNKS):
        ref[pl.ds(s, rows, stride=SUBLANES), :] = value[:, s * LANES:(s + 1) * LANES]


def _load_tile_rows(ref, rows):
    return jnp.concatenate([ref[pl.ds(s, rows, stride=SUBLANES), :] for s in range(ROW_CHUNKS)], axis=1)


def _bcast_col(x, j, width):
    return jnp.broadcast_to(x[:, j:j + 1], (x.shape[0], width))


IN_SEGS = (3 * DN_WIDTH, DN_WIDTH, NSA_WIDTH, 6 * NSA_HEAD_DIM, 2 * CONV_WIDTH, LANES)
IN_COLS = sum(IN_SEGS)


def _in_proj_kernel(x_ref, nw_ref, w_ref, wabt_ref,
                    qkv_ref, z_ref, nq_ref, nkv_ref, cu_ref, small_ref, abt_ref):
    xf = x_ref[...]
    ms = jnp.mean(xf * xf, axis=-1, keepdims=True)
    hb = (xf * lax.rsqrt(ms + EPS) * nw_ref[...]).astype(BF16)
    hd = NSA_HEAD_DIM
    off = 0
    for ref, width in zip((qkv_ref, z_ref, nq_ref, nkv_ref, cu_ref, small_ref), IN_SEGS):
        res = jnp.dot(hb, w_ref[:, off:off + width], preferred_element_type=F32)
        if ref is nq_ref or ref is nkv_ref:
            for j in range(width // hd):
                ref[j] = res[:, j * hd:(j + 1) * hd]
        else:
            ref[...] = res
        off += width
    abt_ref[...] = lax.dot_general(wabt_ref[...], hb, (((1,), (1,)), ((), ())),
                                   preferred_element_type=F32)


def _in_proj(x2, norm_w, w_cat, w_abt):
    n = x2.shape[0]
    tm = ROW_TILE
    hd = NSA_HEAD_DIM
    pieces = {2: NSA_HEADS, 3: 6}
    out_shape, out_specs = [], []
    for k, w in enumerate(IN_SEGS):
        if k in pieces:
            out_shape.append(jax.ShapeDtypeStruct((pieces[k], n, hd), F32))
            out_specs.append(pl.BlockSpec((pieces[k], tm, hd), lambda i: (0, i, 0)))
        else:
            out_shape.append(jax.ShapeDtypeStruct((n, w), F32))
            out_specs.append(pl.BlockSpec((tm, w), lambda i: (i, 0)))
    out_shape.append(jax.ShapeDtypeStruct((16, n), F32))
    out_specs.append(pl.BlockSpec((16, tm), lambda i: (0, i)))
    return pl.pallas_call(
        _in_proj_kernel,
        out_shape=out_shape,
        grid=(n // tm,),
        in_specs=[
            pl.BlockSpec((tm, D_MODEL), lambda i: (i, 0)),
            pl.BlockSpec((1, D_MODEL), lambda i: (0, 0)),
            pl.BlockSpec((D_MODEL, IN_COLS), lambda i: (0, 0)),
            pl.BlockSpec((16, D_MODEL), lambda i: (0, 0)),
        ],
        out_specs=out_specs,
        compiler_params=_params(("parallel",)),
        name="in_proj",
    )(x2, norm_w, w_cat, w_abt)


def _dn_prep_kernel(qkv_ref, small_ref, abt_ref, cw_ref, pcol_ref, prow_ref,
                    q_ref, k_ref, v_ref, gb_ref, grow_ref, buf):
    t = pl.program_id(1)
    ts = SEQ_TILE

    @pl.when(t == 0)
    def _():
        buf[0:8, :] = jnp.zeros((8, 3 * DN_WIDTH), F32)

    buf[8:8 + ts, :] = qkv_ref[0]
    for c in range(3 * DN_HEADS):
        cs = slice(c * LANES, (c + 1) * LANES)
        acc = cw_ref[0:1, cs] * buf[5:5 + ts, cs]
        for j in range(1, DN_CONV):
            acc = acc + cw_ref[j:j + 1, cs] * buf[5 + j:5 + j + ts, cs]
        y = _silu(acc)
        if c < 2 * DN_HEADS:
            y = y * lax.rsqrt(jnp.sum(y * y, axis=-1, keepdims=True) + EPS)
        if c < DN_HEADS:
            q_ref[0, :, cs] = y * DN_HEAD_DIM ** -0.5
        elif c < 2 * DN_HEADS:
            k_ref[0, :, (c - DN_HEADS) * LANES:(c - DN_HEADS + 1) * LANES] = y
        else:
            v_ref[0, :, (c - 2 * DN_HEADS) * LANES:(c - 2 * DN_HEADS + 1) * LANES] = y
    buf[0:8, :] = buf[ts:ts + 8, :]

    sm = small_ref[0]
    lane = lax.broadcasted_iota(jnp.int32, sm.shape, 1)
    g_col = -jnp.exp(pcol_ref[0:1, :]) * _softplus(sm + pcol_ref[1:2, :])
    gb_ref[0] = jnp.where(lane < DN_HEADS, g_col, _sigmoid(sm))
    a_t = abt_ref[...]
    g_row = -jnp.exp(prow_ref[:, 0:1]) * _softplus(a_t + prow_ref[:, 1:2])
    grow_ref[...] = g_row[0:8, :]


def _dn_prep(qkv, small, abt, conv_w, pcol, prow):
    b = qkv.shape[0]
    ts = SEQ_TILE
    nt = SEQ // ts
    seq_spec = lambda w: pl.BlockSpec((1, ts, w), lambda i, t: (i, t, 0))
    full = lambda shape: pl.BlockSpec(shape, lambda i, t: (0,) * len(shape))
    return pl.pallas_call(
        _dn_prep_kernel,
        out_shape=[jax.ShapeDtypeStruct((b, SEQ, DN_WIDTH), F32)] * 3
        + [jax.ShapeDtypeStruct((b, SEQ, LANES), F32), jax.ShapeDtypeStruct((8, b * SEQ), F32)],
        grid=(b, nt),
        in_specs=[
            seq_spec(3 * DN_WIDTH), seq_spec(LANES),
            pl.BlockSpec((16, ts), lambda i, t: (0, i * nt + t)),
            full((DN_CONV, 3 * DN_WIDTH)), full((8, LANES)), full((16, LANES)),
        ],
        out_specs=[seq_spec(DN_WIDTH)] * 3
        + [seq_spec(LANES), pl.BlockSpec((8, ts), lambda i, t: (0, i * nt + t))],
        scratch_shapes=[pltpu.VMEM((ts + 8, 3 * DN_WIDTH), F32)],
        compiler_params=_params(("parallel", "arbitrary")),
        name="dn_prep",
    )(qkv, small, abt, conv_w, pcol, prow)


def _delta_prepare(q_ref, k_ref, v_ref, gb_ref, grow_ref, wy):
    u_ref, w_ref, attn_ref, qg_ref, kk_ref, gc_ref = wy
    ts = SEQ_TILE
    ch = DN_CHUNK
    hd = DN_HEAD_DIM
    r = lax.broadcasted_iota(jnp.int32, (ts, ts), 0)
    c = lax.broadcasted_iota(jnp.int32, (ts, ts), 1)
    same_chunk = (r // ch) == (c // ch)
    tril = same_chunk & (r >= c)
    strict = same_chunk & (r > c)
    same16 = (r // 16) == (c // 16)
    eye = jnp.where(r == c, 1.0, 0.0).astype(F32)
    m_col = jnp.where(tril, 1.0, 0.0).astype(BF16)
    m_row = jnp.where(same_chunk & (r <= c), 1.0, 0.0).astype(BF16)

    gb = gb_ref[0]
    gc_col = _dot01_left(m_col, gb)
    gc_row = _dot01_right(grow_ref[...], m_row)

    heads = range(DN_HEADS)
    hsl = [slice(h * hd, (h + 1) * hd) for h in heads]
    kh = [k_ref[0, :, hsl[h]] for h in heads]
    gcb = [_bcast_col(gc_col, h, ts) for h in heads]
    decay = [jnp.where(tril, jnp.exp(jnp.where(
        tril, gcb[h] - jnp.broadcast_to(gc_row[h:h + 1, :], (ts, ts)), 0.0)), 0.0) for h in heads]
    beta = [_bcast_col(gb, DN_HEADS + h, hd) for h in heads]
    kb = [kh[h] * beta[h] for h in heads]
    yield
    a_mat = [jnp.where(strict, _mm_nt(kb[h], kh[h]) * decay[h], 0.0) for h in heads]
    yield
    d_mat = [jnp.where(same16, a_mat[h], 0.0) for h in heads]
    e_mat = [a_mat[h] - d_mat[h] for h in heads]
    d2 = [_mm(d_mat[h], d_mat[h]) for h in heads]
    yield
    d4 = [_mm(d2[h], d2[h]) for h in heads]
    yield
    t1 = [_mm(eye - d_mat[h], eye + d2[h]) for h in heads]
    yield
    d8 = [_mm(d4[h], d4[h]) for h in heads]
    yield
    t2 = [_mm(t1[h], eye + d4[h]) for h in heads]
    yield
    p_mat = [_mm(t2[h], eye + d8[h]) for h in heads]
    yield
    m_mat = [_mm(p_mat[h], e_mat[h]) for h in heads]
    yield
    m2 = [_mm(m_mat[h], m_mat[h]) for h in heads]
    yield
    t3 = [_mm(eye - m_mat[h], eye + m2[h]) for h in heads]
    yield
    t_mat = [_mm(t3[h], p_mat[h]) for h in heads]
    yield
    for h in heads:
        gc128 = gcb[h][:, :hd]
        expg = jnp.exp(gc128)
        qh = q_ref[0, :, hsl[h]]
        sol = _mm(t_mat[h], jnp.concatenate([v_ref[0, :, hsl[h]] * beta[h], kb[h] * expg], axis=1))
        u_ref[h] = sol[:, :hd]
        w_ref[h] = sol[:, hd:].astype(BF16)
        qg_ref[h] = (qh * expg).astype(BF16)
        kk_ref[h] = kh[h]
        gc_ref[h] = gc128
    yield
    for h in heads:
        attn_ref[h] = jnp.where(tril, _mm_nt(q_ref[0, :, hsl[h]], kh[h]) * decay[h], 0.0).astype(BF16)


def _delta_recur(wy, z_ref, nw_ref, o_ref, s_ref):
    u_ref, w_ref, attn_ref, qg_ref, kk_ref, gc_ref = wy
    ch = DN_CHUNK
    hd = DN_HEAD_DIM
    heads = range(DN_HEADS)
    state = [s_ref[h] for h in heads]
    outs = [[] for _ in heads]
    for ci in range(SEQ_TILE // ch):
        rs = slice(ci * ch, (ci + 1) * ch)
        v_new = [u_ref[h, rs, :] - _mm(w_ref[h, rs, :], state[h]) for h in heads]
        o_state = [_mm(qg_ref[h, rs, :], state[h]) for h in heads]
        yield
        for h in heads:
            gc = gc_ref[h, rs, :]
            g_last = gc[ch - 1:ch, :]
            k_dec = kk_ref[h, rs, :] * jnp.exp(g_last - gc)
            outs[h].append(o_state[h] + _mm(attn_ref[h, rs, ci * ch:(ci + 1) * ch], v_new[h]))
            state[h] = state[h] * jnp.exp(g_last) + _mm_tn(k_dec, v_new[h])
        yield

    for h in heads:
        s_ref[h] = state[h]
        hs = slice(h * hd, (h + 1) * hd)
        o = jnp.concatenate(outs[h], axis=0)
        o = o * lax.rsqrt(jnp.mean(o * o, axis=-1, keepdims=True) + EPS) * nw_ref[...]
        o_ref[0, :, hs] = o * _silu(z_ref[0, :, hs])


def _interleave(first, second, ratio):
    live = [first, second]
    while live:
        for gen, n in ((first, ratio), (second, 1)):
            for _ in range(n):
                if gen in live and next(gen, live) is live:
                    live.remove(gen)


def _delta_kernel(q_ref, k_ref, v_ref, gb_ref, grow_ref, z_ref, nw_ref, o_ref, s_ref, *wy_refs):
    t = pl.program_id(1)
    set_a, set_b = wy_refs[:6], wy_refs[6:]

    @pl.when(t == 0)
    def _():
        s_ref[...] = jnp.zeros(s_ref.shape, F32)
        for ref in set_b:
            ref[...] = jnp.zeros(ref.shape, ref.dtype)

    def step(read_set, write_set):
        _interleave(_delta_prepare(q_ref, k_ref, v_ref, gb_ref, grow_ref, write_set),
                    _delta_recur(read_set, z_ref, nw_ref, o_ref, s_ref), 2)

    @pl.when(lax.rem(t, 2) == 0)
    def _():
        step(set_b, set_a)

    @pl.when(lax.rem(t, 2) == 1)
    def _():
        step(set_a, set_b)


def _delta_rule(q, k, v, gb, grow, z, norm_w):
    b = q.shape[0]
    ts = SEQ_TILE
    nt = SEQ // ts
    nxt = lambda w: pl.BlockSpec((1, ts, w), lambda i, t: (i, jnp.minimum(t, nt - 1), 0))
    cur = lambda w: pl.BlockSpec((1, ts, w), lambda i, t: (i, jnp.maximum(t - 1, 0), 0))
    hd = DN_HEAD_DIM
    wy_set = [pltpu.VMEM((DN_HEADS, ts, hd), F32), pltpu.VMEM((DN_HEADS, ts, hd), BF16),
              pltpu.VMEM((DN_HEADS, ts, ts), BF16), pltpu.VMEM((DN_HEADS, ts, hd), BF16),
              pltpu.VMEM((DN_HEADS, ts, hd), F32), pltpu.VMEM((DN_HEADS, ts, hd), F32)]
    return pl.pallas_call(
        _delta_kernel,
        out_shape=jax.ShapeDtypeStruct((b, SEQ, DN_WIDTH), F32),
        grid=(b, nt + 1),
        in_specs=[
            nxt(DN_WIDTH), nxt(DN_WIDTH), nxt(DN_WIDTH), nxt(LANES),
            pl.BlockSpec((8, ts), lambda i, t: (0, i * nt + jnp.minimum(t, nt - 1))),
            cur(DN_WIDTH),
            pl.BlockSpec((1, DN_HEAD_DIM), lambda i, t: (0, 0)),
        ],
        out_specs=cur(DN_WIDTH),
        scratch_shapes=[pltpu.VMEM((DN_HEADS, hd, hd), F32)] + wy_set + wy_set,
        compiler_params=_params(("parallel", "arbitrary")),
        name="delta_rule",
    )(q, k, v, gb, grow, z, norm_w)


CONV_HALO = 32


def _conformer_kernel(u_ref, w_ref, b_ref, lnw_ref, lnb_ref, o_ref, buf):
    t = pl.program_id(1)
    ts = SEQ_TILE

    @pl.when(t == 0)
    def _():
        buf[0:CONV_HALO, :] = jnp.zeros((CONV_HALO, CONV_WIDTH), F32)

    u = u_ref[0]
    buf[CONV_HALO:CONV_HALO + ts, :] = u[:, :CONV_WIDTH] * _sigmoid(u[:, CONV_WIDTH:])
    base = CONV_HALO - (CONV_KERNEL - 1)
    rows = 64
    for rc in range(ts // rows):
        parts = []
        for cc in range(CONV_WIDTH // LANES):
            cs = slice(cc * LANES, (cc + 1) * LANES)
            r0 = base + rc * rows
            acc = w_ref[0:1, cs] * buf[r0:r0 + rows, cs]
            for j in range(1, CONV_KERNEL):
                acc = acc + w_ref[j:j + 1, cs] * buf[r0 + j:r0 + j + rows, cs]
            parts.append(acc)
        h = jnp.concatenate(parts, axis=1) + b_ref[...]
        mu = jnp.mean(h, axis=-1, keepdims=True)
        var = jnp.mean(jnp.square(h - mu), axis=-1, keepdims=True)
        hn = (h - mu) * lax.rsqrt(var + EPS) * lnw_ref[...] + lnb_ref[...]
        o_ref[0, rc * rows:(rc + 1) * rows, :] = _silu(hn)
    buf[0:CONV_HALO, :] = buf[ts:ts + CONV_HALO, :]


def _conformer(u, dw_w, dw_b, ln_w, ln_b):
    b = u.shape[0]
    ts = SEQ_TILE
    full = lambda shape: pl.BlockSpec(shape, lambda i, t: (0,) * len(shape))
    return pl.pallas_call(
        _conformer_kernel,
        out_shape=jax.ShapeDtypeStruct((b, SEQ, CONV_WIDTH), F32),
        grid=(b, SEQ // ts),
        in_specs=[
            pl.BlockSpec((1, ts, 2 * CONV_WIDTH), lambda i, t: (i, t, 0)),
            full((CONV_KERNEL, CONV_WIDTH)), full((1, CONV_WIDTH)),
            full((1, CONV_WIDTH)), full((1, CONV_WIDTH)),
        ],
        out_specs=pl.BlockSpec((1, ts, CONV_WIDTH), lambda i, t: (i, t, 0)),
        scratch_shapes=[pltpu.VMEM((ts + CONV_HALO, CONV_WIDTH), F32)],
        compiler_params=_params(("parallel", "arbitrary")),
        name="conformer",
    )(u, dw_w, dw_b, ln_w, ln_b)


def _rms_rows(x, w):
    return x * lax.rsqrt(jnp.mean(x * x, axis=-1, keepdims=True) + EPS) * w


def _nsa_prep_kernel(kc_ref, vc_ref, ks_ref, kw_ref, pos_ref, w1_ref, w2_ref, knw_ref,
                     kcmp_ref, vcmp_ref, ksn_ref, kwn_ref):
    hd = NSA_HEAD_DIM

    def compress(x_ref, i):
        u_lo = jnp.zeros((N_CMP_PAD, CMP_HIDDEN), F32)
        u_hi = jnp.zeros((N_CMP_PAD, CMP_HIDDEN), F32)
        for r in range(CMP_STRIDE):
            xr = x_ref[pl.ds(r, N_CMP_PAD, stride=CMP_STRIDE), :]
            lo, hi = r, CMP_STRIDE + r
            u_lo = u_lo + _mm(xr + pos_ref[i, lo:lo + 1, :], w1_ref[i, lo * hd:(lo + 1) * hd, :])
            u_hi = u_hi + _mm(xr + pos_ref[i, hi:hi + 1, :], w1_ref[i, hi * hd:(hi + 1) * hd, :])
        hid = _silu(u_lo + pltpu.roll(u_hi, N_CMP_PAD - 1, axis=0))
        return _mm(hid, w2_ref[i])

    kcmp_ref[0] = _rms_rows(compress(kc_ref, 0), knw_ref[0:1, :])
    vcmp_ref[0] = compress(vc_ref, 1)
    ksn_ref[0] = _rms_rows(ks_ref[...], knw_ref[1:2, :])
    kwn_ref[0] = _rms_rows(kw_ref[...], knw_ref[2:3, :])


def _nsa_prep(kv6, pos, w1, w2, knw):
    b = kv6.shape[1]
    hd = NSA_HEAD_DIM
    full = lambda shape: pl.BlockSpec(shape, lambda i: (0,) * len(shape))
    piece = lambda j: pl.BlockSpec((None, None, SEQ, hd), lambda i: (j, i, 0, 0))
    bspec = lambda r, w: pl.BlockSpec((1, r, w), lambda i: (i, 0, 0))
    return pl.pallas_call(
        _nsa_prep_kernel,
        out_shape=[jax.ShapeDtypeStruct((b, N_CMP_PAD, hd), F32)] * 2
        + [jax.ShapeDtypeStruct((b, SEQ, hd), F32)] * 2,
        grid=(b,),
        in_specs=[
            piece(0), piece(1), piece(2), piece(4),
            full((2, CMP_LEN, hd)), full((2, CMP_LEN * hd, CMP_HIDDEN)), full((2, CMP_HIDDEN, hd)),
            full((3, hd)),
        ],
        out_specs=[bspec(N_CMP_PAD, hd)] * 2 + [bspec(SEQ, hd)] * 2,
        compiler_params=_params(("parallel",)),
        name="nsa_prep",
    )(kv6, kv6, kv6, kv6, pos, w1, w2, knw)


def _nsa_attn_kernel(q_ref, qnw_ref, kcmp_ref, vcmp_ref, ks_ref, vs_ref, kw_ref, vw_ref,
                     gate_ref, bcmp_ref, btab_ref, o_ref,
                     madd_ref, s_ref, ksb_ref, kwb_ref, vsa_ref, vwa_ref):
    i = pl.program_id(1)
    tq = Q_TILE
    nh = NSA_HEADS
    hd = NSA_HEAD_DIM
    t0 = i * tq

    @pl.when(i == 0)
    def _():
        ones = jnp.ones((SEQ, hd), BF16)
        ksb_ref[...] = ks_ref[0].astype(BF16)
        kwb_ref[...] = kw_ref[0].astype(BF16)
        vsa_ref[...] = jnp.concatenate([vs_ref[...].astype(BF16), ones], axis=1)
        vwa_ref[...] = jnp.concatenate([vw_ref[...].astype(BF16), ones], axis=1)

    qs = jnp.concatenate(
        [_rms_rows(q_ref[h], qnw_ref[...]) * hd ** -0.5 for h in range(nh)], axis=0).astype(BF16)

    row = lax.broadcasted_iota(jnp.int32, (tq, LANES), 0)
    lane = lax.broadcasted_iota(jnp.int32, (tq, LANES), 1)
    qpos = t0 + row

    s_all = _mm_nt(qs, kcmp_ref[0])
    cmp_valid = (qpos >= lane * CMP_STRIDE + (CMP_LEN - 1)) & (lane < N_CMP_PAD - 1)
    p_rows = []
    p_sum = jnp.zeros((tq, LANES), F32)
    for h in range(nh):
        s = jnp.where(cmp_valid, s_all[h * tq:(h + 1) * tq] + bcmp_ref[h], NEG_INF)
        m = jnp.max(s, axis=-1, keepdims=True)
        p = jnp.where(cmp_valid, jnp.exp(s - m), 0.0)
        l = jnp.sum(p, axis=-1, keepdims=True)
        p = p * jnp.where(l > 0.0, 1.0 / l, 0.0)
        p_rows.append(p)
        p_sum = p_sum + p
    o_cmp = _mm(jnp.concatenate(p_rows, axis=0), vcmp_ref[0])

    ss = lax.broadcasted_iota(jnp.int32, (N_SLC, N_CMP_PAD), 0)
    jj = lax.broadcasted_iota(jnp.int32, (N_SLC, N_CMP_PAD), 1)
    overlap_t = ((jj * CMP_STRIDE < ss * SLC_BLOCK + SLC_BLOCK)
                 & (jj * CMP_STRIDE + CMP_LEN > ss * SLC_BLOCK) & (jj < N_CMP_PAD - 1))
    overlap_t = jnp.where(overlap_t, 1.0, 0.0).astype(BF16)
    nt = lambda a, b_: lax.dot_general(a, b_, (((1,), (1,)), ((), ())), preferred_element_type=F32)
    p_hi, p_mid, p_lo = _split3(p_sum)
    imp = nt(overlap_t, p_hi) + nt(overlap_t, p_mid) + nt(overlap_t, p_lo)
    blk = lax.broadcasted_iota(jnp.int32, (N_SLC, tq), 0)
    cur = (t0 + lax.broadcasted_iota(jnp.int32, (N_SLC, tq), 1)) // SLC_BLOCK
    causal_blk = blk <= cur
    forced = (blk == 0) | (blk == cur) | (blk == cur - 1)
    imp = jnp.where(causal_blk & forced, FORCE, jnp.where(causal_blk, imp, -1.0))
    rank = jnp.zeros((N_SLC, tq), jnp.int32)
    for s2 in range(N_SLC):
        other = jnp.broadcast_to(imp[s2:s2 + 1, :], (N_SLC, tq))
        beats = (other > imp) | ((other == imp) & (blk > s2))
        rank = rank + jnp.where(beats, 1, 0)
    sel_t = jnp.where(rank < SLC_TOP_N, 1.0, 0.0).astype(BF16)
    er = lax.broadcasted_iota(jnp.int32, (N_SLC, SEQ), 0)
    ec = lax.broadcasted_iota(jnp.int32, (N_SLC, SEQ), 1)
    expand = jnp.where(ec // SLC_BLOCK == er, 1.0, 0.0).astype(BF16)
    key = lax.broadcasted_iota(jnp.int32, (tq, SEQ), 1)
    qall = t0 + lax.broadcasted_iota(jnp.int32, (tq, SEQ), 0)
    chosen = (_mm_tn(sel_t, expand) > 0.5) & (qall >= key)
    madd_ref[...] = jnp.where(chosen, 0.0, NEG_INF)

    neg = jnp.full((nh * tq, tq), NEG_INF, F32)
    zero = jnp.zeros((nh * tq, LANES), F32)

    def scores(k_tile, bias_idx, add):
        s_t = _mm_nt(qs, k_tile)
        return jnp.concatenate(
            [s_t[h * tq:(h + 1) * tq] + (btab_ref[h, bias_idx] + add) for h in range(nh)], axis=0)

    def normalise(acc):
        return acc[:, :hd] * (1.0 / _bcast_col(acc, hd, hd))

    n_pairs = (i + 2) // 2

    def slc_pass1(j, m_run):
        for u in range(2):
            kt = 2 * j + u
            k0 = pl.multiple_of(kt * tq, tq)
            s_m = scores(ksb_ref[pl.ds(k0, tq), :], jnp.clip(i - kt, 0, 2), madd_ref[:, pl.ds(k0, tq)])
            s_ref[:, pl.ds(k0, tq)] = s_m
            m_run = jnp.maximum(m_run, s_m)
        return m_run

    m_slc = jnp.broadcast_to(
        jnp.max(lax.fori_loop(0, n_pairs, slc_pass1, neg), axis=-1, keepdims=True), (nh * tq, tq))

    def slc_pass2(j, acc):
        for u in range(2):
            k0 = pl.multiple_of((2 * j + u) * tq, tq)
            p = jnp.exp(s_ref[:, pl.ds(k0, tq)] - m_slc).astype(BF16)
            acc = acc + jnp.dot(p, vsa_ref[pl.ds(k0, tq), :], preferred_element_type=F32)
        return acc

    o_slc = normalise(lax.fori_loop(0, n_pairs, slc_pass2, zero))

    kcol = lax.broadcasted_iota(jnp.int32, (tq, tq), 1)
    qrow = t0 + lax.broadcasted_iota(jnp.int32, (tq, tq), 0)
    n_win = WINDOW // tq + 1
    starts = []
    m_run = neg
    for j in range(n_win):
        d = n_win - 1 - j
        k0 = pl.multiple_of(jnp.maximum(i - d, 0) * tq, tq)
        starts.append(k0)
        dist = qrow - (k0 + kcol)
        ok = (dist >= 0) & (dist < WINDOW) & (i >= d)
        s_m = scores(kwb_ref[pl.ds(k0, tq), :], min(d, 2), jnp.where(ok, 0.0, NEG_INF))
        s_ref[:, j * tq:(j + 1) * tq] = s_m
        m_run = jnp.maximum(m_run, s_m)
    m_win = jnp.broadcast_to(jnp.max(m_run, axis=-1, keepdims=True), (nh * tq, tq))
    acc = zero
    for j in range(n_win):
        p = jnp.exp(s_ref[:, j * tq:(j + 1) * tq] - m_win).astype(BF16)
        acc = acc + jnp.dot(p, vwa_ref[pl.ds(starts[j], tq), :], preferred_element_type=F32)
    o_win = normalise(acc)

    gates = _sigmoid(gate_ref[0])
    for h in range(nh):
        hs = slice(h * tq, (h + 1) * tq)
        g0 = 2 * DN_HEADS + 3 * h
        o_ref[h] = (_bcast_col(gates, g0, hd) * o_cmp[hs]
                       + _bcast_col(gates, g0 + 1, hd) * o_slc[hs]
                       + _bcast_col(gates, g0 + 2, hd) * o_win[hs])


def _nsa_attn(q, qnw, kcmp, vcmp, ks, kv6, kw, small, bias_cmp, btab):
    b = kv6.shape[1]
    tq = Q_TILE
    nq = SEQ // tq
    nh, hd = NSA_HEADS, NSA_HEAD_DIM
    full = lambda shape: pl.BlockSpec(shape, lambda bi, i: (0,) * len(shape))
    per_b = lambda r: pl.BlockSpec((1, r, hd), lambda bi, i: (bi, 0, 0))
    piece = lambda j: pl.BlockSpec((None, None, SEQ, hd), lambda bi, i: (j, bi, 0, 0))
    heads = pl.BlockSpec((nh, tq, hd), lambda bi, i: (0, bi * nq + i, 0))
    return pl.pallas_call(
        _nsa_attn_kernel,
        out_shape=jax.ShapeDtypeStruct((nh, b * SEQ, hd), F32),
        grid=(b, nq),
        in_specs=[
            heads,
            full((1, hd)),
            per_b(N_CMP_PAD), per_b(N_CMP_PAD), per_b(SEQ), piece(3), per_b(SEQ), piece(5),
            pl.BlockSpec((1, tq, LANES), lambda bi, i: (bi, i, 0)),
            pl.BlockSpec((nh, tq, N_CMP_PAD), lambda bi, i: (0, i, 0)),
            full((nh, 3, tq, tq)),
        ],
        out_specs=heads,
        scratch_shapes=[pltpu.VMEM((tq, SEQ), F32), pltpu.VMEM((nh * tq, SEQ), F32),
                        pltpu.VMEM((SEQ, hd), BF16), pltpu.VMEM((SEQ, hd), BF16),
                        pltpu.VMEM((SEQ, 2 * hd), BF16), pltpu.VMEM((SEQ, 2 * hd), BF16)],
        compiler_params=_params(("parallel", "arbitrary")),
        name="nsa_attn",
    )(q, qnw, kcmp, vcmp, ks, kv6, kw, kv6, small, bias_cmp, btab)


def _out_router_kernel(ya_ref, yb_ref, yc_ref, x_ref, wo_ref, fnw_ref, rw_ref, rb_ref,
                       xo_ref, h_ref, idx_ref, wt_ref, hist_ref):
    wa = DN_WIDTH
    wb = wa + NSA_WIDTH
    y = (jnp.dot(ya_ref[...].astype(BF16), wo_ref[0:wa, :], preferred_element_type=F32)
         + jnp.dot(yc_ref[...].astype(BF16), wo_ref[wb:, :], preferred_element_type=F32))
    for hh in range(NSA_HEADS):
        r0 = wa + hh * NSA_HEAD_DIM
        y = y + jnp.dot(yb_ref[hh].astype(BF16), wo_ref[r0:r0 + NSA_HEAD_DIM, :],
                        preferred_element_type=F32)
    xn = x_ref[...] + y
    xo_ref[...] = xn
    h = xn * lax.rsqrt(jnp.mean(xn * xn, axis=-1, keepdims=True) + EPS) * fnw_ref[...]
    _store_tile_rows(h_ref, h, h.shape[0])
    logits = _dot_f32(h, rw_ref[...]) + rb_ref[...]
    lane = lax.broadcasted_iota(jnp.int32, logits.shape, 1)
    vals, idxs = [], []
    for _ in range(TOP_K):
        m = jnp.max(logits, axis=-1, keepdims=True)
        ix = jnp.min(jnp.where(logits == m, lane, LANES), axis=-1, keepdims=True)
        vals.append(m)
        idxs.append(ix)
        logits = jnp.where(lane == ix, -jnp.inf, logits)
    es = [jnp.exp(v - vals[0]) for v in vals]
    inv = 1.0 / (es[0] + es[1] + es[2] + es[3])
    idx_out = jnp.zeros(lane.shape, jnp.int32)
    wt_out = jnp.zeros(lane.shape, F32)
    chosen = jnp.zeros(lane.shape, F32)
    for k in range(TOP_K):
        idx_out = jnp.where(lane == k, idxs[k], idx_out)
        wt_out = jnp.where(lane == k, es[k] * inv, wt_out)
        chosen = chosen + jnp.where(lane == idxs[k], 1.0, 0.0)
    idx_ref[...] = idx_out
    wt_ref[...] = wt_out
    hist_ref[0] = jnp.broadcast_to(jnp.sum(chosen, axis=0, keepdims=True), (8, LANES))


def _out_router(ya, yb, yc, x2, w_out, fnw, rw, rb):
    n = x2.shape[0]
    tm = ROW_TILE
    row = lambda w: pl.BlockSpec((tm, w), lambda i: (i, 0))
    full = lambda shape: pl.BlockSpec(shape, lambda i: (0,) * len(shape))
    return pl.pallas_call(
        _out_router_kernel,
        out_shape=[jax.ShapeDtypeStruct((n, D_MODEL), F32), jax.ShapeDtypeStruct((n * SUBLANES, LANES), F32),
                   jax.ShapeDtypeStruct((n, LANES), jnp.int32), jax.ShapeDtypeStruct((n, LANES), F32),
                   jax.ShapeDtypeStruct((n // tm, 8, LANES), F32)],
        grid=(n // tm,),
        in_specs=[row(DN_WIDTH), pl.BlockSpec((NSA_HEADS, tm, NSA_HEAD_DIM), lambda i: (0, i, 0)),
                  row(CONV_WIDTH), row(D_MODEL),
                  full((D_MODEL, D_MODEL)), full((1, D_MODEL)), full((D_MODEL, LANES)), full((1, LANES))],
        out_specs=[row(D_MODEL), pl.BlockSpec((tm * SUBLANES, LANES), lambda i: (i, 0)), row(LANES), row(LANES),
                   pl.BlockSpec((1, 8, LANES), lambda i: (i, 0, 0))],
        compiler_params=_params(("parallel",)),
        name="out_router",
    )(ya, yb, yc, x2, w_out, fnw, rw, rb)


def _slots_kernel(idx_ref, base_ref, slot_ref):
    tm = ROW_TILE
    idx = idx_ref[...]
    lane = lax.broadcasted_iota(jnp.int32, (tm, LANES), 1)
    onehots = [jnp.where(lane == _bcast_col(idx, k, LANES), 1.0, 0.0) for k in range(TOP_K)]
    cnt = (onehots[0] + onehots[1]) + (onehots[2] + onehots[3])
    r = lax.broadcasted_iota(jnp.int32, (tm, tm), 0)
    c = lax.broadcasted_iota(jnp.int32, (tm, tm), 1)
    earlier = jnp.where(r > c, 1.0, 0.0).astype(BF16)
    rank = jnp.dot(earlier, cnt.astype(BF16), preferred_element_type=F32) + base_ref[0, 0:1, :]
    ones = jnp.ones((8, LANES), BF16)
    row8 = lax.broadcasted_iota(jnp.int32, (8, tm), 0)
    out = jnp.zeros((8, tm), F32)
    for k in range(TOP_K):
        hi, mid, lo = _split3(rank * onehots[k])
        d = lambda p: lax.dot_general(ones, p, (((1,), (1,)), ((), ())), preferred_element_type=F32)
        out = jnp.where(row8 == k, d(hi) + d(mid) + d(lo), out)
    slot_ref[...] = out.astype(jnp.int32)


def _slots(idx, base3):
    n = idx.shape[0]
    tm = ROW_TILE
    return pl.pallas_call(
        _slots_kernel,
        out_shape=jax.ShapeDtypeStruct((8, n), jnp.int32),
        grid=(n // tm,),
        in_specs=[pl.BlockSpec((tm, LANES), lambda i: (i, 0)),
                  pl.BlockSpec((1, 8, LANES), lambda i: (i, 0, 0))],
        out_specs=pl.BlockSpec((8, tm), lambda i: (0, i)),
        compiler_params=_params(("parallel",)),
        name="moe_slots",
    )(idx, base3)


def _dispatch_kernel(slot_ref, zoff_ref, nv_ref, h_ref, xs_hbm, zbuf, sem, zsem):
    i = pl.program_id(0)
    tm = ROW_TILE
    n = slot_ref.shape[0] // TOP_K
    base = i * tm

    @pl.when(i == 0)
    def _():
        zbuf[...] = jnp.zeros(zbuf.shape, F32)
        blk = MOE_TILE * SUBLANES
        n_blocks = xs_hbm.shape[0] // blk

        def fill(row0):
            return pltpu.make_async_copy(
                zbuf, xs_hbm.at[pl.ds(pl.multiple_of(row0 * SUBLANES, blk), blk)], zsem)

        def fill_region_end(e, c):
            fill(zoff_ref[e]).start()
            return c

        def fill_tail(b, c):
            fill(b * MOE_TILE).start()
            return c

        def wait_fill(b, c):
            fill(0).wait()
            return c

        lax.fori_loop(0, N_EXPERTS, fill_region_end, 0)
        lax.fori_loop(nv_ref[0], n_blocks, fill_tail, 0)
        lax.fori_loop(0, N_EXPERTS + n_blocks - nv_ref[0], wait_fill, 0)

    def body(r, c):
        src = h_ref.at[pl.ds(pl.multiple_of(r * SUBLANES, SUBLANES), SUBLANES)]
        for k in range(TOP_K):
            row0 = pl.multiple_of(slot_ref[k * n + base + r] * SUBLANES, SUBLANES)
            pltpu.make_async_copy(src, xs_hbm.at[pl.ds(row0, SUBLANES)], sem).start(priority=k % 2)
        return c

    lax.fori_loop(0, tm, body, 0, unroll=4)
    done = xs_hbm.at[pl.ds(0, TOP_K * tm * SUBLANES)]
    pltpu.make_async_copy(done, done, sem).wait()


def _dispatch(slot_flat, zero_off, n_valid, h, n_slots):
    n = h.shape[0] // SUBLANES
    tm = ROW_TILE
    return pl.pallas_call(
        _dispatch_kernel,
        out_shape=jax.ShapeDtypeStruct((n_slots * SUBLANES, LANES), F32),
        grid_spec=pltpu.PrefetchScalarGridSpec(
            num_scalar_prefetch=3,
            grid=(n // tm,),
            in_specs=[pl.BlockSpec((tm * SUBLANES, LANES), lambda i, s, z, nv: (i, 0))],
            out_specs=pl.BlockSpec(memory_space=pl.ANY),
            scratch_shapes=[pltpu.VMEM((MOE_TILE * SUBLANES, LANES), F32),
                            pltpu.SemaphoreType.DMA, pltpu.SemaphoreType.DMA],
        ),
        compiler_params=_params(("arbitrary",)),
        name="moe_dispatch",
    )(slot_flat, zero_off, n_valid, h)


def _expert_kernel(be_ref, nv_ref, xs_ref, wgu_ref, bgu_ref, wd_ref, bd_ref, o_ref, wgu_bf, wd_bf):
    i = pl.program_id(0)
    new_expert = (i == 0) | (be_ref[i] != be_ref[jnp.maximum(i - 1, 0)])

    @pl.when((i < nv_ref[0]) & new_expert)
    def _():
        step = 512
        for c0 in range(0, 2 * D_FF, step):
            wgu_bf[:, c0:c0 + step] = wgu_ref[:, c0:c0 + step].astype(BF16)
        for c0 in range(0, D_MODEL, step):
            wd_bf[:, c0:c0 + step] = wd_ref[:, c0:c0 + step].astype(BF16)

    @pl.when(i < nv_ref[0])
    def _():
        xb = _load_tile_rows(xs_ref, MOE_TILE).astype(BF16)

        def gate_up(f0):
            g = jnp.dot(xb, wgu_bf[:, f0:f0 + FF_CHUNK], preferred_element_type=F32)
            u = jnp.dot(xb, wgu_bf[:, D_FF + f0:D_FF + f0 + FF_CHUNK], preferred_element_type=F32)
            return (g + bgu_ref[:, f0:f0 + FF_CHUNK], u + bgu_ref[:, D_FF + f0:D_FF + f0 + FF_CHUNK])

        y = None
        nxt = gate_up(0)
        for f0 in range(0, D_FF, FF_CHUNK):
            g, u = nxt
            if f0 + FF_CHUNK < D_FF:
                nxt = gate_up(f0 + FF_CHUNK)
            gate = jnp.minimum(g, SWIGLU_LIMIT)
            up = jnp.clip(u, -SWIGLU_LIMIT, SWIGLU_LIMIT)
            act = ((up + 1.0) * gate * _sigmoid(SWIGLU_ALPHA * gate)).astype(BF16)
            part = jnp.dot(act, wd_bf[f0:f0 + FF_CHUNK, :], preferred_element_type=F32)
            y = part if y is None else y + part
        _store_tile_rows(o_ref, y + bd_ref[...], MOE_TILE)

    @pl.when(i >= nv_ref[0])
    def _():
        o_ref[...] = jnp.zeros(o_ref.shape, F32)


def _expert_ffn(layer, block_expert, n_valid, xs, wgu, bgu, wd, bd):
    n_slots = xs.shape[0] // SUBLANES
    tm = MOE_TILE
    return pl.pallas_call(
        _expert_kernel,
        out_shape=jax.ShapeDtypeStruct(xs.shape, F32),
        grid_spec=pltpu.PrefetchScalarGridSpec(
            num_scalar_prefetch=2,
            grid=(n_slots // tm,),
            in_specs=[
                pl.BlockSpec((tm * SUBLANES, LANES), lambda i, be, nv: (i, 0)),
                pl.BlockSpec((None, None, D_MODEL, 2 * D_FF), lambda i, be, nv: (layer, be[i], 0, 0)),
                pl.BlockSpec((None, None, 1, 2 * D_FF), lambda i, be, nv: (layer, be[i], 0, 0)),
                pl.BlockSpec((None, None, D_FF, D_MODEL), lambda i, be, nv: (layer, be[i], 0, 0)),
                pl.BlockSpec((None, None, 1, D_MODEL), lambda i, be, nv: (layer, be[i], 0, 0)),
            ],
            out_specs=pl.BlockSpec((tm * SUBLANES, LANES), lambda i, be, nv: (i, 0)),
            scratch_shapes=[pltpu.VMEM((D_MODEL, 2 * D_FF), BF16), pltpu.VMEM((D_FF, D_MODEL), BF16)],
        ),
        compiler_params=pltpu.CompilerParams(dimension_semantics=("arbitrary",),
                                             vmem_limit_bytes=EXPERT_VMEM_LIMIT_BYTES),
        name="expert_ffn",
    )(block_expert, n_valid, xs, wgu, bgu, wd, bd)


def _combine_kernel(slot_ref, ys_hbm, x_ref, wt_ref, o_ref, buf, sems):
    i = pl.program_id(0)
    tc = COMBINE_TILE
    nsteps = pl.num_programs(0)
    n = slot_ref.shape[0] // TOP_K
    cur = lax.rem(i, 2)

    def issue(step, par):
        base = step * tc

        def body(r, c):
            dst0 = pl.multiple_of(r * SUBLANES, SUBLANES)
            for k in range(TOP_K):
                row0 = pl.multiple_of(slot_ref[k * n + base + r] * SUBLANES, SUBLANES)
                pltpu.make_async_copy(ys_hbm.at[pl.ds(row0, SUBLANES)],
                                      buf.at[par, k, pl.ds(dst0, SUBLANES)],
                                      sems.at[par]).start(priority=k % 2)
            return c

        lax.fori_loop(0, tc, body, 0, unroll=4)

    @pl.when(i == 0)
    def _():
        issue(0, 0)

    @pl.when(i + 1 < nsteps)
    def _():
        issue(i + 1, 1 - cur)

    pltpu.make_async_copy(buf.at[cur], buf.at[cur], sems.at[cur]).wait()
    w = [_bcast_col(wt_ref[...], k, LANES) for k in range(TOP_K)]
    for s in range(ROW_CHUNKS):
        cs = slice(s * LANES, (s + 1) * LANES)
        acc = x_ref[:, cs]
        for k in range(TOP_K):
            acc = acc + w[k] * buf[cur, k, pl.ds(s, tc, stride=SUBLANES), :]
        o_ref[:, cs] = acc


def _combine(slot_flat, ys, x2, wt):
    n = x2.shape[0]
    tc = COMBINE_TILE
    return pl.pallas_call(
        _combine_kernel,
        out_shape=jax.ShapeDtypeStruct((n, D_MODEL), F32),
        grid_spec=pltpu.PrefetchScalarGridSpec(
            num_scalar_prefetch=1,
            grid=(n // tc,),
            in_specs=[pl.BlockSpec(memory_space=pl.ANY),
                      pl.BlockSpec((tc, D_MODEL), lambda i, s: (i, 0)),
                      pl.BlockSpec((tc, LANES), lambda i, s: (i, 0))],
            out_specs=pl.BlockSpec((tc, D_MODEL), lambda i, s: (i, 0)),
            scratch_shapes=[pltpu.VMEM((2, TOP_K, tc * SUBLANES, LANES), F32), pltpu.SemaphoreType.DMA((2,))],
        ),
        compiler_params=_params(("arbitrary",)),
        name="moe_combine",
    )(slot_flat, ys, x2, wt)


def _moe(layer, h, x2, idx, wt, hist3, wgu, bgu, wd, bd):
    n = x2.shape[0]
    tm = MOE_TILE
    n_slots = n * TOP_K + N_EXPERTS * tm
    n_blocks = n_slots // tm
    hist = hist3[:, 0, :N_EXPERTS]
    counts = jnp.sum(hist, axis=0).astype(jnp.int32)
    padded = (counts + tm - 1) // tm * tm
    pad_end = jnp.cumsum(padded)
    pad_start = pad_end - padded
    tile_base = pad_start[None, :].astype(F32) + (jnp.cumsum(hist, axis=0) - hist)
    base3 = jnp.broadcast_to(jnp.pad(tile_base, ((0, 0), (0, LANES - N_EXPERTS)))[:, None, :],
                             (hist.shape[0], 8, LANES))
    blk0 = jnp.arange(n_blocks) * tm
    block_expert = jnp.minimum(jnp.sum(blk0[:, None] >= pad_end[None, :], axis=1), N_EXPERTS - 1).astype(jnp.int32)
    n_valid = (pad_end[-1:] // tm).astype(jnp.int32)
    zero_off = jnp.maximum(pad_end - tm, 0).astype(jnp.int32)

    slot_flat = _slots(idx, base3)[:TOP_K].reshape(-1)
    xs = _dispatch(slot_flat, zero_off, n_valid, h, n_slots)
    ys = _expert_ffn(layer, block_expert, n_valid, xs, wgu, bgu, wd, bd)
    return _combine(slot_flat, ys, x2, wt)


def _t5_bucket(dist):
    n = jnp.maximum(dist, 0)
    max_exact = REL_BUCKETS // 2
    nf = jnp.maximum(n, 1).astype(F32)
    large = max_exact + (jnp.log(nf / max_exact) / math.log(REL_MAX_DIST / max_exact)
                         * (REL_BUCKETS - max_exact)).astype(jnp.int32)
    large = jnp.minimum(large, REL_BUCKETS - 1)
    return jnp.where(n < max_exact, n, large)


def _bias_tables(rel_bias):
    rel_bias = rel_bias.astype(F32)
    tq = Q_TILE

    def lookup(dist):
        bucket = _t5_bucket(dist)
        out = jnp.zeros((NSA_HEADS,) + dist.shape, F32)
        for bk in range(REL_BUCKETS):
            out = jnp.where(bucket[None] == bk, rel_bias[bk].reshape((NSA_HEADS,) + (1,) * dist.ndim), out)
        return out

    t_pos = jnp.arange(SEQ)
    cmp_end = jnp.arange(N_CMP_PAD) * CMP_STRIDE + CMP_LEN - 1
    bias_cmp = lookup(t_pos[:, None] - cmp_end[None, :])
    rr = jnp.arange(tq)[:, None] - jnp.arange(tq)[None, :]
    btab = lookup(jnp.stack([rr, rr + tq, rr + 2 * tq]))
    return bias_cmp, btab


def _layer(layer, x2, b, p, experts, bias_cmp, btab):
    n = x2.shape[0]
    w = p['w_in']
    o_a, o_b, o_q, o_kv, o_g, o_u = 2048, 2052, 2056, 2312, 2696, 2708
    w_small = jnp.concatenate([w[:, o_a:o_q], w[:, o_g:o_u],
                               jnp.zeros((D_MODEL, LANES - 8 - 3 * NSA_HEADS), F32)], axis=1)
    w_cat = jnp.concatenate([w[:, :o_a], w[:, o_q:o_kv], w[:, o_kv:o_g], w[:, o_u:], w_small],
                            axis=1).astype(BF16)
    w_abt = jnp.concatenate([w[:, o_a:o_q].T, jnp.zeros((8, D_MODEL), F32)], axis=0).astype(BF16)
    qkv, z, nq, nkv, cu, small, abt = _in_proj(x2, p['attn_norm_w'][None, :], w_cat, w_abt)

    pcol = jnp.zeros((8, LANES), F32).at[0, :DN_HEADS].set(p['dn_a_log']).at[1, :DN_HEADS].set(p['dn_dt_bias'])
    prow = jnp.zeros((16, LANES), F32).at[:DN_HEADS, 0].set(p['dn_a_log']).at[:DN_HEADS, 1].set(p['dn_dt_bias'])
    small3 = small.reshape(b, SEQ, LANES)
    dq, dk, dv, gb, grow = _dn_prep(qkv.reshape(b, SEQ, 3 * DN_WIDTH), small3, abt,
                                    p['dn_conv_w'], pcol, prow)
    y_a = _delta_rule(dq, dk, dv, gb, grow, z.reshape(b, SEQ, DN_WIDTH), p['dn_norm_w'][None, :])

    y_c = _conformer(cu.reshape(b, SEQ, 2 * CONV_WIDTH), p['conv_dw_w'], p['conv_dw_b'][None, :],
                     p['conv_ln_w'][None, :], p['conv_ln_b'][None, :])

    kv6 = nkv.reshape(6, b, SEQ, NSA_HEAD_DIM)
    kcmp, vcmp, ksn, kwn = _nsa_prep(kv6, p['nsa_cmp_pos'], p['nsa_cmp_w1'].astype(BF16),
                                     p['nsa_cmp_w2'].astype(BF16), p['nsa_k_norm_w'])
    y_b = _nsa_attn(nq, p['nsa_q_norm_w'][None, :], kcmp, vcmp, ksn, kv6, kwn, small3, bias_cmp, btab)

    rw = jnp.concatenate([p['router_w'], jnp.zeros((D_MODEL, LANES - N_EXPERTS), F32)], axis=1)
    rb = jnp.concatenate([p['router_b'], jnp.full((LANES - N_EXPERTS,), NEG_INF, F32)])[None, :]
    x_new, h, idx, wt, hist3 = _out_router(y_a.reshape(n, DN_WIDTH), y_b, y_c.reshape(n, CONV_WIDTH), x2,
                                           p['w_out'].astype(BF16), p['ffn_norm_w'][None, :], rw, rb)
    return _moe(layer, h, x_new, idx, wt, hist3, *experts)


def kernel(x, attn_norm_w, w_in, dn_conv_w, dn_a_log, dn_dt_bias, dn_norm_w, nsa_q_norm_w, nsa_k_norm_w, nsa_cmp_pos, nsa_cmp_w1, nsa_cmp_w2, conv_dw_w, conv_dw_b, conv_ln_w, conv_ln_b, w_out, ffn_norm_w, router_w, router_b, w_gate_up, b_gate_up, w_down, b_down, rel_bias):
    b, t, d = x.shape
    assert (t, d) == (SEQ, D_MODEL)
    stacked = dict(attn_norm_w=attn_norm_w, w_in=w_in, dn_conv_w=dn_conv_w, dn_a_log=dn_a_log,
                   dn_dt_bias=dn_dt_bias, dn_norm_w=dn_norm_w, nsa_q_norm_w=nsa_q_norm_w,
                   nsa_k_norm_w=nsa_k_norm_w, nsa_cmp_pos=nsa_cmp_pos, nsa_cmp_w1=nsa_cmp_w1,
                   nsa_cmp_w2=nsa_cmp_w2, conv_dw_w=conv_dw_w, conv_dw_b=conv_dw_b,
                   conv_ln_w=conv_ln_w, conv_ln_b=conv_ln_b, w_out=w_out, ffn_norm_w=ffn_norm_w,
                   router_w=router_w, router_b=router_b)
    experts = (w_gate_up, b_gate_up[:, :, None, :], w_down, b_down[:, :, None, :])
    bias_cmp, btab = _bias_tables(rel_bias)
    x2 = x.reshape(b * t, d)
    for l in range(w_in.shape[0]):
        x2 = _layer(l, x2, b, {k: v[l] for k, v in stacked.items()}, experts, bias_cmp, btab)
    return x2.reshape(b, t, d)
```

```python
import functools
import math

import numpy as np
import jax
import jax.numpy as jnp
from jax import lax
from jax.experimental import pallas as pl
from jax.experimental.pallas import tpu as pltpu

F32 = jnp.float32
BF16 = jnp.bfloat16

D_MODEL = 1024
SEQ = 2048
DN_HEADS = 4
DN_HEAD_DIM = 128
DN_WIDTH = DN_HEADS * DN_HEAD_DIM
DN_CONV = 4
DN_CHUNK = 64
NSA_HEADS = 4
NSA_HEAD_DIM = 64
NSA_WIDTH = NSA_HEADS * NSA_HEAD_DIM
CMP_LEN = 32
CMP_STRIDE = 16
CMP_HIDDEN = 2 * NSA_HEAD_DIM
SLC_BLOCK = 64
SLC_TOP_N = 16
WINDOW = 512
CONV_WIDTH = 256
CONV_KERNEL = 31
REL_BUCKETS = 32
REL_MAX_DIST = 128
N_EXPERTS = 32
TOP_K = 4
D_FF = D_MODEL
SWIGLU_LIMIT = 7.0
SWIGLU_ALPHA = 1.702
EPS = 1e-6
NEG_INF = -1e30
FORCE = 1e4

LANES = 128
VMEM_LIMIT_BYTES = 48 * 1024 * 1024
EXPERT_VMEM_LIMIT_BYTES = 56 * 1024 * 1024

ROW_TILE = 512
SEQ_TILE = 256
Q_TILE = 128
MOE_TILE = 512
COMBINE_TILE = 256

N_CMP_PAD = 128
N_SLC = SEQ // SLC_BLOCK


def _params(sem=None):
    return pltpu.CompilerParams(dimension_semantics=sem, vmem_limit_bytes=VMEM_LIMIT_BYTES)


def _mm(a, b):
    return jnp.dot(a.astype(BF16), b.astype(BF16), preferred_element_type=F32)


def _mm_nt(a, b):
    return lax.dot_general(a.astype(BF16), b.astype(BF16), (((1,), (1,)), ((), ())),
                           preferred_element_type=F32)


def _mm_tn(a, b):
    return lax.dot_general(a.astype(BF16), b.astype(BF16), (((0,), (0,)), ((), ())),
                           preferred_element_type=F32)


def _split3(x):
    hi = x.astype(BF16)
    r1 = x - hi.astype(F32)
    mid = r1.astype(BF16)
    lo = (r1 - mid.astype(F32)).astype(BF16)
    return hi, mid, lo


def _dot01_right(x, m01):
    hi, mid, lo = _split3(x)
    d = lambda p: jnp.dot(p, m01, preferred_element_type=F32)
    return d(hi) + d(mid) + d(lo)


def _dot01_left(m01, x):
    hi, mid, lo = _split3(x)
    d = lambda p: jnp.dot(m01, p, preferred_element_type=F32)
    return d(hi) + d(mid) + d(lo)


def _dot_f32(a, b):
    a_hi = a.astype(BF16)
    a_lo = (a - a_hi.astype(F32)).astype(BF16)
    b_hi = b.astype(BF16)
    b_lo = (b - b_hi.astype(F32)).astype(BF16)
    d = lambda p, q: jnp.dot(p, q, preferred_element_type=F32)
    return d(a_hi, b_hi) + d(a_hi, b_lo) + d(a_lo, b_hi) + d(a_lo, b_lo)


def _sigmoid(x):
    return 1.0 / (1.0 + jnp.exp(-x))


def _silu(x):
    return x * _sigmoid(x)


def _softplus(x):
    return jnp.maximum(x, 0.0) + jnp.log(1.0 + jnp.exp(-jnp.abs(x)))


SUBLANES = 8
ROW_CHUNKS = D_MODEL // LANES
assert ROW_CHUNKS == SUBLANES


def _store_tile_rows(ref, value, rows):
    for s in range(ROW_CHUNKS):
        ref[pl.ds(s, rows, stride=SUBLANES), :] = value[:, s * LANES:(s + 1) * LANES]


def _load_tile_rows(ref, rows):
    return jnp.concatenate([ref[pl.ds(s, rows, stride=SUBLANES), :] for s in range(ROW_CHUNKS)], axis=1)


def _bcast_col(x, j, width):
    return jnp.broadcast_to(x[:, j:j + 1], (x.shape[0], width))


IN_SEGS = (3 * DN_WIDTH, DN_WIDTH, NSA_WIDTH, 6 * NSA_HEAD_DIM, 2 * CONV_WIDTH, LANES)
IN_COLS = sum(IN_SEGS)


def _in_proj_kernel(x_ref, nw_ref, w_ref, wabt_ref,
                    qkv_ref, z_ref, nq_ref, nkv_ref, cu_ref, small_ref, abt_ref):
    xf = x_ref[...]
    ms = jnp.mean(xf * xf, axis=-1, keepdims=True)
    hb = (xf * lax.rsqrt(ms + EPS) * nw_ref[...]).astype(BF16)
    hd = NSA_HEAD_DIM
    off = 0
    for ref, width in zip((qkv_ref, z_ref, nq_ref, nkv_ref, cu_ref, small_ref), IN_SEGS):
        res = jnp.dot(hb, w_ref[:, off:off + width], preferred_element_type=F32)
        if ref is nq_ref or ref is nkv_ref:
            for j in range(width // hd):
                ref[j] = res[:, j * hd:(j + 1) * hd]
        else:
            ref[...] = res
        off += width
    abt_ref[...] = lax.dot_general(wabt_ref[...], hb, (((1,), (1,)), ((), ())),
                                   preferred_element_type=F32)


def _in_proj(x2, norm_w, w_cat, w_abt):
    n = x2.shape[0]
    tm = ROW_TILE
    hd = NSA_HEAD_DIM
    pieces = {2: NSA_HEADS, 3: 6}
    out_shape, out_specs = [], []
    for k, w in enumerate(IN_SEGS):
        if k in pieces:
            out_shape.append(jax.ShapeDtypeStruct((pieces[k], n, hd), F32))
            out_specs.append(pl.BlockSpec((pieces[k], tm, hd), lambda i: (0, i, 0)))
        else:
            out_shape.append(jax.ShapeDtypeStruct((n, w), F32))
            out_specs.append(pl.BlockSpec((tm, w), lambda i: (i, 0)))
    out_shape.append(jax.ShapeDtypeStruct((16, n), F32))
    out_specs.append(pl.BlockSpec((16, tm), lambda i: (0, i)))
    return pl.pallas_call(
        _in_proj_kernel,
        out_shape=out_shape,
        grid=(n // tm,),
        in_specs=[
            pl.BlockSpec((tm, D_MODEL), lambda i: (i, 0)),
            pl.BlockSpec((1, D_MODEL), lambda i: (0, 0)),
            pl.BlockSpec((D_MODEL, IN_COLS), lambda i: (0, 0)),
            pl.BlockSpec((16, D_MODEL), lambda i: (0, 0)),
        ],
        out_specs=out_specs,
        compiler_params=_params(("parallel",)),
        name="in_proj",
    )(x2, norm_w, w_cat, w_abt)


def _dn_prep_kernel(qkv_ref, small_ref, abt_ref, cw_ref, pcol_ref, prow_ref,
                    q_ref, k_ref, v_ref, gb_ref, grow_ref, buf):
    t = pl.program_id(1)
    ts = SEQ_TILE

    @pl.when(t == 0)
    def _():
        buf[0:8, :] = jnp.zeros((8, 3 * DN_WIDTH), F32)

    buf[8:8 + ts, :] = qkv_ref[0]
    for c in range(3 * DN_HEADS):
        cs = slice(c * LANES, (c + 1) * LANES)
        acc = cw_ref[0:1, cs] * buf[5:5 + ts, cs]
        for j in range(1, DN_CONV):
            acc = acc + cw_ref[j:j + 1, cs] * buf[5 + j:5 + j + ts, cs]
        y = _silu(acc)
        if c < 2 * DN_HEADS:
            y = y * lax.rsqrt(jnp.sum(y * y, axis=-1, keepdims=True) + EPS)
        if c < DN_HEADS:
            q_ref[0, :, cs] = y * DN_HEAD_DIM ** -0.5
        elif c < 2 * DN_HEADS:
            k_ref[0, :, (c - DN_HEADS) * LANES:(c - DN_HEADS + 1) * LANES] = y
        else:
            v_ref[0, :, (c - 2 * DN_HEADS) * LANES:(c - 2 * DN_HEADS + 1) * LANES] = y
    buf[0:8, :] = buf[ts:ts + 8, :]

    sm = small_ref[0]
    lane = lax.broadcasted_iota(jnp.int32, sm.shape, 1)
    g_col = -jnp.exp(pcol_ref[0:1, :]) * _softplus(sm + pcol_ref[1:2, :])
    gb_ref[0] = jnp.where(lane < DN_HEADS, g_col, _sigmoid(sm))
    a_t = abt_ref[...]
    g_row = -jnp.exp(prow_ref[:, 0:1]) * _softplus(a_t + prow_ref[:, 1:2])
    grow_ref[...] = g_row[0:8, :]


def _dn_prep(qkv, small, abt, conv_w, pcol, prow):
    b = qkv.shape[0]
    ts = SEQ_TILE
    nt = SEQ // ts
    seq_spec = lambda w: pl.BlockSpec((1, ts, w), lambda i, t: (i, t, 0))
    full = lambda shape: pl.BlockSpec(shape, lambda i, t: (0,) * len(shape))
    return pl.pallas_call(
        _dn_prep_kernel,
        out_shape=[jax.ShapeDtypeStruct((b, SEQ, DN_WIDTH), F32)] * 3
        + [jax.ShapeDtypeStruct((b, SEQ, LANES), F32), jax.ShapeDtypeStruct((8, b * SEQ), F32)],
        grid=(b, nt),
        in_specs=[
            seq_spec(3 * DN_WIDTH), seq_spec(LANES),
            pl.BlockSpec((16, ts), lambda i, t: (0, i * nt + t)),
            full((DN_CONV, 3 * DN_WIDTH)), full((8, LANES)), full((16, LANES)),
        ],
        out_specs=[seq_spec(DN_WIDTH)] * 3
        + [seq_spec(LANES), pl.BlockSpec((8, ts), lambda i, t: (0, i * nt + t))],
        scratch_shapes=[pltpu.VMEM((ts + 8, 3 * DN_WIDTH), F32)],
        compiler_params=_params(("parallel", "arbitrary")),
        name="dn_prep",
    )(qkv, small, abt, conv_w, pcol, prow)


def _delta_prepare(q_ref, k_ref, v_ref, gb_ref, grow_ref, wy):
    u_ref, w_ref, attn_ref, qg_ref, kk_ref, gc_ref = wy
    ts = SEQ_TILE
    ch = DN_CHUNK
    hd = DN_HEAD_DIM
    r = lax.broadcasted_iota(jnp.int32, (ts, ts), 0)
    c = lax.broadcasted_iota(jnp.int32, (ts, ts), 1)
    same_chunk = (r // ch) == (c // ch)
    tril = same_chunk & (r >= c)
    strict = same_chunk & (r > c)
    same16 = (r // 16) == (c // 16)
    eye = jnp.where(r == c, 1.0, 0.0).astype(F32)
    m_col = jnp.where(tril, 1.0, 0.0).astype(BF16)
    m_row = jnp.where(same_chunk & (r <= c), 1.0, 0.0).astype(BF16)

    gb = gb_ref[0]
    gc_col = _dot01_left(m_col, gb)
    gc_row = _dot01_right(grow_ref[...], m_row)

    heads = range(DN_HEADS)
    hsl = [slice(h * hd, (h + 1) * hd) for h in heads]
    kh = [k_ref[0, :, hsl[h]] for h in heads]
    gcb = [_bcast_col(gc_col, h, ts) for h in heads]
    decay = [jnp.where(tril, jnp.exp(jnp.where(
        tril, gcb[h] - jnp.broadcast_to(gc_row[h:h + 1, :], (ts, ts)), 0.0)), 0.0) for h in heads]
    beta = [_bcast_col(gb, DN_HEADS + h, hd) for h in heads]
    kb = [kh[h] * beta[h] for h in heads]
    yield
    a_mat = [jnp.where(strict, _mm_nt(kb[h], kh[h]) * decay[h], 0.0) for h in heads]
    yield
    d_mat = [jnp.where(same16, a_mat[h], 0.0) for h in heads]
    e_mat = [a_mat[h] - d_mat[h] for h in heads]
    d2 = [_mm(d_mat[h], d_mat[h]) for h in heads]
    yield
    d4 = [_mm(d2[h], d2[h]) for h in heads]
    yield
    t1 = [_mm(eye - d_mat[h], eye + d2[h]) for h in heads]
    yield
    d8 = [_mm(d4[h], d4[h]) for h in heads]
    yield
    t2 = [_mm(t1[h], eye + d4[h]) for h in heads]
    yield
    p_mat = [_mm(t2[h], eye + d8[h]) for h in heads]
    yield
    m_mat = [_mm(p_mat[h], e_mat[h]) for h in heads]
    yield
    m2 = [_mm(m_mat[h], m_mat[h]) for h in heads]
    yield
    t3 = [_mm(eye - m_mat[h], eye + m2[h]) for h in heads]
    yield
    t_mat = [_mm(t3[h], p_mat[h]) for h in heads]
    yield
    for h in heads:
        gc128 = gcb[h][:, :hd]
        expg = jnp.exp(gc128)
        qh = q_ref[0, :, hsl[h]]
        sol = _mm(t_mat[h], jnp.concatenate([v_ref[0, :, hsl[h]] * beta[h], kb[h] * expg], axis=1))
        u_ref[h] = sol[:, :hd]
        w_ref[h] = sol[:, hd:].astype(BF16)
        qg_ref[h] = (qh * expg).astype(BF16)
        kk_ref[h] = kh[h]
        gc_ref[h] = gc128
    yield
    for h in heads:
        attn_ref[h] = jnp.where(tril, _mm_nt(q_ref[0, :, hsl[h]], kh[h]) * decay[h], 0.0).astype(BF16)


def _delta_recur(wy, z_ref, nw_ref, o_ref, s_ref):
    u_ref, w_ref, attn_ref, qg_ref, kk_ref, gc_ref = wy
    ch = DN_CHUNK
    hd = DN_HEAD_DIM
    heads = range(DN_HEADS)
    state = [s_ref[h] for h in heads]
    outs = [[] for _ in heads]
    for ci in range(SEQ_TILE // ch):
        rs = slice(ci * ch, (ci + 1) * ch)
        v_new = [u_ref[h, rs, :] - _mm(w_ref[h, rs, :], state[h]) for h in heads]
        o_state = [_mm(qg_ref[h, rs, :], state[h]) for h in heads]
        yield
        for h in heads:
            gc = gc_ref[h, rs, :]
            g_last = gc[ch - 1:ch, :]
            k_dec = kk_ref[h, rs, :] * jnp.exp(g_last - gc)
            outs[h].append(o_state[h] + _mm(attn_ref[h, rs, ci * ch:(ci + 1) * ch], v_new[h]))
            state[h] = state[h] * jnp.exp(g_last) + _mm_tn(k_dec, v_new[h])
        yield

    for h in heads:
        s_ref[h] = state[h]
        hs = slice(h * hd, (h + 1) * hd)
        o = jnp.concatenate(outs[h], axis=0)
        o = o * lax.rsqrt(jnp.mean(o * o, axis=-1, keepdims=True) + EPS) * nw_ref[...]
        o_ref[0, :, hs] = o * _silu(z_ref[0, :, hs])


def _interleave(first, second, ratio):
    live = [first, second]
    while live:
        for gen, n in ((first, ratio), (second, 1)):
            for _ in range(n):
                if gen in live and next(gen, live) is live:
                    live.remove(gen)


def _delta_kernel(q_ref, k_ref, v_ref, gb_ref, grow_ref, z_ref, nw_ref, o_ref, s_ref, *wy_refs):
    t = pl.program_id(1)
    set_a, set_b = wy_refs[:6], wy_refs[6:]

    @pl.when(t == 0)
    def _():
        s_ref[...] = jnp.zeros(s_ref.shape, F32)
        for ref in set_b:
            ref[...] = jnp.zeros(ref.shape, ref.dtype)

    def step(read_set, write_set):
        _interleave(_delta_prepare(q_ref, k_ref, v_ref, gb_ref, grow_ref, write_set),
                    _delta_recur(read_set, z_ref, nw_ref, o_ref, s_ref), 2)

    @pl.when(lax.rem(t, 2) == 0)
    def _():
        step(set_b, set_a)

    @pl.when(lax.rem(t, 2) == 1)
    def _():
        step(set_a, set_b)


def _delta_rule(q, k, v, gb, grow, z, norm_w):
    b = q.shape[0]
    ts = SEQ_TILE
    nt = SEQ // ts
    nxt = lambda w: pl.BlockSpec((1, ts, w), lambda i, t: (i, jnp.minimum(t, nt - 1), 0))
    cur = lambda w: pl.BlockSpec((1, ts, w), lambda i, t: (i, jnp.maximum(t - 1, 0), 0))
    hd = DN_HEAD_DIM
    wy_set = [pltpu.VMEM((DN_HEADS, ts, hd), F32), pltpu.VMEM((DN_HEADS, ts, hd), BF16),
              pltpu.VMEM((DN_HEADS, ts, ts), BF16), pltpu.VMEM((DN_HEADS, ts, hd), BF16),
              pltpu.VMEM((DN_HEADS, ts, hd), F32), pltpu.VMEM((DN_HEADS, ts, hd), F32)]
    return pl.pallas_call(
        _delta_kernel,
        out_shape=jax.ShapeDtypeStruct((b, SEQ, DN_WIDTH), F32),
        grid=(b, nt + 1),
        in_specs=[
            nxt(DN_WIDTH), nxt(DN_WIDTH), nxt(DN_WIDTH), nxt(LANES),
            pl.BlockSpec((8, ts), lambda i, t: (0, i * nt + jnp.minimum(t, nt - 1))),
            cur(DN_WIDTH),
            pl.BlockSpec((1, DN_HEAD_DIM), lambda i, t: (0, 0)),
        ],
        out_specs=cur(DN_WIDTH),
        scratch_shapes=[pltpu.VMEM((DN_HEADS, hd, hd), F32)] + wy_set + wy_set,
        compiler_params=_params(("parallel", "arbitrary")),
        name="delta_rule",
    )(q, k, v, gb, grow, z, norm_w)


CONV_HALO = 32


def _conformer_kernel(u_ref, w_ref, b_ref, lnw_ref, lnb_ref, o_ref, buf, shifted):
    t = pl.program_id(1)
    ts = SEQ_TILE

    @pl.when(t == 0)
    def _():
        buf[0:CONV_HALO, :] = jnp.zeros((CONV_HALO, CONV_WIDTH), F32)

    u = u_ref[0]
    buf[CONV_HALO:CONV_HALO + ts, :] = u[:, :CONV_WIDTH] * _sigmoid(u[:, CONV_WIDTH:])
    span = ts + CONV_HALO - SUBLANES
    for b in range(1, SUBLANES):
        shifted[b] = buf[b:b + span, :]
    base = CONV_HALO - (CONV_KERNEL - 1)
    rows = 64
    for rc in range(ts // rows):
        parts = []
        for cc in range(CONV_WIDTH // LANES):
            cs = slice(cc * LANES, (cc + 1) * LANES)
            acc = None
            for j in range(CONV_KERNEL):
                start = base + rc * rows + j
                b = start % SUBLANES
                a0 = start - b
                win = buf[a0:a0 + rows, cs] if b == 0 else shifted[b, a0:a0 + rows, cs]
                term = w_ref[j:j + 1, cs] * win
                acc = term if acc is None else acc + term
            parts.append(acc)
        h = jnp.concatenate(parts, axis=1) + b_ref[...]
        mu = jnp.mean(h, axis=-1, keepdims=True)
        var = jnp.mean(jnp.square(h - mu), axis=-1, keepdims=True)
        hn = (h - mu) * lax.rsqrt(var + EPS) * lnw_ref[...] + lnb_ref[...]
        o_ref[0, rc * rows:(rc + 1) * rows, :] = _silu(hn)
    buf[0:CONV_HALO, :] = buf[ts:ts + CONV_HALO, :]


def _conformer(u, dw_w, dw_b, ln_w, ln_b):
    b = u.shape[0]
    ts = SEQ_TILE
    full = lambda shape: pl.BlockSpec(shape, lambda i, t: (0,) * len(shape))
    return pl.pallas_call(
        _conformer_kernel,
        out_shape=jax.ShapeDtypeStruct((b, SEQ, CONV_WIDTH), F32),
        grid=(b, SEQ // ts),
        in_specs=[
            pl.BlockSpec((1, ts, 2 * CONV_WIDTH), lambda i, t: (i, t, 0)),
            full((CONV_KERNEL, CONV_WIDTH)), full((1, CONV_WIDTH)),
            full((1, CONV_WIDTH)), full((1, CONV_WIDTH)),
        ],
        out_specs=pl.BlockSpec((1, ts, CONV_WIDTH), lambda i, t: (i, t, 0)),
        scratch_shapes=[pltpu.VMEM((ts + CONV_HALO, CONV_WIDTH), F32),
                        pltpu.VMEM((SUBLANES, ts + CONV_HALO - SUBLANES, CONV_WIDTH), F32)],
        compiler_params=_params(("parallel", "arbitrary")),
        name="conformer",
    )(u, dw_w, dw_b, ln_w, ln_b)


def _rms_rows(x, w):
    return x * lax.rsqrt(jnp.mean(x * x, axis=-1, keepdims=True) + EPS) * w


def _nsa_prep_kernel(kc_ref, vc_ref, ks_ref, kw_ref, pos_ref, w1_ref, w2_ref, knw_ref,
                     kcmp_ref, vcmp_ref, ksn_ref, kwn_ref):
    hd = NSA_HEAD_DIM

    def compress(x_ref, i):
        u_lo = jnp.zeros((N_CMP_PAD, CMP_HIDDEN), F32)
        u_hi = jnp.zeros((N_CMP_PAD, CMP_HIDDEN), F32)
        for r in range(CMP_STRIDE):
            xr = x_ref[pl.ds(r, N_CMP_PAD, stride=CMP_STRIDE), :]
            lo, hi = r, CMP_STRIDE + r
            u_lo = u_lo + _mm(xr + pos_ref[i, lo:lo + 1, :], w1_ref[i, lo * hd:(lo + 1) * hd, :])
            u_hi = u_hi + _mm(xr + pos_ref[i, hi:hi + 1, :], w1_ref[i, hi * hd:(hi + 1) * hd, :])
        hid = _silu(u_lo + pltpu.roll(u_hi, N_CMP_PAD - 1, axis=0))
        return _mm(hid, w2_ref[i])

    kcmp_ref[0] = _rms_rows(compress(kc_ref, 0), knw_ref[0:1, :])
    vcmp_ref[0] = compress(vc_ref, 1)
    ksn_ref[0] = _rms_rows(ks_ref[...], knw_ref[1:2, :])
    kwn_ref[0] = _rms_rows(kw_ref[...], knw_ref[2:3, :])


def _nsa_prep(kv6, pos, w1, w2, knw):
    b = kv6.shape[1]
    hd = NSA_HEAD_DIM
    full = lambda shape: pl.BlockSpec(shape, lambda i: (0,) * len(shape))
    piece = lambda j: pl.BlockSpec((None, None, SEQ, hd), lambda i: (j, i, 0, 0))
    bspec = lambda r, w: pl.BlockSpec((1, r, w), lambda i: (i, 0, 0))
    return pl.pallas_call(
        _nsa_prep_kernel,
        out_shape=[jax.ShapeDtypeStruct((b, N_CMP_PAD, hd), F32)] * 2
        + [jax.ShapeDtypeStruct((b, SEQ, hd), F32)] * 2,
        grid=(b,),
        in_specs=[
            piece(0), piece(1), piece(2), piece(4),
            full((2, CMP_LEN, hd)), full((2, CMP_LEN * hd, CMP_HIDDEN)), full((2, CMP_HIDDEN, hd)),
            full((3, hd)),
        ],
        out_specs=[bspec(N_CMP_PAD, hd)] * 2 + [bspec(SEQ, hd)] * 2,
        compiler_params=_params(("parallel",)),
        name="nsa_prep",
    )(kv6, kv6, kv6, kv6, pos, w1, w2, knw)


def _nsa_attn_kernel(q_ref, qnw_ref, kcmp_ref, vcmp_ref, ks_ref, vs_ref, kw_ref, vw_ref,
                     gate_ref, bcmp_ref, btab_ref, o_ref,
                     madd_ref, s_ref, ksb_ref, kwb_ref, vsa_ref, vwa_ref):
    i = pl.program_id(1)
    tq = Q_TILE
    nh = NSA_HEADS
    hd = NSA_HEAD_DIM
    t0 = i * tq

    @pl.when(i == 0)
    def _():
        ones = jnp.ones((SEQ, hd), BF16)
        ksb_ref[...] = ks_ref[0].astype(BF16)
        kwb_ref[...] = kw_ref[0].astype(BF16)
        vsa_ref[...] = jnp.concatenate([vs_ref[...].astype(BF16), ones], axis=1)
        vwa_ref[...] = jnp.concatenate([vw_ref[...].astype(BF16), ones], axis=1)

    qs = jnp.concatenate(
        [_rms_rows(q_ref[h], qnw_ref[...]) * hd ** -0.5 for h in range(nh)], axis=0).astype(BF16)

    row = lax.broadcasted_iota(jnp.int32, (tq, LANES), 0)
    lane = lax.broadcasted_iota(jnp.int32, (tq, LANES), 1)
    qpos = t0 + row

    s_all = _mm_nt(qs, kcmp_ref[0])
    cmp_valid = (qpos >= lane * CMP_STRIDE + (CMP_LEN - 1)) & (lane < N_CMP_PAD - 1)
    p_rows = []
    p_sum = jnp.zeros((tq, LANES), F32)
    for h in range(nh):
        s = jnp.where(cmp_valid, s_all[h * tq:(h + 1) * tq] + bcmp_ref[h], NEG_INF)
        m = jnp.max(s, axis=-1, keepdims=True)
        p = jnp.where(cmp_valid, jnp.exp(s - m), 0.0)
        l = jnp.sum(p, axis=-1, keepdims=True)
        p = p * jnp.where(l > 0.0, 1.0 / l, 0.0)
        p_rows.append(p)
        p_sum = p_sum + p
    o_cmp = _mm(jnp.concatenate(p_rows, axis=0), vcmp_ref[0])

    ss = lax.broadcasted_iota(jnp.int32, (N_SLC, N_CMP_PAD), 0)
    jj = lax.broadcasted_iota(jnp.int32, (N_SLC, N_CMP_PAD), 1)
    overlap_t = ((jj * CMP_STRIDE < ss * SLC_BLOCK + SLC_BLOCK)
                 & (jj * CMP_STRIDE + CMP_LEN > ss * SLC_BLOCK) & (jj < N_CMP_PAD - 1))
    overlap_t = jnp.where(overlap_t, 1.0, 0.0).astype(BF16)
    nt = lambda a, b_: lax.dot_general(a, b_, (((1,), (1,)), ((), ())), preferred_element_type=F32)
    p_hi, p_mid, p_lo = _split3(p_sum)
    imp = nt(overlap_t, p_hi) + nt(overlap_t, p_mid) + nt(overlap_t, p_lo)
    blk = lax.broadcasted_iota(jnp.int32, (N_SLC, tq), 0)
    cur = (t0 + lax.broadcasted_iota(jnp.int32, (N_SLC, tq), 1)) // SLC_BLOCK
    causal_blk = blk <= cur
    forced = (blk == 0) | (blk == cur) | (blk == cur - 1)
    imp = jnp.where(causal_blk & forced, FORCE, jnp.where(causal_blk, imp, -1.0))
    rank = jnp.zeros((N_SLC, tq), jnp.int32)
    for s2 in range(N_SLC):
        other = jnp.broadcast_to(imp[s2:s2 + 1, :], (N_SLC, tq))
        beats = (other > imp) | ((other == imp) & (blk > s2))
        rank = rank + jnp.where(beats, 1, 0)
    sel_t = jnp.where(rank < SLC_TOP_N, 1.0, 0.0).astype(BF16)
    er = lax.broadcasted_iota(jnp.int32, (N_SLC, SEQ), 0)
    ec = lax.broadcasted_iota(jnp.int32, (N_SLC, SEQ), 1)
    expand = jnp.where(ec // SLC_BLOCK == er, 1.0, 0.0).astype(BF16)
    key = lax.broadcasted_iota(jnp.int32, (tq, SEQ), 1)
    qall = t0 + lax.broadcasted_iota(jnp.int32, (tq, SEQ), 0)
    chosen = (_mm_tn(sel_t, expand) > 0.5) & (qall >= key)
    madd_ref[...] = jnp.where(chosen, 0.0, NEG_INF)

    neg = jnp.full((nh * tq, tq), NEG_INF, F32)
    zero = jnp.zeros((nh * tq, LANES), F32)

    def scores(k_tile, bias_idx, add):
        s_t = _mm_nt(qs, k_tile)
        return jnp.concatenate(
            [s_t[h * tq:(h + 1) * tq] + (btab_ref[h, bias_idx] + add) for h in range(nh)], axis=0)

    def normalise(acc):
        return acc[:, :hd] * (1.0 / _bcast_col(acc, hd, hd))

    n_pairs = (i + 2) // 2

    def slc_pass1(j, m_run):
        for u in range(2):
            kt = 2 * j + u
            k0 = pl.multiple_of(kt * tq, tq)
            s_m = scores(ksb_ref[pl.ds(k0, tq), :], jnp.clip(i - kt, 0, 2), madd_ref[:, pl.ds(k0, tq)])
            s_ref[:, pl.ds(k0, tq)] = s_m
            m_run = jnp.maximum(m_run, s_m)
        return m_run

    m_slc = jnp.broadcast_to(
        jnp.max(lax.fori_loop(0, n_pairs, slc_pass1, neg), axis=-1, keepdims=True), (nh * tq, tq))

    def slc_pass2(j, acc):
        for u in range(2):
            k0 = pl.multiple_of((2 * j + u) * tq, tq)
            p = jnp.exp(s_ref[:, pl.ds(k0, tq)] - m_slc).astype(BF16)
            acc = acc + jnp.dot(p, vsa_ref[pl.ds(k0, tq), :], preferred_element_type=F32)
        return acc

    o_slc = normalise(lax.fori_loop(0, n_pairs, slc_pass2, zero))

    kcol = lax.broadcasted_iota(jnp.int32, (tq, tq), 1)
    qrow = t0 + lax.broadcasted_iota(jnp.int32, (tq, tq), 0)
    n_win = WINDOW // tq + 1
    starts = []
    m_run = neg
    for j in range(n_win):
        d = n_win - 1 - j
        k0 = pl.multiple_of(jnp.maximum(i - d, 0) * tq, tq)
        starts.append(k0)
        dist = qrow - (k0 + kcol)
        ok = (dist >= 0) & (dist < WINDOW) & (i >= d)
        s_m = scores(kwb_ref[pl.ds(k0, tq), :], min(d, 2), jnp.where(ok, 0.0, NEG_INF))
        s_ref[:, j * tq:(j + 1) * tq] = s_m
        m_run = jnp.maximum(m_run, s_m)
    m_win = jnp.broadcast_to(jnp.max(m_run, axis=-1, keepdims=True), (nh * tq, tq))
    acc = zero
    for j in range(n_win):
        p = jnp.exp(s_ref[:, j * tq:(j + 1) * tq] - m_win).astype(BF16)
        acc = acc + jnp.dot(p, vwa_ref[pl.ds(starts[j], tq), :], preferred_element_type=F32)
    o_win = normalise(acc)

    gates = _sigmoid(gate_ref[0])
    for h in range(nh):
        hs = slice(h * tq, (h + 1) * tq)
        g0 = 2 * DN_HEADS + 3 * h
        o_ref[h] = (_bcast_col(gates, g0, hd) * o_cmp[hs]
                       + _bcast_col(gates, g0 + 1, hd) * o_slc[hs]
                       + _bcast_col(gates, g0 + 2, hd) * o_win[hs])


def _nsa_attn(q, qnw, kcmp, vcmp, ks, kv6, kw, small, bias_cmp, btab):
    b = kv6.shape[1]
    tq = Q_TILE
    nq = SEQ // tq
    nh, hd = NSA_HEADS, NSA_HEAD_DIM
    full = lambda shape: pl.BlockSpec(shape, lambda bi, i: (0,) * len(shape))
    per_b = lambda r: pl.BlockSpec((1, r, hd), lambda bi, i: (bi, 0, 0))
    piece = lambda j: pl.BlockSpec((None, None, SEQ, hd), lambda bi, i: (j, bi, 0, 0))
    heads = pl.BlockSpec((nh, tq, hd), lambda bi, i: (0, bi * nq + i, 0))
    return pl.pallas_call(
        _nsa_attn_kernel,
        out_shape=jax.ShapeDtypeStruct((nh, b * SEQ, hd), F32),
        grid=(b, nq),
        in_specs=[
            heads,
            full((1, hd)),
            per_b(N_CMP_PAD), per_b(N_CMP_PAD), per_b(SEQ), piece(3), per_b(SEQ), piece(5),
            pl.BlockSpec((1, tq, LANES), lambda bi, i: (bi, i, 0)),
            pl.BlockSpec((nh, tq, N_CMP_PAD), lambda bi, i: (0, i, 0)),
            full((nh, 3, tq, tq)),
        ],
        out_specs=heads,
        scratch_shapes=[pltpu.VMEM((tq, SEQ), F32), pltpu.VMEM((nh * tq, SEQ), F32),
                        pltpu.VMEM((SEQ, hd), BF16), pltpu.VMEM((SEQ, hd), BF16),
                        pltpu.VMEM((SEQ, 2 * hd), BF16), pltpu.VMEM((SEQ, 2 * hd), BF16)],
        compiler_params=_params(("parallel", "arbitrary")),
        name="nsa_attn",
    )(q, qnw, kcmp, vcmp, ks, kv6, kw, kv6, small, bias_cmp, btab)


def _out_router_kernel(ya_ref, yb_ref, yc_ref, x_ref, wo_ref, fnw_ref, rw_ref, rb_ref,
                       xo_ref, h_ref, idx_ref, wt_ref, hist_ref):
    wa = DN_WIDTH
    wb = wa + NSA_WIDTH
    y = (jnp.dot(ya_ref[...].astype(BF16), wo_ref[0:wa, :], preferred_element_type=F32)
         + jnp.dot(yc_ref[...].astype(BF16), wo_ref[wb:, :], preferred_element_type=F32))
    for hh in range(NSA_HEADS):
        r0 = wa + hh * NSA_HEAD_DIM
        y = y + jnp.dot(yb_ref[hh].astype(BF16), wo_ref[r0:r0 + NSA_HEAD_DIM, :],
                        preferred_element_type=F32)
    xn = x_ref[...] + y
    xo_ref[...] = xn
    h = xn * lax.rsqrt(jnp.mean(xn * xn, axis=-1, keepdims=True) + EPS) * fnw_ref[...]
    _store_tile_rows(h_ref, h, h.shape[0])
    logits = _dot_f32(h, rw_ref[...]) + rb_ref[...]
    lane = lax.broadcasted_iota(jnp.int32, logits.shape, 1)
    vals, idxs = [], []
    for _ in range(TOP_K):
        m = jnp.max(logits, axis=-1, keepdims=True)
        ix = jnp.min(jnp.where(logits == m, lane, LANES), axis=-1, keepdims=True)
        vals.append(m)
        idxs.append(ix)
        logits = jnp.where(lane == ix, -jnp.inf, logits)
    es = [jnp.exp(v - vals[0]) for v in vals]
    inv = 1.0 / (es[0] + es[1] + es[2] + es[3])
    idx_out = jnp.zeros(lane.shape, jnp.int32)
    wt_out = jnp.zeros(lane.shape, F32)
    chosen = jnp.zeros(lane.shape, F32)
    for k in range(TOP_K):
        idx_out = jnp.where(lane == k, idxs[k], idx_out)
        wt_out = jnp.where(lane == k, es[k] * inv, wt_out)
        chosen = chosen + jnp.where(lane == idxs[k], 1.0, 0.0)
    idx_ref[...] = idx_out
    wt_ref[...] = wt_out
    hist_ref[0] = jnp.broadcast_to(jnp.sum(chosen, axis=0, keepdims=True), (8, LANES))


def _out_router(ya, yb, yc, x2, w_out, fnw, rw, rb):
    n = x2.shape[0]
    tm = ROW_TILE
    row = lambda w: pl.BlockSpec((tm, w), lambda i: (i, 0))
    full = lambda shape: pl.BlockSpec(shape, lambda i: (0,) * len(shape))
    return pl.pallas_call(
        _out_router_kernel,
        out_shape=[jax.ShapeDtypeStruct((n, D_MODEL), F32), jax.ShapeDtypeStruct((n * SUBLANES, LANES), F32),
                   jax.ShapeDtypeStruct((n, LANES), jnp.int32), jax.ShapeDtypeStruct((n, LANES), F32),
                   jax.ShapeDtypeStruct((n // tm, 8, LANES), F32)],
        grid=(n // tm,),
        in_specs=[row(DN_WIDTH), pl.BlockSpec((NSA_HEADS, tm, NSA_HEAD_DIM), lambda i: (0, i, 0)),
                  row(CONV_WIDTH), row(D_MODEL),
                  full((D_MODEL, D_MODEL)), full((1, D_MODEL)), full((D_MODEL, LANES)), full((1, LANES))],
        out_specs=[row(D_MODEL), pl.BlockSpec((tm * SUBLANES, LANES), lambda i: (i, 0)), row(LANES), row(LANES),
                   pl.BlockSpec((1, 8, LANES), lambda i: (i, 0, 0))],
        compiler_params=_params(("parallel",)),
        name="out_router",
    )(ya, yb, yc, x2, w_out, fnw, rw, rb)


def _slots_kernel(idx_ref, base_ref, slot_ref):
    tm = ROW_TILE
    idx = idx_ref[...]
    lane = lax.broadcasted_iota(jnp.int32, (tm, LANES), 1)
    onehots = [jnp.where(lane == _bcast_col(idx, k, LANES), 1.0, 0.0) for k in range(TOP_K)]
    cnt = (onehots[0] + onehots[1]) + (onehots[2] + onehots[3])
    r = lax.broadcasted_iota(jnp.int32, (tm, tm), 0)
    c = lax.broadcasted_iota(jnp.int32, (tm, tm), 1)
    earlier = jnp.where(r > c, 1.0, 0.0).astype(BF16)
    rank = jnp.dot(earlier, cnt.astype(BF16), preferred_element_type=F32) + base_ref[0, 0:1, :]
    ones = jnp.ones((8, LANES), BF16)
    row8 = lax.broadcasted_iota(jnp.int32, (8, tm), 0)
    out = jnp.zeros((8, tm), F32)
    for k in range(TOP_K):
        hi, mid, lo = _split3(rank * onehots[k])
        d = lambda p: lax.dot_general(ones, p, (((1,), (1,)), ((), ())), preferred_element_type=F32)
        out = jnp.where(row8 == k, d(hi) + d(mid) + d(lo), out)
    slot_ref[...] = out.astype(jnp.int32)


def _slots(idx, base3):
    n = idx.shape[0]
    tm = ROW_TILE
    return pl.pallas_call(
        _slots_kernel,
        out_shape=jax.ShapeDtypeStruct((8, n), jnp.int32),
        grid=(n // tm,),
        in_specs=[pl.BlockSpec((tm, LANES), lambda i: (i, 0)),
                  pl.BlockSpec((1, 8, LANES), lambda i: (i, 0, 0))],
        out_specs=pl.BlockSpec((8, tm), lambda i: (0, i)),
        compiler_params=_params(("parallel",)),
        name="moe_slots",
    )(idx, base3)


def _dispatch_kernel(slot_ref, zoff_ref, nv_ref, h_ref, xs_hbm, zbuf, sem, zsem):
    i = pl.program_id(0)
    tm = ROW_TILE
    n = slot_ref.shape[0] // TOP_K
    base = i * tm

    @pl.when(i == 0)
    def _():
        zbuf[...] = jnp.zeros(zbuf.shape, F32)
        blk = MOE_TILE * SUBLANES
        n_blocks = xs_hbm.shape[0] // blk

        def fill(row0):
            return pltpu.make_async_copy(
                zbuf, xs_hbm.at[pl.ds(pl.multiple_of(row0 * SUBLANES, blk), blk)], zsem)

        def fill_region_end(e, c):
            fill(zoff_ref[e]).start()
            return c

        def fill_tail(b, c):
            fill(b * MOE_TILE).start()
            return c

        def wait_fill(b, c):
            fill(0).wait()
            return c

        lax.fori_loop(0, N_EXPERTS, fill_region_end, 0)
        lax.fori_loop(nv_ref[0], n_blocks, fill_tail, 0)
        lax.fori_loop(0, N_EXPERTS + n_blocks - nv_ref[0], wait_fill, 0)

    def body(r, c):
        src = h_ref.at[pl.ds(pl.multiple_of(r * SUBLANES, SUBLANES), SUBLANES)]
        for k in range(TOP_K):
            row0 = pl.multiple_of(slot_ref[k * n + base + r] * SUBLANES, SUBLANES)
            pltpu.make_async_copy(src, xs_hbm.at[pl.ds(row0, SUBLANES)], sem).start(priority=k % 2)
        return c

    lax.fori_loop(0, tm, body, 0, unroll=4)
    done = xs_hbm.at[pl.ds(0, TOP_K * tm * SUBLANES)]
    pltpu.make_async_copy(done, done, sem).wait()


def _dispatch(slot_flat, zero_off, n_valid, h, n_slots):
    n = h.shape[0] // SUBLANES
    tm = ROW_TILE
    return pl.pallas_call(
        _dispatch_kernel,
        out_shape=jax.ShapeDtypeStruct((n_slots * SUBLANES, LANES), F32),
        grid_spec=pltpu.PrefetchScalarGridSpec(
            num_scalar_prefetch=3,
            grid=(n // tm,),
            in_specs=[pl.BlockSpec((tm * SUBLANES, LANES), lambda i, s, z, nv: (i, 0))],
            out_specs=pl.BlockSpec(memory_space=pl.ANY),
            scratch_shapes=[pltpu.VMEM((MOE_TILE * SUBLANES, LANES), F32),
                            pltpu.SemaphoreType.DMA, pltpu.SemaphoreType.DMA],
        ),
        compiler_params=_params(("arbitrary",)),
        name="moe_dispatch",
    )(slot_flat, zero_off, n_valid, h)


def _expert_kernel(be_ref, nv_ref, xs_ref, wgu_ref, bgu_ref, wd_ref, bd_ref, o_ref, wgu_bf, wd_bf):
    i = pl.program_id(0)
    new_expert = (i == 0) | (be_ref[i] != be_ref[jnp.maximum(i - 1, 0)])

    @pl.when((i < nv_ref[0]) & new_expert)
    def _():
        step = 512
        for c0 in range(0, 2 * D_FF, step):
            wgu_bf[:, c0:c0 + step] = wgu_ref[:, c0:c0 + step].astype(BF16)
        for c0 in range(0, D_MODEL, step):
            wd_bf[:, c0:c0 + step] = wd_ref[:, c0:c0 + step].astype(BF16)

    @pl.when(i < nv_ref[0])
    def _():
        xb = _load_tile_rows(xs_ref, MOE_TILE).astype(BF16)
        gu = jnp.dot(xb, wgu_bf[...], preferred_element_type=F32) + bgu_ref[...]
        gate = jnp.minimum(gu[:, :D_FF], SWIGLU_LIMIT)
        up = jnp.clip(gu[:, D_FF:], -SWIGLU_LIMIT, SWIGLU_LIMIT)
        act = (up + 1.0) * gate * _sigmoid(SWIGLU_ALPHA * gate)
        y = jnp.dot(act.astype(BF16), wd_bf[...], preferred_element_type=F32) + bd_ref[...]
        _store_tile_rows(o_ref, y, MOE_TILE)

    @pl.when(i >= nv_ref[0])
    def _():
        o_ref[...] = jnp.zeros(o_ref.shape, F32)


def _expert_ffn(layer, block_expert, n_valid, xs, wgu, bgu, wd, bd):
    n_slots = xs.shape[0] // SUBLANES
    tm = MOE_TILE
    return pl.pallas_call(
        _expert_kernel,
        out_shape=jax.ShapeDtypeStruct(xs.shape, F32),
        grid_spec=pltpu.PrefetchScalarGridSpec(
            num_scalar_prefetch=2,
            grid=(n_slots // tm,),
            in_specs=[
                pl.BlockSpec((tm * SUBLANES, LANES), lambda i, be, nv: (i, 0)),
                pl.BlockSpec((None, None, D_MODEL, 2 * D_FF), lambda i, be, nv: (layer, be[i], 0, 0)),
                pl.BlockSpec((None, None, 1, 2 * D_FF), lambda i, be, nv: (layer, be[i], 0, 0)),
                pl.BlockSpec((None, None, D_FF, D_MODEL), lambda i, be, nv: (layer, be[i], 0, 0)),
                pl.BlockSpec((None, None, 1, D_MODEL), lambda i, be, nv: (layer, be[i], 0, 0)),
            ],
            out_specs=pl.BlockSpec((tm * SUBLANES, LANES), lambda i, be, nv: (i, 0)),
            scratch_shapes=[pltpu.VMEM((D_MODEL, 2 * D_FF), BF16), pltpu.VMEM((D_FF, D_MODEL), BF16)],
        ),
        compiler_params=pltpu.CompilerParams(dimension_semantics=("arbitrary",),
                                             vmem_limit_bytes=EXPERT_VMEM_LIMIT_BYTES),
        name="expert_ffn",
    )(block_expert, n_valid, xs, wgu, bgu, wd, bd)


def _combine_kernel(slot_ref, ys_hbm, x_ref, wt_ref, o_ref, buf, sems):
    i = pl.program_id(0)
    tc = COMBINE_TILE
    nsteps = pl.num_programs(0)
    n = slot_ref.shape[0] // TOP_K
    cur = lax.rem(i, 2)

    def issue(step, par):
        base = step * tc

        def body(r, c):
            dst0 = pl.multiple_of(r * SUBLANES, SUBLANES)
            for k in range(TOP_K):
                row0 = pl.multiple_of(slot_ref[k * n + base + r] * SUBLANES, SUBLANES)
                pltpu.make_async_copy(ys_hbm.at[pl.ds(row0, SUBLANES)],
                                      buf.at[par, k, pl.ds(dst0, SUBLANES)],
                                      sems.at[par]).start(priority=k % 2)
            return c

        lax.fori_loop(0, tc, body, 0, unroll=4)

    @pl.when(i == 0)
    def _():
        issue(0, 0)

    @pl.when(i + 1 < nsteps)
    def _():
        issue(i + 1, 1 - cur)

    pltpu.make_async_copy(buf.at[cur], buf.at[cur], sems.at[cur]).wait()
    w = [_bcast_col(wt_ref[...], k, LANES) for k in range(TOP_K)]
    for s in range(ROW_CHUNKS):
        cs = slice(s * LANES, (s + 1) * LANES)
        acc = x_ref[:, cs]
        for k in range(TOP_K):
            acc = acc + w[k] * buf[cur, k, pl.ds(s, tc, stride=SUBLANES), :]
        o_ref[:, cs] = acc


def _combine(slot_flat, ys, x2, wt):
    n = x2.shape[0]
    tc = COMBINE_TILE
    return pl.pallas_call(
        _combine_kernel,
        out_shape=jax.ShapeDtypeStruct((n, D_MODEL), F32),
        grid_spec=pltpu.PrefetchScalarGridSpec(
            num_scalar_prefetch=1,
            grid=(n // tc,),
            in_specs=[pl.BlockSpec(memory_space=pl.ANY),
                      pl.BlockSpec((tc, D_MODEL), lambda i, s: (i, 0)),
                      pl.BlockSpec((tc, LANES), lambda i, s: (i, 0))],
            out_specs=pl.BlockSpec((tc, D_MODEL), lambda i, s: (i, 0)),
            scratch_shapes=[pltpu.VMEM((2, TOP_K, tc * SUBLANES, LANES), F32), pltpu.SemaphoreType.DMA((2,))],
        ),
        compiler_params=_params(("arbitrary",)),
        name="moe_combine",
    )(slot_flat, ys, x2, wt)


def _moe(layer, h, x2, idx, wt, hist3, wgu, bgu, wd, bd):
    n = x2.shape[0]
    tm = MOE_TILE
    n_slots = n * TOP_K + N_EXPERTS * tm
    n_blocks = n_slots // tm
    hist = hist3[:, 0, :N_EXPERTS]
    counts = jnp.sum(hist, axis=0).astype(jnp.int32)
    padded = (counts + tm - 1) // tm * tm
    pad_end = jnp.cumsum(padded)
    pad_start = pad_end - padded
    tile_base = pad_start[None, :].astype(F32) + (jnp.cumsum(hist, axis=0) - hist)
    base3 = jnp.broadcast_to(jnp.pad(tile_base, ((0, 0), (0, LANES - N_EXPERTS)))[:, None, :],
                             (hist.shape[0], 8, LANES))
    blk0 = jnp.arange(n_blocks) * tm
    block_expert = jnp.minimum(jnp.sum(blk0[:, None] >= pad_end[None, :], axis=1), N_EXPERTS - 1).astype(jnp.int32)
    n_valid = (pad_end[-1:] // tm).astype(jnp.int32)
    zero_off = jnp.maximum(pad_end - tm, 0).astype(jnp.int32)

    slot_flat = _slots(idx, base3)[:TOP_K].reshape(-1)
    xs = _dispatch(slot_flat, zero_off, n_valid, h, n_slots)
    ys = _expert_ffn(layer, block_expert, n_valid, xs, wgu, bgu, wd, bd)
    return _combine(slot_flat, ys, x2, wt)


def _t5_bucket(dist):
    n = jnp.maximum(dist, 0)
    max_exact = REL_BUCKETS // 2
    nf = jnp.maximum(n, 1).astype(F32)
    large = max_exact + (jnp.log(nf / max_exact) / math.log(REL_MAX_DIST / max_exact)
                         * (REL_BUCKETS - max_exact)).astype(jnp.int32)
    large = jnp.minimum(large, REL_BUCKETS - 1)
    return jnp.where(n < max_exact, n, large)


def _bias_tables(rel_bias):
    rel_bias = rel_bias.astype(F32)
    tq = Q_TILE

    def lookup(dist):
        bucket = _t5_bucket(dist)
        out = jnp.zeros((NSA_HEADS,) + dist.shape, F32)
        for bk in range(REL_BUCKETS):
            out = jnp.where(bucket[None] == bk, rel_bias[bk].reshape((NSA_HEADS,) + (1,) * dist.ndim), out)
        return out

    t_pos = jnp.arange(SEQ)
    cmp_end = jnp.arange(N_CMP_PAD) * CMP_STRIDE + CMP_LEN - 1
    bias_cmp = lookup(t_pos[:, None] - cmp_end[None, :])
    rr = jnp.arange(tq)[:, None] - jnp.arange(tq)[None, :]
    btab = lookup(jnp.stack([rr, rr + tq, rr + 2 * tq]))
    return bias_cmp, btab


def _layer(layer, x2, b, p, experts, bias_cmp, btab):
    n = x2.shape[0]
    w = p['w_in']
    o_a, o_b, o_q, o_kv, o_g, o_u = 2048, 2052, 2056, 2312, 2696, 2708
    w_small = jnp.concatenate([w[:, o_a:o_q], w[:, o_g:o_u],
                               jnp.zeros((D_MODEL, LANES - 8 - 3 * NSA_HEADS), F32)], axis=1)
    w_cat = jnp.concatenate([w[:, :o_a], w[:, o_q:o_kv], w[:, o_kv:o_g], w[:, o_u:], w_small],
                            axis=1).astype(BF16)
    w_abt = jnp.concatenate([w[:, o_a:o_q].T, jnp.zeros((8, D_MODEL), F32)], axis=0).astype(BF16)
    qkv, z, nq, nkv, cu, small, abt = _in_proj(x2, p['attn_norm_w'][None, :], w_cat, w_abt)

    pcol = jnp.zeros((8, LANES), F32).at[0, :DN_HEADS].set(p['dn_a_log']).at[1, :DN_HEADS].set(p['dn_dt_bias'])
    prow = jnp.zeros((16, LANES), F32).at[:DN_HEADS, 0].set(p['dn_a_log']).at[:DN_HEADS, 1].set(p['dn_dt_bias'])
    small3 = small.reshape(b, SEQ, LANES)
    dq, dk, dv, gb, grow = _dn_prep(qkv.reshape(b, SEQ, 3 * DN_WIDTH), small3, abt,
                                    p['dn_conv_w'], pcol, prow)
    y_a = _delta_rule(dq, dk, dv, gb, grow, z.reshape(b, SEQ, DN_WIDTH), p['dn_norm_w'][None, :])

    y_c = _conformer(cu.reshape(b, SEQ, 2 * CONV_WIDTH), p['conv_dw_w'], p['conv_dw_b'][None, :],
                     p['conv_ln_w'][None, :], p['conv_ln_b'][None, :])

    kv6 = nkv.reshape(6, b, SEQ, NSA_HEAD_DIM)
    kcmp, vcmp, ksn, kwn = _nsa_prep(kv6, p['nsa_cmp_pos'], p['nsa_cmp_w1'].astype(BF16),
                                     p['nsa_cmp_w2'].astype(BF16), p['nsa_k_norm_w'])
    y_b = _nsa_attn(nq, p['nsa_q_norm_w'][None, :], kcmp, vcmp, ksn, kv6, kwn, small3, bias_cmp, btab)

    rw = jnp.concatenate([p['router_w'], jnp.zeros((D_MODEL, LANES - N_EXPERTS), F32)], axis=1)
    rb = jnp.concatenate([p['router_b'], jnp.full((LANES - N_EXPERTS,), NEG_INF, F32)])[None, :]
    x_new, h, idx, wt, hist3 = _out_router(y_a.reshape(n, DN_WIDTH), y_b, y_c.reshape(n, CONV_WIDTH), x2,
                                           p['w_out'].astype(BF16), p['ffn_norm_w'][None, :], rw, rb)
    return _moe(layer, h, x_new, idx, wt, hist3, *experts)


def kernel(x, attn_norm_w, w_in, dn_conv_w, dn_a_log, dn_dt_bias, dn_norm_w, nsa_q_norm_w, nsa_k_norm_w, nsa_cmp_pos, nsa_cmp_w1, nsa_cmp_w2, conv_dw_w, conv_dw_b, conv_ln_w, conv_ln_b, w_out, ffn_norm_w, router_w, router_b, w_gate_up, b_gate_up, w_down, b_down, rel_bias):
    b, t, d = x.shape
    assert (t, d) == (SEQ, D_MODEL)
    stacked = dict(attn_norm_w=attn_norm_w, w_in=w_in, dn_conv_w=dn_conv_w, dn_a_log=dn_a_log,
                   dn_dt_bias=dn_dt_bias, dn_norm_w=dn_norm_w, nsa_q_norm_w=nsa_q_norm_w,
                   nsa_k_norm_w=nsa_k_norm_w, nsa_cmp_pos=nsa_cmp_pos, nsa_cmp_w1=nsa_cmp_w1,
                   nsa_cmp_w2=nsa_cmp_w2, conv_dw_w=conv_dw_w, conv_dw_b=conv_dw_b,
                   conv_ln_w=conv_ln_w, conv_ln_b=conv_ln_b, w_out=w_out, ffn_norm_w=ffn_norm_w,
                   router_w=router_w, router_b=router_b)
    experts = (w_gate_up, b_gate_up[:, :, None, :], w_down, b_down[:, :, None, :])
    bias_cmp, btab = _bias_tables(rel_bias)
    x2 = x.reshape(b * t, d)
    for l in range(w_in.shape[0]):
        x2 = _layer(l, x2, b, {k: v[l] for k, v in stacked.items()}, experts, bias_cmp, btab)
    return x2.reshape(b, t, d)
```

```python
import functools
import math

import numpy as np
import jax
import jax.numpy as jnp
from jax import lax
from jax.experimental import pallas as pl
from jax.experimental.pallas import tpu as pltpu

F32 = jnp.float32
BF16 = jnp.bfloat16

D_MODEL = 1024
SEQ = 2048
DN_HEADS = 4
DN_HEAD_DIM = 128
DN_WIDTH = DN_HEADS * DN_HEAD_DIM
DN_CONV = 4
DN_CHUNK = 64
NSA_HEADS = 4
NSA_HEAD_DIM = 64
NSA_WIDTH = NSA_HEADS * NSA_HEAD_DIM
CMP_LEN = 32
CMP_STRIDE = 16
CMP_HIDDEN = 2 * NSA_HEAD_DIM
SLC_BLOCK = 64
SLC_TOP_N = 16
WINDOW = 512
CONV_WIDTH = 256
CONV_KERNEL = 31
REL_BUCKETS = 32
REL_MAX_DIST = 128
N_EXPERTS = 32
TOP_K = 4
D_FF = D_MODEL
SWIGLU_LIMIT = 7.0
SWIGLU_ALPHA = 1.702
EPS = 1e-6
NEG_INF = -1e30
FORCE = 1e4

LANES = 128
VMEM_LIMIT_BYTES = 48 * 1024 * 1024
EXPERT_VMEM_LIMIT_BYTES = 56 * 1024 * 1024

ROW_TILE = 512
SEQ_TILE = 256
Q_TILE = 128
MOE_TILE = 512
COMBINE_TILE = 256

N_CMP_PAD = 128
N_SLC = SEQ // SLC_BLOCK


def _params(sem=None):
    return pltpu.CompilerParams(dimension_semantics=sem, vmem_limit_bytes=VMEM_LIMIT_BYTES)


def _mm(a, b):
    return jnp.dot(a.astype(BF16), b.astype(BF16), preferred_element_type=F32)


def _mm_nt(a, b):
    return lax.dot_general(a.astype(BF16), b.astype(BF16), (((1,), (1,)), ((), ())),
                           preferred_element_type=F32)


def _mm_tn(a, b):
    return lax.dot_general(a.astype(BF16), b.astype(BF16), (((0,), (0,)), ((), ())),
                           preferred_element_type=F32)


def _split3(x):
    hi = x.astype(BF16)
    r1 = x - hi.astype(F32)
    mid = r1.astype(BF16)
    lo = (r1 - mid.astype(F32)).astype(BF16)
    return hi, mid, lo


def _dot01_right(x, m01):
    hi, mid, lo = _split3(x)
    d = lambda p: jnp.dot(p, m01, preferred_element_type=F32)
    return d(hi) + d(mid) + d(lo)


def _dot01_left(m01, x):
    hi, mid, lo = _split3(x)
    d = lambda p: jnp.dot(m01, p, preferred_element_type=F32)
    return d(hi) + d(mid) + d(lo)


def _dot_f32(a, b):
    a_hi = a.astype(BF16)
    a_lo = (a - a_hi.astype(F32)).astype(BF16)
    b_hi = b.astype(BF16)
    b_lo = (b - b_hi.astype(F32)).astype(BF16)
    d = lambda p, q: jnp.dot(p, q, preferred_element_type=F32)
    return d(a_hi, b_hi) + d(a_hi, b_lo) + d(a_lo, b_hi) + d(a_lo, b_lo)


def _sigmoid(x):
    return 1.0 / (1.0 + jnp.exp(-x))


def _silu(x):
    return x * _sigmoid(x)


def _softplus(x):
    return jnp.maximum(x, 0.0) + jnp.log(1.0 + jnp.exp(-jnp.abs(x)))


SUBLANES = 8
ROW_CHUNKS = D_MODEL // LANES
assert ROW_CHUNKS == SUBLANES


def _store_tile_rows(ref, value, rows):
    for s in range(ROW_CHUNKS):
        ref[pl.ds(s, rows, stride=SUBLANES), :] = value[:, s * LANES:(s + 1) * LANES]


def _load_tile_rows(ref, rows):
    return jnp.concatenate([ref[pl.ds(s, rows, stride=SUBLANES), :] for s in range(ROW_CHUNKS)], axis=1)


def _bcast_col(x, j, width):
    return jnp.broadcast_to(x[:, j:j + 1], (x.shape[0], width))


IN_SEGS = (3 * DN_WIDTH, DN_WIDTH, NSA_WIDTH, 6 * NSA_HEAD_DIM, 2 * CONV_WIDTH, LANES)
IN_COLS = sum(IN_SEGS)


def _in_proj_kernel(x_ref, nw_ref, w_ref, wabt_ref,
                    qkv_ref, z_ref, nq_ref, nkv_ref, cu_ref, small_ref, abt_ref):
    xf = x_ref[...]
    ms = jnp.mean(xf * xf, axis=-1, keepdims=True)
    hb = (xf * lax.rsqrt(ms + EPS) * nw_ref[...]).astype(BF16)
    hd = NSA_HEAD_DIM
    off = 0
    for ref, width in zip((qkv_ref, z_ref, nq_ref, nkv_ref, cu_ref, small_ref), IN_SEGS):
        res = jnp.dot(hb, w_ref[:, off:off + width], preferred_element_type=F32)
        if ref is nq_ref or ref is nkv_ref:
            for j in range(width // hd):
                ref[j] = res[:, j * hd:(j + 1) * hd]
        else:
            ref[...] = res
        off += width
    abt_ref[...] = lax.dot_general(wabt_ref[...], hb, (((1,), (1,)), ((), ())),
                                   preferred_element_type=F32)


def _in_proj(x2, norm_w, w_cat, w_abt):
    n = x2.shape[0]
    tm = ROW_TILE
    hd = NSA_HEAD_DIM
    pieces = {2: NSA_HEADS, 3: 6}
    out_shape, out_specs = [], []
    for k, w in enumerate(IN_SEGS):
        if k in pieces:
            out_shape.append(jax.ShapeDtypeStruct((pieces[k], n, hd), F32))
            out_specs.append(pl.BlockSpec((pieces[k], tm, hd), lambda i: (0, i, 0)))
        else:
            out_shape.append(jax.ShapeDtypeStruct((n, w), F32))
            out_specs.append(pl.BlockSpec((tm, w), lambda i: (i, 0)))
    out_shape.append(jax.ShapeDtypeStruct((16, n), F32))
    out_specs.append(pl.BlockSpec((16, tm), lambda i: (0, i)))
    return pl.pallas_call(
        _in_proj_kernel,
        out_shape=out_shape,
        grid=(n // tm,),
        in_specs=[
            pl.BlockSpec((tm, D_MODEL), lambda i: (i, 0)),
            pl.BlockSpec((1, D_MODEL), lambda i: (0, 0)),
            pl.BlockSpec((D_MODEL, IN_COLS), lambda i: (0, 0)),
            pl.BlockSpec((16, D_MODEL), lambda i: (0, 0)),
        ],
        out_specs=out_specs,
        compiler_params=_params(("parallel",)),
        name="in_proj",
    )(x2, norm_w, w_cat, w_abt)


def _dn_prep_kernel(qkv_ref, small_ref, abt_ref, cw_ref, pcol_ref, prow_ref,
                    q_ref, k_ref, v_ref, gb_ref, grow_ref, buf):
    t = pl.program_id(1)
    ts = SEQ_TILE

    @pl.when(t == 0)
    def _():
        buf[0:8, :] = jnp.zeros((8, 3 * DN_WIDTH), F32)

    buf[8:8 + ts, :] = qkv_ref[0]
    for c in range(3 * DN_HEADS):
        cs = slice(c * LANES, (c + 1) * LANES)
        acc = cw_ref[0:1, cs] * buf[5:5 + ts, cs]
        for j in range(1, DN_CONV):
            acc = acc + cw_ref[j:j + 1, cs] * buf[5 + j:5 + j + ts, cs]
        y = _silu(acc)
        if c < 2 * DN_HEADS:
            y = y * lax.rsqrt(jnp.sum(y * y, axis=-1, keepdims=True) + EPS)
        if c < DN_HEADS:
            q_ref[0, :, cs] = y * DN_HEAD_DIM ** -0.5
        elif c < 2 * DN_HEADS:
            k_ref[0, :, (c - DN_HEADS) * LANES:(c - DN_HEADS + 1) * LANES] = y
        else:
            v_ref[0, :, (c - 2 * DN_HEADS) * LANES:(c - 2 * DN_HEADS + 1) * LANES] = y
    buf[0:8, :] = buf[ts:ts + 8, :]

    sm = small_ref[0]
    lane = lax.broadcasted_iota(jnp.int32, sm.shape, 1)
    g_col = -jnp.exp(pcol_ref[0:1, :]) * _softplus(sm + pcol_ref[1:2, :])
    gb_ref[0] = jnp.where(lane < DN_HEADS, g_col, _sigmoid(sm))
    a_t = abt_ref[...]
    g_row = -jnp.exp(prow_ref[:, 0:1]) * _softplus(a_t + prow_ref[:, 1:2])
    grow_ref[...] = g_row[0:8, :]


def _dn_prep(qkv, small, abt, conv_w, pcol, prow):
    b = qkv.shape[0]
    ts = SEQ_TILE
    nt = SEQ // ts
    seq_spec = lambda w: pl.BlockSpec((1, ts, w), lambda i, t: (i, t, 0))
    full = lambda shape: pl.BlockSpec(shape, lambda i, t: (0,) * len(shape))
    return pl.pallas_call(
        _dn_prep_kernel,
        out_shape=[jax.ShapeDtypeStruct((b, SEQ, DN_WIDTH), F32)] * 3
        + [jax.ShapeDtypeStruct((b, SEQ, LANES), F32), jax.ShapeDtypeStruct((8, b * SEQ), F32)],
        grid=(b, nt),
        in_specs=[
            seq_spec(3 * DN_WIDTH), seq_spec(LANES),
            pl.BlockSpec((16, ts), lambda i, t: (0, i * nt + t)),
            full((DN_CONV, 3 * DN_WIDTH)), full((8, LANES)), full((16, LANES)),
        ],
        out_specs=[seq_spec(DN_WIDTH)] * 3
        + [seq_spec(LANES), pl.BlockSpec((8, ts), lambda i, t: (0, i * nt + t))],
        scratch_shapes=[pltpu.VMEM((ts + 8, 3 * DN_WIDTH), F32)],
        compiler_params=_params(("parallel", "arbitrary")),
        name="dn_prep",
    )(qkv, small, abt, conv_w, pcol, prow)


def _delta_prepare(q_ref, k_ref, v_ref, gb_ref, grow_ref, wy):
    u_ref, w_ref, attn_ref, qg_ref, kk_ref, gc_ref = wy
    ts = SEQ_TILE
    ch = DN_CHUNK
    hd = DN_HEAD_DIM
    r = lax.broadcasted_iota(jnp.int32, (ts, ts), 0)
    c = lax.broadcasted_iota(jnp.int32, (ts, ts), 1)
    same_chunk = (r // ch) == (c // ch)
    tril = same_chunk & (r >= c)
    strict = same_chunk & (r > c)
    same16 = (r // 16) == (c // 16)
    eye = jnp.where(r == c, 1.0, 0.0).astype(F32)
    m_col = jnp.where(tril, 1.0, 0.0).astype(BF16)
    m_row = jnp.where(same_chunk & (r <= c), 1.0, 0.0).astype(BF16)

    gb = gb_ref[0]
    gc_col = _dot01_left(m_col, gb)
    gc_row = _dot01_right(grow_ref[...], m_row)

    heads = range(DN_HEADS)
    hsl = [slice(h * hd, (h + 1) * hd) for h in heads]
    kh = [k_ref[0, :, hsl[h]] for h in heads]
    gcb = [_bcast_col(gc_col, h, ts) for h in heads]
    decay = [jnp.where(tril, jnp.exp(jnp.where(
        tril, gcb[h] - jnp.broadcast_to(gc_row[h:h + 1, :], (ts, ts)), 0.0)), 0.0) for h in heads]
    beta = [_bcast_col(gb, DN_HEADS + h, hd) for h in heads]
    kb = [kh[h] * beta[h] for h in heads]
    yield
    a_mat = [jnp.where(strict, _mm_nt(kb[h], kh[h]) * decay[h], 0.0) for h in heads]
    yield
    d_mat = [jnp.where(same16, a_mat[h], 0.0) for h in heads]
    e_mat = [a_mat[h] - d_mat[h] for h in heads]
    d2 = [_mm(d_mat[h], d_mat[h]) for h in heads]
    yield
    d4 = [_mm(d2[h], d2[h]) for h in heads]
    yield
    t1 = [_mm(eye - d_mat[h], eye + d2[h]) for h in heads]
    yield
    d8 = [_mm(d4[h], d4[h]) for h in heads]
    yield
    t2 = [_mm(t1[h], eye + d4[h]) for h in heads]
    yield
    p_mat = [_mm(t2[h], eye + d8[h]) for h in heads]
    yield
    m_mat = [_mm(p_mat[h], e_mat[h]) for h in heads]
    yield
    m2 = [_mm(m_mat[h], m_mat[h]) for h in heads]
    yield
    t3 = [_mm(eye - m_mat[h], eye + m2[h]) for h in heads]
    yield
    t_mat = [_mm(t3[h], p_mat[h]) for h in heads]
    yield
    for h in heads:
        gc128 = gcb[h][:, :hd]
        expg = jnp.exp(gc128)
        qh = q_ref[0, :, hsl[h]]
        sol = _mm(t_mat[h], jnp.concatenate([v_ref[0, :, hsl[h]] * beta[h], kb[h] * expg], axis=1))
        u_ref[h] = sol[:, :hd]
        w_ref[h] = sol[:, hd:].astype(BF16)
        qg_ref[h] = (qh * expg).astype(BF16)
        kk_ref[h] = kh[h]
        gc_ref[h] = gc128
    yield
    for h in heads:
        attn_ref[h] = jnp.where(tril, _mm_nt(q_ref[0, :, hsl[h]], kh[h]) * decay[h], 0.0).astype(BF16)


def _delta_recur(wy, z_ref, nw_ref, o_ref, s_ref):
    u_ref, w_ref, attn_ref, qg_ref, kk_ref, gc_ref = wy
    ch = DN_CHUNK
    hd = DN_HEAD_DIM
    heads = range(DN_HEADS)
    state = [s_ref[h] for h in heads]
    outs = [[] for _ in heads]
    for ci in range(SEQ_TILE // ch):
        rs = slice(ci * ch, (ci + 1) * ch)
        v_new = [u_ref[h, rs, :] - _mm(w_ref[h, rs, :], state[h]) for h in heads]
        o_state = [_mm(qg_ref[h, rs, :], state[h]) for h in heads]
        yield
        for h in heads:
            gc = gc_ref[h, rs, :]
            g_last = gc[ch - 1:ch, :]
            k_dec = kk_ref[h, rs, :] * jnp.exp(g_last - gc)
            outs[h].append(o_state[h] + _mm(attn_ref[h, rs, ci * ch:(ci + 1) * ch], v_new[h]))
            state[h] = state[h] * jnp.exp(g_last) + _mm_tn(k_dec, v_new[h])
        yield

    for h in heads:
        s_ref[h] = state[h]
        hs = slice(h * hd, (h + 1) * hd)
        o = jnp.concatenate(outs[h], axis=0)
        o = o * lax.rsqrt(jnp.mean(o * o, axis=-1, keepdims=True) + EPS) * nw_ref[...]
        o_ref[0, :, hs] = o * _silu(z_ref[0, :, hs])


def _interleave(weighted):
    live = [gen for gen, _ in weighted]
    while live:
        for gen, n in weighted:
            for _ in range(n):
                if gen in live and next(gen, live) is live:
                    live.remove(gen)


DN_BATCH = 2


def _delta_kernel(q_ref, k_ref, v_ref, gb_ref, grow0_ref, grow1_ref, z_ref, nw_ref, o_ref, s_ref,
                  *wy_refs):
    t = pl.program_id(1)
    set_a, set_b = wy_refs[:6], wy_refs[6:]
    grow_refs = (grow0_ref, grow1_ref)

    @pl.when(t == 0)
    def _():
        s_ref[...] = jnp.zeros(s_ref.shape, F32)
        for ref in set_b:
            ref[...] = jnp.zeros(ref.shape, ref.dtype)

    def step(read_set, write_set):
        one = lambda ref, bb: ref.at[pl.ds(bb, 1)]
        work = []
        for bb in range(DN_BATCH):
            work.append((_delta_prepare(one(q_ref, bb), one(k_ref, bb), one(v_ref, bb), one(gb_ref, bb),
                                        grow_refs[bb], [ref.at[bb] for ref in write_set]), 2))
        for bb in range(DN_BATCH):
            work.append((_delta_recur([ref.at[bb] for ref in read_set], one(z_ref, bb), nw_ref,
                                      one(o_ref, bb), s_ref.at[bb]), 1))
        _interleave(work)

    @pl.when(lax.rem(t, 2) == 0)
    def _():
        step(set_b, set_a)

    @pl.when(lax.rem(t, 2) == 1)
    def _():
        step(set_a, set_b)


def _delta_rule(q, k, v, gb, grow, z, norm_w):
    b = q.shape[0]
    nb = DN_BATCH
    assert b % nb == 0
    ts = SEQ_TILE
    nt = SEQ // ts
    nxt = lambda w: pl.BlockSpec((nb, ts, w), lambda i, t: (i, jnp.minimum(t, nt - 1), 0))
    cur = lambda w: pl.BlockSpec((nb, ts, w), lambda i, t: (i, jnp.maximum(t - 1, 0), 0))
    grow_spec = lambda bb: pl.BlockSpec(
        (8, ts), lambda i, t: (0, (i * nb + bb) * nt + jnp.minimum(t, nt - 1)))
    hd = DN_HEAD_DIM
    wy_set = [pltpu.VMEM((nb, DN_HEADS, ts, hd), F32), pltpu.VMEM((nb, DN_HEADS, ts, hd), BF16),
              pltpu.VMEM((nb, DN_HEADS, ts, ts), BF16), pltpu.VMEM((nb, DN_HEADS, ts, hd), BF16),
              pltpu.VMEM((nb, DN_HEADS, ts, hd), F32), pltpu.VMEM((nb, DN_HEADS, ts, hd), F32)]
    return pl.pallas_call(
        _delta_kernel,
        out_shape=jax.ShapeDtypeStruct((b, SEQ, DN_WIDTH), F32),
        grid=(b // nb, nt + 1),
        in_specs=[
            nxt(DN_WIDTH), nxt(DN_WIDTH), nxt(DN_WIDTH), nxt(LANES),
            grow_spec(0), grow_spec(1),
            cur(DN_WIDTH),
            pl.BlockSpec((1, DN_HEAD_DIM), lambda i, t: (0, 0)),
        ],
        out_specs=cur(DN_WIDTH),
        scratch_shapes=[pltpu.VMEM((nb, DN_HEADS, hd, hd), F32)] + wy_set + wy_set,
        compiler_params=_params(("parallel", "arbitrary")),
        name="delta_rule",
    )(q, k, v, gb, grow, grow, z, norm_w)


CONV_HALO = 32


def _conformer_kernel(u_ref, w_ref, b_ref, lnw_ref, lnb_ref, o_ref, buf, shifted):
    t = pl.program_id(1)
    ts = SEQ_TILE

    @pl.when(t == 0)
    def _():
        buf[0:CONV_HALO, :] = jnp.zeros((CONV_HALO, CONV_WIDTH), F32)

    u = u_ref[0]
    buf[CONV_HALO:CONV_HALO + ts, :] = u[:, :CONV_WIDTH] * _sigmoid(u[:, CONV_WIDTH:])
    span = ts + CONV_HALO - SUBLANES
    for b in range(1, SUBLANES):
        shifted[b] = buf[b:b + span, :]
    base = CONV_HALO - (CONV_KERNEL - 1)
    rows = 64
    for rc in range(ts // rows):
        parts = []
        for cc in range(CONV_WIDTH // LANES):
            cs = slice(cc * LANES, (cc + 1) * LANES)
            acc = None
            for j in range(CONV_KERNEL):
                start = base + rc * rows + j
                b = start % SUBLANES
                a0 = start - b
                win = buf[a0:a0 + rows, cs] if b == 0 else shifted[b, a0:a0 + rows, cs]
                term = w_ref[j:j + 1, cs] * win
                acc = term if acc is None else acc + term
            parts.append(acc)
        h = jnp.concatenate(parts, axis=1) + b_ref[...]
        mu = jnp.mean(h, axis=-1, keepdims=True)
        var = jnp.mean(jnp.square(h - mu), axis=-1, keepdims=True)
        hn = (h - mu) * lax.rsqrt(var + EPS) * lnw_ref[...] + lnb_ref[...]
        o_ref[0, rc * rows:(rc + 1) * rows, :] = _silu(hn)
    buf[0:CONV_HALO, :] = buf[ts:ts + CONV_HALO, :]


def _conformer(u, dw_w, dw_b, ln_w, ln_b):
    b = u.shape[0]
    ts = SEQ_TILE
    full = lambda shape: pl.BlockSpec(shape, lambda i, t: (0,) * len(shape))
    return pl.pallas_call(
        _conformer_kernel,
        out_shape=jax.ShapeDtypeStruct((b, SEQ, CONV_WIDTH), F32),
        grid=(b, SEQ // ts),
        in_specs=[
            pl.BlockSpec((1, ts, 2 * CONV_WIDTH), lambda i, t: (i, t, 0)),
            full((CONV_KERNEL, CONV_WIDTH)), full((1, CONV_WIDTH)),
            full((1, CONV_WIDTH)), full((1, CONV_WIDTH)),
        ],
        out_specs=pl.BlockSpec((1, ts, CONV_WIDTH), lambda i, t: (i, t, 0)),
        scratch_shapes=[pltpu.VMEM((ts + CONV_HALO, CONV_WIDTH), F32),
                        pltpu.VMEM((SUBLANES, ts + CONV_HALO - SUBLANES, CONV_WIDTH), F32)],
        compiler_params=_params(("parallel", "arbitrary")),
        name="conformer",
    )(u, dw_w, dw_b, ln_w, ln_b)


def _rms_rows(x, w):
    return x * lax.rsqrt(jnp.mean(x * x, axis=-1, keepdims=True) + EPS) * w


def _nsa_prep_kernel(kc_ref, vc_ref, ks_ref, kw_ref, pos_ref, w1_ref, w2_ref, knw_ref,
                     kcmp_ref, vcmp_ref, ksn_ref, kwn_ref):
    hd = NSA_HEAD_DIM

    def compress(x_ref, i):
        u_lo = jnp.zeros((N_CMP_PAD, CMP_HIDDEN), F32)
        u_hi = jnp.zeros((N_CMP_PAD, CMP_HIDDEN), F32)
        for r in range(CMP_STRIDE):
            xr = x_ref[pl.ds(r, N_CMP_PAD, stride=CMP_STRIDE), :]
            lo, hi = r, CMP_STRIDE + r
            u_lo = u_lo + _mm(xr + pos_ref[i, lo:lo + 1, :], w1_ref[i, lo * hd:(lo + 1) * hd, :])
            u_hi = u_hi + _mm(xr + pos_ref[i, hi:hi + 1, :], w1_ref[i, hi * hd:(hi + 1) * hd, :])
        hid = _silu(u_lo + pltpu.roll(u_hi, N_CMP_PAD - 1, axis=0))
        return _mm(hid, w2_ref[i])

    kcmp_ref[0] = _rms_rows(compress(kc_ref, 0), knw_ref[0:1, :])
    vcmp_ref[0] = compress(vc_ref, 1)
    ksn_ref[0] = _rms_rows(ks_ref[...], knw_ref[1:2, :])
    kwn_ref[0] = _rms_rows(kw_ref[...], knw_ref[2:3, :])


def _nsa_prep(kv6, pos, w1, w2, knw):
    b = kv6.shape[1]
    hd = NSA_HEAD_DIM
    full = lambda shape: pl.BlockSpec(shape, lambda i: (0,) * len(shape))
    piece = lambda j: pl.BlockSpec((None, None, SEQ, hd), lambda i: (j, i, 0, 0))
    bspec = lambda r, w: pl.BlockSpec((1, r, w), lambda i: (i, 0, 0))
    return pl.pallas_call(
        _nsa_prep_kernel,
        out_shape=[jax.ShapeDtypeStruct((b, N_CMP_PAD, hd), F32)] * 2
        + [jax.ShapeDtypeStruct((b, SEQ, hd), F32)] * 2,
        grid=(b,),
        in_specs=[
            piece(0), piece(1), piece(2), piece(4),
            full((2, CMP_LEN, hd)), full((2, CMP_LEN * hd, CMP_HIDDEN)), full((2, CMP_HIDDEN, hd)),
            full((3, hd)),
        ],
        out_specs=[bspec(N_CMP_PAD, hd)] * 2 + [bspec(SEQ, hd)] * 2,
        compiler_params=_params(("parallel",)),
        name="nsa_prep",
    )(kv6, kv6, kv6, kv6, pos, w1, w2, knw)


def _nsa_attn_kernel(q_ref, qnw_ref, kcmp_ref, vcmp_ref, ks_ref, vs_ref, kw_ref, vw_ref,
                     gate_ref, bcmp_ref, btab_ref, o_ref,
                     madd_ref, s_ref, ksb_ref, kwb_ref, vsa_ref, vwa_ref):
    i = pl.program_id(1)
    tq = Q_TILE
    nh = NSA_HEADS
    hd = NSA_HEAD_DIM
    t0 = i * tq

    @pl.when(i == 0)
    def _():
        ones = jnp.ones((SEQ, hd), BF16)
        ksb_ref[...] = ks_ref[0].astype(BF16)
        kwb_ref[...] = kw_ref[0].astype(BF16)
        vsa_ref[...] = jnp.concatenate([vs_ref[...].astype(BF16), ones], axis=1)
        vwa_ref[...] = jnp.concatenate([vw_ref[...].astype(BF16), ones], axis=1)

    qs = jnp.concatenate(
        [_rms_rows(q_ref[h], qnw_ref[...]) * hd ** -0.5 for h in range(nh)], axis=0).astype(BF16)

    row = lax.broadcasted_iota(jnp.int32, (tq, LANES), 0)
    lane = lax.broadcasted_iota(jnp.int32, (tq, LANES), 1)
    qpos = t0 + row

    s_all = _mm_nt(qs, kcmp_ref[0])
    cmp_valid = (qpos >= lane * CMP_STRIDE + (CMP_LEN - 1)) & (lane < N_CMP_PAD - 1)
    p_rows = []
    p_sum = jnp.zeros((tq, LANES), F32)
    for h in range(nh):
        s = jnp.where(cmp_valid, s_all[h * tq:(h + 1) * tq] + bcmp_ref[h], NEG_INF)
        m = jnp.max(s, axis=-1, keepdims=True)
        p = jnp.where(cmp_valid, jnp.exp(s - m), 0.0)
        l = jnp.sum(p, axis=-1, keepdims=True)
        p = p * jnp.where(l > 0.0, 1.0 / l, 0.0)
        p_rows.append(p)
        p_sum = p_sum + p
    o_cmp = _mm(jnp.concatenate(p_rows, axis=0), vcmp_ref[0])

    ss = lax.broadcasted_iota(jnp.int32, (N_SLC, N_CMP_PAD), 0)
    jj = lax.broadcasted_iota(jnp.int32, (N_SLC, N_CMP_PAD), 1)
    overlap_t = ((jj * CMP_STRIDE < ss * SLC_BLOCK + SLC_BLOCK)
                 & (jj * CMP_STRIDE + CMP_LEN > ss * SLC_BLOCK) & (jj < N_CMP_PAD - 1))
    overlap_t = jnp.where(overlap_t, 1.0, 0.0).astype(BF16)
    nt = lambda a, b_: lax.dot_general(a, b_, (((1,), (1,)), ((), ())), preferred_element_type=F32)
    p_hi, p_mid, p_lo = _split3(p_sum)
    imp = nt(overlap_t, p_hi) + nt(overlap_t, p_mid) + nt(overlap_t, p_lo)
    blk = lax.broadcasted_iota(jnp.int32, (N_SLC, tq), 0)
    cur = (t0 + lax.broadcasted_iota(jnp.int32, (N_SLC, tq), 1)) // SLC_BLOCK
    causal_blk = blk <= cur
    forced = (blk == 0) | (blk == cur) | (blk == cur - 1)
    imp = jnp.where(causal_blk & forced, FORCE, jnp.where(causal_blk, imp, -1.0))
    rank = jnp.zeros((N_SLC, tq), jnp.int32)
    for s2 in range(N_SLC):
        other = jnp.broadcast_to(imp[s2:s2 + 1, :], (N_SLC, tq))
        beats = (other > imp) | ((other == imp) & (blk > s2))
        rank = rank + jnp.where(beats, 1, 0)
    sel_t = jnp.where(rank < SLC_TOP_N, 1.0, 0.0).astype(BF16)
    er = lax.broadcasted_iota(jnp.int32, (N_SLC, SEQ), 0)
    ec = lax.broadcasted_iota(jnp.int32, (N_SLC, SEQ), 1)
    expand = jnp.where(ec // SLC_BLOCK == er, 1.0, 0.0).astype(BF16)
    key = lax.broadcasted_iota(jnp.int32, (tq, SEQ), 1)
    qall = t0 + lax.broadcasted_iota(jnp.int32, (tq, SEQ), 0)
    chosen = (_mm_tn(sel_t, expand) > 0.5) & (qall >= key)
    madd_ref[...] = jnp.where(chosen, 0.0, NEG_INF)

    neg = jnp.full((nh * tq, tq), NEG_INF, F32)
    zero = jnp.zeros((nh * tq, LANES), F32)

    def scores(k_tile, bias_idx, add):
        s_t = _mm_nt(qs, k_tile)
        return jnp.concatenate(
            [s_t[h * tq:(h + 1) * tq] + (btab_ref[h, bias_idx] + add) for h in range(nh)], axis=0)

    def normalise(acc):
        return acc[:, :hd] * (1.0 / _bcast_col(acc, hd, hd))

    n_pairs = (i + 2) // 2

    def slc_pass1(j, m_run):
        for u in range(2):
            kt = 2 * j + u
            k0 = pl.multiple_of(kt * tq, tq)
            s_m = scores(ksb_ref[pl.ds(k0, tq), :], jnp.clip(i - kt, 0, 2), madd_ref[:, pl.ds(k0, tq)])
            s_ref[:, pl.ds(k0, tq)] = s_m
            m_run = jnp.maximum(m_run, s_m)
        return m_run

    m_slc = jnp.broadcast_to(
        jnp.max(lax.fori_loop(0, n_pairs, slc_pass1, neg), axis=-1, keepdims=True), (nh * tq, tq))

    def slc_pass2(j, acc):
        for u in range(2):
            k0 = pl.multiple_of((2 * j + u) * tq, tq)
            p = jnp.exp(s_ref[:, pl.ds(k0, tq)] - m_slc).astype(BF16)
            acc = acc + jnp.dot(p, vsa_ref[pl.ds(k0, tq), :], preferred_element_type=F32)
        return acc

    o_slc = normalise(lax.fori_loop(0, n_pairs, slc_pass2, zero))

    kcol = lax.broadcasted_iota(jnp.int32, (tq, tq), 1)
    qrow = t0 + lax.broadcasted_iota(jnp.int32, (tq, tq), 0)
    n_win = WINDOW // tq + 1
    starts = []
    m_run = neg
    for j in range(n_win):
        d = n_win - 1 - j
        k0 = pl.multiple_of(jnp.maximum(i - d, 0) * tq, tq)
        starts.append(k0)
        dist = qrow - (k0 + kcol)
        ok = (dist >= 0) & (dist < WINDOW) & (i >= d)
        s_m = scores(kwb_ref[pl.ds(k0, tq), :], min(d, 2), jnp.where(ok, 0.0, NEG_INF))
        s_ref[:, j * tq:(j + 1) * tq] = s_m
        m_run = jnp.maximum(m_run, s_m)
    m_win = jnp.broadcast_to(jnp.max(m_run, axis=-1, keepdims=True), (nh * tq, tq))
    acc = zero
    for j in range(n_win):
        p = jnp.exp(s_ref[:, j * tq:(j + 1) * tq] - m_win).astype(BF16)
        acc = acc + jnp.dot(p, vwa_ref[pl.ds(starts[j], tq), :], preferred_element_type=F32)
    o_win = normalise(acc)

    gates = _sigmoid(gate_ref[0])
    outs = []
    for h in range(nh):
        hs = slice(h * tq, (h + 1) * tq)
        g0 = 2 * DN_HEADS + 3 * h
        outs.append(_bcast_col(gates, g0, hd) * o_cmp[hs]
                    + _bcast_col(gates, g0 + 1, hd) * o_slc[hs]
                    + _bcast_col(gates, g0 + 2, hd) * o_win[hs])
    o_ref[...] = jnp.concatenate(outs, axis=1)


def _nsa_attn(q, qnw, kcmp, vcmp, ks, kv6, kw, small, bias_cmp, btab):
    b = kv6.shape[1]
    tq = Q_TILE
    nq = SEQ // tq
    nh, hd = NSA_HEADS, NSA_HEAD_DIM
    full = lambda shape: pl.BlockSpec(shape, lambda bi, i: (0,) * len(shape))
    per_b = lambda r: pl.BlockSpec((1, r, hd), lambda bi, i: (bi, 0, 0))
    piece = lambda j: pl.BlockSpec((None, None, SEQ, hd), lambda bi, i: (j, bi, 0, 0))
    heads = pl.BlockSpec((nh, tq, hd), lambda bi, i: (0, bi * nq + i, 0))
    return pl.pallas_call(
        _nsa_attn_kernel,
        out_shape=jax.ShapeDtypeStruct((b * SEQ, NSA_WIDTH), F32),
        grid=(b, nq),
        in_specs=[
            heads,
            full((1, hd)),
            per_b(N_CMP_PAD), per_b(N_CMP_PAD), per_b(SEQ), piece(3), per_b(SEQ), piece(5),
            pl.BlockSpec((1, tq, LANES), lambda bi, i: (bi, i, 0)),
            pl.BlockSpec((nh, tq, N_CMP_PAD), lambda bi, i: (0, i, 0)),
            full((nh, 3, tq, tq)),
        ],
        out_specs=pl.BlockSpec((tq, NSA_WIDTH), lambda bi, i: (bi * nq + i, 0)),
        scratch_shapes=[pltpu.VMEM((tq, SEQ), F32), pltpu.VMEM((nh * tq, SEQ), F32),
                        pltpu.VMEM((SEQ, hd), BF16), pltpu.VMEM((SEQ, hd), BF16),
                        pltpu.VMEM((SEQ, 2 * hd), BF16), pltpu.VMEM((SEQ, 2 * hd), BF16)],
        compiler_params=_params(("parallel", "arbitrary")),
        name="nsa_attn",
    )(q, qnw, kcmp, vcmp, ks, kv6, kw, kv6, small, bias_cmp, btab)


def _out_router_kernel(ya_ref, yb_ref, yc_ref, x_ref, wo_ref, fnw_ref, rw_ref, rb_ref,
                       xo_ref, h_ref, idx_ref, wt_ref, hist_ref):
    wa = DN_WIDTH
    wb = wa + NSA_WIDTH
    y = (jnp.dot(ya_ref[...].astype(BF16), wo_ref[0:wa, :], preferred_element_type=F32)
         + jnp.dot(yb_ref[...].astype(BF16), wo_ref[wa:wb, :], preferred_element_type=F32)
         + jnp.dot(yc_ref[...].astype(BF16), wo_ref[wb:, :], preferred_element_type=F32))
    xn = x_ref[...] + y
    xo_ref[...] = xn
    h = xn * lax.rsqrt(jnp.mean(xn * xn, axis=-1, keepdims=True) + EPS) * fnw_ref[...]
    _store_tile_rows(h_ref, h, h.shape[0])
    logits = _dot_f32(h, rw_ref[...]) + rb_ref[...]
    lane = lax.broadcasted_iota(jnp.int32, logits.shape, 1)
    vals, idxs = [], []
    for _ in range(TOP_K):
        m = jnp.max(logits, axis=-1, keepdims=True)
        ix = jnp.min(jnp.where(logits == m, lane, LANES), axis=-1, keepdims=True)
        vals.append(m)
        idxs.append(ix)
        logits = jnp.where(lane == ix, -jnp.inf, logits)
    es = [jnp.exp(v - vals[0]) for v in vals]
    inv = 1.0 / (es[0] + es[1] + es[2] + es[3])
    idx_out = jnp.zeros(lane.shape, jnp.int32)
    wt_out = jnp.zeros(lane.shape, F32)
    chosen = jnp.zeros(lane.shape, F32)
    for k in range(TOP_K):
        idx_out = jnp.where(lane == k, idxs[k], idx_out)
        wt_out = jnp.where(lane == k, es[k] * inv, wt_out)
        chosen = chosen + jnp.where(lane == idxs[k], 1.0, 0.0)
    idx_ref[...] = idx_out
    wt_ref[...] = wt_out
    hist_ref[0] = jnp.broadcast_to(jnp.sum(chosen, axis=0, keepdims=True), (8, LANES))


def _out_router(ya, yb, yc, x2, w_out, fnw, rw, rb):
    n = x2.shape[0]
    tm = ROW_TILE
    row = lambda w: pl.BlockSpec((tm, w), lambda i: (i, 0))
    full = lambda shape: pl.BlockSpec(shape, lambda i: (0,) * len(shape))
    return pl.pallas_call(
        _out_router_kernel,
        out_shape=[jax.ShapeDtypeStruct((n, D_MODEL), F32), jax.ShapeDtypeStruct((n * SUBLANES, LANES), F32),
                   jax.ShapeDtypeStruct((n, LANES), jnp.int32), jax.ShapeDtypeStruct((n, LANES), F32),
                   jax.ShapeDtypeStruct((n // tm, 8, LANES), F32)],
        grid=(n // tm,),
        in_specs=[row(DN_WIDTH), row(NSA_WIDTH), row(CONV_WIDTH), row(D_MODEL),
                  full((D_MODEL, D_MODEL)), full((1, D_MODEL)), full((D_MODEL, LANES)), full((1, LANES))],
        out_specs=[row(D_MODEL), pl.BlockSpec((tm * SUBLANES, LANES), lambda i: (i, 0)), row(LANES), row(LANES),
                   pl.BlockSpec((1, 8, LANES), lambda i: (i, 0, 0))],
        compiler_params=_params(("parallel",)),
        name="out_router",
    )(ya, yb, yc, x2, w_out, fnw, rw, rb)


def _slots_kernel(idx_ref, base_ref, slot_ref):
    tm = ROW_TILE
    idx = idx_ref[...]
    lane = lax.broadcasted_iota(jnp.int32, (tm, LANES), 1)
    onehots = [jnp.where(lane == _bcast_col(idx, k, LANES), 1.0, 0.0) for k in range(TOP_K)]
    cnt = (onehots[0] + onehots[1]) + (onehots[2] + onehots[3])
    r = lax.broadcasted_iota(jnp.int32, (tm, tm), 0)
    c = lax.broadcasted_iota(jnp.int32, (tm, tm), 1)
    earlier = jnp.where(r > c, 1.0, 0.0).astype(BF16)
    rank = jnp.dot(earlier, cnt.astype(BF16), preferred_element_type=F32) + base_ref[0, 0:1, :]
    ones = jnp.ones((8, LANES), BF16)
    row8 = lax.broadcasted_iota(jnp.int32, (8, tm), 0)
    out = jnp.zeros((8, tm), F32)
    for k in range(TOP_K):
        hi, mid, lo = _split3(rank * onehots[k])
        d = lambda p: lax.dot_general(ones, p, (((1,), (1,)), ((), ())), preferred_element_type=F32)
        out = jnp.where(row8 == k, d(hi) + d(mid) + d(lo), out)
    slot_ref[...] = (out * SUBLANES).astype(jnp.int32)


def _slots(idx, base3):
    n = idx.shape[0]
    tm = ROW_TILE
    return pl.pallas_call(
        _slots_kernel,
        out_shape=jax.ShapeDtypeStruct((8, n), jnp.int32),
        grid=(n // tm,),
        in_specs=[pl.BlockSpec((tm, LANES), lambda i: (i, 0)),
                  pl.BlockSpec((1, 8, LANES), lambda i: (i, 0, 0))],
        out_specs=pl.BlockSpec((8, tm), lambda i: (0, i)),
        compiler_params=_params(("parallel",)),
        name="moe_slots",
    )(idx, base3)


def _dispatch_kernel(slot_ref, zoff_ref, nv_ref, h_ref, xs_hbm, zbuf, sem, zsem):
    i = pl.program_id(0)
    tm = ROW_TILE
    n = slot_ref.shape[0] // TOP_K
    base = i * tm

    @pl.when(i == 0)
    def _():
        zbuf[...] = jnp.zeros(zbuf.shape, F32)
        blk = MOE_TILE * SUBLANES
        n_blocks = xs_hbm.shape[0] // blk

        def fill(row0):
            return pltpu.make_async_copy(
                zbuf, xs_hbm.at[pl.ds(pl.multiple_of(row0 * SUBLANES, blk), blk)], zsem)

        def fill_region_end(e, c):
            fill(zoff_ref[e]).start()
            return c

        def fill_tail(b, c):
            fill(b * MOE_TILE).start()
            return c

        def wait_fill(b, c):
            fill(0).wait()
            return c

        lax.fori_loop(0, N_EXPERTS, fill_region_end, 0)
        lax.fori_loop(nv_ref[0], n_blocks, fill_tail, 0)
        lax.fori_loop(0, N_EXPERTS + n_blocks - nv_ref[0], wait_fill, 0)

    def body(r, c):
        src = h_ref.at[pl.ds(pl.multiple_of(r * SUBLANES, SUBLANES), SUBLANES)]
        for k in range(TOP_K):
            row0 = pl.multiple_of(slot_ref[k * n + base + r], SUBLANES)
            pltpu.make_async_copy(src, xs_hbm.at[pl.ds(row0, SUBLANES)], sem).start(priority=k % 2)
        return c

    lax.fori_loop(0, tm, body, 0, unroll=4)
    done = xs_hbm.at[pl.ds(0, TOP_K * tm * SUBLANES)]
    pltpu.make_async_copy(done, done, sem).wait()


def _dispatch(slot_flat, zero_off, n_valid, h, n_slots):
    n = h.shape[0] // SUBLANES
    tm = ROW_TILE
    return pl.pallas_call(
        _dispatch_kernel,
        out_shape=jax.ShapeDtypeStruct((n_slots * SUBLANES, LANES), F32),
        grid_spec=pltpu.PrefetchScalarGridSpec(
            num_scalar_prefetch=3,
            grid=(n // tm,),
            in_specs=[pl.BlockSpec((tm * SUBLANES, LANES), lambda i, s, z, nv: (i, 0))],
            out_specs=pl.BlockSpec(memory_space=pl.ANY),
            scratch_shapes=[pltpu.VMEM((MOE_TILE * SUBLANES, LANES), F32),
                            pltpu.SemaphoreType.DMA, pltpu.SemaphoreType.DMA],
        ),
        compiler_params=_params(("arbitrary",)),
        name="moe_dispatch",
    )(slot_flat, zero_off, n_valid, h)


def _expert_kernel(be_ref, nv_ref, xs_ref, wgu_ref, bgu_ref, wd_ref, bd_ref, o_ref, wgu_bf, wd_bf):
    i = pl.program_id(0)
    new_expert = (i == 0) | (be_ref[i] != be_ref[jnp.maximum(i - 1, 0)])

    @pl.when((i < nv_ref[0]) & new_expert)
    def _():
        step = 512
        for c0 in range(0, 2 * D_FF, step):
            wgu_bf[:, c0:c0 + step] = wgu_ref[:, c0:c0 + step].astype(BF16)
        for c0 in range(0, D_MODEL, step):
            wd_bf[:, c0:c0 + step] = wd_ref[:, c0:c0 + step].astype(BF16)

    @pl.when(i < nv_ref[0])
    def _():
        xb = _load_tile_rows(xs_ref, MOE_TILE).astype(BF16)
        gu = jnp.dot(xb, wgu_bf[...], preferred_element_type=F32) + bgu_ref[...]
        gate = jnp.minimum(gu[:, :D_FF], SWIGLU_LIMIT)
        up = jnp.clip(gu[:, D_FF:], -SWIGLU_LIMIT, SWIGLU_LIMIT)
        act = (up + 1.0) * gate * _sigmoid(SWIGLU_ALPHA * gate)
        y = jnp.dot(act.astype(BF16), wd_bf[...], preferred_element_type=F32) + bd_ref[...]
        _store_tile_rows(o_ref, y, MOE_TILE)

    @pl.when(i >= nv_ref[0])
    def _():
        o_ref[...] = jnp.zeros(o_ref.shape, F32)


def _expert_ffn(layer, block_expert, n_valid, xs, wgu, bgu, wd, bd):
    n_slots = xs.shape[0] // SUBLANES
    tm = MOE_TILE
    return pl.pallas_call(
        _expert_kernel,
        out_shape=jax.ShapeDtypeStruct(xs.shape, F32),
        grid_spec=pltpu.PrefetchScalarGridSpec(
            num_scalar_prefetch=2,
            grid=(n_slots // tm,),
            in_specs=[
                pl.BlockSpec((tm * SUBLANES, LANES), lambda i, be, nv: (i, 0)),
                pl.BlockSpec((None, None, D_MODEL, 2 * D_FF), lambda i, be, nv: (layer, be[i], 0, 0)),
                pl.BlockSpec((None, None, 1, 2 * D_FF), lambda i, be, nv: (layer, be[i], 0, 0)),
                pl.BlockSpec((None, None, D_FF, D_MODEL), lambda i, be, nv: (layer, be[i], 0, 0)),
                pl.BlockSpec((None, None, 1, D_MODEL), lambda i, be, nv: (layer, be[i], 0, 0)),
            ],
            out_specs=pl.BlockSpec((tm * SUBLANES, LANES), lambda i, be, nv: (i, 0)),
            scratch_shapes=[pltpu.VMEM((D_MODEL, 2 * D_FF), BF16), pltpu.VMEM((D_FF, D_MODEL), BF16)],
        ),
        compiler_params=pltpu.CompilerParams(dimension_semantics=("arbitrary",),
                                             vmem_limit_bytes=EXPERT_VMEM_LIMIT_BYTES),
        name="expert_ffn",
    )(block_expert, n_valid, xs, wgu, bgu, wd, bd)


def _combine_kernel(slot_ref, ys_hbm, x_ref, wt_ref, o_ref, buf, sems):
    i = pl.program_id(0)
    tc = COMBINE_TILE
    nsteps = pl.num_programs(0)
    n = slot_ref.shape[0] // TOP_K
    cur = lax.rem(i, 2)

    def issue(step, par):
        base = step * tc

        def body(r, c):
            dst0 = pl.multiple_of(r * SUBLANES, SUBLANES)
            for k in range(TOP_K):
                row0 = pl.multiple_of(slot_ref[k * n + base + r], SUBLANES)
                pltpu.make_async_copy(ys_hbm.at[pl.ds(row0, SUBLANES)],
                                      buf.at[par, k, pl.ds(dst0, SUBLANES)],
                                      sems.at[par]).start(priority=k % 2)
            return c

        lax.fori_loop(0, tc, body, 0, unroll=4)

    @pl.when(i == 0)
    def _():
        issue(0, 0)

    @pl.when(i + 1 < nsteps)
    def _():
        issue(i + 1, 1 - cur)

    pltpu.make_async_copy(buf.at[cur], buf.at[cur], sems.at[cur]).wait()
    w = [_bcast_col(wt_ref[...], k, LANES) for k in range(TOP_K)]
    for s in range(ROW_CHUNKS):
        cs = slice(s * LANES, (s + 1) * LANES)
        acc = x_ref[:, cs]
        for k in range(TOP_K):
            acc = acc + w[k] * buf[cur, k, pl.ds(s, tc, stride=SUBLANES), :]
        o_ref[:, cs] = acc


def _combine(slot_flat, ys, x2, wt):
    n = x2.shape[0]
    tc = COMBINE_TILE
    return pl.pallas_call(
        _combine_kernel,
        out_shape=jax.ShapeDtypeStruct((n, D_MODEL), F32),
        grid_spec=pltpu.PrefetchScalarGridSpec(
            num_scalar_prefetch=1,
            grid=(n // tc,),
            in_specs=[pl.BlockSpec(memory_space=pl.ANY),
                      pl.BlockSpec((tc, D_MODEL), lambda i, s: (i, 0)),
                      pl.BlockSpec((tc, LANES), lambda i, s: (i, 0))],
            out_specs=pl.BlockSpec((tc, D_MODEL), lambda i, s: (i, 0)),
            scratch_shapes=[pltpu.VMEM((2, TOP_K, tc * SUBLANES, LANES), F32), pltpu.SemaphoreType.DMA((2,))],
        ),
        compiler_params=_params(("arbitrary",)),
        name="moe_combine",
    )(slot_flat, ys, x2, wt)


def _moe(layer, h, x2, idx, wt, hist3, wgu, bgu, wd, bd):
    n = x2.shape[0]
    tm = MOE_TILE
    n_slots = n * TOP_K + N_EXPERTS * tm
    n_blocks = n_slots // tm
    hist = hist3[:, 0, :N_EXPERTS]
    counts = jnp.sum(hist, axis=0).astype(jnp.int32)
    padded = (counts + tm - 1) // tm * tm
    pad_end = jnp.cumsum(padded)
    pad_start = pad_end - padded
    tile_base = pad_start[None, :].astype(F32) + (jnp.cumsum(hist, axis=0) - hist)
    base3 = jnp.broadcast_to(jnp.pad(tile_base, ((0, 0), (0, LANES - N_EXPERTS)))[:, None, :],
                             (hist.shape[0], 8, LANES))
    blk0 = jnp.arange(n_blocks) * tm
    block_expert = jnp.minimum(jnp.sum(blk0[:, None] >= pad_end[None, :], axis=1), N_EXPERTS - 1).astype(jnp.int32)
    n_valid = (pad_end[-1:] // tm).astype(jnp.int32)
    zero_off = jnp.maximum(pad_end - tm, 0).astype(jnp.int32)

    slot_flat = _slots(idx, base3)[:TOP_K].reshape(-1)
    xs = _dispatch(slot_flat, zero_off, n_valid, h, n_slots)
    ys = _expert_ffn(layer, block_expert, n_valid, xs, wgu, bgu, wd, bd)
    return _combine(slot_flat, ys, x2, wt)


def _t5_bucket(dist):
    n = jnp.maximum(dist, 0)
    max_exact = REL_BUCKETS // 2
    nf = jnp.maximum(n, 1).astype(F32)
    large = max_exact + (jnp.log(nf / max_exact) / math.log(REL_MAX_DIST / max_exact)
                         * (REL_BUCKETS - max_exact)).astype(jnp.int32)
    large = jnp.minimum(large, REL_BUCKETS - 1)
    return jnp.where(n < max_exact, n, large)


def _bias_tables(rel_bias):
    rel_bias = rel_bias.astype(F32)
    tq = Q_TILE

    def lookup(dist):
        bucket = _t5_bucket(dist)
        out = jnp.zeros((NSA_HEADS,) + dist.shape, F32)
        for bk in range(REL_BUCKETS):
            out = jnp.where(bucket[None] == bk, rel_bias[bk].reshape((NSA_HEADS,) + (1,) * dist.ndim), out)
        return out

    t_pos = jnp.arange(SEQ)
    cmp_end = jnp.arange(N_CMP_PAD) * CMP_STRIDE + CMP_LEN - 1
    bias_cmp = lookup(t_pos[:, None] - cmp_end[None, :])
    rr = jnp.arange(tq)[:, None] - jnp.arange(tq)[None, :]
    btab = lookup(jnp.stack([rr, rr + tq, rr + 2 * tq]))
    return bias_cmp, btab


def _layer(layer, x2, b, p, experts, bias_cmp, btab):
    n = x2.shape[0]
    w = p['w_in']
    o_a, o_b, o_q, o_kv, o_g, o_u = 2048, 2052, 2056, 2312, 2696, 2708
    w_small = jnp.concatenate([w[:, o_a:o_q], w[:, o_g:o_u],
                               jnp.zeros((D_MODEL, LANES - 8 - 3 * NSA_HEADS), F32)], axis=1)
    w_cat = jnp.concatenate([w[:, :o_a], w[:, o_q:o_kv], w[:, o_kv:o_g], w[:, o_u:], w_small],
                            axis=1).astype(BF16)
    w_abt = jnp.concatenate([w[:, o_a:o_q].T, jnp.zeros((8, D_MODEL), F32)], axis=0).astype(BF16)
    qkv, z, nq, nkv, cu, small, abt = _in_proj(x2, p['attn_norm_w'][None, :], w_cat, w_abt)

    pcol = jnp.zeros((8, LANES), F32).at[0, :DN_HEADS].set(p['dn_a_log']).at[1, :DN_HEADS].set(p['dn_dt_bias'])
    prow = jnp.zeros((16, LANES), F32).at[:DN_HEADS, 0].set(p['dn_a_log']).at[:DN_HEADS, 1].set(p['dn_dt_bias'])
    small3 = small.reshape(b, SEQ, LANES)
    dq, dk, dv, gb, grow = _dn_prep(qkv.reshape(b, SEQ, 3 * DN_WIDTH), small3, abt,
                                    p['dn_conv_w'], pcol, prow)
    y_a = _delta_rule(dq, dk, dv, gb, grow, z.reshape(b, SEQ, DN_WIDTH), p['dn_norm_w'][None, :])

    y_c = _conformer(cu.reshape(b, SEQ, 2 * CONV_WIDTH), p['conv_dw_w'], p['conv_dw_b'][None, :],
                     p['conv_ln_w'][None, :], p['conv_ln_b'][None, :])

    kv6 = nkv.reshape(6, b, SEQ, NSA_HEAD_DIM)
    kcmp, vcmp, ksn, kwn = _nsa_prep(kv6, p['nsa_cmp_pos'], p['nsa_cmp_w1'].astype(BF16),
                                     p['nsa_cmp_w2'].astype(BF16), p['nsa_k_norm_w'])
    y_b = _nsa_attn(nq, p['nsa_q_norm_w'][None, :], kcmp, vcmp, ksn, kv6, kwn, small3, bias_cmp, btab)

    rw = jnp.concatenate([p['router_w'], jnp.zeros((D_MODEL, LANES - N_EXPERTS), F32)], axis=1)
    rb = jnp.concatenate([p['router_b'], jnp.full((LANES - N_EXPERTS,), NEG_INF, F32)])[None, :]
    x_new, h, idx, wt, hist3 = _out_router(y_a.reshape(n, DN_WIDTH), y_b, y_c.reshape(n, CONV_WIDTH), x2,
                                           p['w_out'].astype(BF16), p['ffn_norm_w'][None, :], rw, rb)
    return _moe(layer, h, x_new, idx, wt, hist3, *experts)


def kernel(x, attn_norm_w, w_in, dn_conv_w, dn_a_log, dn_dt_bias, dn_norm_w, nsa_q_norm_w, nsa_k_norm_w, nsa_cmp_pos, nsa_cmp_w1, nsa_cmp_w2, conv_dw_w, conv_dw_b, conv_ln_w, conv_ln_b, w_out, ffn_norm_w, router_w, router_b, w_gate_up, b_gate_up, w_down, b_down, rel_bias):
    b, t, d = x.shape
    assert (t, d) == (SEQ, D_MODEL)
    stacked = dict(attn_norm_w=attn_norm_w, w_in=w_in, dn_conv_w=dn_conv_w, dn_a_log=dn_a_log,
                   dn_dt_bias=dn_dt_bias, dn_norm_w=dn_norm_w, nsa_q_norm_w=nsa_q_norm_w,
                   nsa_k_norm_w=nsa_k_norm_w, nsa_cmp_pos=nsa_cmp_pos, nsa_cmp_w1=nsa_cmp_w1,
                   nsa_cmp_w2=nsa_cmp_w2, conv_dw_w=conv_dw_w, conv_dw_b=conv_dw_b,
                   conv_ln_w=conv_ln_w, conv_ln_b=conv_ln_b, w_out=w_out, ffn_norm_w=ffn_norm_w,
                   router_w=router_w, router_b=router_b)
    experts = (w_gate_up, b_gate_up[:, :, None, :], w_down, b_down[:, :, None, :])
    bias_cmp, btab = _bias_tables(rel_bias)
    x2 = x.reshape(b * t, d)
    for l in range(w_in.shape[0]):
        x2 = _layer(l, x2, b, {k: v[l] for k, v in stacked.items()}, experts, bias_cmp, btab)
    return x2.reshape(b, t, d)
```

```python
import functools
import math

import numpy as np
import jax
import jax.numpy as jnp
from jax import lax
from jax.experimental import pallas as pl
from jax.experimental.pallas import tpu as pltpu

F32 = jnp.float32
BF16 = jnp.bfloat16

D_MODEL = 1024
SEQ = 2048
DN_HEADS = 4
DN_HEAD_DIM = 128
DN_WIDTH = DN_HEADS * DN_HEAD_DIM
DN_CONV = 4
DN_CHUNK = 64
NSA_HEADS = 4
NSA_HEAD_DIM = 64
NSA_WIDTH = NSA_HEADS * NSA_HEAD_DIM
CMP_LEN = 32
CMP_STRIDE = 16
CMP_HIDDEN = 2 * NSA_HEAD_DIM
SLC_BLOCK = 64
SLC_TOP_N = 16
WINDOW = 512
CONV_WIDTH = 256
CONV_KERNEL = 31
REL_BUCKETS = 32
REL_MAX_DIST = 128
N_EXPERTS = 32
TOP_K = 4
D_FF = D_MODEL
SWIGLU_LIMIT = 7.0
SWIGLU_ALPHA = 1.702
EPS = 1e-6
NEG_INF = -1e30
FORCE = 1e4

LANES = 128
VMEM_LIMIT_BYTES = 48 * 1024 * 1024
EXPERT_VMEM_LIMIT_BYTES = 56 * 1024 * 1024

ROW_TILE = 512
SEQ_TILE = 256
Q_TILE = 128
MOE_TILE = 512
COMBINE_TILE = 256

N_CMP_PAD = 128
N_SLC = SEQ // SLC_BLOCK


def _params(sem=None):
    return pltpu.CompilerParams(dimension_semantics=sem, vmem_limit_bytes=VMEM_LIMIT_BYTES)


def _mm(a, b):
    return jnp.dot(a.astype(BF16), b.astype(BF16), preferred_element_type=F32)


def _mm_nt(a, b):
    return lax.dot_general(a.astype(BF16), b.astype(BF16), (((1,), (1,)), ((), ())),
                           preferred_element_type=F32)


def _mm_tn(a, b):
    return lax.dot_general(a.astype(BF16), b.astype(BF16), (((0,), (0,)), ((), ())),
                           preferred_element_type=F32)


def _split3(x):
    hi = x.astype(BF16)
    r1 = x - hi.astype(F32)
    mid = r1.astype(BF16)
    lo = (r1 - mid.astype(F32)).astype(BF16)
    return hi, mid, lo


def _dot01_right(x, m01):
    hi, mid, lo = _split3(x)
    d = lambda p: jnp.dot(p, m01, preferred_element_type=F32)
    return d(hi) + d(mid) + d(lo)


def _dot01_left(m01, x):
    hi, mid, lo = _split3(x)
    d = lambda p: jnp.dot(m01, p, preferred_element_type=F32)
    return d(hi) + d(mid) + d(lo)


def _dot_f32(a, b):
    a_hi = a.astype(BF16)
    a_lo = (a - a_hi.astype(F32)).astype(BF16)
    b_hi = b.astype(BF16)
    b_lo = (b - b_hi.astype(F32)).astype(BF16)
    d = lambda p, q: jnp.dot(p, q, preferred_element_type=F32)
    return d(a_hi, b_hi) + d(a_hi, b_lo) + d(a_lo, b_hi) + d(a_lo, b_lo)


def _sigmoid(x):
    return 1.0 / (1.0 + jnp.exp(-x))


def _silu(x):
    return x * _sigmoid(x)


def _softplus(x):
    return jnp.maximum(x, 0.0) + jnp.log(1.0 + jnp.exp(-jnp.abs(x)))


SUBLANES = 8
ROW_CHUNKS = D_MODEL // LANES
assert ROW_CHUNKS == SUBLANES


def _store_tile_rows(ref, value, rows):
    for s in range(ROW_CHUNKS):
        ref[pl.ds(s, rows, stride=SUBLANES), :] = value[:, s * LANES:(s + 1) * LANES]


def _load_tile_rows(ref, rows):
    return jnp.concatenate([ref[pl.ds(s, rows, stride=SUBLANES), :] for s in range(ROW_CHUNKS)], axis=1)


def _bcast_col(x, j, width):
    return jnp.broadcast_to(x[:, j:j + 1], (x.shape[0], width))


IN_SEGS = (3 * DN_WIDTH, DN_WIDTH, NSA_WIDTH, 6 * NSA_HEAD_DIM, 2 * CONV_WIDTH, LANES)
IN_COLS = sum(IN_SEGS)


def _delta_prepare(q_ref, k_ref, v_ref, gb_ref, grow_ref, wy):
    u_ref, w_ref, attn_ref, qg_ref, kk_ref, gc_ref = wy
    ts = SEQ_TILE
    ch = DN_CHUNK
    hd = DN_HEAD_DIM
    r = lax.broadcasted_iota(jnp.int32, (ts, ts), 0)
    c = lax.broadcasted_iota(jnp.int32, (ts, ts), 1)
    same_chunk = (r // ch) == (c // ch)
    tril = same_chunk & (r >= c)
    strict = same_chunk & (r > c)
    same16 = (r // 16) == (c // 16)
    eye = jnp.where(r == c, 1.0, 0.0).astype(F32)
    m_col = jnp.where(tril, 1.0, 0.0).astype(BF16)
    m_row = jnp.where(same_chunk & (r <= c), 1.0, 0.0).astype(BF16)

    gb = gb_ref[0]
    gc_col = _dot01_left(m_col, gb)
    gc_row = _dot01_right(grow_ref[...], m_row)

    heads = range(DN_HEADS)
    hsl = [slice(h * hd, (h + 1) * hd) for h in heads]
    kh = [k_ref[0, :, hsl[h]] for h in heads]
    gcb = [_bcast_col(gc_col, h, ts) for h in heads]
    decay = [jnp.where(tril, jnp.exp(jnp.where(
        tril, gcb[h] - jnp.broadcast_to(gc_row[h:h + 1, :], (ts, ts)), 0.0)), 0.0) for h in heads]
    beta = [_bcast_col(gb, DN_HEADS + h, hd) for h in heads]
    kb = [kh[h] * beta[h] for h in heads]
    yield
    a_mat = [jnp.where(strict, _mm_nt(kb[h], kh[h]) * decay[h], 0.0) for h in heads]
    yield
    d_mat = [jnp.where(same16, a_mat[h], 0.0) for h in heads]
    e_mat = [a_mat[h] - d_mat[h] for h in heads]
    d2 = [_mm(d_mat[h], d_mat[h]) for h in heads]
    yield
    d4 = [_mm(d2[h], d2[h]) for h in heads]
    yield
    t1 = [_mm(eye - d_mat[h], eye + d2[h]) for h in heads]
    yield
    d8 = [_mm(d4[h], d4[h]) for h in heads]
    yield
    t2 = [_mm(t1[h], eye + d4[h]) for h in heads]
    yield
    p_mat = [_mm(t2[h], eye + d8[h]) for h in heads]
    yield
    m_mat = [_mm(p_mat[h], e_mat[h]) for h in heads]
    yield
    m2 = [_mm(m_mat[h], m_mat[h]) for h in heads]
    yield
    t3 = [_mm(eye - m_mat[h], eye + m2[h]) for h in heads]
    yield
    t_mat = [_mm(t3[h], p_mat[h]) for h in heads]
    yield
    for h in heads:
        gc128 = gcb[h][:, :hd]
        expg = jnp.exp(gc128)
        qh = q_ref[0, :, hsl[h]]
        sol = _mm(t_mat[h], jnp.concatenate([v_ref[0, :, hsl[h]] * beta[h], kb[h] * expg], axis=1))
        u_ref[h] = sol[:, :hd]
        w_ref[h] = sol[:, hd:].astype(BF16)
        qg_ref[h] = (qh * expg).astype(BF16)
        kk_ref[h] = kh[h]
        gc_ref[h] = gc128
    yield
    for h in heads:
        attn_ref[h] = jnp.where(tril, _mm_nt(q_ref[0, :, hsl[h]], kh[h]) * decay[h], 0.0).astype(BF16)


def _delta_recur(wy, z_ref, nw_ref, o_ref, s_ref):
    u_ref, w_ref, attn_ref, qg_ref, kk_ref, gc_ref = wy
    ch = DN_CHUNK
    hd = DN_HEAD_DIM
    heads = range(DN_HEADS)
    state = [s_ref[h] for h in heads]
    outs = [[] for _ in heads]
    for ci in range(SEQ_TILE // ch):
        rs = slice(ci * ch, (ci + 1) * ch)
        v_new = [u_ref[h, rs, :] - _mm(w_ref[h, rs, :], state[h]) for h in heads]
        o_state = [_mm(qg_ref[h, rs, :], state[h]) for h in heads]
        yield
        for h in heads:
            gc = gc_ref[h, rs, :]
            g_last = gc[ch - 1:ch, :]
            k_dec = kk_ref[h, rs, :] * jnp.exp(g_last - gc)
            outs[h].append(o_state[h] + _mm(attn_ref[h, rs, ci * ch:(ci + 1) * ch], v_new[h]))
            state[h] = state[h] * jnp.exp(g_last) + _mm_tn(k_dec, v_new[h])
        yield

    for h in heads:
        s_ref[h] = state[h]
        hs = slice(h * hd, (h + 1) * hd)
        o = jnp.concatenate(outs[h], axis=0)
        o = o * lax.rsqrt(jnp.mean(o * o, axis=-1, keepdims=True) + EPS) * nw_ref[...]
        o_ref[0, :, hs] = o * _silu(z_ref[0, :, hs])


def _interleave(weighted):
    live = [gen for gen, _ in weighted]
    while live:
        for gen, n in weighted:
            for _ in range(n):
                if gen in live and next(gen, live) is live:
                    live.remove(gen)


DN_BATCH = 2


def _delta_kernel(q_ref, k_ref, v_ref, gb_ref, grow0_ref, grow1_ref, z_ref, nw_ref, o_ref, s_ref,
                  *wy_refs):
    t = pl.program_id(1)
    set_a, set_b = wy_refs[:6], wy_refs[6:]
    grow_refs = (grow0_ref, grow1_ref)

    @pl.when(t == 0)
    def _():
        s_ref[...] = jnp.zeros(s_ref.shape, F32)
        for ref in set_b:
            ref[...] = jnp.zeros(ref.shape, ref.dtype)

    def step(read_set, write_set):
        one = lambda ref, bb: ref.at[pl.ds(bb, 1)]
        work = []
        for bb in range(DN_BATCH):
            work.append((_delta_prepare(one(q_ref, bb), one(k_ref, bb), one(v_ref, bb), one(gb_ref, bb),
                                        grow_refs[bb], [ref.at[bb] for ref in write_set]), 2))
        for bb in range(DN_BATCH):
            work.append((_delta_recur([ref.at[bb] for ref in read_set], one(z_ref, bb), nw_ref,
                                      one(o_ref, bb), s_ref.at[bb]), 1))
        _interleave(work)

    @pl.when(lax.rem(t, 2) == 0)
    def _():
        step(set_b, set_a)

    @pl.when(lax.rem(t, 2) == 1)
    def _():
        step(set_a, set_b)


def _delta_rule(q, k, v, gb, grow, z, norm_w):
    b = q.shape[0]
    nb = DN_BATCH
    assert b % nb == 0
    ts = SEQ_TILE
    nt = SEQ // ts
    nxt = lambda w: pl.BlockSpec((nb, ts, w), lambda i, t: (i, jnp.minimum(t, nt - 1), 0))
    cur = lambda w: pl.BlockSpec((nb, ts, w), lambda i, t: (i, jnp.maximum(t - 1, 0), 0))
    grow_spec = lambda bb: pl.BlockSpec(
        (8, ts), lambda i, t: (0, (i * nb + bb) * nt + jnp.minimum(t, nt - 1)))
    hd = DN_HEAD_DIM
    wy_set = [pltpu.VMEM((nb, DN_HEADS, ts, hd), F32), pltpu.VMEM((nb, DN_HEADS, ts, hd), BF16),
              pltpu.VMEM((nb, DN_HEADS, ts, ts), BF16), pltpu.VMEM((nb, DN_HEADS, ts, hd), BF16),
              pltpu.VMEM((nb, DN_HEADS, ts, hd), F32), pltpu.VMEM((nb, DN_HEADS, ts, hd), F32)]
    return pl.pallas_call(
        _delta_kernel,
        out_shape=jax.ShapeDtypeStruct((b, SEQ, DN_WIDTH), F32),
        grid=(b // nb, nt + 1),
        in_specs=[
            nxt(DN_WIDTH), nxt(DN_WIDTH), nxt(DN_WIDTH), nxt(LANES),
            grow_spec(0), grow_spec(1),
            cur(DN_WIDTH),
            pl.BlockSpec((1, DN_HEAD_DIM), lambda i, t: (0, 0)),
        ],
        out_specs=cur(DN_WIDTH),
        scratch_shapes=[pltpu.VMEM((nb, DN_HEADS, hd, hd), F32)] + wy_set + wy_set,
        compiler_params=_params(("parallel", "arbitrary")),
        name="delta_rule",
    )(q, k, v, gb, grow, grow, z, norm_w)


CONV_HALO = 32


MIX_COLS = {'qkv': (0, 3 * DN_WIDTH), 'z': (3 * DN_WIDTH, 4 * DN_WIDTH)}
_off = 4 * DN_WIDTH
for _name, _w in (('nq', NSA_WIDTH), ('nkv', 6 * NSA_HEAD_DIM), ('cu', 2 * CONV_WIDTH), ('small', LANES)):
    MIX_COLS[_name] = (_off, _off + _w)
    _off += _w
assert _off == IN_COLS
DN_HALO = 8
MXU_CHUNK = 256


def _front_kernel(x_ref, nw_ref, w_ref, wabt_ref, cw_ref, pcol_ref, prow_ref,
                  dww_ref, dwb_ref, lnw_ref, lnb_ref,
                  q_ref, k_ref, v_ref, gb_ref, grow_ref, z_ref, nq_ref, nkv_ref, small_ref, yc_ref,
                  dn_buf, cf_buf, cf_shift):
    i = pl.program_id(0)
    tm = ROW_TILE
    hd = NSA_HEAD_DIM

    @pl.when(lax.rem(i, SEQ // tm) == 0)
    def _():
        dn_buf[0:DN_HALO, :] = jnp.zeros((DN_HALO, 3 * DN_WIDTH), F32)
        cf_buf[0:CONV_HALO, :] = jnp.zeros((CONV_HALO, CONV_WIDTH), F32)

    xf = x_ref[...]
    hb = (xf * lax.rsqrt(jnp.mean(xf * xf, axis=-1, keepdims=True) + EPS) * nw_ref[...]).astype(BF16)

    def proj(name, lo=0, hi=None):
        c0, c1 = MIX_COLS[name]
        hi = c1 - c0 if hi is None else hi
        return jnp.dot(hb, w_ref[:, c0 + lo:c0 + hi], preferred_element_type=F32)

    def mxu_items():
        cu = proj('cu')
        cf_buf[CONV_HALO:CONV_HALO + tm, :] = cu[:, :CONV_WIDTH] * _sigmoid(cu[:, CONV_WIDTH:])
        yield
        sm = proj('small')
        small_ref[...] = sm
        lane = lax.broadcasted_iota(jnp.int32, sm.shape, 1)
        g_col = -jnp.exp(pcol_ref[0:1, :]) * _softplus(sm + pcol_ref[1:2, :])
        gb_ref[...] = jnp.where(lane < DN_HEADS, g_col, _sigmoid(sm))
        a_t = lax.dot_general(wabt_ref[...], hb, (((1,), (1,)), ((), ())), preferred_element_type=F32)
        g_row = -jnp.exp(prow_ref[:, 0:1]) * _softplus(a_t + prow_ref[:, 1:2])
        grow_ref[...] = g_row[0:8, :]
        yield
        for c0 in range(0, 3 * DN_WIDTH, MXU_CHUNK):
            dn_buf[DN_HALO:DN_HALO + tm, c0:c0 + MXU_CHUNK] = proj('qkv', c0, c0 + MXU_CHUNK)
            yield
        for c0 in range(0, DN_WIDTH, MXU_CHUNK):
            z_ref[:, c0:c0 + MXU_CHUNK] = proj('z', c0, c0 + MXU_CHUNK)
            yield
        res = proj('nq')
        for j in range(NSA_HEADS):
            nq_ref[j] = res[:, j * hd:(j + 1) * hd]
        yield
        res = proj('nkv')
        for j in range(6):
            nkv_ref[j] = res[:, j * hd:(j + 1) * hd]

    def conformer_items():
        span = tm + CONV_HALO - SUBLANES
        for b in range(1, SUBLANES):
            cf_shift[b] = cf_buf[b:b + span, :]
        yield
        base = CONV_HALO - (CONV_KERNEL - 1)
        rows = 64
        for rc in range(tm // rows):
            parts = []
            for cc in range(CONV_WIDTH // LANES):
                cs = slice(cc * LANES, (cc + 1) * LANES)
                acc = None
                for j in range(CONV_KERNEL):
                    start = base + rc * rows + j
                    b = start % SUBLANES
                    a0 = start - b
                    win = cf_buf[a0:a0 + rows, cs] if b == 0 else cf_shift[b, a0:a0 + rows, cs]
                    term = dww_ref[j:j + 1, cs] * win
                    acc = term if acc is None else acc + term
                parts.append(acc)
            h = jnp.concatenate(parts, axis=1) + dwb_ref[...]
            mu = jnp.mean(h, axis=-1, keepdims=True)
            var = jnp.mean(jnp.square(h - mu), axis=-1, keepdims=True)
            hn = (h - mu) * lax.rsqrt(var + EPS) * lnw_ref[...] + lnb_ref[...]
            yc_ref[rc * rows:(rc + 1) * rows, :] = _silu(hn)
            yield
        cf_buf[0:CONV_HALO, :] = cf_buf[tm:tm + CONV_HALO, :]

    def deltanet_items():
        rows = tm // 2
        for c in range(3 * DN_HEADS):
            cs = slice(c * LANES, (c + 1) * LANES)
            for r0 in range(0, tm, rows):
                lo = DN_HALO - (DN_CONV - 1) + r0
                acc = cw_ref[0:1, cs] * dn_buf[lo:lo + rows, cs]
                for j in range(1, DN_CONV):
                    acc = acc + cw_ref[j:j + 1, cs] * dn_buf[lo + j:lo + j + rows, cs]
                y = _silu(acc)
                if c < 2 * DN_HEADS:
                    y = y * lax.rsqrt(jnp.sum(y * y, axis=-1, keepdims=True) + EPS)
                rs = slice(r0, r0 + rows)
                if c < DN_HEADS:
                    q_ref[rs, cs] = y * DN_HEAD_DIM ** -0.5
                elif c < 2 * DN_HEADS:
                    k_ref[rs, (c - DN_HEADS) * LANES:(c - DN_HEADS + 1) * LANES] = y
                else:
                    v_ref[rs, (c - 2 * DN_HEADS) * LANES:(c - 2 * DN_HEADS + 1) * LANES] = y
            dn_buf[0:DN_HALO, cs] = dn_buf[tm:tm + DN_HALO, cs]
            yield

    mxu, cf, dn = mxu_items(), conformer_items(), deltanet_items()
    next(mxu)
    next(mxu)
    next(cf)
    next(mxu)
    _interleave([(mxu, 1), (dn, 2), (cf, 2)])


def _front(x2, norm_w, w_cat, w_abt, conv_w, pcol, prow, dw_w, dw_b, ln_w, ln_b):
    n = x2.shape[0]
    tm = ROW_TILE
    hd = NSA_HEAD_DIM
    row = lambda w: pl.BlockSpec((tm, w), lambda i: (i, 0))
    full = lambda shape: pl.BlockSpec(shape, lambda i: (0,) * len(shape))
    pieces = lambda k: pl.BlockSpec((k, tm, hd), lambda i: (0, i, 0))
    return pl.pallas_call(
        _front_kernel,
        out_shape=[jax.ShapeDtypeStruct((n, DN_WIDTH), F32)] * 3
        + [jax.ShapeDtypeStruct((n, LANES), F32), jax.ShapeDtypeStruct((8, n), F32),
           jax.ShapeDtypeStruct((n, DN_WIDTH), F32),
           jax.ShapeDtypeStruct((NSA_HEADS, n, hd), F32), jax.ShapeDtypeStruct((6, n, hd), F32),
           jax.ShapeDtypeStruct((n, LANES), F32), jax.ShapeDtypeStruct((n, CONV_WIDTH), F32)],
        grid=(n // tm,),
        in_specs=[row(D_MODEL), full((1, D_MODEL)), full((D_MODEL, IN_COLS)), full((16, D_MODEL)),
                  full((DN_CONV, 3 * DN_WIDTH)), full((8, LANES)), full((16, LANES)),
                  full((CONV_KERNEL, CONV_WIDTH)), full((1, CONV_WIDTH)), full((1, CONV_WIDTH)),
                  full((1, CONV_WIDTH))],
        out_specs=[row(DN_WIDTH)] * 3
        + [row(LANES), pl.BlockSpec((8, tm), lambda i: (0, i)), row(DN_WIDTH),
           pieces(NSA_HEADS), pieces(6), row(LANES), row(CONV_WIDTH)],
        scratch_shapes=[pltpu.VMEM((tm + DN_HALO, 3 * DN_WIDTH), F32),
                        pltpu.VMEM((tm + CONV_HALO, CONV_WIDTH), F32),
                        pltpu.VMEM((SUBLANES, tm + CONV_HALO - SUBLANES, CONV_WIDTH), F32)],
        compiler_params=_params(("arbitrary",)),
        name="front",
    )(x2, norm_w, w_cat, w_abt, conv_w, pcol, prow, dw_w, dw_b, ln_w, ln_b)


def _rms_rows(x, w):
    return x * lax.rsqrt(jnp.mean(x * x, axis=-1, keepdims=True) + EPS) * w


def _nsa_prep_kernel(kc_ref, vc_ref, ks_ref, kw_ref, pos_ref, w1_ref, w2_ref, knw_ref,
                     kcmp_ref, vcmp_ref, ksn_ref, kwn_ref):
    hd = NSA_HEAD_DIM

    def compress(x_ref, i):
        u_lo = jnp.zeros((N_CMP_PAD, CMP_HIDDEN), F32)
        u_hi = jnp.zeros((N_CMP_PAD, CMP_HIDDEN), F32)
        for r in range(CMP_STRIDE):
            xr = x_ref[pl.ds(r, N_CMP_PAD, stride=CMP_STRIDE), :]
            lo, hi = r, CMP_STRIDE + r
            u_lo = u_lo + _mm(xr + pos_ref[i, lo:lo + 1, :], w1_ref[i, lo * hd:(lo + 1) * hd, :])
            u_hi = u_hi + _mm(xr + pos_ref[i, hi:hi + 1, :], w1_ref[i, hi * hd:(hi + 1) * hd, :])
        hid = _silu(u_lo + pltpu.roll(u_hi, N_CMP_PAD - 1, axis=0))
        return _mm(hid, w2_ref[i])

    kcmp_ref[0] = _rms_rows(compress(kc_ref, 0), knw_ref[0:1, :])
    vcmp_ref[0] = compress(vc_ref, 1)
    ksn_ref[0] = _rms_rows(ks_ref[...], knw_ref[1:2, :])
    kwn_ref[0] = _rms_rows(kw_ref[...], knw_ref[2:3, :])


def _nsa_prep(kv6, pos, w1, w2, knw):
    b = kv6.shape[1]
    hd = NSA_HEAD_DIM
    full = lambda shape: pl.BlockSpec(shape, lambda i: (0,) * len(shape))
    piece = lambda j: pl.BlockSpec((None, None, SEQ, hd), lambda i: (j, i, 0, 0))
    bspec = lambda r, w: pl.BlockSpec((1, r, w), lambda i: (i, 0, 0))
    return pl.pallas_call(
        _nsa_prep_kernel,
        out_shape=[jax.ShapeDtypeStruct((b, N_CMP_PAD, hd), F32)] * 2
        + [jax.ShapeDtypeStruct((b, SEQ, hd), F32)] * 2,
        grid=(b,),
        in_specs=[
            piece(0), piece(1), piece(2), piece(4),
            full((2, CMP_LEN, hd)), full((2, CMP_LEN * hd, CMP_HIDDEN)), full((2, CMP_HIDDEN, hd)),
            full((3, hd)),
        ],
        out_specs=[bspec(N_CMP_PAD, hd)] * 2 + [bspec(SEQ, hd)] * 2,
        compiler_params=_params(("parallel",)),
        name="nsa_prep",
    )(kv6, kv6, kv6, kv6, pos, w1, w2, knw)


def _nsa_attn_kernel(q_ref, qnw_ref, kcmp_ref, vcmp_ref, ks_ref, vs_ref, kw_ref, vw_ref,
                     gate_ref, bcmp_ref, btab_ref, o_ref,
                     madd_ref, s_ref, ksb_ref, kwb_ref, vsa_ref, vwa_ref):
    i = pl.program_id(1)
    tq = Q_TILE
    nh = NSA_HEADS
    hd = NSA_HEAD_DIM
    t0 = i * tq

    @pl.when(i == 0)
    def _():
        ones = jnp.ones((SEQ, hd), BF16)
        ksb_ref[...] = ks_ref[0].astype(BF16)
        kwb_ref[...] = kw_ref[0].astype(BF16)
        vsa_ref[...] = jnp.concatenate([vs_ref[...].astype(BF16), ones], axis=1)
        vwa_ref[...] = jnp.concatenate([vw_ref[...].astype(BF16), ones], axis=1)

    qs = jnp.concatenate(
        [_rms_rows(q_ref[h], qnw_ref[...]) * hd ** -0.5 for h in range(nh)], axis=0).astype(BF16)

    row = lax.broadcasted_iota(jnp.int32, (tq, LANES), 0)
    lane = lax.broadcasted_iota(jnp.int32, (tq, LANES), 1)
    qpos = t0 + row

    s_all = _mm_nt(qs, kcmp_ref[0])
    cmp_valid = (qpos >= lane * CMP_STRIDE + (CMP_LEN - 1)) & (lane < N_CMP_PAD - 1)
    p_rows = []
    p_sum = jnp.zeros((tq, LANES), F32)
    for h in range(nh):
        s = jnp.where(cmp_valid, s_all[h * tq:(h + 1) * tq] + bcmp_ref[h], NEG_INF)
        m = jnp.max(s, axis=-1, keepdims=True)
        p = jnp.where(cmp_valid, jnp.exp(s - m), 0.0)
        l = jnp.sum(p, axis=-1, keepdims=True)
        p = p * jnp.where(l > 0.0, 1.0 / l, 0.0)
        p_rows.append(p)
        p_sum = p_sum + p
    o_cmp = _mm(jnp.concatenate(p_rows, axis=0), vcmp_ref[0])

    ss = lax.broadcasted_iota(jnp.int32, (N_SLC, N_CMP_PAD), 0)
    jj = lax.broadcasted_iota(jnp.int32, (N_SLC, N_CMP_PAD), 1)
    overlap_t = ((jj * CMP_STRIDE < ss * SLC_BLOCK + SLC_BLOCK)
                 & (jj * CMP_STRIDE + CMP_LEN > ss * SLC_BLOCK) & (jj < N_CMP_PAD - 1))
    overlap_t = jnp.where(overlap_t, 1.0, 0.0).astype(BF16)
    nt = lambda a, b_: lax.dot_general(a, b_, (((1,), (1,)), ((), ())), preferred_element_type=F32)
    p_hi, p_mid, p_lo = _split3(p_sum)
    imp = nt(overlap_t, p_hi) + nt(overlap_t, p_mid) + nt(overlap_t, p_lo)
    blk = lax.broadcasted_iota(jnp.int32, (N_SLC, tq), 0)
    cur = (t0 + lax.broadcasted_iota(jnp.int32, (N_SLC, tq), 1)) // SLC_BLOCK
    causal_blk = blk <= cur
    forced = (blk == 0) | (blk == cur) | (blk == cur - 1)
    imp = jnp.where(causal_blk & forced, FORCE, jnp.where(causal_blk, imp, -1.0))
    rank = jnp.zeros((N_SLC, tq), jnp.int32)
    for s2 in range(N_SLC):
        other = jnp.broadcast_to(imp[s2:s2 + 1, :], (N_SLC, tq))
        beats = (other > imp) | ((other == imp) & (blk > s2))
        rank = rank + jnp.where(beats, 1, 0)
    sel_t = jnp.where(rank < SLC_TOP_N, 1.0, 0.0).astype(BF16)
    er = lax.broadcasted_iota(jnp.int32, (N_SLC, SEQ), 0)
    ec = lax.broadcasted_iota(jnp.int32, (N_SLC, SEQ), 1)
    expand = jnp.where(ec // SLC_BLOCK == er, 1.0, 0.0).astype(BF16)
    key = lax.broadcasted_iota(jnp.int32, (tq, SEQ), 1)
    qall = t0 + lax.broadcasted_iota(jnp.int32, (tq, SEQ), 0)
    chosen = (_mm_tn(sel_t, expand) > 0.5) & (qall >= key)
    madd_ref[...] = jnp.where(chosen, 0.0, NEG_INF)

    neg = jnp.full((nh * tq, tq), NEG_INF, F32)
    zero = jnp.zeros((nh * tq, LANES), F32)

    def scores(k_tile, bias_idx, add):
        s_t = _mm_nt(qs, k_tile)
        return jnp.concatenate(
            [s_t[h * tq:(h + 1) * tq] + (btab_ref[h, bias_idx] + add) for h in range(nh)], axis=0)

    def normalise(acc):
        return acc[:, :hd] * (1.0 / _bcast_col(acc, hd, hd))

    n_pairs = (i + 2) // 2

    def slc_pass1(j, m_run):
        for u in range(2):
            kt = 2 * j + u
            k0 = pl.multiple_of(kt * tq, tq)
            s_m = scores(ksb_ref[pl.ds(k0, tq), :], jnp.clip(i - kt, 0, 2), madd_ref[:, pl.ds(k0, tq)])
            s_ref[:, pl.ds(k0, tq)] = s_m
            m_run = jnp.maximum(m_run, s_m)
        return m_run

    m_slc = jnp.broadcast_to(
        jnp.max(lax.fori_loop(0, n_pairs, slc_pass1, neg), axis=-1, keepdims=True), (nh * tq, tq))

    def slc_pass2(j, acc):
        for u in range(2):
            k0 = pl.multiple_of((2 * j + u) * tq, tq)
            p = jnp.exp(s_ref[:, pl.ds(k0, tq)] - m_slc).astype(BF16)
            acc = acc + jnp.dot(p, vsa_ref[pl.ds(k0, tq), :], preferred_element_type=F32)
        return acc

    o_slc = normalise(lax.fori_loop(0, n_pairs, slc_pass2, zero))

    kcol = lax.broadcasted_iota(jnp.int32, (tq, tq), 1)
    qrow = t0 + lax.broadcasted_iota(jnp.int32, (tq, tq), 0)
    n_win = WINDOW // tq + 1
    starts = []
    m_run = neg
    for j in range(n_win):
        d = n_win - 1 - j
        k0 = pl.multiple_of(jnp.maximum(i - d, 0) * tq, tq)
        starts.append(k0)
        dist = qrow - (k0 + kcol)
        ok = (dist >= 0) & (dist < WINDOW) & (i >= d)
        s_m = scores(kwb_ref[pl.ds(k0, tq), :], min(d, 2), jnp.where(ok, 0.0, NEG_INF))
        s_ref[:, j * tq:(j + 1) * tq] = s_m
        m_run = jnp.maximum(m_run, s_m)
    m_win = jnp.broadcast_to(jnp.max(m_run, axis=-1, keepdims=True), (nh * tq, tq))
    acc = zero
    for j in range(n_win):
        p = jnp.exp(s_ref[:, j * tq:(j + 1) * tq] - m_win).astype(BF16)
        acc = acc + jnp.dot(p, vwa_ref[pl.ds(starts[j], tq), :], preferred_element_type=F32)
    o_win = normalise(acc)

    gates = _sigmoid(gate_ref[0])
    outs = []
    for h in range(nh):
        hs = slice(h * tq, (h + 1) * tq)
        g0 = 2 * DN_HEADS + 3 * h
        outs.append(_bcast_col(gates, g0, hd) * o_cmp[hs]
                    + _bcast_col(gates, g0 + 1, hd) * o_slc[hs]
                    + _bcast_col(gates, g0 + 2, hd) * o_win[hs])
    o_ref[...] = jnp.concatenate(outs, axis=1)


def _nsa_attn(q, qnw, kcmp, vcmp, ks, kv6, kw, small, bias_cmp, btab):
    b = kv6.shape[1]
    tq = Q_TILE
    nq = SEQ // tq
    nh, hd = NSA_HEADS, NSA_HEAD_DIM
    full = lambda shape: pl.BlockSpec(shape, lambda bi, i: (0,) * len(shape))
    per_b = lambda r: pl.BlockSpec((1, r, hd), lambda bi, i: (bi, 0, 0))
    piece = lambda j: pl.BlockSpec((None, None, SEQ, hd), lambda bi, i: (j, bi, 0, 0))
    heads = pl.BlockSpec((nh, tq, hd), lambda bi, i: (0, bi * nq + i, 0))
    return pl.pallas_call(
        _nsa_attn_kernel,
        out_shape=jax.ShapeDtypeStruct((b * SEQ, NSA_WIDTH), F32),
        grid=(b, nq),
        in_specs=[
            heads,
            full((1, hd)),
            per_b(N_CMP_PAD), per_b(N_CMP_PAD), per_b(SEQ), piece(3), per_b(SEQ), piece(5),
            pl.BlockSpec((1, tq, LANES), lambda bi, i: (bi, i, 0)),
            pl.BlockSpec((nh, tq, N_CMP_PAD), lambda bi, i: (0, i, 0)),
            full((nh, 3, tq, tq)),
        ],
        out_specs=pl.BlockSpec((tq, NSA_WIDTH), lambda bi, i: (bi * nq + i, 0)),
        scratch_shapes=[pltpu.VMEM((tq, SEQ), F32), pltpu.VMEM((nh * tq, SEQ), F32),
                        pltpu.VMEM((SEQ, hd), BF16), pltpu.VMEM((SEQ, hd), BF16),
                        pltpu.VMEM((SEQ, 2 * hd), BF16), pltpu.VMEM((SEQ, 2 * hd), BF16)],
        compiler_params=_params(("parallel", "arbitrary")),
        name="nsa_attn",
    )(q, qnw, kcmp, vcmp, ks, kv6, kw, kv6, small, bias_cmp, btab)


def _out_router_kernel(ya_ref, yb_ref, yc_ref, x_ref, wo_ref, fnw_ref, rw_ref, rb_ref,
                       xo_ref, h_ref, idx_ref, wt_ref, hist_ref):
    wa = DN_WIDTH
    wb = wa + NSA_WIDTH
    y = (jnp.dot(ya_ref[...].astype(BF16), wo_ref[0:wa, :], preferred_element_type=F32)
         + jnp.dot(yb_ref[...].astype(BF16), wo_ref[wa:wb, :], preferred_element_type=F32)
         + jnp.dot(yc_ref[...].astype(BF16), wo_ref[wb:, :], preferred_element_type=F32))
    xn = x_ref[...] + y
    xo_ref[...] = xn
    h = xn * lax.rsqrt(jnp.mean(xn * xn, axis=-1, keepdims=True) + EPS) * fnw_ref[...]
    _store_tile_rows(h_ref, h, h.shape[0])
    logits = _dot_f32(h, rw_ref[...]) + rb_ref[...]
    lane = lax.broadcasted_iota(jnp.int32, logits.shape, 1)
    vals, idxs = [], []
    for _ in range(TOP_K):
        m = jnp.max(logits, axis=-1, keepdims=True)
        ix = jnp.min(jnp.where(logits == m, lane, LANES), axis=-1, keepdims=True)
        vals.append(m)
        idxs.append(ix)
        logits = jnp.where(lane == ix, -jnp.inf, logits)
    es = [jnp.exp(v - vals[0]) for v in vals]
    inv = 1.0 / (es[0] + es[1] + es[2] + es[3])
    idx_out = jnp.zeros(lane.shape, jnp.int32)
    wt_out = jnp.zeros(lane.shape, F32)
    chosen = jnp.zeros(lane.shape, F32)
    for k in range(TOP_K):
        idx_out = jnp.where(lane == k, idxs[k], idx_out)
        wt_out = jnp.where(lane == k, es[k] * inv, wt_out)
        chosen = chosen + jnp.where(lane == idxs[k], 1.0, 0.0)
    idx_ref[...] = idx_out
    wt_ref[...] = wt_out
    hist_ref[0] = jnp.broadcast_to(jnp.sum(chosen, axis=0, keepdims=True), (8, LANES))


def _out_router(ya, yb, yc, x2, w_out, fnw, rw, rb):
    n = x2.shape[0]
    tm = ROW_TILE
    row = lambda w: pl.BlockSpec((tm, w), lambda i: (i, 0))
    full = lambda shape: pl.BlockSpec(shape, lambda i: (0,) * len(shape))
    return pl.pallas_call(
        _out_router_kernel,
        out_shape=[jax.ShapeDtypeStruct((n, D_MODEL), F32), jax.ShapeDtypeStruct((n * SUBLANES, LANES), F32),
                   jax.ShapeDtypeStruct((n, LANES), jnp.int32), jax.ShapeDtypeStruct((n, LANES), F32),
                   jax.ShapeDtypeStruct((n // tm, 8, LANES), F32)],
        grid=(n // tm,),
        in_specs=[row(DN_WIDTH), row(NSA_WIDTH), row(CONV_WIDTH), row(D_MODEL),
                  full((D_MODEL, D_MODEL)), full((1, D_MODEL)), full((D_MODEL, LANES)), full((1, LANES))],
        out_specs=[row(D_MODEL), pl.BlockSpec((tm * SUBLANES, LANES), lambda i: (i, 0)), row(LANES), row(LANES),
                   pl.BlockSpec((1, 8, LANES), lambda i: (i, 0, 0))],
        compiler_params=_params(("parallel",)),
        name="out_router",
    )(ya, yb, yc, x2, w_out, fnw, rw, rb)


def _slots_kernel(idx_ref, base_ref, slot_ref):
    tm = ROW_TILE
    idx = idx_ref[...]
    lane = lax.broadcasted_iota(jnp.int32, (tm, LANES), 1)
    onehots = [jnp.where(lane == _bcast_col(idx, k, LANES), 1.0, 0.0) for k in range(TOP_K)]
    cnt = (onehots[0] + onehots[1]) + (onehots[2] + onehots[3])
    r = lax.broadcasted_iota(jnp.int32, (tm, tm), 0)
    c = lax.broadcasted_iota(jnp.int32, (tm, tm), 1)
    earlier = jnp.where(r > c, 1.0, 0.0).astype(BF16)
    rank = jnp.dot(earlier, cnt.astype(BF16), preferred_element_type=F32) + base_ref[0, 0:1, :]
    ones = jnp.ones((8, LANES), BF16)
    row8 = lax.broadcasted_iota(jnp.int32, (8, tm), 0)
    out = jnp.zeros((8, tm), F32)
    for k in range(TOP_K):
        hi, mid, lo = _split3(rank * onehots[k])
        d = lambda p: lax.dot_general(ones, p, (((1,), (1,)), ((), ())), preferred_element_type=F32)
        out = jnp.where(row8 == k, d(hi) + d(mid) + d(lo), out)
    slot_ref[...] = (out * SUBLANES).astype(jnp.int32)


def _slots(idx, base3):
    n = idx.shape[0]
    tm = ROW_TILE
    return pl.pallas_call(
        _slots_kernel,
        out_shape=jax.ShapeDtypeStruct((8, n), jnp.int32),
        grid=(n // tm,),
        in_specs=[pl.BlockSpec((tm, LANES), lambda i: (i, 0)),
                  pl.BlockSpec((1, 8, LANES), lambda i: (i, 0, 0))],
        out_specs=pl.BlockSpec((8, tm), lambda i: (0, i)),
        compiler_params=_params(("parallel",)),
        name="moe_slots",
    )(idx, base3)


def _dispatch_kernel(slot_ref, zoff_ref, nv_ref, h_ref, xs_hbm, zbuf, sem, zsem):
    i = pl.program_id(0)
    tm = ROW_TILE
    n = slot_ref.shape[0] // TOP_K
    base = i * tm

    @pl.when(i == 0)
    def _():
        zbuf[...] = jnp.zeros(zbuf.shape, F32)
        blk = MOE_TILE * SUBLANES
        n_blocks = xs_hbm.shape[0] // blk

        def fill(row0):
            return pltpu.make_async_copy(
                zbuf, xs_hbm.at[pl.ds(pl.multiple_of(row0 * SUBLANES, blk), blk)], zsem)

        def fill_region_end(e, c):
            fill(zoff_ref[e]).start()
            return c

        def fill_tail(b, c):
            fill(b * MOE_TILE).start()
            return c

        def wait_fill(b, c):
            fill(0).wait()
            return c

        lax.fori_loop(0, N_EXPERTS, fill_region_end, 0)
        lax.fori_loop(nv_ref[0], n_blocks, fill_tail, 0)
        lax.fori_loop(0, N_EXPERTS + n_blocks - nv_ref[0], wait_fill, 0)

    def body(r, c):
        src = h_ref.at[pl.ds(pl.multiple_of(r * SUBLANES, SUBLANES), SUBLANES)]
        for k in range(TOP_K):
            row0 = pl.multiple_of(slot_ref[k * n + base + r], SUBLANES)
            pltpu.make_async_copy(src, xs_hbm.at[pl.ds(row0, SUBLANES)], sem).start(priority=k % 2)
        return c

    lax.fori_loop(0, tm, body, 0, unroll=4)
    done = xs_hbm.at[pl.ds(0, TOP_K * tm * SUBLANES)]
    pltpu.make_async_copy(done, done, sem).wait()


def _dispatch(slot_flat, zero_off, n_valid, h, n_slots):
    n = h.shape[0] // SUBLANES
    tm = ROW_TILE
    return pl.pallas_call(
        _dispatch_kernel,
        out_shape=jax.ShapeDtypeStruct((n_slots * SUBLANES, LANES), F32),
        grid_spec=pltpu.PrefetchScalarGridSpec(
            num_scalar_prefetch=3,
            grid=(n // tm,),
            in_specs=[pl.BlockSpec((tm * SUBLANES, LANES), lambda i, s, z, nv: (i, 0))],
            out_specs=pl.BlockSpec(memory_space=pl.ANY),
            scratch_shapes=[pltpu.VMEM((MOE_TILE * SUBLANES, LANES), F32),
                            pltpu.SemaphoreType.DMA, pltpu.SemaphoreType.DMA],
        ),
        compiler_params=_params(("arbitrary",)),
        name="moe_dispatch",
    )(slot_flat, zero_off, n_valid, h)


def _expert_kernel(be_ref, nv_ref, xs_ref, wgu_ref, bgu_ref, wd_ref, bd_ref, o_ref, wgu_bf, wd_bf):
    i = pl.program_id(0)
    new_expert = (i == 0) | (be_ref[i] != be_ref[jnp.maximum(i - 1, 0)])

    @pl.when((i < nv_ref[0]) & new_expert)
    def _():
        step = 512
        for c0 in range(0, 2 * D_FF, step):
            wgu_bf[:, c0:c0 + step] = wgu_ref[:, c0:c0 + step].astype(BF16)
        for c0 in range(0, D_MODEL, step):
            wd_bf[:, c0:c0 + step] = wd_ref[:, c0:c0 + step].astype(BF16)

    @pl.when(i < nv_ref[0])
    def _():
        xb = _load_tile_rows(xs_ref, MOE_TILE).astype(BF16)
        gu = jnp.dot(xb, wgu_bf[...], preferred_element_type=F32) + bgu_ref[...]
        gate = jnp.minimum(gu[:, :D_FF], SWIGLU_LIMIT)
        up = jnp.clip(gu[:, D_FF:], -SWIGLU_LIMIT, SWIGLU_LIMIT)
        act = (up + 1.0) * gate * _sigmoid(SWIGLU_ALPHA * gate)
        y = jnp.dot(act.astype(BF16), wd_bf[...], preferred_element_type=F32) + bd_ref[...]
        _store_tile_rows(o_ref, y, MOE_TILE)

    @pl.when(i >= nv_ref[0])
    def _():
        o_ref[...] = jnp.zeros(o_ref.shape, F32)


def _expert_ffn(layer, block_expert, n_valid, xs, wgu, bgu, wd, bd):
    n_slots = xs.shape[0] // SUBLANES
    tm = MOE_TILE
    return pl.pallas_call(
        _expert_kernel,
        out_shape=jax.ShapeDtypeStruct(xs.shape, F32),
        grid_spec=pltpu.PrefetchScalarGridSpec(
            num_scalar_prefetch=2,
            grid=(n_slots // tm,),
            in_specs=[
                pl.BlockSpec((tm * SUBLANES, LANES), lambda i, be, nv: (i, 0)),
                pl.BlockSpec((None, None, D_MODEL, 2 * D_FF), lambda i, be, nv: (layer, be[i], 0, 0)),
                pl.BlockSpec((None, None, 1, 2 * D_FF), lambda i, be, nv: (layer, be[i], 0, 0)),
                pl.BlockSpec((None, None, D_FF, D_MODEL), lambda i, be, nv: (layer, be[i], 0, 0)),
                pl.BlockSpec((None, None, 1, D_MODEL), lambda i, be, nv: (layer, be[i], 0, 0)),
            ],
            out_specs=pl.BlockSpec((tm * SUBLANES, LANES), lambda i, be, nv: (i, 0)),
            scratch_shapes=[pltpu.VMEM((D_MODEL, 2 * D_FF), BF16), pltpu.VMEM((D_FF, D_MODEL), BF16)],
        ),
        compiler_params=pltpu.CompilerParams(dimension_semantics=("arbitrary",),
                                             vmem_limit_bytes=EXPERT_VMEM_LIMIT_BYTES),
        name="expert_ffn",
    )(block_expert, n_valid, xs, wgu, bgu, wd, bd)


def _combine_kernel(slot_ref, ys_hbm, x_ref, wt_ref, o_ref, buf, sems):
    i = pl.program_id(0)
    tc = COMBINE_TILE
    nsteps = pl.num_programs(0)
    n = slot_ref.shape[0] // TOP_K
    cur = lax.rem(i, 2)

    def issue(step, par):
        base = step * tc

        def body(r, c):
            dst0 = pl.multiple_of(r * SUBLANES, SUBLANES)
            for k in range(TOP_K):
                row0 = pl.multiple_of(slot_ref[k * n + base + r], SUBLANES)
                pltpu.make_async_copy(ys_hbm.at[pl.ds(row0, SUBLANES)],
                                      buf.at[par, k, pl.ds(dst0, SUBLANES)],
                                      sems.at[par]).start(priority=k % 2)
            return c

        lax.fori_loop(0, tc, body, 0, unroll=4)

    @pl.when(i == 0)
    def _():
        issue(0, 0)

    @pl.when(i + 1 < nsteps)
    def _():
        issue(i + 1, 1 - cur)

    pltpu.make_async_copy(buf.at[cur], buf.at[cur], sems.at[cur]).wait()
    w = [_bcast_col(wt_ref[...], k, LANES) for k in range(TOP_K)]
    for s in range(ROW_CHUNKS):
        cs = slice(s * LANES, (s + 1) * LANES)
        acc = x_ref[:, cs]
        for k in range(TOP_K):
            acc = acc + w[k] * buf[cur, k, pl.ds(s, tc, stride=SUBLANES), :]
        o_ref[:, cs] = acc


def _combine(slot_flat, ys, x2, wt):
    n = x2.shape[0]
    tc = COMBINE_TILE
    return pl.pallas_call(
        _combine_kernel,
        out_shape=jax.ShapeDtypeStruct((n, D_MODEL), F32),
        grid_spec=pltpu.PrefetchScalarGridSpec(
            num_scalar_prefetch=1,
            grid=(n // tc,),
            in_specs=[pl.BlockSpec(memory_space=pl.ANY),
                      pl.BlockSpec((tc, D_MODEL), lambda i, s: (i, 0)),
                      pl.BlockSpec((tc, LANES), lambda i, s: (i, 0))],
            out_specs=pl.BlockSpec((tc, D_MODEL), lambda i, s: (i, 0)),
            scratch_shapes=[pltpu.VMEM((2, TOP_K, tc * SUBLANES, LANES), F32), pltpu.SemaphoreType.DMA((2,))],
        ),
        compiler_params=_params(("arbitrary",)),
        name="moe_combine",
    )(slot_flat, ys, x2, wt)


def _moe(layer, h, x2, idx, wt, hist3, wgu, bgu, wd, bd):
    n = x2.shape[0]
    tm = MOE_TILE
    n_slots = n * TOP_K + N_EXPERTS * tm
    n_blocks = n_slots // tm
    hist = hist3[:, 0, :N_EXPERTS]
    counts = jnp.sum(hist, axis=0).astype(jnp.int32)
    padded = (counts + tm - 1) // tm * tm
    pad_end = jnp.cumsum(padded)
    pad_start = pad_end - padded
    tile_base = pad_start[None, :].astype(F32) + (jnp.cumsum(hist, axis=0) - hist)
    base3 = jnp.broadcast_to(jnp.pad(tile_base, ((0, 0), (0, LANES - N_EXPERTS)))[:, None, :],
                             (hist.shape[0], 8, LANES))
    blk0 = jnp.arange(n_blocks) * tm
    block_expert = jnp.minimum(jnp.sum(blk0[:, None] >= pad_end[None, :], axis=1), N_EXPERTS - 1).astype(jnp.int32)
    n_valid = (pad_end[-1:] // tm).astype(jnp.int32)
    zero_off = jnp.maximum(pad_end - tm, 0).astype(jnp.int32)

    slot_flat = _slots(idx, base3)[:TOP_K].reshape(-1)
    xs = _dispatch(slot_flat, zero_off, n_valid, h, n_slots)
    ys = _expert_ffn(layer, block_expert, n_valid, xs, wgu, bgu, wd, bd)
    return _combine(slot_flat, ys, x2, wt)


def _t5_bucket(dist):
    n = jnp.maximum(dist, 0)
    max_exact = REL_BUCKETS // 2
    nf = jnp.maximum(n, 1).astype(F32)
    large = max_exact + (jnp.log(nf / max_exact) / math.log(REL_MAX_DIST / max_exact)
                         * (REL_BUCKETS - max_exact)).astype(jnp.int32)
    large = jnp.minimum(large, REL_BUCKETS - 1)
    return jnp.where(n < max_exact, n, large)


def _bias_tables(rel_bias):
    rel_bias = rel_bias.astype(F32)
    tq = Q_TILE

    def lookup(dist):
        bucket = _t5_bucket(dist)
        out = jnp.zeros((NSA_HEADS,) + dist.shape, F32)
        for bk in range(REL_BUCKETS):
            out = jnp.where(bucket[None] == bk, rel_bias[bk].reshape((NSA_HEADS,) + (1,) * dist.ndim), out)
        return out

    t_pos = jnp.arange(SEQ)
    cmp_end = jnp.arange(N_CMP_PAD) * CMP_STRIDE + CMP_LEN - 1
    bias_cmp = lookup(t_pos[:, None] - cmp_end[None, :])
    rr = jnp.arange(tq)[:, None] - jnp.arange(tq)[None, :]
    btab = lookup(jnp.stack([rr, rr + tq, rr + 2 * tq]))
    return bias_cmp, btab


def _layer(layer, x2, b, p, experts, bias_cmp, btab):
    n = x2.shape[0]
    w = p['w_in']
    o_a, o_b, o_q, o_kv, o_g, o_u = 2048, 2052, 2056, 2312, 2696, 2708
    w_small = jnp.concatenate([w[:, o_a:o_q], w[:, o_g:o_u],
                               jnp.zeros((D_MODEL, LANES - 8 - 3 * NSA_HEADS), F32)], axis=1)
    w_cat = jnp.concatenate([w[:, :o_a], w[:, o_q:o_kv], w[:, o_kv:o_g], w[:, o_u:], w_small],
                            axis=1).astype(BF16)
    w_abt = jnp.concatenate([w[:, o_a:o_q].T, jnp.zeros((8, D_MODEL), F32)], axis=0).astype(BF16)
    pcol = jnp.zeros((8, LANES), F32).at[0, :DN_HEADS].set(p['dn_a_log']).at[1, :DN_HEADS].set(p['dn_dt_bias'])
    prow = jnp.zeros((16, LANES), F32).at[:DN_HEADS, 0].set(p['dn_a_log']).at[:DN_HEADS, 1].set(p['dn_dt_bias'])
    dq, dk, dv, gb, grow, z, nq, nkv, small, y_c = _front(
        x2, p['attn_norm_w'][None, :], w_cat, w_abt, p['dn_conv_w'], pcol, prow,
        p['conv_dw_w'], p['conv_dw_b'][None, :], p['conv_ln_w'][None, :], p['conv_ln_b'][None, :])
    small3 = small.reshape(b, SEQ, LANES)

    seq = lambda a: a.reshape(b, SEQ, a.shape[-1])
    y_a = _delta_rule(seq(dq), seq(dk), seq(dv), seq(gb), grow, seq(z), p['dn_norm_w'][None, :])

    kv6 = nkv.reshape(6, b, SEQ, NSA_HEAD_DIM)
    kcmp, vcmp, ksn, kwn = _nsa_prep(kv6, p['nsa_cmp_pos'], p['nsa_cmp_w1'].astype(BF16),
                                     p['nsa_cmp_w2'].astype(BF16), p['nsa_k_norm_w'])
    y_b = _nsa_attn(nq, p['nsa_q_norm_w'][None, :], kcmp, vcmp, ksn, kv6, kwn, small3, bias_cmp, btab)

    rw = jnp.concatenate([p['router_w'], jnp.zeros((D_MODEL, LANES - N_EXPERTS), F32)], axis=1)
    rb = jnp.concatenate([p['router_b'], jnp.full((LANES - N_EXPERTS,), NEG_INF, F32)])[None, :]
    x_new, h, idx, wt, hist3 = _out_router(y_a.reshape(n, DN_WIDTH), y_b, y_c.reshape(n, CONV_WIDTH), x2,
                                           p['w_out'].astype(BF16), p['ffn_norm_w'][None, :], rw, rb)
    return _moe(layer, h, x_new, idx, wt, hist3, *experts)


def kernel(x, attn_norm_w, w_in, dn_conv_w, dn_a_log, dn_dt_bias, dn_norm_w, nsa_q_norm_w, nsa_k_norm_w, nsa_cmp_pos, nsa_cmp_w1, nsa_cmp_w2, conv_dw_w, conv_dw_b, conv_ln_w, conv_ln_b, w_out, ffn_norm_w, router_w, router_b, w_gate_up, b_gate_up, w_down, b_down, rel_bias):
    b, t, d = x.shape
    assert (t, d) == (SEQ, D_MODEL)
    stacked = dict(attn_norm_w=attn_norm_w, w_in=w_in, dn_conv_w=dn_conv_w, dn_a_log=dn_a_log,
                   dn_dt_bias=dn_dt_bias, dn_norm_w=dn_norm_w, nsa_q_norm_w=nsa_q_norm_w,
                   nsa_k_norm_w=nsa_k_norm_w, nsa_cmp_pos=nsa_cmp_pos, nsa_cmp_w1=nsa_cmp_w1,
                   nsa_cmp_w2=nsa_cmp_w2, conv_dw_w=conv_dw_w, conv_dw_b=conv_dw_b,
                   conv_ln_w=conv_ln_w, conv_ln_b=conv_ln_b, w_out=w_out, ffn_norm_w=ffn_norm_w,
                   router_w=router_w, router_b=router_b)
    experts = (w_gate_up, b_gate_up[:, :, None, :], w_down, b_down[:, :, None, :])
    bias_cmp, btab = _bias_tables(rel_bias)
    x2 = x.reshape(b * t, d)
    for l in range(w_in.shape[0]):
        x2 = _layer(l, x2, b, {k: v[l] for k, v in stacked.items()}, experts, bias_cmp, btab)
    return x2.reshape(b, t, d)
```

```python
import functools
import math

import numpy as np
import jax
import jax.numpy as jnp
from jax import lax
from jax.experimental import pallas as pl
from jax.experimental.pallas import tpu as pltpu

F32 = jnp.float32
BF16 = jnp.bfloat16

D_MODEL = 1024
SEQ = 2048
DN_HEADS = 4
DN_HEAD_DIM = 128
DN_WIDTH = DN_HEADS * DN_HEAD_DIM
DN_CONV = 4
DN_CHUNK = 64
NSA_HEADS = 4
NSA_HEAD_DIM = 64
NSA_WIDTH = NSA_HEADS * NSA_HEAD_DIM
CMP_LEN = 32
CMP_STRIDE = 16
CMP_HIDDEN = 2 * NSA_HEAD_DIM
SLC_BLOCK = 64
SLC_TOP_N = 16
WINDOW = 512
CONV_WIDTH = 256
CONV_KERNEL = 31
REL_BUCKETS = 32
REL_MAX_DIST = 128
N_EXPERTS = 32
TOP_K = 4
D_FF = D_MODEL
SWIGLU_LIMIT = 7.0
SWIGLU_ALPHA = 1.702
EPS = 1e-6
NEG_INF = -1e30
FORCE = 1e4

LANES = 128
VMEM_LIMIT_BYTES = 48 * 1024 * 1024
EXPERT_VMEM_LIMIT_BYTES = 56 * 1024 * 1024

ROW_TILE = 512
SEQ_TILE = 256
Q_TILE = 128
MOE_TILE = 512
COMBINE_TILE = 256

N_CMP_PAD = 128
N_SLC = SEQ // SLC_BLOCK


def _params(sem=None):
    return pltpu.CompilerParams(dimension_semantics=sem, vmem_limit_bytes=VMEM_LIMIT_BYTES)


def _mm(a, b):
    return jnp.dot(a.astype(BF16), b.astype(BF16), preferred_element_type=F32)


def _mm_nt(a, b):
    return lax.dot_general(a.astype(BF16), b.astype(BF16), (((1,), (1,)), ((), ())),
                           preferred_element_type=F32)


def _mm_tn(a, b):
    return lax.dot_general(a.astype(BF16), b.astype(BF16), (((0,), (0,)), ((), ())),
                           preferred_element_type=F32)


def _split3(x):
    hi = x.astype(BF16)
    r1 = x - hi.astype(F32)
    mid = r1.astype(BF16)
    lo = (r1 - mid.astype(F32)).astype(BF16)
    return hi, mid, lo


def _dot01_right(x, m01):
    hi, mid, lo = _split3(x)
    d = lambda p: jnp.dot(p, m01, preferred_element_type=F32)
    return d(hi) + d(mid) + d(lo)


def _dot01_left(m01, x):
    hi, mid, lo = _split3(x)
    d = lambda p: jnp.dot(m01, p, preferred_element_type=F32)
    return d(hi) + d(mid) + d(lo)


def _hi_lo(x):
    hi = x.astype(BF16)
    return hi, (x - hi.astype(F32)).astype(BF16)


def _sigmoid(x):
    return 1.0 / (1.0 + jnp.exp(-x))


def _silu(x):
    return x * _sigmoid(x)


def _softplus(x):
    return jnp.maximum(x, 0.0) + jnp.log(1.0 + jnp.exp(-jnp.abs(x)))


SUBLANES = 8
ROW_CHUNKS = D_MODEL // LANES
assert ROW_CHUNKS == SUBLANES


def _store_tile_rows(ref, value, rows):
    for s in range(ROW_CHUNKS):
        ref[pl.ds(s, rows, stride=SUBLANES), :] = value[:, s * LANES:(s + 1) * LANES]


def _load_tile_rows(ref, rows):
    return jnp.concatenate([ref[pl.ds(s, rows, stride=SUBLANES), :] for s in range(ROW_CHUNKS)], axis=1)


def _bcast_col(x, j, width):
    return jnp.broadcast_to(x[:, j:j + 1], (x.shape[0], width))


IN_SEGS = (3 * DN_WIDTH, DN_WIDTH, NSA_WIDTH, 6 * NSA_HEAD_DIM, 2 * CONV_WIDTH, LANES)
IN_COLS = sum(IN_SEGS)


def _delta_prepare(q_ref, k_ref, v_ref, gb_ref, grow_ref, wy):
    u_ref, w_ref, attn_ref, qg_ref, kk_ref, gc_ref = wy
    ts = SEQ_TILE
    ch = DN_CHUNK
    hd = DN_HEAD_DIM
    r = lax.broadcasted_iota(jnp.int32, (ts, ts), 0)
    c = lax.broadcasted_iota(jnp.int32, (ts, ts), 1)
    same_chunk = (r // ch) == (c // ch)
    tril = same_chunk & (r >= c)
    strict = same_chunk & (r > c)
    same16 = (r // 16) == (c // 16)
    eye = jnp.where(r == c, 1.0, 0.0).astype(F32)
    m_col = jnp.where(tril, 1.0, 0.0).astype(BF16)
    m_row = jnp.where(same_chunk & (r <= c), 1.0, 0.0).astype(BF16)

    gb = gb_ref[0]
    gc_col = _dot01_left(m_col, gb)
    gc_row = _dot01_right(grow_ref[...], m_row)

    heads = range(DN_HEADS)
    hsl = [slice(h * hd, (h + 1) * hd) for h in heads]
    kh = [k_ref[0, :, hsl[h]] for h in heads]
    gcb = [_bcast_col(gc_col, h, ts) for h in heads]
    decay = [jnp.where(tril, jnp.exp(jnp.where(
        tril, gcb[h] - jnp.broadcast_to(gc_row[h:h + 1, :], (ts, ts)), 0.0)), 0.0) for h in heads]
    beta = [_bcast_col(gb, DN_HEADS + h, hd) for h in heads]
    kb = [kh[h] * beta[h] for h in heads]
    yield
    a_mat = [jnp.where(strict, _mm_nt(kb[h], kh[h]) * decay[h], 0.0) for h in heads]
    yield
    d_mat = [jnp.where(same16, a_mat[h], 0.0) for h in heads]
    e_mat = [a_mat[h] - d_mat[h] for h in heads]
    d2 = [_mm(d_mat[h], d_mat[h]) for h in heads]
    yield
    d4 = [_mm(d2[h], d2[h]) for h in heads]
    yield
    t1 = [_mm(eye - d_mat[h], eye + d2[h]) for h in heads]
    yield
    d8 = [_mm(d4[h], d4[h]) for h in heads]
    yield
    t2 = [_mm(t1[h], eye + d4[h]) for h in heads]
    yield
    p_mat = [_mm(t2[h], eye + d8[h]) for h in heads]
    yield
    m_mat = [_mm(p_mat[h], e_mat[h]) for h in heads]
    yield
    m2 = [_mm(m_mat[h], m_mat[h]) for h in heads]
    yield
    t3 = [_mm(eye - m_mat[h], eye + m2[h]) for h in heads]
    yield
    t_mat = [_mm(t3[h], p_mat[h]) for h in heads]
    yield
    for h in heads:
        gc128 = gcb[h][:, :hd]
        expg = jnp.exp(gc128)
        qh = q_ref[0, :, hsl[h]]
        sol = _mm(t_mat[h], jnp.concatenate([v_ref[0, :, hsl[h]] * beta[h], kb[h] * expg], axis=1))
        u_ref[h] = sol[:, :hd]
        w_ref[h] = sol[:, hd:].astype(BF16)
        qg_ref[h] = (qh * expg).astype(BF16)
        kk_ref[h] = kh[h]
        gc_ref[h] = gc128
    yield
    for h in heads:
        attn_ref[h] = jnp.where(tril, _mm_nt(q_ref[0, :, hsl[h]], kh[h]) * decay[h], 0.0).astype(BF16)


def _delta_recur(wy, z_ref, nw_ref, o_ref, s_ref):
    u_ref, w_ref, attn_ref, qg_ref, kk_ref, gc_ref = wy
    ch = DN_CHUNK
    hd = DN_HEAD_DIM
    heads = range(DN_HEADS)
    state = [s_ref[h] for h in heads]
    outs = [[] for _ in heads]
    for ci in range(SEQ_TILE // ch):
        rs = slice(ci * ch, (ci + 1) * ch)
        v_new = [u_ref[h, rs, :] - _mm(w_ref[h, rs, :], state[h]) for h in heads]
        o_state = [_mm(qg_ref[h, rs, :], state[h]) for h in heads]
        yield
        for h in heads:
            gc = gc_ref[h, rs, :]
            g_last = gc[ch - 1:ch, :]
            k_dec = kk_ref[h, rs, :] * jnp.exp(g_last - gc)
            outs[h].append(o_state[h] + _mm(attn_ref[h, rs, ci * ch:(ci + 1) * ch], v_new[h]))
            state[h] = state[h] * jnp.exp(g_last) + _mm_tn(k_dec, v_new[h])
        yield

    for h in heads:
        s_ref[h] = state[h]
        hs = slice(h * hd, (h + 1) * hd)
        o = jnp.concatenate(outs[h], axis=0)
        o = o * lax.rsqrt(jnp.mean(o * o, axis=-1, keepdims=True) + EPS) * nw_ref[...]
        o_ref[0, :, hs] = o * _silu(z_ref[0, :, hs])


def _interleave(weighted):
    live = [gen for gen, _ in weighted]
    while live:
        for gen, n in weighted:
            for _ in range(n):
                if gen in live and next(gen, live) is live:
                    live.remove(gen)


DN_BATCH = 2


def _delta_kernel(q_ref, k_ref, v_ref, gb_ref, grow0_ref, grow1_ref, z_ref, nw_ref, o_ref, s_ref,
                  *wy_refs):
    t = pl.program_id(1)
    set_a, set_b = wy_refs[:6], wy_refs[6:]
    grow_refs = (grow0_ref, grow1_ref)

    @pl.when(t == 0)
    def _():
        s_ref[...] = jnp.zeros(s_ref.shape, F32)
        for ref in set_b:
            ref[...] = jnp.zeros(ref.shape, ref.dtype)

    def step(read_set, write_set):
        one = lambda ref, bb: ref.at[pl.ds(bb, 1)]
        work = []
        for bb in range(DN_BATCH):
            work.append((_delta_prepare(one(q_ref, bb), one(k_ref, bb), one(v_ref, bb), one(gb_ref, bb),
                                        grow_refs[bb], [ref.at[bb] for ref in write_set]), 2))
        for bb in range(DN_BATCH):
            work.append((_delta_recur([ref.at[bb] for ref in read_set], one(z_ref, bb), nw_ref,
                                      one(o_ref, bb), s_ref.at[bb]), 1))
        _interleave(work)

    @pl.when(lax.rem(t, 2) == 0)
    def _():
        step(set_b, set_a)

    @pl.when(lax.rem(t, 2) == 1)
    def _():
        step(set_a, set_b)


def _delta_rule(q, k, v, gb, grow, z, norm_w):
    b = q.shape[0]
    nb = DN_BATCH
    assert b % nb == 0
    ts = SEQ_TILE
    nt = SEQ // ts
    nxt = lambda w: pl.BlockSpec((nb, ts, w), lambda i, t: (i, jnp.minimum(t, nt - 1), 0))
    cur = lambda w: pl.BlockSpec((nb, ts, w), lambda i, t: (i, jnp.maximum(t - 1, 0), 0))
    grow_spec = lambda bb: pl.BlockSpec(
        (8, ts), lambda i, t: (0, (i * nb + bb) * nt + jnp.minimum(t, nt - 1)))
    hd = DN_HEAD_DIM
    wy_set = [pltpu.VMEM((nb, DN_HEADS, ts, hd), F32), pltpu.VMEM((nb, DN_HEADS, ts, hd), BF16),
              pltpu.VMEM((nb, DN_HEADS, ts, ts), BF16), pltpu.VMEM((nb, DN_HEADS, ts, hd), BF16),
              pltpu.VMEM((nb, DN_HEADS, ts, hd), F32), pltpu.VMEM((nb, DN_HEADS, ts, hd), F32)]
    return pl.pallas_call(
        _delta_kernel,
        out_shape=jax.ShapeDtypeStruct((b, SEQ, DN_WIDTH), F32),
        grid=(b // nb, nt + 1),
        in_specs=[
            nxt(DN_WIDTH), nxt(DN_WIDTH), nxt(DN_WIDTH), nxt(LANES),
            grow_spec(0), grow_spec(1),
            cur(DN_WIDTH),
            pl.BlockSpec((1, DN_HEAD_DIM), lambda i, t: (0, 0)),
        ],
        out_specs=cur(DN_WIDTH),
        scratch_shapes=[pltpu.VMEM((nb, DN_HEADS, hd, hd), F32)] + wy_set + wy_set,
        compiler_params=_params(("parallel", "arbitrary")),
        name="delta_rule",
    )(q, k, v, gb, grow, grow, z, norm_w)


CONV_HALO = 32


MIX_COLS = {'qkv': (0, 3 * DN_WIDTH), 'z': (3 * DN_WIDTH, 4 * DN_WIDTH)}
_off = 4 * DN_WIDTH
for _name, _w in (('nq', NSA_WIDTH), ('nkv', 6 * NSA_HEAD_DIM), ('cu', 2 * CONV_WIDTH), ('small', LANES)):
    MIX_COLS[_name] = (_off, _off + _w)
    _off += _w
assert _off == IN_COLS
DN_HALO = 8
MXU_CHUNK = 256


def _front_kernel(x_ref, nw_ref, w_ref, wabt_ref, cw_ref, pcol_ref, prow_ref,
                  dww_ref, dwb_ref, lnw_ref, lnb_ref,
                  q_ref, k_ref, v_ref, gb_ref, grow_ref, z_ref, nq_ref, nkv_ref, small_ref, yc_ref,
                  dn_buf, cf_buf, cf_shift):
    i = pl.program_id(0)
    tm = ROW_TILE
    hd = NSA_HEAD_DIM

    @pl.when(lax.rem(i, SEQ // tm) == 0)
    def _():
        dn_buf[0:DN_HALO, :] = jnp.zeros((DN_HALO, 3 * DN_WIDTH), F32)
        cf_buf[0:CONV_HALO, :] = jnp.zeros((CONV_HALO, CONV_WIDTH), F32)

    xf = x_ref[...]
    hb = (xf * lax.rsqrt(jnp.mean(xf * xf, axis=-1, keepdims=True) + EPS) * nw_ref[...]).astype(BF16)

    def proj(name, lo=0, hi=None):
        c0, c1 = MIX_COLS[name]
        hi = c1 - c0 if hi is None else hi
        return jnp.dot(hb, w_ref[:, c0 + lo:c0 + hi], preferred_element_type=F32)

    def mxu_items():
        cu = proj('cu')
        cf_buf[CONV_HALO:CONV_HALO + tm, :] = cu[:, :CONV_WIDTH] * _sigmoid(cu[:, CONV_WIDTH:])
        yield
        sm = proj('small')
        small_ref[...] = sm
        lane = lax.broadcasted_iota(jnp.int32, sm.shape, 1)
        g_col = -jnp.exp(pcol_ref[0:1, :]) * _softplus(sm + pcol_ref[1:2, :])
        gb_ref[...] = jnp.where(lane < DN_HEADS, g_col, _sigmoid(sm))
        a_t = lax.dot_general(wabt_ref[...], hb, (((1,), (1,)), ((), ())), preferred_element_type=F32)
        g_row = -jnp.exp(prow_ref[:, 0:1]) * _softplus(a_t + prow_ref[:, 1:2])
        grow_ref[...] = g_row[0:8, :]
        yield
        for c0 in range(0, 3 * DN_WIDTH, MXU_CHUNK):
            dn_buf[DN_HALO:DN_HALO + tm, c0:c0 + MXU_CHUNK] = proj('qkv', c0, c0 + MXU_CHUNK)
            yield
        for c0 in range(0, DN_WIDTH, MXU_CHUNK):
            z_ref[:, c0:c0 + MXU_CHUNK] = proj('z', c0, c0 + MXU_CHUNK)
            yield
        res = proj('nq')
        for j in range(NSA_HEADS):
            nq_ref[j] = res[:, j * hd:(j + 1) * hd]
        yield
        res = proj('nkv')
        for j in range(6):
            nkv_ref[j] = res[:, j * hd:(j + 1) * hd]

    def conformer_items():
        span = tm + CONV_HALO - SUBLANES
        for b in range(1, SUBLANES):
            cf_shift[b] = cf_buf[b:b + span, :]
        yield
        base = CONV_HALO - (CONV_KERNEL - 1)
        rows = 64
        for rc in range(tm // rows):
            parts = []
            for cc in range(CONV_WIDTH // LANES):
                cs = slice(cc * LANES, (cc + 1) * LANES)
                acc = None
                for j in range(CONV_KERNEL):
                    start = base + rc * rows + j
                    b = start % SUBLANES
                    a0 = start - b
                    win = cf_buf[a0:a0 + rows, cs] if b == 0 else cf_shift[b, a0:a0 + rows, cs]
                    term = dww_ref[j:j + 1, cs] * win
                    acc = term if acc is None else acc + term
                parts.append(acc)
            h = jnp.concatenate(parts, axis=1) + dwb_ref[...]
            mu = jnp.mean(h, axis=-1, keepdims=True)
            var = jnp.mean(jnp.square(h - mu), axis=-1, keepdims=True)
            hn = (h - mu) * lax.rsqrt(var + EPS) * lnw_ref[...] + lnb_ref[...]
            yc_ref[rc * rows:(rc + 1) * rows, :] = _silu(hn)
            yield
        cf_buf[0:CONV_HALO, :] = cf_buf[tm:tm + CONV_HALO, :]

    def deltanet_items():
        rows = tm // 2
        for c in range(3 * DN_HEADS):
            cs = slice(c * LANES, (c + 1) * LANES)
            for r0 in range(0, tm, rows):
                lo = DN_HALO - (DN_CONV - 1) + r0
                acc = cw_ref[0:1, cs] * dn_buf[lo:lo + rows, cs]
                for j in range(1, DN_CONV):
                    acc = acc + cw_ref[j:j + 1, cs] * dn_buf[lo + j:lo + j + rows, cs]
                y = _silu(acc)
                if c < 2 * DN_HEADS:
                    y = y * lax.rsqrt(jnp.sum(y * y, axis=-1, keepdims=True) + EPS)
                rs = slice(r0, r0 + rows)
                if c < DN_HEADS:
                    q_ref[rs, cs] = y * DN_HEAD_DIM ** -0.5
                elif c < 2 * DN_HEADS:
                    k_ref[rs, (c - DN_HEADS) * LANES:(c - DN_HEADS + 1) * LANES] = y
                else:
                    v_ref[rs, (c - 2 * DN_HEADS) * LANES:(c - 2 * DN_HEADS + 1) * LANES] = y
            dn_buf[0:DN_HALO, cs] = dn_buf[tm:tm + DN_HALO, cs]
            yield

    mxu, cf, dn = mxu_items(), conformer_items(), deltanet_items()
    next(mxu)
    next(mxu)
    next(cf)
    next(mxu)
    _interleave([(mxu, 1), (dn, 2), (cf, 2)])


def _front(x2, norm_w, w_cat, w_abt, conv_w, pcol, prow, dw_w, dw_b, ln_w, ln_b):
    n = x2.shape[0]
    tm = ROW_TILE
    hd = NSA_HEAD_DIM
    row = lambda w: pl.BlockSpec((tm, w), lambda i: (i, 0))
    full = lambda shape: pl.BlockSpec(shape, lambda i: (0,) * len(shape))
    pieces = lambda k: pl.BlockSpec((k, tm, hd), lambda i: (0, i, 0))
    return pl.pallas_call(
        _front_kernel,
        out_shape=[jax.ShapeDtypeStruct((n, DN_WIDTH), F32)] * 3
        + [jax.ShapeDtypeStruct((n, LANES), F32), jax.ShapeDtypeStruct((8, n), F32),
           jax.ShapeDtypeStruct((n, DN_WIDTH), F32),
           jax.ShapeDtypeStruct((NSA_HEADS, n, hd), F32), jax.ShapeDtypeStruct((6, n, hd), F32),
           jax.ShapeDtypeStruct((n, LANES), F32), jax.ShapeDtypeStruct((n, CONV_WIDTH), F32)],
        grid=(n // tm,),
        in_specs=[row(D_MODEL), full((1, D_MODEL)), full((D_MODEL, IN_COLS)), full((16, D_MODEL)),
                  full((DN_CONV, 3 * DN_WIDTH)), full((8, LANES)), full((16, LANES)),
                  full((CONV_KERNEL, CONV_WIDTH)), full((1, CONV_WIDTH)), full((1, CONV_WIDTH)),
                  full((1, CONV_WIDTH))],
        out_specs=[row(DN_WIDTH)] * 3
        + [row(LANES), pl.BlockSpec((8, tm), lambda i: (0, i)), row(DN_WIDTH),
           pieces(NSA_HEADS), pieces(6), row(LANES), row(CONV_WIDTH)],
        scratch_shapes=[pltpu.VMEM((tm + DN_HALO, 3 * DN_WIDTH), F32),
                        pltpu.VMEM((tm + CONV_HALO, CONV_WIDTH), F32),
                        pltpu.VMEM((SUBLANES, tm + CONV_HALO - SUBLANES, CONV_WIDTH), F32)],
        compiler_params=_params(("arbitrary",)),
        name="front",
    )(x2, norm_w, w_cat, w_abt, conv_w, pcol, prow, dw_w, dw_b, ln_w, ln_b)


def _rms_rows(x, w):
    return x * lax.rsqrt(jnp.mean(x * x, axis=-1, keepdims=True) + EPS) * w


def _nsa_prep_kernel(kc_ref, vc_ref, ks_ref, kw_ref, pos_ref, w1_ref, w2_ref, knw_ref,
                     kcmp_ref, vcmp_ref, ksn_ref, kwn_ref):
    hd = NSA_HEAD_DIM

    def compress(x_ref, i):
        u_lo = jnp.zeros((N_CMP_PAD, CMP_HIDDEN), F32)
        u_hi = jnp.zeros((N_CMP_PAD, CMP_HIDDEN), F32)
        for r in range(CMP_STRIDE):
            xr = x_ref[pl.ds(r, N_CMP_PAD, stride=CMP_STRIDE), :]
            lo, hi = r, CMP_STRIDE + r
            u_lo = u_lo + _mm(xr + pos_ref[i, lo:lo + 1, :], w1_ref[i, lo * hd:(lo + 1) * hd, :])
            u_hi = u_hi + _mm(xr + pos_ref[i, hi:hi + 1, :], w1_ref[i, hi * hd:(hi + 1) * hd, :])
        hid = _silu(u_lo + pltpu.roll(u_hi, N_CMP_PAD - 1, axis=0))
        return _mm(hid, w2_ref[i])

    kcmp_ref[0] = _rms_rows(compress(kc_ref, 0), knw_ref[0:1, :])
    vcmp_ref[0] = compress(vc_ref, 1)
    ksn_ref[0] = _rms_rows(ks_ref[...], knw_ref[1:2, :])
    kwn_ref[0] = _rms_rows(kw_ref[...], knw_ref[2:3, :])


def _nsa_prep(kv6, pos, w1, w2, knw):
    b = kv6.shape[1]
    hd = NSA_HEAD_DIM
    full = lambda shape: pl.BlockSpec(shape, lambda i: (0,) * len(shape))
    piece = lambda j: pl.BlockSpec((None, None, SEQ, hd), lambda i: (j, i, 0, 0))
    bspec = lambda r, w: pl.BlockSpec((1, r, w), lambda i: (i, 0, 0))
    return pl.pallas_call(
        _nsa_prep_kernel,
        out_shape=[jax.ShapeDtypeStruct((b, N_CMP_PAD, hd), F32)] * 2
        + [jax.ShapeDtypeStruct((b, SEQ, hd), F32)] * 2,
        grid=(b,),
        in_specs=[
            piece(0), piece(1), piece(2), piece(4),
            full((2, CMP_LEN, hd)), full((2, CMP_LEN * hd, CMP_HIDDEN)), full((2, CMP_HIDDEN, hd)),
            full((3, hd)),
        ],
        out_specs=[bspec(N_CMP_PAD, hd)] * 2 + [bspec(SEQ, hd)] * 2,
        compiler_params=_params(("parallel",)),
        name="nsa_prep",
    )(kv6, kv6, kv6, kv6, pos, w1, w2, knw)


def _nsa_attn_kernel(q_ref, qnw_ref, kcmp_ref, vcmp_ref, ks_ref, vs_ref, kw_ref, vw_ref,
                     gate_ref, bcmp_ref, btab_ref, o_ref,
                     madd_ref, s_ref, ksb_ref, kwb_ref, vsa_ref, vwa_ref):
    i = pl.program_id(1)
    tq = Q_TILE
    nh = NSA_HEADS
    hd = NSA_HEAD_DIM
    t0 = i * tq

    @pl.when(i == 0)
    def _():
        ones = jnp.ones((SEQ, hd), BF16)
        ksb_ref[...] = ks_ref[0].astype(BF16)
        kwb_ref[...] = kw_ref[0].astype(BF16)
        vsa_ref[...] = jnp.concatenate([vs_ref[...].astype(BF16), ones], axis=1)
        vwa_ref[...] = jnp.concatenate([vw_ref[...].astype(BF16), ones], axis=1)

    qs = jnp.concatenate(
        [_rms_rows(q_ref[h], qnw_ref[...]) * hd ** -0.5 for h in range(nh)], axis=0).astype(BF16)

    row = lax.broadcasted_iota(jnp.int32, (tq, LANES), 0)
    lane = lax.broadcasted_iota(jnp.int32, (tq, LANES), 1)
    qpos = t0 + row

    s_all = _mm_nt(qs, kcmp_ref[0])
    cmp_valid = (qpos >= lane * CMP_STRIDE + (CMP_LEN - 1)) & (lane < N_CMP_PAD - 1)
    p_rows = []
    p_sum = jnp.zeros((tq, LANES), F32)
    for h in range(nh):
        s = jnp.where(cmp_valid, s_all[h * tq:(h + 1) * tq] + bcmp_ref[h], NEG_INF)
        m = jnp.max(s, axis=-1, keepdims=True)
        p = jnp.where(cmp_valid, jnp.exp(s - m), 0.0)
        l = jnp.sum(p, axis=-1, keepdims=True)
        p = p * jnp.where(l > 0.0, 1.0 / l, 0.0)
        p_rows.append(p)
        p_sum = p_sum + p
    o_cmp = _mm(jnp.concatenate(p_rows, axis=0), vcmp_ref[0])

    ss = lax.broadcasted_iota(jnp.int32, (N_SLC, N_CMP_PAD), 0)
    jj = lax.broadcasted_iota(jnp.int32, (N_SLC, N_CMP_PAD), 1)
    overlap_t = ((jj * CMP_STRIDE < ss * SLC_BLOCK + SLC_BLOCK)
                 & (jj * CMP_STRIDE + CMP_LEN > ss * SLC_BLOCK) & (jj < N_CMP_PAD - 1))
    overlap_t = jnp.where(overlap_t, 1.0, 0.0).astype(BF16)
    nt = lambda a, b_: lax.dot_general(a, b_, (((1,), (1,)), ((), ())), preferred_element_type=F32)
    p_hi, p_mid, p_lo = _split3(p_sum)
    imp = nt(overlap_t, p_hi) + nt(overlap_t, p_mid) + nt(overlap_t, p_lo)
    blk = lax.broadcasted_iota(jnp.int32, (N_SLC, tq), 0)
    cur = (t0 + lax.broadcasted_iota(jnp.int32, (N_SLC, tq), 1)) // SLC_BLOCK
    causal_blk = blk <= cur
    forced = (blk == 0) | (blk == cur) | (blk == cur - 1)
    imp = jnp.where(causal_blk & forced, FORCE, jnp.where(causal_blk, imp, -1.0))
    rank = jnp.zeros((N_SLC, tq), jnp.int32)
    for s2 in range(N_SLC):
        other = jnp.broadcast_to(imp[s2:s2 + 1, :], (N_SLC, tq))
        beats = (other > imp) | ((other == imp) & (blk > s2))
        rank = rank + jnp.where(beats, 1, 0)
    sel_t = jnp.where(rank < SLC_TOP_N, 1.0, 0.0).astype(BF16)
    er = lax.broadcasted_iota(jnp.int32, (N_SLC, SEQ), 0)
    ec = lax.broadcasted_iota(jnp.int32, (N_SLC, SEQ), 1)
    expand = jnp.where(ec // SLC_BLOCK == er, 1.0, 0.0).astype(BF16)
    key = lax.broadcasted_iota(jnp.int32, (tq, SEQ), 1)
    qall = t0 + lax.broadcasted_iota(jnp.int32, (tq, SEQ), 0)
    chosen = (_mm_tn(sel_t, expand) > 0.5) & (qall >= key)
    madd_ref[...] = jnp.where(chosen, 0.0, NEG_INF)

    neg = jnp.full((nh * tq, tq), NEG_INF, F32)
    zero = jnp.zeros((nh * tq, LANES), F32)

    def scores(k_tile, bias_idx, add):
        s_t = _mm_nt(qs, k_tile)
        return jnp.concatenate(
            [s_t[h * tq:(h + 1) * tq] + (btab_ref[h, bias_idx] + add) for h in range(nh)], axis=0)

    def normalise(acc):
        return acc[:, :hd] * (1.0 / _bcast_col(acc, hd, hd))

    n_pairs = (i + 2) // 2

    def slc_pass1(j, m_run):
        for u in range(2):
            kt = 2 * j + u
            k0 = pl.multiple_of(kt * tq, tq)
            s_m = scores(ksb_ref[pl.ds(k0, tq), :], jnp.clip(i - kt, 0, 2), madd_ref[:, pl.ds(k0, tq)])
            s_ref[:, pl.ds(k0, tq)] = s_m
            m_run = jnp.maximum(m_run, s_m)
        return m_run

    m_slc = jnp.broadcast_to(
        jnp.max(lax.fori_loop(0, n_pairs, slc_pass1, neg), axis=-1, keepdims=True), (nh * tq, tq))

    def slc_pass2(j, acc):
        for u in range(2):
            k0 = pl.multiple_of((2 * j + u) * tq, tq)
            p = jnp.exp(s_ref[:, pl.ds(k0, tq)] - m_slc).astype(BF16)
            acc = acc + jnp.dot(p, vsa_ref[pl.ds(k0, tq), :], preferred_element_type=F32)
        return acc

    o_slc = normalise(lax.fori_loop(0, n_pairs, slc_pass2, zero))

    kcol = lax.broadcasted_iota(jnp.int32, (tq, tq), 1)
    qrow = t0 + lax.broadcasted_iota(jnp.int32, (tq, tq), 0)
    n_win = WINDOW // tq + 1
    starts = []
    m_run = neg
    for j in range(n_win):
        d = n_win - 1 - j
        k0 = pl.multiple_of(jnp.maximum(i - d, 0) * tq, tq)
        starts.append(k0)
        dist = qrow - (k0 + kcol)
        ok = (dist >= 0) & (dist < WINDOW) & (i >= d)
        s_m = scores(kwb_ref[pl.ds(k0, tq), :], min(d, 2), jnp.where(ok, 0.0, NEG_INF))
        s_ref[:, j * tq:(j + 1) * tq] = s_m
        m_run = jnp.maximum(m_run, s_m)
    m_win = jnp.broadcast_to(jnp.max(m_run, axis=-1, keepdims=True), (nh * tq, tq))
    acc = zero
    for j in range(n_win):
        p = jnp.exp(s_ref[:, j * tq:(j + 1) * tq] - m_win).astype(BF16)
        acc = acc + jnp.dot(p, vwa_ref[pl.ds(starts[j], tq), :], preferred_element_type=F32)
    o_win = normalise(acc)

    gates = _sigmoid(gate_ref[0])
    outs = []
    for h in range(nh):
        hs = slice(h * tq, (h + 1) * tq)
        g0 = 2 * DN_HEADS + 3 * h
        outs.append(_bcast_col(gates, g0, hd) * o_cmp[hs]
                    + _bcast_col(gates, g0 + 1, hd) * o_slc[hs]
                    + _bcast_col(gates, g0 + 2, hd) * o_win[hs])
    o_ref[...] = jnp.concatenate(outs, axis=1)


def _nsa_attn(q, qnw, kcmp, vcmp, ks, kv6, kw, small, bias_cmp, btab):
    b = kv6.shape[1]
    tq = Q_TILE
    nq = SEQ // tq
    nh, hd = NSA_HEADS, NSA_HEAD_DIM
    full = lambda shape: pl.BlockSpec(shape, lambda bi, i: (0,) * len(shape))
    per_b = lambda r: pl.BlockSpec((1, r, hd), lambda bi, i: (bi, 0, 0))
    piece = lambda j: pl.BlockSpec((None, None, SEQ, hd), lambda bi, i: (j, bi, 0, 0))
    heads = pl.BlockSpec((nh, tq, hd), lambda bi, i: (0, bi * nq + i, 0))
    return pl.pallas_call(
        _nsa_attn_kernel,
        out_shape=jax.ShapeDtypeStruct((b * SEQ, NSA_WIDTH), F32),
        grid=(b, nq),
        in_specs=[
            heads,
            full((1, hd)),
            per_b(N_CMP_PAD), per_b(N_CMP_PAD), per_b(SEQ), piece(3), per_b(SEQ), piece(5),
            pl.BlockSpec((1, tq, LANES), lambda bi, i: (bi, i, 0)),
            pl.BlockSpec((nh, tq, N_CMP_PAD), lambda bi, i: (0, i, 0)),
            full((nh, 3, tq, tq)),
        ],
        out_specs=pl.BlockSpec((tq, NSA_WIDTH), lambda bi, i: (bi * nq + i, 0)),
        scratch_shapes=[pltpu.VMEM((tq, SEQ), F32), pltpu.VMEM((nh * tq, SEQ), F32),
                        pltpu.VMEM((SEQ, hd), BF16), pltpu.VMEM((SEQ, hd), BF16),
                        pltpu.VMEM((SEQ, 2 * hd), BF16), pltpu.VMEM((SEQ, 2 * hd), BF16)],
        compiler_params=_params(("parallel", "arbitrary")),
        name="nsa_attn",
    )(q, qnw, kcmp, vcmp, ks, kv6, kw, kv6, small, bias_cmp, btab)


def _out_router_kernel(ya_ref, yb_ref, yc_ref, x_ref, wo_ref, fnw_ref, rw_ref, rb_ref,
                       xo_ref, h_ref, idx_ref, wt_ref, hist_ref):
    wa = DN_WIDTH
    wb = wa + NSA_WIDTH
    y = (jnp.dot(ya_ref[...].astype(BF16), wo_ref[0:wa, :], preferred_element_type=F32)
         + jnp.dot(yb_ref[...].astype(BF16), wo_ref[wa:wb, :], preferred_element_type=F32)
         + jnp.dot(yc_ref[...].astype(BF16), wo_ref[wb:, :], preferred_element_type=F32))
    xn = x_ref[...] + y
    xo_ref[...] = xn
    h = xn * lax.rsqrt(jnp.mean(xn * xn, axis=-1, keepdims=True) + EPS) * fnw_ref[...]
    _store_tile_rows(h_ref, h, h.shape[0])
    h_hi, h_lo = _hi_lo(h)
    r_hi = jnp.dot(h_hi, rw_ref[...], preferred_element_type=F32)
    r_lo = jnp.dot(h_lo, rw_ref[...], preferred_element_type=F32)
    logits = ((r_hi[:, :LANES] + r_hi[:, LANES:]) + (r_lo[:, :LANES] + r_lo[:, LANES:])) + rb_ref[...]
    lane = lax.broadcasted_iota(jnp.int32, logits.shape, 1)
    vals, idxs = [], []
    for _ in range(TOP_K):
        m = jnp.max(logits, axis=-1, keepdims=True)
        ix = jnp.min(jnp.where(logits == m, lane, LANES), axis=-1, keepdims=True)
        vals.append(m)
        idxs.append(ix)
        logits = jnp.where(lane == ix, -jnp.inf, logits)
    es = [jnp.exp(v - vals[0]) for v in vals]
    inv = 1.0 / (es[0] + es[1] + es[2] + es[3])
    idx_out = jnp.zeros(lane.shape, jnp.int32)
    wt_out = jnp.zeros(lane.shape, F32)
    chosen = jnp.zeros(lane.shape, F32)
    for k in range(TOP_K):
        idx_out = jnp.where(lane == k, idxs[k], idx_out)
        wt_out = jnp.where(lane == k, es[k] * inv, wt_out)
        chosen = chosen + jnp.where(lane == idxs[k], 1.0, 0.0)
    idx_ref[...] = idx_out
    wt_ref[...] = wt_out
    hist_ref[0] = jnp.broadcast_to(jnp.sum(chosen, axis=0, keepdims=True), (8, LANES))


def _out_router(ya, yb, yc, x2, w_out, fnw, rw, rb):
    n = x2.shape[0]
    tm = ROW_TILE
    row = lambda w: pl.BlockSpec((tm, w), lambda i: (i, 0))
    full = lambda shape: pl.BlockSpec(shape, lambda i: (0,) * len(shape))
    return pl.pallas_call(
        _out_router_kernel,
        out_shape=[jax.ShapeDtypeStruct((n, D_MODEL), F32), jax.ShapeDtypeStruct((n * SUBLANES, LANES), F32),
                   jax.ShapeDtypeStruct((n, LANES), jnp.int32), jax.ShapeDtypeStruct((n, LANES), F32),
                   jax.ShapeDtypeStruct((n // tm, 8, LANES), F32)],
        grid=(n // tm,),
        in_specs=[row(DN_WIDTH), row(NSA_WIDTH), row(CONV_WIDTH), row(D_MODEL),
                  full((D_MODEL, D_MODEL)), full((1, D_MODEL)), full((D_MODEL, 2 * LANES)), full((1, LANES))],
        out_specs=[row(D_MODEL), pl.BlockSpec((tm * SUBLANES, LANES), lambda i: (i, 0)), row(LANES), row(LANES),
                   pl.BlockSpec((1, 8, LANES), lambda i: (i, 0, 0))],
        compiler_params=_params(("parallel",)),
        name="out_router",
    )(ya, yb, yc, x2, w_out, fnw, rw, rb)


def _slots_kernel(idx_ref, base_ref, slot_ref):
    tm = ROW_TILE
    idx = idx_ref[...]
    lane = lax.broadcasted_iota(jnp.int32, (tm, LANES), 1)
    onehots = [jnp.where(lane == _bcast_col(idx, k, LANES), 1.0, 0.0) for k in range(TOP_K)]
    cnt = (onehots[0] + onehots[1]) + (onehots[2] + onehots[3])
    r = lax.broadcasted_iota(jnp.int32, (tm, tm), 0)
    c = lax.broadcasted_iota(jnp.int32, (tm, tm), 1)
    earlier = jnp.where(r > c, 1.0, 0.0).astype(BF16)
    rank = jnp.dot(earlier, cnt.astype(BF16), preferred_element_type=F32) + base_ref[0, 0:1, :]
    ones = jnp.ones((8, LANES), BF16)
    row8 = lax.broadcasted_iota(jnp.int32, (8, tm), 0)
    out = jnp.zeros((8, tm), F32)
    for k in range(TOP_K):
        hi, mid, lo = _split3(rank * onehots[k])
        d = lambda p: lax.dot_general(ones, p, (((1,), (1,)), ((), ())), preferred_element_type=F32)
        out = jnp.where(row8 == k, d(hi) + d(mid) + d(lo), out)
    slot_ref[...] = (out * SUBLANES).astype(jnp.int32)


def _slots(idx, base3):
    n = idx.shape[0]
    tm = ROW_TILE
    return pl.pallas_call(
        _slots_kernel,
        out_shape=jax.ShapeDtypeStruct((8, n), jnp.int32),
        grid=(n // tm,),
        in_specs=[pl.BlockSpec((tm, LANES), lambda i: (i, 0)),
                  pl.BlockSpec((1, 8, LANES), lambda i: (i, 0, 0))],
        out_specs=pl.BlockSpec((8, tm), lambda i: (0, i)),
        compiler_params=_params(("parallel",)),
        name="moe_slots",
    )(idx, base3)


def _dispatch_kernel(slot_ref, zoff_ref, nv_ref, h_ref, xs_hbm, zbuf, sem, zsem):
    i = pl.program_id(0)
    tm = ROW_TILE
    n = slot_ref.shape[0] // TOP_K
    base = i * tm

    @pl.when(i == 0)
    def _():
        zbuf[...] = jnp.zeros(zbuf.shape, F32)
        blk = MOE_TILE * SUBLANES
        n_blocks = xs_hbm.shape[0] // blk

        def fill(row0):
            return pltpu.make_async_copy(
                zbuf, xs_hbm.at[pl.ds(pl.multiple_of(row0 * SUBLANES, blk), blk)], zsem)

        def fill_region_end(e, c):
            fill(zoff_ref[e]).start()
            return c

        def fill_tail(b, c):
            fill(b * MOE_TILE).start()
            return c

        def wait_fill(b, c):
            fill(0).wait()
            return c

        lax.fori_loop(0, N_EXPERTS, fill_region_end, 0)
        lax.fori_loop(nv_ref[0], n_blocks, fill_tail, 0)
        lax.fori_loop(0, N_EXPERTS + n_blocks - nv_ref[0], wait_fill, 0)

    def body(r, c):
        src = h_ref.at[pl.ds(pl.multiple_of(r * SUBLANES, SUBLANES), SUBLANES)]
        for k in range(TOP_K):
            row0 = pl.multiple_of(slot_ref[k * n + base + r], SUBLANES)
            pltpu.make_async_copy(src, xs_hbm.at[pl.ds(row0, SUBLANES)], sem).start(priority=k % 2)
        return c

    lax.fori_loop(0, tm, body, 0, unroll=8)
    done = xs_hbm.at[pl.ds(0, TOP_K * tm * SUBLANES)]
    pltpu.make_async_copy(done, done, sem).wait()


def _dispatch(slot_flat, zero_off, n_valid, h, n_slots):
    n = h.shape[0] // SUBLANES
    tm = ROW_TILE
    return pl.pallas_call(
        _dispatch_kernel,
        out_shape=jax.ShapeDtypeStruct((n_slots * SUBLANES, LANES), F32),
        grid_spec=pltpu.PrefetchScalarGridSpec(
            num_scalar_prefetch=3,
            grid=(n // tm,),
            in_specs=[pl.BlockSpec((tm * SUBLANES, LANES), lambda i, s, z, nv: (i, 0))],
            out_specs=pl.BlockSpec(memory_space=pl.ANY),
            scratch_shapes=[pltpu.VMEM((MOE_TILE * SUBLANES, LANES), F32),
                            pltpu.SemaphoreType.DMA, pltpu.SemaphoreType.DMA],
        ),
        compiler_params=_params(("arbitrary",)),
        name="moe_dispatch",
    )(slot_flat, zero_off, n_valid, h)


def _expert_kernel(be_ref, nv_ref, xs_ref, wgu_ref, bgu_ref, wd_ref, bd_ref, o_ref, wgu_bf, wd_bf):
    i = pl.program_id(0)
    new_expert = (i == 0) | (be_ref[i] != be_ref[jnp.maximum(i - 1, 0)])

    @pl.when((i < nv_ref[0]) & new_expert)
    def _():
        step = 512
        for c0 in range(0, 2 * D_FF, step):
            wgu_bf[:, c0:c0 + step] = wgu_ref[:, c0:c0 + step].astype(BF16)
        for c0 in range(0, D_MODEL, step):
            wd_bf[:, c0:c0 + step] = wd_ref[:, c0:c0 + step].astype(BF16)

    @pl.when(i < nv_ref[0])
    def _():
        xb = _load_tile_rows(xs_ref, MOE_TILE).astype(BF16)
        gu = jnp.dot(xb, wgu_bf[...], preferred_element_type=F32) + bgu_ref[...]
        gate = jnp.minimum(gu[:, :D_FF], SWIGLU_LIMIT)
        up = jnp.clip(gu[:, D_FF:], -SWIGLU_LIMIT, SWIGLU_LIMIT)
        act = (up + 1.0) * gate * _sigmoid(SWIGLU_ALPHA * gate)
        y = jnp.dot(act.astype(BF16), wd_bf[...], preferred_element_type=F32) + bd_ref[...]
        _store_tile_rows(o_ref, y, MOE_TILE)

    @pl.when(i >= nv_ref[0])
    def _():
        o_ref[...] = jnp.zeros(o_ref.shape, F32)


def _expert_ffn(layer, block_expert, n_valid, xs, wgu, bgu, wd, bd):
    n_slots = xs.shape[0] // SUBLANES
    tm = MOE_TILE
    return pl.pallas_call(
        _expert_kernel,
        out_shape=jax.ShapeDtypeStruct(xs.shape, F32),
        grid_spec=pltpu.PrefetchScalarGridSpec(
            num_scalar_prefetch=2,
            grid=(n_slots // tm,),
            in_specs=[
                pl.BlockSpec((tm * SUBLANES, LANES), lambda i, be, nv: (i, 0)),
                pl.BlockSpec((None, None, D_MODEL, 2 * D_FF), lambda i, be, nv: (layer, be[i], 0, 0)),
                pl.BlockSpec((None, None, 1, 2 * D_FF), lambda i, be, nv: (layer, be[i], 0, 0)),
                pl.BlockSpec((None, None, D_FF, D_MODEL), lambda i, be, nv: (layer, be[i], 0, 0)),
                pl.BlockSpec((None, None, 1, D_MODEL), lambda i, be, nv: (layer, be[i], 0, 0)),
            ],
            out_specs=pl.BlockSpec((tm * SUBLANES, LANES), lambda i, be, nv: (i, 0)),
            scratch_shapes=[pltpu.VMEM((D_MODEL, 2 * D_FF), BF16), pltpu.VMEM((D_FF, D_MODEL), BF16)],
        ),
        compiler_params=pltpu.CompilerParams(dimension_semantics=("arbitrary",),
                                             vmem_limit_bytes=EXPERT_VMEM_LIMIT_BYTES),
        name="expert_ffn",
    )(block_expert, n_valid, xs, wgu, bgu, wd, bd)


def _combine_kernel(slot_ref, ys_hbm, x_ref, wt_ref, o_ref, buf, sems):
    i = pl.program_id(0)
    tc = COMBINE_TILE
    nsteps = pl.num_programs(0)
    n = slot_ref.shape[0] // TOP_K
    cur = lax.rem(i, 2)

    def issue(step, par):
        base = step * tc

        def body(r, c):
            dst0 = pl.multiple_of(r * SUBLANES, SUBLANES)
            for k in range(TOP_K):
                row0 = pl.multiple_of(slot_ref[k * n + base + r], SUBLANES)
                pltpu.make_async_copy(ys_hbm.at[pl.ds(row0, SUBLANES)],
                                      buf.at[par, k, pl.ds(dst0, SUBLANES)],
                                      sems.at[par]).start(priority=k % 2)
            return c

        lax.fori_loop(0, tc, body, 0, unroll=8)

    @pl.when(i == 0)
    def _():
        issue(0, 0)

    @pl.when(i + 1 < nsteps)
    def _():
        issue(i + 1, 1 - cur)

    pltpu.make_async_copy(buf.at[cur], buf.at[cur], sems.at[cur]).wait()
    w = [_bcast_col(wt_ref[...], k, LANES) for k in range(TOP_K)]
    for s in range(ROW_CHUNKS):
        cs = slice(s * LANES, (s + 1) * LANES)
        acc = x_ref[:, cs]
        for k in range(TOP_K):
            acc = acc + w[k] * buf[cur, k, pl.ds(s, tc, stride=SUBLANES), :]
        o_ref[:, cs] = acc


def _combine(slot_flat, ys, x2, wt):
    n = x2.shape[0]
    tc = COMBINE_TILE
    return pl.pallas_call(
        _combine_kernel,
        out_shape=jax.ShapeDtypeStruct((n, D_MODEL), F32),
        grid_spec=pltpu.PrefetchScalarGridSpec(
            num_scalar_prefetch=1,
            grid=(n // tc,),
            in_specs=[pl.BlockSpec(memory_space=pl.ANY),
                      pl.BlockSpec((tc, D_MODEL), lambda i, s: (i, 0)),
                      pl.BlockSpec((tc, LANES), lambda i, s: (i, 0))],
            out_specs=pl.BlockSpec((tc, D_MODEL), lambda i, s: (i, 0)),
            scratch_shapes=[pltpu.VMEM((2, TOP_K, tc * SUBLANES, LANES), F32), pltpu.SemaphoreType.DMA((2,))],
        ),
        compiler_params=_params(("arbitrary",)),
        name="moe_combine",
    )(slot_flat, ys, x2, wt)


def _moe(layer, h, x2, idx, wt, hist3, wgu, bgu, wd, bd):
    n = x2.shape[0]
    tm = MOE_TILE
    n_slots = n * TOP_K + N_EXPERTS * tm
    n_blocks = n_slots // tm
    hist = hist3[:, 0, :N_EXPERTS]
    counts = jnp.sum(hist, axis=0).astype(jnp.int32)
    padded = (counts + tm - 1) // tm * tm
    pad_end = jnp.cumsum(padded)
    pad_start = pad_end - padded
    tile_base = pad_start[None, :].astype(F32) + (jnp.cumsum(hist, axis=0) - hist)
    base3 = jnp.broadcast_to(jnp.pad(tile_base, ((0, 0), (0, LANES - N_EXPERTS)))[:, None, :],
                             (hist.shape[0], 8, LANES))
    blk0 = jnp.arange(n_blocks) * tm
    block_expert = jnp.minimum(jnp.sum(blk0[:, None] >= pad_end[None, :], axis=1), N_EXPERTS - 1).astype(jnp.int32)
    n_valid = (pad_end[-1:] // tm).astype(jnp.int32)
    zero_off = jnp.maximum(pad_end - tm, 0).astype(jnp.int32)

    slot_flat = _slots(idx, base3)[:TOP_K].reshape(-1)
    xs = _dispatch(slot_flat, zero_off, n_valid, h, n_slots)
    ys = _expert_ffn(layer, block_expert, n_valid, xs, wgu, bgu, wd, bd)
    return _combine(slot_flat, ys, x2, wt)


def _t5_bucket(dist):
    n = jnp.maximum(dist, 0)
    max_exact = REL_BUCKETS // 2
    nf = jnp.maximum(n, 1).astype(F32)
    large = max_exact + (jnp.log(nf / max_exact) / math.log(REL_MAX_DIST / max_exact)
                         * (REL_BUCKETS - max_exact)).astype(jnp.int32)
    large = jnp.minimum(large, REL_BUCKETS - 1)
    return jnp.where(n < max_exact, n, large)


def _bias_tables(rel_bias):
    rel_bias = rel_bias.astype(F32)
    tq = Q_TILE

    def lookup(dist):
        bucket = _t5_bucket(dist)
        out = jnp.zeros((NSA_HEADS,) + dist.shape, F32)
        for bk in range(REL_BUCKETS):
            out = jnp.where(bucket[None] == bk, rel_bias[bk].reshape((NSA_HEADS,) + (1,) * dist.ndim), out)
        return out

    t_pos = jnp.arange(SEQ)
    cmp_end = jnp.arange(N_CMP_PAD) * CMP_STRIDE + CMP_LEN - 1
    bias_cmp = lookup(t_pos[:, None] - cmp_end[None, :])
    rr = jnp.arange(tq)[:, None] - jnp.arange(tq)[None, :]
    btab = lookup(jnp.stack([rr, rr + tq, rr + 2 * tq]))
    return bias_cmp, btab


def _layer(layer, x2, b, p, experts, bias_cmp, btab):
    n = x2.shape[0]
    w = p['w_in']
    o_a, o_b, o_q, o_kv, o_g, o_u = 2048, 2052, 2056, 2312, 2696, 2708
    w_small = jnp.concatenate([w[:, o_a:o_q], w[:, o_g:o_u],
                               jnp.zeros((D_MODEL, LANES - 8 - 3 * NSA_HEADS), F32)], axis=1)
    w_cat = jnp.concatenate([w[:, :o_a], w[:, o_q:o_kv], w[:, o_kv:o_g], w[:, o_u:], w_small],
                            axis=1).astype(BF16)
    w_abt = jnp.concatenate([w[:, o_a:o_q].T, jnp.zeros((8, D_MODEL), F32)], axis=0).astype(BF16)
    pcol = jnp.zeros((8, LANES), F32).at[0, :DN_HEADS].set(p['dn_a_log']).at[1, :DN_HEADS].set(p['dn_dt_bias'])
    prow = jnp.zeros((16, LANES), F32).at[:DN_HEADS, 0].set(p['dn_a_log']).at[:DN_HEADS, 1].set(p['dn_dt_bias'])
    dq, dk, dv, gb, grow, z, nq, nkv, small, y_c = _front(
        x2, p['attn_norm_w'][None, :], w_cat, w_abt, p['dn_conv_w'], pcol, prow,
        p['conv_dw_w'], p['conv_dw_b'][None, :], p['conv_ln_w'][None, :], p['conv_ln_b'][None, :])
    small3 = small.reshape(b, SEQ, LANES)

    seq = lambda a: a.reshape(b, SEQ, a.shape[-1])
    y_a = _delta_rule(seq(dq), seq(dk), seq(dv), seq(gb), grow, seq(z), p['dn_norm_w'][None, :])

    kv6 = nkv.reshape(6, b, SEQ, NSA_HEAD_DIM)
    kcmp, vcmp, ksn, kwn = _nsa_prep(kv6, p['nsa_cmp_pos'], p['nsa_cmp_w1'].astype(BF16),
                                     p['nsa_cmp_w2'].astype(BF16), p['nsa_k_norm_w'])
    y_b = _nsa_attn(nq, p['nsa_q_norm_w'][None, :], kcmp, vcmp, ksn, kv6, kwn, small3, bias_cmp, btab)

    rw = jnp.concatenate(_hi_lo(jnp.pad(p['router_w'], ((0, 0), (0, LANES - N_EXPERTS)))), axis=1)
    rb =jnp.concatenate([p['router_b'], jnp.full((LANES - N_EXPERTS,), NEG_INF, F32)])[None, :]
    x_new, h, idx, wt, hist3 = _out_router(y_a.reshape(n, DN_WIDTH), y_b, y_c.reshape(n, CONV_WIDTH), x2,
                                           p['w_out'].astype(BF16), p['ffn_norm_w'][None, :], rw, rb)
    return _moe(layer, h, x_new, idx, wt, hist3, *experts)


def kernel(x, attn_norm_w, w_in, dn_conv_w, dn_a_log, dn_dt_bias, dn_norm_w, nsa_q_norm_w, nsa_k_norm_w, nsa_cmp_pos, nsa_cmp_w1, nsa_cmp_w2, conv_dw_w, conv_dw_b, conv_ln_w, conv_ln_b, w_out, ffn_norm_w, router_w, router_b, w_gate_up, b_gate_up, w_down, b_down, rel_bias):
    b, t, d = x.shape
    assert (t, d) == (SEQ, D_MODEL)
    stacked = dict(attn_norm_w=attn_norm_w, w_in=w_in, dn_conv_w=dn_conv_w, dn_a_log=dn_a_log,
                   dn_dt_bias=dn_dt_bias, dn_norm_w=dn_norm_w, nsa_q_norm_w=nsa_q_norm_w,
                   nsa_k_norm_w=nsa_k_norm_w, nsa_cmp_pos=nsa_cmp_pos, nsa_cmp_w1=nsa_cmp_w1,
                   nsa_cmp_w2=nsa_cmp_w2, conv_dw_w=conv_dw_w, conv_dw_b=conv_dw_b,
                   conv_ln_w=conv_ln_w, conv_ln_b=conv_ln_b, w_out=w_out, ffn_norm_w=ffn_norm_w,
                   router_w=router_w, router_b=router_b)
    experts = (w_gate_up, b_gate_up[:, :, None, :], w_down, b_down[:, :, None, :])
    bias_cmp, btab = _bias_tables(rel_bias)
    x2 = x.reshape(b * t, d)
    for l in range(w_in.shape[0]):
        x2 = _layer(l, x2, b, {k: v[l] for k, v in stacked.items()}, experts, bias_cmp, btab)
    return x2.reshape(b, t, d)
```

```python
import functools
import math

import numpy as np
import jax
import jax.numpy as jnp
from jax import lax
from jax.experimental import pallas as pl
from jax.experimental.pallas import tpu as pltpu

F32 = jnp.float32
BF16 = jnp.bfloat16

D_MODEL = 1024
SEQ = 2048
DN_HEADS = 4
DN_HEAD_DIM = 128
DN_WIDTH = DN_HEADS * DN_HEAD_DIM
DN_CONV = 4
DN_CHUNK = 64
NSA_HEADS = 4
NSA_HEAD_DIM = 64
NSA_WIDTH = NSA_HEADS * NSA_HEAD_DIM
CMP_LEN = 32
CMP_STRIDE = 16
CMP_HIDDEN = 2 * NSA_HEAD_DIM
SLC_BLOCK = 64
SLC_TOP_N = 16
WINDOW = 512
CONV_WIDTH = 256
CONV_KERNEL = 31
REL_BUCKETS = 32
REL_MAX_DIST = 128
N_EXPERTS = 32
TOP_K = 4
D_FF = D_MODEL
SWIGLU_LIMIT = 7.0
SWIGLU_ALPHA = 1.702
EPS = 1e-6
NEG_INF = -1e30
FORCE = 1e4

LANES = 128
VMEM_LIMIT_BYTES = 48 * 1024 * 1024
EXPERT_VMEM_LIMIT_BYTES = 56 * 1024 * 1024

ROW_TILE = 512
SEQ_TILE = 256
Q_TILE = 128
SLC_GROUP = 4
MOE_TILE = 512
COMBINE_TILE = 256

N_CMP_PAD = 128
N_SLC = SEQ // SLC_BLOCK


def _params(sem=None):
    return pltpu.CompilerParams(dimension_semantics=sem, vmem_limit_bytes=VMEM_LIMIT_BYTES)


def _mm(a, b):
    return jnp.dot(a.astype(BF16), b.astype(BF16), preferred_element_type=F32)


def _mm_nt(a, b):
    return lax.dot_general(a.astype(BF16), b.astype(BF16), (((1,), (1,)), ((), ())),
                           preferred_element_type=F32)


def _mm_tn(a, b):
    return lax.dot_general(a.astype(BF16), b.astype(BF16), (((0,), (0,)), ((), ())),
                           preferred_element_type=F32)


def _split3(x):
    hi = x.astype(BF16)
    r1 = x - hi.astype(F32)
    mid = r1.astype(BF16)
    lo = (r1 - mid.astype(F32)).astype(BF16)
    return hi, mid, lo


def _dot01_right(x, m01):
    hi, mid, lo = _split3(x)
    d = lambda p: jnp.dot(p, m01, preferred_element_type=F32)
    return d(hi) + d(mid) + d(lo)


def _dot01_left(m01, x):
    hi, mid, lo = _split3(x)
    d = lambda p: jnp.dot(m01, p, preferred_element_type=F32)
    return d(hi) + d(mid) + d(lo)


def _hi_lo(x):
    hi = x.astype(BF16)
    return hi, (x - hi.astype(F32)).astype(BF16)


def _sigmoid(x):
    return 1.0 / (1.0 + jnp.exp(-x))


def _silu(x):
    return x * _sigmoid(x)


def _softplus(x):
    return jnp.maximum(x, 0.0) + jnp.log(1.0 + jnp.exp(-jnp.abs(x)))


SUBLANES = 8
ROW_CHUNKS = D_MODEL // LANES
assert ROW_CHUNKS == SUBLANES


def _store_tile_rows(ref, value, rows):
    for s in range(ROW_CHUNKS):
        ref[pl.ds(s, rows, stride=SUBLANES), :] = value[:, s * LANES:(s + 1) * LANES]


def _load_tile_rows(ref, rows):
    return jnp.concatenate([ref[pl.ds(s, rows, stride=SUBLANES), :] for s in range(ROW_CHUNKS)], axis=1)


def _bcast_col(x, j, width):
    return jnp.broadcast_to(x[:, j:j + 1], (x.shape[0], width))


IN_SEGS = (3 * DN_WIDTH, DN_WIDTH, NSA_WIDTH, 6 * NSA_HEAD_DIM, 2 * CONV_WIDTH, LANES)
IN_COLS = sum(IN_SEGS)


def _delta_prepare(q_ref, k_ref, v_ref, gb_ref, grow_ref, wy):
    u_ref, w_ref, attn_ref, qg_ref, kk_ref, gc_ref = wy
    ts = SEQ_TILE
    ch = DN_CHUNK
    hd = DN_HEAD_DIM
    r = lax.broadcasted_iota(jnp.int32, (ts, ts), 0)
    c = lax.broadcasted_iota(jnp.int32, (ts, ts), 1)
    same_chunk = (r // ch) == (c // ch)
    tril = same_chunk & (r >= c)
    strict = same_chunk & (r > c)
    same16 = (r // 16) == (c // 16)
    eye = jnp.where(r == c, 1.0, 0.0).astype(F32)
    m_col = jnp.where(tril, 1.0, 0.0).astype(BF16)
    m_row = jnp.where(same_chunk & (r <= c), 1.0, 0.0).astype(BF16)

    gb = gb_ref[0]
    gc_col = _dot01_left(m_col, gb)
    gc_row = _dot01_right(grow_ref[...], m_row)

    heads = range(DN_HEADS)
    hsl = [slice(h * hd, (h + 1) * hd) for h in heads]
    kh = [k_ref[0, :, hsl[h]] for h in heads]
    gcb = [_bcast_col(gc_col, h, ts) for h in heads]
    decay = [jnp.where(tril, jnp.exp(jnp.where(
        tril, gcb[h] - jnp.broadcast_to(gc_row[h:h + 1, :], (ts, ts)), 0.0)), 0.0) for h in heads]
    beta = [_bcast_col(gb, DN_HEADS + h, hd) for h in heads]
    kb = [kh[h] * beta[h] for h in heads]
    yield
    a_mat = [jnp.where(strict, _mm_nt(kb[h], kh[h]) * decay[h], 0.0) for h in heads]
    yield
    d_mat = [jnp.where(same16, a_mat[h], 0.0) for h in heads]
    e_mat = [a_mat[h] - d_mat[h] for h in heads]
    d2 = [_mm(d_mat[h], d_mat[h]) for h in heads]
    yield
    d4 = [_mm(d2[h], d2[h]) for h in heads]
    yield
    t1 = [_mm(eye - d_mat[h], eye + d2[h]) for h in heads]
    yield
    d8 = [_mm(d4[h], d4[h]) for h in heads]
    yield
    t2 = [_mm(t1[h], eye + d4[h]) for h in heads]
    yield
    p_mat = [_mm(t2[h], eye + d8[h]) for h in heads]
    yield
    m_mat = [_mm(p_mat[h], e_mat[h]) for h in heads]
    yield
    m2 = [_mm(m_mat[h], m_mat[h]) for h in heads]
    yield
    t3 = [_mm(eye - m_mat[h], eye + m2[h]) for h in heads]
    yield
    t_mat = [_mm(t3[h], p_mat[h]) for h in heads]
    yield
    for h in heads:
        gc128 = gcb[h][:, :hd]
        expg = jnp.exp(gc128)
        qh = q_ref[0, :, hsl[h]]
        sol = _mm(t_mat[h], jnp.concatenate([v_ref[0, :, hsl[h]] * beta[h], kb[h] * expg], axis=1))
        u_ref[h] = sol[:, :hd]
        w_ref[h] = sol[:, hd:].astype(BF16)
        qg_ref[h] = (qh * expg).astype(BF16)
        kk_ref[h] = kh[h]
        gc_ref[h] = gc128
    yield
    for h in heads:
        attn_ref[h] = jnp.where(tril, _mm_nt(q_ref[0, :, hsl[h]], kh[h]) * decay[h], 0.0).astype(BF16)


def _delta_recur(wy, z_ref, nw_ref, o_ref, s_ref):
    u_ref, w_ref, attn_ref, qg_ref, kk_ref, gc_ref = wy
    ch = DN_CHUNK
    hd = DN_HEAD_DIM
    heads = range(DN_HEADS)
    state = [s_ref[h] for h in heads]
    outs = [[] for _ in heads]
    for ci in range(SEQ_TILE // ch):
        rs = slice(ci * ch, (ci + 1) * ch)
        v_new = [u_ref[h, rs, :] - _mm(w_ref[h, rs, :], state[h]) for h in heads]
        o_state = [_mm(qg_ref[h, rs, :], state[h]) for h in heads]
        yield
        for h in heads:
            gc = gc_ref[h, rs, :]
            g_last = gc[ch - 1:ch, :]
            k_dec = kk_ref[h, rs, :] * jnp.exp(g_last - gc)
            outs[h].append(o_state[h] + _mm(attn_ref[h, rs, ci * ch:(ci + 1) * ch], v_new[h]))
            state[h] = state[h] * jnp.exp(g_last) + _mm_tn(k_dec, v_new[h])
        yield

    for h in heads:
        s_ref[h] = state[h]
        hs = slice(h * hd, (h + 1) * hd)
        o = jnp.concatenate(outs[h], axis=0)
        o = o * lax.rsqrt(jnp.mean(o * o, axis=-1, keepdims=True) + EPS) * nw_ref[...]
        o_ref[0, :, hs] = o * _silu(z_ref[0, :, hs])


def _interleave(weighted):
    live = [gen for gen, _ in weighted]
    while live:
        for gen, n in weighted:
            for _ in range(n):
                if gen in live and next(gen, live) is live:
                    live.remove(gen)


DN_BATCH = 2


def _delta_kernel(q_ref, k_ref, v_ref, gb_ref, grow0_ref, grow1_ref, z_ref, nw_ref, o_ref, s_ref,
                  *wy_refs):
    t = pl.program_id(1)
    set_a, set_b = wy_refs[:6], wy_refs[6:]
    grow_refs = (grow0_ref, grow1_ref)

    @pl.when(t == 0)
    def _():
        s_ref[...] = jnp.zeros(s_ref.shape, F32)
        for ref in set_b:
            ref[...] = jnp.zeros(ref.shape, ref.dtype)

    def step(read_set, write_set):
        one = lambda ref, bb: ref.at[pl.ds(bb, 1)]
        work = []
        for bb in range(DN_BATCH):
            work.append((_delta_prepare(one(q_ref, bb), one(k_ref, bb), one(v_ref, bb), one(gb_ref, bb),
                                        grow_refs[bb], [ref.at[bb] for ref in write_set]), 2))
        for bb in range(DN_BATCH):
            work.append((_delta_recur([ref.at[bb] for ref in read_set], one(z_ref, bb), nw_ref,
                                      one(o_ref, bb), s_ref.at[bb]), 1))
        _interleave(work)

    @pl.when(lax.rem(t, 2) == 0)
    def _():
        step(set_b, set_a)

    @pl.when(lax.rem(t, 2) == 1)
    def _():
        step(set_a, set_b)


def _delta_rule(q, k, v, gb, grow, z, norm_w):
    b = q.shape[0]
    nb = DN_BATCH
    assert b % nb == 0
    ts = SEQ_TILE
    nt = SEQ // ts
    nxt = lambda w: pl.BlockSpec((nb, ts, w), lambda i, t: (i, jnp.minimum(t, nt - 1), 0))
    cur = lambda w: pl.BlockSpec((nb, ts, w), lambda i, t: (i, jnp.maximum(t - 1, 0), 0))
    grow_spec = lambda bb: pl.BlockSpec(
        (8, ts), lambda i, t: (0, (i * nb + bb) * nt + jnp.minimum(t, nt - 1)))
    hd = DN_HEAD_DIM
    wy_set = [pltpu.VMEM((nb, DN_HEADS, ts, hd), F32), pltpu.VMEM((nb, DN_HEADS, ts, hd), BF16),
              pltpu.VMEM((nb, DN_HEADS, ts, ts), BF16), pltpu.VMEM((nb, DN_HEADS, ts, hd), BF16),
              pltpu.VMEM((nb, DN_HEADS, ts, hd), F32), pltpu.VMEM((nb, DN_HEADS, ts, hd), F32)]
    return pl.pallas_call(
        _delta_kernel,
        out_shape=jax.ShapeDtypeStruct((b, SEQ, DN_WIDTH), F32),
        grid=(b // nb, nt + 1),
        in_specs=[
            nxt(DN_WIDTH), nxt(DN_WIDTH), nxt(DN_WIDTH), nxt(LANES),
            grow_spec(0), grow_spec(1),
            cur(DN_WIDTH),
            pl.BlockSpec((1, DN_HEAD_DIM), lambda i, t: (0, 0)),
        ],
        out_specs=cur(DN_WIDTH),
        scratch_shapes=[pltpu.VMEM((nb, DN_HEADS, hd, hd), F32)] + wy_set + wy_set,
        compiler_params=_params(("parallel", "arbitrary")),
        name="delta_rule",
    )(q, k, v, gb, grow, grow, z, norm_w)


CONV_HALO = 32


MIX_COLS = {'qkv': (0, 3 * DN_WIDTH), 'z': (3 * DN_WIDTH, 4 * DN_WIDTH)}
_off = 4 * DN_WIDTH
for _name, _w in (('nq', NSA_WIDTH), ('nkv', 6 * NSA_HEAD_DIM), ('cu', 2 * CONV_WIDTH), ('small', LANES)):
    MIX_COLS[_name] = (_off, _off + _w)
    _off += _w
assert _off == IN_COLS
DN_HALO = 8
MXU_CHUNK = 256


def _front_kernel(x_ref, nw_ref, w_ref, wabt_ref, cw_ref, pcol_ref, prow_ref,
                  dww_ref, dwb_ref, lnw_ref, lnb_ref,
                  q_ref, k_ref, v_ref, gb_ref, grow_ref, z_ref, nq_ref, nkv_ref, small_ref, yc_ref,
                  dn_buf, cf_buf, cf_shift):
    i = pl.program_id(0)
    tm = ROW_TILE
    hd = NSA_HEAD_DIM

    @pl.when(lax.rem(i, SEQ // tm) == 0)
    def _():
        dn_buf[0:DN_HALO, :] = jnp.zeros((DN_HALO, 3 * DN_WIDTH), F32)
        cf_buf[0:CONV_HALO, :] = jnp.zeros((CONV_HALO, CONV_WIDTH), F32)

    xf = x_ref[...]
    hb = (xf * lax.rsqrt(jnp.mean(xf * xf, axis=-1, keepdims=True) + EPS) * nw_ref[...]).astype(BF16)

    def proj(name, lo=0, hi=None):
        c0, c1 = MIX_COLS[name]
        hi = c1 - c0 if hi is None else hi
        return jnp.dot(hb, w_ref[:, c0 + lo:c0 + hi], preferred_element_type=F32)

    def mxu_items():
        cu = proj('cu')
        cf_buf[CONV_HALO:CONV_HALO + tm, :] = cu[:, :CONV_WIDTH] * _sigmoid(cu[:, CONV_WIDTH:])
        yield
        sm = proj('small')
        small_ref[...] = sm
        lane = lax.broadcasted_iota(jnp.int32, sm.shape, 1)
        g_col = -jnp.exp(pcol_ref[0:1, :]) * _softplus(sm + pcol_ref[1:2, :])
        gb_ref[...] = jnp.where(lane < DN_HEADS, g_col, _sigmoid(sm))
        a_t = lax.dot_general(wabt_ref[...], hb, (((1,), (1,)), ((), ())), preferred_element_type=F32)
        g_row = -jnp.exp(prow_ref[:, 0:1]) * _softplus(a_t + prow_ref[:, 1:2])
        grow_ref[...] = g_row[0:8, :]
        yield
        for c0 in range(0, 3 * DN_WIDTH, MXU_CHUNK):
            dn_buf[DN_HALO:DN_HALO + tm, c0:c0 + MXU_CHUNK] = proj('qkv', c0, c0 + MXU_CHUNK)
            yield
        for c0 in range(0, DN_WIDTH, MXU_CHUNK):
            z_ref[:, c0:c0 + MXU_CHUNK] = proj('z', c0, c0 + MXU_CHUNK)
            yield
        res = proj('nq')
        for j in range(NSA_HEADS):
            nq_ref[j] = res[:, j * hd:(j + 1) * hd]
        yield
        res = proj('nkv')
        for j in range(6):
            nkv_ref[j] = res[:, j * hd:(j + 1) * hd]

    def conformer_items():
        span = tm + CONV_HALO - SUBLANES
        for b in range(1, SUBLANES):
            cf_shift[b] = cf_buf[b:b + span, :]
        yield
        base = CONV_HALO - (CONV_KERNEL - 1)
        rows = 64
        for rc in range(tm // rows):
            parts = []
            for cc in range(CONV_WIDTH // LANES):
                cs = slice(cc * LANES, (cc + 1) * LANES)
                acc = None
                for j in range(CONV_KERNEL):
                    start = base + rc * rows + j
                    b = start % SUBLANES
                    a0 = start - b
                    win = cf_buf[a0:a0 + rows, cs] if b == 0 else cf_shift[b, a0:a0 + rows, cs]
                    term = dww_ref[j:j + 1, cs] * win
                    acc = term if acc is None else acc + term
                parts.append(acc)
            h = jnp.concatenate(parts, axis=1) + dwb_ref[...]
            mu = jnp.mean(h, axis=-1, keepdims=True)
            var = jnp.mean(jnp.square(h - mu), axis=-1, keepdims=True)
            hn = (h - mu) * lax.rsqrt(var + EPS) * lnw_ref[...] + lnb_ref[...]
            yc_ref[rc * rows:(rc + 1) * rows, :] = _silu(hn)
            yield
        cf_buf[0:CONV_HALO, :] = cf_buf[tm:tm + CONV_HALO, :]

    def deltanet_items():
        rows = tm // 2
        for c in range(3 * DN_HEADS):
            cs = slice(c * LANES, (c + 1) * LANES)
            for r0 in range(0, tm, rows):
                lo = DN_HALO - (DN_CONV - 1) + r0
                acc = cw_ref[0:1, cs] * dn_buf[lo:lo + rows, cs]
                for j in range(1, DN_CONV):
                    acc = acc + cw_ref[j:j + 1, cs] * dn_buf[lo + j:lo + j + rows, cs]
                y = _silu(acc)
                if c < 2 * DN_HEADS:
                    y = y * lax.rsqrt(jnp.sum(y * y, axis=-1, keepdims=True) + EPS)
                rs = slice(r0, r0 + rows)
                if c < DN_HEADS:
                    q_ref[rs, cs] = y * DN_HEAD_DIM ** -0.5
                elif c < 2 * DN_HEADS:
                    k_ref[rs, (c - DN_HEADS) * LANES:(c - DN_HEADS + 1) * LANES] = y
                else:
                    v_ref[rs, (c - 2 * DN_HEADS) * LANES:(c - 2 * DN_HEADS + 1) * LANES] = y
            dn_buf[0:DN_HALO, cs] = dn_buf[tm:tm + DN_HALO, cs]
            yield

    mxu, cf, dn = mxu_items(), conformer_items(), deltanet_items()
    next(mxu)
    next(mxu)
    next(cf)
    next(mxu)
    _interleave([(mxu, 1), (dn, 2), (cf, 2)])


def _front(x2, norm_w, w_cat, w_abt, conv_w, pcol, prow, dw_w, dw_b, ln_w, ln_b):
    n = x2.shape[0]
    tm = ROW_TILE
    hd = NSA_HEAD_DIM
    row = lambda w: pl.BlockSpec((tm, w), lambda i: (i, 0))
    full = lambda shape: pl.BlockSpec(shape, lambda i: (0,) * len(shape))
    pieces = lambda k: pl.BlockSpec((k, tm, hd), lambda i: (0, i, 0))
    return pl.pallas_call(
        _front_kernel,
        out_shape=[jax.ShapeDtypeStruct((n, DN_WIDTH), F32)] * 3
        + [jax.ShapeDtypeStruct((n, LANES), F32), jax.ShapeDtypeStruct((8, n), F32),
           jax.ShapeDtypeStruct((n, DN_WIDTH), F32),
           jax.ShapeDtypeStruct((NSA_HEADS, n, hd), F32), jax.ShapeDtypeStruct((6, n, hd), F32),
           jax.ShapeDtypeStruct((n, LANES), F32), jax.ShapeDtypeStruct((n, CONV_WIDTH), F32)],
        grid=(n // tm,),
        in_specs=[row(D_MODEL), full((1, D_MODEL)), full((D_MODEL, IN_COLS)), full((16, D_MODEL)),
                  full((DN_CONV, 3 * DN_WIDTH)), full((8, LANES)), full((16, LANES)),
                  full((CONV_KERNEL, CONV_WIDTH)), full((1, CONV_WIDTH)), full((1, CONV_WIDTH)),
                  full((1, CONV_WIDTH))],
        out_specs=[row(DN_WIDTH)] * 3
        + [row(LANES), pl.BlockSpec((8, tm), lambda i: (0, i)), row(DN_WIDTH),
           pieces(NSA_HEADS), pieces(6), row(LANES), row(CONV_WIDTH)],
        scratch_shapes=[pltpu.VMEM((tm + DN_HALO, 3 * DN_WIDTH), F32),
                        pltpu.VMEM((tm + CONV_HALO, CONV_WIDTH), F32),
                        pltpu.VMEM((SUBLANES, tm + CONV_HALO - SUBLANES, CONV_WIDTH), F32)],
        compiler_params=_params(("arbitrary",)),
        name="front",
    )(x2, norm_w, w_cat, w_abt, conv_w, pcol, prow, dw_w, dw_b, ln_w, ln_b)


def _rms_rows(x, w):
    return x * lax.rsqrt(jnp.mean(x * x, axis=-1, keepdims=True) + EPS) * w


def _nsa_prep_kernel(kc_ref, vc_ref, ks_ref, kw_ref, pos_ref, w1_ref, w2_ref, knw_ref,
                     kcmp_ref, vcmp_ref, ksn_ref, kwn_ref):
    hd = NSA_HEAD_DIM

    def compress(x_ref, i):
        u_lo = jnp.zeros((N_CMP_PAD, CMP_HIDDEN), F32)
        u_hi = jnp.zeros((N_CMP_PAD, CMP_HIDDEN), F32)
        for r in range(CMP_STRIDE):
            xr = x_ref[pl.ds(r, N_CMP_PAD, stride=CMP_STRIDE), :]
            lo, hi = r, CMP_STRIDE + r
            u_lo = u_lo + _mm(xr + pos_ref[i, lo:lo + 1, :], w1_ref[i, lo * hd:(lo + 1) * hd, :])
            u_hi = u_hi + _mm(xr + pos_ref[i, hi:hi + 1, :], w1_ref[i, hi * hd:(hi + 1) * hd, :])
        hid = _silu(u_lo + pltpu.roll(u_hi, N_CMP_PAD - 1, axis=0))
        return _mm(hid, w2_ref[i])

    kcmp_ref[0] = _rms_rows(compress(kc_ref, 0), knw_ref[0:1, :])
    vcmp_ref[0] = compress(vc_ref, 1)
    ksn_ref[0] = _rms_rows(ks_ref[...], knw_ref[1:2, :])
    kwn_ref[0] = _rms_rows(kw_ref[...], knw_ref[2:3, :])


def _nsa_prep(kv6, pos, w1, w2, knw):
    b = kv6.shape[1]
    hd = NSA_HEAD_DIM
    full = lambda shape: pl.BlockSpec(shape, lambda i: (0,) * len(shape))
    piece = lambda j: pl.BlockSpec((None, None, SEQ, hd), lambda i: (j, i, 0, 0))
    bspec = lambda r, w: pl.BlockSpec((1, r, w), lambda i: (i, 0, 0))
    return pl.pallas_call(
        _nsa_prep_kernel,
        out_shape=[jax.ShapeDtypeStruct((b, N_CMP_PAD, hd), F32)] * 2
        + [jax.ShapeDtypeStruct((b, SEQ, hd), F32)] * 2,
        grid=(b,),
        in_specs=[
            piece(0), piece(1), piece(2), piece(4),
            full((2, CMP_LEN, hd)), full((2, CMP_LEN * hd, CMP_HIDDEN)), full((2, CMP_HIDDEN, hd)),
            full((3, hd)),
        ],
        out_specs=[bspec(N_CMP_PAD, hd)] * 2 + [bspec(SEQ, hd)] * 2,
        compiler_params=_params(("parallel",)),
        name="nsa_prep",
    )(kv6, kv6, kv6, kv6, pos, w1, w2, knw)


def _nsa_attn_kernel(q_ref, qnw_ref, kcmp_ref, vcmp_ref, ks_ref, vs_ref, kw_ref, vw_ref,
                     gate_ref, bcmp_ref, btab_ref, o_ref,
                     madd_ref, s_ref, ksb_ref, kwb_ref, vsa_ref, vwa_ref):
    i = pl.program_id(1)
    tq = Q_TILE
    nh = NSA_HEADS
    hd = NSA_HEAD_DIM
    t0 = i * tq

    @pl.when(i == 0)
    def _():
        ones = jnp.ones((SEQ, hd), BF16)
        ksb_ref[...] = ks_ref[0].astype(BF16)
        kwb_ref[...] = kw_ref[0].astype(BF16)
        vsa_ref[...] = jnp.concatenate([vs_ref[...].astype(BF16), ones], axis=1)
        vwa_ref[...] = jnp.concatenate([vw_ref[...].astype(BF16), ones], axis=1)

    qs = jnp.concatenate(
        [_rms_rows(q_ref[h], qnw_ref[...]) * hd ** -0.5 for h in range(nh)], axis=0).astype(BF16)

    row = lax.broadcasted_iota(jnp.int32, (tq, LANES), 0)
    lane = lax.broadcasted_iota(jnp.int32, (tq, LANES), 1)
    qpos = t0 + row

    s_all = _mm_nt(qs, kcmp_ref[0])
    cmp_valid = (qpos >= lane * CMP_STRIDE + (CMP_LEN - 1)) & (lane < N_CMP_PAD - 1)
    p_rows = []
    p_sum = jnp.zeros((tq, LANES), F32)
    for h in range(nh):
        s = jnp.where(cmp_valid, s_all[h * tq:(h + 1) * tq] + bcmp_ref[h], NEG_INF)
        m = jnp.max(s, axis=-1, keepdims=True)
        p = jnp.where(cmp_valid, jnp.exp(s - m), 0.0)
        l = jnp.sum(p, axis=-1, keepdims=True)
        p = p * jnp.where(l > 0.0, 1.0 / l, 0.0)
        p_rows.append(p)
        p_sum = p_sum + p
    o_cmp = _mm(jnp.concatenate(p_rows, axis=0), vcmp_ref[0])

    ss = lax.broadcasted_iota(jnp.int32, (N_SLC, N_CMP_PAD), 0)
    jj = lax.broadcasted_iota(jnp.int32, (N_SLC, N_CMP_PAD), 1)
    overlap_t = ((jj * CMP_STRIDE < ss * SLC_BLOCK + SLC_BLOCK)
                 & (jj * CMP_STRIDE + CMP_LEN > ss * SLC_BLOCK) & (jj < N_CMP_PAD - 1))
    overlap_t = jnp.where(overlap_t, 1.0, 0.0).astype(BF16)
    nt = lambda a, b_: lax.dot_general(a, b_, (((1,), (1,)), ((), ())), preferred_element_type=F32)
    p_hi, p_mid, p_lo = _split3(p_sum)
    imp = nt(overlap_t, p_hi) + nt(overlap_t, p_mid) + nt(overlap_t, p_lo)
    blk = lax.broadcasted_iota(jnp.int32, (N_SLC, tq), 0)
    cur = (t0 + lax.broadcasted_iota(jnp.int32, (N_SLC, tq), 1)) // SLC_BLOCK
    causal_blk = blk <= cur
    forced = (blk == 0) | (blk == cur) | (blk == cur - 1)
    imp = jnp.where(causal_blk & forced, FORCE, jnp.where(causal_blk, imp, -1.0))
    rank = jnp.zeros((N_SLC, tq), jnp.int32)
    for s2 in range(N_SLC):
        other = jnp.broadcast_to(imp[s2:s2 + 1, :], (N_SLC, tq))
        beats = (other > imp) | ((other == imp) & (blk > s2))
        rank = rank + jnp.where(beats, 1, 0)
    sel_t = jnp.where(rank < SLC_TOP_N, 1.0, 0.0).astype(BF16)
    er = lax.broadcasted_iota(jnp.int32, (N_SLC, SEQ), 0)
    ec = lax.broadcasted_iota(jnp.int32, (N_SLC, SEQ), 1)
    expand = jnp.where(ec // SLC_BLOCK == er, 1.0, 0.0).astype(BF16)
    key = lax.broadcasted_iota(jnp.int32, (tq, SEQ), 1)
    qall = t0 + lax.broadcasted_iota(jnp.int32, (tq, SEQ), 0)
    chosen = (_mm_tn(sel_t, expand) > 0.5) & (qall >= key)
    madd_ref[...] = jnp.where(chosen, 0.0, NEG_INF)

    neg = jnp.full((nh * tq, tq), NEG_INF, F32)
    zero = jnp.zeros((nh * tq, LANES), F32)

    def scores(k_tile, bias_idx, add):
        s_t = _mm_nt(qs, k_tile)
        return jnp.concatenate(
            [s_t[h * tq:(h + 1) * tq] + (btab_ref[h, bias_idx] + add) for h in range(nh)], axis=0)

    def normalise(acc):
        return acc[:, :hd] * (1.0 / _bcast_col(acc, hd, hd))

    n_groups = (i + SLC_GROUP) // SLC_GROUP

    def slc_pass1(j, m_run):
        for u in range(SLC_GROUP):
            kt = SLC_GROUP * j + u
            k0 = pl.multiple_of(kt * tq, tq)
            s_m = scores(ksb_ref[pl.ds(k0, tq), :], jnp.clip(i - kt, 0, 2), madd_ref[:, pl.ds(k0, tq)])
            s_ref[:, pl.ds(k0, tq)] = s_m
            m_run = jnp.maximum(m_run, s_m)
        return m_run

    m_slc = jnp.broadcast_to(
        jnp.max(lax.fori_loop(0, n_groups, slc_pass1, neg), axis=-1, keepdims=True), (nh * tq, tq))

    def slc_pass2(j, acc):
        for u in range(SLC_GROUP):
            k0 = pl.multiple_of((SLC_GROUP * j + u) * tq, tq)
            p = jnp.exp(s_ref[:, pl.ds(k0, tq)] - m_slc).astype(BF16)
            acc = acc + jnp.dot(p, vsa_ref[pl.ds(k0, tq), :], preferred_element_type=F32)
        return acc

    o_slc = normalise(lax.fori_loop(0, n_groups, slc_pass2, zero))

    kcol = lax.broadcasted_iota(jnp.int32, (tq, tq), 1)
    qrow = t0 + lax.broadcasted_iota(jnp.int32, (tq, tq), 0)
    n_win = WINDOW // tq + 1
    starts = []
    m_run = neg
    for j in range(n_win):
        d = n_win - 1 - j
        k0 = pl.multiple_of(jnp.maximum(i - d, 0) * tq, tq)
        starts.append(k0)
        dist = qrow - (k0 + kcol)
        ok = (dist >= 0) & (dist < WINDOW) & (i >= d)
        s_m = scores(kwb_ref[pl.ds(k0, tq), :], min(d, 2), jnp.where(ok, 0.0, NEG_INF))
        s_ref[:, j * tq:(j + 1) * tq] = s_m
        m_run = jnp.maximum(m_run, s_m)
    m_win = jnp.broadcast_to(jnp.max(m_run, axis=-1, keepdims=True), (nh * tq, tq))
    acc = zero
    for j in range(n_win):
        p = jnp.exp(s_ref[:, j * tq:(j + 1) * tq] - m_win).astype(BF16)
        acc = acc + jnp.dot(p, vwa_ref[pl.ds(starts[j], tq), :], preferred_element_type=F32)
    o_win = normalise(acc)

    gates = _sigmoid(gate_ref[0])
    outs = []
    for h in range(nh):
        hs = slice(h * tq, (h + 1) * tq)
        g0 = 2 * DN_HEADS + 3 * h
        outs.append(_bcast_col(gates, g0, hd) * o_cmp[hs]
                    + _bcast_col(gates, g0 + 1, hd) * o_slc[hs]
                    + _bcast_col(gates, g0 + 2, hd) * o_win[hs])
    o_ref[...] = jnp.concatenate(outs, axis=1)


def _nsa_attn(q, qnw, kcmp, vcmp, ks, kv6, kw, small, bias_cmp, btab):
    b = kv6.shape[1]
    tq = Q_TILE
    nq = SEQ // tq
    nh, hd = NSA_HEADS, NSA_HEAD_DIM
    full = lambda shape: pl.BlockSpec(shape, lambda bi, i: (0,) * len(shape))
    per_b = lambda r: pl.BlockSpec((1, r, hd), lambda bi, i: (bi, 0, 0))
    piece = lambda j: pl.BlockSpec((None, None, SEQ, hd), lambda bi, i: (j, bi, 0, 0))
    heads = pl.BlockSpec((nh, tq, hd), lambda bi, i: (0, bi * nq + i, 0))
    return pl.pallas_call(
        _nsa_attn_kernel,
        out_shape=jax.ShapeDtypeStruct((b * SEQ, NSA_WIDTH), F32),
        grid=(b, nq),
        in_specs=[
            heads,
            full((1, hd)),
            per_b(N_CMP_PAD), per_b(N_CMP_PAD), per_b(SEQ), piece(3), per_b(SEQ), piece(5),
            pl.BlockSpec((1, tq, LANES), lambda bi, i: (bi, i, 0)),
            pl.BlockSpec((nh, tq, N_CMP_PAD), lambda bi, i: (0, i, 0)),
            full((nh, 3, tq, tq)),
        ],
        out_specs=pl.BlockSpec((tq, NSA_WIDTH), lambda bi, i: (bi * nq + i, 0)),
        scratch_shapes=[pltpu.VMEM((tq, SEQ), F32), pltpu.VMEM((nh * tq, SEQ), F32),
                        pltpu.VMEM((SEQ, hd), BF16), pltpu.VMEM((SEQ, hd), BF16),
                        pltpu.VMEM((SEQ, 2 * hd), BF16), pltpu.VMEM((SEQ, 2 * hd), BF16)],
        compiler_params=_params(("parallel", "arbitrary")),
        name="nsa_attn",
    )(q, qnw, kcmp, vcmp, ks, kv6, kw, kv6, small, bias_cmp, btab)


def _out_router_kernel(ya_ref, yb_ref, yc_ref, x_ref, wo_ref, fnw_ref, rw_ref, rb_ref,
                       xo_ref, h_ref, idx_ref, wt_ref, hist_ref):
    wa = DN_WIDTH
    wb = wa + NSA_WIDTH
    y = (jnp.dot(ya_ref[...].astype(BF16), wo_ref[0:wa, :], preferred_element_type=F32)
         + jnp.dot(yb_ref[...].astype(BF16), wo_ref[wa:wb, :], preferred_element_type=F32)
         + jnp.dot(yc_ref[...].astype(BF16), wo_ref[wb:, :], preferred_element_type=F32))
    xn = x_ref[...] + y
    xo_ref[...] = xn
    h = xn * lax.rsqrt(jnp.mean(xn * xn, axis=-1, keepdims=True) + EPS) * fnw_ref[...]
    _store_tile_rows(h_ref, h, h.shape[0])
    h_hi, h_lo = _hi_lo(h)
    r_hi = jnp.dot(h_hi, rw_ref[...], preferred_element_type=F32)
    r_lo = jnp.dot(h_lo, rw_ref[...], preferred_element_type=F32)
    logits = ((r_hi[:, :LANES] + r_hi[:, LANES:]) + (r_lo[:, :LANES] + r_lo[:, LANES:])) + rb_ref[...]
    lane = lax.broadcasted_iota(jnp.int32, logits.shape, 1)
    vals, idxs = [], []
    for _ in range(TOP_K):
        m = jnp.max(logits, axis=-1, keepdims=True)
        ix = jnp.min(jnp.where(logits == m, lane, LANES), axis=-1, keepdims=True)
        vals.append(m)
        idxs.append(ix)
        logits = jnp.where(lane == ix, -jnp.inf, logits)
    es = [jnp.exp(v - vals[0]) for v in vals]
    inv = 1.0 / (es[0] + es[1] + es[2] + es[3])
    idx_out = jnp.zeros(lane.shape, jnp.int32)
    wt_out = jnp.zeros(lane.shape, F32)
    chosen = jnp.zeros(lane.shape, F32)
    for k in range(TOP_K):
        idx_out = jnp.where(lane == k, idxs[k], idx_out)
        wt_out = jnp.where(lane == k, es[k] * inv, wt_out)
        chosen = chosen + jnp.where(lane == idxs[k], 1.0, 0.0)
    idx_ref[...] = idx_out
    wt_ref[...] = wt_out
    hist_ref[0] = jnp.broadcast_to(jnp.sum(chosen, axis=0, keepdims=True), (8, LANES))


def _out_router(ya, yb, yc, x2, w_out, fnw, rw, rb):
    n = x2.shape[0]
    tm = ROW_TILE
    row = lambda w: pl.BlockSpec((tm, w), lambda i: (i, 0))
    full = lambda shape: pl.BlockSpec(shape, lambda i: (0,) * len(shape))
    return pl.pallas_call(
        _out_router_kernel,
        out_shape=[jax.ShapeDtypeStruct((n, D_MODEL), F32), jax.ShapeDtypeStruct((n * SUBLANES, LANES), F32),
                   jax.ShapeDtypeStruct((n, LANES), jnp.int32), jax.ShapeDtypeStruct((n, LANES), F32),
                   jax.ShapeDtypeStruct((n // tm, 8, LANES), F32)],
        grid=(n // tm,),
        in_specs=[row(DN_WIDTH), row(NSA_WIDTH), row(CONV_WIDTH), row(D_MODEL),
                  full((D_MODEL, D_MODEL)), full((1, D_MODEL)), full((D_MODEL, 2 * LANES)), full((1, LANES))],
        out_specs=[row(D_MODEL), pl.BlockSpec((tm * SUBLANES, LANES), lambda i: (i, 0)), row(LANES), row(LANES),
                   pl.BlockSpec((1, 8, LANES), lambda i: (i, 0, 0))],
        compiler_params=_params(("parallel",)),
        name="out_router",
    )(ya, yb, yc, x2, w_out, fnw, rw, rb)


def _slots_kernel(idx_ref, base_ref, slot_ref):
    tm = ROW_TILE
    idx = idx_ref[...]
    lane = lax.broadcasted_iota(jnp.int32, (tm, LANES), 1)
    onehots = [jnp.where(lane == _bcast_col(idx, k, LANES), 1.0, 0.0) for k in range(TOP_K)]
    cnt = (onehots[0] + onehots[1]) + (onehots[2] + onehots[3])
    r = lax.broadcasted_iota(jnp.int32, (tm, tm), 0)
    c = lax.broadcasted_iota(jnp.int32, (tm, tm), 1)
    earlier = jnp.where(r > c, 1.0, 0.0).astype(BF16)
    rank = jnp.dot(earlier, cnt.astype(BF16), preferred_element_type=F32) + base_ref[0, 0:1, :]
    ones = jnp.ones((8, LANES), BF16)
    row8 = lax.broadcasted_iota(jnp.int32, (8, tm), 0)
    out = jnp.zeros((8, tm), F32)
    for k in range(TOP_K):
        hi, mid, lo = _split3(rank * onehots[k])
        d = lambda p: lax.dot_general(ones, p, (((1,), (1,)), ((), ())), preferred_element_type=F32)
        out = jnp.where(row8 == k, d(hi) + d(mid) + d(lo), out)
    slot_ref[...] = (out * SUBLANES).astype(jnp.int32)


def _slots(idx, base3):
    n = idx.shape[0]
    tm = ROW_TILE
    return pl.pallas_call(
        _slots_kernel,
        out_shape=jax.ShapeDtypeStruct((8, n), jnp.int32),
        grid=(n // tm,),
        in_specs=[pl.BlockSpec((tm, LANES), lambda i: (i, 0)),
                  pl.BlockSpec((1, 8, LANES), lambda i: (i, 0, 0))],
        out_specs=pl.BlockSpec((8, tm), lambda i: (0, i)),
        compiler_params=_params(("parallel",)),
        name="moe_slots",
    )(idx, base3)


def _dispatch_kernel(slot_ref, zoff_ref, nv_ref, h_ref, xs_hbm, zbuf, sem, zsem):
    i = pl.program_id(0)
    tm = ROW_TILE
    n = slot_ref.shape[0] // TOP_K
    base = i * tm

    @pl.when(i == 0)
    def _():
        zbuf[...] = jnp.zeros(zbuf.shape, F32)
        blk = MOE_TILE * SUBLANES
        n_blocks = xs_hbm.shape[0] // blk

        def fill(row0):
            return pltpu.make_async_copy(
                zbuf, xs_hbm.at[pl.ds(pl.multiple_of(row0 * SUBLANES, blk), blk)], zsem)

        def fill_region_end(e, c):
            fill(zoff_ref[e]).start()
            return c

        def fill_tail(b, c):
            fill(b * MOE_TILE).start()
            return c

        def wait_fill(b, c):
            fill(0).wait()
            return c

        lax.fori_loop(0, N_EXPERTS, fill_region_end, 0)
        lax.fori_loop(nv_ref[0], n_blocks, fill_tail, 0)
        lax.fori_loop(0, N_EXPERTS + n_blocks - nv_ref[0], wait_fill, 0)

    def body(r, c):
        src = h_ref.at[pl.ds(pl.multiple_of(r * SUBLANES, SUBLANES), SUBLANES)]
        for k in range(TOP_K):
            row0 = pl.multiple_of(slot_ref[k * n + base + r], SUBLANES)
            pltpu.make_async_copy(src, xs_hbm.at[pl.ds(row0, SUBLANES)], sem).start(priority=k % 2)
        return c

    lax.fori_loop(0, tm, body, 0, unroll=8)
    done = xs_hbm.at[pl.ds(0, TOP_K * tm * SUBLANES)]
    pltpu.make_async_copy(done, done, sem).wait()


def _dispatch(slot_flat, zero_off, n_valid, h, n_slots):
    n = h.shape[0] // SUBLANES
    tm = ROW_TILE
    return pl.pallas_call(
        _dispatch_kernel,
        out_shape=jax.ShapeDtypeStruct((n_slots * SUBLANES, LANES), F32),
        grid_spec=pltpu.PrefetchScalarGridSpec(
            num_scalar_prefetch=3,
            grid=(n // tm,),
            in_specs=[pl.BlockSpec((tm * SUBLANES, LANES), lambda i, s, z, nv: (i, 0))],
            out_specs=pl.BlockSpec(memory_space=pl.ANY),
            scratch_shapes=[pltpu.VMEM((MOE_TILE * SUBLANES, LANES), F32),
                            pltpu.SemaphoreType.DMA, pltpu.SemaphoreType.DMA],
        ),
        compiler_params=_params(("arbitrary",)),
        name="moe_dispatch",
    )(slot_flat, zero_off, n_valid, h)


def _expert_kernel(be_ref, nv_ref, xs_ref, wgu_ref, bgu_ref, wd_ref, bd_ref, o_ref, wgu_bf, wd_bf):
    i = pl.program_id(0)
    new_expert = (i == 0) | (be_ref[i] != be_ref[jnp.maximum(i - 1, 0)])

    @pl.when((i < nv_ref[0]) & new_expert)
    def _():
        step = 512
        for c0 in range(0, 2 * D_FF, step):
            wgu_bf[:, c0:c0 + step] = wgu_ref[:, c0:c0 + step].astype(BF16)
        for c0 in range(0, D_MODEL, step):
            wd_bf[:, c0:c0 + step] = wd_ref[:, c0:c0 + step].astype(BF16)

    @pl.when(i < nv_ref[0])
    def _():
        xb = _load_tile_rows(xs_ref, MOE_TILE).astype(BF16)
        gu = jnp.dot(xb, wgu_bf[...], preferred_element_type=F32) + bgu_ref[...]
        gate = jnp.minimum(gu[:, :D_FF], SWIGLU_LIMIT)
        up = jnp.clip(gu[:, D_FF:], -SWIGLU_LIMIT, SWIGLU_LIMIT)
        act = (up + 1.0) * gate * _sigmoid(SWIGLU_ALPHA * gate)
        y = jnp.dot(act.astype(BF16), wd_bf[...], preferred_element_type=F32) + bd_ref[...]
        _store_tile_rows(o_ref, y, MOE_TILE)

    @pl.when(i >= nv_ref[0])
    def _():
        o_ref[...] = jnp.zeros(o_ref.shape, F32)


def _expert_ffn(layer, block_expert, n_valid, xs, wgu, bgu, wd, bd):
    n_slots = xs.shape[0] // SUBLANES
    tm = MOE_TILE
    return pl.pallas_call(
        _expert_kernel,
        out_shape=jax.ShapeDtypeStruct(xs.shape, F32),
        grid_spec=pltpu.PrefetchScalarGridSpec(
            num_scalar_prefetch=2,
            grid=(n_slots // tm,),
            in_specs=[
                pl.BlockSpec((tm * SUBLANES, LANES), lambda i, be, nv: (i, 0)),
                pl.BlockSpec((None, None, D_MODEL, 2 * D_FF), lambda i, be, nv: (layer, be[i], 0, 0)),
                pl.BlockSpec((None, None, 1, 2 * D_FF), lambda i, be, nv: (layer, be[i], 0, 0)),
                pl.BlockSpec((None, None, D_FF, D_MODEL), lambda i, be, nv: (layer, be[i], 0, 0)),
                pl.BlockSpec((None, None, 1, D_MODEL), lambda i, be, nv: (layer, be[i], 0, 0)),
            ],
            out_specs=pl.BlockSpec((tm * SUBLANES, LANES), lambda i, be, nv: (i, 0)),
            scratch_shapes=[pltpu.VMEM((D_MODEL, 2 * D_FF), BF16), pltpu.VMEM((D_FF, D_MODEL), BF16)],
        ),
        compiler_params=pltpu.CompilerParams(dimension_semantics=("arbitrary",),
                                             vmem_limit_bytes=EXPERT_VMEM_LIMIT_BYTES),
        name="expert_ffn",
    )(block_expert, n_valid, xs, wgu, bgu, wd, bd)


def _combine_kernel(slot_ref, ys_hbm, x_ref, wt_ref, o_ref, buf, sems):
    i = pl.program_id(0)
    tc = COMBINE_TILE
    nsteps = pl.num_programs(0)
    n = slot_ref.shape[0] // TOP_K
    cur = lax.rem(i, 2)

    def issue(step, par):
        base = step * tc

        def body(r, c):
            dst0 = pl.multiple_of(r * SUBLANES, SUBLANES)
            for k in range(TOP_K):
                row0 = pl.multiple_of(slot_ref[k * n + base + r], SUBLANES)
                pltpu.make_async_copy(ys_hbm.at[pl.ds(row0, SUBLANES)],
                                      buf.at[par, k, pl.ds(dst0, SUBLANES)],
                                      sems.at[par]).start(priority=k % 2)
            return c

        lax.fori_loop(0, tc, body, 0, unroll=8)

    @pl.when(i == 0)
    def _():
        issue(0, 0)

    @pl.when(i + 1 < nsteps)
    def _():
        issue(i + 1, 1 - cur)

    pltpu.make_async_copy(buf.at[cur], buf.at[cur], sems.at[cur]).wait()
    w = [_bcast_col(wt_ref[...], k, LANES) for k in range(TOP_K)]
    for s in range(ROW_CHUNKS):
        cs = slice(s * LANES, (s + 1) * LANES)
        acc = x_ref[:, cs]
        for k in range(TOP_K):
            acc = acc + w[k] * buf[cur, k, pl.ds(s, tc, stride=SUBLANES), :]
        o_ref[:, cs] = acc


def _combine(slot_flat, ys, x2, wt):
    n = x2.shape[0]
    tc = COMBINE_TILE
    return pl.pallas_call(
        _combine_kernel,
        out_shape=jax.ShapeDtypeStruct((n, D_MODEL), F32),
        grid_spec=pltpu.PrefetchScalarGridSpec(
            num_scalar_prefetch=1,
            grid=(n // tc,),
            in_specs=[pl.BlockSpec(memory_space=pl.ANY),
                      pl.BlockSpec((tc, D_MODEL), lambda i, s: (i, 0)),
                      pl.BlockSpec((tc, LANES), lambda i, s: (i, 0))],
            out_specs=pl.BlockSpec((tc, D_MODEL), lambda i, s: (i, 0)),
            scratch_shapes=[pltpu.VMEM((2, TOP_K, tc * SUBLANES, LANES), F32), pltpu.SemaphoreType.DMA((2,))],
        ),
        compiler_params=_params(("arbitrary",)),
        name="moe_combine",
    )(slot_flat, ys, x2, wt)


def _moe(layer, h, x2, idx, wt, hist3, wgu, bgu, wd, bd):
    n = x2.shape[0]
    tm = MOE_TILE
    n_slots = n * TOP_K + N_EXPERTS * tm
    n_blocks = n_slots // tm
    hist = hist3[:, 0, :N_EXPERTS]
    counts = jnp.sum(hist, axis=0).astype(jnp.int32)
    padded = (counts + tm - 1) // tm * tm
    pad_end = jnp.cumsum(padded)
    pad_start = pad_end - padded
    tile_base = pad_start[None, :].astype(F32) + (jnp.cumsum(hist, axis=0) - hist)
    base3 = jnp.broadcast_to(jnp.pad(tile_base, ((0, 0), (0, LANES - N_EXPERTS)))[:, None, :],
                             (hist.shape[0], 8, LANES))
    blk0 = jnp.arange(n_blocks) * tm
    block_expert = jnp.minimum(jnp.sum(blk0[:, None] >= pad_end[None, :], axis=1), N_EXPERTS - 1).astype(jnp.int32)
    n_valid = (pad_end[-1:] // tm).astype(jnp.int32)
    zero_off = jnp.maximum(pad_end - tm, 0).astype(jnp.int32)

    slot_flat = _slots(idx, base3)[:TOP_K].reshape(-1)
    xs = _dispatch(slot_flat, zero_off, n_valid, h, n_slots)
    ys = _expert_ffn(layer, block_expert, n_valid, xs, wgu, bgu, wd, bd)
    return _combine(slot_flat, ys, x2, wt)


def _t5_bucket(dist):
    n = jnp.maximum(dist, 0)
    max_exact = REL_BUCKETS // 2
    nf = jnp.maximum(n, 1).astype(F32)
    large = max_exact + (jnp.log(nf / max_exact) / math.log(REL_MAX_DIST / max_exact)
                         * (REL_BUCKETS - max_exact)).astype(jnp.int32)
    large = jnp.minimum(large, REL_BUCKETS - 1)
    return jnp.where(n < max_exact, n, large)


def _bias_tables(rel_bias):
    rel_bias = rel_bias.astype(F32)
    tq = Q_TILE

    def lookup(dist):
        bucket = _t5_bucket(dist)
        out = jnp.zeros((NSA_HEADS,) + dist.shape, F32)
        for bk in range(REL_BUCKETS):
            out = jnp.where(bucket[None] == bk, rel_bias[bk].reshape((NSA_HEADS,) + (1,) * dist.ndim), out)
        return out

    t_pos = jnp.arange(SEQ)
    cmp_end = jnp.arange(N_CMP_PAD) * CMP_STRIDE + CMP_LEN - 1
    bias_cmp = lookup(t_pos[:, None] - cmp_end[None, :])
    rr = jnp.arange(tq)[:, None] - jnp.arange(tq)[None, :]
    btab = lookup(jnp.stack([rr, rr + tq, rr + 2 * tq]))
    return bias_cmp, btab


def _layer(layer, x2, b, p, experts, bias_cmp, btab):
    n = x2.shape[0]
    w = p['w_in']
    o_a, o_b, o_q, o_kv, o_g, o_u = 2048, 2052, 2056, 2312, 2696, 2708
    w_small = jnp.concatenate([w[:, o_a:o_q], w[:, o_g:o_u],
                               jnp.zeros((D_MODEL, LANES - 8 - 3 * NSA_HEADS), F32)], axis=1)
    w_cat = jnp.concatenate([w[:, :o_a], w[:, o_q:o_kv], w[:, o_kv:o_g], w[:, o_u:], w_small],
                            axis=1).astype(BF16)
    w_abt = jnp.concatenate([w[:, o_a:o_q].T, jnp.zeros((8, D_MODEL), F32)], axis=0).astype(BF16)
    pcol = jnp.zeros((8, LANES), F32).at[0, :DN_HEADS].set(p['dn_a_log']).at[1, :DN_HEADS].set(p['dn_dt_bias'])
    prow = jnp.zeros((16, LANES), F32).at[:DN_HEADS, 0].set(p['dn_a_log']).at[:DN_HEADS, 1].set(p['dn_dt_bias'])
    dq, dk, dv, gb, grow, z, nq, nkv, small, y_c = _front(
        x2, p['attn_norm_w'][None, :], w_cat, w_abt, p['dn_conv_w'], pcol, prow,
        p['conv_dw_w'], p['conv_dw_b'][None, :], p['conv_ln_w'][None, :], p['conv_ln_b'][None, :])
    small3 = small.reshape(b, SEQ, LANES)

    seq = lambda a: a.reshape(b, SEQ, a.shape[-1])
    y_a = _delta_rule(seq(dq), seq(dk), seq(dv), seq(gb), grow, seq(z), p['dn_norm_w'][None, :])

    kv6 = nkv.reshape(6, b, SEQ, NSA_HEAD_DIM)
    kcmp, vcmp, ksn, kwn = _nsa_prep(kv6, p['nsa_cmp_pos'], p['nsa_cmp_w1'].astype(BF16),
                                     p['nsa_cmp_w2'].astype(BF16), p['nsa_k_norm_w'])
    y_b = _nsa_attn(nq, p['nsa_q_norm_w'][None, :], kcmp, vcmp, ksn, kv6, kwn, small3, bias_cmp, btab)

    rw = jnp.concatenate(_hi_lo(jnp.pad(p['router_w'], ((0, 0), (0, LANES - N_EXPERTS)))), axis=1)
    rb =jnp.concatenate([p['router_b'], jnp.full((LANES - N_EXPERTS,), NEG_INF, F32)])[None, :]
    x_new, h, idx, wt, hist3 = _out_router(y_a.reshape(n, DN_WIDTH), y_b, y_c.reshape(n, CONV_WIDTH), x2,
                                           p['w_out'].astype(BF16), p['ffn_norm_w'][None, :], rw, rb)
    return _moe(layer, h, x_new, idx, wt, hist3, *experts)


def kernel(x, attn_norm_w, w_in, dn_conv_w, dn_a_log, dn_dt_bias, dn_norm_w, nsa_q_norm_w, nsa_k_norm_w, nsa_cmp_pos, nsa_cmp_w1, nsa_cmp_w2, conv_dw_w, conv_dw_b, conv_ln_w, conv_ln_b, w_out, ffn_norm_w, router_w, router_b, w_gate_up, b_gate_up, w_down, b_down, rel_bias):
    b, t, d = x.shape
    assert (t, d) == (SEQ, D_MODEL)
    stacked = dict(attn_norm_w=attn_norm_w, w_in=w_in, dn_conv_w=dn_conv_w, dn_a_log=dn_a_log,
                   dn_dt_bias=dn_dt_bias, dn_norm_w=dn_norm_w, nsa_q_norm_w=nsa_q_norm_w,
                   nsa_k_norm_w=nsa_k_norm_w, nsa_cmp_pos=nsa_cmp_pos, nsa_cmp_w1=nsa_cmp_w1,
                   nsa_cmp_w2=nsa_cmp_w2, conv_dw_w=conv_dw_w, conv_dw_b=conv_dw_b,
                   conv_ln_w=conv_ln_w, conv_ln_b=conv_ln_b, w_out=w_out, ffn_norm_w=ffn_norm_w,
                   router_w=router_w, router_b=router_b)
    experts = (w_gate_up, b_gate_up[:, :, None, :], w_down, b_down[:, :, None, :])
    bias_cmp, btab = _bias_tables(rel_bias)
    x2 = x.reshape(b * t, d)
    for l in range(w_in.shape[0]):
        x2 = _layer(l, x2, b, {k: v[l] for k, v in stacked.items()}, experts, bias_cmp, btab)
    return x2.reshape(b, t, d)
```

```python
import functools
import math

import numpy as np
import jax
import jax.numpy as jnp
from jax import lax
from jax.experimental import pallas as pl
from jax.experimental.pallas import tpu as pltpu

F32 = jnp.float32
BF16 = jnp.bfloat16

D_MODEL = 1024
SEQ = 2048
DN_HEADS = 4
DN_HEAD_DIM = 128
DN_WIDTH = DN_HEADS * DN_HEAD_DIM
DN_CONV = 4
DN_CHUNK = 64
NSA_HEADS = 4
NSA_HEAD_DIM = 64
NSA_WIDTH = NSA_HEADS * NSA_HEAD_DIM
CMP_LEN = 32
CMP_STRIDE = 16
CMP_HIDDEN = 2 * NSA_HEAD_DIM
SLC_BLOCK = 64
SLC_TOP_N = 16
WINDOW = 512
CONV_WIDTH = 256
CONV_KERNEL = 31
REL_BUCKETS = 32
REL_MAX_DIST = 128
N_EXPERTS = 32
TOP_K = 4
D_FF = D_MODEL
SWIGLU_LIMIT = 7.0
SWIGLU_ALPHA = 1.702
EPS = 1e-6
NEG_INF = -1e30
FORCE = 1e4

LANES = 128
VMEM_LIMIT_BYTES = 48 * 1024 * 1024
EXPERT_VMEM_LIMIT_BYTES = 56 * 1024 * 1024

ROW_TILE = 512
SEQ_TILE = 256
Q_TILE = 128
SLC_GROUP = 4
MOE_TILE = 512
COMBINE_TILE = 256

N_CMP_PAD = 128
N_SLC = SEQ // SLC_BLOCK


def _params(sem=None):
    return pltpu.CompilerParams(dimension_semantics=sem, vmem_limit_bytes=VMEM_LIMIT_BYTES)


def _mm(a, b):
    return jnp.dot(a.astype(BF16), b.astype(BF16), preferred_element_type=F32)


def _mm_nt(a, b):
    return lax.dot_general(a.astype(BF16), b.astype(BF16), (((1,), (1,)), ((), ())),
                           preferred_element_type=F32)


def _mm_tn(a, b):
    return lax.dot_general(a.astype(BF16), b.astype(BF16), (((0,), (0,)), ((), ())),
                           preferred_element_type=F32)


def _split3(x):
    hi = x.astype(BF16)
    r1 = x - hi.astype(F32)
    mid = r1.astype(BF16)
    lo = (r1 - mid.astype(F32)).astype(BF16)
    return hi, mid, lo


def _dot01_right(x, m01):
    hi, mid, lo = _split3(x)
    d = lambda p: jnp.dot(p, m01, preferred_element_type=F32)
    return d(hi) + d(mid) + d(lo)


def _dot01_left(m01, x):
    hi, mid, lo = _split3(x)
    d = lambda p: jnp.dot(m01, p, preferred_element_type=F32)
    return d(hi) + d(mid) + d(lo)


def _hi_lo(x):
    hi = x.astype(BF16)
    return hi, (x - hi.astype(F32)).astype(BF16)


def _sigmoid(x):
    return 1.0 / (1.0 + jnp.exp(-x))


def _silu(x):
    return x * _sigmoid(x)


def _softplus(x):
    return jnp.maximum(x, 0.0) + jnp.log(1.0 + jnp.exp(-jnp.abs(x)))


SUBLANES = 8
ROW_CHUNKS = D_MODEL // LANES
assert ROW_CHUNKS == SUBLANES


def _store_tile_rows(ref, value, rows):
    for s in range(ROW_CHUNKS):
        ref[pl.ds(s, rows, stride=SUBLANES), :] = value[:, s * LANES:(s + 1) * LANES]


def _load_tile_rows(ref, rows):
    return jnp.concatenate([ref[pl.ds(s, rows, stride=SUBLANES), :] for s in range(ROW_CHUNKS)], axis=1)


def _bcast_col(x, j, width):
    return jnp.broadcast_to(x[:, j:j + 1], (x.shape[0], width))


IN_SEGS = (3 * DN_WIDTH, DN_WIDTH, NSA_WIDTH, 6 * NSA_HEAD_DIM, 2 * CONV_WIDTH, LANES)
IN_COLS = sum(IN_SEGS)


def _delta_prepare(q_ref, k_ref, v_ref, gb_ref, grow_ref, wy):
    u_ref, w_ref, attn_ref, qg_ref, kk_ref, gc_ref = wy
    ts = SEQ_TILE
    ch = DN_CHUNK
    hd = DN_HEAD_DIM
    r = lax.broadcasted_iota(jnp.int32, (ts, ts), 0)
    c = lax.broadcasted_iota(jnp.int32, (ts, ts), 1)
    same_chunk = (r // ch) == (c // ch)
    tril = same_chunk & (r >= c)
    strict = same_chunk & (r > c)
    same16 = (r // 16) == (c // 16)
    eye = jnp.where(r == c, 1.0, 0.0).astype(F32)
    m_col = jnp.where(tril, 1.0, 0.0).astype(BF16)
    m_row = jnp.where(same_chunk & (r <= c), 1.0, 0.0).astype(BF16)

    gb = gb_ref[0]
    gc_col = _dot01_left(m_col, gb)
    gc_row = _dot01_right(grow_ref[...], m_row)

    heads = range(DN_HEADS)
    hsl = [slice(h * hd, (h + 1) * hd) for h in heads]
    kh = [k_ref[0, :, hsl[h]] for h in heads]
    gcb = [_bcast_col(gc_col, h, ts) for h in heads]
    decay = [jnp.where(tril, jnp.exp(jnp.where(
        tril, gcb[h] - jnp.broadcast_to(gc_row[h:h + 1, :], (ts, ts)), 0.0)), 0.0) for h in heads]
    beta = [_bcast_col(gb, DN_HEADS + h, hd) for h in heads]
    kb = [kh[h] * beta[h] for h in heads]
    yield
    a_mat = [jnp.where(strict, _mm_nt(kb[h], kh[h]) * decay[h], 0.0) for h in heads]
    yield
    d_mat = [jnp.where(same16, a_mat[h], 0.0) for h in heads]
    e_mat = [a_mat[h] - d_mat[h] for h in heads]
    d2 = [_mm(d_mat[h], d_mat[h]) for h in heads]
    yield
    d4 = [_mm(d2[h], d2[h]) for h in heads]
    yield
    t1 = [_mm(eye - d_mat[h], eye + d2[h]) for h in heads]
    yield
    d8 = [_mm(d4[h], d4[h]) for h in heads]
    yield
    t2 = [_mm(t1[h], eye + d4[h]) for h in heads]
    yield
    p_mat = [_mm(t2[h], eye + d8[h]) for h in heads]
    yield
    m_mat = [_mm(p_mat[h], e_mat[h]) for h in heads]
    yield
    m2 = [_mm(m_mat[h], m_mat[h]) for h in heads]
    yield
    t3 = [_mm(eye - m_mat[h], eye + m2[h]) for h in heads]
    yield
    t_mat = [_mm(t3[h], p_mat[h]) for h in heads]
    yield
    for h in heads:
        gc128 = gcb[h][:, :hd]
        expg = jnp.exp(gc128)
        qh = q_ref[0, :, hsl[h]]
        sol = _mm(t_mat[h], jnp.concatenate([v_ref[0, :, hsl[h]] * beta[h], kb[h] * expg], axis=1))
        u_ref[h] = sol[:, :hd]
        w_ref[h] = sol[:, hd:].astype(BF16)
        qg_ref[h] = (qh * expg).astype(BF16)
        kk_ref[h] = kh[h]
        gc_ref[h] = gc128
    yield
    for h in heads:
        attn_ref[h] = jnp.where(tril, _mm_nt(q_ref[0, :, hsl[h]], kh[h]) * decay[h], 0.0).astype(BF16)


def _delta_recur(wy, z_ref, nw_ref, o_ref, s_ref):
    u_ref, w_ref, attn_ref, qg_ref, kk_ref, gc_ref = wy
    ch = DN_CHUNK
    hd = DN_HEAD_DIM
    heads = range(DN_HEADS)
    state = [s_ref[h] for h in heads]
    outs = [[] for _ in heads]
    for ci in range(SEQ_TILE // ch):
        rs = slice(ci * ch, (ci + 1) * ch)
        v_new = [u_ref[h, rs, :] - _mm(w_ref[h, rs, :], state[h]) for h in heads]
        o_state = [_mm(qg_ref[h, rs, :], state[h]) for h in heads]
        yield
        for h in heads:
            gc = gc_ref[h, rs, :]
            g_last = gc[ch - 1:ch, :]
            k_dec = kk_ref[h, rs, :] * jnp.exp(g_last - gc)
            outs[h].append(o_state[h] + _mm(attn_ref[h, rs, ci * ch:(ci + 1) * ch], v_new[h]))
            state[h] = state[h] * jnp.exp(g_last) + _mm_tn(k_dec, v_new[h])
        yield

    for h in heads:
        s_ref[h] = state[h]
        hs = slice(h * hd, (h + 1) * hd)
        o = jnp.concatenate(outs[h], axis=0)
        o = o * lax.rsqrt(jnp.mean(o * o, axis=-1, keepdims=True) + EPS) * nw_ref[...]
        o_ref[0, :, hs] = o * _silu(z_ref[0, :, hs])


def _interleave(weighted):
    live = [gen for gen, _ in weighted]
    while live:
        for gen, n in weighted:
            for _ in range(n):
                if gen in live and next(gen, live) is live:
                    live.remove(gen)


DN_BATCH = 2


def _delta_kernel(q_ref, k_ref, v_ref, gb_ref, grow0_ref, grow1_ref, z_ref, nw_ref, o_ref, s_ref,
                  *wy_refs):
    t = pl.program_id(1)
    set_a, set_b = wy_refs[:6], wy_refs[6:]
    grow_refs = (grow0_ref, grow1_ref)

    @pl.when(t == 0)
    def _():
        s_ref[...] = jnp.zeros(s_ref.shape, F32)
        for ref in set_b:
            ref[...] = jnp.zeros(ref.shape, ref.dtype)

    def step(read_set, write_set):
        one = lambda ref, bb: ref.at[pl.ds(bb, 1)]
        work = []
        for bb in range(DN_BATCH):
            work.append((_delta_prepare(one(q_ref, bb), one(k_ref, bb), one(v_ref, bb), one(gb_ref, bb),
                                        grow_refs[bb], [ref.at[bb] for ref in write_set]), 2))
        for bb in range(DN_BATCH):
            work.append((_delta_recur([ref.at[bb] for ref in read_set], one(z_ref, bb), nw_ref,
                                      one(o_ref, bb), s_ref.at[bb]), 1))
        _interleave(work)

    @pl.when(lax.rem(t, 2) == 0)
    def _():
        step(set_b, set_a)

    @pl.when(lax.rem(t, 2) == 1)
    def _():
        step(set_a, set_b)


def _delta_rule(q, k, v, gb, grow, z, norm_w):
    b = q.shape[0]
    nb = DN_BATCH
    assert b % nb == 0
    ts = SEQ_TILE
    nt = SEQ // ts
    nxt = lambda w: pl.BlockSpec((nb, ts, w), lambda i, t: (i, jnp.minimum(t, nt - 1), 0))
    cur = lambda w: pl.BlockSpec((nb, ts, w), lambda i, t: (i, jnp.maximum(t - 1, 0), 0))
    grow_spec = lambda bb: pl.BlockSpec(
        (8, ts), lambda i, t: (0, (i * nb + bb) * nt + jnp.minimum(t, nt - 1)))
    hd = DN_HEAD_DIM
    wy_set = [pltpu.VMEM((nb, DN_HEADS, ts, hd), F32), pltpu.VMEM((nb, DN_HEADS, ts, hd), BF16),
              pltpu.VMEM((nb, DN_HEADS, ts, ts), BF16), pltpu.VMEM((nb, DN_HEADS, ts, hd), BF16),
              pltpu.VMEM((nb, DN_HEADS, ts, hd), F32), pltpu.VMEM((nb, DN_HEADS, ts, hd), F32)]
    return pl.pallas_call(
        _delta_kernel,
        out_shape=jax.ShapeDtypeStruct((b, SEQ, DN_WIDTH), F32),
        grid=(b // nb, nt + 1),
        in_specs=[
            nxt(DN_WIDTH), nxt(DN_WIDTH), nxt(DN_WIDTH), nxt(LANES),
            grow_spec(0), grow_spec(1),
            cur(DN_WIDTH),
            pl.BlockSpec((1, DN_HEAD_DIM), lambda i, t: (0, 0)),
        ],
        out_specs=cur(DN_WIDTH),
        scratch_shapes=[pltpu.VMEM((nb, DN_HEADS, hd, hd), F32)] + wy_set + wy_set,
        compiler_params=_params(("parallel", "arbitrary")),
        name="delta_rule",
    )(q, k, v, gb, grow, grow, z, norm_w)


CONV_HALO = 32


MIX_COLS = {'qkv': (0, 3 * DN_WIDTH), 'z': (3 * DN_WIDTH, 4 * DN_WIDTH)}
_off = 4 * DN_WIDTH
for _name, _w in (('nq', NSA_WIDTH), ('nkv', 6 * NSA_HEAD_DIM), ('cu', 2 * CONV_WIDTH), ('small', LANES)):
    MIX_COLS[_name] = (_off, _off + _w)
    _off += _w
assert _off == IN_COLS
DN_HALO = 8
MXU_CHUNK = 256


def _front_kernel(x_ref, nw_ref, w_ref, wabt_ref, cw_ref, pcol_ref, prow_ref,
                  dww_ref, dwb_ref, lnw_ref, lnb_ref,
                  q_ref, k_ref, v_ref, gb_ref, grow_ref, z_ref, nq_ref, nkv_ref, small_ref, yc_ref,
                  dn_buf, cf_buf, cf_shift):
    i = pl.program_id(0)
    tm = ROW_TILE
    hd = NSA_HEAD_DIM

    @pl.when(lax.rem(i, SEQ // tm) == 0)
    def _():
        dn_buf[0:DN_HALO, :] = jnp.zeros((DN_HALO, 3 * DN_WIDTH), F32)
        cf_buf[0:CONV_HALO, :] = jnp.zeros((CONV_HALO, CONV_WIDTH), F32)

    xf = x_ref[...]
    hb = (xf * lax.rsqrt(jnp.mean(xf * xf, axis=-1, keepdims=True) + EPS) * nw_ref[...]).astype(BF16)

    def proj(name, lo=0, hi=None):
        c0, c1 = MIX_COLS[name]
        hi = c1 - c0 if hi is None else hi
        return jnp.dot(hb, w_ref[:, c0 + lo:c0 + hi], preferred_element_type=F32)

    def mxu_items():
        cu = proj('cu')
        cf_buf[CONV_HALO:CONV_HALO + tm, :] = cu[:, :CONV_WIDTH] * _sigmoid(cu[:, CONV_WIDTH:])
        yield
        sm = proj('small')
        small_ref[...] = sm
        lane = lax.broadcasted_iota(jnp.int32, sm.shape, 1)
        g_col = -jnp.exp(pcol_ref[0:1, :]) * _softplus(sm + pcol_ref[1:2, :])
        gb_ref[...] = jnp.where(lane < DN_HEADS, g_col, _sigmoid(sm))
        a_t = lax.dot_general(wabt_ref[...], hb, (((1,), (1,)), ((), ())), preferred_element_type=F32)
        g_row = -jnp.exp(prow_ref[:, 0:1]) * _softplus(a_t + prow_ref[:, 1:2])
        grow_ref[...] = g_row[0:8, :]
        yield
        for c0 in range(0, 3 * DN_WIDTH, MXU_CHUNK):
            dn_buf[DN_HALO:DN_HALO + tm, c0:c0 + MXU_CHUNK] = proj('qkv', c0, c0 + MXU_CHUNK)
            yield
        for c0 in range(0, DN_WIDTH, MXU_CHUNK):
            z_ref[:, c0:c0 + MXU_CHUNK] = proj('z', c0, c0 + MXU_CHUNK)
            yield
        res = proj('nq')
        for j in range(NSA_HEADS):
            nq_ref[j] = res[:, j * hd:(j + 1) * hd]
        yield
        res = proj('nkv')
        for j in range(6):
            nkv_ref[j] = res[:, j * hd:(j + 1) * hd]

    def conformer_items():
        span = tm + CONV_HALO - SUBLANES
        for b in range(1, SUBLANES):
            cf_shift[b] = cf_buf[b:b + span, :]
        yield
        base = CONV_HALO - (CONV_KERNEL - 1)
        rows = 64
        for rc in range(tm // rows):
            parts = []
            for cc in range(CONV_WIDTH // LANES):
                cs = slice(cc * LANES, (cc + 1) * LANES)
                acc = None
                for j in range(CONV_KERNEL):
                    start = base + rc * rows + j
                    b = start % SUBLANES
                    a0 = start - b
                    win = cf_buf[a0:a0 + rows, cs] if b == 0 else cf_shift[b, a0:a0 + rows, cs]
                    term = dww_ref[j:j + 1, cs] * win
                    acc = term if acc is None else acc + term
                parts.append(acc)
            h = jnp.concatenate(parts, axis=1) + dwb_ref[...]
            mu = jnp.mean(h, axis=-1, keepdims=True)
            var = jnp.mean(jnp.square(h - mu), axis=-1, keepdims=True)
            hn = (h - mu) * lax.rsqrt(var + EPS) * lnw_ref[...] + lnb_ref[...]
            yc_ref[rc * rows:(rc + 1) * rows, :] = _silu(hn)
            yield
        cf_buf[0:CONV_HALO, :] = cf_buf[tm:tm + CONV_HALO, :]

    def deltanet_items():
        rows = tm // 2
        for c in range(3 * DN_HEADS):
            cs = slice(c * LANES, (c + 1) * LANES)
            for r0 in range(0, tm, rows):
                lo = DN_HALO - (DN_CONV - 1) + r0
                acc = cw_ref[0:1, cs] * dn_buf[lo:lo + rows, cs]
                for j in range(1, DN_CONV):
                    acc = acc + cw_ref[j:j + 1, cs] * dn_buf[lo + j:lo + j + rows, cs]
                y = _silu(acc)
                if c < 2 * DN_HEADS:
                    y = y * lax.rsqrt(jnp.sum(y * y, axis=-1, keepdims=True) + EPS)
                rs = slice(r0, r0 + rows)
                if c < DN_HEADS:
                    q_ref[rs, cs] = y * DN_HEAD_DIM ** -0.5
                elif c < 2 * DN_HEADS:
                    k_ref[rs, (c - DN_HEADS) * LANES:(c - DN_HEADS + 1) * LANES] = y
                else:
                    v_ref[rs, (c - 2 * DN_HEADS) * LANES:(c - 2 * DN_HEADS + 1) * LANES] = y
            dn_buf[0:DN_HALO, cs] = dn_buf[tm:tm + DN_HALO, cs]
            yield

    mxu, cf, dn = mxu_items(), conformer_items(), deltanet_items()
    next(mxu)
    next(mxu)
    next(cf)
    next(mxu)
    _interleave([(mxu, 1), (dn, 2), (cf, 2)])


def _front(x2, norm_w, w_cat, w_abt, conv_w, pcol, prow, dw_w, dw_b, ln_w, ln_b):
    n = x2.shape[0]
    tm = ROW_TILE
    hd = NSA_HEAD_DIM
    row = lambda w: pl.BlockSpec((tm, w), lambda i: (i, 0))
    full = lambda shape: pl.BlockSpec(shape, lambda i: (0,) * len(shape))
    pieces = lambda k: pl.BlockSpec((k, tm, hd), lambda i: (0, i, 0))
    return pl.pallas_call(
        _front_kernel,
        out_shape=[jax.ShapeDtypeStruct((n, DN_WIDTH), F32)] * 3
        + [jax.ShapeDtypeStruct((n, LANES), F32), jax.ShapeDtypeStruct((8, n), F32),
           jax.ShapeDtypeStruct((n, DN_WIDTH), F32),
           jax.ShapeDtypeStruct((NSA_HEADS, n, hd), F32), jax.ShapeDtypeStruct((6, n, hd), F32),
           jax.ShapeDtypeStruct((n, LANES), F32), jax.ShapeDtypeStruct((n, CONV_WIDTH), F32)],
        grid=(n // tm,),
        in_specs=[row(D_MODEL), full((1, D_MODEL)), full((D_MODEL, IN_COLS)), full((16, D_MODEL)),
                  full((DN_CONV, 3 * DN_WIDTH)), full((8, LANES)), full((16, LANES)),
                  full((CONV_KERNEL, CONV_WIDTH)), full((1, CONV_WIDTH)), full((1, CONV_WIDTH)),
                  full((1, CONV_WIDTH))],
        out_specs=[row(DN_WIDTH)] * 3
        + [row(LANES), pl.BlockSpec((8, tm), lambda i: (0, i)), row(DN_WIDTH),
           pieces(NSA_HEADS), pieces(6), row(LANES), row(CONV_WIDTH)],
        scratch_shapes=[pltpu.VMEM((tm + DN_HALO, 3 * DN_WIDTH), F32),
                        pltpu.VMEM((tm + CONV_HALO, CONV_WIDTH), F32),
                        pltpu.VMEM((SUBLANES, tm + CONV_HALO - SUBLANES, CONV_WIDTH), F32)],
        compiler_params=_params(("arbitrary",)),
        name="front",
    )(x2, norm_w, w_cat, w_abt, conv_w, pcol, prow, dw_w, dw_b, ln_w, ln_b)


def _rms_rows(x, w):
    return x * lax.rsqrt(jnp.mean(x * x, axis=-1, keepdims=True) + EPS) * w


def _nsa_prep_kernel(kc_ref, vc_ref, ks_ref, kw_ref, pos_ref, w1_ref, w2_ref, knw_ref,
                     kcmp_ref, vcmp_ref, ksn_ref, kwn_ref):
    hd = NSA_HEAD_DIM

    def compress(x_ref, i):
        u_lo = jnp.zeros((N_CMP_PAD, CMP_HIDDEN), F32)
        u_hi = jnp.zeros((N_CMP_PAD, CMP_HIDDEN), F32)
        for r in range(CMP_STRIDE):
            xr = x_ref[pl.ds(r, N_CMP_PAD, stride=CMP_STRIDE), :]
            lo, hi = r, CMP_STRIDE + r
            u_lo = u_lo + _mm(xr + pos_ref[i, lo:lo + 1, :], w1_ref[i, lo * hd:(lo + 1) * hd, :])
            u_hi = u_hi + _mm(xr + pos_ref[i, hi:hi + 1, :], w1_ref[i, hi * hd:(hi + 1) * hd, :])
        hid = _silu(u_lo + pltpu.roll(u_hi, N_CMP_PAD - 1, axis=0))
        return _mm(hid, w2_ref[i])

    kcmp_ref[0] = _rms_rows(compress(kc_ref, 0), knw_ref[0:1, :])
    vcmp_ref[0] = compress(vc_ref, 1)
    ksn_ref[0] = _rms_rows(ks_ref[...], knw_ref[1:2, :])
    kwn_ref[0] = _rms_rows(kw_ref[...], knw_ref[2:3, :])


def _nsa_prep(kv6, pos, w1, w2, knw):
    b = kv6.shape[1]
    hd = NSA_HEAD_DIM
    full = lambda shape: pl.BlockSpec(shape, lambda i: (0,) * len(shape))
    piece = lambda j: pl.BlockSpec((None, None, SEQ, hd), lambda i: (j, i, 0, 0))
    bspec = lambda r, w: pl.BlockSpec((1, r, w), lambda i: (i, 0, 0))
    return pl.pallas_call(
        _nsa_prep_kernel,
        out_shape=[jax.ShapeDtypeStruct((b, N_CMP_PAD, hd), F32)] * 2
        + [jax.ShapeDtypeStruct((b, SEQ, hd), F32)] * 2,
        grid=(b,),
        in_specs=[
            piece(0), piece(1), piece(2), piece(4),
            full((2, CMP_LEN, hd)), full((2, CMP_LEN * hd, CMP_HIDDEN)), full((2, CMP_HIDDEN, hd)),
            full((3, hd)),
        ],
        out_specs=[bspec(N_CMP_PAD, hd)] * 2 + [bspec(SEQ, hd)] * 2,
        compiler_params=_params(("parallel",)),
        name="nsa_prep",
    )(kv6, kv6, kv6, kv6, pos, w1, w2, knw)


def _nsa_attn_kernel(q_ref, qnw_ref, kcmp_ref, vcmp_ref, ks_ref, vs_ref, kw_ref, vw_ref,
                     gate_ref, bcmp_ref, btab_ref, o_ref,
                     madd_ref, s_ref, ksb_ref, kwb_ref, vsa_ref, vwa_ref):
    i = pl.program_id(1)
    tq = Q_TILE
    nh = NSA_HEADS
    hd = NSA_HEAD_DIM
    t0 = i * tq

    @pl.when(i == 0)
    def _():
        ones = jnp.ones((SEQ, hd), BF16)
        ksb_ref[...] = ks_ref[0].astype(BF16)
        kwb_ref[...] = kw_ref[0].astype(BF16)
        vsa_ref[...] = jnp.concatenate([vs_ref[...].astype(BF16), ones], axis=1)
        vwa_ref[...] = jnp.concatenate([vw_ref[...].astype(BF16), ones], axis=1)

    qs = jnp.concatenate(
        [_rms_rows(q_ref[h], qnw_ref[...]) * hd ** -0.5 for h in range(nh)], axis=0).astype(BF16)

    row = lax.broadcasted_iota(jnp.int32, (tq, LANES), 0)
    lane = lax.broadcasted_iota(jnp.int32, (tq, LANES), 1)
    qpos = t0 + row

    s_all = _mm_nt(qs, kcmp_ref[0])
    cmp_valid = (qpos >= lane * CMP_STRIDE + (CMP_LEN - 1)) & (lane < N_CMP_PAD - 1)
    p_rows = []
    p_sum = jnp.zeros((tq, LANES), F32)
    for h in range(nh):
        s = jnp.where(cmp_valid, s_all[h * tq:(h + 1) * tq] + bcmp_ref[h], NEG_INF)
        m = jnp.max(s, axis=-1, keepdims=True)
        p = jnp.where(cmp_valid, jnp.exp(s - m), 0.0)
        l = jnp.sum(p, axis=-1, keepdims=True)
        p = p * jnp.where(l > 0.0, 1.0 / l, 0.0)
        p_rows.append(p)
        p_sum = p_sum + p
    o_cmp = _mm(jnp.concatenate(p_rows, axis=0), vcmp_ref[0])

    ss = lax.broadcasted_iota(jnp.int32, (N_SLC, N_CMP_PAD), 0)
    jj = lax.broadcasted_iota(jnp.int32, (N_SLC, N_CMP_PAD), 1)
    overlap_t = ((jj * CMP_STRIDE < ss * SLC_BLOCK + SLC_BLOCK)
                 & (jj * CMP_STRIDE + CMP_LEN > ss * SLC_BLOCK) & (jj < N_CMP_PAD - 1))
    overlap_t = jnp.where(overlap_t, 1.0, 0.0).astype(BF16)
    nt = lambda a, b_: lax.dot_general(a, b_, (((1,), (1,)), ((), ())), preferred_element_type=F32)
    p_hi, p_mid, p_lo = _split3(p_sum)
    imp = nt(overlap_t, p_hi) + nt(overlap_t, p_mid) + nt(overlap_t, p_lo)
    blk = lax.broadcasted_iota(jnp.int32, (N_SLC, tq), 0)
    cur = (t0 + lax.broadcasted_iota(jnp.int32, (N_SLC, tq), 1)) // SLC_BLOCK
    causal_blk = blk <= cur
    forced = (blk == 0) | (blk == cur) | (blk == cur - 1)
    imp = jnp.where(causal_blk & forced, FORCE, jnp.where(causal_blk, imp, -1.0))
    rank = jnp.zeros((N_SLC, tq), jnp.int32)
    for s2 in range(N_SLC):
        other = jnp.broadcast_to(imp[s2:s2 + 1, :], (N_SLC, tq))
        beats = (other > imp) | ((other == imp) & (blk > s2))
        rank = rank + jnp.where(beats, 1, 0)
    sel_t = jnp.where(rank < SLC_TOP_N, 1.0, 0.0).astype(BF16)
    er = lax.broadcasted_iota(jnp.int32, (N_SLC, SEQ), 0)
    ec = lax.broadcasted_iota(jnp.int32, (N_SLC, SEQ), 1)
    expand = jnp.where(ec // SLC_BLOCK == er, 1.0, 0.0).astype(BF16)
    key = lax.broadcasted_iota(jnp.int32, (tq, SEQ), 1)
    qall = t0 + lax.broadcasted_iota(jnp.int32, (tq, SEQ), 0)
    chosen = (_mm_tn(sel_t, expand) > 0.5) & (qall >= key)
    madd_ref[...] = jnp.where(chosen, 0.0, NEG_INF)

    neg = jnp.full((nh * tq, tq), NEG_INF, F32)
    zero = jnp.zeros((nh * tq, LANES), F32)

    def scores(k_tile, bias_idx, add):
        s_t = _mm_nt(qs, k_tile)
        return jnp.concatenate(
            [s_t[h * tq:(h + 1) * tq] + (btab_ref[h, bias_idx] + add) for h in range(nh)], axis=0)

    def normalise(acc):
        return acc[:, :hd] * (1.0 / _bcast_col(acc, hd, hd))

    n_groups = (i + SLC_GROUP) // SLC_GROUP

    def slc_pass1(j, m_run):
        for u in range(SLC_GROUP):
            kt = SLC_GROUP * j + u
            k0 = pl.multiple_of(kt * tq, tq)
            s_m = scores(ksb_ref[pl.ds(k0, tq), :], jnp.clip(i - kt, 0, 2), madd_ref[:, pl.ds(k0, tq)])
            s_ref[:, pl.ds(k0, tq)] = s_m
            m_run = jnp.maximum(m_run, s_m)
        return m_run

    m_slc = jnp.broadcast_to(
        jnp.max(lax.fori_loop(0, n_groups, slc_pass1, neg), axis=-1, keepdims=True), (nh * tq, tq))

    def slc_pass2(j, acc):
        for u in range(SLC_GROUP):
            k0 = pl.multiple_of((SLC_GROUP * j + u) * tq, tq)
            p = jnp.exp(s_ref[:, pl.ds(k0, tq)] - m_slc).astype(BF16)
            acc = acc + jnp.dot(p, vsa_ref[pl.ds(k0, tq), :], preferred_element_type=F32)
        return acc

    o_slc = normalise(lax.fori_loop(0, n_groups, slc_pass2, zero))

    kcol = lax.broadcasted_iota(jnp.int32, (tq, tq), 1)
    qrow = t0 + lax.broadcasted_iota(jnp.int32, (tq, tq), 0)
    n_win = WINDOW // tq + 1
    starts = []
    m_run = neg
    for j in range(n_win):
        d = n_win - 1 - j
        k0 = pl.multiple_of(jnp.maximum(i - d, 0) * tq, tq)
        starts.append(k0)
        dist = qrow - (k0 + kcol)
        ok = (dist >= 0) & (dist < WINDOW) & (i >= d)
        s_m = scores(kwb_ref[pl.ds(k0, tq), :], min(d, 2), jnp.where(ok, 0.0, NEG_INF))
        s_ref[:, j * tq:(j + 1) * tq] = s_m
        m_run = jnp.maximum(m_run, s_m)
    m_win = jnp.broadcast_to(jnp.max(m_run, axis=-1, keepdims=True), (nh * tq, tq))
    acc = zero
    for j in range(n_win):
        p = jnp.exp(s_ref[:, j * tq:(j + 1) * tq] - m_win).astype(BF16)
        acc = acc + jnp.dot(p, vwa_ref[pl.ds(starts[j], tq), :], preferred_element_type=F32)
    o_win = normalise(acc)

    gates = _sigmoid(gate_ref[0])
    outs = []
    for h in range(nh):
        hs = slice(h * tq, (h + 1) * tq)
        g0 = 2 * DN_HEADS + 3 * h
        outs.append(_bcast_col(gates, g0, hd) * o_cmp[hs]
                    + _bcast_col(gates, g0 + 1, hd) * o_slc[hs]
                    + _bcast_col(gates, g0 + 2, hd) * o_win[hs])
    o_ref[...] = jnp.concatenate(outs, axis=1)


def _nsa_attn(q, qnw, kcmp, vcmp, ks, kv6, kw, small, bias_cmp, btab):
    b = kv6.shape[1]
    tq = Q_TILE
    nq = SEQ // tq
    nh, hd = NSA_HEADS, NSA_HEAD_DIM
    full = lambda shape: pl.BlockSpec(shape, lambda bi, i: (0,) * len(shape))
    per_b = lambda r: pl.BlockSpec((1, r, hd), lambda bi, i: (bi, 0, 0))
    piece = lambda j: pl.BlockSpec((None, None, SEQ, hd), lambda bi, i: (j, bi, 0, 0))
    heads = pl.BlockSpec((nh, tq, hd), lambda bi, i: (0, bi * nq + i, 0))
    return pl.pallas_call(
        _nsa_attn_kernel,
        out_shape=jax.ShapeDtypeStruct((b * SEQ, NSA_WIDTH), F32),
        grid=(b, nq),
        in_specs=[
            heads,
            full((1, hd)),
            per_b(N_CMP_PAD), per_b(N_CMP_PAD), per_b(SEQ), piece(3), per_b(SEQ), piece(5),
            pl.BlockSpec((1, tq, LANES), lambda bi, i: (bi, i, 0)),
            pl.BlockSpec((nh, tq, N_CMP_PAD), lambda bi, i: (0, i, 0)),
            full((nh, 3, tq, tq)),
        ],
        out_specs=pl.BlockSpec((tq, NSA_WIDTH), lambda bi, i: (bi * nq + i, 0)),
        scratch_shapes=[pltpu.VMEM((tq, SEQ), F32), pltpu.VMEM((nh * tq, SEQ), F32),
                        pltpu.VMEM((SEQ, hd), BF16), pltpu.VMEM((SEQ, hd), BF16),
                        pltpu.VMEM((SEQ, 2 * hd), BF16), pltpu.VMEM((SEQ, 2 * hd), BF16)],
        compiler_params=_params(("parallel", "arbitrary")),
        name="nsa_attn",
    )(q, qnw, kcmp, vcmp, ks, kv6, kw, kv6, small, bias_cmp, btab)


def _out_router_kernel(ya_ref, yb_ref, yc_ref, x_ref, wo_ref, fnw_ref, rw_ref, rb_ref,
                       xo_ref, h_ref, idx_ref, wt_ref, hist_ref):
    wa = DN_WIDTH
    wb = wa + NSA_WIDTH
    y = (jnp.dot(ya_ref[...].astype(BF16), wo_ref[0:wa, :], preferred_element_type=F32)
         + jnp.dot(yb_ref[...].astype(BF16), wo_ref[wa:wb, :], preferred_element_type=F32)
         + jnp.dot(yc_ref[...].astype(BF16), wo_ref[wb:, :], preferred_element_type=F32))
    xn = x_ref[...] + y
    xo_ref[...] = xn
    h = xn * lax.rsqrt(jnp.mean(xn * xn, axis=-1, keepdims=True) + EPS) * fnw_ref[...]
    _store_tile_rows(h_ref, h, h.shape[0])
    h_hi, h_lo = _hi_lo(h)
    r_hi = jnp.dot(h_hi, rw_ref[...], preferred_element_type=F32)
    r_lo = jnp.dot(h_lo, rw_ref[...], preferred_element_type=F32)
    logits = ((r_hi[:, :LANES] + r_hi[:, LANES:]) + (r_lo[:, :LANES] + r_lo[:, LANES:])) + rb_ref[...]
    lane = lax.broadcasted_iota(jnp.int32, logits.shape, 1)
    vals, idxs = [], []
    for _ in range(TOP_K):
        m = jnp.max(logits, axis=-1, keepdims=True)
        ix = jnp.min(jnp.where(logits == m, lane, LANES), axis=-1, keepdims=True)
        vals.append(m)
        idxs.append(ix)
        logits = jnp.where(lane == ix, -jnp.inf, logits)
    es = [jnp.exp(v - vals[0]) for v in vals]
    inv = 1.0 / (es[0] + es[1] + es[2] + es[3])
    idx_out = jnp.zeros(lane.shape, jnp.int32)
    wt_out = jnp.zeros(lane.shape, F32)
    chosen = jnp.zeros(lane.shape, F32)
    for k in range(TOP_K):
        idx_out = jnp.where(lane == k, idxs[k], idx_out)
        wt_out = jnp.where(lane == k, es[k] * inv, wt_out)
        chosen = chosen + jnp.where(lane == idxs[k], 1.0, 0.0)
    idx_ref[...] = idx_out
    wt_ref[...] = wt_out
    hist_ref[0] = jnp.broadcast_to(jnp.sum(chosen, axis=0, keepdims=True), (8, LANES))


def _out_router(ya, yb, yc, x2, w_out, fnw, rw, rb):
    n = x2.shape[0]
    tm = ROW_TILE
    row = lambda w: pl.BlockSpec((tm, w), lambda i: (i, 0))
    full = lambda shape: pl.BlockSpec(shape, lambda i: (0,) * len(shape))
    return pl.pallas_call(
        _out_router_kernel,
        out_shape=[jax.ShapeDtypeStruct((n, D_MODEL), F32), jax.ShapeDtypeStruct((n * SUBLANES, LANES), F32),
                   jax.ShapeDtypeStruct((n, LANES), jnp.int32), jax.ShapeDtypeStruct((n, LANES), F32),
                   jax.ShapeDtypeStruct((n // tm, 8, LANES), F32)],
        grid=(n // tm,),
        in_specs=[row(DN_WIDTH), row(NSA_WIDTH), row(CONV_WIDTH), row(D_MODEL),
                  full((D_MODEL, D_MODEL)), full((1, D_MODEL)), full((D_MODEL, 2 * LANES)), full((1, LANES))],
        out_specs=[row(D_MODEL), pl.BlockSpec((tm * SUBLANES, LANES), lambda i: (i, 0)), row(LANES), row(LANES),
                   pl.BlockSpec((1, 8, LANES), lambda i: (i, 0, 0))],
        compiler_params=_params(("parallel",)),
        name="out_router",
    )(ya, yb, yc, x2, w_out, fnw, rw, rb)


def _slots_kernel(idx_ref, base_ref, slot_ref):
    tm = ROW_TILE
    idx = idx_ref[...]
    lane = lax.broadcasted_iota(jnp.int32, (tm, LANES), 1)
    onehots = [jnp.where(lane == _bcast_col(idx, k, LANES), 1.0, 0.0) for k in range(TOP_K)]
    cnt = (onehots[0] + onehots[1]) + (onehots[2] + onehots[3])
    r = lax.broadcasted_iota(jnp.int32, (tm, tm), 0)
    c = lax.broadcasted_iota(jnp.int32, (tm, tm), 1)
    earlier = jnp.where(r > c, 1.0, 0.0).astype(BF16)
    rank = jnp.dot(earlier, cnt.astype(BF16), preferred_element_type=F32) + base_ref[0, 0:1, :]
    out = jnp.zeros((tm, LANES), F32)
    for k in range(TOP_K):
        out = jnp.where(lane == k, jnp.sum(rank * onehots[k], axis=-1, keepdims=True), out)
    slot_ref[...] = (out * SUBLANES).astype(jnp.int32)


def _slots(idx, base3):
    n = idx.shape[0]
    tm = ROW_TILE
    return pl.pallas_call(
        _slots_kernel,
        out_shape=jax.ShapeDtypeStruct((n, LANES), jnp.int32),
        grid=(n // tm,),
        in_specs=[pl.BlockSpec((tm, LANES), lambda i: (i, 0)),
                  pl.BlockSpec((1, 8, LANES), lambda i: (i, 0, 0))],
        out_specs=pl.BlockSpec((tm, LANES), lambda i: (i, 0)),
        compiler_params=_params(("parallel",)),
        name="moe_slots",
    )(idx, base3)


def _dispatch_kernel(slot_ref, zoff_ref, nv_ref, h_ref, xs_hbm, zbuf, sem, zsem):
    i = pl.program_id(0)
    tm = ROW_TILE
    base = i * tm

    @pl.when(i == 0)
    def _():
        zbuf[...] = jnp.zeros(zbuf.shape, F32)
        blk = MOE_TILE * SUBLANES
        n_blocks = xs_hbm.shape[0] // blk

        def fill(row0):
            return pltpu.make_async_copy(
                zbuf, xs_hbm.at[pl.ds(pl.multiple_of(row0 * SUBLANES, blk), blk)], zsem)

        def fill_region_end(e, c):
            fill(zoff_ref[e]).start()
            return c

        def fill_tail(b, c):
            fill(b * MOE_TILE).start()
            return c

        def wait_fill(b, c):
            fill(0).wait()
            return c

        lax.fori_loop(0, N_EXPERTS, fill_region_end, 0)
        lax.fori_loop(nv_ref[0], n_blocks, fill_tail, 0)
        lax.fori_loop(0, N_EXPERTS + n_blocks - nv_ref[0], wait_fill, 0)

    def body(r, c):
        src = h_ref.at[pl.ds(pl.multiple_of(r * SUBLANES, SUBLANES), SUBLANES)]
        for k in range(TOP_K):
            row0 = pl.multiple_of(slot_ref[(base + r) * TOP_K + k], SUBLANES)
            pltpu.make_async_copy(src, xs_hbm.at[pl.ds(row0, SUBLANES)], sem).start(priority=k % 2)
        return c

    lax.fori_loop(0, tm, body, 0, unroll=8)
    done = xs_hbm.at[pl.ds(0, TOP_K * tm * SUBLANES)]
    pltpu.make_async_copy(done, done, sem).wait()


def _dispatch(slot_flat, zero_off, n_valid, h, n_slots):
    n = h.shape[0] // SUBLANES
    tm = ROW_TILE
    return pl.pallas_call(
        _dispatch_kernel,
        out_shape=jax.ShapeDtypeStruct((n_slots * SUBLANES, LANES), F32),
        grid_spec=pltpu.PrefetchScalarGridSpec(
            num_scalar_prefetch=3,
            grid=(n // tm,),
            in_specs=[pl.BlockSpec((tm * SUBLANES, LANES), lambda i, s, z, nv: (i, 0))],
            out_specs=pl.BlockSpec(memory_space=pl.ANY),
            scratch_shapes=[pltpu.VMEM((MOE_TILE * SUBLANES, LANES), F32),
                            pltpu.SemaphoreType.DMA, pltpu.SemaphoreType.DMA],
        ),
        compiler_params=_params(("arbitrary",)),
        name="moe_dispatch",
    )(slot_flat, zero_off, n_valid, h)


def _expert_kernel(be_ref, nv_ref, xs_ref, wgu_ref, bgu_ref, wd_ref, bd_ref, o_ref, wgu_bf, wd_bf):
    i = pl.program_id(0)
    new_expert = (i == 0) | (be_ref[i] != be_ref[jnp.maximum(i - 1, 0)])

    @pl.when((i < nv_ref[0]) & new_expert)
    def _():
        step = 512
        for c0 in range(0, 2 * D_FF, step):
            wgu_bf[:, c0:c0 + step] = wgu_ref[:, c0:c0 + step].astype(BF16)
        for c0 in range(0, D_MODEL, step):
            wd_bf[:, c0:c0 + step] = wd_ref[:, c0:c0 + step].astype(BF16)

    @pl.when(i < nv_ref[0])
    def _():
        xb = _load_tile_rows(xs_ref, MOE_TILE).astype(BF16)
        gu = jnp.dot(xb, wgu_bf[...], preferred_element_type=F32) + bgu_ref[...]
        gate = jnp.minimum(gu[:, :D_FF], SWIGLU_LIMIT)
        up = jnp.clip(gu[:, D_FF:], -SWIGLU_LIMIT, SWIGLU_LIMIT)
        act = (up + 1.0) * gate * _sigmoid(SWIGLU_ALPHA * gate)
        y = jnp.dot(act.astype(BF16), wd_bf[...], preferred_element_type=F32) + bd_ref[...]
        _store_tile_rows(o_ref, y, MOE_TILE)

    @pl.when(i >= nv_ref[0])
    def _():
        o_ref[...] = jnp.zeros(o_ref.shape, F32)


def _expert_ffn(layer, block_expert, n_valid, xs, wgu, bgu, wd, bd):
    n_slots = xs.shape[0] // SUBLANES
    tm = MOE_TILE
    return pl.pallas_call(
        _expert_kernel,
        out_shape=jax.ShapeDtypeStruct(xs.shape, F32),
        grid_spec=pltpu.PrefetchScalarGridSpec(
            num_scalar_prefetch=2,
            grid=(n_slots // tm,),
            in_specs=[
                pl.BlockSpec((tm * SUBLANES, LANES), lambda i, be, nv: (i, 0)),
                pl.BlockSpec((None, None, D_MODEL, 2 * D_FF), lambda i, be, nv: (layer, be[i], 0, 0)),
                pl.BlockSpec((None, None, 1, 2 * D_FF), lambda i, be, nv: (layer, be[i], 0, 0)),
                pl.BlockSpec((None, None, D_FF, D_MODEL), lambda i, be, nv: (layer, be[i], 0, 0)),
                pl.BlockSpec((None, None, 1, D_MODEL), lambda i, be, nv: (layer, be[i], 0, 0)),
            ],
            out_specs=pl.BlockSpec((tm * SUBLANES, LANES), lambda i, be, nv: (i, 0)),
            scratch_shapes=[pltpu.VMEM((D_MODEL, 2 * D_FF), BF16), pltpu.VMEM((D_FF, D_MODEL), BF16)],
        ),
        compiler_params=pltpu.CompilerParams(dimension_semantics=("arbitrary",),
                                             vmem_limit_bytes=EXPERT_VMEM_LIMIT_BYTES),
        name="expert_ffn",
    )(block_expert, n_valid, xs, wgu, bgu, wd, bd)


def _combine_kernel(slot_ref, ys_hbm, x_ref, wt_ref, o_ref, buf, acc_ref, sems):
    i = pl.program_id(0)
    tc = COMBINE_TILE
    nsteps = pl.num_programs(0)
    cur = lax.rem(i, 2)

    def issue(step, par):
        base = step * tc

        def body(r, c):
            dst0 = pl.multiple_of(r * SUBLANES, SUBLANES)
            for k in range(TOP_K):
                row0 = pl.multiple_of(slot_ref[(base + r) * TOP_K + k], SUBLANES)
                pltpu.make_async_copy(ys_hbm.at[pl.ds(row0, SUBLANES)],
                                      buf.at[par, k, pl.ds(dst0, SUBLANES)],
                                      sems.at[par]).start(priority=k % 2)
            return c

        lax.fori_loop(0, tc, body, 0, unroll=8)

    @pl.when(i == 0)
    def _():
        issue(0, 0)

    @pl.when(i + 1 < nsteps)
    def _():
        issue(i + 1, 1 - cur)

    pltpu.make_async_copy(buf.at[cur], buf.at[cur], sems.at[cur]).wait()
    wt = wt_ref[...]
    acc = None
    for k in range(TOP_K):
        w_tiles = jnp.broadcast_to(_bcast_col(wt, k, LANES)[:, None, :], (tc, SUBLANES, LANES))
        term = w_tiles.reshape(tc * SUBLANES, LANES) * buf[cur, k]
        acc = term if acc is None else acc + term
    acc_ref[...] = acc
    for s in range(ROW_CHUNKS):
        cs = slice(s * LANES, (s + 1) * LANES)
        o_ref[:, cs] = x_ref[:, cs] + acc_ref[pl.ds(s, tc, stride=SUBLANES), :]


def _combine(slot_flat, ys, x2, wt):
    n = x2.shape[0]
    tc = COMBINE_TILE
    return pl.pallas_call(
        _combine_kernel,
        out_shape=jax.ShapeDtypeStruct((n, D_MODEL), F32),
        grid_spec=pltpu.PrefetchScalarGridSpec(
            num_scalar_prefetch=1,
            grid=(n // tc,),
            in_specs=[pl.BlockSpec(memory_space=pl.ANY),
                      pl.BlockSpec((tc, D_MODEL), lambda i, s: (i, 0)),
                      pl.BlockSpec((tc, LANES), lambda i, s: (i, 0))],
            out_specs=pl.BlockSpec((tc, D_MODEL), lambda i, s: (i, 0)),
            scratch_shapes=[pltpu.VMEM((2, TOP_K, tc * SUBLANES, LANES), F32),
                            pltpu.VMEM((tc * SUBLANES, LANES), F32), pltpu.SemaphoreType.DMA((2,))],
        ),
        compiler_params=_params(("arbitrary",)),
        name="moe_combine",
    )(slot_flat, ys, x2, wt)


def _moe(layer, h, x2, idx, wt, hist3, wgu, bgu, wd, bd):
    n = x2.shape[0]
    tm = MOE_TILE
    n_slots = n * TOP_K + N_EXPERTS * tm
    n_blocks = n_slots // tm
    hist = hist3[:, 0, :N_EXPERTS]
    counts = jnp.sum(hist, axis=0).astype(jnp.int32)
    padded = (counts + tm - 1) // tm * tm
    pad_end = jnp.cumsum(padded)
    pad_start = pad_end - padded
    tile_base = pad_start[None, :].astype(F32) + (jnp.cumsum(hist, axis=0) - hist)
    base3 = jnp.broadcast_to(jnp.pad(tile_base, ((0, 0), (0, LANES - N_EXPERTS)))[:, None, :],
                             (hist.shape[0], 8, LANES))
    blk0 = jnp.arange(n_blocks) * tm
    block_expert = jnp.minimum(jnp.sum(blk0[:, None] >= pad_end[None, :], axis=1), N_EXPERTS - 1).astype(jnp.int32)
    n_valid = (pad_end[-1:] // tm).astype(jnp.int32)
    zero_off = jnp.maximum(pad_end - tm, 0).astype(jnp.int32)

    slot_flat = _slots(idx, base3)[:, :TOP_K].reshape(-1)
    xs = _dispatch(slot_flat, zero_off, n_valid, h, n_slots)
    ys = _expert_ffn(layer, block_expert, n_valid, xs, wgu, bgu, wd, bd)
    return _combine(slot_flat, ys, x2, wt)


def _t5_bucket(dist):
    n = jnp.maximum(dist, 0)
    max_exact = REL_BUCKETS // 2
    nf = jnp.maximum(n, 1).astype(F32)
    large = max_exact + (jnp.log(nf / max_exact) / math.log(REL_MAX_DIST / max_exact)
                         * (REL_BUCKETS - max_exact)).astype(jnp.int32)
    large = jnp.minimum(large, REL_BUCKETS - 1)
    return jnp.where(n < max_exact, n, large)


def _bias_tables(rel_bias):
    rel_bias = rel_bias.astype(F32)
    tq = Q_TILE

    def lookup(dist):
        bucket = _t5_bucket(dist)
        out = jnp.zeros((NSA_HEADS,) + dist.shape, F32)
        for bk in range(REL_BUCKETS):
            out = jnp.where(bucket[None] == bk, rel_bias[bk].reshape((NSA_HEADS,) + (1,) * dist.ndim), out)
        return out

    t_pos = jnp.arange(SEQ)
    cmp_end = jnp.arange(N_CMP_PAD) * CMP_STRIDE + CMP_LEN - 1
    bias_cmp = lookup(t_pos[:, None] - cmp_end[None, :])
    rr = jnp.arange(tq)[:, None] - jnp.arange(tq)[None, :]
    btab = lookup(jnp.stack([rr, rr + tq, rr + 2 * tq]))
    return bias_cmp, btab


def _layer(layer, x2, b, p, experts, bias_cmp, btab):
    n = x2.shape[0]
    w = p['w_in']
    o_a, o_b, o_q, o_kv, o_g, o_u = 2048, 2052, 2056, 2312, 2696, 2708
    w_small = jnp.concatenate([w[:, o_a:o_q], w[:, o_g:o_u],
                               jnp.zeros((D_MODEL, LANES - 8 - 3 * NSA_HEADS), F32)], axis=1)
    w_cat = jnp.concatenate([w[:, :o_a], w[:, o_q:o_kv], w[:, o_kv:o_g], w[:, o_u:], w_small],
                            axis=1).astype(BF16)
    w_abt = jnp.concatenate([w[:, o_a:o_q].T, jnp.zeros((8, D_MODEL), F32)], axis=0).astype(BF16)
    pcol = jnp.zeros((8, LANES), F32).at[0, :DN_HEADS].set(p['dn_a_log']).at[1, :DN_HEADS].set(p['dn_dt_bias'])
    prow = jnp.zeros((16, LANES), F32).at[:DN_HEADS, 0].set(p['dn_a_log']).at[:DN_HEADS, 1].set(p['dn_dt_bias'])
    dq, dk, dv, gb, grow, z, nq, nkv, small, y_c = _front(
        x2, p['attn_norm_w'][None, :], w_cat, w_abt, p['dn_conv_w'], pcol, prow,
        p['conv_dw_w'], p['conv_dw_b'][None, :], p['conv_ln_w'][None, :], p['conv_ln_b'][None, :])
    small3 = small.reshape(b, SEQ, LANES)

    seq = lambda a: a.reshape(b, SEQ, a.shape[-1])
    y_a = _delta_rule(seq(dq), seq(dk), seq(dv), seq(gb), grow, seq(z), p['dn_norm_w'][None, :])

    kv6 = nkv.reshape(6, b, SEQ, NSA_HEAD_DIM)
    kcmp, vcmp, ksn, kwn = _nsa_prep(kv6, p['nsa_cmp_pos'], p['nsa_cmp_w1'].astype(BF16),
                                     p['nsa_cmp_w2'].astype(BF16), p['nsa_k_norm_w'])
    y_b = _nsa_attn(nq, p['nsa_q_norm_w'][None, :], kcmp, vcmp, ksn, kv6, kwn, small3, bias_cmp, btab)

    rw = jnp.concatenate(_hi_lo(jnp.pad(p['router_w'], ((0, 0), (0, LANES - N_EXPERTS)))), axis=1)
    rb =jnp.concatenate([p['router_b'], jnp.full((LANES - N_EXPERTS,), NEG_INF, F32)])[None, :]
    x_new, h, idx, wt, hist3 = _out_router(y_a.reshape(n, DN_WIDTH), y_b, y_c.reshape(n, CONV_WIDTH), x2,
                                           p['w_out'].astype(BF16), p['ffn_norm_w'][None, :], rw, rb)
    return _moe(layer, h, x_new, idx, wt, hist3, *experts)


def kernel(x, attn_norm_w, w_in, dn_conv_w, dn_a_log, dn_dt_bias, dn_norm_w, nsa_q_norm_w, nsa_k_norm_w, nsa_cmp_pos, nsa_cmp_w1, nsa_cmp_w2, conv_dw_w, conv_dw_b, conv_ln_w, conv_ln_b, w_out, ffn_norm_w, router_w, router_b, w_gate_up, b_gate_up, w_down, b_down, rel_bias):
    b, t, d = x.shape
    assert (t, d) == (SEQ, D_MODEL)
    stacked = dict(attn_norm_w=attn_norm_w, w_in=w_in, dn_conv_w=dn_conv_w, dn_a_log=dn_a_log,
                   dn_dt_bias=dn_dt_bias, dn_norm_w=dn_norm_w, nsa_q_norm_w=nsa_q_norm_w,
                   nsa_k_norm_w=nsa_k_norm_w, nsa_cmp_pos=nsa_cmp_pos, nsa_cmp_w1=nsa_cmp_w1,
                   nsa_cmp_w2=nsa_cmp_w2, conv_dw_w=conv_dw_w, conv_dw_b=conv_dw_b,
                   conv_ln_w=conv_ln_w, conv_ln_b=conv_ln_b, w_out=w_out, ffn_norm_w=ffn_norm_w,
                   router_w=router_w, router_b=router_b)
    experts = (w_gate_up, b_gate_up[:, :, None, :], w_down, b_down[:, :, None, :])
    bias_cmp, btab = _bias_tables(rel_bias)
    x2 = x.reshape(b * t, d)
    for l in range(w_in.shape[0]):
        x2 = _layer(l, x2, b, {k: v[l] for k, v in stacked.items()}, experts, bias_cmp, btab)
    return x2.reshape(b, t, d)
```

```python
import functools
import math

import numpy as np
import jax
import jax.numpy as jnp
from jax import lax
from jax.experimental import pallas as pl
from jax.experimental.pallas import tpu as pltpu

F32 = jnp.float32
BF16 = jnp.bfloat16

D_MODEL = 1024
SEQ = 2048
DN_HEADS = 4
DN_HEAD_DIM = 128
DN_WIDTH = DN_HEADS * DN_HEAD_DIM
DN_CONV = 4
DN_CHUNK = 64
NSA_HEADS = 4
NSA_HEAD_DIM = 64
NSA_WIDTH = NSA_HEADS * NSA_HEAD_DIM
CMP_LEN = 32
CMP_STRIDE = 16
CMP_HIDDEN = 2 * NSA_HEAD_DIM
SLC_BLOCK = 64
SLC_TOP_N = 16
WINDOW = 512
CONV_WIDTH = 256
CONV_KERNEL = 31
REL_BUCKETS = 32
REL_MAX_DIST = 128
N_EXPERTS = 32
TOP_K = 4
D_FF = D_MODEL
SWIGLU_LIMIT = 7.0
SWIGLU_ALPHA = 1.702
EPS = 1e-6
NEG_INF = -1e30
FORCE = 1e4

LANES = 128
VMEM_LIMIT_BYTES = 48 * 1024 * 1024
EXPERT_VMEM_LIMIT_BYTES = 56 * 1024 * 1024

ROW_TILE = 512
SEQ_TILE = 256
Q_TILE = 128
SLC_GROUP = 4
MOE_TILE = 512
COMBINE_TILE = 256

N_CMP_PAD = 128
N_SLC = SEQ // SLC_BLOCK


def _params(sem=None):
    return pltpu.CompilerParams(dimension_semantics=sem, vmem_limit_bytes=VMEM_LIMIT_BYTES)


def _mm(a, b):
    return jnp.dot(a.astype(BF16), b.astype(BF16), preferred_element_type=F32)


def _mm_nt(a, b):
    return lax.dot_general(a.astype(BF16), b.astype(BF16), (((1,), (1,)), ((), ())),
                           preferred_element_type=F32)


def _mm_tn(a, b):
    return lax.dot_general(a.astype(BF16), b.astype(BF16), (((0,), (0,)), ((), ())),
                           preferred_element_type=F32)


def _split3(x):
    hi = x.astype(BF16)
    r1 = x - hi.astype(F32)
    mid = r1.astype(BF16)
    lo = (r1 - mid.astype(F32)).astype(BF16)
    return hi, mid, lo


def _dot01_right(x, m01):
    hi, mid, lo = _split3(x)
    d = lambda p: jnp.dot(p, m01, preferred_element_type=F32)
    return d(hi) + d(mid) + d(lo)


def _dot01_left(m01, x):
    hi, mid, lo = _split3(x)
    d = lambda p: jnp.dot(m01, p, preferred_element_type=F32)
    return d(hi) + d(mid) + d(lo)


def _hi_lo(x):
    hi = x.astype(BF16)
    return hi, (x - hi.astype(F32)).astype(BF16)


def _sigmoid(x):
    return 1.0 / (1.0 + jnp.exp(-x))


def _silu(x):
    return x * _sigmoid(x)


def _softplus(x):
    return jnp.maximum(x, 0.0) + jnp.log(1.0 + jnp.exp(-jnp.abs(x)))


SUBLANES = 8
ROW_CHUNKS = D_MODEL // LANES
assert ROW_CHUNKS == SUBLANES


def _store_tile_rows(ref, value, rows):
    for s in range(ROW_CHUNKS):
        ref[pl.ds(s, rows, stride=SUBLANES), :] = value[:, s * LANES:(s + 1) * LANES]


def _load_tile_rows(ref, rows):
    return jnp.concatenate([ref[pl.ds(s, rows, stride=SUBLANES), :] for s in range(ROW_CHUNKS)], axis=1)


def _bcast_col(x, j, width):
    return jnp.broadcast_to(x[:, j:j + 1], (x.shape[0], width))


IN_SEGS = (3 * DN_WIDTH, DN_WIDTH, NSA_WIDTH, 6 * NSA_HEAD_DIM, 2 * CONV_WIDTH, LANES)
IN_COLS = sum(IN_SEGS)


def _delta_prepare(q_ref, k_ref, v_ref, gb_ref, grow_ref, wy):
    u_ref, w_ref, attn_ref, qg_ref, kk_ref, gc_ref = wy
    ts = SEQ_TILE
    ch = DN_CHUNK
    hd = DN_HEAD_DIM
    r = lax.broadcasted_iota(jnp.int32, (ts, ts), 0)
    c = lax.broadcasted_iota(jnp.int32, (ts, ts), 1)
    same_chunk = (r // ch) == (c // ch)
    tril = same_chunk & (r >= c)
    strict = same_chunk & (r > c)
    same16 = (r // 16) == (c // 16)
    eye = jnp.where(r == c, 1.0, 0.0).astype(F32)
    m_col = jnp.where(tril, 1.0, 0.0).astype(BF16)
    m_row = jnp.where(same_chunk & (r <= c), 1.0, 0.0).astype(BF16)

    gb = gb_ref[0]
    gc_col = _dot01_left(m_col, gb)
    gc_row = _dot01_right(grow_ref[...], m_row)

    heads = range(DN_HEADS)
    hsl = [slice(h * hd, (h + 1) * hd) for h in heads]
    kh = [k_ref[0, :, hsl[h]] for h in heads]
    gcb = [_bcast_col(gc_col, h, ts) for h in heads]
    decay = [jnp.where(tril, jnp.exp(jnp.where(
        tril, gcb[h] - jnp.broadcast_to(gc_row[h:h + 1, :], (ts, ts)), 0.0)), 0.0) for h in heads]
    beta = [_bcast_col(gb, DN_HEADS + h, hd) for h in heads]
    kb = [kh[h] * beta[h] for h in heads]
    yield
    a_mat = [jnp.where(strict, _mm_nt(kb[h], kh[h]) * decay[h], 0.0) for h in heads]
    yield
    d_mat = [jnp.where(same16, a_mat[h], 0.0) for h in heads]
    e_mat = [a_mat[h] - d_mat[h] for h in heads]
    d2 = [_mm(d_mat[h], d_mat[h]) for h in heads]
    yield
    d4 = [_mm(d2[h], d2[h]) for h in heads]
    yield
    t1 = [_mm(eye - d_mat[h], eye + d2[h]) for h in heads]
    yield
    d8 = [_mm(d4[h], d4[h]) for h in heads]
    yield
    t2 = [_mm(t1[h], eye + d4[h]) for h in heads]
    yield
    p_mat = [_mm(t2[h], eye + d8[h]) for h in heads]
    yield
    m_mat = [_mm(p_mat[h], e_mat[h]) for h in heads]
    yield
    m2 = [_mm(m_mat[h], m_mat[h]) for h in heads]
    yield
    t3 = [_mm(eye - m_mat[h], eye + m2[h]) for h in heads]
    yield
    t_mat = [_mm(t3[h], p_mat[h]) for h in heads]
    yield
    for h in heads:
        gc128 = gcb[h][:, :hd]
        expg = jnp.exp(gc128)
        qh = q_ref[0, :, hsl[h]]
        sol = _mm(t_mat[h], jnp.concatenate([v_ref[0, :, hsl[h]] * beta[h], kb[h] * expg], axis=1))
        u_ref[h] = sol[:, :hd]
        w_ref[h] = sol[:, hd:].astype(BF16)
        qg_ref[h] = (qh * expg).astype(BF16)
        kk_ref[h] = kh[h]
        gc_ref[h] = gc128
    yield
    for h in heads:
        attn_ref[h] = jnp.where(tril, _mm_nt(q_ref[0, :, hsl[h]], kh[h]) * decay[h], 0.0).astype(BF16)


def _delta_recur(wy, z_ref, nw_ref, o_ref, s_ref):
    u_ref, w_ref, attn_ref, qg_ref, kk_ref, gc_ref = wy
    ch = DN_CHUNK
    hd = DN_HEAD_DIM
    heads = range(DN_HEADS)
    state = [s_ref[h] for h in heads]
    outs = [[] for _ in heads]
    for ci in range(SEQ_TILE // ch):
        rs = slice(ci * ch, (ci + 1) * ch)
        v_new = [u_ref[h, rs, :] - _mm(w_ref[h, rs, :], state[h]) for h in heads]
        o_state = [_mm(qg_ref[h, rs, :], state[h]) for h in heads]
        yield
        for h in heads:
            gc = gc_ref[h, rs, :]
            g_last = gc[ch - 1:ch, :]
            k_dec = kk_ref[h, rs, :] * jnp.exp(g_last - gc)
            outs[h].append(o_state[h] + _mm(attn_ref[h, rs, ci * ch:(ci + 1) * ch], v_new[h]))
            state[h] = state[h] * jnp.exp(g_last) + _mm_tn(k_dec, v_new[h])
        yield

    for h in heads:
        s_ref[h] = state[h]
        hs = slice(h * hd, (h + 1) * hd)
        o = jnp.concatenate(outs[h], axis=0)
        o = o * lax.rsqrt(jnp.mean(o * o, axis=-1, keepdims=True) + EPS) * nw_ref[...]
        o_ref[0, :, hs] = o * _silu(z_ref[0, :, hs])


def _interleave(weighted):
    live = [gen for gen, _ in weighted]
    while live:
        for gen, n in weighted:
            for _ in range(n):
                if gen in live and next(gen, live) is live:
                    live.remove(gen)


DN_BATCH = 2


def _delta_kernel(q_ref, k_ref, v_ref, gb_ref, grow0_ref, grow1_ref, z_ref, nw_ref, o_ref, s_ref,
                  *wy_refs):
    t = pl.program_id(1)
    set_a, set_b = wy_refs[:6], wy_refs[6:]
    grow_refs = (grow0_ref, grow1_ref)

    @pl.when(t == 0)
    def _():
        s_ref[...] = jnp.zeros(s_ref.shape, F32)
        for ref in set_b:
            ref[...] = jnp.zeros(ref.shape, ref.dtype)

    def step(read_set, write_set):
        one = lambda ref, bb: ref.at[pl.ds(bb, 1)]
        work = []
        for bb in range(DN_BATCH):
            work.append((_delta_prepare(one(q_ref, bb), one(k_ref, bb), one(v_ref, bb), one(gb_ref, bb),
                                        grow_refs[bb], [ref.at[bb] for ref in write_set]), 2))
        for bb in range(DN_BATCH):
            work.append((_delta_recur([ref.at[bb] for ref in read_set], one(z_ref, bb), nw_ref,
                                      one(o_ref, bb), s_ref.at[bb]), 1))
        _interleave(work)

    @pl.when(lax.rem(t, 2) == 0)
    def _():
        step(set_b, set_a)

    @pl.when(lax.rem(t, 2) == 1)
    def _():
        step(set_a, set_b)


def _delta_rule(q, k, v, gb, grow, z, norm_w):
    b = q.shape[0]
    nb = DN_BATCH
    assert b % nb == 0
    ts = SEQ_TILE
    nt = SEQ // ts
    nxt = lambda w: pl.BlockSpec((nb, ts, w), lambda i, t: (i, jnp.minimum(t, nt - 1), 0))
    cur = lambda w: pl.BlockSpec((nb, ts, w), lambda i, t: (i, jnp.maximum(t - 1, 0), 0))
    grow_spec = lambda bb: pl.BlockSpec(
        (8, ts), lambda i, t: (0, (i * nb + bb) * nt + jnp.minimum(t, nt - 1)))
    hd = DN_HEAD_DIM
    wy_set = [pltpu.VMEM((nb, DN_HEADS, ts, hd), F32), pltpu.VMEM((nb, DN_HEADS, ts, hd), BF16),
              pltpu.VMEM((nb, DN_HEADS, ts, ts), BF16), pltpu.VMEM((nb, DN_HEADS, ts, hd), BF16),
              pltpu.VMEM((nb, DN_HEADS, ts, hd), F32), pltpu.VMEM((nb, DN_HEADS, ts, hd), F32)]
    return pl.pallas_call(
        _delta_kernel,
        out_shape=jax.ShapeDtypeStruct((b, SEQ, DN_WIDTH), F32),
        grid=(b // nb, nt + 1),
        in_specs=[
            nxt(DN_WIDTH), nxt(DN_WIDTH), nxt(DN_WIDTH), nxt(LANES),
            grow_spec(0), grow_spec(1),
            cur(DN_WIDTH),
            pl.BlockSpec((1, DN_HEAD_DIM), lambda i, t: (0, 0)),
        ],
        out_specs=cur(DN_WIDTH),
        scratch_shapes=[pltpu.VMEM((nb, DN_HEADS, hd, hd), F32)] + wy_set + wy_set,
        compiler_params=_params(("parallel", "arbitrary")),
        name="delta_rule",
    )(q, k, v, gb, grow, grow, z, norm_w)


CONV_HALO = 32


MIX_COLS = {'qkv': (0, 3 * DN_WIDTH), 'z': (3 * DN_WIDTH, 4 * DN_WIDTH)}
_off = 4 * DN_WIDTH
for _name, _w in (('nq', NSA_WIDTH), ('nkv', 6 * NSA_HEAD_DIM), ('cu', 2 * CONV_WIDTH), ('small', LANES)):
    MIX_COLS[_name] = (_off, _off + _w)
    _off += _w
assert _off == IN_COLS
DN_HALO = 8
MXU_CHUNK = 256


def _front_kernel(x_ref, nw_ref, w_ref, wabt_ref, cw_ref, pcol_ref, prow_ref,
                  dww_ref, dwb_ref, lnw_ref, lnb_ref,
                  q_ref, k_ref, v_ref, gb_ref, grow_ref, z_ref, nq_ref, nkv_ref, small_ref, yc_ref,
                  dn_buf, cf_buf, cf_shift):
    i = pl.program_id(0)
    tm = ROW_TILE
    hd = NSA_HEAD_DIM

    @pl.when(lax.rem(i, SEQ // tm) == 0)
    def _():
        dn_buf[0:DN_HALO, :] = jnp.zeros((DN_HALO, 3 * DN_WIDTH), F32)
        cf_buf[0:CONV_HALO, :] = jnp.zeros((CONV_HALO, CONV_WIDTH), F32)

    xf = x_ref[...]
    hb = (xf * lax.rsqrt(jnp.mean(xf * xf, axis=-1, keepdims=True) + EPS) * nw_ref[...]).astype(BF16)

    def proj(name, lo=0, hi=None):
        c0, c1 = MIX_COLS[name]
        hi = c1 - c0 if hi is None else hi
        return jnp.dot(hb, w_ref[:, c0 + lo:c0 + hi], preferred_element_type=F32)

    def mxu_items():
        cu = proj('cu')
        cf_buf[CONV_HALO:CONV_HALO + tm, :] = cu[:, :CONV_WIDTH] * _sigmoid(cu[:, CONV_WIDTH:])
        yield
        sm = proj('small')
        small_ref[...] = sm
        lane = lax.broadcasted_iota(jnp.int32, sm.shape, 1)
        g_col = -jnp.exp(pcol_ref[0:1, :]) * _softplus(sm + pcol_ref[1:2, :])
        gb_ref[...] = jnp.where(lane < DN_HEADS, g_col, _sigmoid(sm))
        a_t = lax.dot_general(wabt_ref[...], hb, (((1,), (1,)), ((), ())), preferred_element_type=F32)
        g_row = -jnp.exp(prow_ref[:, 0:1]) * _softplus(a_t + prow_ref[:, 1:2])
        grow_ref[...] = g_row[0:8, :]
        yield
        for c0 in range(0, 3 * DN_WIDTH, MXU_CHUNK):
            dn_buf[DN_HALO:DN_HALO + tm, c0:c0 + MXU_CHUNK] = proj('qkv', c0, c0 + MXU_CHUNK)
            yield
        for c0 in range(0, DN_WIDTH, MXU_CHUNK):
            z_ref[:, c0:c0 + MXU_CHUNK] = proj('z', c0, c0 + MXU_CHUNK)
            yield
        res = proj('nq')
        for j in range(NSA_HEADS):
            nq_ref[j] = res[:, j * hd:(j + 1) * hd]
        yield
        res = proj('nkv')
        for j in range(6):
            nkv_ref[j] = res[:, j * hd:(j + 1) * hd]

    def conformer_items():
        span = tm + CONV_HALO - SUBLANES
        for b in range(1, SUBLANES):
            cf_shift[b] = cf_buf[b:b + span, :]
        yield
        base = CONV_HALO - (CONV_KERNEL - 1)
        rows = 64
        for rc in range(tm // rows):
            parts = []
            for cc in range(CONV_WIDTH // LANES):
                cs = slice(cc * LANES, (cc + 1) * LANES)
                acc = None
                for j in range(CONV_KERNEL):
                    start = base + rc * rows + j
                    b = start % SUBLANES
                    a0 = start - b
                    win = cf_buf[a0:a0 + rows, cs] if b == 0 else cf_shift[b, a0:a0 + rows, cs]
                    term = dww_ref[j:j + 1, cs] * win
                    acc = term if acc is None else acc + term
                parts.append(acc)
            h = jnp.concatenate(parts, axis=1) + dwb_ref[...]
            mu = jnp.mean(h, axis=-1, keepdims=True)
            var = jnp.mean(jnp.square(h - mu), axis=-1, keepdims=True)
            hn = (h - mu) * lax.rsqrt(var + EPS) * lnw_ref[...] + lnb_ref[...]
            yc_ref[rc * rows:(rc + 1) * rows, :] = _silu(hn)
            yield
        cf_buf[0:CONV_HALO, :] = cf_buf[tm:tm + CONV_HALO, :]

    def deltanet_items():
        rows = tm // 2
        for c in range(3 * DN_HEADS):
            cs = slice(c * LANES, (c + 1) * LANES)
            for r0 in range(0, tm, rows):
                lo = DN_HALO - (DN_CONV - 1) + r0
                acc = cw_ref[0:1, cs] * dn_buf[lo:lo + rows, cs]
                for j in range(1, DN_CONV):
                    acc = acc + cw_ref[j:j + 1, cs] * dn_buf[lo + j:lo + j + rows, cs]
                y = _silu(acc)
                if c < 2 * DN_HEADS:
                    y = y * lax.rsqrt(jnp.sum(y * y, axis=-1, keepdims=True) + EPS)
                rs = slice(r0, r0 + rows)
                if c < DN_HEADS:
                    q_ref[rs, cs] = y * DN_HEAD_DIM ** -0.5
                elif c < 2 * DN_HEADS:
                    k_ref[rs, (c - DN_HEADS) * LANES:(c - DN_HEADS + 1) * LANES] = y
                else:
                    v_ref[rs, (c - 2 * DN_HEADS) * LANES:(c - 2 * DN_HEADS + 1) * LANES] = y
            dn_buf[0:DN_HALO, cs] = dn_buf[tm:tm + DN_HALO, cs]
            yield

    mxu, cf, dn = mxu_items(), conformer_items(), deltanet_items()
    next(mxu)
    next(mxu)
    next(cf)
    next(mxu)
    _interleave([(mxu, 1), (dn, 2), (cf, 2)])


def _front(x2, norm_w, w_cat, w_abt, conv_w, pcol, prow, dw_w, dw_b, ln_w, ln_b):
    n = x2.shape[0]
    tm = ROW_TILE
    hd = NSA_HEAD_DIM
    row = lambda w: pl.BlockSpec((tm, w), lambda i: (i, 0))
    full = lambda shape: pl.BlockSpec(shape, lambda i: (0,) * len(shape))
    pieces = lambda k: pl.BlockSpec((k, tm, hd), lambda i: (0, i, 0))
    return pl.pallas_call(
        _front_kernel,
        out_shape=[jax.ShapeDtypeStruct((n, DN_WIDTH), F32)] * 3
        + [jax.ShapeDtypeStruct((n, LANES), F32), jax.ShapeDtypeStruct((8, n), F32),
           jax.ShapeDtypeStruct((n, DN_WIDTH), F32),
           jax.ShapeDtypeStruct((NSA_HEADS, n, hd), F32), jax.ShapeDtypeStruct((6, n, hd), F32),
           jax.ShapeDtypeStruct((n, LANES), F32), jax.ShapeDtypeStruct((n, CONV_WIDTH), F32)],
        grid=(n // tm,),
        in_specs=[row(D_MODEL), full((1, D_MODEL)), full((D_MODEL, IN_COLS)), full((16, D_MODEL)),
                  full((DN_CONV, 3 * DN_WIDTH)), full((8, LANES)), full((16, LANES)),
                  full((CONV_KERNEL, CONV_WIDTH)), full((1, CONV_WIDTH)), full((1, CONV_WIDTH)),
                  full((1, CONV_WIDTH))],
        out_specs=[row(DN_WIDTH)] * 3
        + [row(LANES), pl.BlockSpec((8, tm), lambda i: (0, i)), row(DN_WIDTH),
           pieces(NSA_HEADS), pieces(6), row(LANES), row(CONV_WIDTH)],
        scratch_shapes=[pltpu.VMEM((tm + DN_HALO, 3 * DN_WIDTH), F32),
                        pltpu.VMEM((tm + CONV_HALO, CONV_WIDTH), F32),
                        pltpu.VMEM((SUBLANES, tm + CONV_HALO - SUBLANES, CONV_WIDTH), F32)],
        compiler_params=_params(("arbitrary",)),
        name="front",
    )(x2, norm_w, w_cat, w_abt, conv_w, pcol, prow, dw_w, dw_b, ln_w, ln_b)


def _rms_rows(x, w):
    return x * lax.rsqrt(jnp.mean(x * x, axis=-1, keepdims=True) + EPS) * w


def _nsa_prep_kernel(kc_ref, vc_ref, ks_ref, kw_ref, pos_ref, w1_ref, w2_ref, knw_ref,
                     kcmp_ref, vcmp_ref, ksn_ref, kwn_ref):
    hd = NSA_HEAD_DIM

    def compress(x_ref, i):
        u_lo = jnp.zeros((N_CMP_PAD, CMP_HIDDEN), F32)
        u_hi = jnp.zeros((N_CMP_PAD, CMP_HIDDEN), F32)
        for r in range(CMP_STRIDE):
            xr = x_ref[pl.ds(r, N_CMP_PAD, stride=CMP_STRIDE), :]
            lo, hi = r, CMP_STRIDE + r
            u_lo = u_lo + _mm(xr + pos_ref[i, lo:lo + 1, :], w1_ref[i, lo * hd:(lo + 1) * hd, :])
            u_hi = u_hi + _mm(xr + pos_ref[i, hi:hi + 1, :], w1_ref[i, hi * hd:(hi + 1) * hd, :])
        hid = _silu(u_lo + pltpu.roll(u_hi, N_CMP_PAD - 1, axis=0))
        return _mm(hid, w2_ref[i])

    kcmp_ref[0] = _rms_rows(compress(kc_ref, 0), knw_ref[0:1, :])
    vcmp_ref[0] = compress(vc_ref, 1)
    ksn_ref[0] = _rms_rows(ks_ref[...], knw_ref[1:2, :])
    kwn_ref[0] = _rms_rows(kw_ref[...], knw_ref[2:3, :])


def _nsa_prep(kv6, pos, w1, w2, knw):
    b = kv6.shape[1]
    hd = NSA_HEAD_DIM
    full = lambda shape: pl.BlockSpec(shape, lambda i: (0,) * len(shape))
    piece = lambda j: pl.BlockSpec((None, None, SEQ, hd), lambda i: (j, i, 0, 0))
    bspec = lambda r, w: pl.BlockSpec((1, r, w), lambda i: (i, 0, 0))
    return pl.pallas_call(
        _nsa_prep_kernel,
        out_shape=[jax.ShapeDtypeStruct((b, N_CMP_PAD, hd), F32)] * 2
        + [jax.ShapeDtypeStruct((b, SEQ, hd), F32)] * 2,
        grid=(b,),
        in_specs=[
            piece(0), piece(1), piece(2), piece(4),
            full((2, CMP_LEN, hd)), full((2, CMP_LEN * hd, CMP_HIDDEN)), full((2, CMP_HIDDEN, hd)),
            full((3, hd)),
        ],
        out_specs=[bspec(N_CMP_PAD, hd)] * 2 + [bspec(SEQ, hd)] * 2,
        compiler_params=_params(("parallel",)),
        name="nsa_prep",
    )(kv6, kv6, kv6, kv6, pos, w1, w2, knw)


def _nsa_attn_kernel(q_ref, qnw_ref, kcmp_ref, vcmp_ref, ks_ref, vs_ref, kw_ref, vw_ref,
                     gate_ref, bcmp_ref, btab_ref, o_ref,
                     madd_ref, s_ref, ksb_ref, kwb_ref, vsa_ref, vwa_ref):
    i = pl.program_id(1)
    tq = Q_TILE
    nh = NSA_HEADS
    hd = NSA_HEAD_DIM
    t0 = i * tq

    @pl.when(i == 0)
    def _():
        ones = jnp.ones((SEQ, hd), BF16)
        ksb_ref[...] = ks_ref[0].astype(BF16)
        kwb_ref[...] = kw_ref[0].astype(BF16)
        vsa_ref[...] = jnp.concatenate([vs_ref[...].astype(BF16), ones], axis=1)
        vwa_ref[...] = jnp.concatenate([vw_ref[...].astype(BF16), ones], axis=1)

    qs = jnp.concatenate(
        [_rms_rows(q_ref[h], qnw_ref[...]) * hd ** -0.5 for h in range(nh)], axis=0).astype(BF16)

    row = lax.broadcasted_iota(jnp.int32, (tq, LANES), 0)
    lane = lax.broadcasted_iota(jnp.int32, (tq, LANES), 1)
    qpos = t0 + row

    branch_out = {}

    def select_items():
        s_all = _mm_nt(qs, kcmp_ref[0])
        cmp_valid = (qpos >= lane * CMP_STRIDE + (CMP_LEN - 1)) & (lane < N_CMP_PAD - 1)
        p_rows = []
        p_sum = jnp.zeros((tq, LANES), F32)
        for h in range(nh):
            s = jnp.where(cmp_valid, s_all[h * tq:(h + 1) * tq] + bcmp_ref[h], NEG_INF)
            m = jnp.max(s, axis=-1, keepdims=True)
            p = jnp.where(cmp_valid, jnp.exp(s - m), 0.0)
            l = jnp.sum(p, axis=-1, keepdims=True)
            p = p * jnp.where(l > 0.0, 1.0 / l, 0.0)
            p_rows.append(p)
            p_sum = p_sum + p
            yield
        branch_out['cmp'] = _mm(jnp.concatenate(p_rows, axis=0), vcmp_ref[0])

        ss = lax.broadcasted_iota(jnp.int32, (N_SLC, N_CMP_PAD), 0)
        jj = lax.broadcasted_iota(jnp.int32, (N_SLC, N_CMP_PAD), 1)
        overlap_t = ((jj * CMP_STRIDE < ss * SLC_BLOCK + SLC_BLOCK)
                     & (jj * CMP_STRIDE + CMP_LEN > ss * SLC_BLOCK) & (jj < N_CMP_PAD - 1))
        overlap_t = jnp.where(overlap_t, 1.0, 0.0).astype(BF16)
        nt = lambda a, b_: lax.dot_general(a, b_, (((1,), (1,)), ((), ())), preferred_element_type=F32)
        p_hi, p_mid, p_lo = _split3(p_sum)
        imp = nt(overlap_t, p_hi) + nt(overlap_t, p_mid) + nt(overlap_t, p_lo)
        blk = lax.broadcasted_iota(jnp.int32, (N_SLC, tq), 0)
        cur = (t0 + lax.broadcasted_iota(jnp.int32, (N_SLC, tq), 1)) // SLC_BLOCK
        causal_blk = blk <= cur
        forced = (blk == 0) | (blk == cur) | (blk == cur - 1)
        imp = jnp.where(causal_blk & forced, FORCE, jnp.where(causal_blk, imp, -1.0))
        yield
        rank = jnp.zeros((N_SLC, tq), jnp.int32)
        for s2 in range(N_SLC):
            other = jnp.broadcast_to(imp[s2:s2 + 1, :], (N_SLC, tq))
            beats = (other > imp) | ((other == imp) & (blk > s2))
            rank = rank + jnp.where(beats, 1, 0)
            if s2 % 8 == 7:
                yield
        sel_t = jnp.where(rank < SLC_TOP_N, 1.0, 0.0).astype(BF16)
        er = lax.broadcasted_iota(jnp.int32, (N_SLC, SEQ), 0)
        ec = lax.broadcasted_iota(jnp.int32, (N_SLC, SEQ), 1)
        expand = jnp.where(ec // SLC_BLOCK == er, 1.0, 0.0).astype(BF16)
        key = lax.broadcasted_iota(jnp.int32, (tq, SEQ), 1)
        qall = t0 + lax.broadcasted_iota(jnp.int32, (tq, SEQ), 0)
        chosen = (_mm_tn(sel_t, expand) > 0.5) & (qall >= key)
        madd_ref[...] = jnp.where(chosen, 0.0, NEG_INF)

    neg = jnp.full((nh * tq, tq), NEG_INF, F32)
    zero = jnp.zeros((nh * tq, LANES), F32)

    def scores(k_tile, bias_idx, add):
        s_t = _mm_nt(qs, k_tile)
        return jnp.concatenate(
            [s_t[h * tq:(h + 1) * tq] + (btab_ref[h, bias_idx] + add) for h in range(nh)], axis=0)

    def normalise(acc):
        return acc[:, :hd] * (1.0 / _bcast_col(acc, hd, hd))

    def window_items():
        kcol = lax.broadcasted_iota(jnp.int32, (tq, tq), 1)
        qrow = t0 + lax.broadcasted_iota(jnp.int32, (tq, tq), 0)
        n_win = WINDOW // tq + 1
        starts = []
        m_run = neg
        for j in range(n_win):
            d = n_win - 1 - j
            k0 = pl.multiple_of(jnp.maximum(i - d, 0) * tq, tq)
            starts.append(k0)
            dist = qrow - (k0 + kcol)
            ok = (dist >= 0) & (dist < WINDOW) & (i >= d)
            s_m = scores(kwb_ref[pl.ds(k0, tq), :], min(d, 2), jnp.where(ok, 0.0, NEG_INF))
            s_ref[:, j * tq:(j + 1) * tq] = s_m
            m_run = jnp.maximum(m_run, s_m)
            yield
        m_win = jnp.broadcast_to(jnp.max(m_run, axis=-1, keepdims=True), (nh * tq, tq))
        acc = zero
        for j in range(n_win):
            p = jnp.exp(s_ref[:, j * tq:(j + 1) * tq] - m_win).astype(BF16)
            acc = acc + jnp.dot(p, vwa_ref[pl.ds(starts[j], tq), :], preferred_element_type=F32)
            yield
        branch_out['win'] = normalise(acc)

    _interleave([(select_items(), 1), (window_items(), 1)])
    o_cmp, o_win = branch_out['cmp'], branch_out['win']

    n_groups = (i + SLC_GROUP) // SLC_GROUP

    def slc_pass1(j, m_run):
        for u in range(SLC_GROUP):
            kt = SLC_GROUP * j + u
            k0 = pl.multiple_of(kt * tq, tq)
            s_m = scores(ksb_ref[pl.ds(k0, tq), :], jnp.clip(i - kt, 0, 2), madd_ref[:, pl.ds(k0, tq)])
            s_ref[:, pl.ds(k0, tq)] = s_m
            m_run = jnp.maximum(m_run, s_m)
        return m_run

    m_slc = jnp.broadcast_to(
        jnp.max(lax.fori_loop(0, n_groups, slc_pass1, neg), axis=-1, keepdims=True), (nh * tq, tq))

    def slc_pass2(j, acc):
        for u in range(SLC_GROUP):
            k0 = pl.multiple_of((SLC_GROUP * j + u) * tq, tq)
            p = jnp.exp(s_ref[:, pl.ds(k0, tq)] - m_slc).astype(BF16)
            acc = acc + jnp.dot(p, vsa_ref[pl.ds(k0, tq), :], preferred_element_type=F32)
        return acc

    o_slc = normalise(lax.fori_loop(0, n_groups, slc_pass2, zero))

    gates = _sigmoid(gate_ref[0])
    outs = []
    for h in range(nh):
        hs = slice(h * tq, (h + 1) * tq)
        g0 = 2 * DN_HEADS + 3 * h
        outs.append(_bcast_col(gates, g0, hd) * o_cmp[hs]
                    + _bcast_col(gates, g0 + 1, hd) * o_slc[hs]
                    + _bcast_col(gates, g0 + 2, hd) * o_win[hs])
    o_ref[...] = jnp.concatenate(outs, axis=1)


def _nsa_attn(q, qnw, kcmp, vcmp, ks, kv6, kw, small, bias_cmp, btab):
    b = kv6.shape[1]
    tq = Q_TILE
    nq = SEQ // tq
    nh, hd = NSA_HEADS, NSA_HEAD_DIM
    full = lambda shape: pl.BlockSpec(shape, lambda bi, i: (0,) * len(shape))
    per_b = lambda r: pl.BlockSpec((1, r, hd), lambda bi, i: (bi, 0, 0))
    piece = lambda j: pl.BlockSpec((None, None, SEQ, hd), lambda bi, i: (j, bi, 0, 0))
    heads = pl.BlockSpec((nh, tq, hd), lambda bi, i: (0, bi * nq + i, 0))
    return pl.pallas_call(
        _nsa_attn_kernel,
        out_shape=jax.ShapeDtypeStruct((b * SEQ, NSA_WIDTH), F32),
        grid=(b, nq),
        in_specs=[
            heads,
            full((1, hd)),
            per_b(N_CMP_PAD), per_b(N_CMP_PAD), per_b(SEQ), piece(3), per_b(SEQ), piece(5),
            pl.BlockSpec((1, tq, LANES), lambda bi, i: (bi, i, 0)),
            pl.BlockSpec((nh, tq, N_CMP_PAD), lambda bi, i: (0, i, 0)),
            full((nh, 3, tq, tq)),
        ],
        out_specs=pl.BlockSpec((tq, NSA_WIDTH), lambda bi, i: (bi * nq + i, 0)),
        scratch_shapes=[pltpu.VMEM((tq, SEQ), F32), pltpu.VMEM((nh * tq, SEQ), F32),
                        pltpu.VMEM((SEQ, hd), BF16), pltpu.VMEM((SEQ, hd), BF16),
                        pltpu.VMEM((SEQ, 2 * hd), BF16), pltpu.VMEM((SEQ, 2 * hd), BF16)],
        compiler_params=_params(("parallel", "arbitrary")),
        name="nsa_attn",
    )(q, qnw, kcmp, vcmp, ks, kv6, kw, kv6, small, bias_cmp, btab)


def _out_router_kernel(ya_ref, yb_ref, yc_ref, x_ref, wo_ref, fnw_ref, rw_ref, rb_ref,
                       xo_ref, h_ref, idx_ref, wt_ref, hist_ref):
    wa = DN_WIDTH
    wb = wa + NSA_WIDTH
    y = (jnp.dot(ya_ref[...].astype(BF16), wo_ref[0:wa, :], preferred_element_type=F32)
         + jnp.dot(yb_ref[...].astype(BF16), wo_ref[wa:wb, :], preferred_element_type=F32)
         + jnp.dot(yc_ref[...].astype(BF16), wo_ref[wb:, :], preferred_element_type=F32))
    xn = x_ref[...] + y
    xo_ref[...] = xn
    h = xn * lax.rsqrt(jnp.mean(xn * xn, axis=-1, keepdims=True) + EPS) * fnw_ref[...]
    _store_tile_rows(h_ref, h, h.shape[0])
    h_hi, h_lo = _hi_lo(h)
    r_hi = jnp.dot(h_hi, rw_ref[...], preferred_element_type=F32)
    r_lo = jnp.dot(h_lo, rw_ref[...], preferred_element_type=F32)
    logits = ((r_hi[:, :LANES] + r_hi[:, LANES:]) + (r_lo[:, :LANES] + r_lo[:, LANES:])) + rb_ref[...]
    lane = lax.broadcasted_iota(jnp.int32, logits.shape, 1)
    vals, idxs = [], []
    for _ in range(TOP_K):
        m = jnp.max(logits, axis=-1, keepdims=True)
        ix = jnp.min(jnp.where(logits == m, lane, LANES), axis=-1, keepdims=True)
        vals.append(m)
        idxs.append(ix)
        logits = jnp.where(lane == ix, -jnp.inf, logits)
    es = [jnp.exp(v - vals[0]) for v in vals]
    inv = 1.0 / (es[0] + es[1] + es[2] + es[3])
    idx_out = jnp.zeros(lane.shape, jnp.int32)
    wt_out = jnp.zeros(lane.shape, F32)
    chosen = jnp.zeros(lane.shape, F32)
    for k in range(TOP_K):
        idx_out = jnp.where(lane == k, idxs[k], idx_out)
        wt_out = jnp.where(lane == k, es[k] * inv, wt_out)
        chosen = chosen + jnp.where(lane == idxs[k], 1.0, 0.0)
    idx_ref[...] = idx_out
    wt_ref[...] = wt_out
    hist_ref[0] = jnp.broadcast_to(jnp.sum(chosen, axis=0, keepdims=True), (8, LANES))


def _out_router(ya, yb, yc, x2, w_out, fnw, rw, rb):
    n = x2.shape[0]
    tm = ROW_TILE
    row = lambda w: pl.BlockSpec((tm, w), lambda i: (i, 0))
    full = lambda shape: pl.BlockSpec(shape, lambda i: (0,) * len(shape))
    return pl.pallas_call(
        _out_router_kernel,
        out_shape=[jax.ShapeDtypeStruct((n, D_MODEL), F32), jax.ShapeDtypeStruct((n * SUBLANES, LANES), F32),
                   jax.ShapeDtypeStruct((n, LANES), jnp.int32), jax.ShapeDtypeStruct((n, LANES), F32),
                   jax.ShapeDtypeStruct((n // tm, 8, LANES), F32)],
        grid=(n // tm,),
        in_specs=[row(DN_WIDTH), row(NSA_WIDTH), row(CONV_WIDTH), row(D_MODEL),
                  full((D_MODEL, D_MODEL)), full((1, D_MODEL)), full((D_MODEL, 2 * LANES)), full((1, LANES))],
        out_specs=[row(D_MODEL), pl.BlockSpec((tm * SUBLANES, LANES), lambda i: (i, 0)), row(LANES), row(LANES),
                   pl.BlockSpec((1, 8, LANES), lambda i: (i, 0, 0))],
        compiler_params=_params(("parallel",)),
        name="out_router",
    )(ya, yb, yc, x2, w_out, fnw, rw, rb)


def _slots_kernel(idx_ref, base_ref, slot_ref):
    tm = ROW_TILE
    idx = idx_ref[...]
    lane = lax.broadcasted_iota(jnp.int32, (tm, LANES), 1)
    onehots = [jnp.where(lane == _bcast_col(idx, k, LANES), 1.0, 0.0) for k in range(TOP_K)]
    cnt = (onehots[0] + onehots[1]) + (onehots[2] + onehots[3])
    r = lax.broadcasted_iota(jnp.int32, (tm, tm), 0)
    c = lax.broadcasted_iota(jnp.int32, (tm, tm), 1)
    earlier = jnp.where(r > c, 1.0, 0.0).astype(BF16)
    rank = jnp.dot(earlier, cnt.astype(BF16), preferred_element_type=F32) + base_ref[0, 0:1, :]
    out = jnp.zeros((tm, LANES), F32)
    for k in range(TOP_K):
        out = jnp.where(lane == k, jnp.sum(rank * onehots[k], axis=-1, keepdims=True), out)
    slot_ref[...] = (out * SUBLANES).astype(jnp.int32)


def _slots(idx, base3):
    n = idx.shape[0]
    tm = ROW_TILE
    return pl.pallas_call(
        _slots_kernel,
        out_shape=jax.ShapeDtypeStruct((n, LANES), jnp.int32),
        grid=(n // tm,),
        in_specs=[pl.BlockSpec((tm, LANES), lambda i: (i, 0)),
                  pl.BlockSpec((1, 8, LANES), lambda i: (i, 0, 0))],
        out_specs=pl.BlockSpec((tm, LANES), lambda i: (i, 0)),
        compiler_params=_params(("parallel",)),
        name="moe_slots",
    )(idx, base3)


def _dispatch_kernel(slot_ref, zoff_ref, nv_ref, h_ref, xs_hbm, zbuf, sem, zsem):
    i = pl.program_id(0)
    tm = ROW_TILE
    base = i * tm

    @pl.when(i == 0)
    def _():
        zbuf[...] = jnp.zeros(zbuf.shape, F32)
        blk = MOE_TILE * SUBLANES
        n_blocks = xs_hbm.shape[0] // blk

        def fill(row0):
            return pltpu.make_async_copy(
                zbuf, xs_hbm.at[pl.ds(pl.multiple_of(row0 * SUBLANES, blk), blk)], zsem)

        def fill_region_end(e, c):
            fill(zoff_ref[e]).start()
            return c

        def fill_tail(b, c):
            fill(b * MOE_TILE).start()
            return c

        def wait_fill(b, c):
            fill(0).wait()
            return c

        lax.fori_loop(0, N_EXPERTS, fill_region_end, 0)
        lax.fori_loop(nv_ref[0], n_blocks, fill_tail, 0)
        lax.fori_loop(0, N_EXPERTS + n_blocks - nv_ref[0], wait_fill, 0)

    def body(r, c):
        src = h_ref.at[pl.ds(pl.multiple_of(r * SUBLANES, SUBLANES), SUBLANES)]
        for k in range(TOP_K):
            row0 = pl.multiple_of(slot_ref[(base + r) * TOP_K + k], SUBLANES)
            pltpu.make_async_copy(src, xs_hbm.at[pl.ds(row0, SUBLANES)], sem).start(priority=k % 2)
        return c

    lax.fori_loop(0, tm, body, 0, unroll=8)
    done = xs_hbm.at[pl.ds(0, TOP_K * tm * SUBLANES)]
    pltpu.make_async_copy(done, done, sem).wait()


def _dispatch(slot_flat, zero_off, n_valid, h, n_slots):
    n = h.shape[0] // SUBLANES
    tm = ROW_TILE
    return pl.pallas_call(
        _dispatch_kernel,
        out_shape=jax.ShapeDtypeStruct((n_slots * SUBLANES, LANES), F32),
        grid_spec=pltpu.PrefetchScalarGridSpec(
            num_scalar_prefetch=3,
            grid=(n // tm,),
            in_specs=[pl.BlockSpec((tm * SUBLANES, LANES), lambda i, s, z, nv: (i, 0))],
            out_specs=pl.BlockSpec(memory_space=pl.ANY),
            scratch_shapes=[pltpu.VMEM((MOE_TILE * SUBLANES, LANES), F32),
                            pltpu.SemaphoreType.DMA, pltpu.SemaphoreType.DMA],
        ),
        compiler_params=_params(("arbitrary",)),
        name="moe_dispatch",
    )(slot_flat, zero_off, n_valid, h)


def _expert_kernel(be_ref, nv_ref, xs_ref, wgu_ref, bgu_ref, wd_ref, bd_ref, o_ref, wgu_bf, wd_bf):
    i = pl.program_id(0)
    new_expert = (i == 0) | (be_ref[i] != be_ref[jnp.maximum(i - 1, 0)])

    @pl.when((i < nv_ref[0]) & new_expert)
    def _():
        step = 512
        for c0 in range(0, 2 * D_FF, step):
            wgu_bf[:, c0:c0 + step] = wgu_ref[:, c0:c0 + step].astype(BF16)
        for c0 in range(0, D_MODEL, step):
            wd_bf[:, c0:c0 + step] = wd_ref[:, c0:c0 + step].astype(BF16)

    @pl.when(i < nv_ref[0])
    def _():
        xb = _load_tile_rows(xs_ref, MOE_TILE).astype(BF16)
        gu = jnp.dot(xb, wgu_bf[...], preferred_element_type=F32) + bgu_ref[...]
        gate = jnp.minimum(gu[:, :D_FF], SWIGLU_LIMIT)
        up = jnp.clip(gu[:, D_FF:], -SWIGLU_LIMIT, SWIGLU_LIMIT)
        act = (up + 1.0) * gate * _sigmoid(SWIGLU_ALPHA * gate)
        y = jnp.dot(act.astype(BF16), wd_bf[...], preferred_element_type=F32) + bd_ref[...]
        _store_tile_rows(o_ref, y, MOE_TILE)

    @pl.when(i >= nv_ref[0])
    def _():
        o_ref[...] = jnp.zeros(o_ref.shape, F32)


def _expert_ffn(layer, block_expert, n_valid, xs, wgu, bgu, wd, bd):
    n_slots = xs.shape[0] // SUBLANES
    tm = MOE_TILE
    return pl.pallas_call(
        _expert_kernel,
        out_shape=jax.ShapeDtypeStruct(xs.shape, F32),
        grid_spec=pltpu.PrefetchScalarGridSpec(
            num_scalar_prefetch=2,
            grid=(n_slots // tm,),
            in_specs=[
                pl.BlockSpec((tm * SUBLANES, LANES), lambda i, be, nv: (i, 0)),
                pl.BlockSpec((None, None, D_MODEL, 2 * D_FF), lambda i, be, nv: (layer, be[i], 0, 0)),
                pl.BlockSpec((None, None, 1, 2 * D_FF), lambda i, be, nv: (layer, be[i], 0, 0)),
                pl.BlockSpec((None, None, D_FF, D_MODEL), lambda i, be, nv: (layer, be[i], 0, 0)),
                pl.BlockSpec((None, None, 1, D_MODEL), lambda i, be, nv: (layer, be[i], 0, 0)),
            ],
            out_specs=pl.BlockSpec((tm * SUBLANES, LANES), lambda i, be, nv: (i, 0)),
            scratch_shapes=[pltpu.VMEM((D_MODEL, 2 * D_FF), BF16), pltpu.VMEM((D_FF, D_MODEL), BF16)],
        ),
        compiler_params=pltpu.CompilerParams(dimension_semantics=("arbitrary",),
                                             vmem_limit_bytes=EXPERT_VMEM_LIMIT_BYTES),
        name="expert_ffn",
    )(block_expert, n_valid, xs, wgu, bgu, wd, bd)


def _combine_kernel(slot_ref, ys_hbm, x_ref, wt_ref, o_ref, buf, acc_ref, sems):
    i = pl.program_id(0)
    tc = COMBINE_TILE
    nsteps = pl.num_programs(0)
    cur = lax.rem(i, 2)

    def issue(step, par):
        base = step * tc

        def body(r, c):
            dst0 = pl.multiple_of(r * SUBLANES, SUBLANES)
            for k in range(TOP_K):
                row0 = pl.multiple_of(slot_ref[(base + r) * TOP_K + k], SUBLANES)
                pltpu.make_async_copy(ys_hbm.at[pl.ds(row0, SUBLANES)],
                                      buf.at[par, k, pl.ds(dst0, SUBLANES)],
                                      sems.at[par]).start(priority=k % 2)
            return c

        lax.fori_loop(0, tc, body, 0, unroll=8)

    @pl.when(i == 0)
    def _():
        issue(0, 0)

    @pl.when(i + 1 < nsteps)
    def _():
        issue(i + 1, 1 - cur)

    pltpu.make_async_copy(buf.at[cur], buf.at[cur], sems.at[cur]).wait()
    wt = wt_ref[...]
    acc = None
    for k in range(TOP_K):
        w_tiles = jnp.broadcast_to(_bcast_col(wt, k, LANES)[:, None, :], (tc, SUBLANES, LANES))
        term = w_tiles.reshape(tc * SUBLANES, LANES) * buf[cur, k]
        acc = term if acc is None else acc + term
    acc_ref[...] = acc
    for s in range(ROW_CHUNKS):
        cs = slice(s * LANES, (s + 1) * LANES)
        o_ref[:, cs] = x_ref[:, cs] + acc_ref[pl.ds(s, tc, stride=SUBLANES), :]


def _combine(slot_flat, ys, x2, wt):
    n = x2.shape[0]
    tc = COMBINE_TILE
    return pl.pallas_call(
        _combine_kernel,
        out_shape=jax.ShapeDtypeStruct((n, D_MODEL), F32),
        grid_spec=pltpu.PrefetchScalarGridSpec(
            num_scalar_prefetch=1,
            grid=(n // tc,),
            in_specs=[pl.BlockSpec(memory_space=pl.ANY),
                      pl.BlockSpec((tc, D_MODEL), lambda i, s: (i, 0)),
                      pl.BlockSpec((tc, LANES), lambda i, s: (i, 0))],
            out_specs=pl.BlockSpec((tc, D_MODEL), lambda i, s: (i, 0)),
            scratch_shapes=[pltpu.VMEM((2, TOP_K, tc * SUBLANES, LANES), F32),
                            pltpu.VMEM((tc * SUBLANES, LANES), F32), pltpu.SemaphoreType.DMA((2,))],
        ),
        compiler_params=_params(("arbitrary",)),
        name="moe_combine",
    )(slot_flat, ys, x2, wt)


def _moe(layer, h, x2, idx, wt, hist3, wgu, bgu, wd, bd):
    n = x2.shape[0]
    tm = MOE_TILE
    n_slots = n * TOP_K + N_EXPERTS * tm
    n_blocks = n_slots // tm
    hist = hist3[:, 0, :N_EXPERTS]
    counts = jnp.sum(hist, axis=0).astype(jnp.int32)
    padded = (counts + tm - 1) // tm * tm
    pad_end = jnp.cumsum(padded)
    pad_start = pad_end - padded
    tile_base = pad_start[None, :].astype(F32) + (jnp.cumsum(hist, axis=0) - hist)
    base3 = jnp.broadcast_to(jnp.pad(tile_base, ((0, 0), (0, LANES - N_EXPERTS)))[:, None, :],
                             (hist.shape[0], 8, LANES))
    blk0 = jnp.arange(n_blocks) * tm
    block_expert = jnp.minimum(jnp.sum(blk0[:, None] >= pad_end[None, :], axis=1), N_EXPERTS - 1).astype(jnp.int32)
    n_valid = (pad_end[-1:] // tm).astype(jnp.int32)
    zero_off = jnp.maximum(pad_end - tm, 0).astype(jnp.int32)

    slot_flat = _slots(idx, base3)[:, :TOP_K].reshape(-1)
    xs = _dispatch(slot_flat, zero_off, n_valid, h, n_slots)
    ys = _expert_ffn(layer, block_expert, n_valid, xs, wgu, bgu, wd, bd)
    return _combine(slot_flat, ys, x2, wt)


def _t5_bucket(dist):
    n = jnp.maximum(dist, 0)
    max_exact = REL_BUCKETS // 2
    nf = jnp.maximum(n, 1).astype(F32)
    large = max_exact + (jnp.log(nf / max_exact) / math.log(REL_MAX_DIST / max_exact)
                         * (REL_BUCKETS - max_exact)).astype(jnp.int32)
    large = jnp.minimum(large, REL_BUCKETS - 1)
    return jnp.where(n < max_exact, n, large)


def _bias_tables(rel_bias):
    rel_bias = rel_bias.astype(F32)
    tq = Q_TILE

    def lookup(dist):
        bucket = _t5_bucket(dist)
        out = jnp.zeros((NSA_HEADS,) + dist.shape, F32)
        for bk in range(REL_BUCKETS):
            out = jnp.where(bucket[None] == bk, rel_bias[bk].reshape((NSA_HEADS,) + (1,) * dist.ndim), out)
        return out

    t_pos = jnp.arange(SEQ)
    cmp_end = jnp.arange(N_CMP_PAD) * CMP_STRIDE + CMP_LEN - 1
    bias_cmp = lookup(t_pos[:, None] - cmp_end[None, :])
    rr = jnp.arange(tq)[:, None] - jnp.arange(tq)[None, :]
    btab = lookup(jnp.stack([rr, rr + tq, rr + 2 * tq]))
    return bias_cmp, btab


def _layer(layer, x2, b, p, experts, bias_cmp, btab):
    n = x2.shape[0]
    w = p['w_in']
    o_a, o_b, o_q, o_kv, o_g, o_u = 2048, 2052, 2056, 2312, 2696, 2708
    w_small = jnp.concatenate([w[:, o_a:o_q], w[:, o_g:o_u],
                               jnp.zeros((D_MODEL, LANES - 8 - 3 * NSA_HEADS), F32)], axis=1)
    w_cat = jnp.concatenate([w[:, :o_a], w[:, o_q:o_kv], w[:, o_kv:o_g], w[:, o_u:], w_small],
                            axis=1).astype(BF16)
    w_abt = jnp.concatenate([w[:, o_a:o_q].T, jnp.zeros((8, D_MODEL), F32)], axis=0).astype(BF16)
    pcol = jnp.zeros((8, LANES), F32).at[0, :DN_HEADS].set(p['dn_a_log']).at[1, :DN_HEADS].set(p['dn_dt_bias'])
    prow = jnp.zeros((16, LANES), F32).at[:DN_HEADS, 0].set(p['dn_a_log']).at[:DN_HEADS, 1].set(p['dn_dt_bias'])
    dq, dk, dv, gb, grow, z, nq, nkv, small, y_c = _front(
        x2, p['attn_norm_w'][None, :], w_cat, w_abt, p['dn_conv_w'], pcol, prow,
        p['conv_dw_w'], p['conv_dw_b'][None, :], p['conv_ln_w'][None, :], p['conv_ln_b'][None, :])
    small3 = small.reshape(b, SEQ, LANES)

    seq = lambda a: a.reshape(b, SEQ, a.shape[-1])
    y_a = _delta_rule(seq(dq), seq(dk), seq(dv), seq(gb), grow, seq(z), p['dn_norm_w'][None, :])

    kv6 = nkv.reshape(6, b, SEQ, NSA_HEAD_DIM)
    kcmp, vcmp, ksn, kwn = _nsa_prep(kv6, p['nsa_cmp_pos'], p['nsa_cmp_w1'].astype(BF16),
                                     p['nsa_cmp_w2'].astype(BF16), p['nsa_k_norm_w'])
    y_b = _nsa_attn(nq, p['nsa_q_norm_w'][None, :], kcmp, vcmp, ksn, kv6, kwn, small3, bias_cmp, btab)

    rw = jnp.concatenate(_hi_lo(jnp.pad(p['router_w'], ((0, 0), (0, LANES - N_EXPERTS)))), axis=1)
    rb =jnp.concatenate([p['router_b'], jnp.full((LANES - N_EXPERTS,), NEG_INF, F32)])[None, :]
    x_new, h, idx, wt, hist3 = _out_router(y_a.reshape(n, DN_WIDTH), y_b, y_c.reshape(n, CONV_WIDTH), x2,
                                           p['w_out'].astype(BF16), p['ffn_norm_w'][None, :], rw, rb)
    return _moe(layer, h, x_new, idx, wt, hist3, *experts)


def kernel(x, attn_norm_w, w_in, dn_conv_w, dn_a_log, dn_dt_bias, dn_norm_w, nsa_q_norm_w, nsa_k_norm_w, nsa_cmp_pos, nsa_cmp_w1, nsa_cmp_w2, conv_dw_w, conv_dw_b, conv_ln_w, conv_ln_b, w_out, ffn_norm_w, router_w, router_b, w_gate_up, b_gate_up, w_down, b_down, rel_bias):
    b, t, d = x.shape
    assert (t, d) == (SEQ, D_MODEL)
    stacked = dict(attn_norm_w=attn_norm_w, w_in=w_in, dn_conv_w=dn_conv_w, dn_a_log=dn_a_log,
                   dn_dt_bias=dn_dt_bias, dn_norm_w=dn_norm_w, nsa_q_norm_w=nsa_q_norm_w,
                   nsa_k_norm_w=nsa_k_norm_w, nsa_cmp_pos=nsa_cmp_pos, nsa_cmp_w1=nsa_cmp_w1,
                   nsa_cmp_w2=nsa_cmp_w2, conv_dw_w=conv_dw_w, conv_dw_b=conv_dw_b,
                   conv_ln_w=conv_ln_w, conv_ln_b=conv_ln_b, w_out=w_out, ffn_norm_w=ffn_norm_w,
                   router_w=router_w, router_b=router_b)
    experts = (w_gate_up, b_gate_up[:, :, None, :], w_down, b_down[:, :, None, :])
    bias_cmp, btab = _bias_tables(rel_bias)
    x2 = x.reshape(b * t, d)
    for l in range(w_in.shape[0]):
        x2 = _layer(l, x2, b, {k: v[l] for k, v in stacked.items()}, experts, bias_cmp, btab)
    return x2.reshape(b, t, d)
```

```python
import math

import jax
import jax.numpy as jnp
from jax import lax
from jax.experimental import pallas as pl
from jax.experimental.pallas import tpu as pltpu

F32 = jnp.float32
BF16 = jnp.bfloat16

D_MODEL = 1024
SEQ = 2048
DN_HEADS = 4
DN_HEAD_DIM = 128
DN_WIDTH = DN_HEADS * DN_HEAD_DIM
DN_CONV = 4
DN_CHUNK = 64
NSA_HEADS = 4
NSA_HEAD_DIM = 64
NSA_WIDTH = NSA_HEADS * NSA_HEAD_DIM
CMP_LEN = 32
CMP_STRIDE = 16
CMP_HIDDEN = 2 * NSA_HEAD_DIM
SLC_BLOCK = 64
SLC_TOP_N = 16
WINDOW = 512
CONV_WIDTH = 256
CONV_KERNEL = 31
REL_BUCKETS = 32
REL_MAX_DIST = 128
N_EXPERTS = 32
TOP_K = 4
D_FF = D_MODEL
SWIGLU_LIMIT = 7.0
SWIGLU_ALPHA = 1.702
EPS = 1e-6
NEG_INF = -1e30
FORCE = 1e4

LANES = 128
VMEM_LIMIT_BYTES = 48 * 1024 * 1024
EXPERT_VMEM_LIMIT_BYTES = 56 * 1024 * 1024

ROW_TILE = 512
SEQ_TILE = 256
Q_TILE = 128
SLC_GROUP = 4
MOE_TILE = 512
COMBINE_TILE = 256
DISPATCH_TILE = 1024

N_CMP_PAD = 128
N_SLC = SEQ // SLC_BLOCK


def _params(sem=None):
    return pltpu.CompilerParams(dimension_semantics=sem, vmem_limit_bytes=VMEM_LIMIT_BYTES)


def _mm(a, b):
    return jnp.dot(a.astype(BF16), b.astype(BF16), preferred_element_type=F32)


def _mm_nt(a, b):
    return lax.dot_general(a.astype(BF16), b.astype(BF16), (((1,), (1,)), ((), ())),
                           preferred_element_type=F32)


def _mm_tn(a, b):
    return lax.dot_general(a.astype(BF16), b.astype(BF16), (((0,), (0,)), ((), ())),
                           preferred_element_type=F32)


def _split3(x):
    hi = x.astype(BF16)
    r1 = x - hi.astype(F32)
    mid = r1.astype(BF16)
    lo = (r1 - mid.astype(F32)).astype(BF16)
    return hi, mid, lo


def _dot01_right(x, m01):
    hi, mid, lo = _split3(x)
    d = lambda p: jnp.dot(p, m01, preferred_element_type=F32)
    return d(hi) + d(mid) + d(lo)


def _dot01_left(m01, x):
    hi, mid, lo = _split3(x)
    d = lambda p: jnp.dot(m01, p, preferred_element_type=F32)
    return d(hi) + d(mid) + d(lo)


def _hi_lo(x):
    hi = x.astype(BF16)
    return hi, (x - hi.astype(F32)).astype(BF16)


def _sigmoid(x):
    return 1.0 / (1.0 + jnp.exp(-x))


def _silu(x):
    return x * _sigmoid(x)


def _softplus(x):
    return jnp.maximum(x, 0.0) + jnp.log(1.0 + jnp.exp(-jnp.abs(x)))


SUBLANES = 8
ROW_CHUNKS = D_MODEL // LANES
assert ROW_CHUNKS == SUBLANES


def _store_tile_rows(ref, value, rows):
    for s in range(ROW_CHUNKS):
        ref[pl.ds(s, rows, stride=SUBLANES), :] = value[:, s * LANES:(s + 1) * LANES]


def _load_tile_rows(ref, rows):
    return jnp.concatenate([ref[pl.ds(s, rows, stride=SUBLANES), :] for s in range(ROW_CHUNKS)], axis=1)


def _bcast_col(x, j, width):
    return jnp.broadcast_to(x[:, j:j + 1], (x.shape[0], width))


IN_SEGS = (3 * DN_WIDTH, DN_WIDTH, NSA_WIDTH, 6 * NSA_HEAD_DIM, 2 * CONV_WIDTH, LANES)
IN_COLS = sum(IN_SEGS)


def _delta_prepare(q_ref, k_ref, v_ref, gb_ref, grow_ref, wy):
    u_ref, w_ref, attn_ref, qg_ref, kk_ref, gc_ref = wy
    ts = SEQ_TILE
    ch = DN_CHUNK
    hd = DN_HEAD_DIM
    r = lax.broadcasted_iota(jnp.int32, (ts, ts), 0)
    c = lax.broadcasted_iota(jnp.int32, (ts, ts), 1)
    same_chunk = (r // ch) == (c // ch)
    tril = same_chunk & (r >= c)
    strict = same_chunk & (r > c)
    same16 = (r // 16) == (c // 16)
    eye = jnp.where(r == c, 1.0, 0.0).astype(F32)
    m_col = jnp.where(tril, 1.0, 0.0).astype(BF16)
    m_row = jnp.where(same_chunk & (r <= c), 1.0, 0.0).astype(BF16)

    gb = gb_ref[0]
    gc_col = _dot01_left(m_col, gb)
    gc_row = _dot01_right(grow_ref[...], m_row)

    heads = range(DN_HEADS)
    hsl = [slice(h * hd, (h + 1) * hd) for h in heads]
    kh = [k_ref[0, :, hsl[h]] for h in heads]
    gcb = [_bcast_col(gc_col, h, ts) for h in heads]
    decay = [jnp.where(tril, jnp.exp(jnp.where(
        tril, gcb[h] - jnp.broadcast_to(gc_row[h:h + 1, :], (ts, ts)), 0.0)), 0.0) for h in heads]
    beta = [_bcast_col(gb, DN_HEADS + h, hd) for h in heads]
    kb = [kh[h] * beta[h] for h in heads]
    yield
    a_mat = [jnp.where(strict, _mm_nt(kb[h], kh[h]) * decay[h], 0.0) for h in heads]
    yield
    d_mat = [jnp.where(same16, a_mat[h], 0.0) for h in heads]
    e_mat = [a_mat[h] - d_mat[h] for h in heads]
    d2 = [_mm(d_mat[h], d_mat[h]) for h in heads]
    yield
    d4 = [_mm(d2[h], d2[h]) for h in heads]
    yield
    t1 = [_mm(eye - d_mat[h], eye + d2[h]) for h in heads]
    yield
    d8 = [_mm(d4[h], d4[h]) for h in heads]
    yield
    t2 = [_mm(t1[h], eye + d4[h]) for h in heads]
    yield
    p_mat = [_mm(t2[h], eye + d8[h]) for h in heads]
    yield
    m_mat = [_mm(p_mat[h], e_mat[h]) for h in heads]
    yield
    m2 = [_mm(m_mat[h], m_mat[h]) for h in heads]
    yield
    t3 = [_mm(eye - m_mat[h], eye + m2[h]) for h in heads]
    yield
    t_mat = [_mm(t3[h], p_mat[h]) for h in heads]
    yield
    for h in heads:
        gc128 = gcb[h][:, :hd]
        expg = jnp.exp(gc128)
        qh = q_ref[0, :, hsl[h]]
        sol = _mm(t_mat[h], jnp.concatenate([v_ref[0, :, hsl[h]] * beta[h], kb[h] * expg], axis=1))
        u_ref[h] = sol[:, :hd]
        w_ref[h] = sol[:, hd:].astype(BF16)
        qg_ref[h] = (qh * expg).astype(BF16)
        kk_ref[h] = kh[h]
        gc_ref[h] = gc128
    yield
    for h in heads:
        attn_ref[h] = jnp.where(tril, _mm_nt(q_ref[0, :, hsl[h]], kh[h]) * decay[h], 0.0).astype(BF16)


def _delta_recur(wy, z_ref, nw_ref, o_ref, s_ref):
    u_ref, w_ref, attn_ref, qg_ref, kk_ref, gc_ref = wy
    ch = DN_CHUNK
    hd = DN_HEAD_DIM
    heads = range(DN_HEADS)
    state = [s_ref[h] for h in heads]
    outs = [[] for _ in heads]
    for ci in range(SEQ_TILE // ch):
        rs = slice(ci * ch, (ci + 1) * ch)
        v_new = [u_ref[h, rs, :] - _mm(w_ref[h, rs, :], state[h]) for h in heads]
        o_state = [_mm(qg_ref[h, rs, :], state[h]) for h in heads]
        yield
        for h in heads:
            gc = gc_ref[h, rs, :]
            g_last = gc[ch - 1:ch, :]
            k_dec = kk_ref[h, rs, :] * jnp.exp(g_last - gc)
            outs[h].append(o_state[h] + _mm(attn_ref[h, rs, ci * ch:(ci + 1) * ch], v_new[h]))
            state[h] = state[h] * jnp.exp(g_last) + _mm_tn(k_dec, v_new[h])
        yield

    for h in heads:
        s_ref[h] = state[h]
        hs = slice(h * hd, (h + 1) * hd)
        o = jnp.concatenate(outs[h], axis=0)
        o = o * lax.rsqrt(jnp.mean(o * o, axis=-1, keepdims=True) + EPS) * nw_ref[...]
        o_ref[0, :, hs] = o * _silu(z_ref[0, :, hs])


def _interleave(weighted):
    live = [gen for gen, _ in weighted]
    while live:
        for gen, n in weighted:
            for _ in range(n):
                if gen in live and next(gen, live) is live:
                    live.remove(gen)


DN_BATCH = 2


def _delta_kernel(q_ref, k_ref, v_ref, gb_ref, grow0_ref, grow1_ref, z_ref, nw_ref, o_ref, s_ref,
                  *wy_refs):
    t = pl.program_id(1)
    set_a, set_b = wy_refs[:6], wy_refs[6:]
    grow_refs = (grow0_ref, grow1_ref)

    @pl.when(t == 0)
    def _():
        s_ref[...] = jnp.zeros(s_ref.shape, F32)
        for ref in set_b:
            ref[...] = jnp.zeros(ref.shape, ref.dtype)

    def step(read_set, write_set):
        one = lambda ref, bb: ref.at[pl.ds(bb, 1)]
        work = []
        for bb in range(DN_BATCH):
            work.append((_delta_prepare(one(q_ref, bb), one(k_ref, bb), one(v_ref, bb), one(gb_ref, bb),
                                        grow_refs[bb], [ref.at[bb] for ref in write_set]), 2))
        for bb in range(DN_BATCH):
            work.append((_delta_recur([ref.at[bb] for ref in read_set], one(z_ref, bb), nw_ref,
                                      one(o_ref, bb), s_ref.at[bb]), 1))
        _interleave(work)

    @pl.when(lax.rem(t, 2) == 0)
    def _():
        step(set_b, set_a)

    @pl.when(lax.rem(t, 2) == 1)
    def _():
        step(set_a, set_b)


def _delta_rule(q, k, v, gb, grow, z, norm_w):
    b = q.shape[0]
    nb = DN_BATCH
    assert nb == 2 and b % nb == 0
    ts = SEQ_TILE
    nt = SEQ // ts
    nxt = lambda w: pl.BlockSpec((nb, ts, w), lambda i, t: (i, jnp.minimum(t, nt - 1), 0))
    cur = lambda w: pl.BlockSpec((nb, ts, w), lambda i, t: (i, jnp.maximum(t - 1, 0), 0))
    grow_spec = lambda bb: pl.BlockSpec(
        (8, ts), lambda i, t: (0, (i * nb + bb) * nt + jnp.minimum(t, nt - 1)))
    hd = DN_HEAD_DIM
    wy_set = [pltpu.VMEM((nb, DN_HEADS, ts, hd), F32), pltpu.VMEM((nb, DN_HEADS, ts, hd), BF16),
              pltpu.VMEM((nb, DN_HEADS, ts, ts), BF16), pltpu.VMEM((nb, DN_HEADS, ts, hd), BF16),
              pltpu.VMEM((nb, DN_HEADS, ts, hd), F32), pltpu.VMEM((nb, DN_HEADS, ts, hd), F32)]
    return pl.pallas_call(
        _delta_kernel,
        out_shape=jax.ShapeDtypeStruct((b, SEQ, DN_WIDTH), F32),
        grid=(b // nb, nt + 1),
        in_specs=[
            nxt(DN_WIDTH), nxt(DN_WIDTH), nxt(DN_WIDTH), nxt(LANES),
            grow_spec(0), grow_spec(1),
            cur(DN_WIDTH),
            pl.BlockSpec((1, DN_HEAD_DIM), lambda i, t: (0, 0)),
        ],
        out_specs=cur(DN_WIDTH),
        scratch_shapes=[pltpu.VMEM((nb, DN_HEADS, hd, hd), F32)] + wy_set + wy_set,
        compiler_params=_params(("parallel", "arbitrary")),
        name="delta_rule",
    )(q, k, v, gb, grow, grow, z, norm_w)


CONV_HALO = 32


MIX_COLS = {'qkv': (0, 3 * DN_WIDTH), 'z': (3 * DN_WIDTH, 4 * DN_WIDTH)}
_off = 4 * DN_WIDTH
for _name, _w in (('nq', NSA_WIDTH), ('nkv', 6 * NSA_HEAD_DIM), ('cu', 2 * CONV_WIDTH), ('small', LANES)):
    MIX_COLS[_name] = (_off, _off + _w)
    _off += _w
assert _off == IN_COLS
DN_HALO = 8
MXU_CHUNK = 256


def _front_kernel(x_ref, nw_ref, w_ref, wabt_ref, cw_ref, pcol_ref, prow_ref,
                  dww_ref, dwb_ref, lnw_ref, lnb_ref,
                  q_ref, k_ref, v_ref, gb_ref, grow_ref, z_ref, nq_ref, nkv_ref, small_ref, yc_ref,
                  dn_buf, cf_buf, cf_shift):
    i = pl.program_id(0)
    tm = ROW_TILE
    hd = NSA_HEAD_DIM

    @pl.when(lax.rem(i, SEQ // tm) == 0)
    def _():
        dn_buf[0:DN_HALO, :] = jnp.zeros((DN_HALO, 3 * DN_WIDTH), F32)
        cf_buf[0:CONV_HALO, :] = jnp.zeros((CONV_HALO, CONV_WIDTH), F32)

    xf = x_ref[...]
    hb = (xf * lax.rsqrt(jnp.mean(xf * xf, axis=-1, keepdims=True) + EPS) * nw_ref[...]).astype(BF16)

    def proj(name, lo=0, hi=None):
        c0, c1 = MIX_COLS[name]
        hi = c1 - c0 if hi is None else hi
        return jnp.dot(hb, w_ref[:, c0 + lo:c0 + hi], preferred_element_type=F32)

    def mxu_items():
        cu = proj('cu')
        cf_buf[CONV_HALO:CONV_HALO + tm, :] = cu[:, :CONV_WIDTH] * _sigmoid(cu[:, CONV_WIDTH:])
        yield
        sm = proj('small')
        small_ref[...] = sm
        lane = lax.broadcasted_iota(jnp.int32, sm.shape, 1)
        g_col = -jnp.exp(pcol_ref[0:1, :]) * _softplus(sm + pcol_ref[1:2, :])
        gb_ref[...] = jnp.where(lane < DN_HEADS, g_col, _sigmoid(sm))
        a_t = lax.dot_general(wabt_ref[...], hb, (((1,), (1,)), ((), ())), preferred_element_type=F32)
        g_row = -jnp.exp(prow_ref[:, 0:1]) * _softplus(a_t + prow_ref[:, 1:2])
        grow_ref[...] = g_row[0:8, :]
        yield
        for c0 in range(0, 3 * DN_WIDTH, MXU_CHUNK):
            dn_buf[DN_HALO:DN_HALO + tm, c0:c0 + MXU_CHUNK] = proj('qkv', c0, c0 + MXU_CHUNK)
            yield
        for c0 in range(0, DN_WIDTH, MXU_CHUNK):
            z_ref[:, c0:c0 + MXU_CHUNK] = proj('z', c0, c0 + MXU_CHUNK)
            yield
        res = proj('nq')
        for j in range(NSA_HEADS):
            nq_ref[j] = res[:, j * hd:(j + 1) * hd]
        yield
        res = proj('nkv')
        for j in range(6):
            nkv_ref[j] = res[:, j * hd:(j + 1) * hd]

    def conformer_items():
        span = tm + CONV_HALO - SUBLANES
        for b in range(1, SUBLANES):
            cf_shift[b] = cf_buf[b:b + span, :]
        yield
        base = CONV_HALO - (CONV_KERNEL - 1)
        rows = 64
        for rc in range(tm // rows):
            parts = []
            for cc in range(CONV_WIDTH // LANES):
                cs = slice(cc * LANES, (cc + 1) * LANES)
                acc = None
                for j in range(CONV_KERNEL):
                    start = base + rc * rows + j
                    b = start % SUBLANES
                    a0 = start - b
                    win = cf_buf[a0:a0 + rows, cs] if b == 0 else cf_shift[b, a0:a0 + rows, cs]
                    term = dww_ref[j:j + 1, cs] * win
                    acc = term if acc is None else acc + term
                parts.append(acc)
            h = jnp.concatenate(parts, axis=1) + dwb_ref[...]
            mu = jnp.mean(h, axis=-1, keepdims=True)
            var = jnp.mean(jnp.square(h - mu), axis=-1, keepdims=True)
            hn = (h - mu) * lax.rsqrt(var + EPS) * lnw_ref[...] + lnb_ref[...]
            yc_ref[rc * rows:(rc + 1) * rows, :] = _silu(hn)
            yield
        cf_buf[0:CONV_HALO, :] = cf_buf[tm:tm + CONV_HALO, :]

    def deltanet_items():
        rows = tm // 2
        for c in range(3 * DN_HEADS):
            cs = slice(c * LANES, (c + 1) * LANES)
            for r0 in range(0, tm, rows):
                lo = DN_HALO - (DN_CONV - 1) + r0
                acc = cw_ref[0:1, cs] * dn_buf[lo:lo + rows, cs]
                for j in range(1, DN_CONV):
                    acc = acc + cw_ref[j:j + 1, cs] * dn_buf[lo + j:lo + j + rows, cs]
                y = _silu(acc)
                if c < 2 * DN_HEADS:
                    y = y * lax.rsqrt(jnp.sum(y * y, axis=-1, keepdims=True) + EPS)
                rs = slice(r0, r0 + rows)
                if c < DN_HEADS:
                    q_ref[rs, cs] = y * DN_HEAD_DIM ** -0.5
                elif c < 2 * DN_HEADS:
                    k_ref[rs, (c - DN_HEADS) * LANES:(c - DN_HEADS + 1) * LANES] = y
                else:
                    v_ref[rs, (c - 2 * DN_HEADS) * LANES:(c - 2 * DN_HEADS + 1) * LANES] = y
            dn_buf[0:DN_HALO, cs] = dn_buf[tm:tm + DN_HALO, cs]
            yield

    mxu, cf, dn = mxu_items(), conformer_items(), deltanet_items()
    next(mxu)
    next(mxu)
    next(cf)
    next(mxu)
    _interleave([(mxu, 1), (dn, 2), (cf, 2)])


def _front(x2, norm_w, w_cat, w_abt, conv_w, pcol, prow, dw_w, dw_b, ln_w, ln_b):
    n = x2.shape[0]
    tm = ROW_TILE
    hd = NSA_HEAD_DIM
    row = lambda w: pl.BlockSpec((tm, w), lambda i: (i, 0))
    full = lambda shape: pl.BlockSpec(shape, lambda i: (0,) * len(shape))
    pieces = lambda k: pl.BlockSpec((k, tm, hd), lambda i: (0, i, 0))
    return pl.pallas_call(
        _front_kernel,
        out_shape=[jax.ShapeDtypeStruct((n, DN_WIDTH), F32)] * 3
        + [jax.ShapeDtypeStruct((n, LANES), F32), jax.ShapeDtypeStruct((8, n), F32),
           jax.ShapeDtypeStruct((n, DN_WIDTH), F32),
           jax.ShapeDtypeStruct((NSA_HEADS, n, hd), F32), jax.ShapeDtypeStruct((6, n, hd), F32),
           jax.ShapeDtypeStruct((n, LANES), F32), jax.ShapeDtypeStruct((n, CONV_WIDTH), F32)],
        grid=(n // tm,),
        in_specs=[row(D_MODEL), full((1, D_MODEL)), full((D_MODEL, IN_COLS)), full((16, D_MODEL)),
                  full((DN_CONV, 3 * DN_WIDTH)), full((8, LANES)), full((16, LANES)),
                  full((CONV_KERNEL, CONV_WIDTH)), full((1, CONV_WIDTH)), full((1, CONV_WIDTH)),
                  full((1, CONV_WIDTH))],
        out_specs=[row(DN_WIDTH)] * 3
        + [row(LANES), pl.BlockSpec((8, tm), lambda i: (0, i)), row(DN_WIDTH),
           pieces(NSA_HEADS), pieces(6), row(LANES), row(CONV_WIDTH)],
        scratch_shapes=[pltpu.VMEM((tm + DN_HALO, 3 * DN_WIDTH), F32),
                        pltpu.VMEM((tm + CONV_HALO, CONV_WIDTH), F32),
                        pltpu.VMEM((SUBLANES, tm + CONV_HALO - SUBLANES, CONV_WIDTH), F32)],
        compiler_params=_params(("arbitrary",)),
        name="front",
    )(x2, norm_w, w_cat, w_abt, conv_w, pcol, prow, dw_w, dw_b, ln_w, ln_b)


def _rms_rows(x, w):
    return x * lax.rsqrt(jnp.mean(x * x, axis=-1, keepdims=True) + EPS) * w


def _nsa_prep_kernel(kc_ref, vc_ref, ks_ref, kw_ref, pos_ref, w1_ref, w2_ref, knw_ref,
                     kcmp_ref, vcmp_ref, ksn_ref, kwn_ref):
    hd = NSA_HEAD_DIM

    def compress(x_ref, i):
        u_lo = jnp.zeros((N_CMP_PAD, CMP_HIDDEN), F32)
        u_hi = jnp.zeros((N_CMP_PAD, CMP_HIDDEN), F32)
        for r in range(CMP_STRIDE):
            xr = x_ref[pl.ds(r, N_CMP_PAD, stride=CMP_STRIDE), :]
            lo, hi = r, CMP_STRIDE + r
            u_lo = u_lo + _mm(xr + pos_ref[i, lo:lo + 1, :], w1_ref[i, lo * hd:(lo + 1) * hd, :])
            u_hi = u_hi + _mm(xr + pos_ref[i, hi:hi + 1, :], w1_ref[i, hi * hd:(hi + 1) * hd, :])
        hid = _silu(u_lo + pltpu.roll(u_hi, N_CMP_PAD - 1, axis=0))
        return _mm(hid, w2_ref[i])

    kcmp_ref[0] = _rms_rows(compress(kc_ref, 0), knw_ref[0:1, :])
    vcmp_ref[0] = compress(vc_ref, 1)
    ksn_ref[0] = _rms_rows(ks_ref[...], knw_ref[1:2, :])
    kwn_ref[0] = _rms_rows(kw_ref[...], knw_ref[2:3, :])


def _nsa_prep(kv6, pos, w1, w2, knw):
    b = kv6.shape[1]
    hd = NSA_HEAD_DIM
    full = lambda shape: pl.BlockSpec(shape, lambda i: (0,) * len(shape))
    piece = lambda j: pl.BlockSpec((None, None, SEQ, hd), lambda i: (j, i, 0, 0))
    bspec = lambda r, w: pl.BlockSpec((1, r, w), lambda i: (i, 0, 0))
    return pl.pallas_call(
        _nsa_prep_kernel,
        out_shape=[jax.ShapeDtypeStruct((b, N_CMP_PAD, hd), F32)] * 2
        + [jax.ShapeDtypeStruct((b, SEQ, hd), F32)] * 2,
        grid=(b,),
        in_specs=[
            piece(0), piece(1), piece(2), piece(4),
            full((2, CMP_LEN, hd)), full((2, CMP_LEN * hd, CMP_HIDDEN)), full((2, CMP_HIDDEN, hd)),
            full((3, hd)),
        ],
        out_specs=[bspec(N_CMP_PAD, hd)] * 2 + [bspec(SEQ, hd)] * 2,
        compiler_params=_params(("parallel",)),
        name="nsa_prep",
    )(kv6, kv6, kv6, kv6, pos, w1, w2, knw)


def _nsa_attn_kernel(q_ref, qnw_ref, kcmp_ref, vcmp_ref, ks_ref, vs_ref, kw_ref, vw_ref,
                     gate_ref, bcmp_ref, btab_ref, o_ref,
                     madd_ref, s_ref, ksb_ref, kwb_ref, vsa_ref, vwa_ref):
    i = pl.program_id(1)
    tq = Q_TILE
    nh = NSA_HEADS
    hd = NSA_HEAD_DIM
    t0 = i * tq

    @pl.when(i == 0)
    def _():
        ones = jnp.ones((SEQ, hd), BF16)
        ksb_ref[...] = ks_ref[0].astype(BF16)
        kwb_ref[...] = kw_ref[0].astype(BF16)
        vsa_ref[...] = jnp.concatenate([vs_ref[...].astype(BF16), ones], axis=1)
        vwa_ref[...] = jnp.concatenate([vw_ref[...].astype(BF16), ones], axis=1)

    qs = jnp.concatenate(
        [_rms_rows(q_ref[h], qnw_ref[...]) * hd ** -0.5 for h in range(nh)], axis=0).astype(BF16)

    row = lax.broadcasted_iota(jnp.int32, (tq, LANES), 0)
    lane = lax.broadcasted_iota(jnp.int32, (tq, LANES), 1)
    qpos = t0 + row

    branch_out = {}

    def select_items():
        s_all = _mm_nt(qs, kcmp_ref[0])
        cmp_valid = (qpos >= lane * CMP_STRIDE + (CMP_LEN - 1)) & (lane < N_CMP_PAD - 1)
        p_rows = []
        p_sum = jnp.zeros((tq, LANES), F32)
        for h in range(nh):
            s = jnp.where(cmp_valid, s_all[h * tq:(h + 1) * tq] + bcmp_ref[h], NEG_INF)
            m = jnp.max(s, axis=-1, keepdims=True)
            p = jnp.where(cmp_valid, jnp.exp(s - m), 0.0)
            l = jnp.sum(p, axis=-1, keepdims=True)
            p = p * jnp.where(l > 0.0, 1.0 / l, 0.0)
            p_rows.append(p)
            p_sum = p_sum + p
            yield
        branch_out['cmp'] = _mm(jnp.concatenate(p_rows, axis=0), vcmp_ref[0])

        ss = lax.broadcasted_iota(jnp.int32, (N_SLC, N_CMP_PAD), 0)
        jj = lax.broadcasted_iota(jnp.int32, (N_SLC, N_CMP_PAD), 1)
        overlap_t = ((jj * CMP_STRIDE < ss * SLC_BLOCK + SLC_BLOCK)
                     & (jj * CMP_STRIDE + CMP_LEN > ss * SLC_BLOCK) & (jj < N_CMP_PAD - 1))
        overlap_t = jnp.where(overlap_t, 1.0, 0.0).astype(BF16)
        nt = lambda a, b_: lax.dot_general(a, b_, (((1,), (1,)), ((), ())), preferred_element_type=F32)
        p_hi, p_mid, p_lo = _split3(p_sum)
        imp = nt(overlap_t, p_hi) + nt(overlap_t, p_mid) + nt(overlap_t, p_lo)
        blk = lax.broadcasted_iota(jnp.int32, (N_SLC, tq), 0)
        cur = (t0 + lax.broadcasted_iota(jnp.int32, (N_SLC, tq), 1)) // SLC_BLOCK
        causal_blk = blk <= cur
        forced = (blk == 0) | (blk == cur) | (blk == cur - 1)
        imp = jnp.where(causal_blk & forced, FORCE, jnp.where(causal_blk, imp, -1.0))
        yield
        rank = jnp.zeros((N_SLC, tq), jnp.int32)
        for s2 in range(N_SLC):
            other = jnp.broadcast_to(imp[s2:s2 + 1, :], (N_SLC, tq))
            beats = (other > imp) | ((other == imp) & (blk > s2))
            rank = rank + jnp.where(beats, 1, 0)
            if s2 % 8 == 7:
                yield
        sel_t = jnp.where(rank < SLC_TOP_N, 1.0, 0.0).astype(BF16)
        er = lax.broadcasted_iota(jnp.int32, (N_SLC, SEQ), 0)
        ec = lax.broadcasted_iota(jnp.int32, (N_SLC, SEQ), 1)
        expand = jnp.where(ec // SLC_BLOCK == er, 1.0, 0.0).astype(BF16)
        key = lax.broadcasted_iota(jnp.int32, (tq, SEQ), 1)
        qall = t0 + lax.broadcasted_iota(jnp.int32, (tq, SEQ), 0)
        chosen = (_mm_tn(sel_t, expand) > 0.5) & (qall >= key)
        madd_ref[...] = jnp.where(chosen, 0.0, NEG_INF)

    neg = jnp.full((nh * tq, tq), NEG_INF, F32)
    zero = jnp.zeros((nh * tq, LANES), F32)

    def scores(k_tile, bias_idx, add):
        s_t = _mm_nt(qs, k_tile)
        return jnp.concatenate(
            [s_t[h * tq:(h + 1) * tq] + (btab_ref[h, bias_idx] + add) for h in range(nh)], axis=0)

    def normalise(acc):
        return acc[:, :hd] * (1.0 / _bcast_col(acc, hd, hd))

    def window_items():
        kcol = lax.broadcasted_iota(jnp.int32, (tq, tq), 1)
        qrow = t0 + lax.broadcasted_iota(jnp.int32, (tq, tq), 0)
        n_win = WINDOW // tq + 1
        starts = []
        m_run = neg
        for j in range(n_win):
            d = n_win - 1 - j
            k0 = pl.multiple_of(jnp.maximum(i - d, 0) * tq, tq)
            starts.append(k0)
            dist = qrow - (k0 + kcol)
            ok = (dist >= 0) & (dist < WINDOW) & (i >= d)
            s_m = scores(kwb_ref[pl.ds(k0, tq), :], min(d, 2), jnp.where(ok, 0.0, NEG_INF))
            s_ref[:, j * tq:(j + 1) * tq] = s_m
            m_run = jnp.maximum(m_run, s_m)
            yield
        m_win = jnp.broadcast_to(jnp.max(m_run, axis=-1, keepdims=True), (nh * tq, tq))
        acc = zero
        for j in range(n_win):
            p = jnp.exp(s_ref[:, j * tq:(j + 1) * tq] - m_win).astype(BF16)
            acc = acc + jnp.dot(p, vwa_ref[pl.ds(starts[j], tq), :], preferred_element_type=F32)
            yield
        branch_out['win'] = normalise(acc)

    _interleave([(select_items(), 1), (window_items(), 1)])
    o_cmp, o_win = branch_out['cmp'], branch_out['win']

    n_groups = (i + SLC_GROUP) // SLC_GROUP

    def slc_pass1(j, m_run):
        for u in range(SLC_GROUP):
            kt = SLC_GROUP * j + u
            k0 = pl.multiple_of(kt * tq, tq)
            s_m = scores(ksb_ref[pl.ds(k0, tq), :], jnp.clip(i - kt, 0, 2), madd_ref[:, pl.ds(k0, tq)])
            s_ref[:, pl.ds(k0, tq)] = s_m
            m_run = jnp.maximum(m_run, s_m)
        return m_run

    m_slc = jnp.broadcast_to(
        jnp.max(lax.fori_loop(0, n_groups, slc_pass1, neg), axis=-1, keepdims=True), (nh * tq, tq))

    def slc_pass2(j, acc):
        for u in range(SLC_GROUP):
            k0 = pl.multiple_of((SLC_GROUP * j + u) * tq, tq)
            p = jnp.exp(s_ref[:, pl.ds(k0, tq)] - m_slc).astype(BF16)
            acc = acc + jnp.dot(p, vsa_ref[pl.ds(k0, tq), :], preferred_element_type=F32)
        return acc

    o_slc = normalise(lax.fori_loop(0, n_groups, slc_pass2, zero))

    gates = _sigmoid(gate_ref[0])
    outs = []
    for h in range(nh):
        hs = slice(h * tq, (h + 1) * tq)
        g0 = 2 * DN_HEADS + 3 * h
        outs.append(_bcast_col(gates, g0, hd) * o_cmp[hs]
                    + _bcast_col(gates, g0 + 1, hd) * o_slc[hs]
                    + _bcast_col(gates, g0 + 2, hd) * o_win[hs])
    o_ref[...] = jnp.concatenate(outs, axis=1)


def _nsa_attn(q, qnw, kcmp, vcmp, ks, kv6, kw, small, bias_cmp, btab):
    b = kv6.shape[1]
    tq = Q_TILE
    nq = SEQ // tq
    nh, hd = NSA_HEADS, NSA_HEAD_DIM
    full = lambda shape: pl.BlockSpec(shape, lambda bi, i: (0,) * len(shape))
    per_b = lambda r: pl.BlockSpec((1, r, hd), lambda bi, i: (bi, 0, 0))
    piece = lambda j: pl.BlockSpec((None, None, SEQ, hd), lambda bi, i: (j, bi, 0, 0))
    heads = pl.BlockSpec((nh, tq, hd), lambda bi, i: (0, bi * nq + i, 0))
    return pl.pallas_call(
        _nsa_attn_kernel,
        out_shape=jax.ShapeDtypeStruct((b * SEQ, NSA_WIDTH), F32),
        grid=(b, nq),
        in_specs=[
            heads,
            full((1, hd)),
            per_b(N_CMP_PAD), per_b(N_CMP_PAD), per_b(SEQ), piece(3), per_b(SEQ), piece(5),
            pl.BlockSpec((1, tq, LANES), lambda bi, i: (bi, i, 0)),
            pl.BlockSpec((nh, tq, N_CMP_PAD), lambda bi, i: (0, i, 0)),
            full((nh, 3, tq, tq)),
        ],
        out_specs=pl.BlockSpec((tq, NSA_WIDTH), lambda bi, i: (bi * nq + i, 0)),
        scratch_shapes=[pltpu.VMEM((tq, SEQ), F32), pltpu.VMEM((nh * tq, SEQ), F32),
                        pltpu.VMEM((SEQ, hd), BF16), pltpu.VMEM((SEQ, hd), BF16),
                        pltpu.VMEM((SEQ, 2 * hd), BF16), pltpu.VMEM((SEQ, 2 * hd), BF16)],
        compiler_params=_params(("parallel", "arbitrary")),
        name="nsa_attn",
    )(q, qnw, kcmp, vcmp, ks, kv6, kw, kv6, small, bias_cmp, btab)


def _out_router_kernel(ya_ref, yb_ref, yc_ref, x_ref, wo_ref, fnw_ref, rw_ref, rb_ref,
                       xo_ref, h_ref, idx_ref, wt_ref, hist_ref):
    wa = DN_WIDTH
    wb = wa + NSA_WIDTH
    y = (jnp.dot(ya_ref[...].astype(BF16), wo_ref[0:wa, :], preferred_element_type=F32)
         + jnp.dot(yb_ref[...].astype(BF16), wo_ref[wa:wb, :], preferred_element_type=F32)
         + jnp.dot(yc_ref[...].astype(BF16), wo_ref[wb:, :], preferred_element_type=F32))
    xn = x_ref[...] + y
    xo_ref[...] = xn
    h = xn * lax.rsqrt(jnp.mean(xn * xn, axis=-1, keepdims=True) + EPS) * fnw_ref[...]
    _store_tile_rows(h_ref, h, h.shape[0])
    h_hi, h_lo = _hi_lo(h)
    r_hi = jnp.dot(h_hi, rw_ref[...], preferred_element_type=F32)
    r_lo = jnp.dot(h_lo, rw_ref[...], preferred_element_type=F32)
    logits = ((r_hi[:, :LANES] + r_hi[:, LANES:]) + (r_lo[:, :LANES] + r_lo[:, LANES:])) + rb_ref[...]
    lane = lax.broadcasted_iota(jnp.int32, logits.shape, 1)
    vals, idxs = [], []
    for _ in range(TOP_K):
        m = jnp.max(logits, axis=-1, keepdims=True)
        ix = jnp.min(jnp.where(logits == m, lane, LANES), axis=-1, keepdims=True)
        vals.append(m)
        idxs.append(ix)
        logits = jnp.where(lane == ix, -jnp.inf, logits)
    es = [jnp.exp(v - vals[0]) for v in vals]
    inv = 1.0 / (es[0] + es[1] + es[2] + es[3])
    idx_out = jnp.zeros(lane.shape, jnp.int32)
    wt_out = jnp.zeros(lane.shape, F32)
    chosen = jnp.zeros(lane.shape, F32)
    for k in range(TOP_K):
        idx_out = jnp.where(lane == k, idxs[k], idx_out)
        wt_out = jnp.where(lane == k, es[k] * inv, wt_out)
        chosen = chosen + jnp.where(lane == idxs[k], 1.0, 0.0)
    idx_ref[...] = idx_out
    wt_ref[...] = wt_out
    hist_ref[0] = jnp.broadcast_to(jnp.sum(chosen, axis=0, keepdims=True), (8, LANES))


def _out_router(ya, yb, yc, x2, w_out, fnw, rw, rb):
    n = x2.shape[0]
    tm = ROW_TILE
    row = lambda w: pl.BlockSpec((tm, w), lambda i: (i, 0))
    full = lambda shape: pl.BlockSpec(shape, lambda i: (0,) * len(shape))
    return pl.pallas_call(
        _out_router_kernel,
        out_shape=[jax.ShapeDtypeStruct((n, D_MODEL), F32), jax.ShapeDtypeStruct((n * SUBLANES, LANES), F32),
                   jax.ShapeDtypeStruct((n, LANES), jnp.int32), jax.ShapeDtypeStruct((n, LANES), F32),
                   jax.ShapeDtypeStruct((n // tm, 8, LANES), F32)],
        grid=(n // tm,),
        in_specs=[row(DN_WIDTH), row(NSA_WIDTH), row(CONV_WIDTH), row(D_MODEL),
                  full((D_MODEL, D_MODEL)), full((1, D_MODEL)), full((D_MODEL, 2 * LANES)), full((1, LANES))],
        out_specs=[row(D_MODEL), pl.BlockSpec((tm * SUBLANES, LANES), lambda i: (i, 0)), row(LANES), row(LANES),
                   pl.BlockSpec((1, 8, LANES), lambda i: (i, 0, 0))],
        compiler_params=_params(("parallel",)),
        name="out_router",
    )(ya, yb, yc, x2, w_out, fnw, rw, rb)


def _slots_kernel(idx_ref, base_ref, slot_ref):
    tm = ROW_TILE
    idx = idx_ref[...]
    lane = lax.broadcasted_iota(jnp.int32, (tm, LANES), 1)
    onehots = [jnp.where(lane == _bcast_col(idx, k, LANES), 1.0, 0.0) for k in range(TOP_K)]
    cnt = (onehots[0] + onehots[1]) + (onehots[2] + onehots[3])
    r = lax.broadcasted_iota(jnp.int32, (tm, tm), 0)
    c = lax.broadcasted_iota(jnp.int32, (tm, tm), 1)
    earlier = jnp.where(r > c, 1.0, 0.0).astype(BF16)
    rank = jnp.dot(earlier, cnt.astype(BF16), preferred_element_type=F32) + base_ref[0, 0:1, :]
    out = jnp.zeros((tm, LANES), F32)
    for k in range(TOP_K):
        out = jnp.where(lane == k, jnp.sum(rank * onehots[k], axis=-1, keepdims=True), out)
    slot_ref[...] = (out * SUBLANES).astype(jnp.int32)


def _slots(idx, base3):
    n = idx.shape[0]
    tm = ROW_TILE
    return pl.pallas_call(
        _slots_kernel,
        out_shape=jax.ShapeDtypeStruct((n, LANES), jnp.int32),
        grid=(n // tm,),
        in_specs=[pl.BlockSpec((tm, LANES), lambda i: (i, 0)),
                  pl.BlockSpec((1, 8, LANES), lambda i: (i, 0, 0))],
        out_specs=pl.BlockSpec((tm, LANES), lambda i: (i, 0)),
        compiler_params=_params(("parallel",)),
        name="moe_slots",
    )(idx, base3)


def _dispatch_kernel(slot_ref, zoff_ref, nv_ref, h_ref, xs_hbm, zbuf, sem, zsem):
    i = pl.program_id(0)
    tm = DISPATCH_TILE
    base = i * tm

    @pl.when(i == 0)
    def _():
        zbuf[...] = jnp.zeros(zbuf.shape, F32)
        blk = MOE_TILE * SUBLANES
        n_blocks = xs_hbm.shape[0] // blk

        def fill(row0):
            return pltpu.make_async_copy(
                zbuf, xs_hbm.at[pl.ds(pl.multiple_of(row0 * SUBLANES, blk), blk)], zsem)

        def fill_region_end(e, c):
            fill(zoff_ref[e]).start()
            return c

        def fill_tail(b, c):
            fill(b * MOE_TILE).start()
            return c

        def wait_fill(b, c):
            fill(0).wait()
            return c

        lax.fori_loop(0, N_EXPERTS, fill_region_end, 0)
        lax.fori_loop(nv_ref[0], n_blocks, fill_tail, 0)
        lax.fori_loop(0, N_EXPERTS + n_blocks - nv_ref[0], wait_fill, 0)

    def body(r, c):
        src = h_ref.at[pl.ds(pl.multiple_of(r * SUBLANES, SUBLANES), SUBLANES)]
        for k in range(TOP_K):
            row0 = pl.multiple_of(slot_ref[(base + r) * TOP_K + k], SUBLANES)
            pltpu.make_async_copy(src, xs_hbm.at[pl.ds(row0, SUBLANES)], sem).start(priority=k % 2)
        return c

    lax.fori_loop(0, tm, body, 0, unroll=8)
    done = xs_hbm.at[pl.ds(0, TOP_K * tm * SUBLANES)]
    pltpu.make_async_copy(done, done, sem).wait()


def _dispatch(slot_flat, zero_off, n_valid, h, n_slots):
    n = h.shape[0] // SUBLANES
    tm = DISPATCH_TILE
    return pl.pallas_call(
        _dispatch_kernel,
        out_shape=jax.ShapeDtypeStruct((n_slots * SUBLANES, LANES), F32),
        grid_spec=pltpu.PrefetchScalarGridSpec(
            num_scalar_prefetch=3,
            grid=(n // tm,),
            in_specs=[pl.BlockSpec((tm * SUBLANES, LANES), lambda i, s, z, nv: (i, 0))],
            out_specs=pl.BlockSpec(memory_space=pl.ANY),
            scratch_shapes=[pltpu.VMEM((MOE_TILE * SUBLANES, LANES), F32),
                            pltpu.SemaphoreType.DMA, pltpu.SemaphoreType.DMA],
        ),
        compiler_params=_params(("arbitrary",)),
        name="moe_dispatch",
    )(slot_flat, zero_off, n_valid, h)


def _expert_kernel(be_ref, nv_ref, xs_ref, wgu_ref, bgu_ref, wd_ref, bd_ref, o_ref, wgu_bf, wd_bf):
    i = pl.program_id(0)
    new_expert = (i == 0) | (be_ref[i] != be_ref[jnp.maximum(i - 1, 0)])

    @pl.when((i < nv_ref[0]) & new_expert)
    def _():
        step = 4 * LANES
        for c0 in range(0, 2 * D_FF, step):
            wgu_bf[:, c0:c0 + step] = wgu_ref[:, c0:c0 + step].astype(BF16)
        for c0 in range(0, D_MODEL, step):
            wd_bf[:, c0:c0 + step] = wd_ref[:, c0:c0 + step].astype(BF16)

    @pl.when(i < nv_ref[0])
    def _():
        xb = _load_tile_rows(xs_ref, MOE_TILE).astype(BF16)
        gu = jnp.dot(xb, wgu_bf[...], preferred_element_type=F32) + bgu_ref[...]
        gate = jnp.minimum(gu[:, :D_FF], SWIGLU_LIMIT)
        up = jnp.clip(gu[:, D_FF:], -SWIGLU_LIMIT, SWIGLU_LIMIT)
        act = (up + 1.0) * gate * _sigmoid(SWIGLU_ALPHA * gate)
        y = jnp.dot(act.astype(BF16), wd_bf[...], preferred_element_type=F32) + bd_ref[...]
        _store_tile_rows(o_ref, y, MOE_TILE)

    @pl.when(i >= nv_ref[0])
    def _():
        o_ref[...] = jnp.zeros(o_ref.shape, F32)


def _expert_ffn(layer, block_expert, n_valid, xs, wgu, bgu, wd, bd):
    n_slots = xs.shape[0] // SUBLANES
    tm = MOE_TILE
    return pl.pallas_call(
        _expert_kernel,
        out_shape=jax.ShapeDtypeStruct(xs.shape, F32),
        grid_spec=pltpu.PrefetchScalarGridSpec(
            num_scalar_prefetch=2,
            grid=(n_slots // tm,),
            in_specs=[
                pl.BlockSpec((tm * SUBLANES, LANES), lambda i, be, nv: (i, 0)),
                pl.BlockSpec((None, None, D_MODEL, 2 * D_FF), lambda i, be, nv: (layer, be[i], 0, 0)),
                pl.BlockSpec((None, None, 1, 2 * D_FF), lambda i, be, nv: (layer, be[i], 0, 0)),
                pl.BlockSpec((None, None, D_FF, D_MODEL), lambda i, be, nv: (layer, be[i], 0, 0)),
                pl.BlockSpec((None, None, 1, D_MODEL), lambda i, be, nv: (layer, be[i], 0, 0)),
            ],
            out_specs=pl.BlockSpec((tm * SUBLANES, LANES), lambda i, be, nv: (i, 0)),
            scratch_shapes=[pltpu.VMEM((D_MODEL, 2 * D_FF), BF16), pltpu.VMEM((D_FF, D_MODEL), BF16)],
        ),
        compiler_params=pltpu.CompilerParams(dimension_semantics=("arbitrary",),
                                             vmem_limit_bytes=EXPERT_VMEM_LIMIT_BYTES),
        name="expert_ffn",
    )(block_expert, n_valid, xs, wgu, bgu, wd, bd)


def _combine_kernel(slot_ref, ys_hbm, x_ref, wt_ref, o_ref, buf, acc_ref, sems):
    i = pl.program_id(0)
    tc = COMBINE_TILE
    nsteps = pl.num_programs(0)
    cur = lax.rem(i, 2)

    def issue(step, par):
        base = step * tc

        def body(r, c):
            dst0 = pl.multiple_of(r * SUBLANES, SUBLANES)
            for k in range(TOP_K):
                row0 = pl.multiple_of(slot_ref[(base + r) * TOP_K + k], SUBLANES)
                pltpu.make_async_copy(ys_hbm.at[pl.ds(row0, SUBLANES)],
                                      buf.at[par, k, pl.ds(dst0, SUBLANES)],
                                      sems.at[par]).start(priority=k % 2)
            return c

        lax.fori_loop(0, tc, body, 0, unroll=8)

    @pl.when(i == 0)
    def _():
        issue(0, 0)

    @pl.when(i + 1 < nsteps)
    def _():
        issue(i + 1, 1 - cur)

    pltpu.make_async_copy(buf.at[cur], buf.at[cur], sems.at[cur]).wait()
    wt = wt_ref[...]
    acc = None
    for k in range(TOP_K):
        w_tiles = jnp.broadcast_to(_bcast_col(wt, k, LANES)[:, None, :], (tc, SUBLANES, LANES))
        term = w_tiles.reshape(tc * SUBLANES, LANES) * buf[cur, k]
        acc = term if acc is None else acc + term
    acc_ref[...] = acc
    for s in range(ROW_CHUNKS):
        cs = slice(s * LANES, (s + 1) * LANES)
        o_ref[:, cs] = x_ref[:, cs] + acc_ref[pl.ds(s, tc, stride=SUBLANES), :]


def _combine(slot_flat, ys, x2, wt):
    n = x2.shape[0]
    tc = COMBINE_TILE
    return pl.pallas_call(
        _combine_kernel,
        out_shape=jax.ShapeDtypeStruct((n, D_MODEL), F32),
        grid_spec=pltpu.PrefetchScalarGridSpec(
            num_scalar_prefetch=1,
            grid=(n // tc,),
            in_specs=[pl.BlockSpec(memory_space=pl.ANY),
                      pl.BlockSpec((tc, D_MODEL), lambda i, s: (i, 0)),
                      pl.BlockSpec((tc, LANES), lambda i, s: (i, 0))],
            out_specs=pl.BlockSpec((tc, D_MODEL), lambda i, s: (i, 0)),
            scratch_shapes=[pltpu.VMEM((2, TOP_K, tc * SUBLANES, LANES), F32),
                            pltpu.VMEM((tc * SUBLANES, LANES), F32), pltpu.SemaphoreType.DMA((2,))],
        ),
        compiler_params=_params(("arbitrary",)),
        name="moe_combine",
    )(slot_flat, ys, x2, wt)


def _moe(layer, h, x2, idx, wt, hist3, wgu, bgu, wd, bd):
    n = x2.shape[0]
    tm = MOE_TILE
    n_slots = n * TOP_K + N_EXPERTS * tm
    n_blocks = n_slots // tm
    hist = hist3[:, 0, :N_EXPERTS]
    counts = jnp.sum(hist, axis=0).astype(jnp.int32)
    padded = (counts + tm - 1) // tm * tm
    pad_end = jnp.cumsum(padded)
    pad_start = pad_end - padded
    tile_base = pad_start[None, :].astype(F32) + (jnp.cumsum(hist, axis=0) - hist)
    base3 = jnp.broadcast_to(jnp.pad(tile_base, ((0, 0), (0, LANES - N_EXPERTS)))[:, None, :],
                             (hist.shape[0], 8, LANES))
    blk0 = jnp.arange(n_blocks) * tm
    block_expert = jnp.minimum(jnp.sum(blk0[:, None] >= pad_end[None, :], axis=1), N_EXPERTS - 1).astype(jnp.int32)
    n_valid = (pad_end[-1:] // tm).astype(jnp.int32)
    zero_off = jnp.maximum(pad_end - tm, 0).astype(jnp.int32)

    slot_flat = _slots(idx, base3)[:, :TOP_K].reshape(-1)
    xs = _dispatch(slot_flat, zero_off, n_valid, h, n_slots)
    ys = _expert_ffn(layer, block_expert, n_valid, xs, wgu, bgu, wd, bd)
    return _combine(slot_flat, ys, x2, wt)


def _t5_bucket(dist):
    n = jnp.maximum(dist, 0)
    max_exact = REL_BUCKETS // 2
    nf = jnp.maximum(n, 1).astype(F32)
    large = max_exact + (jnp.log(nf / max_exact) / math.log(REL_MAX_DIST / max_exact)
                         * (REL_BUCKETS - max_exact)).astype(jnp.int32)
    large = jnp.minimum(large, REL_BUCKETS - 1)
    return jnp.where(n < max_exact, n, large)


def _bias_tables(rel_bias):
    rel_bias = rel_bias.astype(F32)
    tq = Q_TILE

    def lookup(dist):
        bucket = _t5_bucket(dist)
        out = jnp.zeros((NSA_HEADS,) + dist.shape, F32)
        for bk in range(REL_BUCKETS):
            out = jnp.where(bucket[None] == bk, rel_bias[bk].reshape((NSA_HEADS,) + (1,) * dist.ndim), out)
        return out

    t_pos = jnp.arange(SEQ)
    cmp_end = jnp.arange(N_CMP_PAD) * CMP_STRIDE + CMP_LEN - 1
    bias_cmp = lookup(t_pos[:, None] - cmp_end[None, :])
    rr = jnp.arange(tq)[:, None] - jnp.arange(tq)[None, :]
    btab = lookup(jnp.stack([rr, rr + tq, rr + 2 * tq]))
    return bias_cmp, btab


def _layer(layer, x2, b, p, experts, bias_cmp, btab):
    n = x2.shape[0]
    w = p['w_in']
    o_a = 4 * DN_WIDTH
    o_q = o_a + 2 * DN_HEADS
    o_kv = o_q + NSA_WIDTH
    o_g = o_kv + 6 * NSA_HEAD_DIM
    o_u = o_g + 3 * NSA_HEADS
    w_small = jnp.concatenate([w[:, o_a:o_q], w[:, o_g:o_u],
                               jnp.zeros((D_MODEL, LANES - 2 * DN_HEADS - 3 * NSA_HEADS), F32)], axis=1)
    w_cat = jnp.concatenate([w[:, :o_a], w[:, o_q:o_kv], w[:, o_kv:o_g], w[:, o_u:], w_small],
                            axis=1).astype(BF16)
    w_abt = jnp.concatenate([w[:, o_a:o_q].T, jnp.zeros((8, D_MODEL), F32)], axis=0).astype(BF16)
    pcol = jnp.zeros((8, LANES), F32).at[0, :DN_HEADS].set(p['dn_a_log']).at[1, :DN_HEADS].set(p['dn_dt_bias'])
    prow = jnp.zeros((16, LANES), F32).at[:DN_HEADS, 0].set(p['dn_a_log']).at[:DN_HEADS, 1].set(p['dn_dt_bias'])
    dq, dk, dv, gb, grow, z, nq, nkv, small, y_c = _front(
        x2, p['attn_norm_w'][None, :], w_cat, w_abt, p['dn_conv_w'], pcol, prow,
        p['conv_dw_w'], p['conv_dw_b'][None, :], p['conv_ln_w'][None, :], p['conv_ln_b'][None, :])
    small3 = small.reshape(b, SEQ, LANES)

    seq = lambda a: a.reshape(b, SEQ, a.shape[-1])
    y_a = _delta_rule(seq(dq), seq(dk), seq(dv), seq(gb), grow, seq(z), p['dn_norm_w'][None, :])

    kv6 = nkv.reshape(6, b, SEQ, NSA_HEAD_DIM)
    kcmp, vcmp, ksn, kwn = _nsa_prep(kv6, p['nsa_cmp_pos'], p['nsa_cmp_w1'].astype(BF16),
                                     p['nsa_cmp_w2'].astype(BF16), p['nsa_k_norm_w'])
    y_b = _nsa_attn(nq, p['nsa_q_norm_w'][None, :], kcmp, vcmp, ksn, kv6, kwn, small3, bias_cmp, btab)

    rw = jnp.concatenate(_hi_lo(jnp.pad(p['router_w'], ((0, 0), (0, LANES - N_EXPERTS)))), axis=1)
    rb = jnp.concatenate([p['router_b'], jnp.full((LANES - N_EXPERTS,), NEG_INF, F32)])[None, :]
    x_new, h, idx, wt, hist3 = _out_router(y_a.reshape(n, DN_WIDTH), y_b, y_c.reshape(n, CONV_WIDTH), x2,
                                           p['w_out'].astype(BF16), p['ffn_norm_w'][None, :], rw, rb)
    return _moe(layer, h, x_new, idx, wt, hist3, *experts)


def kernel(x, attn_norm_w, w_in, dn_conv_w, dn_a_log, dn_dt_bias, dn_norm_w, nsa_q_norm_w, nsa_k_norm_w, nsa_cmp_pos, nsa_cmp_w1, nsa_cmp_w2, conv_dw_w, conv_dw_b, conv_ln_w, conv_ln_b, w_out, ffn_norm_w, router_w, router_b, w_gate_up, b_gate_up, w_down, b_down, rel_bias):
    b, t, d = x.shape
    assert (t, d) == (SEQ, D_MODEL)
    stacked = dict(attn_norm_w=attn_norm_w, w_in=w_in, dn_conv_w=dn_conv_w, dn_a_log=dn_a_log,
                   dn_dt_bias=dn_dt_bias, dn_norm_w=dn_norm_w, nsa_q_norm_w=nsa_q_norm_w,
                   nsa_k_norm_w=nsa_k_norm_w, nsa_cmp_pos=nsa_cmp_pos, nsa_cmp_w1=nsa_cmp_w1,
                   nsa_cmp_w2=nsa_cmp_w2, conv_dw_w=conv_dw_w, conv_dw_b=conv_dw_b,
                   conv_ln_w=conv_ln_w, conv_ln_b=conv_ln_b, w_out=w_out, ffn_norm_w=ffn_norm_w,
                   router_w=router_w, router_b=router_b)
    experts = (w_gate_up, b_gate_up[:, :, None, :], w_down, b_down[:, :, None, :])
    bias_cmp, btab = _bias_tables(rel_bias)
    x2 = x.reshape(b * t, d)
    for l in range(w_in.shape[0]):
        x2 = _layer(l, x2, b, {k: v[l] for k, v in stacked.items()}, experts, bias_cmp, btab)
    return x2.reshape(b, t, d)
```

```python
import math

import jax
import jax.numpy as jnp
from jax import lax
from jax.experimental import pallas as pl
from jax.experimental.pallas import tpu as pltpu

F32 = jnp.float32
BF16 = jnp.bfloat16

D_MODEL = 1024
SEQ = 2048
DN_HEADS = 4
DN_HEAD_DIM = 128
DN_WIDTH = DN_HEADS * DN_HEAD_DIM
DN_CONV = 4
DN_CHUNK = 64
NSA_HEADS = 4
NSA_HEAD_DIM = 64
NSA_WIDTH = NSA_HEADS * NSA_HEAD_DIM
CMP_LEN = 32
CMP_STRIDE = 16
CMP_HIDDEN = 2 * NSA_HEAD_DIM
SLC_BLOCK = 64
SLC_TOP_N = 16
WINDOW = 512
CONV_WIDTH = 256
CONV_KERNEL = 31
REL_BUCKETS = 32
REL_MAX_DIST = 128
N_EXPERTS = 32
TOP_K = 4
D_FF = D_MODEL
SWIGLU_LIMIT = 7.0
SWIGLU_ALPHA = 1.702
EPS = 1e-6
NEG_INF = -1e30
FORCE = 1e4

LANES = 128
VMEM_LIMIT_BYTES = 48 * 1024 * 1024
EXPERT_VMEM_LIMIT_BYTES = 56 * 1024 * 1024

ROW_TILE = 512
SEQ_TILE = 256
Q_TILE = 128
SLC_GROUP = 4
MOE_TILE = 512
COMBINE_TILE = 512
DISPATCH_TILE = 1024

N_CMP_PAD = 128
N_SLC = SEQ // SLC_BLOCK


def _params(sem=None):
    return pltpu.CompilerParams(dimension_semantics=sem, vmem_limit_bytes=VMEM_LIMIT_BYTES)


def _mm(a, b):
    return jnp.dot(a.astype(BF16), b.astype(BF16), preferred_element_type=F32)


def _mm_nt(a, b):
    return lax.dot_general(a.astype(BF16), b.astype(BF16), (((1,), (1,)), ((), ())),
                           preferred_element_type=F32)


def _mm_tn(a, b):
    return lax.dot_general(a.astype(BF16), b.astype(BF16), (((0,), (0,)), ((), ())),
                           preferred_element_type=F32)


def _split3(x):
    hi = x.astype(BF16)
    r1 = x - hi.astype(F32)
    mid = r1.astype(BF16)
    lo = (r1 - mid.astype(F32)).astype(BF16)
    return hi, mid, lo


def _dot01_right(x, m01):
    hi, mid, lo = _split3(x)
    d = lambda p: jnp.dot(p, m01, preferred_element_type=F32)
    return d(hi) + d(mid) + d(lo)


def _dot01_left(m01, x):
    hi, mid, lo = _split3(x)
    d = lambda p: jnp.dot(m01, p, preferred_element_type=F32)
    return d(hi) + d(mid) + d(lo)


def _hi_lo(x):
    hi = x.astype(BF16)
    return hi, (x - hi.astype(F32)).astype(BF16)


def _sigmoid(x):
    return 1.0 / (1.0 + jnp.exp(-x))


def _silu(x):
    return x * _sigmoid(x)


def _softplus(x):
    return jnp.maximum(x, 0.0) + jnp.log(1.0 + jnp.exp(-jnp.abs(x)))


SUBLANES = 8
ROW_CHUNKS = D_MODEL // LANES
assert ROW_CHUNKS == SUBLANES


def _store_tile_rows(ref, value, rows):
    for s in range(ROW_CHUNKS):
        ref[pl.ds(s, rows, stride=SUBLANES), :] = value[:, s * LANES:(s + 1) * LANES]


def _load_tile_rows(ref, rows):
    return jnp.concatenate([ref[pl.ds(s, rows, stride=SUBLANES), :] for s in range(ROW_CHUNKS)], axis=1)


def _bcast_col(x, j, width):
    return jnp.broadcast_to(x[:, j:j + 1], (x.shape[0], width))


IN_SEGS = (3 * DN_WIDTH, DN_WIDTH, NSA_WIDTH, 6 * NSA_HEAD_DIM, 2 * CONV_WIDTH, LANES)
IN_COLS = sum(IN_SEGS)


INV_BLOCK = 16


def _delta_prepare(q_ref, k_ref, v_ref, gb_ref, grow_ref, wy):
    u_ref, w_ref, attn_ref, qg_ref, kk_ref, gc_ref = wy
    ts = SEQ_TILE
    ch = DN_CHUNK
    hd = DN_HEAD_DIM
    r = lax.broadcasted_iota(jnp.int32, (ts, ts), 0)
    c = lax.broadcasted_iota(jnp.int32, (ts, ts), 1)
    same_chunk = (r // ch) == (c // ch)
    tril = same_chunk & (r >= c)
    strict = same_chunk & (r > c)
    same16 = (r // INV_BLOCK) == (c // INV_BLOCK)
    eye = jnp.where(r == c, 1.0, 0.0).astype(F32)
    m_col = jnp.where(tril, 1.0, 0.0).astype(BF16)
    m_row = jnp.where(same_chunk & (r <= c), 1.0, 0.0).astype(BF16)

    gb = gb_ref[0]
    gc_col = _dot01_left(m_col, gb)
    gc_row = _dot01_right(grow_ref[...], m_row)

    heads = range(DN_HEADS)
    hsl = [slice(h * hd, (h + 1) * hd) for h in heads]
    kh = [k_ref[0, :, hsl[h]] for h in heads]
    gcb = [_bcast_col(gc_col, h, ts) for h in heads]
    decay = [jnp.where(tril, jnp.exp(jnp.where(
        tril, gcb[h] - jnp.broadcast_to(gc_row[h:h + 1, :], (ts, ts)), 0.0)), 0.0) for h in heads]
    beta = [_bcast_col(gb, DN_HEADS + h, hd) for h in heads]
    kb = [kh[h] * beta[h] for h in heads]
    yield
    a_mat = [jnp.where(strict, _mm_nt(kb[h], kh[h]) * decay[h], 0.0) for h in heads]
    yield
    d_mat = [jnp.where(same16, a_mat[h], 0.0) for h in heads]
    e_mat = [a_mat[h] - d_mat[h] for h in heads]
    d2 = [_mm(d_mat[h], d_mat[h]) for h in heads]
    yield
    d4 = [_mm(d2[h], d2[h]) for h in heads]
    yield
    t1 = [_mm(eye - d_mat[h], eye + d2[h]) for h in heads]
    yield
    d8 = [_mm(d4[h], d4[h]) for h in heads]
    yield
    t2 = [_mm(t1[h], eye + d4[h]) for h in heads]
    yield
    p_mat = [_mm(t2[h], eye + d8[h]) for h in heads]
    yield
    m_mat = [_mm(p_mat[h], e_mat[h]) for h in heads]
    yield
    m2 = [_mm(m_mat[h], m_mat[h]) for h in heads]
    yield
    t3 = [_mm(eye - m_mat[h], eye + m2[h]) for h in heads]
    yield
    t_mat = [_mm(t3[h], p_mat[h]) for h in heads]
    yield
    for h in heads:
        gc128 = gcb[h][:, :hd]
        expg = jnp.exp(gc128)
        qh = q_ref[0, :, hsl[h]]
        sol = _mm(t_mat[h], jnp.concatenate([v_ref[0, :, hsl[h]] * beta[h], kb[h] * expg], axis=1))
        u_ref[h] = sol[:, :hd]
        w_ref[h] = sol[:, hd:].astype(BF16)
        qg_ref[h] = (qh * expg).astype(BF16)
        kk_ref[h] = kh[h]
        gc_ref[h] = gc128
    yield
    for h in heads:
        attn_ref[h] = jnp.where(tril, _mm_nt(q_ref[0, :, hsl[h]], kh[h]) * decay[h], 0.0).astype(BF16)


def _delta_recur(wy, z_ref, nw_ref, o_ref, s_ref):
    u_ref, w_ref, attn_ref, qg_ref, kk_ref, gc_ref = wy
    ch = DN_CHUNK
    hd = DN_HEAD_DIM
    heads = range(DN_HEADS)
    state = [s_ref[h] for h in heads]
    outs = [[] for _ in heads]
    for ci in range(SEQ_TILE // ch):
        rs = slice(ci * ch, (ci + 1) * ch)
        v_new = [u_ref[h, rs, :] - _mm(w_ref[h, rs, :], state[h]) for h in heads]
        o_state = [_mm(qg_ref[h, rs, :], state[h]) for h in heads]
        yield
        for h in heads:
            gc = gc_ref[h, rs, :]
            g_last = gc[ch - 1:ch, :]
            k_dec = kk_ref[h, rs, :] * jnp.exp(g_last - gc)
            outs[h].append(o_state[h] + _mm(attn_ref[h, rs, ci * ch:(ci + 1) * ch], v_new[h]))
            state[h] = state[h] * jnp.exp(g_last) + _mm_tn(k_dec, v_new[h])
        yield

    for h in heads:
        s_ref[h] = state[h]
        hs = slice(h * hd, (h + 1) * hd)
        o = jnp.concatenate(outs[h], axis=0)
        o = o * lax.rsqrt(jnp.mean(o * o, axis=-1, keepdims=True) + EPS) * nw_ref[...]
        o_ref[0, :, hs] = o * _silu(z_ref[0, :, hs])


def _interleave(weighted):
    live = [gen for gen, _ in weighted]
    while live:
        for gen, n in weighted:
            for _ in range(n):
                if gen in live and next(gen, live) is live:
                    live.remove(gen)


DN_BATCH = 2


def _delta_kernel(q_ref, k_ref, v_ref, gb_ref, grow0_ref, grow1_ref, z_ref, nw_ref, o_ref, s_ref,
                  *wy_refs):
    t = pl.program_id(1)
    set_a, set_b = wy_refs[:6], wy_refs[6:]
    grow_refs = (grow0_ref, grow1_ref)

    @pl.when(t == 0)
    def _():
        s_ref[...] = jnp.zeros(s_ref.shape, F32)
        for ref in set_b:
            ref[...] = jnp.zeros(ref.shape, ref.dtype)

    def step(read_set, write_set):
        one = lambda ref, bb: ref.at[pl.ds(bb, 1)]
        work = []
        for bb in range(DN_BATCH):
            work.append((_delta_prepare(one(q_ref, bb), one(k_ref, bb), one(v_ref, bb), one(gb_ref, bb),
                                        grow_refs[bb], [ref.at[bb] for ref in write_set]), 2))
        for bb in range(DN_BATCH):
            work.append((_delta_recur([ref.at[bb] for ref in read_set], one(z_ref, bb), nw_ref,
                                      one(o_ref, bb), s_ref.at[bb]), 1))
        _interleave(work)

    @pl.when(lax.rem(t, 2) == 0)
    def _():
        step(set_b, set_a)

    @pl.when(lax.rem(t, 2) == 1)
    def _():
        step(set_a, set_b)


def _delta_rule(q, k, v, gb, grow, z, norm_w):
    b = q.shape[0]
    nb = DN_BATCH
    assert nb == 2 and b % nb == 0
    ts = SEQ_TILE
    nt = SEQ // ts
    nxt = lambda w: pl.BlockSpec((nb, ts, w), lambda i, t: (i, jnp.minimum(t, nt - 1), 0))
    cur = lambda w: pl.BlockSpec((nb, ts, w), lambda i, t: (i, jnp.maximum(t - 1, 0), 0))
    grow_spec = lambda bb: pl.BlockSpec(
        (8, ts), lambda i, t: (0, (i * nb + bb) * nt + jnp.minimum(t, nt - 1)))
    hd = DN_HEAD_DIM
    wy_set = [pltpu.VMEM((nb, DN_HEADS, ts, hd), F32), pltpu.VMEM((nb, DN_HEADS, ts, hd), BF16),
              pltpu.VMEM((nb, DN_HEADS, ts, ts), BF16), pltpu.VMEM((nb, DN_HEADS, ts, hd), BF16),
              pltpu.VMEM((nb, DN_HEADS, ts, hd), F32), pltpu.VMEM((nb, DN_HEADS, ts, hd), F32)]
    return pl.pallas_call(
        _delta_kernel,
        out_shape=jax.ShapeDtypeStruct((b, SEQ, DN_WIDTH), F32),
        grid=(b // nb, nt + 1),
        in_specs=[
            nxt(DN_WIDTH), nxt(DN_WIDTH), nxt(DN_WIDTH), nxt(LANES),
            grow_spec(0), grow_spec(1),
            cur(DN_WIDTH),
            pl.BlockSpec((1, DN_HEAD_DIM), lambda i, t: (0, 0)),
        ],
        out_specs=cur(DN_WIDTH),
        scratch_shapes=[pltpu.VMEM((nb, DN_HEADS, hd, hd), F32)] + wy_set + wy_set,
        compiler_params=_params(("parallel", "arbitrary")),
        name="delta_rule",
    )(q, k, v, gb, grow, grow, z, norm_w)


CONV_HALO = 32


MIX_COLS = {'qkv': (0, 3 * DN_WIDTH), 'z': (3 * DN_WIDTH, 4 * DN_WIDTH)}
_off = 4 * DN_WIDTH
for _name, _w in (('nq', NSA_WIDTH), ('nkv', 6 * NSA_HEAD_DIM), ('cu', 2 * CONV_WIDTH), ('small', LANES)):
    MIX_COLS[_name] = (_off, _off + _w)
    _off += _w
assert _off == IN_COLS
DN_HALO = 8
MXU_CHUNK = 256


def _front_kernel(x_ref, nw_ref, w_ref, wabt_ref, cw_ref, pcol_ref, prow_ref,
                  dww_ref, dwb_ref, lnw_ref, lnb_ref,
                  q_ref, k_ref, v_ref, gb_ref, grow_ref, z_ref, nq_ref, nkv_ref, small_ref, yc_ref,
                  dn_buf, cf_buf, cf_shift):
    i = pl.program_id(0)
    tm = ROW_TILE
    hd = NSA_HEAD_DIM

    @pl.when(lax.rem(i, SEQ // tm) == 0)
    def _():
        dn_buf[0:DN_HALO, :] = jnp.zeros((DN_HALO, 3 * DN_WIDTH), F32)
        cf_buf[0:CONV_HALO, :] = jnp.zeros((CONV_HALO, CONV_WIDTH), F32)

    xf = x_ref[...]
    hb = (xf * lax.rsqrt(jnp.mean(xf * xf, axis=-1, keepdims=True) + EPS) * nw_ref[...]).astype(BF16)

    def proj(name, lo=0, hi=None):
        c0, c1 = MIX_COLS[name]
        hi = c1 - c0 if hi is None else hi
        return jnp.dot(hb, w_ref[:, c0 + lo:c0 + hi], preferred_element_type=F32)

    def mxu_items():
        cu = proj('cu')
        cf_buf[CONV_HALO:CONV_HALO + tm, :] = cu[:, :CONV_WIDTH] * _sigmoid(cu[:, CONV_WIDTH:])
        yield
        sm = proj('small')
        small_ref[...] = sm
        lane = lax.broadcasted_iota(jnp.int32, sm.shape, 1)
        g_col = -jnp.exp(pcol_ref[0:1, :]) * _softplus(sm + pcol_ref[1:2, :])
        gb_ref[...] = jnp.where(lane < DN_HEADS, g_col, _sigmoid(sm))
        a_t = lax.dot_general(wabt_ref[...], hb, (((1,), (1,)), ((), ())), preferred_element_type=F32)
        g_row = -jnp.exp(prow_ref[:, 0:1]) * _softplus(a_t + prow_ref[:, 1:2])
        grow_ref[...] = g_row[0:8, :]
        yield
        for c0 in range(0, 3 * DN_WIDTH, MXU_CHUNK):
            dn_buf[DN_HALO:DN_HALO + tm, c0:c0 + MXU_CHUNK] = proj('qkv', c0, c0 + MXU_CHUNK)
            yield
        for c0 in range(0, DN_WIDTH, MXU_CHUNK):
            z_ref[:, c0:c0 + MXU_CHUNK] = proj('z', c0, c0 + MXU_CHUNK)
            yield
        res = proj('nq')
        for j in range(NSA_HEADS):
            nq_ref[j] = res[:, j * hd:(j + 1) * hd]
        yield
        res = proj('nkv')
        for j in range(6):
            nkv_ref[j] = res[:, j * hd:(j + 1) * hd]

    def conformer_items():
        span = tm + CONV_HALO - SUBLANES
        for b in range(1, SUBLANES):
            cf_shift[b] = cf_buf[b:b + span, :]
        yield
        base = CONV_HALO - (CONV_KERNEL - 1)
        rows = 64
        for rc in range(tm // rows):
            parts = []
            for cc in range(CONV_WIDTH // LANES):
                cs = slice(cc * LANES, (cc + 1) * LANES)
                acc = None
                for j in range(CONV_KERNEL):
                    start = base + rc * rows + j
                    b = start % SUBLANES
                    a0 = start - b
                    win = cf_buf[a0:a0 + rows, cs] if b == 0 else cf_shift[b, a0:a0 + rows, cs]
                    term = dww_ref[j:j + 1, cs] * win
                    acc = term if acc is None else acc + term
                parts.append(acc)
            h = jnp.concatenate(parts, axis=1) + dwb_ref[...]
            mu = jnp.mean(h, axis=-1, keepdims=True)
            var = jnp.mean(jnp.square(h - mu), axis=-1, keepdims=True)
            hn = (h - mu) * lax.rsqrt(var + EPS) * lnw_ref[...] + lnb_ref[...]
            yc_ref[rc * rows:(rc + 1) * rows, :] = _silu(hn)
            yield
        cf_buf[0:CONV_HALO, :] = cf_buf[tm:tm + CONV_HALO, :]

    def deltanet_items():
        rows = tm // 2
        for c in range(3 * DN_HEADS):
            cs = slice(c * LANES, (c + 1) * LANES)
            for r0 in range(0, tm, rows):
                lo = DN_HALO - (DN_CONV - 1) + r0
                acc = cw_ref[0:1, cs] * dn_buf[lo:lo + rows, cs]
                for j in range(1, DN_CONV):
                    acc = acc + cw_ref[j:j + 1, cs] * dn_buf[lo + j:lo + j + rows, cs]
                y = _silu(acc)
                if c < 2 * DN_HEADS:
                    y = y * lax.rsqrt(jnp.sum(y * y, axis=-1, keepdims=True) + EPS)
                rs = slice(r0, r0 + rows)
                if c < DN_HEADS:
                    q_ref[rs, cs] = y * DN_HEAD_DIM ** -0.5
                elif c < 2 * DN_HEADS:
                    k_ref[rs, (c - DN_HEADS) * LANES:(c - DN_HEADS + 1) * LANES] = y
                else:
                    v_ref[rs, (c - 2 * DN_HEADS) * LANES:(c - 2 * DN_HEADS + 1) * LANES] = y
            dn_buf[0:DN_HALO, cs] = dn_buf[tm:tm + DN_HALO, cs]
            yield

    mxu, cf, dn = mxu_items(), conformer_items(), deltanet_items()
    next(mxu)
    next(mxu)
    next(cf)
    next(mxu)
    _interleave([(mxu, 1), (dn, 2), (cf, 2)])


def _front(x2, norm_w, w_cat, w_abt, conv_w, pcol, prow, dw_w, dw_b, ln_w, ln_b):
    n = x2.shape[0]
    tm = ROW_TILE
    hd = NSA_HEAD_DIM
    row = lambda w: pl.BlockSpec((tm, w), lambda i: (i, 0))
    full = lambda shape: pl.BlockSpec(shape, lambda i: (0,) * len(shape))
    pieces = lambda k: pl.BlockSpec((k, tm, hd), lambda i: (0, i, 0))
    return pl.pallas_call(
        _front_kernel,
        out_shape=[jax.ShapeDtypeStruct((n, DN_WIDTH), F32)] * 3
        + [jax.ShapeDtypeStruct((n, LANES), F32), jax.ShapeDtypeStruct((8, n), F32),
           jax.ShapeDtypeStruct((n, DN_WIDTH), F32),
           jax.ShapeDtypeStruct((NSA_HEADS, n, hd), F32), jax.ShapeDtypeStruct((6, n, hd), F32),
           jax.ShapeDtypeStruct((n, LANES), F32), jax.ShapeDtypeStruct((n, CONV_WIDTH), F32)],
        grid=(n // tm,),
        in_specs=[row(D_MODEL), full((1, D_MODEL)), full((D_MODEL, IN_COLS)), full((16, D_MODEL)),
                  full((DN_CONV, 3 * DN_WIDTH)), full((8, LANES)), full((16, LANES)),
                  full((CONV_KERNEL, CONV_WIDTH)), full((1, CONV_WIDTH)), full((1, CONV_WIDTH)),
                  full((1, CONV_WIDTH))],
        out_specs=[row(DN_WIDTH)] * 3
        + [row(LANES), pl.BlockSpec((8, tm), lambda i: (0, i)), row(DN_WIDTH),
           pieces(NSA_HEADS), pieces(6), row(LANES), row(CONV_WIDTH)],
        scratch_shapes=[pltpu.VMEM((tm + DN_HALO, 3 * DN_WIDTH), F32),
                        pltpu.VMEM((tm + CONV_HALO, CONV_WIDTH), F32),
                        pltpu.VMEM((SUBLANES, tm + CONV_HALO - SUBLANES, CONV_WIDTH), F32)],
        compiler_params=_params(("arbitrary",)),
        name="front",
    )(x2, norm_w, w_cat, w_abt, conv_w, pcol, prow, dw_w, dw_b, ln_w, ln_b)


def _rms_rows(x, w):
    return x * lax.rsqrt(jnp.mean(x * x, axis=-1, keepdims=True) + EPS) * w


def _nsa_prep_kernel(kc_ref, vc_ref, ks_ref, kw_ref, pos_ref, w1_ref, w2_ref, knw_ref,
                     kcmp_ref, vcmp_ref, ksn_ref, kwn_ref):
    hd = NSA_HEAD_DIM

    def compress(x_ref, i):
        u_lo = jnp.zeros((N_CMP_PAD, CMP_HIDDEN), F32)
        u_hi = jnp.zeros((N_CMP_PAD, CMP_HIDDEN), F32)
        for r in range(CMP_STRIDE):
            xr = x_ref[pl.ds(r, N_CMP_PAD, stride=CMP_STRIDE), :]
            lo, hi = r, CMP_STRIDE + r
            u_lo = u_lo + _mm(xr + pos_ref[i, lo:lo + 1, :], w1_ref[i, lo * hd:(lo + 1) * hd, :])
            u_hi = u_hi + _mm(xr + pos_ref[i, hi:hi + 1, :], w1_ref[i, hi * hd:(hi + 1) * hd, :])
        hid = _silu(u_lo + pltpu.roll(u_hi, N_CMP_PAD - 1, axis=0))
        return _mm(hid, w2_ref[i])

    kcmp_ref[0] = _rms_rows(compress(kc_ref, 0), knw_ref[0:1, :])
    vcmp_ref[0] = compress(vc_ref, 1)
    ksn_ref[0] = _rms_rows(ks_ref[...], knw_ref[1:2, :])
    kwn_ref[0] = _rms_rows(kw_ref[...], knw_ref[2:3, :])


def _nsa_prep(kv6, pos, w1, w2, knw):
    b = kv6.shape[1]
    hd = NSA_HEAD_DIM
    full = lambda shape: pl.BlockSpec(shape, lambda i: (0,) * len(shape))
    piece = lambda j: pl.BlockSpec((None, None, SEQ, hd), lambda i: (j, i, 0, 0))
    bspec = lambda r, w: pl.BlockSpec((1, r, w), lambda i: (i, 0, 0))
    return pl.pallas_call(
        _nsa_prep_kernel,
        out_shape=[jax.ShapeDtypeStruct((b, N_CMP_PAD, hd), F32)] * 2
        + [jax.ShapeDtypeStruct((b, SEQ, hd), F32)] * 2,
        grid=(b,),
        in_specs=[
            piece(0), piece(1), piece(2), piece(4),
            full((2, CMP_LEN, hd)), full((2, CMP_LEN * hd, CMP_HIDDEN)), full((2, CMP_HIDDEN, hd)),
            full((3, hd)),
        ],
        out_specs=[bspec(N_CMP_PAD, hd)] * 2 + [bspec(SEQ, hd)] * 2,
        compiler_params=_params(("parallel",)),
        name="nsa_prep",
    )(kv6, kv6, kv6, kv6, pos, w1, w2, knw)


def _nsa_attn_kernel(q_ref, qnw_ref, kcmp_ref, vcmp_ref, ks_ref, vs_ref, kw_ref, vw_ref,
                     gate_ref, bcmp_ref, btab_ref, o_ref,
                     madd_ref, s_ref, ksb_ref, kwb_ref, vsa_ref, vwa_ref):
    i = pl.program_id(1)
    tq = Q_TILE
    nh = NSA_HEADS
    hd = NSA_HEAD_DIM
    t0 = i * tq

    @pl.when(i == 0)
    def _():
        ones = jnp.ones((SEQ, hd), BF16)
        ksb_ref[...] = ks_ref[0].astype(BF16)
        kwb_ref[...] = kw_ref[0].astype(BF16)
        vsa_ref[...] = jnp.concatenate([vs_ref[...].astype(BF16), ones], axis=1)
        vwa_ref[...] = jnp.concatenate([vw_ref[...].astype(BF16), ones], axis=1)

    qs = jnp.concatenate(
        [_rms_rows(q_ref[h], qnw_ref[...]) * hd ** -0.5 for h in range(nh)], axis=0).astype(BF16)

    row = lax.broadcasted_iota(jnp.int32, (tq, LANES), 0)
    lane = lax.broadcasted_iota(jnp.int32, (tq, LANES), 1)
    qpos = t0 + row

    branch_out = {}

    def select_items():
        s_all = _mm_nt(qs, kcmp_ref[0])
        cmp_valid = (qpos >= lane * CMP_STRIDE + (CMP_LEN - 1)) & (lane < N_CMP_PAD - 1)
        p_rows = []
        p_sum = jnp.zeros((tq, LANES), F32)
        for h in range(nh):
            s = jnp.where(cmp_valid, s_all[h * tq:(h + 1) * tq] + bcmp_ref[h], NEG_INF)
            m = jnp.max(s, axis=-1, keepdims=True)
            p = jnp.where(cmp_valid, jnp.exp(s - m), 0.0)
            l = jnp.sum(p, axis=-1, keepdims=True)
            p = p * jnp.where(l > 0.0, 1.0 / l, 0.0)
            p_rows.append(p)
            p_sum = p_sum + p
            yield
        branch_out['cmp'] = _mm(jnp.concatenate(p_rows, axis=0), vcmp_ref[0])

        ss = lax.broadcasted_iota(jnp.int32, (N_SLC, N_CMP_PAD), 0)
        jj = lax.broadcasted_iota(jnp.int32, (N_SLC, N_CMP_PAD), 1)
        overlap_t = ((jj * CMP_STRIDE < ss * SLC_BLOCK + SLC_BLOCK)
                     & (jj * CMP_STRIDE + CMP_LEN > ss * SLC_BLOCK) & (jj < N_CMP_PAD - 1))
        overlap_t = jnp.where(overlap_t, 1.0, 0.0).astype(BF16)
        nt = lambda a, b_: lax.dot_general(a, b_, (((1,), (1,)), ((), ())), preferred_element_type=F32)
        p_hi, p_mid, p_lo = _split3(p_sum)
        imp = nt(overlap_t, p_hi) + nt(overlap_t, p_mid) + nt(overlap_t, p_lo)
        blk = lax.broadcasted_iota(jnp.int32, (N_SLC, tq), 0)
        cur = (t0 + lax.broadcasted_iota(jnp.int32, (N_SLC, tq), 1)) // SLC_BLOCK
        causal_blk = blk <= cur
        forced = (blk == 0) | (blk == cur) | (blk == cur - 1)
        imp = jnp.where(causal_blk & forced, FORCE, jnp.where(causal_blk, imp, -1.0))
        yield
        rank = jnp.zeros((N_SLC, tq), jnp.int32)
        for s2 in range(N_SLC):
            other = jnp.broadcast_to(imp[s2:s2 + 1, :], (N_SLC, tq))
            beats = (other > imp) | ((other == imp) & (blk > s2))
            rank = rank + jnp.where(beats, 1, 0)
            if s2 % 8 == 7:
                yield
        sel_t = jnp.where(rank < SLC_TOP_N, 1.0, 0.0).astype(BF16)
        er = lax.broadcasted_iota(jnp.int32, (N_SLC, SEQ), 0)
        ec = lax.broadcasted_iota(jnp.int32, (N_SLC, SEQ), 1)
        expand = jnp.where(ec // SLC_BLOCK == er, 1.0, 0.0).astype(BF16)
        key = lax.broadcasted_iota(jnp.int32, (tq, SEQ), 1)
        qall = t0 + lax.broadcasted_iota(jnp.int32, (tq, SEQ), 0)
        chosen = (_mm_tn(sel_t, expand) > 0.5) & (qall >= key)
        madd_ref[...] = jnp.where(chosen, 0.0, NEG_INF)

    neg = jnp.full((nh * tq, tq), NEG_INF, F32)
    zero = jnp.zeros((nh * tq, LANES), F32)

    def scores(k_tile, bias_idx, add):
        s_t = _mm_nt(qs, k_tile)
        return jnp.concatenate(
            [s_t[h * tq:(h + 1) * tq] + (btab_ref[h, bias_idx] + add) for h in range(nh)], axis=0)

    def normalise(acc):
        return acc[:, :hd] * (1.0 / _bcast_col(acc, hd, hd))

    def window_items():
        kcol = lax.broadcasted_iota(jnp.int32, (tq, tq), 1)
        qrow = t0 + lax.broadcasted_iota(jnp.int32, (tq, tq), 0)
        n_win = WINDOW // tq + 1
        starts = []
        m_run = neg
        for j in range(n_win):
            d = n_win - 1 - j
            k0 = pl.multiple_of(jnp.maximum(i - d, 0) * tq, tq)
            starts.append(k0)
            dist = qrow - (k0 + kcol)
            ok = (dist >= 0) & (dist < WINDOW) & (i >= d)
            s_m = scores(kwb_ref[pl.ds(k0, tq), :], min(d, 2), jnp.where(ok, 0.0, NEG_INF))
            s_ref[:, j * tq:(j + 1) * tq] = s_m
            m_run = jnp.maximum(m_run, s_m)
            yield
        m_win = jnp.broadcast_to(jnp.max(m_run, axis=-1, keepdims=True), (nh * tq, tq))
        acc = zero
        for j in range(n_win):
            p = jnp.exp(s_ref[:, j * tq:(j + 1) * tq] - m_win).astype(BF16)
            acc = acc + jnp.dot(p, vwa_ref[pl.ds(starts[j], tq), :], preferred_element_type=F32)
            yield
        branch_out['win'] = normalise(acc)

    _interleave([(select_items(), 1), (window_items(), 1)])
    o_cmp, o_win = branch_out['cmp'], branch_out['win']

    n_groups = (i + SLC_GROUP) // SLC_GROUP

    def slc_pass1(j, m_run):
        for u in range(SLC_GROUP):
            kt = SLC_GROUP * j + u
            k0 = pl.multiple_of(kt * tq, tq)
            s_m = scores(ksb_ref[pl.ds(k0, tq), :], jnp.clip(i - kt, 0, 2), madd_ref[:, pl.ds(k0, tq)])
            s_ref[:, pl.ds(k0, tq)] = s_m
            m_run = jnp.maximum(m_run, s_m)
        return m_run

    m_slc = jnp.broadcast_to(
        jnp.max(lax.fori_loop(0, n_groups, slc_pass1, neg), axis=-1, keepdims=True), (nh * tq, tq))

    def slc_pass2(j, acc):
        for u in range(SLC_GROUP):
            k0 = pl.multiple_of((SLC_GROUP * j + u) * tq, tq)
            p = jnp.exp(s_ref[:, pl.ds(k0, tq)] - m_slc).astype(BF16)
            acc = acc + jnp.dot(p, vsa_ref[pl.ds(k0, tq), :], preferred_element_type=F32)
        return acc

    o_slc = normalise(lax.fori_loop(0, n_groups, slc_pass2, zero))

    gates = _sigmoid(gate_ref[0])
    outs = []
    for h in range(nh):
        hs = slice(h * tq, (h + 1) * tq)
        g0 = 2 * DN_HEADS + 3 * h
        outs.append(_bcast_col(gates, g0, hd) * o_cmp[hs]
                    + _bcast_col(gates, g0 + 1, hd) * o_slc[hs]
                    + _bcast_col(gates, g0 + 2, hd) * o_win[hs])
    o_ref[...] = jnp.concatenate(outs, axis=1)


def _nsa_attn(q, qnw, kcmp, vcmp, ks, kv6, kw, small, bias_cmp, btab):
    b = kv6.shape[1]
    tq = Q_TILE
    nq = SEQ // tq
    nh, hd = NSA_HEADS, NSA_HEAD_DIM
    full = lambda shape: pl.BlockSpec(shape, lambda bi, i: (0,) * len(shape))
    per_b = lambda r: pl.BlockSpec((1, r, hd), lambda bi, i: (bi, 0, 0))
    piece = lambda j: pl.BlockSpec((None, None, SEQ, hd), lambda bi, i: (j, bi, 0, 0))
    heads = pl.BlockSpec((nh, tq, hd), lambda bi, i: (0, bi * nq + i, 0))
    return pl.pallas_call(
        _nsa_attn_kernel,
        out_shape=jax.ShapeDtypeStruct((b * SEQ, NSA_WIDTH), F32),
        grid=(b, nq),
        in_specs=[
            heads,
            full((1, hd)),
            per_b(N_CMP_PAD), per_b(N_CMP_PAD), per_b(SEQ), piece(3), per_b(SEQ), piece(5),
            pl.BlockSpec((1, tq, LANES), lambda bi, i: (bi, i, 0)),
            pl.BlockSpec((nh, tq, N_CMP_PAD), lambda bi, i: (0, i, 0)),
            full((nh, 3, tq, tq)),
        ],
        out_specs=pl.BlockSpec((tq, NSA_WIDTH), lambda bi, i: (bi * nq + i, 0)),
        scratch_shapes=[pltpu.VMEM((tq, SEQ), F32), pltpu.VMEM((nh * tq, SEQ), F32),
                        pltpu.VMEM((SEQ, hd), BF16), pltpu.VMEM((SEQ, hd), BF16),
                        pltpu.VMEM((SEQ, 2 * hd), BF16), pltpu.VMEM((SEQ, 2 * hd), BF16)],
        compiler_params=_params(("parallel", "arbitrary")),
        name="nsa_attn",
    )(q, qnw, kcmp, vcmp, ks, kv6, kw, kv6, small, bias_cmp, btab)


def _out_router_kernel(ya_ref, yb_ref, yc_ref, x_ref, wo_ref, fnw_ref, rw_ref, rb_ref,
                       xo_ref, h_ref, idx_ref, wt_ref, hist_ref):
    wa = DN_WIDTH
    wb = wa + NSA_WIDTH
    y = (jnp.dot(ya_ref[...].astype(BF16), wo_ref[0:wa, :], preferred_element_type=F32)
         + jnp.dot(yb_ref[...].astype(BF16), wo_ref[wa:wb, :], preferred_element_type=F32)
         + jnp.dot(yc_ref[...].astype(BF16), wo_ref[wb:, :], preferred_element_type=F32))
    xn = x_ref[...] + y
    xo_ref[...] = xn
    h = xn * lax.rsqrt(jnp.mean(xn * xn, axis=-1, keepdims=True) + EPS) * fnw_ref[...]
    _store_tile_rows(h_ref, h, h.shape[0])
    h_hi, h_lo = _hi_lo(h)
    r_hi = jnp.dot(h_hi, rw_ref[...], preferred_element_type=F32)
    r_lo = jnp.dot(h_lo, rw_ref[...], preferred_element_type=F32)
    logits = ((r_hi[:, :LANES] + r_hi[:, LANES:]) + (r_lo[:, :LANES] + r_lo[:, LANES:])) + rb_ref[...]
    lane = lax.broadcasted_iota(jnp.int32, logits.shape, 1)
    vals, idxs = [], []
    for _ in range(TOP_K):
        m = jnp.max(logits, axis=-1, keepdims=True)
        ix = jnp.min(jnp.where(logits == m, lane, LANES), axis=-1, keepdims=True)
        vals.append(m)
        idxs.append(ix)
        logits = jnp.where(lane == ix, -jnp.inf, logits)
    es = [jnp.exp(v - vals[0]) for v in vals]
    inv = 1.0 / (es[0] + es[1] + es[2] + es[3])
    idx_out = jnp.zeros(lane.shape, jnp.int32)
    wt_out = jnp.zeros(lane.shape, F32)
    chosen = jnp.zeros(lane.shape, F32)
    for k in range(TOP_K):
        idx_out = jnp.where(lane == k, idxs[k], idx_out)
        wt_out = jnp.where(lane == k, es[k] * inv, wt_out)
        chosen = chosen + jnp.where(lane == idxs[k], 1.0, 0.0)
    idx_ref[...] = idx_out
    wt_ref[...] = wt_out
    hist_ref[0] = jnp.broadcast_to(jnp.sum(chosen, axis=0, keepdims=True), (8, LANES))


def _out_router(ya, yb, yc, x2, w_out, fnw, rw, rb):
    n = x2.shape[0]
    tm = ROW_TILE
    row = lambda w: pl.BlockSpec((tm, w), lambda i: (i, 0))
    full = lambda shape: pl.BlockSpec(shape, lambda i: (0,) * len(shape))
    return pl.pallas_call(
        _out_router_kernel,
        out_shape=[jax.ShapeDtypeStruct((n, D_MODEL), F32), jax.ShapeDtypeStruct((n * SUBLANES, LANES), F32),
                   jax.ShapeDtypeStruct((n, LANES), jnp.int32), jax.ShapeDtypeStruct((n, LANES), F32),
                   jax.ShapeDtypeStruct((n // tm, 8, LANES), F32)],
        grid=(n // tm,),
        in_specs=[row(DN_WIDTH), row(NSA_WIDTH), row(CONV_WIDTH), row(D_MODEL),
                  full((D_MODEL, D_MODEL)), full((1, D_MODEL)), full((D_MODEL, 2 * LANES)), full((1, LANES))],
        out_specs=[row(D_MODEL), pl.BlockSpec((tm * SUBLANES, LANES), lambda i: (i, 0)), row(LANES), row(LANES),
                   pl.BlockSpec((1, 8, LANES), lambda i: (i, 0, 0))],
        compiler_params=_params(("parallel",)),
        name="out_router",
    )(ya, yb, yc, x2, w_out, fnw, rw, rb)


def _slots_kernel(idx_ref, base_ref, slot_ref):
    tm = ROW_TILE
    idx = idx_ref[...]
    lane = lax.broadcasted_iota(jnp.int32, (tm, LANES), 1)
    onehots = [jnp.where(lane == _bcast_col(idx, k, LANES), 1.0, 0.0) for k in range(TOP_K)]
    cnt = (onehots[0] + onehots[1]) + (onehots[2] + onehots[3])
    r = lax.broadcasted_iota(jnp.int32, (tm, tm), 0)
    c = lax.broadcasted_iota(jnp.int32, (tm, tm), 1)
    earlier = jnp.where(r > c, 1.0, 0.0).astype(BF16)
    rank = jnp.dot(earlier, cnt.astype(BF16), preferred_element_type=F32) + base_ref[0, 0:1, :]
    out = jnp.zeros((tm, LANES), F32)
    for k in range(TOP_K):
        out = jnp.where(lane == k, jnp.sum(rank * onehots[k], axis=-1, keepdims=True), out)
    slot_ref[...] = (out * SUBLANES).astype(jnp.int32)


def _slots(idx, base3):
    n = idx.shape[0]
    tm = ROW_TILE
    return pl.pallas_call(
        _slots_kernel,
        out_shape=jax.ShapeDtypeStruct((n, LANES), jnp.int32),
        grid=(n // tm,),
        in_specs=[pl.BlockSpec((tm, LANES), lambda i: (i, 0)),
                  pl.BlockSpec((1, 8, LANES), lambda i: (i, 0, 0))],
        out_specs=pl.BlockSpec((tm, LANES), lambda i: (i, 0)),
        compiler_params=_params(("parallel",)),
        name="moe_slots",
    )(idx, base3)


def _dispatch_kernel(slot_ref, zoff_ref, nv_ref, h_ref, xs_hbm, zbuf, sem, zsem):
    i = pl.program_id(0)
    tm = DISPATCH_TILE
    base = i * tm

    @pl.when(i == 0)
    def _():
        zbuf[...] = jnp.zeros(zbuf.shape, F32)
        blk = MOE_TILE * SUBLANES
        n_blocks = xs_hbm.shape[0] // blk

        def fill(row0):
            return pltpu.make_async_copy(
                zbuf, xs_hbm.at[pl.ds(pl.multiple_of(row0 * SUBLANES, blk), blk)], zsem)

        def fill_region_end(e, c):
            fill(zoff_ref[e]).start()
            return c

        def fill_tail(b, c):
            fill(b * MOE_TILE).start()
            return c

        def wait_fill(b, c):
            fill(0).wait()
            return c

        lax.fori_loop(0, N_EXPERTS, fill_region_end, 0)
        lax.fori_loop(nv_ref[0], n_blocks, fill_tail, 0)
        lax.fori_loop(0, N_EXPERTS + n_blocks - nv_ref[0], wait_fill, 0)

    def body(r, c):
        src = h_ref.at[pl.ds(pl.multiple_of(r * SUBLANES, SUBLANES), SUBLANES)]
        for k in range(TOP_K):
            row0 = pl.multiple_of(slot_ref[(base + r) * TOP_K + k], SUBLANES)
            pltpu.make_async_copy(src, xs_hbm.at[pl.ds(row0, SUBLANES)], sem).start(priority=k % 2)
        return c

    lax.fori_loop(0, tm, body, 0, unroll=8)
    done = xs_hbm.at[pl.ds(0, TOP_K * tm * SUBLANES)]
    pltpu.make_async_copy(done, done, sem).wait()


def _dispatch(slot_flat, zero_off, n_valid, h, n_slots):
    n = h.shape[0] // SUBLANES
    tm = DISPATCH_TILE
    return pl.pallas_call(
        _dispatch_kernel,
        out_shape=jax.ShapeDtypeStruct((n_slots * SUBLANES, LANES), F32),
        grid_spec=pltpu.PrefetchScalarGridSpec(
            num_scalar_prefetch=3,
            grid=(n // tm,),
            in_specs=[pl.BlockSpec((tm * SUBLANES, LANES), lambda i, s, z, nv: (i, 0))],
            out_specs=pl.BlockSpec(memory_space=pl.ANY),
            scratch_shapes=[pltpu.VMEM((MOE_TILE * SUBLANES, LANES), F32),
                            pltpu.SemaphoreType.DMA, pltpu.SemaphoreType.DMA],
        ),
        compiler_params=_params(("arbitrary",)),
        name="moe_dispatch",
    )(slot_flat, zero_off, n_valid, h)


def _expert_kernel(be_ref, nv_ref, xs_ref, wgu_ref, bgu_ref, wd_ref, bd_ref, o_ref, wgu_bf, wd_bf):
    i = pl.program_id(0)
    new_expert = (i == 0) | (be_ref[i] != be_ref[jnp.maximum(i - 1, 0)])

    @pl.when((i < nv_ref[0]) & new_expert)
    def _():
        step = 4 * LANES
        for c0 in range(0, 2 * D_FF, step):
            wgu_bf[:, c0:c0 + step] = wgu_ref[:, c0:c0 + step].astype(BF16)
        for c0 in range(0, D_MODEL, step):
            wd_bf[:, c0:c0 + step] = wd_ref[:, c0:c0 + step].astype(BF16)

    @pl.when(i < nv_ref[0])
    def _():
        xb = _load_tile_rows(xs_ref, MOE_TILE).astype(BF16)
        gu = jnp.dot(xb, wgu_bf[...], preferred_element_type=F32) + bgu_ref[...]
        gate = jnp.minimum(gu[:, :D_FF], SWIGLU_LIMIT)
        up = jnp.clip(gu[:, D_FF:], -SWIGLU_LIMIT, SWIGLU_LIMIT)
        act = (up + 1.0) * gate * _sigmoid(SWIGLU_ALPHA * gate)
        y = jnp.dot(act.astype(BF16), wd_bf[...], preferred_element_type=F32) + bd_ref[...]
        _store_tile_rows(o_ref, y, MOE_TILE)

    @pl.when(i >= nv_ref[0])
    def _():
        o_ref[...] = jnp.zeros(o_ref.shape, F32)


def _expert_ffn(layer, block_expert, n_valid, xs, wgu, bgu, wd, bd):
    n_slots = xs.shape[0] // SUBLANES
    tm = MOE_TILE
    return pl.pallas_call(
        _expert_kernel,
        out_shape=jax.ShapeDtypeStruct(xs.shape, F32),
        grid_spec=pltpu.PrefetchScalarGridSpec(
            num_scalar_prefetch=2,
            grid=(n_slots // tm,),
            in_specs=[
                pl.BlockSpec((tm * SUBLANES, LANES), lambda i, be, nv: (i, 0)),
                pl.BlockSpec((None, None, D_MODEL, 2 * D_FF), lambda i, be, nv: (layer, be[i], 0, 0)),
                pl.BlockSpec((None, None, 1, 2 * D_FF), lambda i, be, nv: (layer, be[i], 0, 0)),
                pl.BlockSpec((None, None, D_FF, D_MODEL), lambda i, be, nv: (layer, be[i], 0, 0)),
                pl.BlockSpec((None, None, 1, D_MODEL), lambda i, be, nv: (layer, be[i], 0, 0)),
            ],
            out_specs=pl.BlockSpec((tm * SUBLANES, LANES), lambda i, be, nv: (i, 0)),
            scratch_shapes=[pltpu.VMEM((D_MODEL, 2 * D_FF), BF16), pltpu.VMEM((D_FF, D_MODEL), BF16)],
        ),
        compiler_params=pltpu.CompilerParams(dimension_semantics=("arbitrary",),
                                             vmem_limit_bytes=EXPERT_VMEM_LIMIT_BYTES),
        name="expert_ffn",
    )(block_expert, n_valid, xs, wgu, bgu, wd, bd)


def _combine_kernel(slot_ref, ys_hbm, x_ref, wt_ref, o_ref, buf, acc_ref, sems):
    i = pl.program_id(0)
    tc = COMBINE_TILE
    nsteps = pl.num_programs(0)
    cur = lax.rem(i, 2)

    def issue(step, par):
        base = step * tc

        def body(r, c):
            dst0 = pl.multiple_of(r * SUBLANES, SUBLANES)
            for k in range(TOP_K):
                row0 = pl.multiple_of(slot_ref[(base + r) * TOP_K + k], SUBLANES)
                pltpu.make_async_copy(ys_hbm.at[pl.ds(row0, SUBLANES)],
                                      buf.at[par, k, pl.ds(dst0, SUBLANES)],
                                      sems.at[par]).start(priority=k % 2)
            return c

        lax.fori_loop(0, tc, body, 0, unroll=8)

    @pl.when(i == 0)
    def _():
        issue(0, 0)

    @pl.when(i + 1 < nsteps)
    def _():
        issue(i + 1, 1 - cur)

    pltpu.make_async_copy(buf.at[cur], buf.at[cur], sems.at[cur]).wait()
    wt = wt_ref[...]
    acc = None
    for k in range(TOP_K):
        w_tiles = jnp.broadcast_to(_bcast_col(wt, k, LANES)[:, None, :], (tc, SUBLANES, LANES))
        term = w_tiles.reshape(tc * SUBLANES, LANES) * buf[cur, k]
        acc = term if acc is None else acc + term
    acc_ref[...] = acc
    for s in range(ROW_CHUNKS):
        cs = slice(s * LANES, (s + 1) * LANES)
        o_ref[:, cs] = x_ref[:, cs] + acc_ref[pl.ds(s, tc, stride=SUBLANES), :]


def _combine(slot_flat, ys, x2, wt):
    n = x2.shape[0]
    tc = COMBINE_TILE
    return pl.pallas_call(
        _combine_kernel,
        out_shape=jax.ShapeDtypeStruct((n, D_MODEL), F32),
        grid_spec=pltpu.PrefetchScalarGridSpec(
            num_scalar_prefetch=1,
            grid=(n // tc,),
            in_specs=[pl.BlockSpec(memory_space=pl.ANY),
                      pl.BlockSpec((tc, D_MODEL), lambda i, s: (i, 0)),
                      pl.BlockSpec((tc, LANES), lambda i, s: (i, 0))],
            out_specs=pl.BlockSpec((tc, D_MODEL), lambda i, s: (i, 0)),
            scratch_shapes=[pltpu.VMEM((2, TOP_K, tc * SUBLANES, LANES), F32),
                            pltpu.VMEM((tc * SUBLANES, LANES), F32), pltpu.SemaphoreType.DMA((2,))],
        ),
        compiler_params=_params(("arbitrary",)),
        name="moe_combine",
    )(slot_flat, ys, x2, wt)


def _moe(layer, h, x2, idx, wt, hist3, wgu, bgu, wd, bd):
    n = x2.shape[0]
    tm = MOE_TILE
    n_slots = n * TOP_K + N_EXPERTS * tm
    n_blocks = n_slots // tm
    hist = hist3[:, 0, :N_EXPERTS]
    counts = jnp.sum(hist, axis=0).astype(jnp.int32)
    padded = (counts + tm - 1) // tm * tm
    pad_end = jnp.cumsum(padded)
    pad_start = pad_end - padded
    tile_base = pad_start[None, :].astype(F32) + (jnp.cumsum(hist, axis=0) - hist)
    base3 = jnp.broadcast_to(jnp.pad(tile_base, ((0, 0), (0, LANES - N_EXPERTS)))[:, None, :],
                             (hist.shape[0], 8, LANES))
    blk0 = jnp.arange(n_blocks) * tm
    block_expert = jnp.minimum(jnp.sum(blk0[:, None] >= pad_end[None, :], axis=1), N_EXPERTS - 1).astype(jnp.int32)
    n_valid = (pad_end[-1:] // tm).astype(jnp.int32)
    zero_off = jnp.maximum(pad_end - tm, 0).astype(jnp.int32)

    slot_flat = _slots(idx, base3)[:, :TOP_K].reshape(-1)
    xs = _dispatch(slot_flat, zero_off, n_valid, h, n_slots)
    ys = _expert_ffn(layer, block_expert, n_valid, xs, wgu, bgu, wd, bd)
    return _combine(slot_flat, ys, x2, wt)


def _t5_bucket(dist):
    n = jnp.maximum(dist, 0)
    max_exact = REL_BUCKETS // 2
    nf = jnp.maximum(n, 1).astype(F32)
    large = max_exact + (jnp.log(nf / max_exact) / math.log(REL_MAX_DIST / max_exact)
                         * (REL_BUCKETS - max_exact)).astype(jnp.int32)
    large = jnp.minimum(large, REL_BUCKETS - 1)
    return jnp.where(n < max_exact, n, large)


def _bias_tables(rel_bias):
    rel_bias = rel_bias.astype(F32)
    tq = Q_TILE

    def lookup(dist):
        bucket = _t5_bucket(dist)
        out = jnp.zeros((NSA_HEADS,) + dist.shape, F32)
        for bk in range(REL_BUCKETS):
            out = jnp.where(bucket[None] == bk, rel_bias[bk].reshape((NSA_HEADS,) + (1,) * dist.ndim), out)
        return out

    t_pos = jnp.arange(SEQ)
    cmp_end = jnp.arange(N_CMP_PAD) * CMP_STRIDE + CMP_LEN - 1
    bias_cmp = lookup(t_pos[:, None] - cmp_end[None, :])
    rr = jnp.arange(tq)[:, None] - jnp.arange(tq)[None, :]
    btab = lookup(jnp.stack([rr, rr + tq, rr + 2 * tq]))
    return bias_cmp, btab


def _layer(layer, x2, b, p, experts, bias_cmp, btab):
    n = x2.shape[0]
    w = p['w_in']
    o_a = 4 * DN_WIDTH
    o_q = o_a + 2 * DN_HEADS
    o_kv = o_q + NSA_WIDTH
    o_g = o_kv + 6 * NSA_HEAD_DIM
    o_u = o_g + 3 * NSA_HEADS
    w_small = jnp.concatenate([w[:, o_a:o_q], w[:, o_g:o_u],
                               jnp.zeros((D_MODEL, LANES - 2 * DN_HEADS - 3 * NSA_HEADS), F32)], axis=1)
    w_cat = jnp.concatenate([w[:, :o_a], w[:, o_q:o_kv], w[:, o_kv:o_g], w[:, o_u:], w_small],
                            axis=1).astype(BF16)
    w_abt = jnp.concatenate([w[:, o_a:o_q].T, jnp.zeros((8, D_MODEL), F32)], axis=0).astype(BF16)
    pcol = jnp.zeros((8, LANES), F32).at[0, :DN_HEADS].set(p['dn_a_log']).at[1, :DN_HEADS].set(p['dn_dt_bias'])
    prow = jnp.zeros((16, LANES), F32).at[:DN_HEADS, 0].set(p['dn_a_log']).at[:DN_HEADS, 1].set(p['dn_dt_bias'])
    dq, dk, dv, gb, grow, z, nq, nkv, small, y_c = _front(
        x2, p['attn_norm_w'][None, :], w_cat, w_abt, p['dn_conv_w'], pcol, prow,
        p['conv_dw_w'], p['conv_dw_b'][None, :], p['conv_ln_w'][None, :], p['conv_ln_b'][None, :])
    small3 = small.reshape(b, SEQ, LANES)

    seq = lambda a: a.reshape(b, SEQ, a.shape[-1])
    y_a = _delta_rule(seq(dq), seq(dk), seq(dv), seq(gb), grow, seq(z), p['dn_norm_w'][None, :])

    kv6 = nkv.reshape(6, b, SEQ, NSA_HEAD_DIM)
    kcmp, vcmp, ksn, kwn = _nsa_prep(kv6, p['nsa_cmp_pos'], p['nsa_cmp_w1'].astype(BF16),
                                     p['nsa_cmp_w2'].astype(BF16), p['nsa_k_norm_w'])
    y_b = _nsa_attn(nq, p['nsa_q_norm_w'][None, :], kcmp, vcmp, ksn, kv6, kwn, small3, bias_cmp, btab)

    rw = jnp.concatenate(_hi_lo(jnp.pad(p['router_w'], ((0, 0), (0, LANES - N_EXPERTS)))), axis=1)
    rb = jnp.concatenate([p['router_b'], jnp.full((LANES - N_EXPERTS,), NEG_INF, F32)])[None, :]
    x_new, h, idx, wt, hist3 = _out_router(y_a.reshape(n, DN_WIDTH), y_b, y_c.reshape(n, CONV_WIDTH), x2,
                                           p['w_out'].astype(BF16), p['ffn_norm_w'][None, :], rw, rb)
    return _moe(layer, h, x_new, idx, wt, hist3, *experts)


def kernel(x, attn_norm_w, w_in, dn_conv_w, dn_a_log, dn_dt_bias, dn_norm_w, nsa_q_norm_w, nsa_k_norm_w, nsa_cmp_pos, nsa_cmp_w1, nsa_cmp_w2, conv_dw_w, conv_dw_b, conv_ln_w, conv_ln_b, w_out, ffn_norm_w, router_w, router_b, w_gate_up, b_gate_up, w_down, b_down, rel_bias):
    b, t, d = x.shape
    assert (t, d) == (SEQ, D_MODEL)
    stacked = dict(attn_norm_w=attn_norm_w, w_in=w_in, dn_conv_w=dn_conv_w, dn_a_log=dn_a_log,
                   dn_dt_bias=dn_dt_bias, dn_norm_w=dn_norm_w, nsa_q_norm_w=nsa_q_norm_w,
                   nsa_k_norm_w=nsa_k_norm_w, nsa_cmp_pos=nsa_cmp_pos, nsa_cmp_w1=nsa_cmp_w1,
                   nsa_cmp_w2=nsa_cmp_w2, conv_dw_w=conv_dw_w, conv_dw_b=conv_dw_b,
                   conv_ln_w=conv_ln_w, conv_ln_b=conv_ln_b, w_out=w_out, ffn_norm_w=ffn_norm_w,
                   router_w=router_w, router_b=router_b)
    experts = (w_gate_up, b_gate_up[:, :, None, :], w_down, b_down[:, :, None, :])
    bias_cmp, btab = _bias_tables(rel_bias)
    x2 = x.reshape(b * t, d)
    for l in range(w_in.shape[0]):
        x2 = _layer(l, x2, b, {k: v[l] for k, v in stacked.items()}, experts, bias_cmp, btab)
    return x2.reshape(b, t, d)
```

```python
import math

import jax
import jax.numpy as jnp
from jax import lax
from jax.experimental import pallas as pl
from jax.experimental.pallas import tpu as pltpu

F32 = jnp.float32
BF16 = jnp.bfloat16

D_MODEL = 1024
SEQ = 2048
DN_HEADS = 4
DN_HEAD_DIM = 128
DN_WIDTH = DN_HEADS * DN_HEAD_DIM
DN_CONV = 4
DN_CHUNK = 64
NSA_HEADS = 4
NSA_HEAD_DIM = 64
NSA_WIDTH = NSA_HEADS * NSA_HEAD_DIM
CMP_LEN = 32
CMP_STRIDE = 16
CMP_HIDDEN = 2 * NSA_HEAD_DIM
SLC_BLOCK = 64
SLC_TOP_N = 16
WINDOW = 512
CONV_WIDTH = 256
CONV_KERNEL = 31
REL_BUCKETS = 32
REL_MAX_DIST = 128
N_EXPERTS = 32
TOP_K = 4
D_FF = D_MODEL
SWIGLU_LIMIT = 7.0
SWIGLU_ALPHA = 1.702
EPS = 1e-6
NEG_INF = -1e30
FORCE = 1e4

LANES = 128
VMEM_LIMIT_BYTES = 48 * 1024 * 1024
EXPERT_VMEM_LIMIT_BYTES = 56 * 1024 * 1024

ROW_TILE = 512
SEQ_TILE = 256
Q_TILE = 128
SLC_GROUP = 4
MOE_TILE = 512
COMBINE_TILE = 256
DISPATCH_TILE = 1024

N_CMP_PAD = 128
N_SLC = SEQ // SLC_BLOCK


def _params(sem=None):
    return pltpu.CompilerParams(dimension_semantics=sem, vmem_limit_bytes=VMEM_LIMIT_BYTES)


def _mm(a, b):
    return jnp.dot(a.astype(BF16), b.astype(BF16), preferred_element_type=F32)


def _mm_nt(a, b):
    return lax.dot_general(a.astype(BF16), b.astype(BF16), (((1,), (1,)), ((), ())),
                           preferred_element_type=F32)


def _mm_tn(a, b):
    return lax.dot_general(a.astype(BF16), b.astype(BF16), (((0,), (0,)), ((), ())),
                           preferred_element_type=F32)


def _split3(x):
    hi = x.astype(BF16)
    r1 = x - hi.astype(F32)
    mid = r1.astype(BF16)
    lo = (r1 - mid.astype(F32)).astype(BF16)
    return hi, mid, lo


def _dot01_right(x, m01):
    hi, mid, lo = _split3(x)
    d = lambda p: jnp.dot(p, m01, preferred_element_type=F32)
    return d(hi) + d(mid) + d(lo)


def _dot01_left(m01, x):
    hi, mid, lo = _split3(x)
    d = lambda p: jnp.dot(m01, p, preferred_element_type=F32)
    return d(hi) + d(mid) + d(lo)


def _hi_lo(x):
    hi = x.astype(BF16)
    return hi, (x - hi.astype(F32)).astype(BF16)


def _sigmoid(x):
    return 1.0 / (1.0 + jnp.exp(-x))


def _silu(x):
    return x * _sigmoid(x)


def _softplus(x):
    return jnp.maximum(x, 0.0) + jnp.log(1.0 + jnp.exp(-jnp.abs(x)))


SUBLANES = 8
ROW_CHUNKS = D_MODEL // LANES
assert ROW_CHUNKS == SUBLANES


def _store_tile_rows(ref, value, rows):
    for s in range(ROW_CHUNKS):
        ref[pl.ds(s, rows, stride=SUBLANES), :] = value[:, s * LANES:(s + 1) * LANES]


def _load_tile_rows(ref, rows):
    return jnp.concatenate([ref[pl.ds(s, rows, stride=SUBLANES), :] for s in range(ROW_CHUNKS)], axis=1)


def _bcast_col(x, j, width):
    return jnp.broadcast_to(x[:, j:j + 1], (x.shape[0], width))


IN_SEGS = (3 * DN_WIDTH, DN_WIDTH, NSA_WIDTH, 6 * NSA_HEAD_DIM, 2 * CONV_WIDTH, LANES)
IN_COLS = sum(IN_SEGS)


INV_BLOCK = 16


def _delta_prepare(q_ref, k_ref, v_ref, gb_ref, grow_ref, wy):
    u_ref, w_ref, attn_ref, qg_ref, kk_ref, gc_ref = wy
    ts = SEQ_TILE
    ch = DN_CHUNK
    hd = DN_HEAD_DIM
    r = lax.broadcasted_iota(jnp.int32, (ts, ts), 0)
    c = lax.broadcasted_iota(jnp.int32, (ts, ts), 1)
    same_chunk = (r // ch) == (c // ch)
    tril = same_chunk & (r >= c)
    strict = same_chunk & (r > c)
    same16 = (r // INV_BLOCK) == (c // INV_BLOCK)
    eye = jnp.where(r == c, 1.0, 0.0).astype(F32)
    m_col = jnp.where(tril, 1.0, 0.0).astype(BF16)
    m_row = jnp.where(same_chunk & (r <= c), 1.0, 0.0).astype(BF16)

    gb = gb_ref[0]
    gc_col = _dot01_left(m_col, gb)
    gc_row = _dot01_right(grow_ref[...], m_row)

    heads = range(DN_HEADS)
    hsl = [slice(h * hd, (h + 1) * hd) for h in heads]
    kh = [k_ref[0, :, hsl[h]] for h in heads]
    gcb = [_bcast_col(gc_col, h, ts) for h in heads]
    decay = [jnp.where(tril, jnp.exp(jnp.where(
        tril, gcb[h] - jnp.broadcast_to(gc_row[h:h + 1, :], (ts, ts)), 0.0)), 0.0) for h in heads]
    beta = [_bcast_col(gb, DN_HEADS + h, hd) for h in heads]
    kb = [kh[h] * beta[h] for h in heads]
    yield
    a_mat = [jnp.where(strict, _mm_nt(kb[h], kh[h]) * decay[h], 0.0) for h in heads]
    yield
    d_mat = [jnp.where(same16, a_mat[h], 0.0) for h in heads]
    e_mat = [a_mat[h] - d_mat[h] for h in heads]
    d2 = [_mm(d_mat[h], d_mat[h]) for h in heads]
    yield
    d4 = [_mm(d2[h], d2[h]) for h in heads]
    yield
    t1 = [_mm(eye - d_mat[h], eye + d2[h]) for h in heads]
    yield
    d8 = [_mm(d4[h], d4[h]) for h in heads]
    yield
    t2 = [_mm(t1[h], eye + d4[h]) for h in heads]
    yield
    p_mat = [_mm(t2[h], eye + d8[h]) for h in heads]
    yield
    m_mat = [_mm(p_mat[h], e_mat[h]) for h in heads]
    yield
    m2 = [_mm(m_mat[h], m_mat[h]) for h in heads]
    yield
    t3 = [_mm(eye - m_mat[h], eye + m2[h]) for h in heads]
    yield
    t_mat = [_mm(t3[h], p_mat[h]) for h in heads]
    yield
    for h in heads:
        gc128 = gcb[h][:, :hd]
        expg = jnp.exp(gc128)
        qh = q_ref[0, :, hsl[h]]
        sol = _mm(t_mat[h], jnp.concatenate([v_ref[0, :, hsl[h]] * beta[h], kb[h] * expg], axis=1))
        u_ref[h] = sol[:, :hd]
        w_ref[h] = sol[:, hd:].astype(BF16)
        qg_ref[h] = (qh * expg).astype(BF16)
        kk_ref[h] = kh[h]
        gc_ref[h] = gc128
    yield
    for h in heads:
        attn_ref[h] = jnp.where(tril, _mm_nt(q_ref[0, :, hsl[h]], kh[h]) * decay[h], 0.0).astype(BF16)


def _delta_recur(wy, z_ref, nw_ref, o_ref, s_ref):
    u_ref, w_ref, attn_ref, qg_ref, kk_ref, gc_ref = wy
    ch = DN_CHUNK
    hd = DN_HEAD_DIM
    heads = range(DN_HEADS)
    state = [s_ref[h] for h in heads]
    outs = [[] for _ in heads]
    for ci in range(SEQ_TILE // ch):
        rs = slice(ci * ch, (ci + 1) * ch)
        v_new = [u_ref[h, rs, :] - _mm(w_ref[h, rs, :], state[h]) for h in heads]
        o_state = [_mm(qg_ref[h, rs, :], state[h]) for h in heads]
        yield
        for h in heads:
            gc = gc_ref[h, rs, :]
            g_last = gc[ch - 1:ch, :]
            k_dec = kk_ref[h, rs, :] * jnp.exp(g_last - gc)
            outs[h].append(o_state[h] + _mm(attn_ref[h, rs, ci * ch:(ci + 1) * ch], v_new[h]))
            state[h] = state[h] * jnp.exp(g_last) + _mm_tn(k_dec, v_new[h])
        yield

    for h in heads:
        s_ref[h] = state[h]
        hs = slice(h * hd, (h + 1) * hd)
        o = jnp.concatenate(outs[h], axis=0)
        o = o * lax.rsqrt(jnp.mean(o * o, axis=-1, keepdims=True) + EPS) * nw_ref[...]
        o_ref[0, :, hs] = o * _silu(z_ref[0, :, hs])


def _interleave(weighted):
    live = [gen for gen, _ in weighted]
    while live:
        for gen, n in weighted:
            for _ in range(n):
                if gen in live and next(gen, live) is live:
                    live.remove(gen)


DN_BATCH = 2


def _delta_kernel(q_ref, k_ref, v_ref, gb_ref, grow0_ref, grow1_ref, z_ref, nw_ref, o_ref, s_ref,
                  *wy_refs):
    t = pl.program_id(1)
    set_a, set_b = wy_refs[:6], wy_refs[6:]
    grow_refs = (grow0_ref, grow1_ref)

    @pl.when(t == 0)
    def _():
        s_ref[...] = jnp.zeros(s_ref.shape, F32)
        for ref in set_b:
            ref[...] = jnp.zeros(ref.shape, ref.dtype)

    def step(read_set, write_set):
        one = lambda ref, bb: ref.at[pl.ds(bb, 1)]
        work = []
        for bb in range(DN_BATCH):
            work.append((_delta_prepare(one(q_ref, bb), one(k_ref, bb), one(v_ref, bb), one(gb_ref, bb),
                                        grow_refs[bb], [ref.at[bb] for ref in write_set]), 2))
        for bb in range(DN_BATCH):
            work.append((_delta_recur([ref.at[bb] for ref in read_set], one(z_ref, bb), nw_ref,
                                      one(o_ref, bb), s_ref.at[bb]), 1))
        _interleave(work)

    @pl.when(lax.rem(t, 2) == 0)
    def _():
        step(set_b, set_a)

    @pl.when(lax.rem(t, 2) == 1)
    def _():
        step(set_a, set_b)


def _delta_rule(q, k, v, gb, grow, z, norm_w):
    b = q.shape[0]
    nb = DN_BATCH
    assert nb == 2 and b % nb == 0
    ts = SEQ_TILE
    nt = SEQ // ts
    nxt = lambda w: pl.BlockSpec((nb, ts, w), lambda i, t: (i, jnp.minimum(t, nt - 1), 0))
    cur = lambda w: pl.BlockSpec((nb, ts, w), lambda i, t: (i, jnp.maximum(t - 1, 0), 0))
    grow_spec = lambda bb: pl.BlockSpec(
        (8, ts), lambda i, t: (0, (i * nb + bb) * nt + jnp.minimum(t, nt - 1)))
    hd = DN_HEAD_DIM
    wy_set = [pltpu.VMEM((nb, DN_HEADS, ts, hd), F32), pltpu.VMEM((nb, DN_HEADS, ts, hd), BF16),
              pltpu.VMEM((nb, DN_HEADS, ts, ts), BF16), pltpu.VMEM((nb, DN_HEADS, ts, hd), BF16),
              pltpu.VMEM((nb, DN_HEADS, ts, hd), F32), pltpu.VMEM((nb, DN_HEADS, ts, hd), F32)]
    return pl.pallas_call(
        _delta_kernel,
        out_shape=jax.ShapeDtypeStruct((b, SEQ, DN_WIDTH), F32),
        grid=(b // nb, nt + 1),
        in_specs=[
            nxt(DN_WIDTH), nxt(DN_WIDTH), nxt(DN_WIDTH), nxt(LANES),
            grow_spec(0), grow_spec(1),
            cur(DN_WIDTH),
            pl.BlockSpec((1, DN_HEAD_DIM), lambda i, t: (0, 0)),
        ],
        out_specs=cur(DN_WIDTH),
        scratch_shapes=[pltpu.VMEM((nb, DN_HEADS, hd, hd), F32)] + wy_set + wy_set,
        compiler_params=_params(("parallel", "arbitrary")),
        name="delta_rule",
    )(q, k, v, gb, grow, grow, z, norm_w)


CONV_HALO = 32


MIX_COLS = {'qkv': (0, 3 * DN_WIDTH), 'z': (3 * DN_WIDTH, 4 * DN_WIDTH)}
_off = 4 * DN_WIDTH
for _name, _w in (('nq', NSA_WIDTH), ('nkv', 6 * NSA_HEAD_DIM), ('cu', 2 * CONV_WIDTH), ('small', LANES)):
    MIX_COLS[_name] = (_off, _off + _w)
    _off += _w
assert _off == IN_COLS
DN_HALO = 8
MXU_CHUNK = 256


def _front_kernel(x_ref, nw_ref, w_ref, wabt_ref, cw_ref, pcol_ref, prow_ref,
                  dww_ref, dwb_ref, lnw_ref, lnb_ref,
                  q_ref, k_ref, v_ref, gb_ref, grow_ref, z_ref, nq_ref, nkv_ref, small_ref, yc_ref,
                  dn_buf, cf_buf, cf_shift):
    i = pl.program_id(0)
    tm = ROW_TILE
    hd = NSA_HEAD_DIM

    @pl.when(lax.rem(i, SEQ // tm) == 0)
    def _():
        dn_buf[0:DN_HALO, :] = jnp.zeros((DN_HALO, 3 * DN_WIDTH), F32)
        cf_buf[0:CONV_HALO, :] = jnp.zeros((CONV_HALO, CONV_WIDTH), F32)

    xf = x_ref[...]
    hb = (xf * lax.rsqrt(jnp.mean(xf * xf, axis=-1, keepdims=True) + EPS) * nw_ref[...]).astype(BF16)

    def proj(name, lo=0, hi=None):
        c0, c1 = MIX_COLS[name]
        hi = c1 - c0 if hi is None else hi
        return jnp.dot(hb, w_ref[:, c0 + lo:c0 + hi], preferred_element_type=F32)

    def mxu_items():
        cu = proj('cu')
        cf_buf[CONV_HALO:CONV_HALO + tm, :] = cu[:, :CONV_WIDTH] * _sigmoid(cu[:, CONV_WIDTH:])
        yield
        sm = proj('small')
        small_ref[...] = sm
        lane = lax.broadcasted_iota(jnp.int32, sm.shape, 1)
        g_col = -jnp.exp(pcol_ref[0:1, :]) * _softplus(sm + pcol_ref[1:2, :])
        gb_ref[...] = jnp.where(lane < DN_HEADS, g_col, _sigmoid(sm))
        a_t = lax.dot_general(wabt_ref[...], hb, (((1,), (1,)), ((), ())), preferred_element_type=F32)
        g_row = -jnp.exp(prow_ref[:, 0:1]) * _softplus(a_t + prow_ref[:, 1:2])
        grow_ref[...] = g_row[0:8, :]
        yield
        for c0 in range(0, 3 * DN_WIDTH, MXU_CHUNK):
            dn_buf[DN_HALO:DN_HALO + tm, c0:c0 + MXU_CHUNK] = proj('qkv', c0, c0 + MXU_CHUNK)
            yield
        for c0 in range(0, DN_WIDTH, MXU_CHUNK):
            z_ref[:, c0:c0 + MXU_CHUNK] = proj('z', c0, c0 + MXU_CHUNK)
            yield
        res = proj('nq')
        for j in range(NSA_HEADS):
            nq_ref[j] = res[:, j * hd:(j + 1) * hd]
        yield
        res = proj('nkv')
        for j in range(6):
            nkv_ref[j] = res[:, j * hd:(j + 1) * hd]

    def conformer_items():
        span = tm + CONV_HALO - SUBLANES
        for b in range(1, SUBLANES):
            cf_shift[b] = cf_buf[b:b + span, :]
        yield
        base = CONV_HALO - (CONV_KERNEL - 1)
        rows = 64
        for rc in range(tm // rows):
            parts = []
            for cc in range(CONV_WIDTH // LANES):
                cs = slice(cc * LANES, (cc + 1) * LANES)
                acc = None
                for j in range(CONV_KERNEL):
                    start = base + rc * rows + j
                    b = start % SUBLANES
                    a0 = start - b
                    win = cf_buf[a0:a0 + rows, cs] if b == 0 else cf_shift[b, a0:a0 + rows, cs]
                    term = dww_ref[j:j + 1, cs] * win
                    acc = term if acc is None else acc + term
                parts.append(acc)
            h = jnp.concatenate(parts, axis=1) + dwb_ref[...]
            mu = jnp.mean(h, axis=-1, keepdims=True)
            var = jnp.mean(jnp.square(h - mu), axis=-1, keepdims=True)
            hn = (h - mu) * lax.rsqrt(var + EPS) * lnw_ref[...] + lnb_ref[...]
            yc_ref[rc * rows:(rc + 1) * rows, :] = _silu(hn)
            yield
        cf_buf[0:CONV_HALO, :] = cf_buf[tm:tm + CONV_HALO, :]

    def deltanet_items():
        rows = tm // 2
        for c in range(3 * DN_HEADS):
            cs = slice(c * LANES, (c + 1) * LANES)
            for r0 in range(0, tm, rows):
                lo = DN_HALO - (DN_CONV - 1) + r0
                acc = cw_ref[0:1, cs] * dn_buf[lo:lo + rows, cs]
                for j in range(1, DN_CONV):
                    acc = acc + cw_ref[j:j + 1, cs] * dn_buf[lo + j:lo + j + rows, cs]
                y = _silu(acc)
                if c < 2 * DN_HEADS:
                    y = y * lax.rsqrt(jnp.sum(y * y, axis=-1, keepdims=True) + EPS)
                rs = slice(r0, r0 + rows)
                if c < DN_HEADS:
                    q_ref[rs, cs] = y * DN_HEAD_DIM ** -0.5
                elif c < 2 * DN_HEADS:
                    k_ref[rs, (c - DN_HEADS) * LANES:(c - DN_HEADS + 1) * LANES] = y
                else:
                    v_ref[rs, (c - 2 * DN_HEADS) * LANES:(c - 2 * DN_HEADS + 1) * LANES] = y
            dn_buf[0:DN_HALO, cs] = dn_buf[tm:tm + DN_HALO, cs]
            yield

    mxu, cf, dn = mxu_items(), conformer_items(), deltanet_items()
    next(mxu)
    next(mxu)
    next(cf)
    next(mxu)
    _interleave([(mxu, 1), (dn, 2), (cf, 2)])


def _front(x2, norm_w, w_cat, w_abt, conv_w, pcol, prow, dw_w, dw_b, ln_w, ln_b):
    n = x2.shape[0]
    tm = ROW_TILE
    hd = NSA_HEAD_DIM
    row = lambda w: pl.BlockSpec((tm, w), lambda i: (i, 0))
    full = lambda shape: pl.BlockSpec(shape, lambda i: (0,) * len(shape))
    pieces = lambda k: pl.BlockSpec((k, tm, hd), lambda i: (0, i, 0))
    return pl.pallas_call(
        _front_kernel,
        out_shape=[jax.ShapeDtypeStruct((n, DN_WIDTH), F32)] * 3
        + [jax.ShapeDtypeStruct((n, LANES), F32), jax.ShapeDtypeStruct((8, n), F32),
           jax.ShapeDtypeStruct((n, DN_WIDTH), F32),
           jax.ShapeDtypeStruct((NSA_HEADS, n, hd), F32), jax.ShapeDtypeStruct((6, n, hd), F32),
           jax.ShapeDtypeStruct((n, LANES), F32), jax.ShapeDtypeStruct((n, CONV_WIDTH), F32)],
        grid=(n // tm,),
        in_specs=[row(D_MODEL), full((1, D_MODEL)), full((D_MODEL, IN_COLS)), full((16, D_MODEL)),
                  full((DN_CONV, 3 * DN_WIDTH)), full((8, LANES)), full((16, LANES)),
                  full((CONV_KERNEL, CONV_WIDTH)), full((1, CONV_WIDTH)), full((1, CONV_WIDTH)),
                  full((1, CONV_WIDTH))],
        out_specs=[row(DN_WIDTH)] * 3
        + [row(LANES), pl.BlockSpec((8, tm), lambda i: (0, i)), row(DN_WIDTH),
           pieces(NSA_HEADS), pieces(6), row(LANES), row(CONV_WIDTH)],
        scratch_shapes=[pltpu.VMEM((tm + DN_HALO, 3 * DN_WIDTH), F32),
                        pltpu.VMEM((tm + CONV_HALO, CONV_WIDTH), F32),
                        pltpu.VMEM((SUBLANES, tm + CONV_HALO - SUBLANES, CONV_WIDTH), F32)],
        compiler_params=_params(("arbitrary",)),
        name="front",
    )(x2, norm_w, w_cat, w_abt, conv_w, pcol, prow, dw_w, dw_b, ln_w, ln_b)


def _rms_rows(x, w):
    return x * lax.rsqrt(jnp.mean(x * x, axis=-1, keepdims=True) + EPS) * w


def _nsa_prep_kernel(kc_ref, vc_ref, ks_ref, kw_ref, pos_ref, w1_ref, w2_ref, knw_ref,
                     kcmp_ref, vcmp_ref, ksn_ref, kwn_ref):
    hd = NSA_HEAD_DIM

    def compress(x_ref, i):
        u_lo = jnp.zeros((N_CMP_PAD, CMP_HIDDEN), F32)
        u_hi = jnp.zeros((N_CMP_PAD, CMP_HIDDEN), F32)
        for r in range(CMP_STRIDE):
            xr = x_ref[pl.ds(r, N_CMP_PAD, stride=CMP_STRIDE), :]
            lo, hi = r, CMP_STRIDE + r
            u_lo = u_lo + _mm(xr + pos_ref[i, lo:lo + 1, :], w1_ref[i, lo * hd:(lo + 1) * hd, :])
            u_hi = u_hi + _mm(xr + pos_ref[i, hi:hi + 1, :], w1_ref[i, hi * hd:(hi + 1) * hd, :])
        hid = _silu(u_lo + pltpu.roll(u_hi, N_CMP_PAD - 1, axis=0))
        return _mm(hid, w2_ref[i])

    kcmp_ref[0] = _rms_rows(compress(kc_ref, 0), knw_ref[0:1, :])
    vcmp_ref[0] = compress(vc_ref, 1)
    ksn_ref[0] = _rms_rows(ks_ref[...], knw_ref[1:2, :])
    kwn_ref[0] = _rms_rows(kw_ref[...], knw_ref[2:3, :])


def _nsa_prep(kv6, pos, w1, w2, knw):
    b = kv6.shape[1]
    hd = NSA_HEAD_DIM
    full = lambda shape: pl.BlockSpec(shape, lambda i: (0,) * len(shape))
    piece = lambda j: pl.BlockSpec((None, None, SEQ, hd), lambda i: (j, i, 0, 0))
    bspec = lambda r, w: pl.BlockSpec((1, r, w), lambda i: (i, 0, 0))
    return pl.pallas_call(
        _nsa_prep_kernel,
        out_shape=[jax.ShapeDtypeStruct((b, N_CMP_PAD, hd), F32)] * 2
        + [jax.ShapeDtypeStruct((b, SEQ, hd), F32)] * 2,
        grid=(b,),
        in_specs=[
            piece(0), piece(1), piece(2), piece(4),
            full((2, CMP_LEN, hd)), full((2, CMP_LEN * hd, CMP_HIDDEN)), full((2, CMP_HIDDEN, hd)),
            full((3, hd)),
        ],
        out_specs=[bspec(N_CMP_PAD, hd)] * 2 + [bspec(SEQ, hd)] * 2,
        compiler_params=_params(("parallel",)),
        name="nsa_prep",
    )(kv6, kv6, kv6, kv6, pos, w1, w2, knw)


def _nsa_attn_kernel(q_ref, qnw_ref, kcmp_ref, vcmp_ref, ks_ref, vs_ref, kw_ref, vw_ref,
                     gate_ref, bcmp_ref, btab_ref, o_ref,
                     madd_ref, s_ref, ksb_ref, kwb_ref, vsa_ref, vwa_ref):
    i = pl.program_id(1)
    tq = Q_TILE
    nh = NSA_HEADS
    hd = NSA_HEAD_DIM
    t0 = i * tq

    @pl.when(i == 0)
    def _():
        ones = jnp.ones((SEQ, hd), BF16)
        ksb_ref[...] = ks_ref[0].astype(BF16)
        kwb_ref[...] = kw_ref[0].astype(BF16)
        vsa_ref[...] = jnp.concatenate([vs_ref[...].astype(BF16), ones], axis=1)
        vwa_ref[...] = jnp.concatenate([vw_ref[...].astype(BF16), ones], axis=1)

    qs = jnp.concatenate(
        [_rms_rows(q_ref[h], qnw_ref[...]) * hd ** -0.5 for h in range(nh)], axis=0).astype(BF16)

    row = lax.broadcasted_iota(jnp.int32, (tq, LANES), 0)
    lane = lax.broadcasted_iota(jnp.int32, (tq, LANES), 1)
    qpos = t0 + row

    branch_out = {}

    def select_items():
        s_all = _mm_nt(qs, kcmp_ref[0])
        cmp_valid = (qpos >= lane * CMP_STRIDE + (CMP_LEN - 1)) & (lane < N_CMP_PAD - 1)
        p_rows = []
        p_sum = jnp.zeros((tq, LANES), F32)
        for h in range(nh):
            s = jnp.where(cmp_valid, s_all[h * tq:(h + 1) * tq] + bcmp_ref[h], NEG_INF)
            m = jnp.max(s, axis=-1, keepdims=True)
            p = jnp.where(cmp_valid, jnp.exp(s - m), 0.0)
            l = jnp.sum(p, axis=-1, keepdims=True)
            p = p * jnp.where(l > 0.0, 1.0 / l, 0.0)
            p_rows.append(p)
            p_sum = p_sum + p
            yield
        branch_out['cmp'] = _mm(jnp.concatenate(p_rows, axis=0), vcmp_ref[0])

        ss = lax.broadcasted_iota(jnp.int32, (N_SLC, N_CMP_PAD), 0)
        jj = lax.broadcasted_iota(jnp.int32, (N_SLC, N_CMP_PAD), 1)
        overlap_t = ((jj * CMP_STRIDE < ss * SLC_BLOCK + SLC_BLOCK)
                     & (jj * CMP_STRIDE + CMP_LEN > ss * SLC_BLOCK) & (jj < N_CMP_PAD - 1))
        overlap_t = jnp.where(overlap_t, 1.0, 0.0).astype(BF16)
        nt = lambda a, b_: lax.dot_general(a, b_, (((1,), (1,)), ((), ())), preferred_element_type=F32)
        p_hi, p_mid, p_lo = _split3(p_sum)
        imp = nt(overlap_t, p_hi) + nt(overlap_t, p_mid) + nt(overlap_t, p_lo)
        blk = lax.broadcasted_iota(jnp.int32, (N_SLC, tq), 0)
        cur = (t0 + lax.broadcasted_iota(jnp.int32, (N_SLC, tq), 1)) // SLC_BLOCK
        causal_blk = blk <= cur
        forced = (blk == 0) | (blk == cur) | (blk == cur - 1)
        imp = jnp.where(causal_blk & forced, FORCE, jnp.where(causal_blk, imp, -1.0))
        yield
        rank = jnp.zeros((N_SLC, tq), jnp.int32)
        for s2 in range(N_SLC):
            other = jnp.broadcast_to(imp[s2:s2 + 1, :], (N_SLC, tq))
            beats = (other > imp) | ((other == imp) & (blk > s2))
            rank = rank + jnp.where(beats, 1, 0)
            if s2 % 8 == 7:
                yield
        sel_t = jnp.where(rank < SLC_TOP_N, 1.0, 0.0).astype(BF16)
        er = lax.broadcasted_iota(jnp.int32, (N_SLC, SEQ), 0)
        ec = lax.broadcasted_iota(jnp.int32, (N_SLC, SEQ), 1)
        expand = jnp.where(ec // SLC_BLOCK == er, 1.0, 0.0).astype(BF16)
        key = lax.broadcasted_iota(jnp.int32, (tq, SEQ), 1)
        qall = t0 + lax.broadcasted_iota(jnp.int32, (tq, SEQ), 0)
        chosen = (_mm_tn(sel_t, expand) > 0.5) & (qall >= key)
        madd_ref[...] = jnp.where(chosen, 0.0, NEG_INF)

    neg = jnp.full((nh * tq, tq), NEG_INF, F32)
    zero = jnp.zeros((nh * tq, LANES), F32)

    def scores(k_tile, bias_idx, add):
        s_t = _mm_nt(qs, k_tile)
        return jnp.concatenate(
            [s_t[h * tq:(h + 1) * tq] + (btab_ref[h, bias_idx] + add) for h in range(nh)], axis=0)

    def normalise(acc):
        return acc[:, :hd] * (1.0 / _bcast_col(acc, hd, hd))

    def window_items():
        kcol = lax.broadcasted_iota(jnp.int32, (tq, tq), 1)
        qrow = t0 + lax.broadcasted_iota(jnp.int32, (tq, tq), 0)
        n_win = WINDOW // tq + 1
        starts = []
        m_run = neg
        for j in range(n_win):
            d = n_win - 1 - j
            k0 = pl.multiple_of(jnp.maximum(i - d, 0) * tq, tq)
            starts.append(k0)
            dist = qrow - (k0 + kcol)
            ok = (dist >= 0) & (dist < WINDOW) & (i >= d)
            s_m = scores(kwb_ref[pl.ds(k0, tq), :], min(d, 2), jnp.where(ok, 0.0, NEG_INF))
            s_ref[:, j * tq:(j + 1) * tq] = s_m
            m_run = jnp.maximum(m_run, s_m)
            yield
        m_win = jnp.broadcast_to(jnp.max(m_run, axis=-1, keepdims=True), (nh * tq, tq))
        acc = zero
        for j in range(n_win):
            p = jnp.exp(s_ref[:, j * tq:(j + 1) * tq] - m_win).astype(BF16)
            acc = acc + jnp.dot(p, vwa_ref[pl.ds(starts[j], tq), :], preferred_element_type=F32)
            yield
        branch_out['win'] = normalise(acc)

    _interleave([(select_items(), 1), (window_items(), 1)])
    o_cmp, o_win = branch_out['cmp'], branch_out['win']

    n_groups = (i + SLC_GROUP) // SLC_GROUP

    def slc_pass1(j, m_run):
        for u in range(SLC_GROUP):
            kt = SLC_GROUP * j + u
            k0 = pl.multiple_of(kt * tq, tq)
            s_m = scores(ksb_ref[pl.ds(k0, tq), :], jnp.clip(i - kt, 0, 2), madd_ref[:, pl.ds(k0, tq)])
            s_ref[:, pl.ds(k0, tq)] = s_m
            m_run = jnp.maximum(m_run, s_m)
        return m_run

    m_slc = jnp.broadcast_to(
        jnp.max(lax.fori_loop(0, n_groups, slc_pass1, neg), axis=-1, keepdims=True), (nh * tq, tq))

    def slc_pass2(j, acc):
        for u in range(SLC_GROUP):
            k0 = pl.multiple_of((SLC_GROUP * j + u) * tq, tq)
            p = jnp.exp(s_ref[:, pl.ds(k0, tq)] - m_slc).astype(BF16)
            acc = acc + jnp.dot(p, vsa_ref[pl.ds(k0, tq), :], preferred_element_type=F32)
        return acc

    o_slc = normalise(lax.fori_loop(0, n_groups, slc_pass2, zero))

    gates = _sigmoid(gate_ref[0])
    outs = []
    for h in range(nh):
        hs = slice(h * tq, (h + 1) * tq)
        g0 = 2 * DN_HEADS + 3 * h
        outs.append(_bcast_col(gates, g0, hd) * o_cmp[hs]
                    + _bcast_col(gates, g0 + 1, hd) * o_slc[hs]
                    + _bcast_col(gates, g0 + 2, hd) * o_win[hs])
    o_ref[...] = jnp.concatenate(outs, axis=1)


def _nsa_attn(q, qnw, kcmp, vcmp, ks, kv6, kw, small, bias_cmp, btab):
    b = kv6.shape[1]
    tq = Q_TILE
    nq = SEQ // tq
    nh, hd = NSA_HEADS, NSA_HEAD_DIM
    full = lambda shape: pl.BlockSpec(shape, lambda bi, i: (0,) * len(shape))
    per_b = lambda r: pl.BlockSpec((1, r, hd), lambda bi, i: (bi, 0, 0))
    piece = lambda j: pl.BlockSpec((None, None, SEQ, hd), lambda bi, i: (j, bi, 0, 0))
    heads = pl.BlockSpec((nh, tq, hd), lambda bi, i: (0, bi * nq + i, 0))
    return pl.pallas_call(
        _nsa_attn_kernel,
        out_shape=jax.ShapeDtypeStruct((b * SEQ, NSA_WIDTH), F32),
        grid=(b, nq),
        in_specs=[
            heads,
            full((1, hd)),
            per_b(N_CMP_PAD), per_b(N_CMP_PAD), per_b(SEQ), piece(3), per_b(SEQ), piece(5),
            pl.BlockSpec((1, tq, LANES), lambda bi, i: (bi, i, 0)),
            pl.BlockSpec((nh, tq, N_CMP_PAD), lambda bi, i: (0, i, 0)),
            full((nh, 3, tq, tq)),
        ],
        out_specs=pl.BlockSpec((tq, NSA_WIDTH), lambda bi, i: (bi * nq + i, 0)),
        scratch_shapes=[pltpu.VMEM((tq, SEQ), F32), pltpu.VMEM((nh * tq, SEQ), F32),
                        pltpu.VMEM((SEQ, hd), BF16), pltpu.VMEM((SEQ, hd), BF16),
                        pltpu.VMEM((SEQ, 2 * hd), BF16), pltpu.VMEM((SEQ, 2 * hd), BF16)],
        compiler_params=_params(("parallel", "arbitrary")),
        name="nsa_attn",
    )(q, qnw, kcmp, vcmp, ks, kv6, kw, kv6, small, bias_cmp, btab)


def _out_router_kernel(ya_ref, yb_ref, yc_ref, x_ref, wo_ref, fnw_ref, rw_ref, rb_ref,
                       xo_ref, h_ref, idx_ref, wt_ref, hist_ref):
    wa = DN_WIDTH
    wb = wa + NSA_WIDTH
    y = (jnp.dot(ya_ref[...].astype(BF16), wo_ref[0:wa, :], preferred_element_type=F32)
         + jnp.dot(yb_ref[...].astype(BF16), wo_ref[wa:wb, :], preferred_element_type=F32)
         + jnp.dot(yc_ref[...].astype(BF16), wo_ref[wb:, :], preferred_element_type=F32))
    xn = x_ref[...] + y
    xo_ref[...] = xn
    h = xn * lax.rsqrt(jnp.mean(xn * xn, axis=-1, keepdims=True) + EPS) * fnw_ref[...]
    _store_tile_rows(h_ref, h, h.shape[0])
    h_hi, h_lo = _hi_lo(h)
    r_hi = jnp.dot(h_hi, rw_ref[...], preferred_element_type=F32)
    r_lo = jnp.dot(h_lo, rw_ref[...], preferred_element_type=F32)
    logits = ((r_hi[:, :LANES] + r_hi[:, LANES:]) + (r_lo[:, :LANES] + r_lo[:, LANES:])) + rb_ref[...]
    lane = lax.broadcasted_iota(jnp.int32, logits.shape, 1)
    vals, idxs = [], []
    for _ in range(TOP_K):
        m = jnp.max(logits, axis=-1, keepdims=True)
        ix = jnp.min(jnp.where(logits == m, lane, LANES), axis=-1, keepdims=True)
        vals.append(m)
        idxs.append(ix)
        logits = jnp.where(lane == ix, -jnp.inf, logits)
    es = [jnp.exp(v - vals[0]) for v in vals]
    inv = 1.0 / (es[0] + es[1] + es[2] + es[3])
    idx_out = jnp.zeros(lane.shape, jnp.int32)
    wt_out = jnp.zeros(lane.shape, F32)
    chosen = jnp.zeros(lane.shape, F32)
    for k in range(TOP_K):
        idx_out = jnp.where(lane == k, idxs[k], idx_out)
        wt_out = jnp.where(lane == k, es[k] * inv, wt_out)
        chosen = chosen + jnp.where(lane == idxs[k], 1.0, 0.0)
    idx_ref[...] = idx_out
    wt_ref[...] = wt_out
    hist_ref[0] = jnp.broadcast_to(jnp.sum(chosen, axis=0, keepdims=True), (8, LANES))


def _out_router(ya, yb, yc, x2, w_out, fnw, rw, rb):
    n = x2.shape[0]
    tm = ROW_TILE
    row = lambda w: pl.BlockSpec((tm, w), lambda i: (i, 0))
    full = lambda shape: pl.BlockSpec(shape, lambda i: (0,) * len(shape))
    return pl.pallas_call(
        _out_router_kernel,
        out_shape=[jax.ShapeDtypeStruct((n, D_MODEL), F32), jax.ShapeDtypeStruct((n * SUBLANES, LANES), F32),
                   jax.ShapeDtypeStruct((n, LANES), jnp.int32), jax.ShapeDtypeStruct((n, LANES), F32),
                   jax.ShapeDtypeStruct((n // tm, 8, LANES), F32)],
        grid=(n // tm,),
        in_specs=[row(DN_WIDTH), row(NSA_WIDTH), row(CONV_WIDTH), row(D_MODEL),
                  full((D_MODEL, D_MODEL)), full((1, D_MODEL)), full((D_MODEL, 2 * LANES)), full((1, LANES))],
        out_specs=[row(D_MODEL), pl.BlockSpec((tm * SUBLANES, LANES), lambda i: (i, 0)), row(LANES), row(LANES),
                   pl.BlockSpec((1, 8, LANES), lambda i: (i, 0, 0))],
        compiler_params=_params(("parallel",)),
        name="out_router",
    )(ya, yb, yc, x2, w_out, fnw, rw, rb)


def _slots_kernel(idx_ref, base_ref, slot_ref):
    tm = ROW_TILE
    idx = idx_ref[...]
    lane = lax.broadcasted_iota(jnp.int32, (tm, LANES), 1)
    onehots = [jnp.where(lane == _bcast_col(idx, k, LANES), 1.0, 0.0) for k in range(TOP_K)]
    cnt = (onehots[0] + onehots[1]) + (onehots[2] + onehots[3])
    r = lax.broadcasted_iota(jnp.int32, (tm, tm), 0)
    c = lax.broadcasted_iota(jnp.int32, (tm, tm), 1)
    earlier = jnp.where(r > c, 1.0, 0.0).astype(BF16)
    rank = jnp.dot(earlier, cnt.astype(BF16), preferred_element_type=F32) + base_ref[0, 0:1, :]
    out = jnp.zeros((tm, LANES), F32)
    for k in range(TOP_K):
        out = jnp.where(lane == k, jnp.sum(rank * onehots[k], axis=-1, keepdims=True), out)
    slot_ref[...] = (out * SUBLANES).astype(jnp.int32)


def _slots(idx, base3):
    n = idx.shape[0]
    tm = ROW_TILE
    return pl.pallas_call(
        _slots_kernel,
        out_shape=jax.ShapeDtypeStruct((n, LANES), jnp.int32),
        grid=(n // tm,),
        in_specs=[pl.BlockSpec((tm, LANES), lambda i: (i, 0)),
                  pl.BlockSpec((1, 8, LANES), lambda i: (i, 0, 0))],
        out_specs=pl.BlockSpec((tm, LANES), lambda i: (i, 0)),
        compiler_params=_params(("parallel",)),
        name="moe_slots",
    )(idx, base3)


def _dispatch_kernel(slot_ref, zoff_ref, nv_ref, h_ref, xs_hbm, zbuf, sem, zsem):
    i = pl.program_id(0)
    tm = DISPATCH_TILE
    base = i * tm

    @pl.when(i == 0)
    def _():
        zbuf[...] = jnp.zeros(zbuf.shape, F32)
        blk = MOE_TILE * SUBLANES
        n_blocks = xs_hbm.shape[0] // blk

        def fill(row0):
            return pltpu.make_async_copy(
                zbuf, xs_hbm.at[pl.ds(pl.multiple_of(row0 * SUBLANES, blk), blk)], zsem)

        def fill_region_end(e, c):
            fill(zoff_ref[e]).start()
            return c

        def fill_tail(b, c):
            fill(b * MOE_TILE).start()
            return c

        def wait_fill(b, c):
            fill(0).wait()
            return c

        lax.fori_loop(0, N_EXPERTS, fill_region_end, 0)
        lax.fori_loop(nv_ref[0], n_blocks, fill_tail, 0)
        lax.fori_loop(0, N_EXPERTS + n_blocks - nv_ref[0], wait_fill, 0)

    def body(r, c):
        src = h_ref.at[pl.ds(pl.multiple_of(r * SUBLANES, SUBLANES), SUBLANES)]
        for k in range(TOP_K):
            row0 = pl.multiple_of(slot_ref[(base + r) * TOP_K + k], SUBLANES)
            pltpu.make_async_copy(src, xs_hbm.at[pl.ds(row0, SUBLANES)], sem).start(priority=k % 2)
        return c

    lax.fori_loop(0, tm, body, 0, unroll=8)
    done = xs_hbm.at[pl.ds(0, TOP_K * tm * SUBLANES)]
    pltpu.make_async_copy(done, done, sem).wait()


def _dispatch(slot_flat, zero_off, n_valid, h, n_slots):
    n = h.shape[0] // SUBLANES
    tm = DISPATCH_TILE
    return pl.pallas_call(
        _dispatch_kernel,
        out_shape=jax.ShapeDtypeStruct((n_slots * SUBLANES, LANES), F32),
        grid_spec=pltpu.PrefetchScalarGridSpec(
            num_scalar_prefetch=3,
            grid=(n // tm,),
            in_specs=[pl.BlockSpec((tm * SUBLANES, LANES), lambda i, s, z, nv: (i, 0))],
            out_specs=pl.BlockSpec(memory_space=pl.ANY),
            scratch_shapes=[pltpu.VMEM((MOE_TILE * SUBLANES, LANES), F32),
                            pltpu.SemaphoreType.DMA, pltpu.SemaphoreType.DMA],
        ),
        compiler_params=_params(("arbitrary",)),
        name="moe_dispatch",
    )(slot_flat, zero_off, n_valid, h)


def _expert_kernel(be_ref, nv_ref, xs_ref, wgu_ref, bgu_ref, wd_ref, bd_ref, o_ref, wgu_bf, wd_bf):
    i = pl.program_id(0)
    new_expert = (i == 0) | (be_ref[i] != be_ref[jnp.maximum(i - 1, 0)])

    @pl.when((i < nv_ref[0]) & new_expert)
    def _():
        step = 4 * LANES
        for c0 in range(0, 2 * D_FF, step):
            wgu_bf[:, c0:c0 + step] = wgu_ref[:, c0:c0 + step].astype(BF16)
        for c0 in range(0, D_MODEL, step):
            wd_bf[:, c0:c0 + step] = wd_ref[:, c0:c0 + step].astype(BF16)

    @pl.when(i < nv_ref[0])
    def _():
        xb = _load_tile_rows(xs_ref, MOE_TILE).astype(BF16)
        gu = jnp.dot(xb, wgu_bf[...], preferred_element_type=F32) + bgu_ref[...]
        gate = jnp.minimum(gu[:, :D_FF], SWIGLU_LIMIT)
        up = jnp.clip(gu[:, D_FF:], -SWIGLU_LIMIT, SWIGLU_LIMIT)
        act = (up + 1.0) * gate * _sigmoid(SWIGLU_ALPHA * gate)
        y = jnp.dot(act.astype(BF16), wd_bf[...], preferred_element_type=F32) + bd_ref[...]
        _store_tile_rows(o_ref, y, MOE_TILE)

    @pl.when(i >= nv_ref[0])
    def _():
        o_ref[...] = jnp.zeros(o_ref.shape, F32)


def _expert_ffn(layer, block_expert, n_valid, xs, wgu, bgu, wd, bd):
    n_slots = xs.shape[0] // SUBLANES
    tm = MOE_TILE
    return pl.pallas_call(
        _expert_kernel,
        out_shape=jax.ShapeDtypeStruct(xs.shape, F32),
        grid_spec=pltpu.PrefetchScalarGridSpec(
            num_scalar_prefetch=2,
            grid=(n_slots // tm,),
            in_specs=[
                pl.BlockSpec((tm * SUBLANES, LANES), lambda i, be, nv: (i, 0)),
                pl.BlockSpec((None, None, D_MODEL, 2 * D_FF), lambda i, be, nv: (layer, be[i], 0, 0)),
                pl.BlockSpec((None, None, 1, 2 * D_FF), lambda i, be, nv: (layer, be[i], 0, 0)),
                pl.BlockSpec((None, None, D_FF, D_MODEL), lambda i, be, nv: (layer, be[i], 0, 0)),
                pl.BlockSpec((None, None, 1, D_MODEL), lambda i, be, nv: (layer, be[i], 0, 0)),
            ],
            out_specs=pl.BlockSpec((tm * SUBLANES, LANES), lambda i, be, nv: (i, 0)),
            scratch_shapes=[pltpu.VMEM((D_MODEL, 2 * D_FF), BF16), pltpu.VMEM((D_FF, D_MODEL), BF16)],
        ),
        compiler_params=pltpu.CompilerParams(dimension_semantics=("arbitrary",),
                                             vmem_limit_bytes=EXPERT_VMEM_LIMIT_BYTES),
        name="expert_ffn",
    )(block_expert, n_valid, xs, wgu, bgu, wd, bd)


def _combine_kernel(slot_ref, ys_hbm, x_ref, wt_ref, o_ref, buf, acc_ref, sems):
    i = pl.program_id(0)
    tc = COMBINE_TILE
    nsteps = pl.num_programs(0)
    cur = lax.rem(i, 2)

    def issue(step, par):
        base = step * tc

        def body(r, c):
            dst0 = pl.multiple_of(r * SUBLANES, SUBLANES)
            for k in range(TOP_K):
                row0 = pl.multiple_of(slot_ref[(base + r) * TOP_K + k], SUBLANES)
                pltpu.make_async_copy(ys_hbm.at[pl.ds(row0, SUBLANES)],
                                      buf.at[par, k, pl.ds(dst0, SUBLANES)],
                                      sems.at[par]).start(priority=k % 2)
            return c

        lax.fori_loop(0, tc, body, 0, unroll=8)

    @pl.when(i == 0)
    def _():
        issue(0, 0)

    @pl.when(i + 1 < nsteps)
    def _():
        issue(i + 1, 1 - cur)

    pltpu.make_async_copy(buf.at[cur], buf.at[cur], sems.at[cur]).wait()
    wt = wt_ref[...]
    acc = None
    for k in range(TOP_K):
        w_tiles = jnp.broadcast_to(_bcast_col(wt, k, LANES)[:, None, :], (tc, SUBLANES, LANES))
        term = w_tiles.reshape(tc * SUBLANES, LANES) * buf[cur, k]
        acc = term if acc is None else acc + term
    acc_ref[...] = acc
    for s in range(ROW_CHUNKS):
        cs = slice(s * LANES, (s + 1) * LANES)
        o_ref[:, cs] = x_ref[:, cs] + acc_ref[pl.ds(s, tc, stride=SUBLANES), :]


def _combine(slot_flat, ys, x2, wt):
    n = x2.shape[0]
    tc = COMBINE_TILE
    return pl.pallas_call(
        _combine_kernel,
        out_shape=jax.ShapeDtypeStruct((n, D_MODEL), F32),
        grid_spec=pltpu.PrefetchScalarGridSpec(
            num_scalar_prefetch=1,
            grid=(n // tc,),
            in_specs=[pl.BlockSpec(memory_space=pl.ANY),
                      pl.BlockSpec((tc, D_MODEL), lambda i, s: (i, 0)),
                      pl.BlockSpec((tc, LANES), lambda i, s: (i, 0))],
            out_specs=pl.BlockSpec((tc, D_MODEL), lambda i, s: (i, 0)),
            scratch_shapes=[pltpu.VMEM((2, TOP_K, tc * SUBLANES, LANES), F32),
                            pltpu.VMEM((tc * SUBLANES, LANES), F32), pltpu.SemaphoreType.DMA((2,))],
        ),
        compiler_params=_params(("arbitrary",)),
        name="moe_combine",
    )(slot_flat, ys, x2, wt)


def _moe(layer, h, x2, idx, wt, hist3, wgu, bgu, wd, bd):
    n = x2.shape[0]
    tm = MOE_TILE
    n_slots = n * TOP_K + N_EXPERTS * tm
    n_blocks = n_slots // tm
    hist = hist3[:, 0, :N_EXPERTS]
    counts = jnp.sum(hist, axis=0).astype(jnp.int32)
    padded = (counts + tm - 1) // tm * tm
    pad_end = jnp.cumsum(padded)
    pad_start = pad_end - padded
    tile_base = pad_start[None, :].astype(F32) + (jnp.cumsum(hist, axis=0) - hist)
    base3 = jnp.broadcast_to(jnp.pad(tile_base, ((0, 0), (0, LANES - N_EXPERTS)))[:, None, :],
                             (hist.shape[0], 8, LANES))
    blk0 = jnp.arange(n_blocks) * tm
    block_expert = jnp.minimum(jnp.sum(blk0[:, None] >= pad_end[None, :], axis=1), N_EXPERTS - 1).astype(jnp.int32)
    n_valid = (pad_end[-1:] // tm).astype(jnp.int32)
    zero_off = jnp.maximum(pad_end - tm, 0).astype(jnp.int32)

    slot_flat = _slots(idx, base3)[:, :TOP_K].reshape(-1)
    xs = _dispatch(slot_flat, zero_off, n_valid, h, n_slots)
    ys = _expert_ffn(layer, block_expert, n_valid, xs, wgu, bgu, wd, bd)
    return _combine(slot_flat, ys, x2, wt)


def _t5_bucket(dist):
    n = jnp.maximum(dist, 0)
    max_exact = REL_BUCKETS // 2
    nf = jnp.maximum(n, 1).astype(F32)
    large = max_exact + (jnp.log(nf / max_exact) / math.log(REL_MAX_DIST / max_exact)
                         * (REL_BUCKETS - max_exact)).astype(jnp.int32)
    large = jnp.minimum(large, REL_BUCKETS - 1)
    return jnp.where(n < max_exact, n, large)


def _bias_tables(rel_bias):
    rel_bias = rel_bias.astype(F32)
    tq = Q_TILE

    def lookup(dist):
        bucket = _t5_bucket(dist)
        out = jnp.zeros((NSA_HEADS,) + dist.shape, F32)
        for bk in range(REL_BUCKETS):
            out = jnp.where(bucket[None] == bk, rel_bias[bk].reshape((NSA_HEADS,) + (1,) * dist.ndim), out)
        return out

    t_pos = jnp.arange(SEQ)
    cmp_end = jnp.arange(N_CMP_PAD) * CMP_STRIDE + CMP_LEN - 1
    bias_cmp = lookup(t_pos[:, None] - cmp_end[None, :])
    rr = jnp.arange(tq)[:, None] - jnp.arange(tq)[None, :]
    btab = lookup(jnp.stack([rr, rr + tq, rr + 2 * tq]))
    return bias_cmp, btab


def _layer(layer, x2, b, p, experts, bias_cmp, btab):
    n = x2.shape[0]
    w = p['w_in']
    o_a = 4 * DN_WIDTH
    o_q = o_a + 2 * DN_HEADS
    o_kv = o_q + NSA_WIDTH
    o_g = o_kv + 6 * NSA_HEAD_DIM
    o_u = o_g + 3 * NSA_HEADS
    w_small = jnp.concatenate([w[:, o_a:o_q], w[:, o_g:o_u],
                               jnp.zeros((D_MODEL, LANES - 2 * DN_HEADS - 3 * NSA_HEADS), F32)], axis=1)
    w_cat = jnp.concatenate([w[:, :o_a], w[:, o_q:o_kv], w[:, o_kv:o_g], w[:, o_u:], w_small],
                            axis=1).astype(BF16)
    w_abt = jnp.concatenate([w[:, o_a:o_q].T, jnp.zeros((8, D_MODEL), F32)], axis=0).astype(BF16)
    pcol = jnp.zeros((8, LANES), F32).at[0, :DN_HEADS].set(p['dn_a_log']).at[1, :DN_HEADS].set(p['dn_dt_bias'])
    prow = jnp.zeros((16, LANES), F32).at[:DN_HEADS, 0].set(p['dn_a_log']).at[:DN_HEADS, 1].set(p['dn_dt_bias'])
    dq, dk, dv, gb, grow, z, nq, nkv, small, y_c = _front(
        x2, p['attn_norm_w'][None, :], w_cat, w_abt, p['dn_conv_w'], pcol, prow,
        p['conv_dw_w'], p['conv_dw_b'][None, :], p['conv_ln_w'][None, :], p['conv_ln_b'][None, :])
    small3 = small.reshape(b, SEQ, LANES)

    seq = lambda a: a.reshape(b, SEQ, a.shape[-1])
    y_a = _delta_rule(seq(dq), seq(dk), seq(dv), seq(gb), grow, seq(z), p['dn_norm_w'][None, :])

    kv6 = nkv.reshape(6, b, SEQ, NSA_HEAD_DIM)
    kcmp, vcmp, ksn, kwn = _nsa_prep(kv6, p['nsa_cmp_pos'], p['nsa_cmp_w1'].astype(BF16),
                                     p['nsa_cmp_w2'].astype(BF16), p['nsa_k_norm_w'])
    y_b = _nsa_attn(nq, p['nsa_q_norm_w'][None, :], kcmp, vcmp, ksn, kv6, kwn, small3, bias_cmp, btab)

    rw = jnp.concatenate(_hi_lo(jnp.pad(p['router_w'], ((0, 0), (0, LANES - N_EXPERTS)))), axis=1)
    rb = jnp.concatenate([p['router_b'], jnp.full((LANES - N_EXPERTS,), NEG_INF, F32)])[None, :]
    x_new, h, idx, wt, hist3 = _out_router(y_a.reshape(n, DN_WIDTH), y_b, y_c.reshape(n, CONV_WIDTH), x2,
                                           p['w_out'].astype(BF16), p['ffn_norm_w'][None, :], rw, rb)
    return _moe(layer, h, x_new, idx, wt, hist3, *experts)


def kernel(x, attn_norm_w, w_in, dn_conv_w, dn_a_log, dn_dt_bias, dn_norm_w, nsa_q_norm_w, nsa_k_norm_w, nsa_cmp_pos, nsa_cmp_w1, nsa_cmp_w2, conv_dw_w, conv_dw_b, conv_ln_w, conv_ln_b, w_out, ffn_norm_w, router_w, router_b, w_gate_up, b_gate_up, w_down, b_down, rel_bias):
    b, t, d = x.shape
    assert (t, d) == (SEQ, D_MODEL)
    stacked = dict(attn_norm_w=attn_norm_w, w_in=w_in, dn_conv_w=dn_conv_w, dn_a_log=dn_a_log,
                   dn_dt_bias=dn_dt_bias, dn_norm_w=dn_norm_w, nsa_q_norm_w=nsa_q_norm_w,
                   nsa_k_norm_w=nsa_k_norm_w, nsa_cmp_pos=nsa_cmp_pos, nsa_cmp_w1=nsa_cmp_w1,
                   nsa_cmp_w2=nsa_cmp_w2, conv_dw_w=conv_dw_w, conv_dw_b=conv_dw_b,
                   conv_ln_w=conv_ln_w, conv_ln_b=conv_ln_b, w_out=w_out, ffn_norm_w=ffn_norm_w,
                   router_w=router_w, router_b=router_b)
    experts = (w_gate_up, b_gate_up[:, :, None, :], w_down, b_down[:, :, None, :])
    bias_cmp, btab = _bias_tables(rel_bias)
    x2 = x.reshape(b * t, d)
    for l in range(w_in.shape[0]):
        x2 = _layer(l, x2, b, {k: v[l] for k, v in stacked.items()}, experts, bias_cmp, btab)
    return x2.reshape(b, t, d)
```

```python
import math

import jax
import jax.numpy as jnp
from jax import lax
from jax.experimental import pallas as pl
from jax.experimental.pallas import tpu as pltpu

F32 = jnp.float32
BF16 = jnp.bfloat16

D_MODEL = 1024
SEQ = 2048
DN_HEADS = 4
DN_HEAD_DIM = 128
DN_WIDTH = DN_HEADS * DN_HEAD_DIM
DN_CONV = 4
DN_CHUNK = 64
NSA_HEADS = 4
NSA_HEAD_DIM = 64
NSA_WIDTH = NSA_HEADS * NSA_HEAD_DIM
CMP_LEN = 32
CMP_STRIDE = 16
CMP_HIDDEN = 2 * NSA_HEAD_DIM
SLC_BLOCK = 64
SLC_TOP_N = 16
WINDOW = 512
CONV_WIDTH = 256
CONV_KERNEL = 31
REL_BUCKETS = 32
REL_MAX_DIST = 128
N_EXPERTS = 32
TOP_K = 4
D_FF = D_MODEL
SWIGLU_LIMIT = 7.0
SWIGLU_ALPHA = 1.702
EPS = 1e-6
NEG_INF = -1e30
FORCE = 1e4

LANES = 128
VMEM_LIMIT_BYTES = 48 * 1024 * 1024
EXPERT_VMEM_LIMIT_BYTES = 56 * 1024 * 1024

ROW_TILE = 512
SEQ_TILE = 256
Q_TILE = 256
SLC_GROUP = 2
MOE_TILE = 512
COMBINE_TILE = 256
DISPATCH_TILE = 1024

N_CMP_PAD = 128
N_SLC = SEQ // SLC_BLOCK


def _params(sem=None):
    return pltpu.CompilerParams(dimension_semantics=sem, vmem_limit_bytes=VMEM_LIMIT_BYTES)


def _mm(a, b):
    return jnp.dot(a.astype(BF16), b.astype(BF16), preferred_element_type=F32)


def _mm_nt(a, b):
    return lax.dot_general(a.astype(BF16), b.astype(BF16), (((1,), (1,)), ((), ())),
                           preferred_element_type=F32)


def _mm_tn(a, b):
    return lax.dot_general(a.astype(BF16), b.astype(BF16), (((0,), (0,)), ((), ())),
                           preferred_element_type=F32)


def _split3(x):
    hi = x.astype(BF16)
    r1 = x - hi.astype(F32)
    mid = r1.astype(BF16)
    lo = (r1 - mid.astype(F32)).astype(BF16)
    return hi, mid, lo


def _dot01_right(x, m01):
    hi, mid, lo = _split3(x)
    d = lambda p: jnp.dot(p, m01, preferred_element_type=F32)
    return d(hi) + d(mid) + d(lo)


def _dot01_left(m01, x):
    hi, mid, lo = _split3(x)
    d = lambda p: jnp.dot(m01, p, preferred_element_type=F32)
    return d(hi) + d(mid) + d(lo)


def _hi_lo(x):
    hi = x.astype(BF16)
    return hi, (x - hi.astype(F32)).astype(BF16)


def _sigmoid(x):
    return 1.0 / (1.0 + jnp.exp(-x))


def _silu(x):
    return x * _sigmoid(x)


def _softplus(x):
    return jnp.maximum(x, 0.0) + jnp.log(1.0 + jnp.exp(-jnp.abs(x)))


SUBLANES = 8
ROW_CHUNKS = D_MODEL // LANES
assert ROW_CHUNKS == SUBLANES


def _store_tile_rows(ref, value, rows):
    for s in range(ROW_CHUNKS):
        ref[pl.ds(s, rows, stride=SUBLANES), :] = value[:, s * LANES:(s + 1) * LANES]


def _load_tile_rows(ref, rows):
    return jnp.concatenate([ref[pl.ds(s, rows, stride=SUBLANES), :] for s in range(ROW_CHUNKS)], axis=1)


def _bcast_col(x, j, width):
    return jnp.broadcast_to(x[:, j:j + 1], (x.shape[0], width))


IN_SEGS = (3 * DN_WIDTH, DN_WIDTH, NSA_WIDTH, 6 * NSA_HEAD_DIM, 2 * CONV_WIDTH, LANES)
IN_COLS = sum(IN_SEGS)


INV_BLOCK = 16


def _delta_prepare(q_ref, k_ref, v_ref, gb_ref, grow_ref, wy):
    u_ref, w_ref, attn_ref, qg_ref, kk_ref, gc_ref = wy
    ts = SEQ_TILE
    ch = DN_CHUNK
    hd = DN_HEAD_DIM
    r = lax.broadcasted_iota(jnp.int32, (ts, ts), 0)
    c = lax.broadcasted_iota(jnp.int32, (ts, ts), 1)
    same_chunk = (r // ch) == (c // ch)
    tril = same_chunk & (r >= c)
    strict = same_chunk & (r > c)
    same16 = (r // INV_BLOCK) == (c // INV_BLOCK)
    eye = jnp.where(r == c, 1.0, 0.0).astype(F32)
    m_col = jnp.where(tril, 1.0, 0.0).astype(BF16)
    m_row = jnp.where(same_chunk & (r <= c), 1.0, 0.0).astype(BF16)

    gb = gb_ref[0]
    gc_col = _dot01_left(m_col, gb)
    gc_row = _dot01_right(grow_ref[...], m_row)

    heads = range(DN_HEADS)
    hsl = [slice(h * hd, (h + 1) * hd) for h in heads]
    kh = [k_ref[0, :, hsl[h]] for h in heads]
    gcb = [_bcast_col(gc_col, h, ts) for h in heads]
    decay = [jnp.where(tril, jnp.exp(jnp.where(
        tril, gcb[h] - jnp.broadcast_to(gc_row[h:h + 1, :], (ts, ts)), 0.0)), 0.0) for h in heads]
    beta = [_bcast_col(gb, DN_HEADS + h, hd) for h in heads]
    kb = [kh[h] * beta[h] for h in heads]
    yield
    a_mat = [jnp.where(strict, _mm_nt(kb[h], kh[h]) * decay[h], 0.0) for h in heads]
    yield
    d_mat = [jnp.where(same16, a_mat[h], 0.0) for h in heads]
    e_mat = [a_mat[h] - d_mat[h] for h in heads]
    d2 = [_mm(d_mat[h], d_mat[h]) for h in heads]
    yield
    d4 = [_mm(d2[h], d2[h]) for h in heads]
    yield
    t1 = [_mm(eye - d_mat[h], eye + d2[h]) for h in heads]
    yield
    d8 = [_mm(d4[h], d4[h]) for h in heads]
    yield
    t2 = [_mm(t1[h], eye + d4[h]) for h in heads]
    yield
    p_mat = [_mm(t2[h], eye + d8[h]) for h in heads]
    yield
    m_mat = [_mm(p_mat[h], e_mat[h]) for h in heads]
    yield
    m2 = [_mm(m_mat[h], m_mat[h]) for h in heads]
    yield
    t3 = [_mm(eye - m_mat[h], eye + m2[h]) for h in heads]
    yield
    t_mat = [_mm(t3[h], p_mat[h]) for h in heads]
    yield
    for h in heads:
        gc128 = gcb[h][:, :hd]
        expg = jnp.exp(gc128)
        qh = q_ref[0, :, hsl[h]]
        sol = _mm(t_mat[h], jnp.concatenate([v_ref[0, :, hsl[h]] * beta[h], kb[h] * expg], axis=1))
        u_ref[h] = sol[:, :hd]
        w_ref[h] = sol[:, hd:].astype(BF16)
        qg_ref[h] = (qh * expg).astype(BF16)
        kk_ref[h] = kh[h]
        gc_ref[h] = gc128
    yield
    for h in heads:
        attn_ref[h] = jnp.where(tril, _mm_nt(q_ref[0, :, hsl[h]], kh[h]) * decay[h], 0.0).astype(BF16)


def _delta_recur(wy, z_ref, nw_ref, o_ref, s_ref):
    u_ref, w_ref, attn_ref, qg_ref, kk_ref, gc_ref = wy
    ch = DN_CHUNK
    hd = DN_HEAD_DIM
    heads = range(DN_HEADS)
    state = [s_ref[h] for h in heads]
    outs = [[] for _ in heads]
    for ci in range(SEQ_TILE // ch):
        rs = slice(ci * ch, (ci + 1) * ch)
        v_new = [u_ref[h, rs, :] - _mm(w_ref[h, rs, :], state[h]) for h in heads]
        o_state = [_mm(qg_ref[h, rs, :], state[h]) for h in heads]
        yield
        for h in heads:
            gc = gc_ref[h, rs, :]
            g_last = gc[ch - 1:ch, :]
            k_dec = kk_ref[h, rs, :] * jnp.exp(g_last - gc)
            outs[h].append(o_state[h] + _mm(attn_ref[h, rs, ci * ch:(ci + 1) * ch], v_new[h]))
            state[h] = state[h] * jnp.exp(g_last) + _mm_tn(k_dec, v_new[h])
        yield

    for h in heads:
        s_ref[h] = state[h]
        hs = slice(h * hd, (h + 1) * hd)
        o = jnp.concatenate(outs[h], axis=0)
        o = o * lax.rsqrt(jnp.mean(o * o, axis=-1, keepdims=True) + EPS) * nw_ref[...]
        o_ref[0, :, hs] = o * _silu(z_ref[0, :, hs])


def _interleave(weighted):
    live = [gen for gen, _ in weighted]
    while live:
        for gen, n in weighted:
            for _ in range(n):
                if gen in live and next(gen, live) is live:
                    live.remove(gen)


DN_BATCH = 2


def _delta_kernel(q_ref, k_ref, v_ref, gb_ref, grow0_ref, grow1_ref, z_ref, nw_ref, o_ref, s_ref,
                  *wy_refs):
    t = pl.program_id(1)
    set_a, set_b = wy_refs[:6], wy_refs[6:]
    grow_refs = (grow0_ref, grow1_ref)

    @pl.when(t == 0)
    def _():
        s_ref[...] = jnp.zeros(s_ref.shape, F32)
        for ref in set_b:
            ref[...] = jnp.zeros(ref.shape, ref.dtype)

    def step(read_set, write_set):
        one = lambda ref, bb: ref.at[pl.ds(bb, 1)]
        work = []
        for bb in range(DN_BATCH):
            work.append((_delta_prepare(one(q_ref, bb), one(k_ref, bb), one(v_ref, bb), one(gb_ref, bb),
                                        grow_refs[bb], [ref.at[bb] for ref in write_set]), 2))
        for bb in range(DN_BATCH):
            work.append((_delta_recur([ref.at[bb] for ref in read_set], one(z_ref, bb), nw_ref,
                                      one(o_ref, bb), s_ref.at[bb]), 1))
        _interleave(work)

    @pl.when(lax.rem(t, 2) == 0)
    def _():
        step(set_b, set_a)

    @pl.when(lax.rem(t, 2) == 1)
    def _():
        step(set_a, set_b)


def _delta_rule(q, k, v, gb, grow, z, norm_w):
    b = q.shape[0]
    nb = DN_BATCH
    assert nb == 2 and b % nb == 0
    ts = SEQ_TILE
    nt = SEQ // ts
    nxt = lambda w: pl.BlockSpec((nb, ts, w), lambda i, t: (i, jnp.minimum(t, nt - 1), 0))
    cur = lambda w: pl.BlockSpec((nb, ts, w), lambda i, t: (i, jnp.maximum(t - 1, 0), 0))
    grow_spec = lambda bb: pl.BlockSpec(
        (8, ts), lambda i, t: (0, (i * nb + bb) * nt + jnp.minimum(t, nt - 1)))
    hd = DN_HEAD_DIM
    wy_set = [pltpu.VMEM((nb, DN_HEADS, ts, hd), F32), pltpu.VMEM((nb, DN_HEADS, ts, hd), BF16),
              pltpu.VMEM((nb, DN_HEADS, ts, ts), BF16), pltpu.VMEM((nb, DN_HEADS, ts, hd), BF16),
              pltpu.VMEM((nb, DN_HEADS, ts, hd), F32), pltpu.VMEM((nb, DN_HEADS, ts, hd), F32)]
    return pl.pallas_call(
        _delta_kernel,
        out_shape=jax.ShapeDtypeStruct((b, SEQ, DN_WIDTH), F32),
        grid=(b // nb, nt + 1),
        in_specs=[
            nxt(DN_WIDTH), nxt(DN_WIDTH), nxt(DN_WIDTH), nxt(LANES),
            grow_spec(0), grow_spec(1),
            cur(DN_WIDTH),
            pl.BlockSpec((1, DN_HEAD_DIM), lambda i, t: (0, 0)),
        ],
        out_specs=cur(DN_WIDTH),
        scratch_shapes=[pltpu.VMEM((nb, DN_HEADS, hd, hd), F32)] + wy_set + wy_set,
        compiler_params=_params(("parallel", "arbitrary")),
        name="delta_rule",
    )(q, k, v, gb, grow, grow, z, norm_w)


CONV_HALO = 32


MIX_COLS = {'qkv': (0, 3 * DN_WIDTH), 'z': (3 * DN_WIDTH, 4 * DN_WIDTH)}
_off = 4 * DN_WIDTH
for _name, _w in (('nq', NSA_WIDTH), ('nkv', 6 * NSA_HEAD_DIM), ('cu', 2 * CONV_WIDTH), ('small', LANES)):
    MIX_COLS[_name] = (_off, _off + _w)
    _off += _w
assert _off == IN_COLS
DN_HALO = 8
MXU_CHUNK = 256


def _front_kernel(x_ref, nw_ref, w_ref, wabt_ref, cw_ref, pcol_ref, prow_ref,
                  dww_ref, dwb_ref, lnw_ref, lnb_ref,
                  q_ref, k_ref, v_ref, gb_ref, grow_ref, z_ref, nq_ref, nkv_ref, small_ref, yc_ref,
                  dn_buf, cf_buf, cf_shift):
    i = pl.program_id(0)
    tm = ROW_TILE
    hd = NSA_HEAD_DIM

    @pl.when(lax.rem(i, SEQ // tm) == 0)
    def _():
        dn_buf[0:DN_HALO, :] = jnp.zeros((DN_HALO, 3 * DN_WIDTH), F32)
        cf_buf[0:CONV_HALO, :] = jnp.zeros((CONV_HALO, CONV_WIDTH), F32)

    xf = x_ref[...]
    hb = (xf * lax.rsqrt(jnp.mean(xf * xf, axis=-1, keepdims=True) + EPS) * nw_ref[...]).astype(BF16)

    def proj(name, lo=0, hi=None):
        c0, c1 = MIX_COLS[name]
        hi = c1 - c0 if hi is None else hi
        return jnp.dot(hb, w_ref[:, c0 + lo:c0 + hi], preferred_element_type=F32)

    def mxu_items():
        cu = proj('cu')
        cf_buf[CONV_HALO:CONV_HALO + tm, :] = cu[:, :CONV_WIDTH] * _sigmoid(cu[:, CONV_WIDTH:])
        yield
        sm = proj('small')
        small_ref[...] = sm
        lane = lax.broadcasted_iota(jnp.int32, sm.shape, 1)
        g_col = -jnp.exp(pcol_ref[0:1, :]) * _softplus(sm + pcol_ref[1:2, :])
        gb_ref[...] = jnp.where(lane < DN_HEADS, g_col, _sigmoid(sm))
        a_t = lax.dot_general(wabt_ref[...], hb, (((1,), (1,)), ((), ())), preferred_element_type=F32)
        g_row = -jnp.exp(prow_ref[:, 0:1]) * _softplus(a_t + prow_ref[:, 1:2])
        grow_ref[...] = g_row[0:8, :]
        yield
        for c0 in range(0, 3 * DN_WIDTH, MXU_CHUNK):
            dn_buf[DN_HALO:DN_HALO + tm, c0:c0 + MXU_CHUNK] = proj('qkv', c0, c0 + MXU_CHUNK)
            yield
        for c0 in range(0, DN_WIDTH, MXU_CHUNK):
            z_ref[:, c0:c0 + MXU_CHUNK] = proj('z', c0, c0 + MXU_CHUNK)
            yield
        res = proj('nq')
        for j in range(NSA_HEADS):
            nq_ref[j] = res[:, j * hd:(j + 1) * hd]
        yield
        res = proj('nkv')
        for j in range(6):
            nkv_ref[j] = res[:, j * hd:(j + 1) * hd]

    def conformer_items():
        span = tm + CONV_HALO - SUBLANES
        for b in range(1, SUBLANES):
            cf_shift[b] = cf_buf[b:b + span, :]
        yield
        base = CONV_HALO - (CONV_KERNEL - 1)
        rows = 64
        for rc in range(tm // rows):
            parts = []
            for cc in range(CONV_WIDTH // LANES):
                cs = slice(cc * LANES, (cc + 1) * LANES)
                acc = None
                for j in range(CONV_KERNEL):
                    start = base + rc * rows + j
                    b = start % SUBLANES
                    a0 = start - b
                    win = cf_buf[a0:a0 + rows, cs] if b == 0 else cf_shift[b, a0:a0 + rows, cs]
                    term = dww_ref[j:j + 1, cs] * win
                    acc = term if acc is None else acc + term
                parts.append(acc)
            h = jnp.concatenate(parts, axis=1) + dwb_ref[...]
            mu = jnp.mean(h, axis=-1, keepdims=True)
            var = jnp.mean(jnp.square(h - mu), axis=-1, keepdims=True)
            hn = (h - mu) * lax.rsqrt(var + EPS) * lnw_ref[...] + lnb_ref[...]
            yc_ref[rc * rows:(rc + 1) * rows, :] = _silu(hn)
            yield
        cf_buf[0:CONV_HALO, :] = cf_buf[tm:tm + CONV_HALO, :]

    def deltanet_items():
        rows = tm // 2
        for c in range(3 * DN_HEADS):
            cs = slice(c * LANES, (c + 1) * LANES)
            for r0 in range(0, tm, rows):
                lo = DN_HALO - (DN_CONV - 1) + r0
                acc = cw_ref[0:1, cs] * dn_buf[lo:lo + rows, cs]
                for j in range(1, DN_CONV):
                    acc = acc + cw_ref[j:j + 1, cs] * dn_buf[lo + j:lo + j + rows, cs]
                y = _silu(acc)
                if c < 2 * DN_HEADS:
                    y = y * lax.rsqrt(jnp.sum(y * y, axis=-1, keepdims=True) + EPS)
                rs = slice(r0, r0 + rows)
                if c < DN_HEADS:
                    q_ref[rs, cs] = y * DN_HEAD_DIM ** -0.5
                elif c < 2 * DN_HEADS:
                    k_ref[rs, (c - DN_HEADS) * LANES:(c - DN_HEADS + 1) * LANES] = y
                else:
                    v_ref[rs, (c - 2 * DN_HEADS) * LANES:(c - 2 * DN_HEADS + 1) * LANES] = y
            dn_buf[0:DN_HALO, cs] = dn_buf[tm:tm + DN_HALO, cs]
            yield

    mxu, cf, dn = mxu_items(), conformer_items(), deltanet_items()
    next(mxu)
    next(mxu)
    next(cf)
    next(mxu)
    _interleave([(mxu, 1), (dn, 2), (cf, 2)])


def _front(x2, norm_w, w_cat, w_abt, conv_w, pcol, prow, dw_w, dw_b, ln_w, ln_b):
    n = x2.shape[0]
    tm = ROW_TILE
    hd = NSA_HEAD_DIM
    row = lambda w: pl.BlockSpec((tm, w), lambda i: (i, 0))
    full = lambda shape: pl.BlockSpec(shape, lambda i: (0,) * len(shape))
    pieces = lambda k: pl.BlockSpec((k, tm, hd), lambda i: (0, i, 0))
    return pl.pallas_call(
        _front_kernel,
        out_shape=[jax.ShapeDtypeStruct((n, DN_WIDTH), F32)] * 3
        + [jax.ShapeDtypeStruct((n, LANES), F32), jax.ShapeDtypeStruct((8, n), F32),
           jax.ShapeDtypeStruct((n, DN_WIDTH), F32),
           jax.ShapeDtypeStruct((NSA_HEADS, n, hd), F32), jax.ShapeDtypeStruct((6, n, hd), F32),
           jax.ShapeDtypeStruct((n, LANES), F32), jax.ShapeDtypeStruct((n, CONV_WIDTH), F32)],
        grid=(n // tm,),
        in_specs=[row(D_MODEL), full((1, D_MODEL)), full((D_MODEL, IN_COLS)), full((16, D_MODEL)),
                  full((DN_CONV, 3 * DN_WIDTH)), full((8, LANES)), full((16, LANES)),
                  full((CONV_KERNEL, CONV_WIDTH)), full((1, CONV_WIDTH)), full((1, CONV_WIDTH)),
                  full((1, CONV_WIDTH))],
        out_specs=[row(DN_WIDTH)] * 3
        + [row(LANES), pl.BlockSpec((8, tm), lambda i: (0, i)), row(DN_WIDTH),
           pieces(NSA_HEADS), pieces(6), row(LANES), row(CONV_WIDTH)],
        scratch_shapes=[pltpu.VMEM((tm + DN_HALO, 3 * DN_WIDTH), F32),
                        pltpu.VMEM((tm + CONV_HALO, CONV_WIDTH), F32),
                        pltpu.VMEM((SUBLANES, tm + CONV_HALO - SUBLANES, CONV_WIDTH), F32)],
        compiler_params=_params(("arbitrary",)),
        name="front",
    )(x2, norm_w, w_cat, w_abt, conv_w, pcol, prow, dw_w, dw_b, ln_w, ln_b)


def _rms_rows(x, w):
    return x * lax.rsqrt(jnp.mean(x * x, axis=-1, keepdims=True) + EPS) * w


def _nsa_prep_kernel(kc_ref, vc_ref, ks_ref, kw_ref, pos_ref, w1_ref, w2_ref, knw_ref,
                     kcmp_ref, vcmp_ref, ksn_ref, kwn_ref):
    hd = NSA_HEAD_DIM

    def compress(x_ref, i):
        u_lo = jnp.zeros((N_CMP_PAD, CMP_HIDDEN), F32)
        u_hi = jnp.zeros((N_CMP_PAD, CMP_HIDDEN), F32)
        for r in range(CMP_STRIDE):
            xr = x_ref[pl.ds(r, N_CMP_PAD, stride=CMP_STRIDE), :]
            lo, hi = r, CMP_STRIDE + r
            u_lo = u_lo + _mm(xr + pos_ref[i, lo:lo + 1, :], w1_ref[i, lo * hd:(lo + 1) * hd, :])
            u_hi = u_hi + _mm(xr + pos_ref[i, hi:hi + 1, :], w1_ref[i, hi * hd:(hi + 1) * hd, :])
        hid = _silu(u_lo + pltpu.roll(u_hi, N_CMP_PAD - 1, axis=0))
        return _mm(hid, w2_ref[i])

    kcmp_ref[0] = _rms_rows(compress(kc_ref, 0), knw_ref[0:1, :])
    vcmp_ref[0] = compress(vc_ref, 1)
    ksn_ref[0] = _rms_rows(ks_ref[...], knw_ref[1:2, :])
    kwn_ref[0] = _rms_rows(kw_ref[...], knw_ref[2:3, :])


def _nsa_prep(kv6, pos, w1, w2, knw):
    b = kv6.shape[1]
    hd = NSA_HEAD_DIM
    full = lambda shape: pl.BlockSpec(shape, lambda i: (0,) * len(shape))
    piece = lambda j: pl.BlockSpec((None, None, SEQ, hd), lambda i: (j, i, 0, 0))
    bspec = lambda r, w: pl.BlockSpec((1, r, w), lambda i: (i, 0, 0))
    return pl.pallas_call(
        _nsa_prep_kernel,
        out_shape=[jax.ShapeDtypeStruct((b, N_CMP_PAD, hd), F32)] * 2
        + [jax.ShapeDtypeStruct((b, SEQ, hd), F32)] * 2,
        grid=(b,),
        in_specs=[
            piece(0), piece(1), piece(2), piece(4),
            full((2, CMP_LEN, hd)), full((2, CMP_LEN * hd, CMP_HIDDEN)), full((2, CMP_HIDDEN, hd)),
            full((3, hd)),
        ],
        out_specs=[bspec(N_CMP_PAD, hd)] * 2 + [bspec(SEQ, hd)] * 2,
        compiler_params=_params(("parallel",)),
        name="nsa_prep",
    )(kv6, kv6, kv6, kv6, pos, w1, w2, knw)


def _nsa_attn_kernel(q_ref, qnw_ref, kcmp_ref, vcmp_ref, ks_ref, vs_ref, kw_ref, vw_ref,
                     gate_ref, bcmp_ref, btab_ref, o_ref,
                     madd_ref, s_ref, ksb_ref, kwb_ref, vsa_ref, vwa_ref):
    i = pl.program_id(1)
    tq = Q_TILE
    nh = NSA_HEADS
    hd = NSA_HEAD_DIM
    t0 = i * tq

    @pl.when(i == 0)
    def _():
        ones = jnp.ones((SEQ, hd), BF16)
        ksb_ref[...] = ks_ref[0].astype(BF16)
        kwb_ref[...] = kw_ref[0].astype(BF16)
        vsa_ref[...] = jnp.concatenate([vs_ref[...].astype(BF16), ones], axis=1)
        vwa_ref[...] = jnp.concatenate([vw_ref[...].astype(BF16), ones], axis=1)

    qs = jnp.concatenate(
        [_rms_rows(q_ref[h], qnw_ref[...]) * hd ** -0.5 for h in range(nh)], axis=0).astype(BF16)

    row = lax.broadcasted_iota(jnp.int32, (tq, LANES), 0)
    lane = lax.broadcasted_iota(jnp.int32, (tq, LANES), 1)
    qpos = t0 + row

    branch_out = {}

    def select_items():
        s_all = _mm_nt(qs, kcmp_ref[0])
        cmp_valid = (qpos >= lane * CMP_STRIDE + (CMP_LEN - 1)) & (lane < N_CMP_PAD - 1)
        p_rows = []
        p_sum = jnp.zeros((tq, LANES), F32)
        for h in range(nh):
            s = jnp.where(cmp_valid, s_all[h * tq:(h + 1) * tq] + bcmp_ref[h], NEG_INF)
            m = jnp.max(s, axis=-1, keepdims=True)
            p = jnp.where(cmp_valid, jnp.exp(s - m), 0.0)
            l = jnp.sum(p, axis=-1, keepdims=True)
            p = p * jnp.where(l > 0.0, 1.0 / l, 0.0)
            p_rows.append(p)
            p_sum = p_sum + p
            yield
        branch_out['cmp'] = _mm(jnp.concatenate(p_rows, axis=0), vcmp_ref[0])

        ss = lax.broadcasted_iota(jnp.int32, (N_SLC, N_CMP_PAD), 0)
        jj = lax.broadcasted_iota(jnp.int32, (N_SLC, N_CMP_PAD), 1)
        overlap_t = ((jj * CMP_STRIDE < ss * SLC_BLOCK + SLC_BLOCK)
                     & (jj * CMP_STRIDE + CMP_LEN > ss * SLC_BLOCK) & (jj < N_CMP_PAD - 1))
        overlap_t = jnp.where(overlap_t, 1.0, 0.0).astype(BF16)
        nt = lambda a, b_: lax.dot_general(a, b_, (((1,), (1,)), ((), ())), preferred_element_type=F32)
        p_hi, p_mid, p_lo = _split3(p_sum)
        imp = nt(overlap_t, p_hi) + nt(overlap_t, p_mid) + nt(overlap_t, p_lo)
        blk = lax.broadcasted_iota(jnp.int32, (N_SLC, tq), 0)
        cur = (t0 + lax.broadcasted_iota(jnp.int32, (N_SLC, tq), 1)) // SLC_BLOCK
        causal_blk = blk <= cur
        forced = (blk == 0) | (blk == cur) | (blk == cur - 1)
        imp = jnp.where(causal_blk & forced, FORCE, jnp.where(causal_blk, imp, -1.0))
        yield
        rank = jnp.zeros((N_SLC, tq), jnp.int32)
        for s2 in range(N_SLC):
            other = jnp.broadcast_to(imp[s2:s2 + 1, :], (N_SLC, tq))
            beats = (other > imp) | ((other == imp) & (blk > s2))
            rank = rank + jnp.where(beats, 1, 0)
            if s2 % 8 == 7:
                yield
        sel_t = jnp.where(rank < SLC_TOP_N, 1.0, 0.0).astype(BF16)
        er = lax.broadcasted_iota(jnp.int32, (N_SLC, SEQ), 0)
        ec = lax.broadcasted_iota(jnp.int32, (N_SLC, SEQ), 1)
        expand = jnp.where(ec // SLC_BLOCK == er, 1.0, 0.0).astype(BF16)
        key = lax.broadcasted_iota(jnp.int32, (tq, SEQ), 1)
        qall = t0 + lax.broadcasted_iota(jnp.int32, (tq, SEQ), 0)
        chosen = (_mm_tn(sel_t, expand) > 0.5) & (qall >= key)
        madd_ref[...] = jnp.where(chosen, 0.0, NEG_INF)

    neg = jnp.full((nh * tq, tq), NEG_INF, F32)
    zero = jnp.zeros((nh * tq, LANES), F32)

    def scores(k_tile, bias_idx, add):
        s_t = _mm_nt(qs, k_tile)
        return jnp.concatenate(
            [s_t[h * tq:(h + 1) * tq] + (btab_ref[h, bias_idx] + add) for h in range(nh)], axis=0)

    def normalise(acc):
        return acc[:, :hd] * (1.0 / _bcast_col(acc, hd, hd))

    def window_items():
        kcol = lax.broadcasted_iota(jnp.int32, (tq, tq), 1)
        qrow = t0 + lax.broadcasted_iota(jnp.int32, (tq, tq), 0)
        n_win = WINDOW // tq + 1
        starts = []
        m_run = neg
        for j in range(n_win):
            d = n_win - 1 - j
            k0 = pl.multiple_of(jnp.maximum(i - d, 0) * tq, tq)
            starts.append(k0)
            dist = qrow - (k0 + kcol)
            ok = (dist >= 0) & (dist < WINDOW) & (i >= d)
            s_m = scores(kwb_ref[pl.ds(k0, tq), :], min(d, 2), jnp.where(ok, 0.0, NEG_INF))
            s_ref[:, j * tq:(j + 1) * tq] = s_m
            m_run = jnp.maximum(m_run, s_m)
            yield
        m_win = jnp.broadcast_to(jnp.max(m_run, axis=-1, keepdims=True), (nh * tq, tq))
        acc = zero
        for j in range(n_win):
            p = jnp.exp(s_ref[:, j * tq:(j + 1) * tq] - m_win).astype(BF16)
            acc = acc + jnp.dot(p, vwa_ref[pl.ds(starts[j], tq), :], preferred_element_type=F32)
            yield
        branch_out['win'] = normalise(acc)

    _interleave([(select_items(), 1), (window_items(), 1)])
    o_cmp, o_win = branch_out['cmp'], branch_out['win']

    n_groups = (i + SLC_GROUP) // SLC_GROUP

    def slc_pass1(j, m_run):
        for u in range(SLC_GROUP):
            kt = SLC_GROUP * j + u
            k0 = pl.multiple_of(kt * tq, tq)
            s_m = scores(ksb_ref[pl.ds(k0, tq), :], jnp.clip(i - kt, 0, 2), madd_ref[:, pl.ds(k0, tq)])
            s_ref[:, pl.ds(k0, tq)] = s_m
            m_run = jnp.maximum(m_run, s_m)
        return m_run

    m_slc = jnp.broadcast_to(
        jnp.max(lax.fori_loop(0, n_groups, slc_pass1, neg), axis=-1, keepdims=True), (nh * tq, tq))

    def slc_pass2(j, acc):
        for u in range(SLC_GROUP):
            k0 = pl.multiple_of((SLC_GROUP * j + u) * tq, tq)
            p = jnp.exp(s_ref[:, pl.ds(k0, tq)] - m_slc).astype(BF16)
            acc = acc + jnp.dot(p, vsa_ref[pl.ds(k0, tq), :], preferred_element_type=F32)
        return acc

    o_slc = normalise(lax.fori_loop(0, n_groups, slc_pass2, zero))

    gates = _sigmoid(gate_ref[0])
    outs = []
    for h in range(nh):
        hs = slice(h * tq, (h + 1) * tq)
        g0 = 2 * DN_HEADS + 3 * h
        outs.append(_bcast_col(gates, g0, hd) * o_cmp[hs]
                    + _bcast_col(gates, g0 + 1, hd) * o_slc[hs]
                    + _bcast_col(gates, g0 + 2, hd) * o_win[hs])
    o_ref[...] = jnp.concatenate(outs, axis=1)


def _nsa_attn(q, qnw, kcmp, vcmp, ks, kv6, kw, small, bias_cmp, btab):
    b = kv6.shape[1]
    tq = Q_TILE
    nq = SEQ // tq
    nh, hd = NSA_HEADS, NSA_HEAD_DIM
    full = lambda shape: pl.BlockSpec(shape, lambda bi, i: (0,) * len(shape))
    per_b = lambda r: pl.BlockSpec((1, r, hd), lambda bi, i: (bi, 0, 0))
    piece = lambda j: pl.BlockSpec((None, None, SEQ, hd), lambda bi, i: (j, bi, 0, 0))
    heads = pl.BlockSpec((nh, tq, hd), lambda bi, i: (0, bi * nq + i, 0))
    return pl.pallas_call(
        _nsa_attn_kernel,
        out_shape=jax.ShapeDtypeStruct((b * SEQ, NSA_WIDTH), F32),
        grid=(b, nq),
        in_specs=[
            heads,
            full((1, hd)),
            per_b(N_CMP_PAD), per_b(N_CMP_PAD), per_b(SEQ), piece(3), per_b(SEQ), piece(5),
            pl.BlockSpec((1, tq, LANES), lambda bi, i: (bi, i, 0)),
            pl.BlockSpec((nh, tq, N_CMP_PAD), lambda bi, i: (0, i, 0)),
            full((nh, 3, tq, tq)),
        ],
        out_specs=pl.BlockSpec((tq, NSA_WIDTH), lambda bi, i: (bi * nq + i, 0)),
        scratch_shapes=[pltpu.VMEM((tq, SEQ), F32), pltpu.VMEM((nh * tq, SEQ), F32),
                        pltpu.VMEM((SEQ, hd), BF16), pltpu.VMEM((SEQ, hd), BF16),
                        pltpu.VMEM((SEQ, 2 * hd), BF16), pltpu.VMEM((SEQ, 2 * hd), BF16)],
        compiler_params=_params(("parallel", "arbitrary")),
        name="nsa_attn",
    )(q, qnw, kcmp, vcmp, ks, kv6, kw, kv6, small, bias_cmp, btab)


def _out_router_kernel(ya_ref, yb_ref, yc_ref, x_ref, wo_ref, fnw_ref, rw_ref, rb_ref,
                       xo_ref, h_ref, idx_ref, wt_ref, hist_ref):
    wa = DN_WIDTH
    wb = wa + NSA_WIDTH
    y = (jnp.dot(ya_ref[...].astype(BF16), wo_ref[0:wa, :], preferred_element_type=F32)
         + jnp.dot(yb_ref[...].astype(BF16), wo_ref[wa:wb, :], preferred_element_type=F32)
         + jnp.dot(yc_ref[...].astype(BF16), wo_ref[wb:, :], preferred_element_type=F32))
    xn = x_ref[...] + y
    xo_ref[...] = xn
    h = xn * lax.rsqrt(jnp.mean(xn * xn, axis=-1, keepdims=True) + EPS) * fnw_ref[...]
    _store_tile_rows(h_ref, h, h.shape[0])
    h_hi, h_lo = _hi_lo(h)
    r_hi = jnp.dot(h_hi, rw_ref[...], preferred_element_type=F32)
    r_lo = jnp.dot(h_lo, rw_ref[...], preferred_element_type=F32)
    logits = ((r_hi[:, :LANES] + r_hi[:, LANES:]) + (r_lo[:, :LANES] + r_lo[:, LANES:])) + rb_ref[...]
    lane = lax.broadcasted_iota(jnp.int32, logits.shape, 1)
    vals, idxs = [], []
    for _ in range(TOP_K):
        m = jnp.max(logits, axis=-1, keepdims=True)
        ix = jnp.min(jnp.where(logits == m, lane, LANES), axis=-1, keepdims=True)
        vals.append(m)
        idxs.append(ix)
        logits = jnp.where(lane == ix, -jnp.inf, logits)
    es = [jnp.exp(v - vals[0]) for v in vals]
    inv = 1.0 / (es[0] + es[1] + es[2] + es[3])
    idx_out = jnp.zeros(lane.shape, jnp.int32)
    wt_out = jnp.zeros(lane.shape, F32)
    chosen = jnp.zeros(lane.shape, F32)
    for k in range(TOP_K):
        idx_out = jnp.where(lane == k, idxs[k], idx_out)
        wt_out = jnp.where(lane == k, es[k] * inv, wt_out)
        chosen = chosen + jnp.where(lane == idxs[k], 1.0, 0.0)
    idx_ref[...] = idx_out
    wt_ref[...] = wt_out
    hist_ref[0] = jnp.broadcast_to(jnp.sum(chosen, axis=0, keepdims=True), (8, LANES))


def _out_router(ya, yb, yc, x2, w_out, fnw, rw, rb):
    n = x2.shape[0]
    tm = ROW_TILE
    row = lambda w: pl.BlockSpec((tm, w), lambda i: (i, 0))
    full = lambda shape: pl.BlockSpec(shape, lambda i: (0,) * len(shape))
    return pl.pallas_call(
        _out_router_kernel,
        out_shape=[jax.ShapeDtypeStruct((n, D_MODEL), F32), jax.ShapeDtypeStruct((n * SUBLANES, LANES), F32),
                   jax.ShapeDtypeStruct((n, LANES), jnp.int32), jax.ShapeDtypeStruct((n, LANES), F32),
                   jax.ShapeDtypeStruct((n // tm, 8, LANES), F32)],
        grid=(n // tm,),
        in_specs=[row(DN_WIDTH), row(NSA_WIDTH), row(CONV_WIDTH), row(D_MODEL),
                  full((D_MODEL, D_MODEL)), full((1, D_MODEL)), full((D_MODEL, 2 * LANES)), full((1, LANES))],
        out_specs=[row(D_MODEL), pl.BlockSpec((tm * SUBLANES, LANES), lambda i: (i, 0)), row(LANES), row(LANES),
                   pl.BlockSpec((1, 8, LANES), lambda i: (i, 0, 0))],
        compiler_params=_params(("parallel",)),
        name="out_router",
    )(ya, yb, yc, x2, w_out, fnw, rw, rb)


def _slots_kernel(idx_ref, base_ref, slot_ref):
    tm = ROW_TILE
    idx = idx_ref[...]
    lane = lax.broadcasted_iota(jnp.int32, (tm, LANES), 1)
    onehots = [jnp.where(lane == _bcast_col(idx, k, LANES), 1.0, 0.0) for k in range(TOP_K)]
    cnt = (onehots[0] + onehots[1]) + (onehots[2] + onehots[3])
    r = lax.broadcasted_iota(jnp.int32, (tm, tm), 0)
    c = lax.broadcasted_iota(jnp.int32, (tm, tm), 1)
    earlier = jnp.where(r > c, 1.0, 0.0).astype(BF16)
    rank = jnp.dot(earlier, cnt.astype(BF16), preferred_element_type=F32) + base_ref[0, 0:1, :]
    out = jnp.zeros((tm, LANES), F32)
    for k in range(TOP_K):
        out = jnp.where(lane == k, jnp.sum(rank * onehots[k], axis=-1, keepdims=True), out)
    slot_ref[...] = (out * SUBLANES).astype(jnp.int32)


def _slots(idx, base3):
    n = idx.shape[0]
    tm = ROW_TILE
    return pl.pallas_call(
        _slots_kernel,
        out_shape=jax.ShapeDtypeStruct((n, LANES), jnp.int32),
        grid=(n // tm,),
        in_specs=[pl.BlockSpec((tm, LANES), lambda i: (i, 0)),
                  pl.BlockSpec((1, 8, LANES), lambda i: (i, 0, 0))],
        out_specs=pl.BlockSpec((tm, LANES), lambda i: (i, 0)),
        compiler_params=_params(("parallel",)),
        name="moe_slots",
    )(idx, base3)


def _dispatch_kernel(slot_ref, zoff_ref, nv_ref, h_ref, xs_hbm, zbuf, sem, zsem):
    i = pl.program_id(0)
    tm = DISPATCH_TILE
    base = i * tm

    @pl.when(i == 0)
    def _():
        zbuf[...] = jnp.zeros(zbuf.shape, F32)
        blk = MOE_TILE * SUBLANES
        n_blocks = xs_hbm.shape[0] // blk

        def fill(row0):
            return pltpu.make_async_copy(
                zbuf, xs_hbm.at[pl.ds(pl.multiple_of(row0 * SUBLANES, blk), blk)], zsem)

        def fill_region_end(e, c):
            fill(zoff_ref[e]).start()
            return c

        def fill_tail(b, c):
            fill(b * MOE_TILE).start()
            return c

        def wait_fill(b, c):
            fill(0).wait()
            return c

        lax.fori_loop(0, N_EXPERTS, fill_region_end, 0)
        lax.fori_loop(nv_ref[0], n_blocks, fill_tail, 0)
        lax.fori_loop(0, N_EXPERTS + n_blocks - nv_ref[0], wait_fill, 0)

    def body(r, c):
        src = h_ref.at[pl.ds(pl.multiple_of(r * SUBLANES, SUBLANES), SUBLANES)]
        for k in range(TOP_K):
            row0 = pl.multiple_of(slot_ref[(base + r) * TOP_K + k], SUBLANES)
            pltpu.make_async_copy(src, xs_hbm.at[pl.ds(row0, SUBLANES)], sem).start(priority=k % 2)
        return c

    lax.fori_loop(0, tm, body, 0, unroll=8)
    done = xs_hbm.at[pl.ds(0, TOP_K * tm * SUBLANES)]
    pltpu.make_async_copy(done, done, sem).wait()


def _dispatch(slot_flat, zero_off, n_valid, h, n_slots):
    n = h.shape[0] // SUBLANES
    tm = DISPATCH_TILE
    return pl.pallas_call(
        _dispatch_kernel,
        out_shape=jax.ShapeDtypeStruct((n_slots * SUBLANES, LANES), F32),
        grid_spec=pltpu.PrefetchScalarGridSpec(
            num_scalar_prefetch=3,
            grid=(n // tm,),
            in_specs=[pl.BlockSpec((tm * SUBLANES, LANES), lambda i, s, z, nv: (i, 0))],
            out_specs=pl.BlockSpec(memory_space=pl.ANY),
            scratch_shapes=[pltpu.VMEM((MOE_TILE * SUBLANES, LANES), F32),
                            pltpu.SemaphoreType.DMA, pltpu.SemaphoreType.DMA],
        ),
        compiler_params=_params(("arbitrary",)),
        name="moe_dispatch",
    )(slot_flat, zero_off, n_valid, h)


def _expert_kernel(be_ref, nv_ref, xs_ref, wgu_ref, bgu_ref, wd_ref, bd_ref, o_ref, wgu_bf, wd_bf):
    i = pl.program_id(0)
    new_expert = (i == 0) | (be_ref[i] != be_ref[jnp.maximum(i - 1, 0)])

    @pl.when((i < nv_ref[0]) & new_expert)
    def _():
        step = 4 * LANES
        for c0 in range(0, 2 * D_FF, step):
            wgu_bf[:, c0:c0 + step] = wgu_ref[:, c0:c0 + step].astype(BF16)
        for c0 in range(0, D_MODEL, step):
            wd_bf[:, c0:c0 + step] = wd_ref[:, c0:c0 + step].astype(BF16)

    @pl.when(i < nv_ref[0])
    def _():
        xb = _load_tile_rows(xs_ref, MOE_TILE).astype(BF16)
        gu = jnp.dot(xb, wgu_bf[...], preferred_element_type=F32) + bgu_ref[...]
        gate = jnp.minimum(gu[:, :D_FF], SWIGLU_LIMIT)
        up = jnp.clip(gu[:, D_FF:], -SWIGLU_LIMIT, SWIGLU_LIMIT)
        act = (up + 1.0) * gate * _sigmoid(SWIGLU_ALPHA * gate)
        y = jnp.dot(act.astype(BF16), wd_bf[...], preferred_element_type=F32) + bd_ref[...]
        _store_tile_rows(o_ref, y, MOE_TILE)

    @pl.when(i >= nv_ref[0])
    def _():
        o_ref[...] = jnp.zeros(o_ref.shape, F32)


def _expert_ffn(layer, block_expert, n_valid, xs, wgu, bgu, wd, bd):
    n_slots = xs.shape[0] // SUBLANES
    tm = MOE_TILE
    return pl.pallas_call(
        _expert_kernel,
        out_shape=jax.ShapeDtypeStruct(xs.shape, F32),
        grid_spec=pltpu.PrefetchScalarGridSpec(
            num_scalar_prefetch=2,
            grid=(n_slots // tm,),
            in_specs=[
                pl.BlockSpec((tm * SUBLANES, LANES), lambda i, be, nv: (i, 0)),
                pl.BlockSpec((None, None, D_MODEL, 2 * D_FF), lambda i, be, nv: (layer, be[i], 0, 0)),
                pl.BlockSpec((None, None, 1, 2 * D_FF), lambda i, be, nv: (layer, be[i], 0, 0)),
                pl.BlockSpec((None, None, D_FF, D_MODEL), lambda i, be, nv: (layer, be[i], 0, 0)),
                pl.BlockSpec((None, None, 1, D_MODEL), lambda i, be, nv: (layer, be[i], 0, 0)),
            ],
            out_specs=pl.BlockSpec((tm * SUBLANES, LANES), lambda i, be, nv: (i, 0)),
            scratch_shapes=[pltpu.VMEM((D_MODEL, 2 * D_FF), BF16), pltpu.VMEM((D_FF, D_MODEL), BF16)],
        ),
        compiler_params=pltpu.CompilerParams(dimension_semantics=("arbitrary",),
                                             vmem_limit_bytes=EXPERT_VMEM_LIMIT_BYTES),
        name="expert_ffn",
    )(block_expert, n_valid, xs, wgu, bgu, wd, bd)


def _combine_kernel(slot_ref, ys_hbm, x_ref, wt_ref, o_ref, buf, acc_ref, sems):
    i = pl.program_id(0)
    tc = COMBINE_TILE
    nsteps = pl.num_programs(0)
    cur = lax.rem(i, 2)

    def issue(step, par):
        base = step * tc

        def body(r, c):
            dst0 = pl.multiple_of(r * SUBLANES, SUBLANES)
            for k in range(TOP_K):
                row0 = pl.multiple_of(slot_ref[(base + r) * TOP_K + k], SUBLANES)
                pltpu.make_async_copy(ys_hbm.at[pl.ds(row0, SUBLANES)],
                                      buf.at[par, k, pl.ds(dst0, SUBLANES)],
                                      sems.at[par]).start(priority=k % 2)
            return c

        lax.fori_loop(0, tc, body, 0, unroll=8)

    @pl.when(i == 0)
    def _():
        issue(0, 0)

    @pl.when(i + 1 < nsteps)
    def _():
        issue(i + 1, 1 - cur)

    pltpu.make_async_copy(buf.at[cur], buf.at[cur], sems.at[cur]).wait()
    wt = wt_ref[...]
    acc = None
    for k in range(TOP_K):
        w_tiles = jnp.broadcast_to(_bcast_col(wt, k, LANES)[:, None, :], (tc, SUBLANES, LANES))
        term = w_tiles.reshape(tc * SUBLANES, LANES) * buf[cur, k]
        acc = term if acc is None else acc + term
    acc_ref[...] = acc
    for s in range(ROW_CHUNKS):
        cs = slice(s * LANES, (s + 1) * LANES)
        o_ref[:, cs] = x_ref[:, cs] + acc_ref[pl.ds(s, tc, stride=SUBLANES), :]


def _combine(slot_flat, ys, x2, wt):
    n = x2.shape[0]
    tc = COMBINE_TILE
    return pl.pallas_call(
        _combine_kernel,
        out_shape=jax.ShapeDtypeStruct((n, D_MODEL), F32),
        grid_spec=pltpu.PrefetchScalarGridSpec(
            num_scalar_prefetch=1,
            grid=(n // tc,),
            in_specs=[pl.BlockSpec(memory_space=pl.ANY),
                      pl.BlockSpec((tc, D_MODEL), lambda i, s: (i, 0)),
                      pl.BlockSpec((tc, LANES), lambda i, s: (i, 0))],
            out_specs=pl.BlockSpec((tc, D_MODEL), lambda i, s: (i, 0)),
            scratch_shapes=[pltpu.VMEM((2, TOP_K, tc * SUBLANES, LANES), F32),
                            pltpu.VMEM((tc * SUBLANES, LANES), F32), pltpu.SemaphoreType.DMA((2,))],
        ),
        compiler_params=_params(("arbitrary",)),
        name="moe_combine",
    )(slot_flat, ys, x2, wt)


def _moe(layer, h, x2, idx, wt, hist3, wgu, bgu, wd, bd):
    n = x2.shape[0]
    tm = MOE_TILE
    n_slots = n * TOP_K + N_EXPERTS * tm
    n_blocks = n_slots // tm
    hist = hist3[:, 0, :N_EXPERTS]
    counts = jnp.sum(hist, axis=0).astype(jnp.int32)
    padded = (counts + tm - 1) // tm * tm
    pad_end = jnp.cumsum(padded)
    pad_start = pad_end - padded
    tile_base = pad_start[None, :].astype(F32) + (jnp.cumsum(hist, axis=0) - hist)
    base3 = jnp.broadcast_to(jnp.pad(tile_base, ((0, 0), (0, LANES - N_EXPERTS)))[:, None, :],
                             (hist.shape[0], 8, LANES))
    blk0 = jnp.arange(n_blocks) * tm
    block_expert = jnp.minimum(jnp.sum(blk0[:, None] >= pad_end[None, :], axis=1), N_EXPERTS - 1).astype(jnp.int32)
    n_valid = (pad_end[-1:] // tm).astype(jnp.int32)
    zero_off = jnp.maximum(pad_end - tm, 0).astype(jnp.int32)

    slot_flat = _slots(idx, base3)[:, :TOP_K].reshape(-1)
    xs = _dispatch(slot_flat, zero_off, n_valid, h, n_slots)
    ys = _expert_ffn(layer, block_expert, n_valid, xs, wgu, bgu, wd, bd)
    return _combine(slot_flat, ys, x2, wt)


def _t5_bucket(dist):
    n = jnp.maximum(dist, 0)
    max_exact = REL_BUCKETS // 2
    nf = jnp.maximum(n, 1).astype(F32)
    large = max_exact + (jnp.log(nf / max_exact) / math.log(REL_MAX_DIST / max_exact)
                         * (REL_BUCKETS - max_exact)).astype(jnp.int32)
    large = jnp.minimum(large, REL_BUCKETS - 1)
    return jnp.where(n < max_exact, n, large)


def _bias_tables(rel_bias):
    rel_bias = rel_bias.astype(F32)
    tq = Q_TILE

    def lookup(dist):
        bucket = _t5_bucket(dist)
        out = jnp.zeros((NSA_HEADS,) + dist.shape, F32)
        for bk in range(REL_BUCKETS):
            out = jnp.where(bucket[None] == bk, rel_bias[bk].reshape((NSA_HEADS,) + (1,) * dist.ndim), out)
        return out

    t_pos = jnp.arange(SEQ)
    cmp_end = jnp.arange(N_CMP_PAD) * CMP_STRIDE + CMP_LEN - 1
    bias_cmp = lookup(t_pos[:, None] - cmp_end[None, :])
    rr = jnp.arange(tq)[:, None] - jnp.arange(tq)[None, :]
    btab = lookup(jnp.stack([rr, rr + tq, rr + 2 * tq]))
    return bias_cmp, btab


def _layer(layer, x2, b, p, experts, bias_cmp, btab):
    n = x2.shape[0]
    w = p['w_in']
    o_a = 4 * DN_WIDTH
    o_q = o_a + 2 * DN_HEADS
    o_kv = o_q + NSA_WIDTH
    o_g = o_kv + 6 * NSA_HEAD_DIM
    o_u = o_g + 3 * NSA_HEADS
    w_small = jnp.concatenate([w[:, o_a:o_q], w[:, o_g:o_u],
                               jnp.zeros((D_MODEL, LANES - 2 * DN_HEADS - 3 * NSA_HEADS), F32)], axis=1)
    w_cat = jnp.concatenate([w[:, :o_a], w[:, o_q:o_kv], w[:, o_kv:o_g], w[:, o_u:], w_small],
                            axis=1).astype(BF16)
    w_abt = jnp.concatenate([w[:, o_a:o_q].T, jnp.zeros((8, D_MODEL), F32)], axis=0).astype(BF16)
    pcol = jnp.zeros((8, LANES), F32).at[0, :DN_HEADS].set(p['dn_a_log']).at[1, :DN_HEADS].set(p['dn_dt_bias'])
    prow = jnp.zeros((16, LANES), F32).at[:DN_HEADS, 0].set(p['dn_a_log']).at[:DN_HEADS, 1].set(p['dn_dt_bias'])
    dq, dk, dv, gb, grow, z, nq, nkv, small, y_c = _front(
        x2, p['attn_norm_w'][None, :], w_cat, w_abt, p['dn_conv_w'], pcol, prow,
        p['conv_dw_w'], p['conv_dw_b'][None, :], p['conv_ln_w'][None, :], p['conv_ln_b'][None, :])
    small3 = small.reshape(b, SEQ, LANES)

    seq = lambda a: a.reshape(b, SEQ, a.shape[-1])
    y_a = _delta_rule(seq(dq), seq(dk), seq(dv), seq(gb), grow, seq(z), p['dn_norm_w'][None, :])

    kv6 = nkv.reshape(6, b, SEQ, NSA_HEAD_DIM)
    kcmp, vcmp, ksn, kwn = _nsa_prep(kv6, p['nsa_cmp_pos'], p['nsa_cmp_w1'].astype(BF16),
                                     p['nsa_cmp_w2'].astype(BF16), p['nsa_k_norm_w'])
    y_b = _nsa_attn(nq, p['nsa_q_norm_w'][None, :], kcmp, vcmp, ksn, kv6, kwn, small3, bias_cmp, btab)

    rw = jnp.concatenate(_hi_lo(jnp.pad(p['router_w'], ((0, 0), (0, LANES - N_EXPERTS)))), axis=1)
    rb = jnp.concatenate([p['router_b'], jnp.full((LANES - N_EXPERTS,), NEG_INF, F32)])[None, :]
    x_new, h, idx, wt, hist3 = _out_router(y_a.reshape(n, DN_WIDTH), y_b, y_c.reshape(n, CONV_WIDTH), x2,
                                           p['w_out'].astype(BF16), p['ffn_norm_w'][None, :], rw, rb)
    return _moe(layer, h, x_new, idx, wt, hist3, *experts)


def kernel(x, attn_norm_w, w_in, dn_conv_w, dn_a_log, dn_dt_bias, dn_norm_w, nsa_q_norm_w, nsa_k_norm_w, nsa_cmp_pos, nsa_cmp_w1, nsa_cmp_w2, conv_dw_w, conv_dw_b, conv_ln_w, conv_ln_b, w_out, ffn_norm_w, router_w, router_b, w_gate_up, b_gate_up, w_down, b_down, rel_bias):
    b, t, d = x.shape
    assert (t, d) == (SEQ, D_MODEL)
    stacked = dict(attn_norm_w=attn_norm_w, w_in=w_in, dn_conv_w=dn_conv_w, dn_a_log=dn_a_log,
                   dn_dt_bias=dn_dt_bias, dn_norm_w=dn_norm_w, nsa_q_norm_w=nsa_q_norm_w,
                   nsa_k_norm_w=nsa_k_norm_w, nsa_cmp_pos=nsa_cmp_pos, nsa_cmp_w1=nsa_cmp_w1,
                   nsa_cmp_w2=nsa_cmp_w2, conv_dw_w=conv_dw_w, conv_dw_b=conv_dw_b,
                   conv_ln_w=conv_ln_w, conv_ln_b=conv_ln_b, w_out=w_out, ffn_norm_w=ffn_norm_w,
                   router_w=router_w, router_b=router_b)
    experts = (w_gate_up, b_gate_up[:, :, None, :], w_down, b_down[:, :, None, :])
    bias_cmp, btab = _bias_tables(rel_bias)
    x2 = x.reshape(b * t, d)
    for l in range(w_in.shape[0]):
        x2 = _layer(l, x2, b, {k: v[l] for k, v in stacked.items()}, experts, bias_cmp, btab)
    return x2.reshape(b, t, d)
```

```python
import math

import jax
import jax.numpy as jnp
from jax import lax
from jax.experimental import pallas as pl
from jax.experimental.pallas import tpu as pltpu

F32 = jnp.float32
BF16 = jnp.bfloat16

D_MODEL = 1024
SEQ = 2048
DN_HEADS = 4
DN_HEAD_DIM = 128
DN_WIDTH = DN_HEADS * DN_HEAD_DIM
DN_CONV = 4
DN_CHUNK = 64
NSA_HEADS = 4
NSA_HEAD_DIM = 64
NSA_WIDTH = NSA_HEADS * NSA_HEAD_DIM
CMP_LEN = 32
CMP_STRIDE = 16
CMP_HIDDEN = 2 * NSA_HEAD_DIM
SLC_BLOCK = 64
SLC_TOP_N = 16
WINDOW = 512
CONV_WIDTH = 256
CONV_KERNEL = 31
REL_BUCKETS = 32
REL_MAX_DIST = 128
N_EXPERTS = 32
TOP_K = 4
D_FF = D_MODEL
SWIGLU_LIMIT = 7.0
SWIGLU_ALPHA = 1.702
EPS = 1e-6
NEG_INF = -1e30
FORCE = 1e4

LANES = 128
VMEM_LIMIT_BYTES = 48 * 1024 * 1024
EXPERT_VMEM_LIMIT_BYTES = 56 * 1024 * 1024

ROW_TILE = 512
SEQ_TILE = 256
Q_TILE = 256
SLC_GROUP = 2
MOE_TILE = 512
COMBINE_TILE = 256
DISPATCH_TILE = 1024

N_CMP_PAD = 128
N_SLC = SEQ // SLC_BLOCK


def _params(sem=None):
    return pltpu.CompilerParams(dimension_semantics=sem, vmem_limit_bytes=VMEM_LIMIT_BYTES)


def _mm(a, b):
    return jnp.dot(a.astype(BF16), b.astype(BF16), preferred_element_type=F32)


def _mm_nt(a, b):
    return lax.dot_general(a.astype(BF16), b.astype(BF16), (((1,), (1,)), ((), ())),
                           preferred_element_type=F32)


def _mm_tn(a, b):
    return lax.dot_general(a.astype(BF16), b.astype(BF16), (((0,), (0,)), ((), ())),
                           preferred_element_type=F32)


def _split3(x):
    hi = x.astype(BF16)
    r1 = x - hi.astype(F32)
    mid = r1.astype(BF16)
    lo = (r1 - mid.astype(F32)).astype(BF16)
    return hi, mid, lo


def _dot01_right(x, m01):
    hi, mid, lo = _split3(x)
    d = lambda p: jnp.dot(p, m01, preferred_element_type=F32)
    return d(hi) + d(mid) + d(lo)


def _dot01_left(m01, x):
    hi, mid, lo = _split3(x)
    d = lambda p: jnp.dot(m01, p, preferred_element_type=F32)
    return d(hi) + d(mid) + d(lo)


def _hi_lo(x):
    hi = x.astype(BF16)
    return hi, (x - hi.astype(F32)).astype(BF16)


def _sigmoid(x):
    return 1.0 / (1.0 + jnp.exp(-x))


def _silu(x):
    return x * _sigmoid(x)


def _softplus(x):
    return jnp.maximum(x, 0.0) + jnp.log(1.0 + jnp.exp(-jnp.abs(x)))


SUBLANES = 8
ROW_CHUNKS = D_MODEL // LANES
assert ROW_CHUNKS == SUBLANES


def _store_tile_rows(ref, value, rows):
    for s in range(ROW_CHUNKS):
        ref[pl.ds(s, rows, stride=SUBLANES), :] = value[:, s * LANES:(s + 1) * LANES]


def _load_tile_rows(ref, rows):
    return jnp.concatenate([ref[pl.ds(s, rows, stride=SUBLANES), :] for s in range(ROW_CHUNKS)], axis=1)


def _bcast_col(x, j, width):
    return jnp.broadcast_to(x[:, j:j + 1], (x.shape[0], width))


IN_SEGS = (3 * DN_WIDTH, DN_WIDTH, NSA_WIDTH, 6 * NSA_HEAD_DIM, 2 * CONV_WIDTH, LANES)
IN_COLS = sum(IN_SEGS)


INV_BLOCK = 16


def _delta_prepare(q_ref, k_ref, v_ref, gb_ref, grow_ref, wy):
    u_ref, w_ref, attn_ref, qg_ref, kk_ref, gc_ref = wy
    ts = SEQ_TILE
    ch = DN_CHUNK
    hd = DN_HEAD_DIM
    r = lax.broadcasted_iota(jnp.int32, (ts, ts), 0)
    c = lax.broadcasted_iota(jnp.int32, (ts, ts), 1)
    same_chunk = (r // ch) == (c // ch)
    tril = same_chunk & (r >= c)
    strict = same_chunk & (r > c)
    same16 = (r // INV_BLOCK) == (c // INV_BLOCK)
    eye = jnp.where(r == c, 1.0, 0.0).astype(F32)
    m_col = jnp.where(tril, 1.0, 0.0).astype(BF16)
    m_row = jnp.where(same_chunk & (r <= c), 1.0, 0.0).astype(BF16)

    gb = gb_ref[0]
    gc_col = _dot01_left(m_col, gb)
    gc_row = _dot01_right(grow_ref[...], m_row)

    heads = range(DN_HEADS)
    hsl = [slice(h * hd, (h + 1) * hd) for h in heads]
    kh = [k_ref[0, :, hsl[h]] for h in heads]
    gcb = [_bcast_col(gc_col, h, ts) for h in heads]
    decay = [jnp.where(tril, jnp.exp(jnp.where(
        tril, gcb[h] - jnp.broadcast_to(gc_row[h:h + 1, :], (ts, ts)), 0.0)), 0.0) for h in heads]
    beta = [_bcast_col(gb, DN_HEADS + h, hd) for h in heads]
    kb = [kh[h] * beta[h] for h in heads]
    yield
    a_mat = [jnp.where(strict, _mm_nt(kb[h], kh[h]) * decay[h], 0.0) for h in heads]
    yield
    d_mat = [jnp.where(same16, a_mat[h], 0.0) for h in heads]
    e_mat = [a_mat[h] - d_mat[h] for h in heads]
    d2 = [_mm(d_mat[h], d_mat[h]) for h in heads]
    yield
    d4 = [_mm(d2[h], d2[h]) for h in heads]
    yield
    t1 = [_mm(eye - d_mat[h], eye + d2[h]) for h in heads]
    yield
    d8 = [_mm(d4[h], d4[h]) for h in heads]
    yield
    t2 = [_mm(t1[h], eye + d4[h]) for h in heads]
    yield
    p_mat = [_mm(t2[h], eye + d8[h]) for h in heads]
    yield
    m_mat = [_mm(p_mat[h], e_mat[h]) for h in heads]
    yield
    m2 = [_mm(m_mat[h], m_mat[h]) for h in heads]
    yield
    t3 = [_mm(eye - m_mat[h], eye + m2[h]) for h in heads]
    yield
    t_mat = [_mm(t3[h], p_mat[h]) for h in heads]
    yield
    for h in heads:
        gc128 = gcb[h][:, :hd]
        expg = jnp.exp(gc128)
        qh = q_ref[0, :, hsl[h]]
        sol = _mm(t_mat[h], jnp.concatenate([v_ref[0, :, hsl[h]] * beta[h], kb[h] * expg], axis=1))
        u_ref[h] = sol[:, :hd]
        w_ref[h] = sol[:, hd:].astype(BF16)
        qg_ref[h] = (qh * expg).astype(BF16)
        kk_ref[h] = kh[h]
        gc_ref[h] = gc128
    yield
    for h in heads:
        attn_ref[h] = jnp.where(tril, _mm_nt(q_ref[0, :, hsl[h]], kh[h]) * decay[h], 0.0).astype(BF16)


def _delta_recur(wy, z_ref, nw_ref, o_ref, s_ref):
    u_ref, w_ref, attn_ref, qg_ref, kk_ref, gc_ref = wy
    ch = DN_CHUNK
    hd = DN_HEAD_DIM
    heads = range(DN_HEADS)
    state = [s_ref[h] for h in heads]
    outs = [[] for _ in heads]
    for ci in range(SEQ_TILE // ch):
        rs = slice(ci * ch, (ci + 1) * ch)
        v_new = [u_ref[h, rs, :] - _mm(w_ref[h, rs, :], state[h]) for h in heads]
        o_state = [_mm(qg_ref[h, rs, :], state[h]) for h in heads]
        yield
        for h in heads:
            gc = gc_ref[h, rs, :]
            g_last = gc[ch - 1:ch, :]
            k_dec = kk_ref[h, rs, :] * jnp.exp(g_last - gc)
            outs[h].append(o_state[h] + _mm(attn_ref[h, rs, ci * ch:(ci + 1) * ch], v_new[h]))
            state[h] = state[h] * jnp.exp(g_last) + _mm_tn(k_dec, v_new[h])
        yield

    for h in heads:
        s_ref[h] = state[h]
        hs = slice(h * hd, (h + 1) * hd)
        o = jnp.concatenate(outs[h], axis=0)
        o = o * lax.rsqrt(jnp.mean(o * o, axis=-1, keepdims=True) + EPS) * nw_ref[...]
        o_ref[0, :, hs] = o * _silu(z_ref[0, :, hs])


def _interleave(weighted):
    live = [gen for gen, _ in weighted]
    while live:
        for gen, n in weighted:
            for _ in range(n):
                if gen in live and next(gen, live) is live:
                    live.remove(gen)


DN_BATCH = 2


def _delta_kernel(q_ref, k_ref, v_ref, gb_ref, grow0_ref, grow1_ref, z_ref, nw_ref, o_ref, s_ref,
                  *wy_refs):
    t = pl.program_id(1)
    set_a, set_b = wy_refs[:6], wy_refs[6:]
    grow_refs = (grow0_ref, grow1_ref)

    @pl.when(t == 0)
    def _():
        s_ref[...] = jnp.zeros(s_ref.shape, F32)
        for ref in set_b:
            ref[...] = jnp.zeros(ref.shape, ref.dtype)

    def step(read_set, write_set):
        one = lambda ref, bb: ref.at[pl.ds(bb, 1)]
        work = []
        for bb in range(DN_BATCH):
            work.append((_delta_prepare(one(q_ref, bb), one(k_ref, bb), one(v_ref, bb), one(gb_ref, bb),
                                        grow_refs[bb], [ref.at[bb] for ref in write_set]), 2))
        for bb in range(DN_BATCH):
            work.append((_delta_recur([ref.at[bb] for ref in read_set], one(z_ref, bb), nw_ref,
                                      one(o_ref, bb), s_ref.at[bb]), 1))
        _interleave(work)

    @pl.when(lax.rem(t, 2) == 0)
    def _():
        step(set_b, set_a)

    @pl.when(lax.rem(t, 2) == 1)
    def _():
        step(set_a, set_b)


def _delta_rule(q, k, v, gb, grow, z, norm_w):
    b = q.shape[0]
    nb = DN_BATCH
    assert nb == 2 and b % nb == 0
    ts = SEQ_TILE
    nt = SEQ // ts
    nxt = lambda w: pl.BlockSpec((nb, ts, w), lambda i, t: (i, jnp.minimum(t, nt - 1), 0))
    cur = lambda w: pl.BlockSpec((nb, ts, w), lambda i, t: (i, jnp.maximum(t - 1, 0), 0))
    grow_spec = lambda bb: pl.BlockSpec(
        (8, ts), lambda i, t: (0, (i * nb + bb) * nt + jnp.minimum(t, nt - 1)))
    hd = DN_HEAD_DIM
    wy_set = [pltpu.VMEM((nb, DN_HEADS, ts, hd), F32), pltpu.VMEM((nb, DN_HEADS, ts, hd), BF16),
              pltpu.VMEM((nb, DN_HEADS, ts, ts), BF16), pltpu.VMEM((nb, DN_HEADS, ts, hd), BF16),
              pltpu.VMEM((nb, DN_HEADS, ts, hd), F32), pltpu.VMEM((nb, DN_HEADS, ts, hd), F32)]
    return pl.pallas_call(
        _delta_kernel,
        out_shape=jax.ShapeDtypeStruct((b, SEQ, DN_WIDTH), F32),
        grid=(b // nb, nt + 1),
        in_specs=[
            nxt(DN_WIDTH), nxt(DN_WIDTH), nxt(DN_WIDTH), nxt(LANES),
            grow_spec(0), grow_spec(1),
            cur(DN_WIDTH),
            pl.BlockSpec((1, DN_HEAD_DIM), lambda i, t: (0, 0)),
        ],
        out_specs=cur(DN_WIDTH),
        scratch_shapes=[pltpu.VMEM((nb, DN_HEADS, hd, hd), F32)] + wy_set + wy_set,
        compiler_params=_params(("parallel", "arbitrary")),
        name="delta_rule",
    )(q, k, v, gb, grow, grow, z, norm_w)


CONV_HALO = 32


MIX_COLS = {'qkv': (0, 3 * DN_WIDTH), 'z': (3 * DN_WIDTH, 4 * DN_WIDTH)}
_off = 4 * DN_WIDTH
for _name, _w in (('nq', NSA_WIDTH), ('nkv', 6 * NSA_HEAD_DIM), ('cu', 2 * CONV_WIDTH), ('small', LANES)):
    MIX_COLS[_name] = (_off, _off + _w)
    _off += _w
assert _off == IN_COLS
DN_HALO = 8
MXU_CHUNK = 256
CF_ROWS = 64


def _ordering_zero(v):
    bits = pltpu.bitcast(v, jnp.uint32)
    sixteen = jnp.uint32(16)
    return pltpu.bitcast(lax.shift_right_logical(lax.shift_right_logical(bits, sixteen), sixteen), F32)


def _front_kernel(x_ref, nw_ref, w_ref, wabt_ref, cw_ref, pcol_ref, prow_ref,
                  dww_ref, dwb_ref, lnw_ref, lnb_ref,
                  q_ref, k_ref, v_ref, gb_ref, grow_ref, z_ref, nq_ref, nkv_ref, small_ref, yc_ref,
                  dn_buf, cf_buf, cf_shift):
    i = pl.program_id(0)
    tm = ROW_TILE
    hd = NSA_HEAD_DIM

    @pl.when(lax.rem(i, SEQ // tm) == 0)
    def _():
        dn_buf[0:DN_HALO, :] = jnp.zeros((DN_HALO, 3 * DN_WIDTH), F32)
        cf_buf[0:CONV_HALO, :] = jnp.zeros((CONV_HALO, CONV_WIDTH), F32)

    xf = x_ref[...]
    hb = (xf * lax.rsqrt(jnp.mean(xf * xf, axis=-1, keepdims=True) + EPS) * nw_ref[...]).astype(BF16)

    def proj(name, lo=0, hi=None):
        c0, c1 = MIX_COLS[name]
        hi = c1 - c0 if hi is None else hi
        return jnp.dot(hb, w_ref[:, c0 + lo:c0 + hi], preferred_element_type=F32)

    ties = []

    def mxu_items():
        cu = proj('cu')
        cf_buf[CONV_HALO:CONV_HALO + tm, :] = cu[:, :CONV_WIDTH] * _sigmoid(cu[:, CONV_WIDTH:])
        yield
        sm = proj('small')
        small_ref[...] = sm
        lane = lax.broadcasted_iota(jnp.int32, sm.shape, 1)
        g_col = -jnp.exp(pcol_ref[0:1, :]) * _softplus(sm + pcol_ref[1:2, :])
        gb_ref[...] = jnp.where(lane < DN_HEADS, g_col, _sigmoid(sm))
        a_t = lax.dot_general(wabt_ref[...], hb, (((1,), (1,)), ((), ())), preferred_element_type=F32)
        g_row = -jnp.exp(prow_ref[:, 0:1]) * _softplus(a_t + prow_ref[:, 1:2])
        grow_ref[...] = g_row[0:8, :]
        yield
        for c0 in range(0, DN_WIDTH, MXU_CHUNK):
            res = proj('z', c0, c0 + MXU_CHUNK)
            z_ref[:, c0:c0 + MXU_CHUNK] = res
            ties.append(_ordering_zero(res[0:CF_ROWS, 0:LANES]))
            yield
        res = proj('nq')
        for j in range(NSA_HEADS):
            nq_ref[j] = res[:, j * hd:(j + 1) * hd]
        ties.append(_ordering_zero(res[0:CF_ROWS, 0:LANES]))
        yield
        res = proj('nkv')
        for j in range(6):
            nkv_ref[j] = res[:, j * hd:(j + 1) * hd]
        ties.append(_ordering_zero(res[0:CF_ROWS, 0:LANES]))
        yield
        for c0 in range(0, 3 * DN_WIDTH, MXU_CHUNK):
            dn_buf[DN_HALO:DN_HALO + tm, c0:c0 + MXU_CHUNK] = proj('qkv', c0, c0 + MXU_CHUNK)
            yield

    def conformer_items():
        span = tm + CONV_HALO - SUBLANES
        for b in range(1, SUBLANES):
            cf_shift[b] = cf_buf[b:b + span, :]
        yield
        base = CONV_HALO - (CONV_KERNEL - 1)
        rows = CF_ROWS
        for rc in range(tm // rows):
            parts = []
            for cc in range(CONV_WIDTH // LANES):
                cs = slice(cc * LANES, (cc + 1) * LANES)
                acc = None
                for j in range(CONV_KERNEL):
                    start = base + rc * rows + j
                    b = start % SUBLANES
                    a0 = start - b
                    win = cf_buf[a0:a0 + rows, cs] if b == 0 else cf_shift[b, a0:a0 + rows, cs]
                    term = dww_ref[j:j + 1, cs] * win
                    acc = term if acc is None else acc + term
                if cc == 0 and rc < len(ties):
                    acc = acc + ties[rc]
                parts.append(acc)
            h = jnp.concatenate(parts, axis=1) + dwb_ref[...]
            mu = jnp.mean(h, axis=-1, keepdims=True)
            var = jnp.mean(jnp.square(h - mu), axis=-1, keepdims=True)
            hn = (h - mu) * lax.rsqrt(var + EPS) * lnw_ref[...] + lnb_ref[...]
            yc_ref[rc * rows:(rc + 1) * rows, :] = _silu(hn)
            yield
        cf_buf[0:CONV_HALO, :] = cf_buf[tm:tm + CONV_HALO, :]

    def deltanet_items():
        rows = tm // 2
        for c in range(3 * DN_HEADS):
            cs = slice(c * LANES, (c + 1) * LANES)
            for r0 in range(0, tm, rows):
                lo = DN_HALO - (DN_CONV - 1) + r0
                acc = cw_ref[0:1, cs] * dn_buf[lo:lo + rows, cs]
                for j in range(1, DN_CONV):
                    acc = acc + cw_ref[j:j + 1, cs] * dn_buf[lo + j:lo + j + rows, cs]
                y = _silu(acc)
                if c < 2 * DN_HEADS:
                    y = y * lax.rsqrt(jnp.sum(y * y, axis=-1, keepdims=True) + EPS)
                rs = slice(r0, r0 + rows)
                if c < DN_HEADS:
                    q_ref[rs, cs] = y * DN_HEAD_DIM ** -0.5
                elif c < 2 * DN_HEADS:
                    k_ref[rs, (c - DN_HEADS) * LANES:(c - DN_HEADS + 1) * LANES] = y
                else:
                    v_ref[rs, (c - 2 * DN_HEADS) * LANES:(c - 2 * DN_HEADS + 1) * LANES] = y
            dn_buf[0:DN_HALO, cs] = dn_buf[tm:tm + DN_HALO, cs]
            yield

    mxu, cf, dn = mxu_items(), conformer_items(), deltanet_items()
    next(mxu)
    next(mxu)
    next(cf)
    for _ in range(4):
        next(mxu)
        next(cf)
    next(mxu)
    _interleave([(mxu, 1), (dn, 2), (cf, 1)])


def _front(x2, norm_w, w_cat, w_abt, conv_w, pcol, prow, dw_w, dw_b, ln_w, ln_b):
    n = x2.shape[0]
    tm = ROW_TILE
    hd = NSA_HEAD_DIM
    row = lambda w: pl.BlockSpec((tm, w), lambda i: (i, 0))
    full = lambda shape: pl.BlockSpec(shape, lambda i: (0,) * len(shape))
    pieces = lambda k: pl.BlockSpec((k, tm, hd), lambda i: (0, i, 0))
    return pl.pallas_call(
        _front_kernel,
        out_shape=[jax.ShapeDtypeStruct((n, DN_WIDTH), F32)] * 3
        + [jax.ShapeDtypeStruct((n, LANES), F32), jax.ShapeDtypeStruct((8, n), F32),
           jax.ShapeDtypeStruct((n, DN_WIDTH), F32),
           jax.ShapeDtypeStruct((NSA_HEADS, n, hd), F32), jax.ShapeDtypeStruct((6, n, hd), F32),
           jax.ShapeDtypeStruct((n, LANES), F32), jax.ShapeDtypeStruct((n, CONV_WIDTH), F32)],
        grid=(n // tm,),
        in_specs=[row(D_MODEL), full((1, D_MODEL)), full((D_MODEL, IN_COLS)), full((16, D_MODEL)),
                  full((DN_CONV, 3 * DN_WIDTH)), full((8, LANES)), full((16, LANES)),
                  full((CONV_KERNEL, CONV_WIDTH)), full((1, CONV_WIDTH)), full((1, CONV_WIDTH)),
                  full((1, CONV_WIDTH))],
        out_specs=[row(DN_WIDTH)] * 3
        + [row(LANES), pl.BlockSpec((8, tm), lambda i: (0, i)), row(DN_WIDTH),
           pieces(NSA_HEADS), pieces(6), row(LANES), row(CONV_WIDTH)],
        scratch_shapes=[pltpu.VMEM((tm + DN_HALO, 3 * DN_WIDTH), F32),
                        pltpu.VMEM((tm + CONV_HALO, CONV_WIDTH), F32),
                        pltpu.VMEM((SUBLANES, tm + CONV_HALO - SUBLANES, CONV_WIDTH), F32)],
        compiler_params=_params(("arbitrary",)),
        name="front",
    )(x2, norm_w, w_cat, w_abt, conv_w, pcol, prow, dw_w, dw_b, ln_w, ln_b)


def _rms_rows(x, w):
    return x * lax.rsqrt(jnp.mean(x * x, axis=-1, keepdims=True) + EPS) * w


def _nsa_prep_kernel(kc_ref, vc_ref, ks_ref, kw_ref, pos_ref, w1_ref, w2_ref, knw_ref,
                     kcmp_ref, vcmp_ref, ksn_ref, kwn_ref):
    hd = NSA_HEAD_DIM

    def compress(x_ref, i):
        u_lo = jnp.zeros((N_CMP_PAD, CMP_HIDDEN), F32)
        u_hi = jnp.zeros((N_CMP_PAD, CMP_HIDDEN), F32)
        for r in range(CMP_STRIDE):
            xr = x_ref[pl.ds(r, N_CMP_PAD, stride=CMP_STRIDE), :]
            lo, hi = r, CMP_STRIDE + r
            u_lo = u_lo + _mm(xr + pos_ref[i, lo:lo + 1, :], w1_ref[i, lo * hd:(lo + 1) * hd, :])
            u_hi = u_hi + _mm(xr + pos_ref[i, hi:hi + 1, :], w1_ref[i, hi * hd:(hi + 1) * hd, :])
        hid = _silu(u_lo + pltpu.roll(u_hi, N_CMP_PAD - 1, axis=0))
        return _mm(hid, w2_ref[i])

    kcmp_ref[0] = _rms_rows(compress(kc_ref, 0), knw_ref[0:1, :])
    vcmp_ref[0] = compress(vc_ref, 1)
    ksn_ref[0] = _rms_rows(ks_ref[...], knw_ref[1:2, :])
    kwn_ref[0] = _rms_rows(kw_ref[...], knw_ref[2:3, :])


def _nsa_prep(kv6, pos, w1, w2, knw):
    b = kv6.shape[1]
    hd = NSA_HEAD_DIM
    full = lambda shape: pl.BlockSpec(shape, lambda i: (0,) * len(shape))
    piece = lambda j: pl.BlockSpec((None, None, SEQ, hd), lambda i: (j, i, 0, 0))
    bspec = lambda r, w: pl.BlockSpec((1, r, w), lambda i: (i, 0, 0))
    return pl.pallas_call(
        _nsa_prep_kernel,
        out_shape=[jax.ShapeDtypeStruct((b, N_CMP_PAD, hd), F32)] * 2
        + [jax.ShapeDtypeStruct((b, SEQ, hd), F32)] * 2,
        grid=(b,),
        in_specs=[
            piece(0), piece(1), piece(2), piece(4),
            full((2, CMP_LEN, hd)), full((2, CMP_LEN * hd, CMP_HIDDEN)), full((2, CMP_HIDDEN, hd)),
            full((3, hd)),
        ],
        out_specs=[bspec(N_CMP_PAD, hd)] * 2 + [bspec(SEQ, hd)] * 2,
        compiler_params=_params(("parallel",)),
        name="nsa_prep",
    )(kv6, kv6, kv6, kv6, pos, w1, w2, knw)


def _nsa_attn_kernel(q_ref, qnw_ref, kcmp_ref, vcmp_ref, ks_ref, vs_ref, kw_ref, vw_ref,
                     gate_ref, bcmp_ref, btab_ref, o_ref,
                     madd_ref, s_ref, ksb_ref, kwb_ref, vsa_ref, vwa_ref):
    i = pl.program_id(1)
    tq = Q_TILE
    nh = NSA_HEADS
    hd = NSA_HEAD_DIM
    t0 = i * tq

    @pl.when(i == 0)
    def _():
        ones = jnp.ones((SEQ, hd), BF16)
        ksb_ref[...] = ks_ref[0].astype(BF16)
        kwb_ref[...] = kw_ref[0].astype(BF16)
        vsa_ref[...] = jnp.concatenate([vs_ref[...].astype(BF16), ones], axis=1)
        vwa_ref[...] = jnp.concatenate([vw_ref[...].astype(BF16), ones], axis=1)

    qs = jnp.concatenate(
        [_rms_rows(q_ref[h], qnw_ref[...]) * hd ** -0.5 for h in range(nh)], axis=0).astype(BF16)

    row = lax.broadcasted_iota(jnp.int32, (tq, LANES), 0)
    lane = lax.broadcasted_iota(jnp.int32, (tq, LANES), 1)
    qpos = t0 + row

    branch_out = {}

    def select_items():
        s_all = _mm_nt(qs, kcmp_ref[0])
        cmp_valid = (qpos >= lane * CMP_STRIDE + (CMP_LEN - 1)) & (lane < N_CMP_PAD - 1)
        p_rows = []
        p_sum = jnp.zeros((tq, LANES), F32)
        for h in range(nh):
            s = jnp.where(cmp_valid, s_all[h * tq:(h + 1) * tq] + bcmp_ref[h], NEG_INF)
            m = jnp.max(s, axis=-1, keepdims=True)
            p = jnp.where(cmp_valid, jnp.exp(s - m), 0.0)
            l = jnp.sum(p, axis=-1, keepdims=True)
            p = p * jnp.where(l > 0.0, 1.0 / l, 0.0)
            p_rows.append(p)
            p_sum = p_sum + p
            yield
        branch_out['cmp'] = _mm(jnp.concatenate(p_rows, axis=0), vcmp_ref[0])

        ss = lax.broadcasted_iota(jnp.int32, (N_SLC, N_CMP_PAD), 0)
        jj = lax.broadcasted_iota(jnp.int32, (N_SLC, N_CMP_PAD), 1)
        overlap_t = ((jj * CMP_STRIDE < ss * SLC_BLOCK + SLC_BLOCK)
                     & (jj * CMP_STRIDE + CMP_LEN > ss * SLC_BLOCK) & (jj < N_CMP_PAD - 1))
        overlap_t = jnp.where(overlap_t, 1.0, 0.0).astype(BF16)
        nt = lambda a, b_: lax.dot_general(a, b_, (((1,), (1,)), ((), ())), preferred_element_type=F32)
        p_hi, p_mid, p_lo = _split3(p_sum)
        imp = nt(overlap_t, p_hi) + nt(overlap_t, p_mid) + nt(overlap_t, p_lo)
        blk = lax.broadcasted_iota(jnp.int32, (N_SLC, tq), 0)
        cur = (t0 + lax.broadcasted_iota(jnp.int32, (N_SLC, tq), 1)) // SLC_BLOCK
        causal_blk = blk <= cur
        forced = (blk == 0) | (blk == cur) | (blk == cur - 1)
        imp = jnp.where(causal_blk & forced, FORCE, jnp.where(causal_blk, imp, -1.0))
        yield
        rank = jnp.zeros((N_SLC, tq), jnp.int32)
        for s2 in range(N_SLC):
            other = jnp.broadcast_to(imp[s2:s2 + 1, :], (N_SLC, tq))
            beats = (other > imp) | ((other == imp) & (blk > s2))
            rank = rank + jnp.where(beats, 1, 0)
            if s2 % 8 == 7:
                yield
        sel_t = jnp.where(rank < SLC_TOP_N, 1.0, 0.0).astype(BF16)
        er = lax.broadcasted_iota(jnp.int32, (N_SLC, SEQ), 0)
        ec = lax.broadcasted_iota(jnp.int32, (N_SLC, SEQ), 1)
        expand = jnp.where(ec // SLC_BLOCK == er, 1.0, 0.0).astype(BF16)
        key = lax.broadcasted_iota(jnp.int32, (tq, SEQ), 1)
        qall = t0 + lax.broadcasted_iota(jnp.int32, (tq, SEQ), 0)
        chosen = (_mm_tn(sel_t, expand) > 0.5) & (qall >= key)
        madd_ref[...] = jnp.where(chosen, 0.0, NEG_INF)

    neg = jnp.full((nh * tq, tq), NEG_INF, F32)
    zero = jnp.zeros((nh * tq, LANES), F32)

    def scores(k_tile, bias_idx, add):
        s_t = _mm_nt(qs, k_tile)
        return jnp.concatenate(
            [s_t[h * tq:(h + 1) * tq] + (btab_ref[h, bias_idx] + add) for h in range(nh)], axis=0)

    def normalise(acc):
        return acc[:, :hd] * (1.0 / _bcast_col(acc, hd, hd))

    def window_items():
        kcol = lax.broadcasted_iota(jnp.int32, (tq, tq), 1)
        qrow = t0 + lax.broadcasted_iota(jnp.int32, (tq, tq), 0)
        n_win = WINDOW // tq + 1
        starts = []
        m_run = neg
        for j in range(n_win):
            d = n_win - 1 - j
            k0 = pl.multiple_of(jnp.maximum(i - d, 0) * tq, tq)
            starts.append(k0)
            dist = qrow - (k0 + kcol)
            ok = (dist >= 0) & (dist < WINDOW) & (i >= d)
            s_m = scores(kwb_ref[pl.ds(k0, tq), :], min(d, 2), jnp.where(ok, 0.0, NEG_INF))
            s_ref[:, j * tq:(j + 1) * tq] = s_m
            m_run = jnp.maximum(m_run, s_m)
            yield
        m_win = jnp.broadcast_to(jnp.max(m_run, axis=-1, keepdims=True), (nh * tq, tq))
        acc = zero
        for j in range(n_win):
            p = jnp.exp(s_ref[:, j * tq:(j + 1) * tq] - m_win).astype(BF16)
            acc = acc + jnp.dot(p, vwa_ref[pl.ds(starts[j], tq), :], preferred_element_type=F32)
            yield
        branch_out['win'] = normalise(acc)

    _interleave([(select_items(), 1), (window_items(), 1)])
    o_cmp, o_win = branch_out['cmp'], branch_out['win']

    n_groups = (i + SLC_GROUP) // SLC_GROUP

    def slc_pass1(j, m_run):
        for u in range(SLC_GROUP):
            kt = SLC_GROUP * j + u
            k0 = pl.multiple_of(kt * tq, tq)
            s_m = scores(ksb_ref[pl.ds(k0, tq), :], jnp.clip(i - kt, 0, 2), madd_ref[:, pl.ds(k0, tq)])
            s_ref[:, pl.ds(k0, tq)] = s_m
            m_run = jnp.maximum(m_run, s_m)
        return m_run

    m_slc = jnp.broadcast_to(
        jnp.max(lax.fori_loop(0, n_groups, slc_pass1, neg), axis=-1, keepdims=True), (nh * tq, tq))

    def slc_pass2(j, acc):
        for u in range(SLC_GROUP):
            k0 = pl.multiple_of((SLC_GROUP * j + u) * tq, tq)
            p = jnp.exp(s_ref[:, pl.ds(k0, tq)] - m_slc).astype(BF16)
            acc = acc + jnp.dot(p, vsa_ref[pl.ds(k0, tq), :], preferred_element_type=F32)
        return acc

    o_slc = normalise(lax.fori_loop(0, n_groups, slc_pass2, zero))

    gates = _sigmoid(gate_ref[0])
    outs = []
    for h in range(nh):
        hs = slice(h * tq, (h + 1) * tq)
        g0 = 2 * DN_HEADS + 3 * h
        outs.append(_bcast_col(gates, g0, hd) * o_cmp[hs]
                    + _bcast_col(gates, g0 + 1, hd) * o_slc[hs]
                    + _bcast_col(gates, g0 + 2, hd) * o_win[hs])
    o_ref[...] = jnp.concatenate(outs, axis=1)


def _nsa_attn(q, qnw, kcmp, vcmp, ks, kv6, kw, small, bias_cmp, btab):
    b = kv6.shape[1]
    tq = Q_TILE
    nq = SEQ // tq
    nh, hd = NSA_HEADS, NSA_HEAD_DIM
    full = lambda shape: pl.BlockSpec(shape, lambda bi, i: (0,) * len(shape))
    per_b = lambda r: pl.BlockSpec((1, r, hd), lambda bi, i: (bi, 0, 0))
    piece = lambda j: pl.BlockSpec((None, None, SEQ, hd), lambda bi, i: (j, bi, 0, 0))
    heads = pl.BlockSpec((nh, tq, hd), lambda bi, i: (0, bi * nq + i, 0))
    return pl.pallas_call(
        _nsa_attn_kernel,
        out_shape=jax.ShapeDtypeStruct((b * SEQ, NSA_WIDTH), F32),
        grid=(b, nq),
        in_specs=[
            heads,
            full((1, hd)),
            per_b(N_CMP_PAD), per_b(N_CMP_PAD), per_b(SEQ), piece(3), per_b(SEQ), piece(5),
            pl.BlockSpec((1, tq, LANES), lambda bi, i: (bi, i, 0)),
            pl.BlockSpec((nh, tq, N_CMP_PAD), lambda bi, i: (0, i, 0)),
            full((nh, 3, tq, tq)),
        ],
        out_specs=pl.BlockSpec((tq, NSA_WIDTH), lambda bi, i: (bi * nq + i, 0)),
        scratch_shapes=[pltpu.VMEM((tq, SEQ), F32), pltpu.VMEM((nh * tq, SEQ), F32),
                        pltpu.VMEM((SEQ, hd), BF16), pltpu.VMEM((SEQ, hd), BF16),
                        pltpu.VMEM((SEQ, 2 * hd), BF16), pltpu.VMEM((SEQ, 2 * hd), BF16)],
        compiler_params=_params(("parallel", "arbitrary")),
        name="nsa_attn",
    )(q, qnw, kcmp, vcmp, ks, kv6, kw, kv6, small, bias_cmp, btab)


def _out_router_kernel(ya_ref, yb_ref, yc_ref, x_ref, wo_ref, fnw_ref, rw_ref, rb_ref,
                       xo_ref, h_ref, idx_ref, wt_ref, hist_ref):
    wa = DN_WIDTH
    wb = wa + NSA_WIDTH
    y = (jnp.dot(ya_ref[...].astype(BF16), wo_ref[0:wa, :], preferred_element_type=F32)
         + jnp.dot(yb_ref[...].astype(BF16), wo_ref[wa:wb, :], preferred_element_type=F32)
         + jnp.dot(yc_ref[...].astype(BF16), wo_ref[wb:, :], preferred_element_type=F32))
    xn = x_ref[...] + y
    xo_ref[...] = xn
    h = xn * lax.rsqrt(jnp.mean(xn * xn, axis=-1, keepdims=True) + EPS) * fnw_ref[...]
    _store_tile_rows(h_ref, h, h.shape[0])
    h_hi, h_lo = _hi_lo(h)
    r_hi = jnp.dot(h_hi, rw_ref[...], preferred_element_type=F32)
    r_lo = jnp.dot(h_lo, rw_ref[...], preferred_element_type=F32)
    logits = ((r_hi[:, :LANES] + r_hi[:, LANES:]) + (r_lo[:, :LANES] + r_lo[:, LANES:])) + rb_ref[...]
    lane = lax.broadcasted_iota(jnp.int32, logits.shape, 1)
    vals, idxs = [], []
    for _ in range(TOP_K):
        m = jnp.max(logits, axis=-1, keepdims=True)
        ix = jnp.min(jnp.where(logits == m, lane, LANES), axis=-1, keepdims=True)
        vals.append(m)
        idxs.append(ix)
        logits = jnp.where(lane == ix, -jnp.inf, logits)
    es = [jnp.exp(v - vals[0]) for v in vals]
    inv = 1.0 / (es[0] + es[1] + es[2] + es[3])
    idx_out = jnp.zeros(lane.shape, jnp.int32)
    wt_out = jnp.zeros(lane.shape, F32)
    chosen = jnp.zeros(lane.shape, F32)
    for k in range(TOP_K):
        idx_out = jnp.where(lane == k, idxs[k], idx_out)
        wt_out = jnp.where(lane == k, es[k] * inv, wt_out)
        chosen = chosen + jnp.where(lane == idxs[k], 1.0, 0.0)
    idx_ref[...] = idx_out
    wt_ref[...] = wt_out
    hist_ref[0] = jnp.broadcast_to(jnp.sum(chosen, axis=0, keepdims=True), (8, LANES))


def _out_router(ya, yb, yc, x2, w_out, fnw, rw, rb):
    n = x2.shape[0]
    tm = ROW_TILE
    row = lambda w: pl.BlockSpec((tm, w), lambda i: (i, 0))
    full = lambda shape: pl.BlockSpec(shape, lambda i: (0,) * len(shape))
    return pl.pallas_call(
        _out_router_kernel,
        out_shape=[jax.ShapeDtypeStruct((n, D_MODEL), F32), jax.ShapeDtypeStruct((n * SUBLANES, LANES), F32),
                   jax.ShapeDtypeStruct((n, LANES), jnp.int32), jax.ShapeDtypeStruct((n, LANES), F32),
                   jax.ShapeDtypeStruct((n // tm, 8, LANES), F32)],
        grid=(n // tm,),
        in_specs=[row(DN_WIDTH), row(NSA_WIDTH), row(CONV_WIDTH), row(D_MODEL),
                  full((D_MODEL, D_MODEL)), full((1, D_MODEL)), full((D_MODEL, 2 * LANES)), full((1, LANES))],
        out_specs=[row(D_MODEL), pl.BlockSpec((tm * SUBLANES, LANES), lambda i: (i, 0)), row(LANES), row(LANES),
                   pl.BlockSpec((1, 8, LANES), lambda i: (i, 0, 0))],
        compiler_params=_params(("parallel",)),
        name="out_router",
    )(ya, yb, yc, x2, w_out, fnw, rw, rb)


def _slots_kernel(idx_ref, base_ref, slot_ref):
    tm = ROW_TILE
    idx = idx_ref[...]
    lane = lax.broadcasted_iota(jnp.int32, (tm, LANES), 1)
    onehots = [jnp.where(lane == _bcast_col(idx, k, LANES), 1.0, 0.0) for k in range(TOP_K)]
    cnt = (onehots[0] + onehots[1]) + (onehots[2] + onehots[3])
    r = lax.broadcasted_iota(jnp.int32, (tm, tm), 0)
    c = lax.broadcasted_iota(jnp.int32, (tm, tm), 1)
    earlier = jnp.where(r > c, 1.0, 0.0).astype(BF16)
    rank = jnp.dot(earlier, cnt.astype(BF16), preferred_element_type=F32) + base_ref[0, 0:1, :]
    out = jnp.zeros((tm, LANES), F32)
    for k in range(TOP_K):
        out = jnp.where(lane == k, jnp.sum(rank * onehots[k], axis=-1, keepdims=True), out)
    slot_ref[...] = (out * SUBLANES).astype(jnp.int32)


def _slots(idx, base3):
    n = idx.shape[0]
    tm = ROW_TILE
    return pl.pallas_call(
        _slots_kernel,
        out_shape=jax.ShapeDtypeStruct((n, LANES), jnp.int32),
        grid=(n // tm,),
        in_specs=[pl.BlockSpec((tm, LANES), lambda i: (i, 0)),
                  pl.BlockSpec((1, 8, LANES), lambda i: (i, 0, 0))],
        out_specs=pl.BlockSpec((tm, LANES), lambda i: (i, 0)),
        compiler_params=_params(("parallel",)),
        name="moe_slots",
    )(idx, base3)


def _dispatch_kernel(slot_ref, zoff_ref, nv_ref, h_ref, xs_hbm, zbuf, sem, zsem):
    i = pl.program_id(0)
    tm = DISPATCH_TILE
    base = i * tm

    @pl.when(i == 0)
    def _():
        zbuf[...] = jnp.zeros(zbuf.shape, F32)
        blk = MOE_TILE * SUBLANES
        n_blocks = xs_hbm.shape[0] // blk

        def fill(row0):
            return pltpu.make_async_copy(
                zbuf, xs_hbm.at[pl.ds(pl.multiple_of(row0 * SUBLANES, blk), blk)], zsem)

        def fill_region_end(e, c):
            fill(zoff_ref[e]).start()
            return c

        def fill_tail(b, c):
            fill(b * MOE_TILE).start()
            return c

        def wait_fill(b, c):
            fill(0).wait()
            return c

        lax.fori_loop(0, N_EXPERTS, fill_region_end, 0)
        lax.fori_loop(nv_ref[0], n_blocks, fill_tail, 0)
        lax.fori_loop(0, N_EXPERTS + n_blocks - nv_ref[0], wait_fill, 0)

    def body(r, c):
        src = h_ref.at[pl.ds(pl.multiple_of(r * SUBLANES, SUBLANES), SUBLANES)]
        for k in range(TOP_K):
            row0 = pl.multiple_of(slot_ref[(base + r) * TOP_K + k], SUBLANES)
            pltpu.make_async_copy(src, xs_hbm.at[pl.ds(row0, SUBLANES)], sem).start(priority=k % 2)
        return c

    lax.fori_loop(0, tm, body, 0, unroll=8)
    done = xs_hbm.at[pl.ds(0, TOP_K * tm * SUBLANES)]
    pltpu.make_async_copy(done, done, sem).wait()


def _dispatch(slot_flat, zero_off, n_valid, h, n_slots):
    n = h.shape[0] // SUBLANES
    tm = DISPATCH_TILE
    return pl.pallas_call(
        _dispatch_kernel,
        out_shape=jax.ShapeDtypeStruct((n_slots * SUBLANES, LANES), F32),
        grid_spec=pltpu.PrefetchScalarGridSpec(
            num_scalar_prefetch=3,
            grid=(n // tm,),
            in_specs=[pl.BlockSpec((tm * SUBLANES, LANES), lambda i, s, z, nv: (i, 0))],
            out_specs=pl.BlockSpec(memory_space=pl.ANY),
            scratch_shapes=[pltpu.VMEM((MOE_TILE * SUBLANES, LANES), F32),
                            pltpu.SemaphoreType.DMA, pltpu.SemaphoreType.DMA],
        ),
        compiler_params=_params(("arbitrary",)),
        name="moe_dispatch",
    )(slot_flat, zero_off, n_valid, h)


def _expert_kernel(be_ref, nv_ref, xs_ref, wgu_ref, bgu_ref, wd_ref, bd_ref, o_ref, wgu_bf, wd_bf):
    i = pl.program_id(0)
    new_expert = (i == 0) | (be_ref[i] != be_ref[jnp.maximum(i - 1, 0)])

    @pl.when((i < nv_ref[0]) & new_expert)
    def _():
        step = 4 * LANES
        for c0 in range(0, 2 * D_FF, step):
            wgu_bf[:, c0:c0 + step] = wgu_ref[:, c0:c0 + step].astype(BF16)
        for c0 in range(0, D_MODEL, step):
            wd_bf[:, c0:c0 + step] = wd_ref[:, c0:c0 + step].astype(BF16)

    @pl.when(i < nv_ref[0])
    def _():
        xb = _load_tile_rows(xs_ref, MOE_TILE).astype(BF16)
        gu = jnp.dot(xb, wgu_bf[...], preferred_element_type=F32) + bgu_ref[...]
        gate = jnp.minimum(gu[:, :D_FF], SWIGLU_LIMIT)
        up = jnp.clip(gu[:, D_FF:], -SWIGLU_LIMIT, SWIGLU_LIMIT)
        act = (up + 1.0) * gate * _sigmoid(SWIGLU_ALPHA * gate)
        y = jnp.dot(act.astype(BF16), wd_bf[...], preferred_element_type=F32) + bd_ref[...]
        _store_tile_rows(o_ref, y, MOE_TILE)

    @pl.when(i >= nv_ref[0])
    def _():
        o_ref[...] = jnp.zeros(o_ref.shape, F32)


def _expert_ffn(layer, block_expert, n_valid, xs, wgu, bgu, wd, bd):
    n_slots = xs.shape[0] // SUBLANES
    tm = MOE_TILE
    return pl.pallas_call(
        _expert_kernel,
        out_shape=jax.ShapeDtypeStruct(xs.shape, F32),
        grid_spec=pltpu.PrefetchScalarGridSpec(
            num_scalar_prefetch=2,
            grid=(n_slots // tm,),
            in_specs=[
                pl.BlockSpec((tm * SUBLANES, LANES), lambda i, be, nv: (i, 0)),
                pl.BlockSpec((None, None, D_MODEL, 2 * D_FF), lambda i, be, nv: (layer, be[i], 0, 0)),
                pl.BlockSpec((None, None, 1, 2 * D_FF), lambda i, be, nv: (layer, be[i], 0, 0)),
                pl.BlockSpec((None, None, D_FF, D_MODEL), lambda i, be, nv: (layer, be[i], 0, 0)),
                pl.BlockSpec((None, None, 1, D_MODEL), lambda i, be, nv: (layer, be[i], 0, 0)),
            ],
            out_specs=pl.BlockSpec((tm * SUBLANES, LANES), lambda i, be, nv: (i, 0)),
            scratch_shapes=[pltpu.VMEM((D_MODEL, 2 * D_FF), BF16), pltpu.VMEM((D_FF, D_MODEL), BF16)],
        ),
        compiler_params=pltpu.CompilerParams(dimension_semantics=("arbitrary",),
                                             vmem_limit_bytes=EXPERT_VMEM_LIMIT_BYTES),
        name="expert_ffn",
    )(block_expert, n_valid, xs, wgu, bgu, wd, bd)


def _combine_kernel(slot_ref, ys_hbm, x_ref, wt_ref, o_ref, buf, acc_ref, sems):
    i = pl.program_id(0)
    tc = COMBINE_TILE
    nsteps = pl.num_programs(0)
    cur = lax.rem(i, 2)

    def issue(step, par):
        base = step * tc

        def body(r, c):
            dst0 = pl.multiple_of(r * SUBLANES, SUBLANES)
            for k in range(TOP_K):
                row0 = pl.multiple_of(slot_ref[(base + r) * TOP_K + k], SUBLANES)
                pltpu.make_async_copy(ys_hbm.at[pl.ds(row0, SUBLANES)],
                                      buf.at[par, k, pl.ds(dst0, SUBLANES)],
                                      sems.at[par]).start(priority=k % 2)
            return c

        lax.fori_loop(0, tc, body, 0, unroll=8)

    @pl.when(i == 0)
    def _():
        issue(0, 0)

    @pl.when(i + 1 < nsteps)
    def _():
        issue(i + 1, 1 - cur)

    pltpu.make_async_copy(buf.at[cur], buf.at[cur], sems.at[cur]).wait()
    wt = wt_ref[...]
    acc = None
    for k in range(TOP_K):
        w_tiles = jnp.broadcast_to(_bcast_col(wt, k, LANES)[:, None, :], (tc, SUBLANES, LANES))
        term = w_tiles.reshape(tc * SUBLANES, LANES) * buf[cur, k]
        acc = term if acc is None else acc + term
    acc_ref[...] = acc
    for s in range(ROW_CHUNKS):
        cs = slice(s * LANES, (s + 1) * LANES)
        o_ref[:, cs] = x_ref[:, cs] + acc_ref[pl.ds(s, tc, stride=SUBLANES), :]


def _combine(slot_flat, ys, x2, wt):
    n = x2.shape[0]
    tc = COMBINE_TILE
    return pl.pallas_call(
        _combine_kernel,
        out_shape=jax.ShapeDtypeStruct((n, D_MODEL), F32),
        grid_spec=pltpu.PrefetchScalarGridSpec(
            num_scalar_prefetch=1,
            grid=(n // tc,),
            in_specs=[pl.BlockSpec(memory_space=pl.ANY),
                      pl.BlockSpec((tc, D_MODEL), lambda i, s: (i, 0)),
                      pl.BlockSpec((tc, LANES), lambda i, s: (i, 0))],
            out_specs=pl.BlockSpec((tc, D_MODEL), lambda i, s: (i, 0)),
            scratch_shapes=[pltpu.VMEM((2, TOP_K, tc * SUBLANES, LANES), F32),
                            pltpu.VMEM((tc * SUBLANES, LANES), F32), pltpu.SemaphoreType.DMA((2,))],
        ),
        compiler_params=_params(("arbitrary",)),
        name="moe_combine",
    )(slot_flat, ys, x2, wt)


def _moe(layer, h, x2, idx, wt, hist3, wgu, bgu, wd, bd):
    n = x2.shape[0]
    tm = MOE_TILE
    n_slots = n * TOP_K + N_EXPERTS * tm
    n_blocks = n_slots // tm
    hist = hist3[:, 0, :N_EXPERTS]
    counts = jnp.sum(hist, axis=0).astype(jnp.int32)
    padded = (counts + tm - 1) // tm * tm
    pad_end = jnp.cumsum(padded)
    pad_start = pad_end - padded
    tile_base = pad_start[None, :].astype(F32) + (jnp.cumsum(hist, axis=0) - hist)
    base3 = jnp.broadcast_to(jnp.pad(tile_base, ((0, 0), (0, LANES - N_EXPERTS)))[:, None, :],
                             (hist.shape[0], 8, LANES))
    blk0 = jnp.arange(n_blocks) * tm
    block_expert = jnp.minimum(jnp.sum(blk0[:, None] >= pad_end[None, :], axis=1), N_EXPERTS - 1).astype(jnp.int32)
    n_valid = (pad_end[-1:] // tm).astype(jnp.int32)
    zero_off = jnp.maximum(pad_end - tm, 0).astype(jnp.int32)

    slot_flat = _slots(idx, base3)[:, :TOP_K].reshape(-1)
    xs = _dispatch(slot_flat, zero_off, n_valid, h, n_slots)
    ys = _expert_ffn(layer, block_expert, n_valid, xs, wgu, bgu, wd, bd)
    return _combine(slot_flat, ys, x2, wt)


def _t5_bucket(dist):
    n = jnp.maximum(dist, 0)
    max_exact = REL_BUCKETS // 2
    nf = jnp.maximum(n, 1).astype(F32)
    large = max_exact + (jnp.log(nf / max_exact) / math.log(REL_MAX_DIST / max_exact)
                         * (REL_BUCKETS - max_exact)).astype(jnp.int32)
    large = jnp.minimum(large, REL_BUCKETS - 1)
    return jnp.where(n < max_exact, n, large)


def _bias_tables(rel_bias):
    rel_bias = rel_bias.astype(F32)
    tq = Q_TILE

    def lookup(dist):
        bucket = _t5_bucket(dist)
        out = jnp.zeros((NSA_HEADS,) + dist.shape, F32)
        for bk in range(REL_BUCKETS):
            out = jnp.where(bucket[None] == bk, rel_bias[bk].reshape((NSA_HEADS,) + (1,) * dist.ndim), out)
        return out

    t_pos = jnp.arange(SEQ)
    cmp_end = jnp.arange(N_CMP_PAD) * CMP_STRIDE + CMP_LEN - 1
    bias_cmp = lookup(t_pos[:, None] - cmp_end[None, :])
    rr = jnp.arange(tq)[:, None] - jnp.arange(tq)[None, :]
    btab = lookup(jnp.stack([rr, rr + tq, rr + 2 * tq]))
    return bias_cmp, btab


def _layer(layer, x2, b, p, experts, bias_cmp, btab):
    n = x2.shape[0]
    w = p['w_in']
    o_a = 4 * DN_WIDTH
    o_q = o_a + 2 * DN_HEADS
    o_kv = o_q + NSA_WIDTH
    o_g = o_kv + 6 * NSA_HEAD_DIM
    o_u = o_g + 3 * NSA_HEADS
    w_small = jnp.concatenate([w[:, o_a:o_q], w[:, o_g:o_u],
                               jnp.zeros((D_MODEL, LANES - 2 * DN_HEADS - 3 * NSA_HEADS), F32)], axis=1)
    w_cat = jnp.concatenate([w[:, :o_a], w[:, o_q:o_kv], w[:, o_kv:o_g], w[:, o_u:], w_small],
                            axis=1).astype(BF16)
    w_abt = jnp.concatenate([w[:, o_a:o_q].T, jnp.zeros((8, D_MODEL), F32)], axis=0).astype(BF16)
    pcol = jnp.zeros((8, LANES), F32).at[0, :DN_HEADS].set(p['dn_a_log']).at[1, :DN_HEADS].set(p['dn_dt_bias'])
    prow = jnp.zeros((16, LANES), F32).at[:DN_HEADS, 0].set(p['dn_a_log']).at[:DN_HEADS, 1].set(p['dn_dt_bias'])
    dq, dk, dv, gb, grow, z, nq, nkv, small, y_c = _front(
        x2, p['attn_norm_w'][None, :], w_cat, w_abt, p['dn_conv_w'], pcol, prow,
        p['conv_dw_w'], p['conv_dw_b'][None, :], p['conv_ln_w'][None, :], p['conv_ln_b'][None, :])
    small3 = small.reshape(b, SEQ, LANES)

    seq = lambda a: a.reshape(b, SEQ, a.shape[-1])
    y_a = _delta_rule(seq(dq), seq(dk), seq(dv), seq(gb), grow, seq(z), p['dn_norm_w'][None, :])

    kv6 = nkv.reshape(6, b, SEQ, NSA_HEAD_DIM)
    kcmp, vcmp, ksn, kwn = _nsa_prep(kv6, p['nsa_cmp_pos'], p['nsa_cmp_w1'].astype(BF16),
                                     p['nsa_cmp_w2'].astype(BF16), p['nsa_k_norm_w'])
    y_b = _nsa_attn(nq, p['nsa_q_norm_w'][None, :], kcmp, vcmp, ksn, kv6, kwn, small3, bias_cmp, btab)

    rw = jnp.concatenate(_hi_lo(jnp.pad(p['router_w'], ((0, 0), (0, LANES - N_EXPERTS)))), axis=1)
    rb = jnp.concatenate([p['router_b'], jnp.full((LANES - N_EXPERTS,), NEG_INF, F32)])[None, :]
    x_new, h, idx, wt, hist3 = _out_router(y_a.reshape(n, DN_WIDTH), y_b, y_c.reshape(n, CONV_WIDTH), x2,
                                           p['w_out'].astype(BF16), p['ffn_norm_w'][None, :], rw, rb)
    return _moe(layer, h, x_new, idx, wt, hist3, *experts)


def kernel(x, attn_norm_w, w_in, dn_conv_w, dn_a_log, dn_dt_bias, dn_norm_w, nsa_q_norm_w, nsa_k_norm_w, nsa_cmp_pos, nsa_cmp_w1, nsa_cmp_w2, conv_dw_w, conv_dw_b, conv_ln_w, conv_ln_b, w_out, ffn_norm_w, router_w, router_b, w_gate_up, b_gate_up, w_down, b_down, rel_bias):
    b, t, d = x.shape
    assert (t, d) == (SEQ, D_MODEL)
    stacked = dict(attn_norm_w=attn_norm_w, w_in=w_in, dn_conv_w=dn_conv_w, dn_a_log=dn_a_log,
                   dn_dt_bias=dn_dt_bias, dn_norm_w=dn_norm_w, nsa_q_norm_w=nsa_q_norm_w,
                   nsa_k_norm_w=nsa_k_norm_w, nsa_cmp_pos=nsa_cmp_pos, nsa_cmp_w1=nsa_cmp_w1,
                   nsa_cmp_w2=nsa_cmp_w2, conv_dw_w=conv_dw_w, conv_dw_b=conv_dw_b,
                   conv_ln_w=conv_ln_w, conv_ln_b=conv_ln_b, w_out=w_out, ffn_norm_w=ffn_norm_w,
                   router_w=router_w, router_b=router_b)
    experts = (w_gate_up, b_gate_up[:, :, None, :], w_down, b_down[:, :, None, :])
    bias_cmp, btab = _bias_tables(rel_bias)
    x2 = x.reshape(b * t, d)
    for l in range(w_in.shape[0]):
        x2 = _layer(l, x2, b, {k: v[l] for k, v in stacked.items()}, experts, bias_cmp, btab)
    return x2.reshape(b, t, d)
```
